```python
import jax, jax.numpy as jnp
from jax import lax
import numpy as np

D_MODEL = 2048
BATCH = 8
SEQ = 8192
DEPTH = 1

D_CONV = D_MODEL // 2
CONV_GROUPS = 8
CONV_K = 31
D_SGU = D_MODEL // 2
SGU_GROUPS = 8
SGU_HEAD = D_SGU // SGU_GROUPS
CHUNK = 128
D_FF = 5632
FFN_K = 3
N_MOD = 6
D_IN = 2 * D_CONV + 2 * D_SGU + 2 * D_MODEL
EPS = 1e-6

kernel_name = "hybrid_conformer_gmlp_convffn_adaln"


def rms_norm(x, g):
    xf = x.astype(jnp.float32)
    y = xf * lax.rsqrt(jnp.mean(xf * xf, axis=-1, keepdims=True) + EPS)
    return (y * g.astype(jnp.float32)).astype(x.dtype)


def layer_norm(x, g, b):
    xf = x.astype(jnp.float32)
    mu = jnp.mean(xf, axis=-1, keepdims=True)
    var = jnp.mean(jnp.square(xf - mu), axis=-1, keepdims=True)
    y = (xf - mu) * lax.rsqrt(var + EPS)
    return (y * g.astype(jnp.float32) + b.astype(jnp.float32)).astype(x.dtype)


def causal_dwconv(x, w, b):
    k, ch = w.shape
    y = lax.conv_general_dilated(
        x, w.astype(x.dtype)[:, None, :], window_strides=(1,), padding=[(k - 1, 0)],
        dimension_numbers=("NWC", "WIO", "NWC"), feature_group_count=ch)
    return y + b.astype(x.dtype)


def _fwd_setup_inputs(seed: int = 0) -> dict:
    key = jax.random.key(seed)
    ks = jax.random.split(key, 24)
    L, D = DEPTH, D_MODEL
    n = lambda k, shape, s: jax.random.normal(k, shape, jnp.float32) * s
    return {
        "x": n(ks[0], (BATCH, SEQ, D), 1.0),
        "c": n(ks[1], (BATCH, D), 1.0),
        "w_ada": n(ks[2], (L, D, N_MOD * D), 0.5 * D ** -0.5),
        "b_ada": n(ks[3], (L, N_MOD * D), 0.01),
        "norm1_g": 1.0 + n(ks[4], (L, D), 0.02),
        "w_in": n(ks[5], (L, D, D_IN), D ** -0.5),
        "b_in": n(ks[6], (L, D_IN), 0.01),
        "conv_dw_w": n(ks[7], (L, CONV_K, D_CONV), CONV_K ** -0.5),
        "conv_dw_b": n(ks[8], (L, D_CONV), 0.01),
        "conv_ln_g": 1.0 + n(ks[9], (L, D_CONV), 0.02),
        "conv_ln_b": n(ks[10], (L, D_CONV), 0.01),
        "w_conv_out": n(ks[11], (L, D_CONV, D), D_CONV ** -0.5),
        "sgu_ln_g": 1.0 + n(ks[12], (L, D_SGU), 0.02),
        "sgu_ln_b": n(ks[13], (L, D_SGU), 0.01),
        "w_spatial": n(ks[14], (L, SGU_GROUPS, CHUNK, CHUNK), CHUNK ** -0.5),
        "b_spatial": 1.0 + n(ks[15], (L, SGU_GROUPS, CHUNK), 0.01),
        "w_sgu_out": n(ks[16], (L, D_SGU, D), D_SGU ** -0.5),
        "w_out": n(ks[17], (L, D, D), D ** -0.5),
        "norm2_g": 1.0 + n(ks[18], (L, D), 0.02),
        "w_up": n(ks[19], (L, D, 2 * D_FF), D ** -0.5),
        "ffn_dw_w": n(ks[20], (L, FFN_K, 2 * D_FF), FFN_K ** -0.5),
        "ffn_dw_b": n(ks[21], (L, 2 * D_FF), 0.01),
        "w_down": n(ks[22], (L, D_FF, D), D_FF ** -0.5),
        "final_g": 1.0 + n(ks[23], (D,), 0.02),
    }


def _fwd_reference(x, c, w_ada, b_ada, norm1_g, w_in, b_in, conv_dw_w, conv_dw_b, conv_ln_g,
              conv_ln_b, w_conv_out, sgu_ln_g, sgu_ln_b, w_spatial, b_spatial, w_sgu_out,
              w_out, norm2_g, w_up, ffn_dw_w, ffn_dw_b, w_down, final_g):
    B, S, D = x.shape
    n_chunks = S // CHUNK
    causal_mask = jnp.tril(jnp.ones((CHUNK, CHUNK), dtype=x.dtype))
    c_act = jax.nn.silu(c)
    for l in range(DEPTH):
        mod = (c_act @ w_ada[l] + b_ada[l])[:, None, :]
        shift1, scale1, gate1, shift2, scale2, gate2 = jnp.split(mod, N_MOD, axis=-1)

        h = rms_norm(x, norm1_g[l]) * (1.0 + scale1) + shift1
        proj = h @ w_in[l] + b_in[l]
        a_in, s_in, gates = jnp.split(proj, [2 * D_CONV, 2 * D_CONV + 2 * D_SGU], axis=-1)

        a_val, a_gate = jnp.split(a_in, 2, axis=-1)
        a = a_val * jax.nn.sigmoid(a_gate)
        a = causal_dwconv(a, conv_dw_w[l], conv_dw_b[l])
        a = jax.nn.silu(layer_norm(a, conv_ln_g[l], conv_ln_b[l]))
        y_a = a @ w_conv_out[l]

        z = jax.nn.gelu(s_in, approximate=False)
        u, v = jnp.split(z, 2, axis=-1)
        v = layer_norm(v, sgu_ln_g[l], sgu_ln_b[l])
        v = v.reshape(B, n_chunks, CHUNK, SGU_GROUPS, SGU_HEAD)
        ws = w_spatial[l] * causal_mask
        v = jnp.einsum("gts,bnsgc->bntgc", ws, v) + b_spatial[l].T[:, :, None]
        y_b = (u * v.reshape(B, S, D_SGU)) @ w_sgu_out[l]

        g_a, g_b = jnp.split(gates, 2, axis=-1)
        merged = jax.nn.sigmoid(g_a) * y_a + jax.nn.sigmoid(g_b) * y_b
        x = x + gate1 * (merged @ w_out[l])

        h = rms_norm(x, norm2_g[l]) * (1.0 + scale2) + shift2
        up = causal_dwconv(h @ w_up[l], ffn_dw_w[l], ffn_dw_b[l])
        val, gt = jnp.split(up, 2, axis=-1)
        x = x + gate2 * ((jax.nn.silu(gt) * val) @ w_down[l])

    return rms_norm(x, final_g)


import jax as _jax
import jax.numpy as _jnp

TWIN_FORMAT = 'train_step'
FWD_PARAMS = ['x', 'c', 'w_ada', 'b_ada', 'norm1_g', 'w_in', 'b_in', 'conv_dw_w', 'conv_dw_b', 'conv_ln_g', 'conv_ln_b', 'w_conv_out', 'sgu_ln_g', 'sgu_ln_b', 'w_spatial', 'b_spatial', 'w_sgu_out', 'w_out', 'norm2_g', 'w_up', 'ffn_dw_w', 'ffn_dw_b', 'w_down', 'final_g']
TWIN_WEIGHTS = ['w_ada', 'b_ada', 'norm1_g', 'w_in', 'b_in', 'conv_dw_w', 'conv_dw_b', 'conv_ln_g', 'conv_ln_b', 'w_conv_out', 'sgu_ln_g', 'sgu_ln_b', 'w_spatial', 'b_spatial', 'w_sgu_out', 'w_out', 'norm2_g', 'w_up', 'ffn_dw_w', 'ffn_dw_b', 'w_down', 'final_g']
TWIN_DIFF_INPUT = 'x'
TWIN_INPUTS = ['x', 'c', 'w_ada', 'b_ada', 'norm1_g', 'w_in', 'b_in', 'conv_dw_w', 'conv_dw_b', 'conv_ln_g', 'conv_ln_b', 'w_conv_out', 'sgu_ln_g', 'sgu_ln_b', 'w_spatial', 'b_spatial', 'w_sgu_out', 'w_out', 'norm2_g', 'w_up', 'ffn_dw_w', 'ffn_dw_b', 'w_down', 'final_g', 'loss_target', 'm_w_ada', 'm_b_ada', 'm_norm1_g', 'm_w_in', 'm_b_in', 'm_conv_dw_w', 'm_conv_dw_b', 'm_conv_ln_g', 'm_conv_ln_b', 'm_w_conv_out', 'm_sgu_ln_g', 'm_sgu_ln_b', 'm_w_spatial', 'm_b_spatial', 'm_w_sgu_out', 'm_w_out', 'm_norm2_g', 'm_w_up', 'm_ffn_dw_w', 'm_ffn_dw_b', 'm_w_down', 'm_final_g', 'v_w_ada', 'v_b_ada', 'v_norm1_g', 'v_w_in', 'v_b_in', 'v_conv_dw_w', 'v_conv_dw_b', 'v_conv_ln_g', 'v_conv_ln_b', 'v_w_conv_out', 'v_sgu_ln_g', 'v_sgu_ln_b', 'v_w_spatial', 'v_b_spatial', 'v_w_sgu_out', 'v_w_out', 'v_norm2_g', 'v_w_up', 'v_ffn_dw_w', 'v_ffn_dw_b', 'v_w_down', 'v_final_g']
TWIN_OUTPUTS = ['loss', 'grad_x', 'grad_w_ada', 'grad_b_ada', 'grad_norm1_g', 'grad_w_in', 'grad_b_in', 'grad_conv_dw_w', 'grad_conv_dw_b', 'grad_conv_ln_g', 'grad_conv_ln_b', 'grad_w_conv_out', 'grad_sgu_ln_g', 'grad_sgu_ln_b', 'grad_w_spatial', 'grad_b_spatial', 'grad_w_sgu_out', 'grad_w_out', 'grad_norm2_g', 'grad_w_up', 'grad_ffn_dw_w', 'grad_ffn_dw_b', 'grad_w_down', 'grad_final_g', 'delta_w_ada', 'delta_b_ada', 'delta_norm1_g', 'delta_w_in', 'delta_b_in', 'delta_conv_dw_w', 'delta_conv_dw_b', 'delta_conv_ln_g', 'delta_conv_ln_b', 'delta_w_conv_out', 'delta_sgu_ln_g', 'delta_sgu_ln_b', 'delta_w_spatial', 'delta_b_spatial', 'delta_w_sgu_out', 'delta_w_out', 'delta_norm2_g', 'delta_w_up', 'delta_ffn_dw_w', 'delta_ffn_dw_b', 'delta_w_down', 'delta_final_g', 'new_m_w_ada', 'new_m_b_ada', 'new_m_norm1_g', 'new_m_w_in', 'new_m_b_in', 'new_m_conv_dw_w', 'new_m_conv_dw_b', 'new_m_conv_ln_g', 'new_m_conv_ln_b', 'new_m_w_conv_out', 'new_m_sgu_ln_g', 'new_m_sgu_ln_b', 'new_m_w_spatial', 'new_m_b_spatial', 'new_m_w_sgu_out', 'new_m_w_out', 'new_m_norm2_g', 'new_m_w_up', 'new_m_ffn_dw_w', 'new_m_ffn_dw_b', 'new_m_w_down', 'new_m_final_g', 'new_v_w_ada', 'new_v_b_ada', 'new_v_norm1_g', 'new_v_w_in', 'new_v_b_in', 'new_v_conv_dw_w', 'new_v_conv_dw_b', 'new_v_conv_ln_g', 'new_v_conv_ln_b', 'new_v_w_conv_out', 'new_v_sgu_ln_g', 'new_v_sgu_ln_b', 'new_v_w_spatial', 'new_v_b_spatial', 'new_v_w_sgu_out', 'new_v_w_out', 'new_v_norm2_g', 'new_v_w_up', 'new_v_ffn_dw_w', 'new_v_ffn_dw_b', 'new_v_w_down', 'new_v_final_g']
TWIN_LEAF_KINDS = {'loss': 'loss', 'grad_x': 'grad_x', 'grad_w_ada': 'grad_w', 'grad_b_ada': 'grad_w', 'grad_norm1_g': 'grad_w', 'grad_w_in': 'grad_w', 'grad_b_in': 'grad_w', 'grad_conv_dw_w': 'grad_w', 'grad_conv_dw_b': 'grad_w', 'grad_conv_ln_g': 'grad_w', 'grad_conv_ln_b': 'grad_w', 'grad_w_conv_out': 'grad_w', 'grad_sgu_ln_g': 'grad_w', 'grad_sgu_ln_b': 'grad_w', 'grad_w_spatial': 'grad_w', 'grad_b_spatial': 'grad_w', 'grad_w_sgu_out': 'grad_w', 'grad_w_out': 'grad_w', 'grad_norm2_g': 'grad_w', 'grad_w_up': 'grad_w', 'grad_ffn_dw_w': 'grad_w', 'grad_ffn_dw_b': 'grad_w', 'grad_w_down': 'grad_w', 'grad_final_g': 'grad_w', 'delta_w_ada': 'delta_w', 'delta_b_ada': 'delta_w', 'delta_norm1_g': 'delta_w', 'delta_w_in': 'delta_w', 'delta_b_in': 'delta_w', 'delta_conv_dw_w': 'delta_w', 'delta_conv_dw_b': 'delta_w', 'delta_conv_ln_g': 'delta_w', 'delta_conv_ln_b': 'delta_w', 'delta_w_conv_out': 'delta_w', 'delta_sgu_ln_g': 'delta_w', 'delta_sgu_ln_b': 'delta_w', 'delta_w_spatial': 'delta_w', 'delta_b_spatial': 'delta_w', 'delta_w_sgu_out': 'delta_w', 'delta_w_out': 'delta_w', 'delta_norm2_g': 'delta_w', 'delta_w_up': 'delta_w', 'delta_ffn_dw_w': 'delta_w', 'delta_ffn_dw_b': 'delta_w', 'delta_w_down': 'delta_w', 'delta_final_g': 'delta_w', 'new_m_w_ada': 'new_m', 'new_m_b_ada': 'new_m', 'new_m_norm1_g': 'new_m', 'new_m_w_in': 'new_m', 'new_m_b_in': 'new_m', 'new_m_conv_dw_w': 'new_m', 'new_m_conv_dw_b': 'new_m', 'new_m_conv_ln_g': 'new_m', 'new_m_conv_ln_b': 'new_m', 'new_m_w_conv_out': 'new_m', 'new_m_sgu_ln_g': 'new_m', 'new_m_sgu_ln_b': 'new_m', 'new_m_w_spatial': 'new_m', 'new_m_b_spatial': 'new_m', 'new_m_w_sgu_out': 'new_m', 'new_m_w_out': 'new_m', 'new_m_norm2_g': 'new_m', 'new_m_w_up': 'new_m', 'new_m_ffn_dw_w': 'new_m', 'new_m_ffn_dw_b': 'new_m', 'new_m_w_down': 'new_m', 'new_m_final_g': 'new_m', 'new_v_w_ada': 'new_v', 'new_v_b_ada': 'new_v', 'new_v_norm1_g': 'new_v', 'new_v_w_in': 'new_v', 'new_v_b_in': 'new_v', 'new_v_conv_dw_w': 'new_v', 'new_v_conv_dw_b': 'new_v', 'new_v_conv_ln_g': 'new_v', 'new_v_conv_ln_b': 'new_v', 'new_v_w_conv_out': 'new_v', 'new_v_sgu_ln_g': 'new_v', 'new_v_sgu_ln_b': 'new_v', 'new_v_w_spatial': 'new_v', 'new_v_b_spatial': 'new_v', 'new_v_w_sgu_out': 'new_v', 'new_v_w_out': 'new_v', 'new_v_norm2_g': 'new_v', 'new_v_w_up': 'new_v', 'new_v_ffn_dw_w': 'new_v', 'new_v_ffn_dw_b': 'new_v', 'new_v_w_down': 'new_v', 'new_v_final_g': 'new_v'}


def _forward(args):
    return _fwd_reference(*[args[k] for k in FWD_PARAMS])


def _output_shape():
    def fwd():
        inp = _fwd_setup_inputs(0)
        return _fwd_reference(*[inp[k] for k in FWD_PARAMS])
    out = _jax.eval_shape(fwd)
    return out.shape, out.dtype

N_MICROBATCH = 1
ADAM_LR = 0.001
ADAM_B1 = 0.9
ADAM_B2 = 0.999
ADAM_EPS = 1e-08
ADAM_WD = 0.01
ADAM_STEP = 10
PER_EXAMPLE_BATCH_AXIS = {'x': 0, 'c': 0, 'loss_target': 0}
SHARED_INPUTS = []
_WEIGHT_DTYPES = {'w_ada': _jnp.float32, 'b_ada': _jnp.float32, 'norm1_g': _jnp.float32, 'w_in': _jnp.float32, 'b_in': _jnp.float32, 'conv_dw_w': _jnp.float32, 'conv_dw_b': _jnp.float32, 'conv_ln_g': _jnp.float32, 'conv_ln_b': _jnp.float32, 'w_conv_out': _jnp.float32, 'sgu_ln_g': _jnp.float32, 'sgu_ln_b': _jnp.float32, 'w_spatial': _jnp.float32, 'b_spatial': _jnp.float32, 'w_sgu_out': _jnp.float32, 'w_out': _jnp.float32, 'norm2_g': _jnp.float32, 'w_up': _jnp.float32, 'ffn_dw_w': _jnp.float32, 'ffn_dw_b': _jnp.float32, 'w_down': _jnp.float32, 'final_g': _jnp.float32}
MOMENT_SCALE = {'w_ada': 3.374559e-02, 'b_ada': 5.930594e-02, 'norm1_g': 2.751780e-02, 'w_in': 1.389258e-02, 'b_in': 1.301885e-02, 'conv_dw_w': 1.834973e-02, 'conv_dw_b': 3.369765e-02, 'conv_ln_g': 2.262034e-02, 'conv_ln_b': 1.970884e-02, 'w_conv_out': 1.251969e-02, 'sgu_ln_g': 1.462246e-02, 'sgu_ln_b': 1.444212e-02, 'w_spatial': 1.463530e-02, 'b_spatial': 2.124703e-02, 'w_sgu_out': 1.836665e-02, 'w_out': 2.236370e-02, 'norm2_g': 3.695892e-02, 'w_up': 1.655108e-02, 'ffn_dw_w': 1.676149e-02, 'ffn_dw_b': 1.491114e-02, 'w_down': 2.686771e-02, 'final_g': 3.195058e+01}


def _to_microbatches(a, axis):
    t = _jnp.moveaxis(a, axis, 0)
    t = t.reshape((N_MICROBATCH, t.shape[0] // N_MICROBATCH) + t.shape[1:])
    return _jnp.moveaxis(t, 1, axis + 1)


def setup_inputs(seed: int = 0) -> dict:
    inp = _fwd_setup_inputs(seed)
    key = _jax.random.fold_in(_jax.random.key(seed), 7919)
    shape, _ = _output_shape()
    out = dict(inp)
    out["loss_target"] = _jax.random.normal(_jax.random.fold_in(key, 0), shape, _jnp.float32)
    for i, name in enumerate(TWIN_WEIGHTS):
        w = inp[name].astype(_jnp.float32)
        if MOMENT_SCALE is None:
            s = _jnp.sqrt(_jnp.mean(_jnp.square(w)) + 1e-30)
        else:
            s = MOMENT_SCALE[name]
        km, kv = _jax.random.split(_jax.random.fold_in(key, i + 1))
        out[name] = w
        out["m_" + name] = s * _jax.random.normal(km, w.shape, _jnp.float32)
        out["v_" + name] = (s * s) * _jax.random.uniform(kv, w.shape, _jnp.float32, 0.5, 1.5)
    if N_MICROBATCH > 1:
        for name, axis in PER_EXAMPLE_BATCH_AXIS.items():
            out[name] = _to_microbatches(out[name], axis)
    return {'x': out['x'], 'c': out['c'], 'w_ada': out['w_ada'], 'b_ada': out['b_ada'], 'norm1_g': out['norm1_g'], 'w_in': out['w_in'], 'b_in': out['b_in'], 'conv_dw_w': out['conv_dw_w'], 'conv_dw_b': out['conv_dw_b'], 'conv_ln_g': out['conv_ln_g'], 'conv_ln_b': out['conv_ln_b'], 'w_conv_out': out['w_conv_out'], 'sgu_ln_g': out['sgu_ln_g'], 'sgu_ln_b': out['sgu_ln_b'], 'w_spatial': out['w_spatial'], 'b_spatial': out['b_spatial'], 'w_sgu_out': out['w_sgu_out'], 'w_out': out['w_out'], 'norm2_g': out['norm2_g'], 'w_up': out['w_up'], 'ffn_dw_w': out['ffn_dw_w'], 'ffn_dw_b': out['ffn_dw_b'], 'w_down': out['w_down'], 'final_g': out['final_g'], 'loss_target': out['loss_target'], 'm_w_ada': out['m_w_ada'], 'm_b_ada': out['m_b_ada'], 'm_norm1_g': out['m_norm1_g'], 'm_w_in': out['m_w_in'], 'm_b_in': out['m_b_in'], 'm_conv_dw_w': out['m_conv_dw_w'], 'm_conv_dw_b': out['m_conv_dw_b'], 'm_conv_ln_g': out['m_conv_ln_g'], 'm_conv_ln_b': out['m_conv_ln_b'], 'm_w_conv_out': out['m_w_conv_out'], 'm_sgu_ln_g': out['m_sgu_ln_g'], 'm_sgu_ln_b': out['m_sgu_ln_b'], 'm_w_spatial': out['m_w_spatial'], 'm_b_spatial': out['m_b_spatial'], 'm_w_sgu_out': out['m_w_sgu_out'], 'm_w_out': out['m_w_out'], 'm_norm2_g': out['m_norm2_g'], 'm_w_up': out['m_w_up'], 'm_ffn_dw_w': out['m_ffn_dw_w'], 'm_ffn_dw_b': out['m_ffn_dw_b'], 'm_w_down': out['m_w_down'], 'm_final_g': out['m_final_g'], 'v_w_ada': out['v_w_ada'], 'v_b_ada': out['v_b_ada'], 'v_norm1_g': out['v_norm1_g'], 'v_w_in': out['v_w_in'], 'v_b_in': out['v_b_in'], 'v_conv_dw_w': out['v_conv_dw_w'], 'v_conv_dw_b': out['v_conv_dw_b'], 'v_conv_ln_g': out['v_conv_ln_g'], 'v_conv_ln_b': out['v_conv_ln_b'], 'v_w_conv_out': out['v_w_conv_out'], 'v_sgu_ln_g': out['v_sgu_ln_g'], 'v_sgu_ln_b': out['v_sgu_ln_b'], 'v_w_spatial': out['v_w_spatial'], 'v_b_spatial': out['v_b_spatial'], 'v_w_sgu_out': out['v_w_sgu_out'], 'v_w_out': out['v_w_out'], 'v_norm2_g': out['v_norm2_g'], 'v_w_up': out['v_w_up'], 'v_ffn_dw_w': out['v_ffn_dw_w'], 'v_ffn_dw_b': out['v_ffn_dw_b'], 'v_w_down': out['v_w_down'], 'v_final_g': out['v_final_g']}


def _loss(weights, diff, rest, loss_target):
    with _jax.named_scope("forward"):
        args = {**rest, TWIN_DIFF_INPUT: diff, **{k: w.astype(_WEIGHT_DTYPES[k]) for k, w in weights.items()}}
        y = _forward(args)
    with _jax.named_scope("loss_head"):
        err = _jnp.square(y.astype(_jnp.float32) - loss_target)
        return 0.5 * _jnp.sum(_jnp.mean(err, axis=-1)) if err.ndim else 0.5 * err


def _adamw(w, g, m, v):
    m = ADAM_B1 * m + (1.0 - ADAM_B1) * g
    v = ADAM_B2 * v + (1.0 - ADAM_B2) * _jnp.square(g)
    m_hat = m / (1.0 - ADAM_B1 ** ADAM_STEP)
    v_hat = v / (1.0 - ADAM_B2 ** ADAM_STEP)
    delta = -ADAM_LR * (m_hat / (_jnp.sqrt(v_hat) + ADAM_EPS) + ADAM_WD * w)
    return delta, m, v


def reference(x, c, w_ada, b_ada, norm1_g, w_in, b_in, conv_dw_w, conv_dw_b, conv_ln_g, conv_ln_b, w_conv_out, sgu_ln_g, sgu_ln_b, w_spatial, b_spatial, w_sgu_out, w_out, norm2_g, w_up, ffn_dw_w, ffn_dw_b, w_down, final_g, loss_target, m_w_ada, m_b_ada, m_norm1_g, m_w_in, m_b_in, m_conv_dw_w, m_conv_dw_b, m_conv_ln_g, m_conv_ln_b, m_w_conv_out, m_sgu_ln_g, m_sgu_ln_b, m_w_spatial, m_b_spatial, m_w_sgu_out, m_w_out, m_norm2_g, m_w_up, m_ffn_dw_w, m_ffn_dw_b, m_w_down, m_final_g, v_w_ada, v_b_ada, v_norm1_g, v_w_in, v_b_in, v_conv_dw_w, v_conv_dw_b, v_conv_ln_g, v_conv_ln_b, v_w_conv_out, v_sgu_ln_g, v_sgu_ln_b, v_w_spatial, v_b_spatial, v_w_sgu_out, v_w_out, v_norm2_g, v_w_up, v_ffn_dw_w, v_ffn_dw_b, v_w_down, v_final_g):
    given = dict(x=x, c=c, w_ada=w_ada, b_ada=b_ada, norm1_g=norm1_g, w_in=w_in, b_in=b_in, conv_dw_w=conv_dw_w, conv_dw_b=conv_dw_b, conv_ln_g=conv_ln_g, conv_ln_b=conv_ln_b, w_conv_out=w_conv_out, sgu_ln_g=sgu_ln_g, sgu_ln_b=sgu_ln_b, w_spatial=w_spatial, b_spatial=b_spatial, w_sgu_out=w_sgu_out, w_out=w_out, norm2_g=norm2_g, w_up=w_up, ffn_dw_w=ffn_dw_w, ffn_dw_b=ffn_dw_b, w_down=w_down, final_g=final_g, loss_target=loss_target, m_w_ada=m_w_ada, m_b_ada=m_b_ada, m_norm1_g=m_norm1_g, m_w_in=m_w_in, m_b_in=m_b_in, m_conv_dw_w=m_conv_dw_w, m_conv_dw_b=m_conv_dw_b, m_conv_ln_g=m_conv_ln_g, m_conv_ln_b=m_conv_ln_b, m_w_conv_out=m_w_conv_out, m_sgu_ln_g=m_sgu_ln_g, m_sgu_ln_b=m_sgu_ln_b, m_w_spatial=m_w_spatial, m_b_spatial=m_b_spatial, m_w_sgu_out=m_w_sgu_out, m_w_out=m_w_out, m_norm2_g=m_norm2_g, m_w_up=m_w_up, m_ffn_dw_w=m_ffn_dw_w, m_ffn_dw_b=m_ffn_dw_b, m_w_down=m_w_down, m_final_g=m_final_g, v_w_ada=v_w_ada, v_b_ada=v_b_ada, v_norm1_g=v_norm1_g, v_w_in=v_w_in, v_b_in=v_b_in, v_conv_dw_w=v_conv_dw_w, v_conv_dw_b=v_conv_dw_b, v_conv_ln_g=v_conv_ln_g, v_conv_ln_b=v_conv_ln_b, v_w_conv_out=v_w_conv_out, v_sgu_ln_g=v_sgu_ln_g, v_sgu_ln_b=v_sgu_ln_b, v_w_spatial=v_w_spatial, v_b_spatial=v_b_spatial, v_w_sgu_out=v_w_sgu_out, v_w_out=v_w_out, v_norm2_g=v_norm2_g, v_w_up=v_w_up, v_ffn_dw_w=v_ffn_dw_w, v_ffn_dw_b=v_ffn_dw_b, v_w_down=v_w_down, v_final_g=v_final_g)
    weights = {n: given[n] for n in TWIN_WEIGHTS}
    shared = {n: given[n] for n in SHARED_INPUTS}
    per_example = {n: given[n] for n in ['x', 'c']}
    grad_fn = _jax.value_and_grad(_loss, argnums=(0, 1))

    def one_microbatch(ex, loss_target):
        ex = dict(ex)
        diff = ex.pop(TWIN_DIFF_INPUT)
        return grad_fn(weights, diff, {**shared, **ex}, loss_target)

    if N_MICROBATCH == 1:
        loss, (grad_w, grad_x) = one_microbatch(per_example, given["loss_target"])
    else:
        def body(carry, xs):
            loss_sum, grad_sum = carry
            l_k, (gw_k, gx_k) = one_microbatch(xs[0], xs[1])
            with _jax.named_scope("update"):
                return (loss_sum + l_k, _jax.tree.map(_jnp.add, grad_sum, gw_k)), gx_k

        init = (_jnp.zeros((), _jnp.float32), _jax.tree.map(_jnp.zeros_like, weights))
        (loss, grad_w), grad_x = _jax.lax.scan(body, init, (per_example, given["loss_target"]))
    with _jax.named_scope("update"):
        delta_w, new_m, new_v = {}, {}, {}
        for n in TWIN_WEIGHTS:
            delta_w[n], new_m[n], new_v[n] = _adamw(weights[n], grad_w[n], given["m_" + n], given["v_" + n])
    return (loss, grad_x, *[grad_w[n] for n in TWIN_WEIGHTS], *[delta_w[n] for n in TWIN_WEIGHTS],
            *[new_m[n] for n in TWIN_WEIGHTS], *[new_v[n] for n in TWIN_WEIGHTS])
```

```python
import functools

import jax
import jax.numpy as jnp
from jax import lax
from jax.experimental import pallas as pl
from jax.experimental.pallas import tpu as pltpu

F32 = jnp.float32
BF16 = jnp.bfloat16
EPS = 1e-6
MESH = pl.DeviceIdType.MESH
N_CHIPS = 4
N_DEV = 8
LANES = 128
SUBLANES = 8
CONV_HALO = 32
FFN_HALO = 8
VMEM_LIMIT_BYTES = 56 * 1024 * 1024

ADAM_LR = 0.001
ADAM_B1 = 0.9
ADAM_B2 = 0.999
ADAM_EPS = 1e-08
ADAM_WD = 0.01
ADAM_STEP = 10

NN = (((1,), (0,)), ((), ()))
NT = (((1,), (1,)), ((), ()))
TN = (((0,), (0,)), ((), ()))


def _params(sem=None):
    return pltpu.CompilerParams(dimension_semantics=sem, vmem_limit_bytes=VMEM_LIMIT_BYTES)


def _pick(dim, pref, mult):
    best = None
    d = mult
    while d <= min(dim, pref):
        if dim % d == 0:
            best = d
        d += mult
    return dim if best is None else best


def _axes():
    return lax.axis_index("x"), lax.axis_index("y"), lax.axis_index("c")


def _modnorm(x, g, scale, shift):
    r = lax.rsqrt(jnp.mean(x * x, axis=-1, keepdims=True) + EPS)
    return (x * r * g) * (1.0 + scale) + shift


def _layer_norm(x, g, b):
    mu = jnp.mean(x, axis=-1, keepdims=True)
    var = jnp.mean(jnp.square(x - mu), axis=-1, keepdims=True)
    return (x - mu) * lax.rsqrt(var + EPS) * g + b


def _gelu(x):
    return 0.5 * x * (1.0 + lax.erf(x * (0.5 ** 0.5)))


def _ln_silu(x, g, b):
    return jax.nn.silu(_layer_norm(x, g, b))


def _tril_mask(ws):
    n = ws.shape[-1]
    row = lax.broadcasted_iota(jnp.int32, (n, n), 0)
    col = lax.broadcasted_iota(jnp.int32, (n, n), 1)
    return jnp.where(row >= col, ws, 0.0)


def _pack(arrs):
    flat = [a.reshape(-1).astype(F32) for a in arrs]
    total = sum(f.shape[0] for f in flat)
    tile = SUBLANES * LANES
    padded = -(-total // tile) * tile
    if padded > total:
        flat = flat + [jnp.zeros((padded - total,), F32)]
    return jnp.concatenate(flat).reshape(padded // LANES, LANES)


def _unpack(buf, shapes):
    flat = buf.reshape(-1)
    out, off = [], 0
    for s in shapes:
        n = 1
        for d in s:
            n *= d
        out.append(flat[off:off + n].reshape(s))
        off += n
    return out


def _allgather8(buf, name):
    R, L = buf.shape

    def body(in_ref, out_ref, send_sems, recv_sems, local_sem):
        x, y, c = _axes()
        me = 4 * x + 2 * y + c
        mine = pltpu.make_async_copy(in_ref, out_ref.at[me], local_sem)
        mine.start()
        peers = []
        for k in range(1, N_DEV):
            px = 1 - x if k & 4 else x
            py = 1 - y if k & 2 else y
            pc = 1 - c if k & 1 else c
            peers.append((px, py, pc))
        sends = []
        for k, peer in enumerate(peers):
            cp = pltpu.make_async_remote_copy(
                src_ref=in_ref, dst_ref=out_ref.at[me], send_sem=send_sems.at[k], recv_sem=recv_sems.at[k],
                device_id=peer, device_id_type=MESH)
            cp.start()
            sends.append(cp)
        for k, (px, py, pc) in enumerate(peers):
            pltpu.make_async_remote_copy(
                src_ref=in_ref, dst_ref=out_ref.at[4 * px + 2 * py + pc], send_sem=send_sems.at[k],
                recv_sem=recv_sems.at[k], device_id=(px, py, pc), device_id_type=MESH).wait_recv()
        for cp in sends:
            cp.wait_send()
        mine.wait()

    return pl.pallas_call(
        body, name=name,
        out_shape=jax.ShapeDtypeStruct((N_DEV, R, L), buf.dtype),
        in_specs=[pl.BlockSpec(memory_space=pltpu.VMEM)],
        out_specs=pl.BlockSpec(memory_space=pltpu.VMEM),
        scratch_shapes=[pltpu.SemaphoreType.DMA((N_DEV - 1,)), pltpu.SemaphoreType.DMA((N_DEV - 1,)),
                        pltpu.SemaphoreType.DMA],
        compiler_params=pltpu.CompilerParams(vmem_limit_bytes=VMEM_LIMIT_BYTES),
    )(buf)


def _other_chips(x, y):
    return [(1 - x, y), (x, 1 - y), (1 - x, 1 - y)]


def _gather_weights(shards, name):
    nw = len(shards)

    def body(*refs):
        ins, outs = refs[:nw], refs[nw:2 * nw]
        send_sems, recv_sems, local_sems = refs[2 * nw:]
        x, y, c = _axes()
        p = 2 * x + y
        chips = _other_chips(x, y)
        started = []
        local = []
        for w in range(nw):
            kh = ins[w].shape[0] // 2
            mine = pl.ds(c * kh, kh)
            lc = pltpu.make_async_copy(ins[w], outs[w].at[p], local_sems.at[w])
            lc.start()
            local.append(lc)
            for j, (qx, qy) in enumerate(chips):
                cp = pltpu.make_async_remote_copy(
                    src_ref=ins[w].at[mine], dst_ref=outs[w].at[p, mine],
                    send_sem=send_sems.at[6 * w + j], recv_sem=recv_sems.at[6 * w + j],
                    device_id=(qx, qy, c), device_id_type=MESH)
                cp.start()
                started.append(cp)
        for w in range(nw):
            kh = ins[w].shape[0] // 2
            mine = pl.ds(c * kh, kh)
            for j, (qx, qy) in enumerate(chips):
                q = 2 * qx + qy
                pltpu.make_async_remote_copy(
                    src_ref=ins[w].at[mine], dst_ref=outs[w].at[q, mine],
                    send_sem=send_sems.at[6 * w + j], recv_sem=recv_sems.at[6 * w + j],
                    device_id=(qx, qy, c), device_id_type=MESH).wait_recv()
                fw = pltpu.make_async_remote_copy(
                    src_ref=outs[w].at[q, mine], dst_ref=outs[w].at[q, mine],
                    send_sem=send_sems.at[6 * w + 3 + j], recv_sem=recv_sems.at[6 * w + 3 + j],
                    device_id=(x, y, 1 - c), device_id_type=MESH)
                fw.start()
                started.append(fw)
        for w in range(nw):
            kh = ins[w].shape[0] // 2
            theirs = pl.ds((1 - c) * kh, kh)
            for j, (qx, qy) in enumerate(chips):
                q = 2 * qx + qy
                pltpu.make_async_remote_copy(
                    src_ref=outs[w].at[q, theirs], dst_ref=outs[w].at[q, theirs],
                    send_sem=send_sems.at[6 * w + 3 + j], recv_sem=recv_sems.at[6 * w + 3 + j],
                    device_id=(x, y, 1 - c), device_id_type=MESH).wait_recv()
        for cp in started:
            cp.wait_send()
        for lc in local:
            lc.wait()

    any_spec = pl.BlockSpec(memory_space=pl.ANY)
    return pl.pallas_call(
        body, name=name,
        out_shape=[jax.ShapeDtypeStruct((N_CHIPS,) + s.shape, s.dtype) for s in shards],
        in_specs=[any_spec] * nw, out_specs=[any_spec] * nw,
        scratch_shapes=[pltpu.SemaphoreType.DMA((6 * nw,)), pltpu.SemaphoreType.DMA((6 * nw,)),
                        pltpu.SemaphoreType.DMA((nw,))],
    )(*shards)


def _swap_halves(grads, name):
    nw = len(grads)

    def body(*refs):
        ins, outs = refs[:nw], refs[nw:2 * nw]
        send_sems, recv_sems = refs[2 * nw:]
        x, y, c = _axes()
        copies = []
        for w in range(nw):
            kh = ins[w].shape[1] // 2
            cp = pltpu.make_async_remote_copy(
                src_ref=ins[w].at[:, pl.ds((1 - c) * kh, kh), :], dst_ref=outs[w],
                send_sem=send_sems.at[w], recv_sem=recv_sems.at[w],
                device_id=(x, y, 1 - c), device_id_type=MESH)
            cp.start()
            copies.append(cp)
        for cp in copies:
            cp.wait()

    any_spec = pl.BlockSpec(memory_space=pl.ANY)
    return pl.pallas_call(
        body, name=name,
        out_shape=[jax.ShapeDtypeStruct((g.shape[0], g.shape[1] // 2, g.shape[2]), g.dtype) for g in grads],
        in_specs=[any_spec] * nw, out_specs=[any_spec] * nw,
        scratch_shapes=[pltpu.SemaphoreType.DMA((nw,)), pltpu.SemaphoreType.DMA((nw,))],
    )(*grads)


def _scatter_blocks(sums, name):
    nw = len(sums)

    def body(*refs):
        ins, outs = refs[:nw], refs[nw:2 * nw]
        send_sems, recv_sems = refs[2 * nw:]
        x, y, c = _axes()
        chips = _other_chips(x, y)
        copies = []
        for w in range(nw):
            for j, (qx, qy) in enumerate(chips):
                cp = pltpu.make_async_remote_copy(
                    src_ref=ins[w].at[2 * qx + qy], dst_ref=outs[w].at[j],
                    send_sem=send_sems.at[3 * w + j], recv_sem=recv_sems.at[3 * w + j],
                    device_id=(qx, qy, c), device_id_type=MESH)
                cp.start()
                copies.append(cp)
        for cp in copies:
            cp.wait()

    any_spec = pl.BlockSpec(memory_space=pl.ANY)
    return pl.pallas_call(
        body, name=name,
        out_shape=[jax.ShapeDtypeStruct((3,) + s.shape[1:], s.dtype) for s in sums],
        in_specs=[any_spec] * nw, out_specs=[any_spec] * nw,
        scratch_shapes=[pltpu.SemaphoreType.DMA((3 * nw,)), pltpu.SemaphoreType.DMA((3 * nw,))],
    )(*sums)


def _join_halves(halves, name):
    nw = len(halves)

    def body(*refs):
        ins, outs = refs[:nw], refs[nw:2 * nw]
        send_sems, recv_sems, local_sems = refs[2 * nw:]
        x, y, c = _axes()
        copies, local = [], []
        for w in range(nw):
            lc = pltpu.make_async_copy(ins[w], outs[w].at[c], local_sems.at[w])
            lc.start()
            local.append(lc)
            cp = pltpu.make_async_remote_copy(
                src_ref=ins[w], dst_ref=outs[w].at[c], send_sem=send_sems.at[w], recv_sem=recv_sems.at[w],
                device_id=(x, y, 1 - c), device_id_type=MESH)
            cp.start()
            copies.append(cp)
        for w in range(nw):
            copies[w].wait_send()
            pltpu.make_async_remote_copy(
                src_ref=ins[w], dst_ref=outs[w].at[1 - c], send_sem=send_sems.at[w], recv_sem=recv_sems.at[w],
                device_id=(x, y, 1 - c), device_id_type=MESH).wait_recv()
        for lc in local:
            lc.wait()

    any_spec = pl.BlockSpec(memory_space=pl.ANY)
    return pl.pallas_call(
        body, name=name,
        out_shape=[jax.ShapeDtypeStruct((2,) + h.shape, h.dtype) for h in halves],
        in_specs=[any_spec] * nw, out_specs=[any_spec] * nw,
        scratch_shapes=[pltpu.SemaphoreType.DMA((nw,)), pltpu.SemaphoreType.DMA((nw,)),
                        pltpu.SemaphoreType.DMA((nw,))],
    )(*halves)


def _add_own_half(g, r, c_idx, name):
    nb, K, n = g.shape
    kh = K // 2
    tr = _pick(kh, max(SUBLANES, (1 << 19) // n), SUBLANES)
    per = kh // tr

    def body(c_ref, g_ref, r_ref, o_ref):
        o_ref[...] = g_ref[...] + r_ref[...]

    return pl.pallas_call(
        body, name=name,
        grid_spec=pltpu.PrefetchScalarGridSpec(
            num_scalar_prefetch=1, grid=(nb, per),
            in_specs=[pl.BlockSpec((None, tr, n), lambda b, i, c: (b, c[0] * per + i, 0)),
                      pl.BlockSpec((None, tr, n), lambda b, i, c: (b, i, 0))],
            out_specs=pl.BlockSpec((None, tr, n), lambda b, i, c: (b, i, 0))),
        out_shape=jax.ShapeDtypeStruct((nb, kh, n), F32),
        compiler_params=_params(("parallel", "parallel")),
    )(c_idx, g, r)


def _add_chips(s, r, p_idx, name):
    _, kh, n = s.shape
    tr = _pick(kh, max(SUBLANES, (1 << 19) // n), SUBLANES)

    def body(p_ref, s_ref, r_ref, o_ref):
        o_ref[...] = ((s_ref[...] + r_ref[0]) + r_ref[1]) + r_ref[2]

    return pl.pallas_call(
        body, name=name,
        grid_spec=pltpu.PrefetchScalarGridSpec(
            num_scalar_prefetch=1, grid=(kh // tr,),
            in_specs=[pl.BlockSpec((None, tr, n), lambda i, p: (p[0], i, 0)),
                      pl.BlockSpec((3, tr, n), lambda i, p: (0, i, 0))],
            out_specs=pl.BlockSpec((tr, n), lambda i, p: (i, 0))),
        out_shape=jax.ShapeDtypeStruct((kh, n), F32),
        compiler_params=_params(("parallel",)),
    )(p_idx, s, r)


def _sum8(g, name):
    _, R, L = g.shape

    def body(g_ref, o_ref):
        acc = g_ref[0]
        for d in range(1, N_DEV):
            acc = acc + g_ref[d]
        o_ref[...] = acc

    return pl.pallas_call(
        body, name=name, out_shape=jax.ShapeDtypeStruct((R, L), F32),
        in_specs=[pl.BlockSpec(memory_space=pltpu.VMEM)], out_specs=pl.BlockSpec(memory_space=pltpu.VMEM),
        compiler_params=_params(),
    )(g)


def _adamw(w, g, m, v, name):
    R, C = w.shape
    tr = _pick(R, max(SUBLANES, (1 << 18) // C), SUBLANES)

    def body(w_ref, g_ref, m_ref, v_ref, d_ref, nm_ref, nv_ref):
        gg = g_ref[...]
        nm = ADAM_B1 * m_ref[...] + (1.0 - ADAM_B1) * gg
        nv = ADAM_B2 * v_ref[...] + (1.0 - ADAM_B2) * jnp.square(gg)
        m_hat = nm / (1.0 - ADAM_B1 ** ADAM_STEP)
        v_hat = nv / (1.0 - ADAM_B2 ** ADAM_STEP)
        d_ref[...] = -ADAM_LR * (m_hat / (jnp.sqrt(v_hat) + ADAM_EPS) + ADAM_WD * w_ref[...])
        nm_ref[...] = nm
        nv_ref[...] = nv

    spec = pl.BlockSpec((tr, C), lambda i: (i, 0))
    sd = jax.ShapeDtypeStruct((R, C), F32)
    return pl.pallas_call(
        body, name=name, grid=(R // tr,), in_specs=[spec] * 4, out_specs=[spec] * 3, out_shape=[sd] * 3,
        compiler_params=_params(("parallel",)),
    )(w, g, m, v)


def _matmul(name, grid, dims, a, a_spec, b, b_spec, outs, out_specs, acc_shape,
            extras=(), extra_specs=(), epilogue=None):
    nk = grid[2]
    nex = len(extras)
    nout = len(outs)

    def body(a_ref, b_ref, *rest):
        ex, o = rest[:nex], rest[nex:nex + nout]
        part = lax.dot_general(a_ref[...].astype(BF16), b_ref[...].astype(BF16), dims,
                               preferred_element_type=F32)

        def finish(res):
            if epilogue is None:
                o[0][...] = res.astype(o[0].dtype)
            else:
                epilogue(res, ex, o)

        if nk == 1:
            finish(part)
        else:
            acc = rest[-1]
            k = pl.program_id(2)

            @pl.when(k == 0)
            def _():
                acc[...] = part

            @pl.when(k > 0)
            def _():
                acc[...] += part

            @pl.when(k == nk - 1)
            def _():
                finish(acc[...])

    return pl.pallas_call(
        body, name=name, grid=grid,
        in_specs=[a_spec, b_spec] + list(extra_specs), out_specs=list(out_specs), out_shape=list(outs),
        scratch_shapes=[] if nk == 1 else [pltpu.VMEM(acc_shape, F32)],
        compiler_params=_params(("parallel", "parallel", "arbitrary")),
    )(a, b, *extras)


def _mm_fwd(name, a, wfull, out_dtype=F32, planes=1, bias=None, resid=None, gate=None):
    S, K = a.shape
    _, _, n = wfull.shape
    N = N_CHIPS * n
    tm = _pick(S, 512 if resid is not None else 1024, LANES)
    tn = _pick(n, 1408, LANES)
    per = n // tn
    nj = N // tn
    pj = nj // planes
    grid = (S // tm, nj, 1)
    a_spec = pl.BlockSpec((tm, K), lambda i, j, k: (i, 0))
    b_spec = pl.BlockSpec((None, K, tn), lambda i, j, k: (j // per, 0, j % per))
    if resid is not None:
        o_spec = pl.BlockSpec((tm, tn), lambda i, j, k: (i, j))
        g_spec = pl.BlockSpec((1, tn), lambda i, j, k: (0, j))

        def epi(res, ex, o):
            o[0][...] = res
            o[1][...] = ex[0][...] + ex[1][...] * res

        sd = jax.ShapeDtypeStruct((S, N), F32)
        return _matmul(name, grid, NN, a, a_spec, wfull, b_spec, [sd, sd], [o_spec, o_spec], (tm, tn),
                       extras=(resid, gate), extra_specs=(o_spec, g_spec), epilogue=epi)
    o_spec = pl.BlockSpec((None, tm, tn), lambda i, j, k: (j // pj, i, j % pj))
    sd = jax.ShapeDtypeStruct((planes, S, N // planes), out_dtype)
    if bias is not None:
        def epi(res, ex, o):
            o[0][...] = (res + ex[0][...]).astype(o[0].dtype)

        out = _matmul(name, grid, NN, a, a_spec, wfull, b_spec, [sd], [o_spec], (tm, tn),
                      extras=(bias,), extra_specs=(pl.BlockSpec((1, tn), lambda i, j, k: (0, j)),), epilogue=epi)
    else:
        out = _matmul(name, grid, NN, a, a_spec, wfull, b_spec, [sd], [o_spec], (tm, tn))
    return out[0]


def _mm_fwd_rows(name, a, wrows, resid, gate):
    S, K = a.shape
    _, N = wrows.shape
    tm = _pick(S, 512, LANES)
    tn = _pick(N, 1024, LANES)
    tk = _pick(K, 2048, LANES)
    grid = (S // tm, N // tn, K // tk)
    a_spec = pl.BlockSpec((tm, tk), lambda i, j, k: (i, k))
    b_spec = pl.BlockSpec((tk, tn), lambda i, j, k: (k, j))
    o_spec = pl.BlockSpec((tm, tn), lambda i, j, k: (i, j))
    g_spec = pl.BlockSpec((1, tn), lambda i, j, k: (0, j))

    def epi(res, ex, o):
        o[0][...] = res
        o[1][...] = ex[0][...] + ex[1][...] * res

    sd = jax.ShapeDtypeStruct((S, N), F32)
    return _matmul(name, grid, NN, a, a_spec, wrows, b_spec, [sd, sd], [o_spec, o_spec], (tm, tn),
                   extras=(resid, gate), extra_specs=(o_spec, g_spec), epilogue=epi)


def _mm_dgrad_cols(name, dy, wfull, out_dtype=F32):
    planes, S, npl = dy.shape
    _, K, n = wfull.shape
    tm = _pick(S, 1024, LANES)
    to = _pick(K, 1024, LANES)
    tk = _pick(n, 1408, LANES)
    per = n // tk
    nk = N_CHIPS * per
    pk = nk // planes
    grid = (S // tm, K // to, nk)
    a_spec = pl.BlockSpec((None, tm, tk), lambda i, j, k: (k // pk, i, k % pk))
    b_spec = pl.BlockSpec((None, to, tk), lambda i, j, k: (k // per, j, k % per))
    o_spec = pl.BlockSpec((tm, to), lambda i, j, k: (i, j))
    return _matmul(name, grid, NT, dy, a_spec, wfull, b_spec, [jax.ShapeDtypeStruct((S, K), out_dtype)], [o_spec],
                   (tm, to))[0]


def _mm_dgrad_rows(name, dy, wrows, out_dtype=F32):
    S, N = dy.shape
    K, _ = wrows.shape
    tm = _pick(S, 1024, LANES)
    to = _pick(K, 1408, LANES)
    tk = _pick(N, 2048, LANES)
    grid = (S // tm, K // to, N // tk)
    a_spec = pl.BlockSpec((tm, tk), lambda i, j, k: (i, k))
    b_spec = pl.BlockSpec((to, tk), lambda i, j, k: (j, k))
    o_spec = pl.BlockSpec((tm, to), lambda i, j, k: (i, j))
    return _matmul(name, grid, NT, dy, a_spec, wrows, b_spec, [jax.ShapeDtypeStruct((S, K), out_dtype)], [o_spec],
                   (tm, to))[0]


def _mm_wgrad_cols(name, a, dy, n):
    S, K = a.shape
    planes, _, npl = dy.shape
    N = planes * npl
    to = _pick(K, 1024, LANES)
    tn = _pick(n, 1408, LANES)
    ts = _pick(S, 512, LANES)
    per = n // tn
    nj = N // tn
    pj = nj // planes
    grid = (K // to, nj, S // ts)
    a_spec = pl.BlockSpec((ts, to), lambda i, j, k: (k, i))
    b_spec = pl.BlockSpec((None, ts, tn), lambda i, j, k: (j // pj, k, j % pj))
    o_spec = pl.BlockSpec((None, to, tn), lambda i, j, k: (j // per, i, j % per))
    return _matmul(name, grid, TN, a, a_spec, dy, b_spec, [jax.ShapeDtypeStruct((N_CHIPS, K, n), F32)], [o_spec],
                   (to, tn))[0]


def _mm_wgrad_rows(name, a, dy, kshard):
    S, K = a.shape
    _, N = dy.shape
    to = _pick(kshard, 1408, LANES)
    tn = _pick(N, 1024, LANES)
    ts = _pick(S, 512, LANES)
    per = kshard // to
    grid = (K // to, N // tn, S // ts)
    a_spec = pl.BlockSpec((ts, to), lambda i, j, k: (k, i))
    b_spec = pl.BlockSpec((ts, tn), lambda i, j, k: (k, j))
    o_spec = pl.BlockSpec((None, to, tn), lambda i, j, k: (i // per, i % per, j))
    return _matmul(name, grid, TN, a, a_spec, dy, b_spec, [jax.ShapeDtypeStruct((N_CHIPS, kshard, N), F32)],
                   [o_spec], (to, tn))[0]


def _rows(tm, width, colblk=0):
    return pl.BlockSpec((tm, width), lambda i: (i, colblk))


def _whole(shape):
    nd = len(shape)
    return pl.BlockSpec(shape, lambda i: (0,) * nd)


def _prev_halo(tm, h, width, colblk=0):
    r = tm // h
    return pl.BlockSpec((h, width), lambda i: (jnp.maximum(i * r - 1, 0), colblk))


def _next_halo(tm, h, width, nblk, colblk=0):
    r = tm // h
    return pl.BlockSpec((h, width), lambda i: (jnp.minimum((i + 1) * r, nblk - 1), colblk))


def _accumulate(i, ref, val):
    @pl.when(i == 0)
    def _():
        ref[...] = val

    @pl.when(i > 0)
    def _():
        ref[...] += val


def _modnorm_fwd(name, x, g, scale, shift):
    S, D = x.shape
    tm = _pick(S, 512, LANES)

    def body(x_ref, g_ref, sc_ref, sh_ref, h_ref):
        h_ref[...] = _modnorm(x_ref[...], g_ref[...], sc_ref[...], sh_ref[...]).astype(BF16)

    vec = _whole((1, D))
    return pl.pallas_call(
        body, name=name, grid=(S // tm,), in_specs=[_rows(tm, D), vec, vec, vec], out_specs=_rows(tm, D),
        out_shape=jax.ShapeDtypeStruct((S, D), BF16), compiler_params=_params(("parallel",)),
    )(x, g, scale, shift)


def _modnorm_bwd(name, x, dh, dx_in, g, scale, shift):
    S, D = x.shape
    tm = _pick(S, 256, LANES)

    def body(x_ref, dh_ref, dxin_ref, g_ref, sc_ref, sh_ref, dx_ref, dg_ref, dsc_ref, dsh_ref):
        i = pl.program_id(0)
        _, pull = jax.vjp(_modnorm, x_ref[...], g_ref[...], sc_ref[...], sh_ref[...])
        dx, dg, dsc, dsh = pull(dh_ref[...])
        dx_ref[...] = dxin_ref[...] + dx
        _accumulate(i, dg_ref, dg)
        _accumulate(i, dsc_ref, dsc)
        _accumulate(i, dsh_ref, dsh)

    vec = _whole((1, D))
    row = _rows(tm, D)
    vsd = jax.ShapeDtypeStruct((1, D), F32)
    return pl.pallas_call(
        body, name=name, grid=(S // tm,), in_specs=[row, row, row, vec, vec, vec],
        out_specs=[row, vec, vec, vec], out_shape=[jax.ShapeDtypeStruct((S, D), F32), vsd, vsd, vsd],
        compiler_params=_params(("arbitrary",)),
    )(x, dh, dx_in, g, scale, shift)


def _conv_fwd(name, proj, w, b, lg, lb, dc):
    S = proj.shape[0]
    K = w.shape[0]
    H = CONV_HALO
    tm = _pick(S, 256, LANES)

    def glu(v):
        return v[:, :dc] * jax.nn.sigmoid(v[:, dc:])

    def body(cur_ref, prev_ref, w_ref, b_ref, lg_ref, lb_ref, conv_ref, act_ref, ext):
        i = pl.program_id(0)
        ext[0:H, :] = jnp.where(i > 0, glu(prev_ref[...]), 0.0)
        ext[H:, :] = glu(cur_ref[...])
        acc = jnp.zeros((tm, dc), F32) + b_ref[...]
        for k in range(K):
            acc = acc + w_ref[k:k + 1, :] * ext[H - (K - 1) + k:H - (K - 1) + k + tm, :]
        conv_ref[...] = acc
        act_ref[...] = _ln_silu(acc, lg_ref[...], lb_ref[...]).astype(BF16)

    vec = _whole((1, dc))
    return pl.pallas_call(
        body, name=name, grid=(S // tm,),
        in_specs=[_rows(tm, 2 * dc), _prev_halo(tm, H, 2 * dc), _whole(w.shape), vec, vec, vec],
        out_specs=[_rows(tm, dc), _rows(tm, dc)],
        out_shape=[jax.ShapeDtypeStruct((S, dc), F32), jax.ShapeDtypeStruct((S, dc), BF16)],
        scratch_shapes=[pltpu.VMEM((tm + H, dc), F32)],
        compiler_params=_params(("parallel",)),
    )(proj, proj, w, b, lg, lb)


def _conv_bwd_ln(name, conv, dact, lg, lb):
    S, dc = conv.shape
    tm = _pick(S, 256, LANES)

    def body(c_ref, d_ref, lg_ref, lb_ref, dc_ref, dlg_ref, dlb_ref, db_ref):
        i = pl.program_id(0)
        _, pull = jax.vjp(_ln_silu, c_ref[...], lg_ref[...], lb_ref[...])
        dcv, dlg, dlb = pull(d_ref[...])
        dc_ref[...] = dcv
        _accumulate(i, dlg_ref, dlg)
        _accumulate(i, dlb_ref, dlb)
        _accumulate(i, db_ref, jnp.sum(dcv, axis=0, keepdims=True))

    vec = _whole((1, dc))
    row = _rows(tm, dc)
    vsd = jax.ShapeDtypeStruct((1, dc), F32)
    return pl.pallas_call(
        body, name=name, grid=(S // tm,), in_specs=[row, row, vec, vec], out_specs=[row, vec, vec, vec],
        out_shape=[jax.ShapeDtypeStruct((S, dc), F32), vsd, vsd, vsd],
        compiler_params=_params(("arbitrary",)),
    )(conv, dact, lg, lb)


def _conv_bwd(name, dconv, proj, w, dproj, dc):
    S = proj.shape[0]
    K = w.shape[0]
    H = CONV_HALO
    tm = _pick(S, 256, LANES)
    nt = S // tm

    def body(d_ref, dn_ref, cur_ref, prev_ref, w_ref, dproj_any, da_ref, dw_ref, dbin_ref, extg, extd):
        i = pl.program_id(0)
        pv = prev_ref[...]
        cv = cur_ref[...]
        sig = jax.nn.sigmoid(cv[:, dc:])
        extg[0:H, :] = jnp.where(i > 0, pv[:, :dc] * jax.nn.sigmoid(pv[:, dc:]), 0.0)
        extg[H:, :] = cv[:, :dc] * sig
        dcur = d_ref[...]
        extd[0:tm, :] = dcur
        extd[tm:, :] = jnp.where(i < nt - 1, dn_ref[...], 0.0)
        @pl.when(i == 0)
        def _():
            dw_ref[...] = jnp.zeros_like(dw_ref)

        dglu = jnp.zeros((tm, dc), F32)
        for k in range(K):
            dglu = dglu + w_ref[k:k + 1, :] * extd[K - 1 - k:K - 1 - k + tm, :]
            dw_ref[k:k + 1, :] += jnp.sum(dcur * extg[H - (K - 1) + k:H - (K - 1) + k + tm, :], axis=0, keepdims=True)
        da = jnp.concatenate([dglu * sig, dglu * cv[:, :dc] * sig * (1.0 - sig)], axis=1)
        da_ref[...] = da.astype(BF16)
        _accumulate(i, dbin_ref, jnp.sum(da, axis=0, keepdims=True))

    kp = -(-K // SUBLANES) * SUBLANES
    return pl.pallas_call(
        body, name=name, grid=(nt,),
        in_specs=[_rows(tm, dc), _next_halo(tm, H, dc, S // H), _rows(tm, 2 * dc), _prev_halo(tm, H, 2 * dc),
                  _whole(w.shape), pl.BlockSpec(memory_space=pl.ANY)],
        out_specs=[_rows(tm, 2 * dc), _whole((kp, dc)), _whole((1, 2 * dc))],
        out_shape=[jax.ShapeDtypeStruct(dproj.shape, BF16), jax.ShapeDtypeStruct((kp, dc), F32),
                   jax.ShapeDtypeStruct((1, 2 * dc), F32)],
        scratch_shapes=[pltpu.VMEM((tm + H, dc), F32), pltpu.VMEM((tm + H, dc), F32)],
        input_output_aliases={5: 0},
        compiler_params=_params(("arbitrary",)),
    )(dconv, dconv, proj, proj, w, dproj)


def _mix(vln, wsm, bst, out_ref, G, CH, hd):
    for n in range(vln.shape[0] // CH):
        for g in range(G):
            blk = vln[n * CH:(n + 1) * CH, g * hd:(g + 1) * hd].astype(BF16)
            out_ref[n * CH:(n + 1) * CH, g * hd:(g + 1) * hd] = (
                jnp.dot(wsm[g], blk, preferred_element_type=F32) + bst[:, g:g + 1])


def _sgu_fwd(name, proj, a_act, wco, wso, lg, lb, ws, bst, D, ds):
    S = proj.shape[0]
    dc = a_act.shape[1]
    G, CH, _ = ws.shape
    hd = ds // G
    nco = wco.shape[2]
    tm = _pick(S, 256, CH)

    def body(s_ref, gt_ref, a_ref, wco_ref, wso_ref, lg_ref, lb_ref, ws_ref, bst_ref,
             ya_ref, yb_ref, uv_ref, mg_ref, vmix):
        z = _gelu(s_ref[...])
        vln = _layer_norm(z[:, ds:], lg_ref[...], lb_ref[...])
        wsm = [_tril_mask(ws_ref[g]).astype(BF16) for g in range(G)]
        _mix(vln, wsm, bst_ref[...], vmix, G, CH, hd)
        uv = (z[:, :ds] * vmix[...]).astype(BF16)
        uv_ref[...] = uv
        a = a_ref[...]
        for p in range(N_CHIPS):
            ya_ref[:, p * nco:(p + 1) * nco] = jnp.dot(a, wco_ref[p], preferred_element_type=F32)
            yb_ref[:, p * nco:(p + 1) * nco] = jnp.dot(uv, wso_ref[p], preferred_element_type=F32)
        gt = gt_ref[...]
        mg_ref[...] = (jax.nn.sigmoid(gt[:, :D]) * ya_ref[...] + jax.nn.sigmoid(gt[:, D:]) * yb_ref[...]).astype(BF16)

    vec = _whole((1, ds))
    sdf = jax.ShapeDtypeStruct((S, D), F32)
    return pl.pallas_call(
        body, name=name, grid=(S // tm,),
        in_specs=[_rows(tm, 2 * ds, 1), _rows(tm, 2 * D, 1), _rows(tm, dc), _whole(wco.shape), _whole(wso.shape),
                  vec, vec, _whole(ws.shape), _whole(bst.shape)],
        out_specs=[_rows(tm, D), _rows(tm, D), _rows(tm, ds), _rows(tm, D)],
        out_shape=[sdf, sdf, jax.ShapeDtypeStruct((S, ds), BF16), jax.ShapeDtypeStruct((S, D), BF16)],
        scratch_shapes=[pltpu.VMEM((tm, ds), F32)],
        compiler_params=_params(("parallel",)),
    )(proj, proj, a_act, wco, wso, lg, lb, ws, bst)


def _merge_bwd(name, dmerged, proj, ya, yb, D):
    S = proj.shape[0]
    tm = _pick(S, 256, LANES)

    def body(dm_ref, gt_ref, ya_ref, yb_ref, dya_ref, dyb_ref, dg_ref, dbin_ref):
        i = pl.program_id(0)
        dm = dm_ref[...]
        gt = gt_ref[...]
        sa = jax.nn.sigmoid(gt[:, :D])
        sb = jax.nn.sigmoid(gt[:, D:])
        dya_ref[...] = (dm * sa).astype(BF16)
        dyb_ref[...] = (dm * sb).astype(BF16)
        dg = jnp.concatenate([dm * ya_ref[...] * sa * (1.0 - sa), dm * yb_ref[...] * sb * (1.0 - sb)], axis=1)
        dg_ref[...] = dg.astype(BF16)
        _accumulate(i, dbin_ref, jnp.sum(dg, axis=0, keepdims=True))

    row = _rows(tm, D)
    sdb = jax.ShapeDtypeStruct((S, D), BF16)
    return pl.pallas_call(
        body, name=name, grid=(S // tm,),
        in_specs=[row, _rows(tm, 2 * D, 1), row, row],
        out_specs=[row, row, _rows(tm, 2 * D, 1), _whole((1, 2 * D))],
        out_shape=[sdb, sdb, jax.ShapeDtypeStruct((S, 4 * D), BF16), jax.ShapeDtypeStruct((1, 2 * D), F32)],
        compiler_params=_params(("arbitrary",)),
    )(dmerged, proj, ya, yb)


def _sgu_bwd(name, proj, duv, lg, lb, ws, bst, dproj, ds):
    S = proj.shape[0]
    G, CH, _ = ws.shape
    hd = ds // G
    tm = _pick(S, 256, CH)

    def body(s_ref, duv_ref, lg_ref, lb_ref, ws_ref, bst_ref, dproj_any,
             dsin_ref, dws_ref, dbs_ref, dlg_ref, dlb_ref, dbin_ref, vmix, dvln):
        i = pl.program_id(0)
        z, pull_gelu = jax.vjp(_gelu, s_ref[...])
        u = z[:, :ds]
        vln, pull_ln = jax.vjp(_layer_norm, z[:, ds:], lg_ref[...], lb_ref[...])
        wsm = [_tril_mask(ws_ref[g]).astype(BF16) for g in range(G)]
        _mix(vln, wsm, bst_ref[...], vmix, G, CH, hd)
        duv = duv_ref[...]
        du = duv * vmix[...]
        dvmix = duv * u
        for g in range(G):
            dws_g = jnp.zeros((CH, CH), F32)
            dbs_g = jnp.zeros((CH, 1), F32)
            for n in range(tm // CH):
                dblk = dvmix[n * CH:(n + 1) * CH, g * hd:(g + 1) * hd]
                vblk = vln[n * CH:(n + 1) * CH, g * hd:(g + 1) * hd].astype(BF16)
                dvln[n * CH:(n + 1) * CH, g * hd:(g + 1) * hd] = lax.dot_general(
                    wsm[g], dblk.astype(BF16), TN, preferred_element_type=F32)
                dws_g = dws_g + lax.dot_general(dblk.astype(BF16), vblk, NT, preferred_element_type=F32)
                dbs_g = dbs_g + jnp.sum(dblk, axis=1, keepdims=True)
            dws_g = _tril_mask(dws_g)
            dbs_g = jnp.broadcast_to(dbs_g, (CH, LANES))

            @pl.when(i == 0)
            def _():
                dws_ref[g] = dws_g
                dbs_ref[g] = dbs_g

            @pl.when(i > 0)
            def _():
                dws_ref[g] += dws_g
                dbs_ref[g] += dbs_g

        dv, dlg, dlb = pull_ln(dvln[...])
        (dsin,) = pull_gelu(jnp.concatenate([du, dv], axis=1))
        dsin_ref[...] = dsin.astype(BF16)
        _accumulate(i, dlg_ref, dlg)
        _accumulate(i, dlb_ref, dlb)
        _accumulate(i, dbin_ref, jnp.sum(dsin, axis=0, keepdims=True))

    vec = _whole((1, ds))
    vsd = jax.ShapeDtypeStruct((1, ds), F32)
    return pl.pallas_call(
        body, name=name, grid=(S // tm,),
        in_specs=[_rows(tm, 2 * ds, 1), _rows(tm, ds), vec, vec, _whole(ws.shape), _whole(bst.shape),
                  pl.BlockSpec(memory_space=pl.ANY)],
        out_specs=[_rows(tm, 2 * ds, 1), _whole((G, CH, CH)), _whole((G, CH, LANES)), vec, vec, _whole((1, 2 * ds))],
        out_shape=[jax.ShapeDtypeStruct(dproj.shape, BF16), jax.ShapeDtypeStruct((G, CH, CH), F32),
                   jax.ShapeDtypeStruct((G, CH, LANES), F32), vsd, vsd, jax.ShapeDtypeStruct((1, 2 * ds), F32)],
        scratch_shapes=[pltpu.VMEM((tm, ds), F32), pltpu.VMEM((tm, ds), F32)],
        input_output_aliases={6: 0},
        compiler_params=_params(("arbitrary",)),
    )(proj, duv, lg, lb, ws, bst, dproj)


def _ffn_tiles(S, Fh):
    return _pick(S, 256, LANES), _pick(Fh, 1408, LANES)


def _ffn_fwd(name, up, w, b):
    _, S, Fh = up.shape
    K = w.shape[1]
    H = FFN_HALO
    tm, cw = _ffn_tiles(S, Fh)
    r = tm // H

    def body(cur_ref, prev_ref, w_ref, b_ref, act_ref, ext):
        i = pl.program_id(1)
        ext[:, 0:H, :] = jnp.where(i > 0, prev_ref[...], 0.0)
        ext[:, H:, :] = cur_ref[...]
        c2 = []
        for pln in range(2):
            acc = jnp.zeros((tm, cw), F32) + b_ref[pln]
            for k in range(K):
                acc = acc + w_ref[pln, k:k + 1, :] * ext[pln, H - (K - 1) + k:H - (K - 1) + k + tm, :]
            c2.append(acc)
        act_ref[...] = (jax.nn.silu(c2[1]) * c2[0]).astype(BF16)

    return pl.pallas_call(
        body, name=name, grid=(Fh // cw, S // tm),
        in_specs=[pl.BlockSpec((2, tm, cw), lambda j, i: (0, i, j)),
                  pl.BlockSpec((2, H, cw), lambda j, i: (0, jnp.maximum(i * r - 1, 0), j)),
                  pl.BlockSpec((2, K, cw), lambda j, i: (0, 0, j)),
                  pl.BlockSpec((2, 1, cw), lambda j, i: (0, 0, j))],
        out_specs=pl.BlockSpec((tm, cw), lambda j, i: (i, j)),
        out_shape=jax.ShapeDtypeStruct((S, Fh), BF16),
        scratch_shapes=[pltpu.VMEM((2, tm + H, cw), F32)],
        compiler_params=_params(("parallel", "parallel")),
    )(up, up, w, b)


def _silu_mul(val, gt):
    return jax.nn.silu(gt) * val


def _ffn_bwd(name, up, dact, w, b):
    _, S, Fh = up.shape
    K = w.shape[1]
    H = FFN_HALO
    tm, cw = _ffn_tiles(S, Fh)
    r = tm // H
    nt = S // tm
    nhb = S // H
    te = tm + H

    def body(cur_ref, prev_ref, next_ref, d_ref, dn_ref, w_ref, b_ref, dup_ref, dwb_ref, ext, dext, dc2):
        i = pl.program_id(1)
        ext[:, 0:H, :] = jnp.where(i > 0, prev_ref[...], 0.0)
        ext[:, H:H + tm, :] = cur_ref[...]
        ext[:, H + tm:, :] = jnp.where(i < nt - 1, next_ref[...], 0.0)
        dext[0:tm, :] = d_ref[...]
        dext[tm:, :] = jnp.where(i < nt - 1, dn_ref[...], 0.0)
        c2 = []
        for pln in range(2):
            acc = jnp.zeros((te, cw), F32) + b_ref[pln]
            for k in range(K):
                acc = acc + w_ref[pln, k:k + 1, :] * ext[pln, H - (K - 1) + k:H - (K - 1) + k + te, :]
            c2.append(acc)
        _, pull = jax.vjp(_silu_mul, c2[0], c2[1])
        dval, dgt = pull(dext[...])
        dc2[0] = dval
        dc2[1] = dgt
        @pl.when(i == 0)
        def _():
            dwb_ref[...] = jnp.zeros_like(dwb_ref)

        for pln in range(2):
            dup = jnp.zeros((tm, cw), F32)
            dcur = dc2[pln, 0:tm, :]
            for k in range(K):
                dup = dup + w_ref[pln, k:k + 1, :] * dc2[pln, K - 1 - k:K - 1 - k + tm, :]
                dwb_ref[pln, k:k + 1, :] += jnp.sum(
                    dcur * ext[pln, H - (K - 1) + k:H - (K - 1) + k + tm, :], axis=0, keepdims=True)
            dwb_ref[pln, K:K + 1, :] += jnp.sum(dcur, axis=0, keepdims=True)
            dup_ref[pln] = dup.astype(BF16)

    return pl.pallas_call(
        body, name=name, grid=(Fh // cw, nt),
        in_specs=[pl.BlockSpec((2, tm, cw), lambda j, i: (0, i, j)),
                  pl.BlockSpec((2, H, cw), lambda j, i: (0, jnp.maximum(i * r - 1, 0), j)),
                  pl.BlockSpec((2, H, cw), lambda j, i: (0, jnp.minimum((i + 1) * r, nhb - 1), j)),
                  pl.BlockSpec((tm, cw), lambda j, i: (i, j)),
                  pl.BlockSpec((H, cw), lambda j, i: (jnp.minimum((i + 1) * r, nhb - 1), j)),
                  pl.BlockSpec((2, K, cw), lambda j, i: (0, 0, j)),
                  pl.BlockSpec((2, 1, cw), lambda j, i: (0, 0, j))],
        out_specs=[pl.BlockSpec((2, tm, cw), lambda j, i: (0, i, j)),
                   pl.BlockSpec((2, SUBLANES, cw), lambda j, i: (0, 0, j))],
        out_shape=[jax.ShapeDtypeStruct((2, S, Fh), BF16), jax.ShapeDtypeStruct((2, SUBLANES, Fh), F32)],
        scratch_shapes=[pltpu.VMEM((2, tm + 2 * H, cw), F32), pltpu.VMEM((te, cw), F32),
                        pltpu.VMEM((2, te, cw), F32)],
        compiler_params=_params(("parallel", "arbitrary")),
    )(up, up, up, dact, dact, w, b)


def _rms(x, g):
    return x * lax.rsqrt(jnp.mean(x * x, axis=-1, keepdims=True) + EPS) * g


def _final(name, x2, target, gf):
    S, D = x2.shape
    tm = _pick(S, 256, LANES)

    def body(x_ref, t_ref, g_ref, dx_ref, dg_ref, loss_ref):
        i = pl.program_id(0)
        y, pull = jax.vjp(_rms, x_ref[...], g_ref[...])
        e = y - t_ref[...]
        dx, dg = pull(e / D)
        dx_ref[...] = dx
        _accumulate(i, dg_ref, dg)
        part = 0.5 * jnp.sum(jnp.mean(jnp.square(e), axis=-1, keepdims=True), axis=0, keepdims=True)
        _accumulate(i, loss_ref, jnp.broadcast_to(part, (SUBLANES, LANES)))

    row = _rows(tm, D)
    return pl.pallas_call(
        body, name=name, grid=(S // tm,), in_specs=[row, row, _whole((1, D))],
        out_specs=[row, _whole((1, D)), _whole((SUBLANES, LANES))],
        out_shape=[jax.ShapeDtypeStruct((S, D), F32), jax.ShapeDtypeStruct((1, D), F32),
                   jax.ShapeDtypeStruct((SUBLANES, LANES), F32)],
        compiler_params=_params(("arbitrary",)),
    )(x2, target, gf)


def _gate_bwd(name, dx, out, gate):
    S, D = dx.shape
    tm = _pick(S, 512, LANES)

    def body(dx_ref, o_ref, g_ref, do_ref, dg_ref):
        i = pl.program_id(0)
        dxv = dx_ref[...]
        do_ref[...] = (dxv * g_ref[...]).astype(BF16)
        _accumulate(i, dg_ref, jnp.sum(dxv * o_ref[...], axis=0, keepdims=True))

    row = _rows(tm, D)
    return pl.pallas_call(
        body, name=name, grid=(S // tm,), in_specs=[row, row, _whole((1, D))],
        out_specs=[row, _whole((1, D))],
        out_shape=[jax.ShapeDtypeStruct((S, D), BF16), jax.ShapeDtypeStruct((1, D), F32)],
        compiler_params=_params(("arbitrary",)),
    )(dx, out, gate)


def _ada_fwd(name, c_pad, w_ada, b_cols):
    nb, D = c_pad.shape
    n = w_ada.shape[1]
    tn = _pick(n, 1024, LANES)

    def body(c_ref, w_ref, b_ref, o_ref):
        o_ref[...] = jnp.dot(jax.nn.silu(c_ref[...]).astype(BF16), w_ref[...].astype(BF16),
                             preferred_element_type=F32) + b_ref[...]

    return pl.pallas_call(
        body, name=name, grid=(n // tn,),
        in_specs=[_whole((nb, D)), pl.BlockSpec((D, tn), lambda j: (0, j)), pl.BlockSpec((1, tn), lambda j: (0, j))],
        out_specs=pl.BlockSpec((nb, tn), lambda j: (0, j)),
        out_shape=jax.ShapeDtypeStruct((nb, n), F32), compiler_params=_params(("parallel",)),
    )(c_pad, w_ada, b_cols)


def _ada_wgrad(name, c_t, dmod_cols):
    D, nb = c_t.shape
    n = dmod_cols.shape[1]
    tr = _pick(D, 256, SUBLANES)

    def body(c_ref, d_ref, o_ref):
        ca = jax.nn.silu(c_ref[...])
        acc = ca[:, 0:1] * d_ref[0:1, :]
        for b in range(1, nb):
            acc = acc + ca[:, b:b + 1] * d_ref[b:b + 1, :]
        o_ref[...] = acc

    return pl.pallas_call(
        body, name=name, grid=(D // tr,),
        in_specs=[pl.BlockSpec((tr, nb), lambda i: (i, 0)), _whole((nb, n))],
        out_specs=pl.BlockSpec((tr, n), lambda i: (i, 0)),
        out_shape=jax.ShapeDtypeStruct((D, n), F32), compiler_params=_params(("parallel",)),
    )(c_t, dmod_cols)


def kernel(x, c, w_ada, b_ada, norm1_g, w_in, b_in, conv_dw_w, conv_dw_b, conv_ln_g, conv_ln_b, w_conv_out, sgu_ln_g, sgu_ln_b, w_spatial, b_spatial, w_sgu_out, w_out, norm2_g, w_up, ffn_dw_w, ffn_dw_b, w_down, final_g, loss_target, m_w_ada, m_b_ada, m_norm1_g, m_w_in, m_b_in, m_conv_dw_w, m_conv_dw_b, m_conv_ln_g, m_conv_ln_b, m_w_conv_out, m_sgu_ln_g, m_sgu_ln_b, m_w_spatial, m_b_spatial, m_w_sgu_out, m_w_out, m_norm2_g, m_w_up, m_ffn_dw_w, m_ffn_dw_b, m_w_down, m_final_g, v_w_ada, v_b_ada, v_norm1_g, v_w_in, v_b_in, v_conv_dw_w, v_conv_dw_b, v_conv_ln_g, v_conv_ln_b, v_w_conv_out, v_sgu_ln_g, v_sgu_ln_b, v_w_spatial, v_b_spatial, v_w_sgu_out, v_w_out, v_norm2_g, v_w_up, v_ffn_dw_w, v_ffn_dw_b, v_w_down, v_final_g):
    S, D = x.shape[1], x.shape[2]
    dc = w_conv_out.shape[1]
    ds = w_sgu_out.shape[1]
    G, CH = w_spatial.shape[1], w_spatial.shape[2]
    KC = conv_dw_w.shape[1]
    KF = ffn_dw_w.shape[1]
    F2 = ffn_dw_b.shape[1]
    Fh = F2 // 2
    n_ada = w_ada.shape[2]
    n_up = w_up.shape[2]
    ax, ay, ac = _axes()
    chip = 2 * ax + ay
    me = 2 * chip + ac
    c_idx = jnp.reshape(ac, (1,)).astype(jnp.int32)
    p_idx = jnp.reshape(chip, (1,)).astype(jnp.int32)

    xs = x[0]
    tgt = loss_target[0]

    g1 = _allgather8(_pack([c[0], conv_dw_w[0], ffn_dw_w[0]]), "gather_small_in")
    parts = [_unpack(g1[2 * q], [(D,), conv_dw_w.shape[1:], ffn_dw_w.shape[1:]]) for q in range(N_CHIPS)]
    c_all = jnp.stack([_unpack(g1[d], [(D,)])[0] for d in range(N_DEV)])
    cw_full = jnp.concatenate([pt[1] for pt in parts], axis=1)
    fw_full = jnp.concatenate([pt[2] for pt in parts], axis=1)

    b_cols = lax.dynamic_slice(b_ada, (0, chip * n_ada), (1, n_ada))
    c_pad = jnp.concatenate([c_all, jnp.zeros_like(c_all)], axis=0)
    mod_blk = _ada_fwd("ada_fwd", c_pad, w_ada[0], b_cols)[:N_DEV]
    g2 = _allgather8(_pack([mod_blk]), "gather_mod")
    mod_all = jnp.concatenate([_unpack(g2[2 * q], [(N_DEV, n_ada)])[0] for q in range(N_CHIPS)], axis=1)
    mod = lax.dynamic_slice(mod_all, (me, 0), (1, 6 * D))
    shift1, scale1, gate1, shift2, scale2, gate2 = [mod[:, k * D:(k + 1) * D] for k in range(6)]

    shards = [w_in[0], w_conv_out[0], w_sgu_out[0], w_out[0], w_up[0], w_down[0]]
    win_f, wco_f, wso_f, wout_f, wup_f, wdown_f = _gather_weights([s.astype(BF16) for s in shards], "gather_weights")
    wout_r = wout_f.reshape(-1, wout_f.shape[2])
    wdown_r = wdown_f.reshape(-1, wdown_f.shape[2])

    h1 = _modnorm_fwd("modnorm1", xs, norm1_g, scale1, shift1)
    proj = _mm_fwd("proj", h1, win_f, bias=b_in)[0]
    conv, a_act = _conv_fwd("conv_fwd", proj, cw_full, conv_dw_b, conv_ln_g, conv_ln_b, dc)
    bst = jnp.transpose(b_spatial[0])
    ya, yb, uv, merged = _sgu_fwd("sgu_fwd", proj, a_act, wco_f, wso_f, sgu_ln_g, sgu_ln_b, w_spatial[0], bst, D, ds)
    out1, x1 = _mm_fwd_rows("out1", merged, wout_r, xs, gate1)
    h2 = _modnorm_fwd("modnorm2", x1, norm2_g, scale2, shift2)
    up = _mm_fwd("up", h2, wup_f, planes=2)
    fw2 = jnp.stack([fw_full[:, :Fh], fw_full[:, Fh:]])
    fb2 = jnp.stack([ffn_dw_b[:, :Fh], ffn_dw_b[:, Fh:]])
    act = _ffn_fwd("ffn_fwd", up, fw2, fb2)
    out2, x2 = _mm_fwd_rows("out2", act, wdown_r, x1, gate2)
    dx2, d_final_g, loss_blk = _final("final", x2, tgt, final_g.reshape(1, D))
    loss = lax.psum(loss_blk[0, 0], ("x", "y", "c"))

    dout2, d_gate2 = _gate_bwd("gate2_bwd", dx2, out2, gate2)
    g_wdown = _mm_wgrad_rows("wgrad_down", act, dout2, w_down.shape[1])
    dact = _mm_dgrad_rows("dgrad_down", dout2, wdown_r)
    dup, d_ffn = _ffn_bwd("ffn_bwd", up, dact, fw2, fb2)
    g_wup = _mm_wgrad_cols("wgrad_up", h2, dup, n_up)
    dh2 = _mm_dgrad_cols("dgrad_up", dup, wup_f)
    dx1, d_norm2, d_scale2, d_shift2 = _modnorm_bwd("modnorm2_bwd", x1, dh2, dx2, norm2_g, scale2, shift2)
    dout1, d_gate1 = _gate_bwd("gate1_bwd", dx1, out1, gate1)
    g_wout = _mm_wgrad_rows("wgrad_out", merged, dout1, w_out.shape[1])
    dmerged = _mm_dgrad_rows("dgrad_out", dout1, wout_r)
    dya, dyb, dproj, dbin_g = _merge_bwd("merge_bwd", dmerged, proj, ya, yb, D)
    g_wco = _mm_wgrad_cols("wgrad_conv_out", a_act, dya[None], w_conv_out.shape[2])
    g_wso = _mm_wgrad_cols("wgrad_sgu_out", uv, dyb[None], w_sgu_out.shape[2])
    da_act = _mm_dgrad_cols("dgrad_conv_out", dya[None], wco_f)
    duv = _mm_dgrad_cols("dgrad_sgu_out", dyb[None], wso_f)
    dproj, d_ws, d_bs, d_sgu_g, d_sgu_b, dbin_s = _sgu_bwd("sgu_bwd", proj, duv, sgu_ln_g, sgu_ln_b, w_spatial[0], bst,
                                                           dproj, ds)
    dconv, d_cln_g, d_cln_b, d_conv_b = _conv_bwd_ln("conv_ln_bwd", conv, da_act, conv_ln_g, conv_ln_b)
    dproj, d_cw, dbin_a = _conv_bwd("conv_bwd", dconv, proj, cw_full, dproj, dc)
    g_win = _mm_wgrad_cols("wgrad_in", h1, dproj[None], w_in.shape[2])
    dh1 = _mm_dgrad_cols("dgrad_in", dproj[None], win_f)
    dxs, d_norm1, d_scale1, d_shift1 = _modnorm_bwd("modnorm1_bwd", xs, dh1, dx1, norm1_g, scale1, shift1)

    d_mod = jnp.concatenate([d_shift1, d_scale1, d_gate1, d_shift2, d_scale2, d_gate2], axis=1)
    d_b_in = jnp.concatenate([dbin_a, dbin_s, dbin_g], axis=1)
    d_fw = jnp.concatenate([d_ffn[0, :KF], d_ffn[1, :KF]], axis=1)
    d_fb = jnp.concatenate([d_ffn[0, KF:KF + 1], d_ffn[1, KF:KF + 1]], axis=1)
    small = [d_mod, d_norm1, d_b_in, d_conv_b, d_cln_g, d_cln_b, d_sgu_g, d_sgu_b, d_ws, d_bs[:, :, 0], d_norm2,
             d_fb, d_final_g, d_cw[:KC], d_fw]
    small_shapes = [a.shape for a in small]
    g3 = _allgather8(_pack(small), "gather_small_grads")
    summed = _unpack(_sum8(g3, "sum_small_grads"), small_shapes)
    (g_b_ada, g_norm1, g_b_in, g_conv_b, g_cln_g, g_cln_b, g_sgu_g, g_sgu_b, g_ws, g_bs, g_norm2, g_fb, g_final,
     g_cw_full, g_fw_full) = summed
    n_cw = conv_dw_w.shape[2]
    n_fw = ffn_dw_w.shape[2]
    g_cw = lax.dynamic_slice(g_cw_full, (0, chip * n_cw), (KC, n_cw))
    g_fw = lax.dynamic_slice(g_fw_full, (0, chip * n_fw), (KF, n_fw))
    dmod_all = jnp.stack([_unpack(g3[d], [(6 * D,)])[0] for d in range(N_DEV)])
    dmod_cols = lax.dynamic_slice(dmod_all, (0, chip * n_ada), (N_DEV, n_ada))
    g_wada = _ada_wgrad("ada_wgrad", jnp.transpose(c_all), dmod_cols)

    big = [g_win, g_wco, g_wso, g_wout, g_wup, g_wdown]
    names = ["in", "conv_out", "sgu_out", "out", "up", "down"]
    recv1 = _swap_halves(big, "reduce_cores")
    sums1 = [_add_own_half(g, r, c_idx, "add_cores_" + nm) for g, r, nm in zip(big, recv1, names)]
    recv2 = _scatter_blocks(sums1, "reduce_chips")
    sums2 = [_add_chips(s, r, p_idx, "add_chips_" + nm) for s, r, nm in zip(sums1, recv2, names)]
    joined = _join_halves(sums2, "join_halves")
    g_win_s, g_wco_s, g_wso_s, g_wout_s, g_wup_s, g_wdown_s = [j.reshape(-1, j.shape[2]) for j in joined]

    grads = {
        "w_ada": g_wada[None], "b_ada": g_b_ada, "norm1_g": g_norm1, "w_in": g_win_s[None], "b_in": g_b_in,
        "conv_dw_w": g_cw[None], "conv_dw_b": g_conv_b, "conv_ln_g": g_cln_g, "conv_ln_b": g_cln_b,
        "w_conv_out": g_wco_s[None], "sgu_ln_g": g_sgu_g, "sgu_ln_b": g_sgu_b, "w_spatial": g_ws[None],
        "b_spatial": g_bs[None], "w_sgu_out": g_wso_s[None], "w_out": g_wout_s[None], "norm2_g": g_norm2,
        "w_up": g_wup_s[None], "ffn_dw_w": g_fw[None], "ffn_dw_b": g_fb, "w_down": g_wdown_s[None],
        "final_g": g_final.reshape(D),
    }
    weights = dict(w_ada=w_ada, b_ada=b_ada, norm1_g=norm1_g, w_in=w_in, b_in=b_in, conv_dw_w=conv_dw_w, conv_dw_b=conv_dw_b, conv_ln_g=conv_ln_g, conv_ln_b=conv_ln_b, w_conv_out=w_conv_out, sgu_ln_g=sgu_ln_g, sgu_ln_b=sgu_ln_b, w_spatial=w_spatial, b_spatial=b_spatial, w_sgu_out=w_sgu_out, w_out=w_out, norm2_g=norm2_g, w_up=w_up, ffn_dw_w=ffn_dw_w, ffn_dw_b=ffn_dw_b, w_down=w_down, final_g=final_g)
    m_in = dict(w_ada=m_w_ada, b_ada=m_b_ada, norm1_g=m_norm1_g, w_in=m_w_in, b_in=m_b_in, conv_dw_w=m_conv_dw_w, conv_dw_b=m_conv_dw_b, conv_ln_g=m_conv_ln_g, conv_ln_b=m_conv_ln_b, w_conv_out=m_w_conv_out, sgu_ln_g=m_sgu_ln_g, sgu_ln_b=m_sgu_ln_b, w_spatial=m_w_spatial, b_spatial=m_b_spatial, w_sgu_out=m_w_sgu_out, w_out=m_w_out, norm2_g=m_norm2_g, w_up=m_w_up, ffn_dw_w=m_ffn_dw_w, ffn_dw_b=m_ffn_dw_b, w_down=m_w_down, final_g=m_final_g)
    v_in = dict(w_ada=v_w_ada, b_ada=v_b_ada, norm1_g=v_norm1_g, w_in=v_w_in, b_in=v_b_in, conv_dw_w=v_conv_dw_w, conv_dw_b=v_conv_dw_b, conv_ln_g=v_conv_ln_g, conv_ln_b=v_conv_ln_b, w_conv_out=v_w_conv_out, sgu_ln_g=v_sgu_ln_g, sgu_ln_b=v_sgu_ln_b, w_spatial=v_w_spatial, b_spatial=v_b_spatial, w_sgu_out=v_w_sgu_out, w_out=v_w_out, norm2_g=v_norm2_g, w_up=v_w_up, ffn_dw_w=v_ffn_dw_w, ffn_dw_b=v_ffn_dw_b, w_down=v_w_down, final_g=v_final_g)
    order = list(weights.keys())
    large = ["w_ada", "w_in", "w_conv_out", "w_sgu_out", "w_out", "w_up", "w_down"]
    little = [n for n in order if n not in large]
    delta, new_m, new_v = {}, {}, {}
    for n in large:
        shp = weights[n].shape
        two = (shp[1], shp[2])
        d_, m_, v_ = _adamw(weights[n].reshape(two), grads[n].reshape(two), m_in[n].reshape(two),
                            v_in[n].reshape(two), "adamw_" + n)
        delta[n], new_m[n], new_v[n] = d_.reshape(shp), m_.reshape(shp), v_.reshape(shp)
    shapes = [weights[n].shape for n in little]
    d_, m_, v_ = _adamw(_pack([weights[n] for n in little]), _pack([grads[n] for n in little]),
                        _pack([m_in[n] for n in little]), _pack([v_in[n] for n in little]), "adamw_small")
    for n, dd, mm, vv in zip(little, _unpack(d_, shapes), _unpack(m_, shapes), _unpack(v_, shapes)):
        delta[n], new_m[n], new_v[n] = dd, mm, vv
    grad_out = [grads[n].reshape(weights[n].shape) for n in order]
    return (loss, dxs[None], *grad_out, *[delta[n] for n in order], *[new_m[n] for n in order],
            *[new_v[n] for n in order])
```

```python
import functools

import jax
import jax.numpy as jnp
from jax import lax
from jax.experimental import pallas as pl
from jax.experimental.pallas import tpu as pltpu

F32 = jnp.float32
BF16 = jnp.bfloat16
EPS = 1e-6
MESH = pl.DeviceIdType.MESH
N_CHIPS = 4
N_DEV = 8
LANES = 128
SUBLANES = 8
CONV_HALO = 32
FFN_HALO = 8
VMEM_LIMIT_BYTES = 56 * 1024 * 1024

ADAM_LR = 0.001
ADAM_B1 = 0.9
ADAM_B2 = 0.999
ADAM_EPS = 1e-08
ADAM_WD = 0.01
ADAM_STEP = 10

NN = (((1,), (0,)), ((), ()))
NT = (((1,), (1,)), ((), ()))
TN = (((0,), (0,)), ((), ()))


def _params(sem=None):
    return pltpu.CompilerParams(dimension_semantics=sem, vmem_limit_bytes=VMEM_LIMIT_BYTES)


def _pick(dim, pref, mult):
    best = None
    d = mult
    while d <= min(dim, pref):
        if dim % d == 0:
            best = d
        d += mult
    return dim if best is None else best


def _axes():
    return lax.axis_index("x"), lax.axis_index("y"), lax.axis_index("c")


def _modnorm(x, g, scale, shift):
    r = lax.rsqrt(jnp.mean(x * x, axis=-1, keepdims=True) + EPS)
    return (x * r * g) * (1.0 + scale) + shift


def _layer_norm(x, g, b):
    mu = jnp.mean(x, axis=-1, keepdims=True)
    var = jnp.mean(jnp.square(x - mu), axis=-1, keepdims=True)
    return (x - mu) * lax.rsqrt(var + EPS) * g + b


def _gelu(x):
    return 0.5 * x * (1.0 + lax.erf(x * (0.5 ** 0.5)))


def _ln_silu(x, g, b):
    return jax.nn.silu(_layer_norm(x, g, b))


def _tril_mask(ws):
    n = ws.shape[-1]
    row = lax.broadcasted_iota(jnp.int32, (n, n), 0)
    col = lax.broadcasted_iota(jnp.int32, (n, n), 1)
    return jnp.where(row >= col, ws, 0.0)


def _pack(arrs):
    flat = [a.reshape(-1).astype(F32) for a in arrs]
    total = sum(f.shape[0] for f in flat)
    tile = SUBLANES * LANES
    padded = -(-total // tile) * tile
    if padded > total:
        flat = flat + [jnp.zeros((padded - total,), F32)]
    return jnp.concatenate(flat).reshape(padded // LANES, LANES)


def _unpack(buf, shapes):
    flat = buf.reshape(-1)
    out, off = [], 0
    for s in shapes:
        n = 1
        for d in s:
            n *= d
        out.append(flat[off:off + n].reshape(s))
        off += n
    return out


def _allgather8(buf, name):
    R, L = buf.shape

    def body(in_ref, out_ref, send_sems, recv_sems, local_sem):
        x, y, c = _axes()
        me = 4 * x + 2 * y + c
        mine = pltpu.make_async_copy(in_ref, out_ref.at[me], local_sem)
        mine.start()
        peers = []
        for k in range(1, N_DEV):
            px = 1 - x if k & 4 else x
            py = 1 - y if k & 2 else y
            pc = 1 - c if k & 1 else c
            peers.append((px, py, pc))
        sends = []
        for k, peer in enumerate(peers):
            cp = pltpu.make_async_remote_copy(
                src_ref=in_ref, dst_ref=out_ref.at[me], send_sem=send_sems.at[k], recv_sem=recv_sems.at[k],
                device_id=peer, device_id_type=MESH)
            cp.start()
            sends.append(cp)
        for k, (px, py, pc) in enumerate(peers):
            pltpu.make_async_remote_copy(
                src_ref=in_ref, dst_ref=out_ref.at[4 * px + 2 * py + pc], send_sem=send_sems.at[k],
                recv_sem=recv_sems.at[k], device_id=(px, py, pc), device_id_type=MESH).wait_recv()
        for cp in sends:
            cp.wait_send()
        mine.wait()

    return pl.pallas_call(
        body, name=name,
        out_shape=jax.ShapeDtypeStruct((N_DEV, R, L), buf.dtype),
        in_specs=[pl.BlockSpec(memory_space=pltpu.VMEM)],
        out_specs=pl.BlockSpec(memory_space=pltpu.VMEM),
        scratch_shapes=[pltpu.SemaphoreType.DMA((N_DEV - 1,)), pltpu.SemaphoreType.DMA((N_DEV - 1,)),
                        pltpu.SemaphoreType.DMA],
        compiler_params=pltpu.CompilerParams(vmem_limit_bytes=VMEM_LIMIT_BYTES),
    )(buf)


def _other_chips(x, y):
    return [(1 - x, y), (x, 1 - y), (1 - x, 1 - y)]


def _cast_into_block(shard, p_idx, name):
    K, n = shard.shape
    tr = _pick(K, max(SUBLANES, (1 << 19) // n), 2 * SUBLANES)

    def body(p_ref, s_ref, o_ref):
        o_ref[...] = s_ref[...].astype(BF16)

    return pl.pallas_call(
        body, name=name,
        grid_spec=pltpu.PrefetchScalarGridSpec(
            num_scalar_prefetch=1, grid=(K // tr,),
            in_specs=[pl.BlockSpec((tr, n), lambda i, p: (i, 0))],
            out_specs=pl.BlockSpec((None, tr, n), lambda i, p: (p[0], i, 0))),
        out_shape=jax.ShapeDtypeStruct((N_CHIPS, K, n), BF16),
        compiler_params=_params(("parallel",)),
    )(p_idx, shard)


def _gather_weights(fulls, name):
    nw = len(fulls)

    def body(*refs):
        outs = refs[nw:2 * nw]
        send_sems, recv_sems = refs[2 * nw:]
        x, y, c = _axes()
        p = 2 * x + y
        chips = _other_chips(x, y)
        started = []
        for w in range(nw):
            kh = outs[w].shape[1] // 2
            mine = pl.ds(c * kh, kh)
            for j, (qx, qy) in enumerate(chips):
                cp = pltpu.make_async_remote_copy(
                    src_ref=outs[w].at[p, mine], dst_ref=outs[w].at[p, mine],
                    send_sem=send_sems.at[6 * w + j], recv_sem=recv_sems.at[6 * w + j],
                    device_id=(qx, qy, c), device_id_type=MESH)
                cp.start()
                started.append(cp)
        for w in range(nw):
            kh = outs[w].shape[1] // 2
            mine = pl.ds(c * kh, kh)
            for j, (qx, qy) in enumerate(chips):
                q = 2 * qx + qy
                pltpu.make_async_remote_copy(
                    src_ref=outs[w].at[q, mine], dst_ref=outs[w].at[q, mine],
                    send_sem=send_sems.at[6 * w + j], recv_sem=recv_sems.at[6 * w + j],
                    device_id=(qx, qy, c), device_id_type=MESH).wait_recv()
                fw = pltpu.make_async_remote_copy(
                    src_ref=outs[w].at[q, mine], dst_ref=outs[w].at[q, mine],
                    send_sem=send_sems.at[6 * w + 3 + j], recv_sem=recv_sems.at[6 * w + 3 + j],
                    device_id=(x, y, 1 - c), device_id_type=MESH)
                fw.start()
                started.append(fw)
        for w in range(nw):
            kh = outs[w].shape[1] // 2
            theirs = pl.ds((1 - c) * kh, kh)
            for j, (qx, qy) in enumerate(chips):
                q = 2 * qx + qy
                pltpu.make_async_remote_copy(
                    src_ref=outs[w].at[q, theirs], dst_ref=outs[w].at[q, theirs],
                    send_sem=send_sems.at[6 * w + 3 + j], recv_sem=recv_sems.at[6 * w + 3 + j],
                    device_id=(x, y, 1 - c), device_id_type=MESH).wait_recv()
        for cp in started:
            cp.wait_send()

    any_spec = pl.BlockSpec(memory_space=pl.ANY)
    return pl.pallas_call(
        body, name=name,
        out_shape=[jax.ShapeDtypeStruct(f.shape, f.dtype) for f in fulls],
        in_specs=[any_spec] * nw, out_specs=[any_spec] * nw,
        input_output_aliases={w: w for w in range(nw)},
        scratch_shapes=[pltpu.SemaphoreType.DMA((6 * nw,)), pltpu.SemaphoreType.DMA((6 * nw,))],
    )(*fulls)


def _swap_halves(grads, name):
    nw = len(grads)

    def body(*refs):
        ins, outs = refs[:nw], refs[nw:2 * nw]
        send_sems, recv_sems = refs[2 * nw:]
        x, y, c = _axes()
        copies = []
        for w in range(nw):
            kh = ins[w].shape[1] // 2
            cp = pltpu.make_async_remote_copy(
                src_ref=ins[w].at[:, pl.ds((1 - c) * kh, kh), :], dst_ref=outs[w],
                send_sem=send_sems.at[w], recv_sem=recv_sems.at[w],
                device_id=(x, y, 1 - c), device_id_type=MESH)
            cp.start()
            copies.append(cp)
        for cp in copies:
            cp.wait()

    any_spec = pl.BlockSpec(memory_space=pl.ANY)
    return pl.pallas_call(
        body, name=name,
        out_shape=[jax.ShapeDtypeStruct((g.shape[0], g.shape[1] // 2, g.shape[2]), g.dtype) for g in grads],
        in_specs=[any_spec] * nw, out_specs=[any_spec] * nw,
        scratch_shapes=[pltpu.SemaphoreType.DMA((nw,)), pltpu.SemaphoreType.DMA((nw,))],
    )(*grads)


def _scatter_blocks(sums, name):
    nw = len(sums)

    def body(*refs):
        ins, outs = refs[:nw], refs[nw:2 * nw]
        send_sems, recv_sems = refs[2 * nw:]
        x, y, c = _axes()
        chips = _other_chips(x, y)
        copies = []
        for w in range(nw):
            for j, (qx, qy) in enumerate(chips):
                cp = pltpu.make_async_remote_copy(
                    src_ref=ins[w].at[2 * qx + qy], dst_ref=outs[w].at[j],
                    send_sem=send_sems.at[3 * w + j], recv_sem=recv_sems.at[3 * w + j],
                    device_id=(qx, qy, c), device_id_type=MESH)
                cp.start()
                copies.append(cp)
        for cp in copies:
            cp.wait()

    any_spec = pl.BlockSpec(memory_space=pl.ANY)
    return pl.pallas_call(
        body, name=name,
        out_shape=[jax.ShapeDtypeStruct((3,) + s.shape[1:], s.dtype) for s in sums],
        in_specs=[any_spec] * nw, out_specs=[any_spec] * nw,
        scratch_shapes=[pltpu.SemaphoreType.DMA((3 * nw,)), pltpu.SemaphoreType.DMA((3 * nw,))],
    )(*sums)


def _send_to_sibling(halves, name):
    nw = len(halves)

    def body(*refs):
        ins, outs = refs[:nw], refs[nw:2 * nw]
        send_sems, recv_sems = refs[2 * nw:]
        x, y, c = _axes()
        copies = []
        for w in range(nw):
            cp = pltpu.make_async_remote_copy(
                src_ref=ins[w], dst_ref=outs[w], send_sem=send_sems.at[w], recv_sem=recv_sems.at[w],
                device_id=(x, y, 1 - c), device_id_type=MESH)
            cp.start()
            copies.append(cp)
        for cp in copies:
            cp.wait()

    any_spec = pl.BlockSpec(memory_space=pl.ANY)
    return pl.pallas_call(
        body, name=name,
        out_shape=[jax.ShapeDtypeStruct(h.shape, h.dtype) for h in halves],
        in_specs=[any_spec] * nw, out_specs=[any_spec] * nw,
        scratch_shapes=[pltpu.SemaphoreType.DMA((nw,)), pltpu.SemaphoreType.DMA((nw,))],
    )(*halves)


def _add_own_half(g, r, c_idx, name):
    nb, K, n = g.shape
    kh = K // 2
    tr = _pick(kh, max(SUBLANES, (1 << 19) // n), 2 * SUBLANES)
    per = kh // tr

    def body(c_ref, g_ref, r_ref, o_ref):
        o_ref[...] = (g_ref[...] + r_ref[...]).astype(BF16)

    return pl.pallas_call(
        body, name=name,
        grid_spec=pltpu.PrefetchScalarGridSpec(
            num_scalar_prefetch=1, grid=(nb, per),
            in_specs=[pl.BlockSpec((None, tr, n), lambda b, i, c: (b, c[0] * per + i, 0)),
                      pl.BlockSpec((None, tr, n), lambda b, i, c: (b, i, 0))],
            out_specs=pl.BlockSpec((None, tr, n), lambda b, i, c: (b, i, 0))),
        out_shape=jax.ShapeDtypeStruct((nb, kh, n), BF16),
        compiler_params=_params(("parallel", "parallel")),
    )(c_idx, g, r)


def _add_chips(g, r1, r2, pc_idx, name):
    _, K, n = g.shape
    kh = K // 2
    tr = _pick(kh, max(SUBLANES, (1 << 19) // n), 2 * SUBLANES)
    per = kh // tr

    def body(pc_ref, g_ref, r1_ref, r2_ref, o_ref):
        own = g_ref[...] + r1_ref[...]
        o_ref[...] = ((own + r2_ref[0].astype(F32)) + r2_ref[1].astype(F32)) + r2_ref[2].astype(F32)

    return pl.pallas_call(
        body, name=name,
        grid_spec=pltpu.PrefetchScalarGridSpec(
            num_scalar_prefetch=1, grid=(per,),
            in_specs=[pl.BlockSpec((None, tr, n), lambda i, pc: (pc[0], pc[1] * per + i, 0)),
                      pl.BlockSpec((None, tr, n), lambda i, pc: (pc[0], i, 0)),
                      pl.BlockSpec((3, tr, n), lambda i, pc: (0, i, 0))],
            out_specs=pl.BlockSpec((tr, n), lambda i, pc: (i, 0))),
        out_shape=jax.ShapeDtypeStruct((kh, n), F32),
        compiler_params=_params(("parallel",)),
    )(pc_idx, g, r1, r2)


def _sum8(g, name):
    _, R, L = g.shape

    def body(g_ref, o_ref):
        acc = g_ref[0]
        for d in range(1, N_DEV):
            acc = acc + g_ref[d]
        o_ref[...] = acc

    return pl.pallas_call(
        body, name=name, out_shape=jax.ShapeDtypeStruct((R, L), F32),
        in_specs=[pl.BlockSpec(memory_space=pltpu.VMEM)], out_specs=pl.BlockSpec(memory_space=pltpu.VMEM),
        compiler_params=_params(),
    )(g)


def _adamw_math(w, gg, m, v):
    nm = ADAM_B1 * m + (1.0 - ADAM_B1) * gg
    nv = ADAM_B2 * v + (1.0 - ADAM_B2) * jnp.square(gg)
    m_hat = nm / (1.0 - ADAM_B1 ** ADAM_STEP)
    v_hat = nv / (1.0 - ADAM_B2 ** ADAM_STEP)
    return -ADAM_LR * (m_hat / (jnp.sqrt(v_hat) + ADAM_EPS) + ADAM_WD * w), nm, nv


def _adamw(w, g, m, v, name):
    R, C = w.shape
    tr = _pick(R, max(SUBLANES, (1 << 18) // C), SUBLANES)

    def body(w_ref, g_ref, m_ref, v_ref, d_ref, nm_ref, nv_ref):
        d_ref[...], nm_ref[...], nv_ref[...] = _adamw_math(w_ref[...], g_ref[...], m_ref[...], v_ref[...])

    spec = pl.BlockSpec((tr, C), lambda i: (i, 0))
    sd = jax.ShapeDtypeStruct((R, C), F32)
    return pl.pallas_call(
        body, name=name, grid=(R // tr,), in_specs=[spec] * 4, out_specs=[spec] * 3, out_shape=[sd] * 3,
        compiler_params=_params(("parallel",)),
    )(w, g, m, v)


def _adamw_halves(w, g_own, g_sib, m, v, c_idx, name):
    K, n = w.shape
    kh = K // 2
    tr = _pick(kh, max(SUBLANES, (1 << 18) // n), SUBLANES)
    per = kh // tr

    def body(c_ref, w_ref, go_ref, gs_ref, m_ref, v_ref, g_ref, d_ref, nm_ref, nv_ref):
        h = pl.program_id(0)

        def step(gg):
            g_ref[...] = gg
            d_ref[...], nm_ref[...], nv_ref[...] = _adamw_math(w_ref[...], gg, m_ref[...], v_ref[...])

        @pl.when(h == 0)
        def _():
            step(go_ref[...])

        @pl.when(h == 1)
        def _():
            step(gs_ref[...])

    full = pl.BlockSpec((tr, n), lambda h, i, c: (((c[0] + h) % 2) * per + i, 0))
    own = pl.BlockSpec((tr, n), lambda h, i, c: (i * (1 - h), 0))
    sib = pl.BlockSpec((tr, n), lambda h, i, c: (i * h, 0))
    sd = jax.ShapeDtypeStruct((K, n), F32)
    return pl.pallas_call(
        body, name=name,
        grid_spec=pltpu.PrefetchScalarGridSpec(
            num_scalar_prefetch=1, grid=(2, per),
            in_specs=[full, own, sib, full, full], out_specs=[full] * 4),
        out_shape=[sd] * 4,
        compiler_params=_params(("arbitrary", "arbitrary")),
    )(c_idx, w, g_own, g_sib, m, v)


def _matmul(name, grid, dims, a, a_spec, b, b_spec, outs, out_specs, acc_shape,
            extras=(), extra_specs=(), epilogue=None):
    nk = grid[2]
    nex = len(extras)
    nout = len(outs)

    def body(a_ref, b_ref, *rest):
        ex, o = rest[:nex], rest[nex:nex + nout]
        part = lax.dot_general(a_ref[...].astype(BF16), b_ref[...].astype(BF16), dims,
                               preferred_element_type=F32)

        def finish(res):
            if epilogue is None:
                o[0][...] = res.astype(o[0].dtype)
            else:
                epilogue(res, ex, o)

        if nk == 1:
            finish(part)
        else:
            acc = rest[-1]
            k = pl.program_id(2)

            @pl.when(k == 0)
            def _():
                acc[...] = part

            @pl.when(k > 0)
            def _():
                acc[...] += part

            @pl.when(k == nk - 1)
            def _():
                finish(acc[...])

    return pl.pallas_call(
        body, name=name, grid=grid,
        in_specs=[a_spec, b_spec] + list(extra_specs), out_specs=list(out_specs), out_shape=list(outs),
        scratch_shapes=[] if nk == 1 else [pltpu.VMEM(acc_shape, F32)],
        compiler_params=_params(("parallel", "parallel", "arbitrary")),
    )(a, b, *extras)


def _mm_fwd(name, a, wfull, out_dtype=F32, planes=1, bias=None, resid=None, gate=None):
    S, K = a.shape
    _, _, n = wfull.shape
    N = N_CHIPS * n
    tm = _pick(S, 512 if resid is not None else 1024, LANES)
    tn = _pick(n, 1408, LANES)
    per = n // tn
    nj = N // tn
    pj = nj // planes
    grid = (S // tm, nj, 1)
    a_spec = pl.BlockSpec((tm, K), lambda i, j, k: (i, 0))
    b_spec = pl.BlockSpec((None, K, tn), lambda i, j, k: (j // per, 0, j % per))
    if resid is not None:
        o_spec = pl.BlockSpec((tm, tn), lambda i, j, k: (i, j))
        g_spec = pl.BlockSpec((1, tn), lambda i, j, k: (0, j))

        def epi(res, ex, o):
            o[0][...] = res
            o[1][...] = ex[0][...] + ex[1][...] * res

        sd = jax.ShapeDtypeStruct((S, N), F32)
        return _matmul(name, grid, NN, a, a_spec, wfull, b_spec, [sd, sd], [o_spec, o_spec], (tm, tn),
                       extras=(resid, gate), extra_specs=(o_spec, g_spec), epilogue=epi)
    o_spec = pl.BlockSpec((None, tm, tn), lambda i, j, k: (j // pj, i, j % pj))
    sd = jax.ShapeDtypeStruct((planes, S, N // planes), out_dtype)
    if bias is not None:
        def epi(res, ex, o):
            o[0][...] = (res + ex[0][...]).astype(o[0].dtype)

        out = _matmul(name, grid, NN, a, a_spec, wfull, b_spec, [sd], [o_spec], (tm, tn),
                      extras=(bias,), extra_specs=(pl.BlockSpec((1, tn), lambda i, j, k: (0, j)),), epilogue=epi)
    else:
        out = _matmul(name, grid, NN, a, a_spec, wfull, b_spec, [sd], [o_spec], (tm, tn))
    return out[0]


def _mm_fwd_rows(name, a, wrows, resid, gate):
    S, K = a.shape
    _, N = wrows.shape
    tm = _pick(S, 512, LANES)
    tn = _pick(N, 1024, LANES)
    tk = _pick(K, 2048, LANES)
    grid = (S // tm, N // tn, K // tk)
    a_spec = pl.BlockSpec((tm, tk), lambda i, j, k: (i, k))
    b_spec = pl.BlockSpec((tk, tn), lambda i, j, k: (k, j))
    o_spec = pl.BlockSpec((tm, tn), lambda i, j, k: (i, j))
    g_spec = pl.BlockSpec((1, tn), lambda i, j, k: (0, j))

    def epi(res, ex, o):
        o[0][...] = res
        o[1][...] = ex[0][...] + ex[1][...] * res

    sd = jax.ShapeDtypeStruct((S, N), F32)
    return _matmul(name, grid, NN, a, a_spec, wrows, b_spec, [sd, sd], [o_spec, o_spec], (tm, tn),
                   extras=(resid, gate), extra_specs=(o_spec, g_spec), epilogue=epi)


def _mm_dgrad_cols(name, dy, wfull, out_dtype=F32):
    planes, S, npl = dy.shape
    _, K, n = wfull.shape
    tm = _pick(S, 1024, LANES)
    to = _pick(K, 1024, LANES)
    tk = _pick(n, 1408, LANES)
    per = n // tk
    nk = N_CHIPS * per
    pk = nk // planes
    grid = (S // tm, K // to, nk)
    a_spec = pl.BlockSpec((None, tm, tk), lambda i, j, k: (k // pk, i, k % pk))
    b_spec = pl.BlockSpec((None, to, tk), lambda i, j, k: (k // per, j, k % per))
    o_spec = pl.BlockSpec((tm, to), lambda i, j, k: (i, j))
    return _matmul(name, grid, NT, dy, a_spec, wfull, b_spec, [jax.ShapeDtypeStruct((S, K), out_dtype)], [o_spec],
                   (tm, to))[0]


def _mm_dgrad_rows(name, dy, wrows, out_dtype=F32):
    S, N = dy.shape
    K, _ = wrows.shape
    tm = _pick(S, 1024, LANES)
    to = _pick(K, 1408, LANES)
    tk = _pick(N, 2048, LANES)
    grid = (S // tm, K // to, N // tk)
    a_spec = pl.BlockSpec((tm, tk), lambda i, j, k: (i, k))
    b_spec = pl.BlockSpec((to, tk), lambda i, j, k: (j, k))
    o_spec = pl.BlockSpec((tm, to), lambda i, j, k: (i, j))
    return _matmul(name, grid, NT, dy, a_spec, wrows, b_spec, [jax.ShapeDtypeStruct((S, K), out_dtype)], [o_spec],
                   (tm, to))[0]


def _mm_wgrad_cols(name, a, dy, n):
    S, K = a.shape
    planes, _, npl = dy.shape
    N = planes * npl
    to = _pick(K, 1024, LANES)
    tn = _pick(n, 1408, LANES)
    ts = _pick(S, 512, LANES)
    per = n // tn
    nj = N // tn
    pj = nj // planes
    grid = (K // to, nj, S // ts)
    a_spec = pl.BlockSpec((ts, to), lambda i, j, k: (k, i))
    b_spec = pl.BlockSpec((None, ts, tn), lambda i, j, k: (j // pj, k, j % pj))
    o_spec = pl.BlockSpec((None, to, tn), lambda i, j, k: (j // per, i, j % per))
    return _matmul(name, grid, TN, a, a_spec, dy, b_spec, [jax.ShapeDtypeStruct((N_CHIPS, K, n), F32)], [o_spec],
                   (to, tn))[0]


def _mm_wgrad_rows(name, a, dy, kshard):
    S, K = a.shape
    _, N = dy.shape
    to = _pick(kshard, 1408, LANES)
    tn = _pick(N, 1024, LANES)
    ts = _pick(S, 512, LANES)
    per = kshard // to
    grid = (K // to, N // tn, S // ts)
    a_spec = pl.BlockSpec((ts, to), lambda i, j, k: (k, i))
    b_spec = pl.BlockSpec((ts, tn), lambda i, j, k: (k, j))
    o_spec = pl.BlockSpec((None, to, tn), lambda i, j, k: (i // per, i % per, j))
    return _matmul(name, grid, TN, a, a_spec, dy, b_spec, [jax.ShapeDtypeStruct((N_CHIPS, kshard, N), F32)],
                   [o_spec], (to, tn))[0]


def _rows(tm, width, colblk=0):
    return pl.BlockSpec((tm, width), lambda i: (i, colblk))


def _whole(shape):
    nd = len(shape)
    return pl.BlockSpec(shape, lambda i: (0,) * nd)


def _prev_halo(tm, h, width, colblk=0):
    r = tm // h
    return pl.BlockSpec((h, width), lambda i: (jnp.maximum(i * r - 1, 0), colblk))


def _next_halo(tm, h, width, nblk, colblk=0):
    r = tm // h
    return pl.BlockSpec((h, width), lambda i: (jnp.minimum((i + 1) * r, nblk - 1), colblk))


def _accumulate(i, ref, val):
    @pl.when(i == 0)
    def _():
        ref[...] = val

    @pl.when(i > 0)
    def _():
        ref[...] += val


def _modnorm_fwd(name, x, g, scale, shift):
    S, D = x.shape
    tm = _pick(S, 512, LANES)

    def body(x_ref, g_ref, sc_ref, sh_ref, h_ref):
        h_ref[...] = _modnorm(x_ref[...], g_ref[...], sc_ref[...], sh_ref[...]).astype(BF16)

    vec = _whole((1, D))
    return pl.pallas_call(
        body, name=name, grid=(S // tm,), in_specs=[_rows(tm, D), vec, vec, vec], out_specs=_rows(tm, D),
        out_shape=jax.ShapeDtypeStruct((S, D), BF16), compiler_params=_params(("parallel",)),
    )(x, g, scale, shift)


def _modnorm_bwd(name, x, dh, dx_in, g, scale, shift):
    S, D = x.shape
    tm = _pick(S, 256, LANES)

    def body(x_ref, dh_ref, dxin_ref, g_ref, sc_ref, sh_ref, dx_ref, dg_ref, dsc_ref, dsh_ref):
        i = pl.program_id(0)
        _, pull = jax.vjp(_modnorm, x_ref[...], g_ref[...], sc_ref[...], sh_ref[...])
        dx, dg, dsc, dsh = pull(dh_ref[...])
        dx_ref[...] = dxin_ref[...] + dx
        _accumulate(i, dg_ref, dg)
        _accumulate(i, dsc_ref, dsc)
        _accumulate(i, dsh_ref, dsh)

    vec = _whole((1, D))
    row = _rows(tm, D)
    vsd = jax.ShapeDtypeStruct((1, D), F32)
    return pl.pallas_call(
        body, name=name, grid=(S // tm,), in_specs=[row, row, row, vec, vec, vec],
        out_specs=[row, vec, vec, vec], out_shape=[jax.ShapeDtypeStruct((S, D), F32), vsd, vsd, vsd],
        compiler_params=_params(("arbitrary",)),
    )(x, dh, dx_in, g, scale, shift)


def _conv_fwd(name, proj, w, b, lg, lb, dc):
    S = proj.shape[0]
    K = w.shape[0]
    H = CONV_HALO
    tm = _pick(S, 256, LANES)

    def glu(v):
        return v[:, :dc] * jax.nn.sigmoid(v[:, dc:])

    def body(cur_ref, prev_ref, w_ref, b_ref, lg_ref, lb_ref, conv_ref, act_ref, ext):
        i = pl.program_id(0)
        ext[0:H, :] = jnp.where(i > 0, glu(prev_ref[...]), 0.0)
        ext[H:, :] = glu(cur_ref[...])
        acc = jnp.zeros((tm, dc), F32) + b_ref[...]
        for k in range(K):
            acc = acc + w_ref[k:k + 1, :] * ext[H - (K - 1) + k:H - (K - 1) + k + tm, :]
        conv_ref[...] = acc
        act_ref[...] = _ln_silu(acc, lg_ref[...], lb_ref[...]).astype(BF16)

    vec = _whole((1, dc))
    return pl.pallas_call(
        body, name=name, grid=(S // tm,),
        in_specs=[_rows(tm, 2 * dc), _prev_halo(tm, H, 2 * dc), _whole(w.shape), vec, vec, vec],
        out_specs=[_rows(tm, dc), _rows(tm, dc)],
        out_shape=[jax.ShapeDtypeStruct((S, dc), F32), jax.ShapeDtypeStruct((S, dc), BF16)],
        scratch_shapes=[pltpu.VMEM((tm + H, dc), F32)],
        compiler_params=_params(("parallel",)),
    )(proj, proj, w, b, lg, lb)


def _conv_bwd_ln(name, conv, dact, lg, lb):
    S, dc = conv.shape
    tm = _pick(S, 256, LANES)

    def body(c_ref, d_ref, lg_ref, lb_ref, dc_ref, dlg_ref, dlb_ref, db_ref):
        i = pl.program_id(0)
        _, pull = jax.vjp(_ln_silu, c_ref[...], lg_ref[...], lb_ref[...])
        dcv, dlg, dlb = pull(d_ref[...])
        dc_ref[...] = dcv
        _accumulate(i, dlg_ref, dlg)
        _accumulate(i, dlb_ref, dlb)
        _accumulate(i, db_ref, jnp.sum(dcv, axis=0, keepdims=True))

    vec = _whole((1, dc))
    row = _rows(tm, dc)
    vsd = jax.ShapeDtypeStruct((1, dc), F32)
    return pl.pallas_call(
        body, name=name, grid=(S // tm,), in_specs=[row, row, vec, vec], out_specs=[row, vec, vec, vec],
        out_shape=[jax.ShapeDtypeStruct((S, dc), F32), vsd, vsd, vsd],
        compiler_params=_params(("arbitrary",)),
    )(conv, dact, lg, lb)


def _conv_bwd(name, dconv, proj, w, dproj, dc):
    S = proj.shape[0]
    K = w.shape[0]
    H = CONV_HALO
    tm = _pick(S, 256, LANES)
    nt = S // tm

    def body(d_ref, dn_ref, cur_ref, prev_ref, w_ref, dproj_any, da_ref, dw_ref, dbin_ref, extg, extd):
        i = pl.program_id(0)
        pv = prev_ref[...]
        cv = cur_ref[...]
        sig = jax.nn.sigmoid(cv[:, dc:])
        extg[0:H, :] = jnp.where(i > 0, pv[:, :dc] * jax.nn.sigmoid(pv[:, dc:]), 0.0)
        extg[H:, :] = cv[:, :dc] * sig
        dcur = d_ref[...]
        extd[0:tm, :] = dcur
        extd[tm:, :] = jnp.where(i < nt - 1, dn_ref[...], 0.0)
        @pl.when(i == 0)
        def _():
            dw_ref[...] = jnp.zeros_like(dw_ref)

        dglu = jnp.zeros((tm, dc), F32)
        for k in range(K):
            dglu = dglu + w_ref[k:k + 1, :] * extd[K - 1 - k:K - 1 - k + tm, :]
            dw_ref[k:k + 1, :] += jnp.sum(dcur * extg[H - (K - 1) + k:H - (K - 1) + k + tm, :], axis=0, keepdims=True)
        da = jnp.concatenate([dglu * sig, dglu * cv[:, :dc] * sig * (1.0 - sig)], axis=1)
        da_ref[...] = da.astype(BF16)
        _accumulate(i, dbin_ref, jnp.sum(da, axis=0, keepdims=True))

    kp = -(-K // SUBLANES) * SUBLANES
    return pl.pallas_call(
        body, name=name, grid=(nt,),
        in_specs=[_rows(tm, dc), _next_halo(tm, H, dc, S // H), _rows(tm, 2 * dc), _prev_halo(tm, H, 2 * dc),
                  _whole(w.shape), pl.BlockSpec(memory_space=pl.ANY)],
        out_specs=[_rows(tm, 2 * dc), _whole((kp, dc)), _whole((1, 2 * dc))],
        out_shape=[jax.ShapeDtypeStruct(dproj.shape, BF16), jax.ShapeDtypeStruct((kp, dc), F32),
                   jax.ShapeDtypeStruct((1, 2 * dc), F32)],
        scratch_shapes=[pltpu.VMEM((tm + H, dc), F32), pltpu.VMEM((tm + H, dc), F32)],
        input_output_aliases={5: 0},
        compiler_params=_params(("arbitrary",)),
    )(dconv, dconv, proj, proj, w, dproj)


def _mix(vln, wsm, bst, out_ref, G, CH, hd):
    for n in range(vln.shape[0] // CH):
        for g in range(G):
            blk = vln[n * CH:(n + 1) * CH, g * hd:(g + 1) * hd].astype(BF16)
            out_ref[n * CH:(n + 1) * CH, g * hd:(g + 1) * hd] = (
                jnp.dot(wsm[g], blk, preferred_element_type=F32) + bst[:, g:g + 1])


def _sgu_fwd(name, proj, a_act, wco, wso, lg, lb, ws, bst, D, ds):
    S = proj.shape[0]
    dc = a_act.shape[1]
    G, CH, _ = ws.shape
    hd = ds // G
    nco = wco.shape[2]
    tm = _pick(S, 256, CH)

    def body(s_ref, gt_ref, a_ref, wco_ref, wso_ref, lg_ref, lb_ref, ws_ref, bst_ref,
             ya_ref, yb_ref, uv_ref, mg_ref, vmix):
        z = _gelu(s_ref[...])
        vln = _layer_norm(z[:, ds:], lg_ref[...], lb_ref[...])
        wsm = [_tril_mask(ws_ref[g]).astype(BF16) for g in range(G)]
        _mix(vln, wsm, bst_ref[...], vmix, G, CH, hd)
        uv = (z[:, :ds] * vmix[...]).astype(BF16)
        uv_ref[...] = uv
        a = a_ref[...]
        for p in range(N_CHIPS):
            ya_ref[:, p * nco:(p + 1) * nco] = jnp.dot(a, wco_ref[p], preferred_element_type=F32)
            yb_ref[:, p * nco:(p + 1) * nco] = jnp.dot(uv, wso_ref[p], preferred_element_type=F32)
        gt = gt_ref[...]
        mg_ref[...] = (jax.nn.sigmoid(gt[:, :D]) * ya_ref[...] + jax.nn.sigmoid(gt[:, D:]) * yb_ref[...]).astype(BF16)

    vec = _whole((1, ds))
    sdf = jax.ShapeDtypeStruct((S, D), F32)
    return pl.pallas_call(
        body, name=name, grid=(S // tm,),
        in_specs=[_rows(tm, 2 * ds, 1), _rows(tm, 2 * D, 1), _rows(tm, dc), _whole(wco.shape), _whole(wso.shape),
                  vec, vec, _whole(ws.shape), _whole(bst.shape)],
        out_specs=[_rows(tm, D), _rows(tm, D), _rows(tm, ds), _rows(tm, D)],
        out_shape=[sdf, sdf, jax.ShapeDtypeStruct((S, ds), BF16), jax.ShapeDtypeStruct((S, D), BF16)],
        scratch_shapes=[pltpu.VMEM((tm, ds), F32)],
        compiler_params=_params(("parallel",)),
    )(proj, proj, a_act, wco, wso, lg, lb, ws, bst)


def _merge_bwd(name, dmerged, proj, ya, yb, D):
    S = proj.shape[0]
    tm = _pick(S, 256, LANES)

    def body(dm_ref, gt_ref, ya_ref, yb_ref, dya_ref, dyb_ref, dg_ref, dbin_ref):
        i = pl.program_id(0)
        dm = dm_ref[...]
        gt = gt_ref[...]
        sa = jax.nn.sigmoid(gt[:, :D])
        sb = jax.nn.sigmoid(gt[:, D:])
        dya_ref[...] = (dm * sa).astype(BF16)
        dyb_ref[...] = (dm * sb).astype(BF16)
        dg = jnp.concatenate([dm * ya_ref[...] * sa * (1.0 - sa), dm * yb_ref[...] * sb * (1.0 - sb)], axis=1)
        dg_ref[...] = dg.astype(BF16)
        _accumulate(i, dbin_ref, jnp.sum(dg, axis=0, keepdims=True))

    row = _rows(tm, D)
    sdb = jax.ShapeDtypeStruct((S, D), BF16)
    return pl.pallas_call(
        body, name=name, grid=(S // tm,),
        in_specs=[row, _rows(tm, 2 * D, 1), row, row],
        out_specs=[row, row, _rows(tm, 2 * D, 1), _whole((1, 2 * D))],
        out_shape=[sdb, sdb, jax.ShapeDtypeStruct((S, 4 * D), BF16), jax.ShapeDtypeStruct((1, 2 * D), F32)],
        compiler_params=_params(("arbitrary",)),
    )(dmerged, proj, ya, yb)


def _sgu_bwd(name, proj, duv, lg, lb, ws, bst, dproj, ds):
    S = proj.shape[0]
    G, CH, _ = ws.shape
    hd = ds // G
    tm = _pick(S, 256, CH)

    def body(s_ref, duv_ref, lg_ref, lb_ref, ws_ref, bst_ref, dproj_any,
             dsin_ref, dws_ref, dbs_ref, dlg_ref, dlb_ref, dbin_ref, vmix, dvln):
        i = pl.program_id(0)
        z, pull_gelu = jax.vjp(_gelu, s_ref[...])
        u = z[:, :ds]
        vln, pull_ln = jax.vjp(_layer_norm, z[:, ds:], lg_ref[...], lb_ref[...])
        wsm = [_tril_mask(ws_ref[g]).astype(BF16) for g in range(G)]
        _mix(vln, wsm, bst_ref[...], vmix, G, CH, hd)
        duv = duv_ref[...]
        du = duv * vmix[...]
        dvmix = duv * u
        for g in range(G):
            dws_g = jnp.zeros((CH, CH), F32)
            dbs_g = jnp.zeros((CH, 1), F32)
            for n in range(tm // CH):
                dblk = dvmix[n * CH:(n + 1) * CH, g * hd:(g + 1) * hd]
                vblk = vln[n * CH:(n + 1) * CH, g * hd:(g + 1) * hd].astype(BF16)
                dvln[n * CH:(n + 1) * CH, g * hd:(g + 1) * hd] = lax.dot_general(
                    wsm[g], dblk.astype(BF16), TN, preferred_element_type=F32)
                dws_g = dws_g + lax.dot_general(dblk.astype(BF16), vblk, NT, preferred_element_type=F32)
                dbs_g = dbs_g + jnp.sum(dblk, axis=1, keepdims=True)
            dws_g = _tril_mask(dws_g)
            dbs_g = jnp.broadcast_to(dbs_g, (CH, LANES))

            @pl.when(i == 0)
            def _():
                dws_ref[g] = dws_g
                dbs_ref[g] = dbs_g

            @pl.when(i > 0)
            def _():
                dws_ref[g] += dws_g
                dbs_ref[g] += dbs_g

        dv, dlg, dlb = pull_ln(dvln[...])
        (dsin,) = pull_gelu(jnp.concatenate([du, dv], axis=1))
        dsin_ref[...] = dsin.astype(BF16)
        _accumulate(i, dlg_ref, dlg)
        _accumulate(i, dlb_ref, dlb)
        _accumulate(i, dbin_ref, jnp.sum(dsin, axis=0, keepdims=True))

    vec = _whole((1, ds))
    vsd = jax.ShapeDtypeStruct((1, ds), F32)
    return pl.pallas_call(
        body, name=name, grid=(S // tm,),
        in_specs=[_rows(tm, 2 * ds, 1), _rows(tm, ds), vec, vec, _whole(ws.shape), _whole(bst.shape),
                  pl.BlockSpec(memory_space=pl.ANY)],
        out_specs=[_rows(tm, 2 * ds, 1), _whole((G, CH, CH)), _whole((G, CH, LANES)), vec, vec, _whole((1, 2 * ds))],
        out_shape=[jax.ShapeDtypeStruct(dproj.shape, BF16), jax.ShapeDtypeStruct((G, CH, CH), F32),
                   jax.ShapeDtypeStruct((G, CH, LANES), F32), vsd, vsd, jax.ShapeDtypeStruct((1, 2 * ds), F32)],
        scratch_shapes=[pltpu.VMEM((tm, ds), F32), pltpu.VMEM((tm, ds), F32)],
        input_output_aliases={6: 0},
        compiler_params=_params(("arbitrary",)),
    )(proj, duv, lg, lb, ws, bst, dproj)


def _ffn_tiles(S, Fh):
    return _pick(S, 256, LANES), _pick(Fh, 1408, LANES)


def _ffn_fwd(name, up, w, b):
    _, S, Fh = up.shape
    K = w.shape[1]
    H = FFN_HALO
    tm, cw = _ffn_tiles(S, Fh)
    r = tm // H

    def body(cur_ref, prev_ref, w_ref, b_ref, act_ref, ext):
        i = pl.program_id(1)
        ext[:, 0:H, :] = jnp.where(i > 0, prev_ref[...], 0.0)
        ext[:, H:, :] = cur_ref[...]
        c2 = []
        for pln in range(2):
            acc = jnp.zeros((tm, cw), F32) + b_ref[pln]
            for k in range(K):
                acc = acc + w_ref[pln, k:k + 1, :] * ext[pln, H - (K - 1) + k:H - (K - 1) + k + tm, :]
            c2.append(acc)
        act_ref[...] = (jax.nn.silu(c2[1]) * c2[0]).astype(BF16)

    return pl.pallas_call(
        body, name=name, grid=(Fh // cw, S // tm),
        in_specs=[pl.BlockSpec((2, tm, cw), lambda j, i: (0, i, j)),
                  pl.BlockSpec((2, H, cw), lambda j, i: (0, jnp.maximum(i * r - 1, 0), j)),
                  pl.BlockSpec((2, K, cw), lambda j, i: (0, 0, j)),
                  pl.BlockSpec((2, 1, cw), lambda j, i: (0, 0, j))],
        out_specs=pl.BlockSpec((tm, cw), lambda j, i: (i, j)),
        out_shape=jax.ShapeDtypeStruct((S, Fh), BF16),
        scratch_shapes=[pltpu.VMEM((2, tm + H, cw), F32)],
        compiler_params=_params(("parallel", "parallel")),
    )(up, up, w, b)


def _silu_mul(val, gt):
    return jax.nn.silu(gt) * val


def _ffn_bwd(name, up, dact, w, b):
    _, S, Fh = up.shape
    K = w.shape[1]
    H = FFN_HALO
    tm, cw = _ffn_tiles(S, Fh)
    r = tm // H
    nt = S // tm
    nhb = S // H
    te = tm + H

    def body(cur_ref, prev_ref, next_ref, d_ref, dn_ref, w_ref, b_ref, dup_ref, dwb_ref, ext, dext, dc2):
        i = pl.program_id(1)
        ext[:, 0:H, :] = jnp.where(i > 0, prev_ref[...], 0.0)
        ext[:, H:H + tm, :] = cur_ref[...]
        ext[:, H + tm:, :] = jnp.where(i < nt - 1, next_ref[...], 0.0)
        dext[0:tm, :] = d_ref[...]
        dext[tm:, :] = jnp.where(i < nt - 1, dn_ref[...], 0.0)
        c2 = []
        for pln in range(2):
            acc = jnp.zeros((te, cw), F32) + b_ref[pln]
            for k in range(K):
                acc = acc + w_ref[pln, k:k + 1, :] * ext[pln, H - (K - 1) + k:H - (K - 1) + k + te, :]
            c2.append(acc)
        _, pull = jax.vjp(_silu_mul, c2[0], c2[1])
        dval, dgt = pull(dext[...])
        dc2[0] = dval
        dc2[1] = dgt
        @pl.when(i == 0)
        def _():
            dwb_ref[...] = jnp.zeros_like(dwb_ref)

        for pln in range(2):
            dup = jnp.zeros((tm, cw), F32)
            dcur = dc2[pln, 0:tm, :]
            for k in range(K):
                dup = dup + w_ref[pln, k:k + 1, :] * dc2[pln, K - 1 - k:K - 1 - k + tm, :]
                dwb_ref[pln, k:k + 1, :] += jnp.sum(
                    dcur * ext[pln, H - (K - 1) + k:H - (K - 1) + k + tm, :], axis=0, keepdims=True)
            dwb_ref[pln, K:K + 1, :] += jnp.sum(dcur, axis=0, keepdims=True)
            dup_ref[pln] = dup.astype(BF16)

    return pl.pallas_call(
        body, name=name, grid=(Fh // cw, nt),
        in_specs=[pl.BlockSpec((2, tm, cw), lambda j, i: (0, i, j)),
                  pl.BlockSpec((2, H, cw), lambda j, i: (0, jnp.maximum(i * r - 1, 0), j)),
                  pl.BlockSpec((2, H, cw), lambda j, i: (0, jnp.minimum((i + 1) * r, nhb - 1), j)),
                  pl.BlockSpec((tm, cw), lambda j, i: (i, j)),
                  pl.BlockSpec((H, cw), lambda j, i: (jnp.minimum((i + 1) * r, nhb - 1), j)),
                  pl.BlockSpec((2, K, cw), lambda j, i: (0, 0, j)),
                  pl.BlockSpec((2, 1, cw), lambda j, i: (0, 0, j))],
        out_specs=[pl.BlockSpec((2, tm, cw), lambda j, i: (0, i, j)),
                   pl.BlockSpec((2, SUBLANES, cw), lambda j, i: (0, 0, j))],
        out_shape=[jax.ShapeDtypeStruct((2, S, Fh), BF16), jax.ShapeDtypeStruct((2, SUBLANES, Fh), F32)],
        scratch_shapes=[pltpu.VMEM((2, tm + 2 * H, cw), F32), pltpu.VMEM((te, cw), F32),
                        pltpu.VMEM((2, te, cw), F32)],
        compiler_params=_params(("parallel", "arbitrary")),
    )(up, up, up, dact, dact, w, b)


def _rms(x, g):
    return x * lax.rsqrt(jnp.mean(x * x, axis=-1, keepdims=True) + EPS) * g


def _final(name, x2, target, gf):
    S, D = x2.shape
    tm = _pick(S, 256, LANES)

    def body(x_ref, t_ref, g_ref, dx_ref, dg_ref, loss_ref):
        i = pl.program_id(0)
        y, pull = jax.vjp(_rms, x_ref[...], g_ref[...])
        e = y - t_ref[...]
        dx, dg = pull(e / D)
        dx_ref[...] = dx
        _accumulate(i, dg_ref, dg)
        part = 0.5 * jnp.sum(jnp.mean(jnp.square(e), axis=-1, keepdims=True), axis=0, keepdims=True)
        _accumulate(i, loss_ref, jnp.broadcast_to(part, (SUBLANES, LANES)))

    row = _rows(tm, D)
    return pl.pallas_call(
        body, name=name, grid=(S // tm,), in_specs=[row, row, _whole((1, D))],
        out_specs=[row, _whole((1, D)), _whole((SUBLANES, LANES))],
        out_shape=[jax.ShapeDtypeStruct((S, D), F32), jax.ShapeDtypeStruct((1, D), F32),
                   jax.ShapeDtypeStruct((SUBLANES, LANES), F32)],
        compiler_params=_params(("arbitrary",)),
    )(x2, target, gf)


def _gate_bwd(name, dx, out, gate):
    S, D = dx.shape
    tm = _pick(S, 512, LANES)

    def body(dx_ref, o_ref, g_ref, do_ref, dg_ref):
        i = pl.program_id(0)
        dxv = dx_ref[...]
        do_ref[...] = (dxv * g_ref[...]).astype(BF16)
        _accumulate(i, dg_ref, jnp.sum(dxv * o_ref[...], axis=0, keepdims=True))

    row = _rows(tm, D)
    return pl.pallas_call(
        body, name=name, grid=(S // tm,), in_specs=[row, row, _whole((1, D))],
        out_specs=[row, _whole((1, D))],
        out_shape=[jax.ShapeDtypeStruct((S, D), BF16), jax.ShapeDtypeStruct((1, D), F32)],
        compiler_params=_params(("arbitrary",)),
    )(dx, out, gate)


def _ada_fwd(name, c_pad, w_ada, b_cols):
    nb, D = c_pad.shape
    n = w_ada.shape[1]
    tn = _pick(n, 1024, LANES)

    def body(c_ref, w_ref, b_ref, o_ref):
        o_ref[...] = jnp.dot(jax.nn.silu(c_ref[...]).astype(BF16), w_ref[...].astype(BF16),
                             preferred_element_type=F32) + b_ref[...]

    return pl.pallas_call(
        body, name=name, grid=(n // tn,),
        in_specs=[_whole((nb, D)), pl.BlockSpec((D, tn), lambda j: (0, j)), pl.BlockSpec((1, tn), lambda j: (0, j))],
        out_specs=pl.BlockSpec((nb, tn), lambda j: (0, j)),
        out_shape=jax.ShapeDtypeStruct((nb, n), F32), compiler_params=_params(("parallel",)),
    )(c_pad, w_ada, b_cols)


def _ada_wgrad(name, c_t, dmod_cols):
    D, nb = c_t.shape
    n = dmod_cols.shape[1]
    tr = _pick(D, 256, SUBLANES)

    def body(c_ref, d_ref, o_ref):
        ca = jax.nn.silu(c_ref[...])
        acc = ca[:, 0:1] * d_ref[0:1, :]
        for b in range(1, nb):
            acc = acc + ca[:, b:b + 1] * d_ref[b:b + 1, :]
        o_ref[...] = acc

    return pl.pallas_call(
        body, name=name, grid=(D // tr,),
        in_specs=[pl.BlockSpec((tr, nb), lambda i: (i, 0)), _whole((nb, n))],
        out_specs=pl.BlockSpec((tr, n), lambda i: (i, 0)),
        out_shape=jax.ShapeDtypeStruct((D, n), F32), compiler_params=_params(("parallel",)),
    )(c_t, dmod_cols)


def kernel(x, c, w_ada, b_ada, norm1_g, w_in, b_in, conv_dw_w, conv_dw_b, conv_ln_g, conv_ln_b, w_conv_out, sgu_ln_g, sgu_ln_b, w_spatial, b_spatial, w_sgu_out, w_out, norm2_g, w_up, ffn_dw_w, ffn_dw_b, w_down, final_g, loss_target, m_w_ada, m_b_ada, m_norm1_g, m_w_in, m_b_in, m_conv_dw_w, m_conv_dw_b, m_conv_ln_g, m_conv_ln_b, m_w_conv_out, m_sgu_ln_g, m_sgu_ln_b, m_w_spatial, m_b_spatial, m_w_sgu_out, m_w_out, m_norm2_g, m_w_up, m_ffn_dw_w, m_ffn_dw_b, m_w_down, m_final_g, v_w_ada, v_b_ada, v_norm1_g, v_w_in, v_b_in, v_conv_dw_w, v_conv_dw_b, v_conv_ln_g, v_conv_ln_b, v_w_conv_out, v_sgu_ln_g, v_sgu_ln_b, v_w_spatial, v_b_spatial, v_w_sgu_out, v_w_out, v_norm2_g, v_w_up, v_ffn_dw_w, v_ffn_dw_b, v_w_down, v_final_g):
    S, D = x.shape[1], x.shape[2]
    dc = w_conv_out.shape[1]
    ds = w_sgu_out.shape[1]
    G, CH = w_spatial.shape[1], w_spatial.shape[2]
    KC = conv_dw_w.shape[1]
    KF = ffn_dw_w.shape[1]
    F2 = ffn_dw_b.shape[1]
    Fh = F2 // 2
    n_ada = w_ada.shape[2]
    n_up = w_up.shape[2]
    ax, ay, ac = _axes()
    chip = 2 * ax + ay
    me = 2 * chip + ac
    c_idx = jnp.reshape(ac, (1,)).astype(jnp.int32)
    p_idx = jnp.reshape(chip, (1,)).astype(jnp.int32)

    xs = x[0]
    tgt = loss_target[0]

    g1 = _allgather8(_pack([c[0], conv_dw_w[0], ffn_dw_w[0]]), "gather_small_in")
    parts = [_unpack(g1[2 * q], [(D,), conv_dw_w.shape[1:], ffn_dw_w.shape[1:]]) for q in range(N_CHIPS)]
    c_all = jnp.stack([_unpack(g1[d], [(D,)])[0] for d in range(N_DEV)])
    cw_full = jnp.concatenate([pt[1] for pt in parts], axis=1)
    fw_full = jnp.concatenate([pt[2] for pt in parts], axis=1)

    b_cols = lax.dynamic_slice(b_ada, (0, chip * n_ada), (1, n_ada))
    c_pad = jnp.concatenate([c_all, jnp.zeros_like(c_all)], axis=0)
    mod_blk = _ada_fwd("ada_fwd", c_pad, w_ada[0], b_cols)[:N_DEV]
    g2 = _allgather8(_pack([mod_blk]), "gather_mod")
    mod_all = jnp.concatenate([_unpack(g2[2 * q], [(N_DEV, n_ada)])[0] for q in range(N_CHIPS)], axis=1)
    mod = lax.dynamic_slice(mod_all, (me, 0), (1, 6 * D))
    shift1, scale1, gate1, shift2, scale2, gate2 = [mod[:, k * D:(k + 1) * D] for k in range(6)]

    shards = [w_in[0], w_conv_out[0], w_sgu_out[0], w_out[0], w_up[0], w_down[0]]
    names = ["in", "conv_out", "sgu_out", "out", "up", "down"]
    own_blocks = [_cast_into_block(s, p_idx, "cast_" + nm) for s, nm in zip(shards, names)]
    win_f, wco_f, wso_f, wout_f, wup_f, wdown_f = _gather_weights(own_blocks, "gather_weights")
    wout_r = wout_f.reshape(-1, wout_f.shape[2])
    wdown_r = wdown_f.reshape(-1, wdown_f.shape[2])

    h1 = _modnorm_fwd("modnorm1", xs, norm1_g, scale1, shift1)
    proj = _mm_fwd("proj", h1, win_f, bias=b_in)[0]
    conv, a_act = _conv_fwd("conv_fwd", proj, cw_full, conv_dw_b, conv_ln_g, conv_ln_b, dc)
    bst = jnp.transpose(b_spatial[0])
    ya, yb, uv, merged = _sgu_fwd("sgu_fwd", proj, a_act, wco_f, wso_f, sgu_ln_g, sgu_ln_b, w_spatial[0], bst, D, ds)
    out1, x1 = _mm_fwd_rows("out1", merged, wout_r, xs, gate1)
    h2 = _modnorm_fwd("modnorm2", x1, norm2_g, scale2, shift2)
    up = _mm_fwd("up", h2, wup_f, planes=2)
    fw2 = jnp.stack([fw_full[:, :Fh], fw_full[:, Fh:]])
    fb2 = jnp.stack([ffn_dw_b[:, :Fh], ffn_dw_b[:, Fh:]])
    act = _ffn_fwd("ffn_fwd", up, fw2, fb2)
    out2, x2 = _mm_fwd_rows("out2", act, wdown_r, x1, gate2)
    dx2, d_final_g, loss_blk = _final("final", x2, tgt, final_g.reshape(1, D))
    loss = lax.psum(loss_blk[0, 0], ("x", "y", "c"))

    dout2, d_gate2 = _gate_bwd("gate2_bwd", dx2, out2, gate2)
    g_wdown = _mm_wgrad_rows("wgrad_down", act, dout2, w_down.shape[1])
    dact = _mm_dgrad_rows("dgrad_down", dout2, wdown_r)
    dup, d_ffn = _ffn_bwd("ffn_bwd", up, dact, fw2, fb2)
    g_wup = _mm_wgrad_cols("wgrad_up", h2, dup, n_up)
    dh2 = _mm_dgrad_cols("dgrad_up", dup, wup_f)
    dx1, d_norm2, d_scale2, d_shift2 = _modnorm_bwd("modnorm2_bwd", x1, dh2, dx2, norm2_g, scale2, shift2)
    dout1, d_gate1 = _gate_bwd("gate1_bwd", dx1, out1, gate1)
    g_wout = _mm_wgrad_rows("wgrad_out", merged, dout1, w_out.shape[1])
    dmerged = _mm_dgrad_rows("dgrad_out", dout1, wout_r)
    dya, dyb, dproj, dbin_g = _merge_bwd("merge_bwd", dmerged, proj, ya, yb, D)
    g_wco = _mm_wgrad_cols("wgrad_conv_out", a_act, dya[None], w_conv_out.shape[2])
    g_wso = _mm_wgrad_cols("wgrad_sgu_out", uv, dyb[None], w_sgu_out.shape[2])
    da_act = _mm_dgrad_cols("dgrad_conv_out", dya[None], wco_f)
    duv = _mm_dgrad_cols("dgrad_sgu_out", dyb[None], wso_f)
    dproj, d_ws, d_bs, d_sgu_g, d_sgu_b, dbin_s = _sgu_bwd("sgu_bwd", proj, duv, sgu_ln_g, sgu_ln_b, w_spatial[0], bst,
                                                           dproj, ds)
    dconv, d_cln_g, d_cln_b, d_conv_b = _conv_bwd_ln("conv_ln_bwd", conv, da_act, conv_ln_g, conv_ln_b)
    dproj, d_cw, dbin_a = _conv_bwd("conv_bwd", dconv, proj, cw_full, dproj, dc)
    g_win = _mm_wgrad_cols("wgrad_in", h1, dproj[None], w_in.shape[2])
    dh1 = _mm_dgrad_cols("dgrad_in", dproj[None], win_f)
    dxs, d_norm1, d_scale1, d_shift1 = _modnorm_bwd("modnorm1_bwd", xs, dh1, dx1, norm1_g, scale1, shift1)

    d_mod = jnp.concatenate([d_shift1, d_scale1, d_gate1, d_shift2, d_scale2, d_gate2], axis=1)
    d_b_in = jnp.concatenate([dbin_a, dbin_s, dbin_g], axis=1)
    d_fw = jnp.concatenate([d_ffn[0, :KF], d_ffn[1, :KF]], axis=1)
    d_fb = jnp.concatenate([d_ffn[0, KF:KF + 1], d_ffn[1, KF:KF + 1]], axis=1)
    small = [d_mod, d_norm1, d_b_in, d_conv_b, d_cln_g, d_cln_b, d_sgu_g, d_sgu_b, d_ws, d_bs[:, :, 0], d_norm2,
             d_fb, d_final_g, d_cw[:KC], d_fw]
    small_shapes = [a.shape for a in small]
    g3 = _allgather8(_pack(small), "gather_small_grads")
    summed = _unpack(_sum8(g3, "sum_small_grads"), small_shapes)
    (g_b_ada, g_norm1, g_b_in, g_conv_b, g_cln_g, g_cln_b, g_sgu_g, g_sgu_b, g_ws, g_bs, g_norm2, g_fb, g_final,
     g_cw_full, g_fw_full) = summed
    n_cw = conv_dw_w.shape[2]
    n_fw = ffn_dw_w.shape[2]
    g_cw = lax.dynamic_slice(g_cw_full, (0, chip * n_cw), (KC, n_cw))
    g_fw = lax.dynamic_slice(g_fw_full, (0, chip * n_fw), (KF, n_fw))
    dmod_all = jnp.stack([_unpack(g3[d], [(6 * D,)])[0] for d in range(N_DEV)])
    dmod_cols = lax.dynamic_slice(dmod_all, (0, chip * n_ada), (N_DEV, n_ada))
    g_wada = _ada_wgrad("ada_wgrad", jnp.transpose(c_all), dmod_cols)

    big = [g_win, g_wco, g_wso, g_wout, g_wup, g_wdown]
    pc_idx = jnp.concatenate([p_idx, c_idx])
    recv1 = _swap_halves(big, "reduce_cores")
    sums1 = [_add_own_half(g, r, c_idx, "add_cores_" + nm) for g, r, nm in zip(big, recv1, names)]
    recv2 = _scatter_blocks(sums1, "reduce_chips")
    own_half = [_add_chips(g, r1, r2, pc_idx, "add_chips_" + nm) for g, r1, r2, nm in zip(big, recv1, recv2, names)]
    sib_half = _send_to_sibling(own_half, "exchange_halves")
    big_halves = dict(zip(["w_in", "w_conv_out", "w_sgu_out", "w_out", "w_up", "w_down"], zip(own_half, sib_half)))

    grads = {
        "w_ada": g_wada[None], "b_ada": g_b_ada, "norm1_g": g_norm1, "b_in": g_b_in,
        "conv_dw_w": g_cw[None], "conv_dw_b": g_conv_b, "conv_ln_g": g_cln_g, "conv_ln_b": g_cln_b,
        "sgu_ln_g": g_sgu_g, "sgu_ln_b": g_sgu_b, "w_spatial": g_ws[None],
        "b_spatial": g_bs[None], "norm2_g": g_norm2, "ffn_dw_w": g_fw[None], "ffn_dw_b": g_fb,
        "final_g": g_final.reshape(D),
    }
    weights = dict(w_ada=w_ada, b_ada=b_ada, norm1_g=norm1_g, w_in=w_in, b_in=b_in, conv_dw_w=conv_dw_w, conv_dw_b=conv_dw_b, conv_ln_g=conv_ln_g, conv_ln_b=conv_ln_b, w_conv_out=w_conv_out, sgu_ln_g=sgu_ln_g, sgu_ln_b=sgu_ln_b, w_spatial=w_spatial, b_spatial=b_spatial, w_sgu_out=w_sgu_out, w_out=w_out, norm2_g=norm2_g, w_up=w_up, ffn_dw_w=ffn_dw_w, ffn_dw_b=ffn_dw_b, w_down=w_down, final_g=final_g)
    m_in = dict(w_ada=m_w_ada, b_ada=m_b_ada, norm1_g=m_norm1_g, w_in=m_w_in, b_in=m_b_in, conv_dw_w=m_conv_dw_w, conv_dw_b=m_conv_dw_b, conv_ln_g=m_conv_ln_g, conv_ln_b=m_conv_ln_b, w_conv_out=m_w_conv_out, sgu_ln_g=m_sgu_ln_g, sgu_ln_b=m_sgu_ln_b, w_spatial=m_w_spatial, b_spatial=m_b_spatial, w_sgu_out=m_w_sgu_out, w_out=m_w_out, norm2_g=m_norm2_g, w_up=m_w_up, ffn_dw_w=m_ffn_dw_w, ffn_dw_b=m_ffn_dw_b, w_down=m_w_down, final_g=m_final_g)
    v_in = dict(w_ada=v_w_ada, b_ada=v_b_ada, norm1_g=v_norm1_g, w_in=v_w_in, b_in=v_b_in, conv_dw_w=v_conv_dw_w, conv_dw_b=v_conv_dw_b, conv_ln_g=v_conv_ln_g, conv_ln_b=v_conv_ln_b, w_conv_out=v_w_conv_out, sgu_ln_g=v_sgu_ln_g, sgu_ln_b=v_sgu_ln_b, w_spatial=v_w_spatial, b_spatial=v_b_spatial, w_sgu_out=v_w_sgu_out, w_out=v_w_out, norm2_g=v_norm2_g, w_up=v_w_up, ffn_dw_w=v_ffn_dw_w, ffn_dw_b=v_ffn_dw_b, w_down=v_w_down, final_g=v_final_g)
    order = list(weights.keys())
    large = ["w_ada", "w_in", "w_conv_out", "w_sgu_out", "w_out", "w_up", "w_down"]
    little = [n for n in order if n not in large]
    delta, new_m, new_v = {}, {}, {}
    for n in large:
        shp = weights[n].shape
        two = (shp[1], shp[2])
        if n in big_halves:
            g_, d_, m_, v_ = _adamw_halves(weights[n].reshape(two), big_halves[n][0], big_halves[n][1],
                                           m_in[n].reshape(two), v_in[n].reshape(two), c_idx, "adamw_" + n)
            grads[n] = g_.reshape(shp)
        else:
            d_, m_, v_ = _adamw(weights[n].reshape(two), grads[n].reshape(two), m_in[n].reshape(two),
                                v_in[n].reshape(two), "adamw_" + n)
        delta[n], new_m[n], new_v[n] = d_.reshape(shp), m_.reshape(shp), v_.reshape(shp)
    shapes = [weights[n].shape for n in little]
    d_, m_, v_ = _adamw(_pack([weights[n] for n in little]), _pack([grads[n] for n in little]),
                        _pack([m_in[n] for n in little]), _pack([v_in[n] for n in little]), "adamw_small")
    for n, dd, mm, vv in zip(little, _unpack(d_, shapes), _unpack(m_, shapes), _unpack(v_, shapes)):
        delta[n], new_m[n], new_v[n] = dd, mm, vv
    grad_out = [grads[n].reshape(weights[n].shape) for n in order]
    return (loss, dxs[None], *grad_out, *[delta[n] for n in order], *[new_m[n] for n in order],
            *[new_v[n] for n in order])
```

```python
import functools

import jax
import jax.numpy as jnp
from jax import lax
from jax.experimental import pallas as pl
from jax.experimental.pallas import tpu as pltpu

F32 = jnp.float32
BF16 = jnp.bfloat16
EPS = 1e-6
MESH = pl.DeviceIdType.MESH
N_CHIPS = 4
N_DEV = 8
LANES = 128
SUBLANES = 8
CONV_HALO = 32
FFN_HALO = 8
VMEM_LIMIT_BYTES = 56 * 1024 * 1024

ADAM_LR = 0.001
ADAM_B1 = 0.9
ADAM_B2 = 0.999
ADAM_EPS = 1e-08
ADAM_WD = 0.01
ADAM_STEP = 10

NN = (((1,), (0,)), ((), ()))
NT = (((1,), (1,)), ((), ()))
TN = (((0,), (0,)), ((), ()))


def _params(sem=None):
    return pltpu.CompilerParams(dimension_semantics=sem, vmem_limit_bytes=VMEM_LIMIT_BYTES)


def _pick(dim, pref, mult):
    best = None
    d = mult
    while d <= min(dim, pref):
        if dim % d == 0:
            best = d
        d += mult
    return dim if best is None else best


def _axes():
    return lax.axis_index("x"), lax.axis_index("y"), lax.axis_index("c")


def _modnorm(x, g, scale, shift):
    r = lax.rsqrt(jnp.mean(x * x, axis=-1, keepdims=True) + EPS)
    return (x * r * g) * (1.0 + scale) + shift


def _layer_norm(x, g, b):
    mu = jnp.mean(x, axis=-1, keepdims=True)
    var = jnp.mean(jnp.square(x - mu), axis=-1, keepdims=True)
    return (x - mu) * lax.rsqrt(var + EPS) * g + b


def _gelu(x):
    return 0.5 * x * (1.0 + lax.erf(x * (0.5 ** 0.5)))


def _ln_silu(x, g, b):
    return jax.nn.silu(_layer_norm(x, g, b))


def _tril_mask(ws):
    n = ws.shape[-1]
    row = lax.broadcasted_iota(jnp.int32, (n, n), 0)
    col = lax.broadcasted_iota(jnp.int32, (n, n), 1)
    return jnp.where(row >= col, ws, 0.0)


def _pack(arrs):
    flat = [a.reshape(-1).astype(F32) for a in arrs]
    total = sum(f.shape[0] for f in flat)
    tile = SUBLANES * LANES
    padded = -(-total // tile) * tile
    if padded > total:
        flat = flat + [jnp.zeros((padded - total,), F32)]
    return jnp.concatenate(flat).reshape(padded // LANES, LANES)


def _unpack(buf, shapes):
    flat = buf.reshape(-1)
    out, off = [], 0
    for s in shapes:
        n = 1
        for d in s:
            n *= d
        out.append(flat[off:off + n].reshape(s))
        off += n
    return out


def _allgather8(buf, name):
    R, L = buf.shape

    def body(in_ref, out_ref, send_sems, recv_sems, local_sem):
        x, y, c = _axes()
        me = 4 * x + 2 * y + c
        mine = pltpu.make_async_copy(in_ref, out_ref.at[me], local_sem)
        mine.start()
        peers = []
        for k in range(1, N_DEV):
            px = 1 - x if k & 4 else x
            py = 1 - y if k & 2 else y
            pc = 1 - c if k & 1 else c
            peers.append((px, py, pc))
        sends = []
        for k, peer in enumerate(peers):
            cp = pltpu.make_async_remote_copy(
                src_ref=in_ref, dst_ref=out_ref.at[me], send_sem=send_sems.at[k], recv_sem=recv_sems.at[k],
                device_id=peer, device_id_type=MESH)
            cp.start()
            sends.append(cp)
        for k, (px, py, pc) in enumerate(peers):
            pltpu.make_async_remote_copy(
                src_ref=in_ref, dst_ref=out_ref.at[4 * px + 2 * py + pc], send_sem=send_sems.at[k],
                recv_sem=recv_sems.at[k], device_id=(px, py, pc), device_id_type=MESH).wait_recv()
        for cp in sends:
            cp.wait_send()
        mine.wait()

    return pl.pallas_call(
        body, name=name,
        out_shape=jax.ShapeDtypeStruct((N_DEV, R, L), buf.dtype),
        in_specs=[pl.BlockSpec(memory_space=pltpu.VMEM)],
        out_specs=pl.BlockSpec(memory_space=pltpu.VMEM),
        scratch_shapes=[pltpu.SemaphoreType.DMA((N_DEV - 1,)), pltpu.SemaphoreType.DMA((N_DEV - 1,)),
                        pltpu.SemaphoreType.DMA],
        compiler_params=pltpu.CompilerParams(vmem_limit_bytes=VMEM_LIMIT_BYTES),
    )(buf)


def _other_chips(x, y):
    return [(1 - x, y), (x, 1 - y), (1 - x, 1 - y)]


def _cast_into_block(shard, p_idx, name):
    K, n = shard.shape
    tr = _pick(K, max(SUBLANES, (1 << 19) // n), 2 * SUBLANES)

    def body(p_ref, s_ref, o_ref):
        o_ref[...] = s_ref[...].astype(BF16)

    return pl.pallas_call(
        body, name=name,
        grid_spec=pltpu.PrefetchScalarGridSpec(
            num_scalar_prefetch=1, grid=(K // tr,),
            in_specs=[pl.BlockSpec((tr, n), lambda i, p: (i, 0))],
            out_specs=pl.BlockSpec((None, tr, n), lambda i, p: (p[0], i, 0))),
        out_shape=jax.ShapeDtypeStruct((N_CHIPS, K, n), BF16),
        compiler_params=_params(("parallel",)),
    )(p_idx, shard)


class _Job:
    def __init__(self, ins, outs, aliases, n_sems, make):
        self.ins, self.outs, self.aliases, self.n_sems, self.make = list(ins), list(outs), list(aliases), n_sems, make


def _merge_jobs(jobs):
    ins, outs, aliases, offs = [], [], [], []
    n_sems = 0
    for jb in jobs:
        offs.append((len(ins), len(outs), n_sems))
        aliases += [(len(ins) + a, len(outs) + b) for a, b in jb.aliases]
        ins += jb.ins
        outs += jb.outs
        n_sems += jb.n_sems

    def make(in_refs, out_refs, send_sems, recv_sems, base=0):
        made = []
        for jb, (oi, oo, os_) in zip(jobs, offs):
            made.append(jb.make(in_refs[oi:oi + len(jb.ins)], out_refs[oo:oo + len(jb.outs)],
                                send_sems, recv_sems, base + os_))

        def start():
            for st, _ in made:
                st()

        def finish():
            for _, fin in made:
                fin()

        return start, finish

    return _Job(ins, outs, aliases, n_sems, make)


def _job_gather(fulls):
    nw = len(fulls)

    def make(in_refs, outs, send_sems, recv_sems, base=0):
        x, y, c = _axes()
        p = 2 * x + y
        chips = _other_chips(x, y)

        def rows(w, mine):
            kh = outs[w].shape[1] // 2
            return pl.ds((c if mine else 1 - c) * kh, kh)

        def over_ici(w, j, block):
            qx, qy = chips[j]
            return pltpu.make_async_remote_copy(
                src_ref=outs[w].at[block, rows(w, True)], dst_ref=outs[w].at[block, rows(w, True)],
                send_sem=send_sems.at[base + 6 * w + j], recv_sem=recv_sems.at[base + 6 * w + j],
                device_id=(qx, qy, c), device_id_type=MESH)

        def over_d2d(w, j, mine):
            qx, qy = chips[j]
            return pltpu.make_async_remote_copy(
                src_ref=outs[w].at[2 * qx + qy, rows(w, mine)], dst_ref=outs[w].at[2 * qx + qy, rows(w, mine)],
                send_sem=send_sems.at[base + 6 * w + 3 + j], recv_sem=recv_sems.at[base + 6 * w + 3 + j],
                device_id=(x, y, 1 - c), device_id_type=MESH)

        def start():
            for w in range(nw):
                for j in range(3):
                    over_ici(w, j, p).start()

        def finish():
            for w in range(nw):
                for j, (qx, qy) in enumerate(chips):
                    over_ici(w, j, 2 * qx + qy).wait_recv()
                    over_d2d(w, j, True).start()
            for w in range(nw):
                for j in range(3):
                    over_d2d(w, j, False).wait_recv()
            for w in range(nw):
                for j in range(3):
                    over_ici(w, j, p).wait_send()
                    over_d2d(w, j, True).wait_send()

        return start, finish

    return _Job(fulls, [jax.ShapeDtypeStruct(f.shape, f.dtype) for f in fulls], [(w, w) for w in range(nw)], 6 * nw, make)


def _job_simple(ins, outs, n_per, copies_of):
    nw = len(ins)

    def make(in_refs, out_refs, send_sems, recv_sems, base=0):
        x, y, c = _axes()
        copies = []
        for w in range(nw):
            for j, (src, dst, dev) in enumerate(copies_of(w, in_refs[w], out_refs[w], x, y, c)):
                k = base + n_per * w + j
                copies.append(pltpu.make_async_remote_copy(
                    src_ref=src, dst_ref=dst, send_sem=send_sems.at[k], recv_sem=recv_sems.at[k],
                    device_id=dev, device_id_type=MESH))

        def start():
            for cp in copies:
                cp.start()

        def finish():
            for cp in copies:
                cp.wait()

        return start, finish

    return _Job(ins, outs, [], n_per * nw, make)


def _job_swap_halves(grads):
    def copies_of(w, src, dst, x, y, c):
        kh = src.shape[1] // 2
        return [(src.at[:, pl.ds((1 - c) * kh, kh), :], dst, (x, y, 1 - c))]

    outs = [jax.ShapeDtypeStruct((g.shape[0], g.shape[1] // 2, g.shape[2]), g.dtype) for g in grads]
    return _job_simple(grads, outs, 1, copies_of)


def _job_scatter_blocks(sums):
    def copies_of(w, src, dst, x, y, c):
        return [(src.at[2 * qx + qy], dst.at[j], (qx, qy, c)) for j, (qx, qy) in enumerate(_other_chips(x, y))]

    outs = [jax.ShapeDtypeStruct((3,) + s.shape[1:], s.dtype) for s in sums]
    return _job_simple(sums, outs, 3, copies_of)


def _job_to_sibling(arrs):
    def copies_of(w, src, dst, x, y, c):
        return [(src, dst, (x, y, 1 - c))]

    outs = [jax.ShapeDtypeStruct(a.shape, a.dtype) for a in arrs]
    return _job_simple(arrs, outs, 1, copies_of)


def _run_job(job, name):
    ni, no = len(job.ins), len(job.outs)

    def body(*refs):
        start, finish = job.make(refs[:ni], refs[ni:ni + no], refs[-2], refs[-1])
        start()
        finish()

    any_spec = pl.BlockSpec(memory_space=pl.ANY)
    return pl.pallas_call(
        body, name=name, out_shape=job.outs, in_specs=[any_spec] * ni, out_specs=[any_spec] * no,
        input_output_aliases=dict(job.aliases),
        scratch_shapes=[pltpu.SemaphoreType.DMA((job.n_sems,)), pltpu.SemaphoreType.DMA((job.n_sems,))],
    )(*job.ins)


def _call(body, name, grid, in_specs, out_specs, out_shape, scratch_shapes, semantics, args, aliases=None, job=None):
    in_specs, out_specs, out_shape = list(in_specs), list(out_specs), list(out_shape)
    scratch_shapes = list(scratch_shapes)
    if job is None:
        return pl.pallas_call(
            body, name=name, grid=grid, in_specs=in_specs, out_specs=out_specs, out_shape=out_shape,
            scratch_shapes=scratch_shapes, input_output_aliases=dict(aliases or {}),
            compiler_params=_params(semantics))(*args)
    n_in, n_out, n_scr = len(args), len(out_shape), len(scratch_shapes)
    ni, no = len(job.ins), len(job.outs)

    def wrapped(*refs):
        ins, cins = refs[:n_in], refs[n_in:n_in + ni]
        outs, couts = refs[n_in + ni:n_in + ni + n_out], refs[n_in + ni + n_out:n_in + ni + n_out + no]
        scr = refs[n_in + ni + n_out + no:n_in + ni + n_out + no + n_scr]
        start, finish = job.make(cins, couts, refs[-2], refs[-1])
        first = functools.reduce(jnp.logical_and, [pl.program_id(a) == 0 for a in range(len(grid))])
        last = functools.reduce(jnp.logical_and, [pl.program_id(a) == grid[a] - 1 for a in range(len(grid))])
        pl.when(first)(start)
        body(*ins, *outs, *scr)
        pl.when(last)(finish)

    any_spec = pl.BlockSpec(memory_space=pl.ANY)
    all_aliases = dict(aliases or {})
    for a, b in job.aliases:
        all_aliases[n_in + a] = n_out + b
    return pl.pallas_call(
        wrapped, name=name, grid=grid, in_specs=in_specs + [any_spec] * ni, out_specs=out_specs + [any_spec] * no,
        out_shape=out_shape + job.outs,
        scratch_shapes=scratch_shapes + [pltpu.SemaphoreType.DMA((job.n_sems,)), pltpu.SemaphoreType.DMA((job.n_sems,))],
        input_output_aliases=all_aliases,
        compiler_params=_params(tuple("arbitrary" for _ in grid)))(*args, *job.ins)


def _add_own_half(g, r, c_idx, name):
    nb, K, n = g.shape
    kh = K // 2
    tr = _pick(kh, max(SUBLANES, (1 << 19) // n), 2 * SUBLANES)
    per = kh // tr

    def body(c_ref, g_ref, r_ref, o_ref):
        o_ref[...] = (g_ref[...] + r_ref[...]).astype(BF16)

    return pl.pallas_call(
        body, name=name,
        grid_spec=pltpu.PrefetchScalarGridSpec(
            num_scalar_prefetch=1, grid=(nb, per),
            in_specs=[pl.BlockSpec((None, tr, n), lambda b, i, c: (b, c[0] * per + i, 0)),
                      pl.BlockSpec((None, tr, n), lambda b, i, c: (b, i, 0))],
            out_specs=pl.BlockSpec((None, tr, n), lambda b, i, c: (b, i, 0))),
        out_shape=jax.ShapeDtypeStruct((nb, kh, n), BF16),
        compiler_params=_params(("parallel", "parallel")),
    )(c_idx, g, r)


def _add_chips(g, r1, r2, pc_idx, name):
    _, K, n = g.shape
    kh = K // 2
    tr = _pick(kh, max(SUBLANES, (1 << 19) // n), 2 * SUBLANES)
    per = kh // tr

    def body(pc_ref, g_ref, r1_ref, r2_ref, o_ref):
        own = g_ref[...] + r1_ref[...]
        o_ref[...] = ((own + r2_ref[0].astype(F32)) + r2_ref[1].astype(F32)) + r2_ref[2].astype(F32)

    return pl.pallas_call(
        body, name=name,
        grid_spec=pltpu.PrefetchScalarGridSpec(
            num_scalar_prefetch=1, grid=(per,),
            in_specs=[pl.BlockSpec((None, tr, n), lambda i, pc: (pc[0], pc[1] * per + i, 0)),
                      pl.BlockSpec((None, tr, n), lambda i, pc: (pc[0], i, 0)),
                      pl.BlockSpec((3, tr, n), lambda i, pc: (0, i, 0))],
            out_specs=pl.BlockSpec((tr, n), lambda i, pc: (i, 0))),
        out_shape=jax.ShapeDtypeStruct((kh, n), F32),
        compiler_params=_params(("parallel",)),
    )(pc_idx, g, r1, r2)


def _sum8(g, name):
    _, R, L = g.shape

    def body(g_ref, o_ref):
        acc = g_ref[0]
        for d in range(1, N_DEV):
            acc = acc + g_ref[d]
        o_ref[...] = acc

    return pl.pallas_call(
        body, name=name, out_shape=jax.ShapeDtypeStruct((R, L), F32),
        in_specs=[pl.BlockSpec(memory_space=pltpu.VMEM)], out_specs=pl.BlockSpec(memory_space=pltpu.VMEM),
        compiler_params=_params(),
    )(g)


def _adamw_math(w, gg, m, v):
    nm = ADAM_B1 * m + (1.0 - ADAM_B1) * gg
    nv = ADAM_B2 * v + (1.0 - ADAM_B2) * jnp.square(gg)
    m_hat = nm / (1.0 - ADAM_B1 ** ADAM_STEP)
    v_hat = nv / (1.0 - ADAM_B2 ** ADAM_STEP)
    return -ADAM_LR * (m_hat / (jnp.sqrt(v_hat) + ADAM_EPS) + ADAM_WD * w), nm, nv


def _adamw(w, g, m, v, name):
    R, C = w.shape
    tr = _pick(R, max(SUBLANES, (1 << 18) // C), SUBLANES)

    def body(w_ref, g_ref, m_ref, v_ref, d_ref, nm_ref, nv_ref):
        d_ref[...], nm_ref[...], nv_ref[...] = _adamw_math(w_ref[...], g_ref[...], m_ref[...], v_ref[...])

    spec = pl.BlockSpec((tr, C), lambda i: (i, 0))
    sd = jax.ShapeDtypeStruct((R, C), F32)
    return pl.pallas_call(
        body, name=name, grid=(R // tr,), in_specs=[spec] * 4, out_specs=[spec] * 3, out_shape=[sd] * 3,
        compiler_params=_params(("parallel",)),
    )(w, g, m, v)


def _adamw_halves(w, g_own, g_sib, m, v, c_idx, name):
    K, n = w.shape
    kh = K // 2
    tr = _pick(kh, max(SUBLANES, (1 << 18) // n), SUBLANES)
    per = kh // tr

    def body(c_ref, w_ref, go_ref, gs_ref, m_ref, v_ref, g_ref, d_ref, nm_ref, nv_ref):
        h = pl.program_id(0)

        def step(gg):
            g_ref[...] = gg
            d_ref[...], nm_ref[...], nv_ref[...] = _adamw_math(w_ref[...], gg, m_ref[...], v_ref[...])

        @pl.when(h == 0)
        def _():
            step(go_ref[...])

        @pl.when(h == 1)
        def _():
            step(gs_ref[...])

    full = pl.BlockSpec((tr, n), lambda h, i, c: (((c[0] + h) % 2) * per + i, 0))
    own = pl.BlockSpec((tr, n), lambda h, i, c: (i * (1 - h), 0))
    sib = pl.BlockSpec((tr, n), lambda h, i, c: (i * h, 0))
    sd = jax.ShapeDtypeStruct((K, n), F32)
    return pl.pallas_call(
        body, name=name,
        grid_spec=pltpu.PrefetchScalarGridSpec(
            num_scalar_prefetch=1, grid=(2, per),
            in_specs=[full, own, sib, full, full], out_specs=[full] * 4),
        out_shape=[sd] * 4,
        compiler_params=_params(("arbitrary", "arbitrary")),
    )(c_idx, w, g_own, g_sib, m, v)


def _matmul(name, grid, dims, a, a_spec, b, b_spec, outs, out_specs, acc_shape,
            extras=(), extra_specs=(), epilogue=None, job=None):
    nk = grid[2]
    nex = len(extras)
    nout = len(outs)

    def body(a_ref, b_ref, *rest):
        ex, o = rest[:nex], rest[nex:nex + nout]
        part = lax.dot_general(a_ref[...].astype(BF16), b_ref[...].astype(BF16), dims,
                               preferred_element_type=F32)

        def finish(res):
            if epilogue is None:
                o[0][...] = res.astype(o[0].dtype)
            else:
                epilogue(res, ex, o)

        if nk == 1:
            finish(part)
        else:
            acc = rest[-1]
            k = pl.program_id(2)

            @pl.when(k == 0)
            def _():
                acc[...] = part

            @pl.when(k > 0)
            def _():
                acc[...] += part

            @pl.when(k == nk - 1)
            def _():
                finish(acc[...])

    res = _call(body, name, grid, [a_spec, b_spec] + list(extra_specs), out_specs, outs,
                [] if nk == 1 else [pltpu.VMEM(acc_shape, F32)], ("parallel", "parallel", "arbitrary"),
                (a, b, *extras), job=job)
    return res if job is None else (res[:nout], res[nout:])


def _first(res, job):
    return res[0] if job is None else (res[0][0], res[1])


def _mm_fwd(name, a, wfull, out_dtype=F32, planes=1, bias=None, job=None):
    S, K = a.shape
    _, _, n = wfull.shape
    N = N_CHIPS * n
    tm = _pick(S, 1024, LANES)
    tn = _pick(n, 1408, LANES)
    per = n // tn
    nj = N // tn
    pj = nj // planes
    grid = (S // tm, nj, 1)
    a_spec = pl.BlockSpec((tm, K), lambda i, j, k: (i, 0))
    b_spec = pl.BlockSpec((None, K, tn), lambda i, j, k: (j // per, 0, j % per))
    o_spec = pl.BlockSpec((None, tm, tn), lambda i, j, k: (j // pj, i, j % pj))
    sd = jax.ShapeDtypeStruct((planes, S, N // planes), out_dtype)
    if bias is not None:
        def epi(res, ex, o):
            o[0][...] = (res + ex[0][...]).astype(o[0].dtype)

        out = _matmul(name, grid, NN, a, a_spec, wfull, b_spec, [sd], [o_spec], (tm, tn), extras=(bias,),
                      extra_specs=(pl.BlockSpec((1, tn), lambda i, j, k: (0, j)),), epilogue=epi, job=job)
    else:
        out = _matmul(name, grid, NN, a, a_spec, wfull, b_spec, [sd], [o_spec], (tm, tn), job=job)
    return _first(out, job)


def _mm_fwd_rows(name, a, wrows, resid, gate, job=None):
    S, K = a.shape
    _, N = wrows.shape
    tm = _pick(S, 512, LANES)
    tn = _pick(N, 1024, LANES)
    tk = _pick(K, 2816, LANES)
    grid = (S // tm, N // tn, K // tk)
    a_spec = pl.BlockSpec((tm, tk), lambda i, j, k: (i, k))
    b_spec = pl.BlockSpec((tk, tn), lambda i, j, k: (k, j))
    o_spec = pl.BlockSpec((tm, tn), lambda i, j, k: (i, j))
    g_spec = pl.BlockSpec((1, tn), lambda i, j, k: (0, j))

    def epi(res, ex, o):
        o[0][...] = res
        o[1][...] = ex[0][...] + ex[1][...] * res

    sd = jax.ShapeDtypeStruct((S, N), F32)
    return _matmul(name, grid, NN, a, a_spec, wrows, b_spec, [sd, sd], [o_spec, o_spec], (tm, tn),
                   extras=(resid, gate), extra_specs=(o_spec, g_spec), epilogue=epi, job=job)


def _mm_dgrad_cols(name, dy, wfull, out_dtype=F32, job=None):
    planes, S, npl = dy.shape
    _, K, n = wfull.shape
    tm = _pick(S, 1024, LANES)
    to = _pick(K, 1024, LANES)
    tk = _pick(n, 2816, LANES)
    per = n // tk
    nk = N_CHIPS * per
    pk = nk // planes
    grid = (S // tm, K // to, nk)
    a_spec = pl.BlockSpec((None, tm, tk), lambda i, j, k: (k // pk, i, k % pk))
    b_spec = pl.BlockSpec((None, to, tk), lambda i, j, k: (k // per, j, k % per))
    o_spec = pl.BlockSpec((tm, to), lambda i, j, k: (i, j))
    return _first(_matmul(name, grid, NT, dy, a_spec, wfull, b_spec, [jax.ShapeDtypeStruct((S, K), out_dtype)],
                          [o_spec], (tm, to), job=job), job)


def _mm_dgrad_rows(name, dy, wrows, out_dtype=F32, job=None):
    S, N = dy.shape
    K, _ = wrows.shape
    tm = _pick(S, 1024, LANES)
    to = _pick(K, 1408, LANES)
    tk = _pick(N, 2048, LANES)
    grid = (S // tm, K // to, N // tk)
    a_spec = pl.BlockSpec((tm, tk), lambda i, j, k: (i, k))
    b_spec = pl.BlockSpec((to, tk), lambda i, j, k: (j, k))
    o_spec = pl.BlockSpec((tm, to), lambda i, j, k: (i, j))
    return _first(_matmul(name, grid, NT, dy, a_spec, wrows, b_spec, [jax.ShapeDtypeStruct((S, K), out_dtype)],
                          [o_spec], (tm, to), job=job), job)


def _mm_wgrad_cols(name, a, dy, n, job=None):
    S, K = a.shape
    planes, _, npl = dy.shape
    N = planes * npl
    to = _pick(K, 1024, LANES)
    tn = _pick(n, 1408, LANES)
    ts = _pick(S, 1024, LANES)
    per = n // tn
    nj = N // tn
    pj = nj // planes
    grid = (K // to, nj, S // ts)
    a_spec = pl.BlockSpec((ts, to), lambda i, j, k: (k, i))
    b_spec = pl.BlockSpec((None, ts, tn), lambda i, j, k: (j // pj, k, j % pj))
    o_spec = pl.BlockSpec((None, to, tn), lambda i, j, k: (j // per, i, j % per))
    return _first(_matmul(name, grid, TN, a, a_spec, dy, b_spec, [jax.ShapeDtypeStruct((N_CHIPS, K, n), F32)],
                          [o_spec], (to, tn), job=job), job)


def _mm_wgrad_rows(name, a, dy, kshard, job=None):
    S, K = a.shape
    _, N = dy.shape
    to = _pick(kshard, 1408, LANES)
    tn = _pick(N, 1024, LANES)
    ts = _pick(S, 1024, LANES)
    per = kshard // to
    grid = (K // to, N // tn, S // ts)
    a_spec = pl.BlockSpec((ts, to), lambda i, j, k: (k, i))
    b_spec = pl.BlockSpec((ts, tn), lambda i, j, k: (k, j))
    o_spec = pl.BlockSpec((None, to, tn), lambda i, j, k: (i // per, i % per, j))
    return _first(_matmul(name, grid, TN, a, a_spec, dy, b_spec, [jax.ShapeDtypeStruct((N_CHIPS, kshard, N), F32)],
                          [o_spec], (to, tn), job=job), job)


def _rows(tm, width, colblk=0):
    return pl.BlockSpec((tm, width), lambda i: (i, colblk))


def _whole(shape):
    nd = len(shape)
    return pl.BlockSpec(shape, lambda i: (0,) * nd)


def _prev_halo(tm, h, width, colblk=0):
    r = tm // h
    return pl.BlockSpec((h, width), lambda i: (jnp.maximum(i * r - 1, 0), colblk))


def _next_halo(tm, h, width, nblk, colblk=0):
    r = tm // h
    return pl.BlockSpec((h, width), lambda i: (jnp.minimum((i + 1) * r, nblk - 1), colblk))


def _accumulate(i, ref, val):
    @pl.when(i == 0)
    def _():
        ref[...] = val

    @pl.when(i > 0)
    def _():
        ref[...] += val


def _fill_rotations(rot, offs):
    n = rot.shape[1]
    for r in sorted({o % SUBLANES for o in offs} - {0}):
        rot[r, 0:n - SUBLANES, :] = rot[0, r:r + n - SUBLANES, :]


def _for_chunks(n_rows, rb, fn):
    def step(j, carry):
        fn(pl.multiple_of(j * rb, rb))
        return carry

    lax.fori_loop(0, n_rows // rb, step, 0)


def _modnorm_fwd(name, x, g, scale, shift):
    S, D = x.shape
    tm = _pick(S, 512, LANES)

    def body(x_ref, g_ref, sc_ref, sh_ref, h_ref):
        h_ref[...] = _modnorm(x_ref[...], g_ref[...], sc_ref[...], sh_ref[...]).astype(BF16)

    vec = _whole((1, D))
    return pl.pallas_call(
        body, name=name, grid=(S // tm,), in_specs=[_rows(tm, D), vec, vec, vec], out_specs=_rows(tm, D),
        out_shape=jax.ShapeDtypeStruct((S, D), BF16), compiler_params=_params(("parallel",)),
    )(x, g, scale, shift)


def _modnorm_bwd(name, x, dh, dx_in, g, scale, shift, job=None):
    S, D = x.shape
    tm = _pick(S, 256, LANES)

    def body(x_ref, dh_ref, dxin_ref, g_ref, sc_ref, sh_ref, dx_ref, dg_ref, dsc_ref, dsh_ref):
        i = pl.program_id(0)
        _, pull = jax.vjp(_modnorm, x_ref[...], g_ref[...], sc_ref[...], sh_ref[...])
        dx, dg, dsc, dsh = pull(dh_ref[...])
        dx_ref[...] = dxin_ref[...] + dx
        _accumulate(i, dg_ref, dg)
        _accumulate(i, dsc_ref, dsc)
        _accumulate(i, dsh_ref, dsh)

    vec = _whole((1, D))
    row = _rows(tm, D)
    vsd = jax.ShapeDtypeStruct((1, D), F32)
    res = _call(body, name, (S // tm,), [row, row, row, vec, vec, vec], [row, vec, vec, vec],
                [jax.ShapeDtypeStruct((S, D), F32), vsd, vsd, vsd], [], ("arbitrary",),
                (x, dh, dx_in, g, scale, shift), job=job)
    return res if job is None else (res[:4], res[4:])


def _conv_fwd(name, proj, w, b, lg, lb, dc):
    S = proj.shape[0]
    K = w.shape[0]
    H = CONV_HALO
    tm = _pick(S, 256, LANES)

    def glu(v):
        return v[:, :dc] * jax.nn.sigmoid(v[:, dc:])

    offs = [H - (K - 1) + k for k in range(K)]
    rb = _pick(tm, 64, SUBLANES)
    lw = min(LANES, dc)

    def body(cur_ref, prev_ref, w_ref, b_ref, lg_ref, lb_ref, conv_ref, act_ref, rot):
        i = pl.program_id(0)
        rot[0, 0:H, :] = jnp.where(i > 0, glu(prev_ref[...]), 0.0)
        rot[0, H:, :] = glu(cur_ref[...])
        _fill_rotations(rot, offs)

        def chunk(row0):
            for l0 in range(0, dc, lw):
                acc = jnp.broadcast_to(b_ref[:, l0:l0 + lw], (rb, lw))
                for k in range(K):
                    a, r = divmod(offs[k], SUBLANES)
                    acc = acc + w_ref[k:k + 1, l0:l0 + lw] * rot[r, pl.ds(row0 + SUBLANES * a, rb), l0:l0 + lw]
                conv_ref[pl.ds(row0, rb), l0:l0 + lw] = acc

        _for_chunks(tm, rb, chunk)
        act_ref[...] = _ln_silu(conv_ref[...], lg_ref[...], lb_ref[...]).astype(BF16)

    vec = _whole((1, dc))
    return pl.pallas_call(
        body, name=name, grid=(S // tm,),
        in_specs=[_rows(tm, 2 * dc), _prev_halo(tm, H, 2 * dc), _whole(w.shape), vec, vec, vec],
        out_specs=[_rows(tm, dc), _rows(tm, dc)],
        out_shape=[jax.ShapeDtypeStruct((S, dc), F32), jax.ShapeDtypeStruct((S, dc), BF16)],
        scratch_shapes=[pltpu.VMEM((SUBLANES, tm + H, dc), F32)],
        compiler_params=_params(("parallel",)),
    )(proj, proj, w, b, lg, lb)


def _conv_bwd_ln(name, conv, dact, lg, lb):
    S, dc = conv.shape
    tm = _pick(S, 256, LANES)

    def body(c_ref, d_ref, lg_ref, lb_ref, dc_ref, dlg_ref, dlb_ref, db_ref):
        i = pl.program_id(0)
        _, pull = jax.vjp(_ln_silu, c_ref[...], lg_ref[...], lb_ref[...])
        dcv, dlg, dlb = pull(d_ref[...])
        dc_ref[...] = dcv
        _accumulate(i, dlg_ref, dlg)
        _accumulate(i, dlb_ref, dlb)
        _accumulate(i, db_ref, jnp.sum(dcv, axis=0, keepdims=True))

    vec = _whole((1, dc))
    row = _rows(tm, dc)
    vsd = jax.ShapeDtypeStruct((1, dc), F32)
    return pl.pallas_call(
        body, name=name, grid=(S // tm,), in_specs=[row, row, vec, vec], out_specs=[row, vec, vec, vec],
        out_shape=[jax.ShapeDtypeStruct((S, dc), F32), vsd, vsd, vsd],
        compiler_params=_params(("arbitrary",)),
    )(conv, dact, lg, lb)


def _conv_bwd(name, dconv, proj, w, dproj, dc, job=None):
    S = proj.shape[0]
    K = w.shape[0]
    H = CONV_HALO
    tm = _pick(S, 256, LANES)
    nt = S // tm

    offs_g = [H - (K - 1) + k for k in range(K)]
    offs_d = [K - 1 - k for k in range(K)]
    rb = _pick(tm, 64, SUBLANES)
    lw = min(LANES, dc)
    kp = -(-K // SUBLANES) * SUBLANES

    def body(d_ref, dn_ref, cur_ref, prev_ref, w_ref, dproj_any, da_ref, dw_ref, dbin_ref, rotg, rotd, accw, dglu_s):
        i = pl.program_id(0)
        pv = prev_ref[...]
        cv = cur_ref[...]
        sig = jax.nn.sigmoid(cv[:, dc:])
        rotg[0, 0:H, :] = jnp.where(i > 0, pv[:, :dc] * jax.nn.sigmoid(pv[:, dc:]), 0.0)
        rotg[0, H:, :] = cv[:, :dc] * sig
        rotd[0, 0:tm, :] = d_ref[...]
        rotd[0, tm:, :] = jnp.where(i < nt - 1, dn_ref[...], 0.0)
        _fill_rotations(rotg, offs_g)
        _fill_rotations(rotd, offs_d)

        @pl.when(i == 0)
        def _():
            accw[...] = jnp.zeros_like(accw)

        def chunk(row0):
            for l0 in range(0, dc, lw):
                lanes = slice(l0, l0 + lw)
                dcur = rotd[0, pl.ds(row0, rb), lanes]
                acc = jnp.zeros((rb, lw), F32)
                for k in range(K):
                    a, r = divmod(offs_d[k], SUBLANES)
                    acc = acc + w_ref[k:k + 1, lanes] * rotd[r, pl.ds(row0 + SUBLANES * a, rb), lanes]
                    a, r = divmod(offs_g[k], SUBLANES)
                    prod = dcur * rotg[r, pl.ds(row0 + SUBLANES * a, rb), lanes]
                    accw[SUBLANES * k:SUBLANES * (k + 1), lanes] += jnp.sum(
                        prod.reshape(rb // SUBLANES, SUBLANES, lw), axis=0)
                dglu_s[pl.ds(row0, rb), lanes] = acc

        _for_chunks(tm, rb, chunk)
        dglu = dglu_s[...]
        da = jnp.concatenate([dglu * sig, dglu * cv[:, :dc] * sig * (1.0 - sig)], axis=1)
        da_ref[...] = da.astype(BF16)
        _accumulate(i, dbin_ref, jnp.sum(da, axis=0, keepdims=True))

        @pl.when(i == nt - 1)
        def _():
            dw_ref[...] = jnp.zeros_like(dw_ref)
            for k in range(K):
                dw_ref[k:k + 1, :] = jnp.sum(accw[SUBLANES * k:SUBLANES * (k + 1), :], axis=0, keepdims=True)

    res = _call(
        body, name, (nt,),
        [_rows(tm, dc), _next_halo(tm, H, dc, S // H), _rows(tm, 2 * dc), _prev_halo(tm, H, 2 * dc),
         _whole(w.shape), pl.BlockSpec(memory_space=pl.ANY)],
        [_rows(tm, 2 * dc), _whole((kp, dc)), _whole((1, 2 * dc))],
        [jax.ShapeDtypeStruct(dproj.shape, BF16), jax.ShapeDtypeStruct((kp, dc), F32),
         jax.ShapeDtypeStruct((1, 2 * dc), F32)],
        [pltpu.VMEM((SUBLANES, tm + H, dc), F32), pltpu.VMEM((SUBLANES, tm + H, dc), F32),
         pltpu.VMEM((SUBLANES * K, dc), F32), pltpu.VMEM((tm, dc), F32)],
        ("arbitrary",), (dconv, dconv, proj, proj, w, dproj), aliases={5: 0}, job=job)
    return res if job is None else (res[:3], res[3:])


def _mix(vln, wsm, bst, out_ref, G, CH, hd):
    for n in range(vln.shape[0] // CH):
        for g in range(G):
            blk = vln[n * CH:(n + 1) * CH, g * hd:(g + 1) * hd].astype(BF16)
            out_ref[n * CH:(n + 1) * CH, g * hd:(g + 1) * hd] = (
                jnp.dot(wsm[g], blk, preferred_element_type=F32) + bst[:, g:g + 1])


def _sgu_fwd(name, proj, a_act, wco, wso, lg, lb, ws, bst, D, ds):
    S = proj.shape[0]
    dc = a_act.shape[1]
    G, CH, _ = ws.shape
    hd = ds // G
    nco = wco.shape[2]
    tm = _pick(S, 256, CH)

    def body(s_ref, gt_ref, a_ref, wco_ref, wso_ref, lg_ref, lb_ref, ws_ref, bst_ref,
             ya_ref, yb_ref, uv_ref, mg_ref, vmix):
        z = _gelu(s_ref[...])
        vln = _layer_norm(z[:, ds:], lg_ref[...], lb_ref[...])
        wsm = [_tril_mask(ws_ref[g]).astype(BF16) for g in range(G)]
        _mix(vln, wsm, bst_ref[...], vmix, G, CH, hd)
        uv = (z[:, :ds] * vmix[...]).astype(BF16)
        uv_ref[...] = uv
        a = a_ref[...]
        for p in range(N_CHIPS):
            ya_ref[:, p * nco:(p + 1) * nco] = jnp.dot(a, wco_ref[p], preferred_element_type=F32)
            yb_ref[:, p * nco:(p + 1) * nco] = jnp.dot(uv, wso_ref[p], preferred_element_type=F32)
        gt = gt_ref[...]
        mg_ref[...] = (jax.nn.sigmoid(gt[:, :D]) * ya_ref[...] + jax.nn.sigmoid(gt[:, D:]) * yb_ref[...]).astype(BF16)

    vec = _whole((1, ds))
    sdf = jax.ShapeDtypeStruct((S, D), F32)
    return pl.pallas_call(
        body, name=name, grid=(S // tm,),
        in_specs=[_rows(tm, 2 * ds, 1), _rows(tm, 2 * D, 1), _rows(tm, dc), _whole(wco.shape), _whole(wso.shape),
                  vec, vec, _whole(ws.shape), _whole(bst.shape)],
        out_specs=[_rows(tm, D), _rows(tm, D), _rows(tm, ds), _rows(tm, D)],
        out_shape=[sdf, sdf, jax.ShapeDtypeStruct((S, ds), BF16), jax.ShapeDtypeStruct((S, D), BF16)],
        scratch_shapes=[pltpu.VMEM((tm, ds), F32)],
        compiler_params=_params(("parallel",)),
    )(proj, proj, a_act, wco, wso, lg, lb, ws, bst)


def _merge_bwd(name, dmerged, proj, ya, yb, D):
    S = proj.shape[0]
    tm = _pick(S, 256, LANES)

    def body(dm_ref, gt_ref, ya_ref, yb_ref, dya_ref, dyb_ref, dg_ref, dbin_ref):
        i = pl.program_id(0)
        dm = dm_ref[...]
        gt = gt_ref[...]
        sa = jax.nn.sigmoid(gt[:, :D])
        sb = jax.nn.sigmoid(gt[:, D:])
        dya_ref[...] = (dm * sa).astype(BF16)
        dyb_ref[...] = (dm * sb).astype(BF16)
        dg = jnp.concatenate([dm * ya_ref[...] * sa * (1.0 - sa), dm * yb_ref[...] * sb * (1.0 - sb)], axis=1)
        dg_ref[...] = dg.astype(BF16)
        _accumulate(i, dbin_ref, jnp.sum(dg, axis=0, keepdims=True))

    row = _rows(tm, D)
    sdb = jax.ShapeDtypeStruct((S, D), BF16)
    return pl.pallas_call(
        body, name=name, grid=(S // tm,),
        in_specs=[row, _rows(tm, 2 * D, 1), row, row],
        out_specs=[row, row, _rows(tm, 2 * D, 1), _whole((1, 2 * D))],
        out_shape=[sdb, sdb, jax.ShapeDtypeStruct((S, 4 * D), BF16), jax.ShapeDtypeStruct((1, 2 * D), F32)],
        compiler_params=_params(("arbitrary",)),
    )(dmerged, proj, ya, yb)


def _sgu_bwd(name, proj, duv, lg, lb, ws, bst, dproj, ds):
    S = proj.shape[0]
    G, CH, _ = ws.shape
    hd = ds // G
    tm = _pick(S, 256, CH)

    def body(s_ref, duv_ref, lg_ref, lb_ref, ws_ref, bst_ref, dproj_any,
             dsin_ref, dws_ref, dbs_ref, dlg_ref, dlb_ref, dbin_ref, vmix, dvln):
        i = pl.program_id(0)
        z, pull_gelu = jax.vjp(_gelu, s_ref[...])
        u = z[:, :ds]
        vln, pull_ln = jax.vjp(_layer_norm, z[:, ds:], lg_ref[...], lb_ref[...])
        wsm = [_tril_mask(ws_ref[g]).astype(BF16) for g in range(G)]
        _mix(vln, wsm, bst_ref[...], vmix, G, CH, hd)
        duv = duv_ref[...]
        du = duv * vmix[...]
        dvmix = duv * u
        for g in range(G):
            dws_g = jnp.zeros((CH, CH), F32)
            dbs_g = jnp.zeros((CH, 1), F32)
            for n in range(tm // CH):
                dblk = dvmix[n * CH:(n + 1) * CH, g * hd:(g + 1) * hd]
                vblk = vln[n * CH:(n + 1) * CH, g * hd:(g + 1) * hd].astype(BF16)
                dvln[n * CH:(n + 1) * CH, g * hd:(g + 1) * hd] = lax.dot_general(
                    wsm[g], dblk.astype(BF16), TN, preferred_element_type=F32)
                dws_g = dws_g + lax.dot_general(dblk.astype(BF16), vblk, NT, preferred_element_type=F32)
                dbs_g = dbs_g + jnp.sum(dblk, axis=1, keepdims=True)
            dws_g = _tril_mask(dws_g)
            dbs_g = jnp.broadcast_to(dbs_g, (CH, LANES))

            @pl.when(i == 0)
            def _():
                dws_ref[g] = dws_g
                dbs_ref[g] = dbs_g

            @pl.when(i > 0)
            def _():
                dws_ref[g] += dws_g
                dbs_ref[g] += dbs_g

        dv, dlg, dlb = pull_ln(dvln[...])
        (dsin,) = pull_gelu(jnp.concatenate([du, dv], axis=1))
        dsin_ref[...] = dsin.astype(BF16)
        _accumulate(i, dlg_ref, dlg)
        _accumulate(i, dlb_ref, dlb)
        _accumulate(i, dbin_ref, jnp.sum(dsin, axis=0, keepdims=True))

    vec = _whole((1, ds))
    vsd = jax.ShapeDtypeStruct((1, ds), F32)
    return pl.pallas_call(
        body, name=name, grid=(S // tm,),
        in_specs=[_rows(tm, 2 * ds, 1), _rows(tm, ds), vec, vec, _whole(ws.shape), _whole(bst.shape),
                  pl.BlockSpec(memory_space=pl.ANY)],
        out_specs=[_rows(tm, 2 * ds, 1), _whole((G, CH, CH)), _whole((G, CH, LANES)), vec, vec, _whole((1, 2 * ds))],
        out_shape=[jax.ShapeDtypeStruct(dproj.shape, BF16), jax.ShapeDtypeStruct((G, CH, CH), F32),
                   jax.ShapeDtypeStruct((G, CH, LANES), F32), vsd, vsd, jax.ShapeDtypeStruct((1, 2 * ds), F32)],
        scratch_shapes=[pltpu.VMEM((tm, ds), F32), pltpu.VMEM((tm, ds), F32)],
        input_output_aliases={6: 0},
        compiler_params=_params(("arbitrary",)),
    )(proj, duv, lg, lb, ws, bst, dproj)


def _silu_mul(val, gt):
    return jax.nn.silu(gt) * val


def _rotation_slots(offs):
    slot = {0: 0}
    for r in sorted({o % SUBLANES for o in offs} - {0}):
        slot[r] = len(slot)
    return slot


def _fill_plane_rotations(rot, slot):
    n = rot.shape[2]
    for r, s in slot.items():
        if r:
            rot[:, s, 0:n - SUBLANES, :] = rot[:, 0, r:r + n - SUBLANES, :]


def _ffn_tiles(S, Fh):
    return _pick(S, 256, LANES), _pick(Fh, 1408, LANES)


def _ffn_fwd(name, up, w, b):
    _, S, Fh = up.shape
    K = w.shape[1]
    H = FFN_HALO
    tm, cw = _ffn_tiles(S, Fh)
    r = tm // H

    offs = [H - (K - 1) + k for k in range(K)]
    slot = _rotation_slots(offs)
    rb = _pick(tm, 64, 2 * SUBLANES)
    lw = min(LANES, cw)

    def body(cur_ref, prev_ref, w_ref, b_ref, act_ref, rot):
        i = pl.program_id(1)
        rot[:, 0, 0:H, :] = jnp.where(i > 0, prev_ref[...], 0.0)
        rot[:, 0, H:, :] = cur_ref[...]
        _fill_plane_rotations(rot, slot)

        def chunk(row0):
            for l0 in range(0, cw, lw):
                lanes = slice(l0, l0 + lw)
                c2 = []
                for pln in range(2):
                    acc = jnp.broadcast_to(b_ref[pln, :, lanes], (rb, lw))
                    for k in range(K):
                        a, rr = divmod(offs[k], SUBLANES)
                        acc = acc + w_ref[pln, k:k + 1, lanes] * rot[pln, slot[rr], pl.ds(row0 + SUBLANES * a, rb), lanes]
                    c2.append(acc)
                act_ref[pl.ds(row0, rb), lanes] = _silu_mul(c2[0], c2[1]).astype(BF16)

        _for_chunks(tm, rb, chunk)

    return pl.pallas_call(
        body, name=name, grid=(Fh // cw, S // tm),
        in_specs=[pl.BlockSpec((2, tm, cw), lambda j, i: (0, i, j)),
                  pl.BlockSpec((2, H, cw), lambda j, i: (0, jnp.maximum(i * r - 1, 0), j)),
                  pl.BlockSpec((2, K, cw), lambda j, i: (0, 0, j)),
                  pl.BlockSpec((2, 1, cw), lambda j, i: (0, 0, j))],
        out_specs=pl.BlockSpec((tm, cw), lambda j, i: (i, j)),
        out_shape=jax.ShapeDtypeStruct((S, Fh), BF16),
        scratch_shapes=[pltpu.VMEM((2, len(slot), tm + H, cw), F32)],
        compiler_params=_params(("parallel", "parallel")),
    )(up, up, w, b)


def _ffn_bwd(name, up, dact, w, b, job=None):
    _, S, Fh = up.shape
    K = w.shape[1]
    H = FFN_HALO
    tm, cw = _ffn_tiles(S, Fh)
    r = tm // H
    nt = S // tm
    nhb = S // H
    te = tm + H

    offs_x = [H - (K - 1) + k for k in range(K)]
    offs_d = [K - 1 - k for k in range(K)]
    slot_x = _rotation_slots(offs_x)
    slot_d = _rotation_slots(offs_d)
    rb = _pick(tm, 64, 2 * SUBLANES)
    rbe = _pick(te, 96, SUBLANES)
    lw = min(LANES, cw)

    def body(cur_ref, prev_ref, next_ref, d_ref, dn_ref, w_ref, b_ref, dup_ref, dwb_ref, rotx, dext, rotd, accw):
        i = pl.program_id(1)
        rotx[:, 0, 0:H, :] = jnp.where(i > 0, prev_ref[...], 0.0)
        rotx[:, 0, H:H + tm, :] = cur_ref[...]
        rotx[:, 0, H + tm:, :] = jnp.where(i < nt - 1, next_ref[...], 0.0)
        dext[0:tm, :] = d_ref[...]
        dext[tm:, :] = jnp.where(i < nt - 1, dn_ref[...], 0.0)
        _fill_plane_rotations(rotx, slot_x)

        def chunk_e(row0):
            for l0 in range(0, cw, lw):
                lanes = slice(l0, l0 + lw)
                c2 = []
                for pln in range(2):
                    acc = jnp.broadcast_to(b_ref[pln, :, lanes], (rbe, lw))
                    for k in range(K):
                        a, rr = divmod(offs_x[k], SUBLANES)
                        acc = acc + w_ref[pln, k:k + 1, lanes] * rotx[pln, slot_x[rr], pl.ds(row0 + SUBLANES * a, rbe), lanes]
                    c2.append(acc)
                _, pull = jax.vjp(_silu_mul, c2[0], c2[1])
                dval, dgt = pull(dext[pl.ds(row0, rbe), lanes])
                rotd[0, 0, pl.ds(row0, rbe), lanes] = dval
                rotd[1, 0, pl.ds(row0, rbe), lanes] = dgt

        _for_chunks(te, rbe, chunk_e)
        _fill_plane_rotations(rotd, slot_d)

        @pl.when(i == 0)
        def _():
            accw[...] = jnp.zeros_like(accw)

        def chunk(row0):
            for l0 in range(0, cw, lw):
                lanes = slice(l0, l0 + lw)
                for pln in range(2):
                    dcur = rotd[pln, 0, pl.ds(row0, rb), lanes]
                    dup = jnp.zeros((rb, lw), F32)
                    for k in range(K):
                        a, rr = divmod(offs_d[k], SUBLANES)
                        dup = dup + w_ref[pln, k:k + 1, lanes] * rotd[pln, slot_d[rr], pl.ds(row0 + SUBLANES * a, rb), lanes]
                        a, rr = divmod(offs_x[k], SUBLANES)
                        prod = dcur * rotx[pln, slot_x[rr], pl.ds(row0 + SUBLANES * a, rb), lanes]
                        accw[pln, SUBLANES * k:SUBLANES * (k + 1), lanes] += jnp.sum(
                            prod.reshape(rb // SUBLANES, SUBLANES, lw), axis=0)
                    accw[pln, SUBLANES * K:SUBLANES * (K + 1), lanes] += jnp.sum(
                        dcur.reshape(rb // SUBLANES, SUBLANES, lw), axis=0)
                    dup_ref[pln, pl.ds(row0, rb), lanes] = dup.astype(BF16)

        _for_chunks(tm, rb, chunk)

        @pl.when(i == nt - 1)
        def _():
            dwb_ref[...] = jnp.zeros_like(dwb_ref)
            for pln in range(2):
                for k in range(K + 1):
                    dwb_ref[pln, k:k + 1, :] = jnp.sum(accw[pln, SUBLANES * k:SUBLANES * (k + 1), :], axis=0,
                                                       keepdims=True)

    res = _call(
        body, name, (Fh // cw, nt),
        [pl.BlockSpec((2, tm, cw), lambda j, i: (0, i, j)),
         pl.BlockSpec((2, H, cw), lambda j, i: (0, jnp.maximum(i * r - 1, 0), j)),
         pl.BlockSpec((2, H, cw), lambda j, i: (0, jnp.minimum((i + 1) * r, nhb - 1), j)),
         pl.BlockSpec((tm, cw), lambda j, i: (i, j)),
         pl.BlockSpec((H, cw), lambda j, i: (jnp.minimum((i + 1) * r, nhb - 1), j)),
         pl.BlockSpec((2, K, cw), lambda j, i: (0, 0, j)),
         pl.BlockSpec((2, 1, cw), lambda j, i: (0, 0, j))],
        [pl.BlockSpec((2, tm, cw), lambda j, i: (0, i, j)), pl.BlockSpec((2, SUBLANES, cw), lambda j, i: (0, 0, j))],
        [jax.ShapeDtypeStruct((2, S, Fh), BF16), jax.ShapeDtypeStruct((2, SUBLANES, Fh), F32)],
        [pltpu.VMEM((2, len(slot_x), tm + 2 * H, cw), F32), pltpu.VMEM((te, cw), F32),
         pltpu.VMEM((2, len(slot_d), te, cw), F32), pltpu.VMEM((2, SUBLANES * (K + 1), cw), F32)],
        ("parallel", "arbitrary"), (up, up, up, dact, dact, w, b), job=job)
    return res if job is None else (res[:2], res[2:])


def _rms(x, g):
    return x * lax.rsqrt(jnp.mean(x * x, axis=-1, keepdims=True) + EPS) * g


def _final(name, x2, target, gf):
    S, D = x2.shape
    tm = _pick(S, 256, LANES)

    def body(x_ref, t_ref, g_ref, dx_ref, dg_ref, loss_ref):
        i = pl.program_id(0)
        y, pull = jax.vjp(_rms, x_ref[...], g_ref[...])
        e = y - t_ref[...]
        dx, dg = pull(e / D)
        dx_ref[...] = dx
        _accumulate(i, dg_ref, dg)
        part = 0.5 * jnp.sum(jnp.mean(jnp.square(e), axis=-1, keepdims=True), axis=0, keepdims=True)
        _accumulate(i, loss_ref, jnp.broadcast_to(part, (SUBLANES, LANES)))

    row = _rows(tm, D)
    return pl.pallas_call(
        body, name=name, grid=(S // tm,), in_specs=[row, row, _whole((1, D))],
        out_specs=[row, _whole((1, D)), _whole((SUBLANES, LANES))],
        out_shape=[jax.ShapeDtypeStruct((S, D), F32), jax.ShapeDtypeStruct((1, D), F32),
                   jax.ShapeDtypeStruct((SUBLANES, LANES), F32)],
        compiler_params=_params(("arbitrary",)),
    )(x2, target, gf)


def _gate_bwd(name, dx, out, gate):
    S, D = dx.shape
    tm = _pick(S, 512, LANES)

    def body(dx_ref, o_ref, g_ref, do_ref, dg_ref):
        i = pl.program_id(0)
        dxv = dx_ref[...]
        do_ref[...] = (dxv * g_ref[...]).astype(BF16)
        _accumulate(i, dg_ref, jnp.sum(dxv * o_ref[...], axis=0, keepdims=True))

    row = _rows(tm, D)
    return pl.pallas_call(
        body, name=name, grid=(S // tm,), in_specs=[row, row, _whole((1, D))],
        out_specs=[row, _whole((1, D))],
        out_shape=[jax.ShapeDtypeStruct((S, D), BF16), jax.ShapeDtypeStruct((1, D), F32)],
        compiler_params=_params(("arbitrary",)),
    )(dx, out, gate)


def _ada_fwd(name, c_pad, w_ada, b_cols):
    nb, D = c_pad.shape
    n = w_ada.shape[1]
    tn = _pick(n, 1024, LANES)

    def body(c_ref, w_ref, b_ref, o_ref):
        o_ref[...] = jnp.dot(jax.nn.silu(c_ref[...]).astype(BF16), w_ref[...].astype(BF16),
                             preferred_element_type=F32) + b_ref[...]

    return pl.pallas_call(
        body, name=name, grid=(n // tn,),
        in_specs=[_whole((nb, D)), pl.BlockSpec((D, tn), lambda j: (0, j)), pl.BlockSpec((1, tn), lambda j: (0, j))],
        out_specs=pl.BlockSpec((nb, tn), lambda j: (0, j)),
        out_shape=jax.ShapeDtypeStruct((nb, n), F32), compiler_params=_params(("parallel",)),
    )(c_pad, w_ada, b_cols)


def _ada_wgrad(name, c_t, dmod_cols):
    D, nb = c_t.shape
    n = dmod_cols.shape[1]
    tr = _pick(D, 256, SUBLANES)

    def body(c_ref, d_ref, o_ref):
        ca = jax.nn.silu(c_ref[...])
        acc = ca[:, 0:1] * d_ref[0:1, :]
        for b in range(1, nb):
            acc = acc + ca[:, b:b + 1] * d_ref[b:b + 1, :]
        o_ref[...] = acc

    return pl.pallas_call(
        body, name=name, grid=(D // tr,),
        in_specs=[pl.BlockSpec((tr, nb), lambda i: (i, 0)), _whole((nb, n))],
        out_specs=pl.BlockSpec((tr, n), lambda i: (i, 0)),
        out_shape=jax.ShapeDtypeStruct((D, n), F32), compiler_params=_params(("parallel",)),
    )(c_t, dmod_cols)


def kernel(x, c, w_ada, b_ada, norm1_g, w_in, b_in, conv_dw_w, conv_dw_b, conv_ln_g, conv_ln_b, w_conv_out, sgu_ln_g, sgu_ln_b, w_spatial, b_spatial, w_sgu_out, w_out, norm2_g, w_up, ffn_dw_w, ffn_dw_b, w_down, final_g, loss_target, m_w_ada, m_b_ada, m_norm1_g, m_w_in, m_b_in, m_conv_dw_w, m_conv_dw_b, m_conv_ln_g, m_conv_ln_b, m_w_conv_out, m_sgu_ln_g, m_sgu_ln_b, m_w_spatial, m_b_spatial, m_w_sgu_out, m_w_out, m_norm2_g, m_w_up, m_ffn_dw_w, m_ffn_dw_b, m_w_down, m_final_g, v_w_ada, v_b_ada, v_norm1_g, v_w_in, v_b_in, v_conv_dw_w, v_conv_dw_b, v_conv_ln_g, v_conv_ln_b, v_w_conv_out, v_sgu_ln_g, v_sgu_ln_b, v_w_spatial, v_b_spatial, v_w_sgu_out, v_w_out, v_norm2_g, v_w_up, v_ffn_dw_w, v_ffn_dw_b, v_w_down, v_final_g):
    S, D = x.shape[1], x.shape[2]
    dc = w_conv_out.shape[1]
    ds = w_sgu_out.shape[1]
    G, CH = w_spatial.shape[1], w_spatial.shape[2]
    KC = conv_dw_w.shape[1]
    KF = ffn_dw_w.shape[1]
    F2 = ffn_dw_b.shape[1]
    Fh = F2 // 2
    n_ada = w_ada.shape[2]
    n_up = w_up.shape[2]
    ax, ay, ac = _axes()
    chip = 2 * ax + ay
    me = 2 * chip + ac
    c_idx = jnp.reshape(ac, (1,)).astype(jnp.int32)
    p_idx = jnp.reshape(chip, (1,)).astype(jnp.int32)

    xs = x[0]
    tgt = loss_target[0]

    g1 = _allgather8(_pack([c[0], conv_dw_w[0], ffn_dw_w[0]]), "gather_small_in")
    parts = [_unpack(g1[2 * q], [(D,), conv_dw_w.shape[1:], ffn_dw_w.shape[1:]]) for q in range(N_CHIPS)]
    c_all = jnp.stack([_unpack(g1[d], [(D,)])[0] for d in range(N_DEV)])
    cw_full = jnp.concatenate([pt[1] for pt in parts], axis=1)
    fw_full = jnp.concatenate([pt[2] for pt in parts], axis=1)

    b_cols = lax.dynamic_slice(b_ada, (0, chip * n_ada), (1, n_ada))
    c_pad = jnp.concatenate([c_all, jnp.zeros_like(c_all)], axis=0)
    mod_blk = _ada_fwd("ada_fwd", c_pad, w_ada[0], b_cols)[:N_DEV]
    g2 = _allgather8(_pack([mod_blk]), "gather_mod")
    mod_all = jnp.concatenate([_unpack(g2[2 * q], [(N_DEV, n_ada)])[0] for q in range(N_CHIPS)], axis=1)
    mod = lax.dynamic_slice(mod_all, (me, 0), (1, 6 * D))
    shift1, scale1, gate1, shift2, scale2, gate2 = [mod[:, k * D:(k + 1) * D] for k in range(6)]

    shards = [w_in[0], w_conv_out[0], w_sgu_out[0], w_out[0], w_up[0], w_down[0]]
    names = ["in", "conv_out", "sgu_out", "out", "up", "down"]
    blk = {nm: _cast_into_block(s, p_idx, "cast_" + nm) for s, nm in zip(shards, names)}
    (win_f,) = _run_job(_job_gather([blk["in"]]), "gather_w_in")
    pc_idx = jnp.concatenate([p_idx, c_idx])

    h1 = _modnorm_fwd("modnorm1", xs, norm1_g, scale1, shift1)
    proj, (wco_f, wso_f, wout_f, wup_f) = _mm_fwd(
        "proj", h1, win_f, bias=b_in, job=_job_gather([blk["conv_out"], blk["sgu_out"], blk["out"], blk["up"]]))
    proj = proj[0]
    wout_r = wout_f.reshape(-1, wout_f.shape[2])
    conv, a_act = _conv_fwd("conv_fwd", proj, cw_full, conv_dw_b, conv_ln_g, conv_ln_b, dc)
    bst = jnp.transpose(b_spatial[0])
    ya, yb, uv, merged = _sgu_fwd("sgu_fwd", proj, a_act, wco_f, wso_f, sgu_ln_g, sgu_ln_b, w_spatial[0], bst, D, ds)
    out1, x1 = _mm_fwd_rows("out1", merged, wout_r, xs, gate1)
    h2 = _modnorm_fwd("modnorm2", x1, norm2_g, scale2, shift2)
    up, (wdown_f,) = _mm_fwd("up", h2, wup_f, planes=2, job=_job_gather([blk["down"]]))
    wdown_r = wdown_f.reshape(-1, wdown_f.shape[2])
    fw2 = jnp.stack([fw_full[:, :Fh], fw_full[:, Fh:]])
    fb2 = jnp.stack([ffn_dw_b[:, :Fh], ffn_dw_b[:, Fh:]])
    act = _ffn_fwd("ffn_fwd", up, fw2, fb2)
    out2, x2 = _mm_fwd_rows("out2", act, wdown_r, x1, gate2)
    dx2, d_final_g, loss_blk = _final("final", x2, tgt, final_g.reshape(1, D))
    loss = lax.psum(loss_blk[0, 0], ("x", "y", "c"))

    def add_cores(nm, g, r1):
        return _add_own_half(g, r1, c_idx, "add_cores_" + nm)

    def add_chips(nm, g, r1, r2):
        return _add_chips(g, r1, r2, pc_idx, "add_chips_" + nm)

    dout2, d_gate2 = _gate_bwd("gate2_bwd", dx2, out2, gate2)
    g_wdown = _mm_wgrad_rows("wgrad_down", act, dout2, w_down.shape[1])
    dact, (r1_down,) = _mm_dgrad_rows("dgrad_down", dout2, wdown_r, job=_job_swap_halves([g_wdown]))
    s_down = add_cores("down", g_wdown, r1_down)
    (dup, d_ffn), (r2_down,) = _ffn_bwd("ffn_bwd", up, dact, fw2, fb2, job=_job_scatter_blocks([s_down]))
    h_down = add_chips("down", g_wdown, r1_down, r2_down)
    g_wup, (sib_down,) = _mm_wgrad_cols("wgrad_up", h2, dup, n_up, job=_job_to_sibling([h_down]))
    dh2, (r1_up,) = _mm_dgrad_cols("dgrad_up", dup, wup_f, job=_job_swap_halves([g_wup]))
    s_up = add_cores("up", g_wup, r1_up)
    dx1, d_norm2, d_scale2, d_shift2 = _modnorm_bwd("modnorm2_bwd", x1, dh2, dx2, norm2_g, scale2, shift2)
    dout1, d_gate1 = _gate_bwd("gate1_bwd", dx1, out1, gate1)
    g_wout = _mm_wgrad_rows("wgrad_out", merged, dout1, w_out.shape[1])
    dmerged, (r1_out,) = _mm_dgrad_rows("dgrad_out", dout1, wout_r, job=_job_swap_halves([g_wout]))
    s_out = add_cores("out", g_wout, r1_out)
    dya, dyb, dproj, dbin_g = _merge_bwd("merge_bwd", dmerged, proj, ya, yb, D)
    g_wco = _mm_wgrad_cols("wgrad_conv_out", a_act, dya[None], w_conv_out.shape[2])
    g_wso = _mm_wgrad_cols("wgrad_sgu_out", uv, dyb[None], w_sgu_out.shape[2])
    da_act, (r1_co, r1_so) = _mm_dgrad_cols("dgrad_conv_out", dya[None], wco_f, job=_job_swap_halves([g_wco, g_wso]))
    s_co = add_cores("conv_out", g_wco, r1_co)
    s_so = add_cores("sgu_out", g_wso, r1_so)
    duv = _mm_dgrad_cols("dgrad_sgu_out", dyb[None], wso_f)
    dproj, d_ws, d_bs, d_sgu_g, d_sgu_b, dbin_s = _sgu_bwd("sgu_bwd", proj, duv, sgu_ln_g, sgu_ln_b, w_spatial[0], bst,
                                                           dproj, ds)
    dconv, d_cln_g, d_cln_b, d_conv_b = _conv_bwd_ln("conv_ln_bwd", conv, da_act, conv_ln_g, conv_ln_b)
    (dproj, d_cw, dbin_a), (r2_up,) = _conv_bwd("conv_bwd", dconv, proj, cw_full, dproj, dc,
                                                job=_job_scatter_blocks([s_up]))
    h_up = add_chips("up", g_wup, r1_up, r2_up)
    g_win, (r2_out, r2_co, r2_so, sib_up) = _mm_wgrad_cols(
        "wgrad_in", h1, dproj[None], w_in.shape[2],
        job=_merge_jobs([_job_scatter_blocks([s_out, s_co, s_so]), _job_to_sibling([h_up])]))
    h_out = add_chips("out", g_wout, r1_out, r2_out)
    h_co = add_chips("conv_out", g_wco, r1_co, r2_co)
    h_so = add_chips("sgu_out", g_wso, r1_so, r2_so)
    dh1, (r1_in, sib_out, sib_co, sib_so) = _mm_dgrad_cols(
        "dgrad_in", dproj[None], win_f,
        job=_merge_jobs([_job_swap_halves([g_win]), _job_to_sibling([h_out, h_co, h_so])]))
    s_in = add_cores("in", g_win, r1_in)
    (dxs, d_norm1, d_scale1, d_shift1), (r2_in,) = _modnorm_bwd(
        "modnorm1_bwd", xs, dh1, dx1, norm1_g, scale1, shift1, job=_job_scatter_blocks([s_in]))
    h_in = add_chips("in", g_win, r1_in, r2_in)
    (sib_in,) = _run_job(_job_to_sibling([h_in]), "exchange_w_in")
    big_halves = {"w_in": (h_in, sib_in), "w_conv_out": (h_co, sib_co), "w_sgu_out": (h_so, sib_so),
                  "w_out": (h_out, sib_out), "w_up": (h_up, sib_up), "w_down": (h_down, sib_down)}

    d_mod = jnp.concatenate([d_shift1, d_scale1, d_gate1, d_shift2, d_scale2, d_gate2], axis=1)
    d_b_in = jnp.concatenate([dbin_a, dbin_s, dbin_g], axis=1)
    d_fw = jnp.concatenate([d_ffn[0, :KF], d_ffn[1, :KF]], axis=1)
    d_fb = jnp.concatenate([d_ffn[0, KF:KF + 1], d_ffn[1, KF:KF + 1]], axis=1)
    small = [d_mod, d_norm1, d_b_in, d_conv_b, d_cln_g, d_cln_b, d_sgu_g, d_sgu_b, d_ws, d_bs[:, :, 0], d_norm2,
             d_fb, d_final_g, d_cw[:KC], d_fw]
    small_shapes = [a.shape for a in small]
    g3 = _allgather8(_pack(small), "gather_small_grads")
    summed = _unpack(_sum8(g3, "sum_small_grads"), small_shapes)
    (g_b_ada, g_norm1, g_b_in, g_conv_b, g_cln_g, g_cln_b, g_sgu_g, g_sgu_b, g_ws, g_bs, g_norm2, g_fb, g_final,
     g_cw_full, g_fw_full) = summed
    n_cw = conv_dw_w.shape[2]
    n_fw = ffn_dw_w.shape[2]
    g_cw = lax.dynamic_slice(g_cw_full, (0, chip * n_cw), (KC, n_cw))
    g_fw = lax.dynamic_slice(g_fw_full, (0, chip * n_fw), (KF, n_fw))
    dmod_all = jnp.stack([_unpack(g3[d], [(6 * D,)])[0] for d in range(N_DEV)])
    dmod_cols = lax.dynamic_slice(dmod_all, (0, chip * n_ada), (N_DEV, n_ada))
    g_wada = _ada_wgrad("ada_wgrad", jnp.transpose(c_all), dmod_cols)

    grads = {
        "w_ada": g_wada[None], "b_ada": g_b_ada, "norm1_g": g_norm1, "b_in": g_b_in,
        "conv_dw_w": g_cw[None], "conv_dw_b": g_conv_b, "conv_ln_g": g_cln_g, "conv_ln_b": g_cln_b,
        "sgu_ln_g": g_sgu_g, "sgu_ln_b": g_sgu_b, "w_spatial": g_ws[None],
        "b_spatial": g_bs[None], "norm2_g": g_norm2, "ffn_dw_w": g_fw[None], "ffn_dw_b": g_fb,
        "final_g": g_final.reshape(D),
    }
    weights = dict(w_ada=w_ada, b_ada=b_ada, norm1_g=norm1_g, w_in=w_in, b_in=b_in, conv_dw_w=conv_dw_w, conv_dw_b=conv_dw_b, conv_ln_g=conv_ln_g, conv_ln_b=conv_ln_b, w_conv_out=w_conv_out, sgu_ln_g=sgu_ln_g, sgu_ln_b=sgu_ln_b, w_spatial=w_spatial, b_spatial=b_spatial, w_sgu_out=w_sgu_out, w_out=w_out, norm2_g=norm2_g, w_up=w_up, ffn_dw_w=ffn_dw_w, ffn_dw_b=ffn_dw_b, w_down=w_down, final_g=final_g)
    m_in = dict(w_ada=m_w_ada, b_ada=m_b_ada, norm1_g=m_norm1_g, w_in=m_w_in, b_in=m_b_in, conv_dw_w=m_conv_dw_w, conv_dw_b=m_conv_dw_b, conv_ln_g=m_conv_ln_g, conv_ln_b=m_conv_ln_b, w_conv_out=m_w_conv_out, sgu_ln_g=m_sgu_ln_g, sgu_ln_b=m_sgu_ln_b, w_spatial=m_w_spatial, b_spatial=m_b_spatial, w_sgu_out=m_w_sgu_out, w_out=m_w_out, norm2_g=m_norm2_g, w_up=m_w_up, ffn_dw_w=m_ffn_dw_w, ffn_dw_b=m_ffn_dw_b, w_down=m_w_down, final_g=m_final_g)
    v_in = dict(w_ada=v_w_ada, b_ada=v_b_ada, norm1_g=v_norm1_g, w_in=v_w_in, b_in=v_b_in, conv_dw_w=v_conv_dw_w, conv_dw_b=v_conv_dw_b, conv_ln_g=v_conv_ln_g, conv_ln_b=v_conv_ln_b, w_conv_out=v_w_conv_out, sgu_ln_g=v_sgu_ln_g, sgu_ln_b=v_sgu_ln_b, w_spatial=v_w_spatial, b_spatial=v_b_spatial, w_sgu_out=v_w_sgu_out, w_out=v_w_out, norm2_g=v_norm2_g, w_up=v_w_up, ffn_dw_w=v_ffn_dw_w, ffn_dw_b=v_ffn_dw_b, w_down=v_w_down, final_g=v_final_g)
    order = list(weights.keys())
    large = ["w_ada", "w_in", "w_conv_out", "w_sgu_out", "w_out", "w_up", "w_down"]
    little = [n for n in order if n not in large]
    delta, new_m, new_v = {}, {}, {}
    for n in large:
        shp = weights[n].shape
        two = (shp[1], shp[2])
        if n in big_halves:
            g_, d_, m_, v_ = _adamw_halves(weights[n].reshape(two), big_halves[n][0], big_halves[n][1],
                                           m_in[n].reshape(two), v_in[n].reshape(two), c_idx, "adamw_" + n)
            grads[n] = g_.reshape(shp)
        else:
            d_, m_, v_ = _adamw(weights[n].reshape(two), grads[n].reshape(two), m_in[n].reshape(two),
                                v_in[n].reshape(two), "adamw_" + n)
        delta[n], new_m[n], new_v[n] = d_.reshape(shp), m_.reshape(shp), v_.reshape(shp)
    shapes = [weights[n].shape for n in little]
    d_, m_, v_ = _adamw(_pack([weights[n] for n in little]), _pack([grads[n] for n in little]),
                        _pack([m_in[n] for n in little]), _pack([v_in[n] for n in little]), "adamw_small")
    for n, dd, mm, vv in zip(little, _unpack(d_, shapes), _unpack(m_, shapes), _unpack(v_, shapes)):
        delta[n], new_m[n], new_v[n] = dd, mm, vv
    grad_out = [grads[n].reshape(weights[n].shape) for n in order]
    return (loss, dxs[None], *grad_out, *[delta[n] for n in order], *[new_m[n] for n in order],
            *[new_v[n] for n in order])
```

```python
import functools

import jax
import jax.numpy as jnp
from jax import lax
from jax.experimental import pallas as pl
from jax.experimental.pallas import tpu as pltpu

F32 = jnp.float32
BF16 = jnp.bfloat16
EPS = 1e-6
MESH = pl.DeviceIdType.MESH
N_CHIPS = 4
N_DEV = 8
LANES = 128
SUBLANES = 8
CONV_HALO = 32
FFN_HALO = 8
VMEM_LIMIT_BYTES = 56 * 1024 * 1024

ADAM_LR = 0.001
ADAM_B1 = 0.9
ADAM_B2 = 0.999
ADAM_EPS = 1e-08
ADAM_WD = 0.01
ADAM_STEP = 10

NN = (((1,), (0,)), ((), ()))
NT = (((1,), (1,)), ((), ()))
TN = (((0,), (0,)), ((), ()))


def _params(sem=None):
    return pltpu.CompilerParams(dimension_semantics=sem, vmem_limit_bytes=VMEM_LIMIT_BYTES)


def _pick(dim, pref, mult):
    best = None
    d = mult
    while d <= min(dim, pref):
        if dim % d == 0:
            best = d
        d += mult
    return dim if best is None else best


def _axes():
    return lax.axis_index("x"), lax.axis_index("y"), lax.axis_index("c")


def _modnorm(x, g, scale, shift):
    r = lax.rsqrt(jnp.mean(x * x, axis=-1, keepdims=True) + EPS)
    return (x * r * g) * (1.0 + scale) + shift


def _layer_norm(x, g, b):
    mu = jnp.mean(x, axis=-1, keepdims=True)
    var = jnp.mean(jnp.square(x - mu), axis=-1, keepdims=True)
    return (x - mu) * lax.rsqrt(var + EPS) * g + b


def _gelu(x):
    return 0.5 * x * (1.0 + lax.erf(x * (0.5 ** 0.5)))


def _ln_silu(x, g, b):
    return jax.nn.silu(_layer_norm(x, g, b))


def _tril_mask(ws):
    n = ws.shape[-1]
    row = lax.broadcasted_iota(jnp.int32, (n, n), 0)
    col = lax.broadcasted_iota(jnp.int32, (n, n), 1)
    return jnp.where(row >= col, ws, 0.0)


def _pack(arrs):
    flat = [a.reshape(-1).astype(F32) for a in arrs]
    total = sum(f.shape[0] for f in flat)
    tile = SUBLANES * LANES
    padded = -(-total // tile) * tile
    if padded > total:
        flat = flat + [jnp.zeros((padded - total,), F32)]
    return jnp.concatenate(flat).reshape(padded // LANES, LANES)


def _unpack(buf, shapes):
    flat = buf.reshape(-1)
    out, off = [], 0
    for s in shapes:
        n = 1
        for d in s:
            n *= d
        out.append(flat[off:off + n].reshape(s))
        off += n
    return out


def _allgather8(buf, name):
    R, L = buf.shape

    def body(in_ref, out_ref, send_sems, recv_sems, local_sem):
        x, y, c = _axes()
        me = 4 * x + 2 * y + c
        mine = pltpu.make_async_copy(in_ref, out_ref.at[me], local_sem)
        mine.start()
        peers = []
        for k in range(1, N_DEV):
            px = 1 - x if k & 4 else x
            py = 1 - y if k & 2 else y
            pc = 1 - c if k & 1 else c
            peers.append((px, py, pc))
        sends = []
        for k, peer in enumerate(peers):
            cp = pltpu.make_async_remote_copy(
                src_ref=in_ref, dst_ref=out_ref.at[me], send_sem=send_sems.at[k], recv_sem=recv_sems.at[k],
                device_id=peer, device_id_type=MESH)
            cp.start()
            sends.append(cp)
        for k, (px, py, pc) in enumerate(peers):
            pltpu.make_async_remote_copy(
                src_ref=in_ref, dst_ref=out_ref.at[4 * px + 2 * py + pc], send_sem=send_sems.at[k],
                recv_sem=recv_sems.at[k], device_id=(px, py, pc), device_id_type=MESH).wait_recv()
        for cp in sends:
            cp.wait_send()
        mine.wait()

    return pl.pallas_call(
        body, name=name,
        out_shape=jax.ShapeDtypeStruct((N_DEV, R, L), buf.dtype),
        in_specs=[pl.BlockSpec(memory_space=pltpu.VMEM)],
        out_specs=pl.BlockSpec(memory_space=pltpu.VMEM),
        scratch_shapes=[pltpu.SemaphoreType.DMA((N_DEV - 1,)), pltpu.SemaphoreType.DMA((N_DEV - 1,)),
                        pltpu.SemaphoreType.DMA],
        compiler_params=pltpu.CompilerParams(vmem_limit_bytes=VMEM_LIMIT_BYTES),
    )(buf)


def _other_chips(x, y):
    return [(1 - x, y), (x, 1 - y), (1 - x, 1 - y)]


def _cast_into_block(shard, p_idx, name):
    K, n = shard.shape
    tr = _pick(K, max(SUBLANES, (1 << 19) // n), 2 * SUBLANES)

    def body(p_ref, s_ref, o_ref):
        o_ref[...] = s_ref[...].astype(BF16)

    return pl.pallas_call(
        body, name=name,
        grid_spec=pltpu.PrefetchScalarGridSpec(
            num_scalar_prefetch=1, grid=(K // tr,),
            in_specs=[pl.BlockSpec((tr, n), lambda i, p: (i, 0))],
            out_specs=pl.BlockSpec((None, tr, n), lambda i, p: (p[0], i, 0))),
        out_shape=jax.ShapeDtypeStruct((N_CHIPS, K, n), BF16),
        compiler_params=_params(("parallel",)),
    )(p_idx, shard)


class _Job:
    def __init__(self, ins, outs, aliases, n_sems, make):
        self.ins, self.outs, self.aliases, self.n_sems, self.make = list(ins), list(outs), list(aliases), n_sems, make


def _merge_jobs(jobs):
    ins, outs, aliases, offs = [], [], [], []
    n_sems = 0
    for jb in jobs:
        offs.append((len(ins), len(outs), n_sems))
        aliases += [(len(ins) + a, len(outs) + b) for a, b in jb.aliases]
        ins += jb.ins
        outs += jb.outs
        n_sems += jb.n_sems

    def make(in_refs, out_refs, send_sems, recv_sems, base=0):
        made = []
        for jb, (oi, oo, os_) in zip(jobs, offs):
            made.append(jb.make(in_refs[oi:oi + len(jb.ins)], out_refs[oo:oo + len(jb.outs)],
                                send_sems, recv_sems, base + os_))

        def start():
            for st, _ in made:
                st()

        def finish():
            for _, fin in made:
                fin()

        return start, finish

    return _Job(ins, outs, aliases, n_sems, make)


def _job_gather(fulls):
    nw = len(fulls)

    def make(in_refs, outs, send_sems, recv_sems, base=0):
        x, y, c = _axes()
        p = 2 * x + y
        chips = _other_chips(x, y)

        def rows(w, mine):
            kh = outs[w].shape[1] // 2
            return pl.ds((c if mine else 1 - c) * kh, kh)

        def over_ici(w, j, block):
            qx, qy = chips[j]
            return pltpu.make_async_remote_copy(
                src_ref=outs[w].at[block, rows(w, True)], dst_ref=outs[w].at[block, rows(w, True)],
                send_sem=send_sems.at[base + 6 * w + j], recv_sem=recv_sems.at[base + 6 * w + j],
                device_id=(qx, qy, c), device_id_type=MESH)

        def over_d2d(w, j, mine):
            qx, qy = chips[j]
            return pltpu.make_async_remote_copy(
                src_ref=outs[w].at[2 * qx + qy, rows(w, mine)], dst_ref=outs[w].at[2 * qx + qy, rows(w, mine)],
                send_sem=send_sems.at[base + 6 * w + 3 + j], recv_sem=recv_sems.at[base + 6 * w + 3 + j],
                device_id=(x, y, 1 - c), device_id_type=MESH)

        def start():
            for w in range(nw):
                for j in range(3):
                    over_ici(w, j, p).start()

        def finish():
            for w in range(nw):
                for j, (qx, qy) in enumerate(chips):
                    over_ici(w, j, 2 * qx + qy).wait_recv()
                    over_d2d(w, j, True).start()
            for w in range(nw):
                for j in range(3):
                    over_d2d(w, j, False).wait_recv()
            for w in range(nw):
                for j in range(3):
                    over_ici(w, j, p).wait_send()
                    over_d2d(w, j, True).wait_send()

        return start, finish

    return _Job(fulls, [jax.ShapeDtypeStruct(f.shape, f.dtype) for f in fulls], [(w, w) for w in range(nw)], 6 * nw, make)


def _job_simple(ins, outs, n_per, copies_of):
    nw = len(ins)

    def make(in_refs, out_refs, send_sems, recv_sems, base=0):
        x, y, c = _axes()
        copies = []
        for w in range(nw):
            for j, (src, dst, dev) in enumerate(copies_of(w, in_refs[w], out_refs[w], x, y, c)):
                k = base + n_per * w + j
                copies.append(pltpu.make_async_remote_copy(
                    src_ref=src, dst_ref=dst, send_sem=send_sems.at[k], recv_sem=recv_sems.at[k],
                    device_id=dev, device_id_type=MESH))

        def start():
            for cp in copies:
                cp.start()

        def finish():
            for cp in copies:
                cp.wait()

        return start, finish

    return _Job(ins, outs, [], n_per * nw, make)


def _job_swap_halves(grads):
    def copies_of(w, src, dst, x, y, c):
        kh = src.shape[1] // 2
        return [(src.at[:, pl.ds((1 - c) * kh, kh), :], dst, (x, y, 1 - c))]

    outs = [jax.ShapeDtypeStruct((g.shape[0], g.shape[1] // 2, g.shape[2]), g.dtype) for g in grads]
    return _job_simple(grads, outs, 1, copies_of)


def _job_scatter_blocks(sums):
    def copies_of(w, src, dst, x, y, c):
        return [(src.at[2 * qx + qy], dst.at[j], (qx, qy, c)) for j, (qx, qy) in enumerate(_other_chips(x, y))]

    outs = [jax.ShapeDtypeStruct((3,) + s.shape[1:], s.dtype) for s in sums]
    return _job_simple(sums, outs, 3, copies_of)


def _job_to_sibling(arrs):
    def copies_of(w, src, dst, x, y, c):
        return [(src, dst, (x, y, 1 - c))]

    outs = [jax.ShapeDtypeStruct(a.shape, a.dtype) for a in arrs]
    return _job_simple(arrs, outs, 1, copies_of)


def _run_job(job, name):
    ni, no = len(job.ins), len(job.outs)

    def body(*refs):
        start, finish = job.make(refs[:ni], refs[ni:ni + no], refs[-2], refs[-1])
        start()
        finish()

    any_spec = pl.BlockSpec(memory_space=pl.ANY)
    return pl.pallas_call(
        body, name=name, out_shape=job.outs, in_specs=[any_spec] * ni, out_specs=[any_spec] * no,
        input_output_aliases=dict(job.aliases),
        scratch_shapes=[pltpu.SemaphoreType.DMA((job.n_sems,)), pltpu.SemaphoreType.DMA((job.n_sems,))],
    )(*job.ins)


def _call(body, name, grid, in_specs, out_specs, out_shape, scratch_shapes, semantics, args, aliases=None, job=None):
    in_specs, out_specs, out_shape = list(in_specs), list(out_specs), list(out_shape)
    scratch_shapes = list(scratch_shapes)
    if job is None:
        return pl.pallas_call(
            body, name=name, grid=grid, in_specs=in_specs, out_specs=out_specs, out_shape=out_shape,
            scratch_shapes=scratch_shapes, input_output_aliases=dict(aliases or {}),
            compiler_params=_params(semantics))(*args)
    n_in, n_out, n_scr = len(args), len(out_shape), len(scratch_shapes)
    ni, no = len(job.ins), len(job.outs)

    def wrapped(*refs):
        ins, cins = refs[:n_in], refs[n_in:n_in + ni]
        outs, couts = refs[n_in + ni:n_in + ni + n_out], refs[n_in + ni + n_out:n_in + ni + n_out + no]
        scr = refs[n_in + ni + n_out + no:n_in + ni + n_out + no + n_scr]
        start, finish = job.make(cins, couts, refs[-2], refs[-1])
        first = functools.reduce(jnp.logical_and, [pl.program_id(a) == 0 for a in range(len(grid))])
        last = functools.reduce(jnp.logical_and, [pl.program_id(a) == grid[a] - 1 for a in range(len(grid))])
        pl.when(first)(start)
        body(*ins, *outs, *scr)
        pl.when(last)(finish)

    any_spec = pl.BlockSpec(memory_space=pl.ANY)
    all_aliases = dict(aliases or {})
    for a, b in job.aliases:
        all_aliases[n_in + a] = n_out + b
    return pl.pallas_call(
        wrapped, name=name, grid=grid, in_specs=in_specs + [any_spec] * ni, out_specs=out_specs + [any_spec] * no,
        out_shape=out_shape + job.outs,
        scratch_shapes=scratch_shapes + [pltpu.SemaphoreType.DMA((job.n_sems,)), pltpu.SemaphoreType.DMA((job.n_sems,))],
        input_output_aliases=all_aliases,
        compiler_params=_params(tuple("arbitrary" for _ in grid)))(*args, *job.ins)


def _add_own_half(g, r, c_idx, name):
    nb, K, n = g.shape
    kh = K // 2
    tr = _pick(kh, max(SUBLANES, (1 << 19) // n), 2 * SUBLANES)
    per = kh // tr

    def body(c_ref, g_ref, r_ref, o_ref):
        o_ref[...] = (g_ref[...] + r_ref[...]).astype(BF16)

    return pl.pallas_call(
        body, name=name,
        grid_spec=pltpu.PrefetchScalarGridSpec(
            num_scalar_prefetch=1, grid=(nb, per),
            in_specs=[pl.BlockSpec((None, tr, n), lambda b, i, c: (b, c[0] * per + i, 0)),
                      pl.BlockSpec((None, tr, n), lambda b, i, c: (b, i, 0))],
            out_specs=pl.BlockSpec((None, tr, n), lambda b, i, c: (b, i, 0))),
        out_shape=jax.ShapeDtypeStruct((nb, kh, n), BF16),
        compiler_params=_params(("parallel", "parallel")),
    )(c_idx, g, r)


def _add_chips(g, r1, r2, pc_idx, name):
    _, K, n = g.shape
    kh = K // 2
    tr = _pick(kh, max(SUBLANES, (1 << 19) // n), 2 * SUBLANES)
    per = kh // tr

    def body(pc_ref, g_ref, r1_ref, r2_ref, o_ref):
        own = g_ref[...] + r1_ref[...]
        o_ref[...] = ((own + r2_ref[0].astype(F32)) + r2_ref[1].astype(F32)) + r2_ref[2].astype(F32)

    return pl.pallas_call(
        body, name=name,
        grid_spec=pltpu.PrefetchScalarGridSpec(
            num_scalar_prefetch=1, grid=(per,),
            in_specs=[pl.BlockSpec((None, tr, n), lambda i, pc: (pc[0], pc[1] * per + i, 0)),
                      pl.BlockSpec((None, tr, n), lambda i, pc: (pc[0], i, 0)),
                      pl.BlockSpec((3, tr, n), lambda i, pc: (0, i, 0))],
            out_specs=pl.BlockSpec((tr, n), lambda i, pc: (i, 0))),
        out_shape=jax.ShapeDtypeStruct((kh, n), F32),
        compiler_params=_params(("parallel",)),
    )(pc_idx, g, r1, r2)


def _sum8(g, name):
    _, R, L = g.shape

    def body(g_ref, o_ref):
        acc = g_ref[0]
        for d in range(1, N_DEV):
            acc = acc + g_ref[d]
        o_ref[...] = acc

    return pl.pallas_call(
        body, name=name, out_shape=jax.ShapeDtypeStruct((R, L), F32),
        in_specs=[pl.BlockSpec(memory_space=pltpu.VMEM)], out_specs=pl.BlockSpec(memory_space=pltpu.VMEM),
        compiler_params=_params(),
    )(g)


def _adamw_math(w, gg, m, v):
    nm = ADAM_B1 * m + (1.0 - ADAM_B1) * gg
    nv = ADAM_B2 * v + (1.0 - ADAM_B2) * jnp.square(gg)
    m_hat = nm / (1.0 - ADAM_B1 ** ADAM_STEP)
    v_hat = nv / (1.0 - ADAM_B2 ** ADAM_STEP)
    return -ADAM_LR * (m_hat / (jnp.sqrt(v_hat) + ADAM_EPS) + ADAM_WD * w), nm, nv


def _adamw(w, g, m, v, name):
    R, C = w.shape
    tr = _pick(R, max(SUBLANES, (1 << 18) // C), SUBLANES)

    def body(w_ref, g_ref, m_ref, v_ref, d_ref, nm_ref, nv_ref):
        d_ref[...], nm_ref[...], nv_ref[...] = _adamw_math(w_ref[...], g_ref[...], m_ref[...], v_ref[...])

    spec = pl.BlockSpec((tr, C), lambda i: (i, 0))
    sd = jax.ShapeDtypeStruct((R, C), F32)
    return pl.pallas_call(
        body, name=name, grid=(R // tr,), in_specs=[spec] * 4, out_specs=[spec] * 3, out_shape=[sd] * 3,
        compiler_params=_params(("parallel",)),
    )(w, g, m, v)


def _adamw_halves(w, g_own, g_sib, m, v, c_idx, name):
    K, n = w.shape
    kh = K // 2
    tr = _pick(kh, max(SUBLANES, (1 << 18) // n), SUBLANES)
    per = kh // tr

    def body(c_ref, w_ref, go_ref, gs_ref, m_ref, v_ref, g_ref, d_ref, nm_ref, nv_ref):
        h = pl.program_id(0)

        def step(gg):
            g_ref[...] = gg
            d_ref[...], nm_ref[...], nv_ref[...] = _adamw_math(w_ref[...], gg, m_ref[...], v_ref[...])

        @pl.when(h == 0)
        def _():
            step(go_ref[...])

        @pl.when(h == 1)
        def _():
            step(gs_ref[...])

    full = pl.BlockSpec((tr, n), lambda h, i, c: (((c[0] + h) % 2) * per + i, 0))
    own = pl.BlockSpec((tr, n), lambda h, i, c: (i * (1 - h), 0))
    sib = pl.BlockSpec((tr, n), lambda h, i, c: (i * h, 0))
    sd = jax.ShapeDtypeStruct((K, n), F32)
    return pl.pallas_call(
        body, name=name,
        grid_spec=pltpu.PrefetchScalarGridSpec(
            num_scalar_prefetch=1, grid=(2, per),
            in_specs=[full, own, sib, full, full], out_specs=[full] * 4),
        out_shape=[sd] * 4,
        compiler_params=_params(("arbitrary", "arbitrary")),
    )(c_idx, w, g_own, g_sib, m, v)


def _matmul(name, grid, dims, a, a_spec, b, b_spec, outs, out_specs, acc_shape,
            extras=(), extra_specs=(), epilogue=None, job=None, fill_into=None):
    nk = grid[2]
    aliases = None
    if fill_into is not None:
        aliases = {2 + len(extras): 0}
        extras = tuple(extras) + (fill_into,)
        extra_specs = tuple(extra_specs) + (pl.BlockSpec(memory_space=pl.ANY),)
    nex = len(extras)
    nout = len(outs)

    def body(a_ref, b_ref, *rest):
        ex, o = rest[:nex], rest[nex:nex + nout]
        part = lax.dot_general(a_ref[...].astype(BF16), b_ref[...].astype(BF16), dims,
                               preferred_element_type=F32)

        def finish(res):
            if epilogue is None:
                o[0][...] = res.astype(o[0].dtype)
            else:
                epilogue(res, ex, o)

        if nk == 1:
            finish(part)
        else:
            acc = o[0] if in_place else rest[-1]
            k = pl.program_id(2)

            @pl.when(k == 0)
            def _():
                acc[...] = part

            @pl.when(k > 0)
            def _():
                acc[...] += part

            if not in_place:
                @pl.when(k == nk - 1)
                def _():
                    finish(acc[...])

    in_place = epilogue is None and nout == 1 and outs[0].dtype == F32
    res = _call(body, name, grid, [a_spec, b_spec] + list(extra_specs), out_specs, outs,
                [] if nk == 1 or in_place else [pltpu.VMEM(acc_shape, F32)], ("parallel", "parallel", "arbitrary"),
                (a, b, *extras), aliases=aliases, job=job)
    return res if job is None else (res[:nout], res[nout:])


def _first(res, job):
    return res[0] if job is None else (res[0][0], res[1])


def _mm_fwd(name, a, wfull, out_dtype=F32, planes=1, bias=None, job=None):
    S, K = a.shape
    _, _, n = wfull.shape
    N = N_CHIPS * n
    tm = _pick(S, 1024, LANES)
    tn = _pick(n, 1408, LANES)
    per = n // tn
    nj = N // tn
    pj = nj // planes
    grid = (S // tm, nj, 1)
    a_spec = pl.BlockSpec((tm, K), lambda i, j, k: (i, 0))
    b_spec = pl.BlockSpec((None, K, tn), lambda i, j, k: (j // per, 0, j % per))
    o_spec = pl.BlockSpec((None, tm, tn), lambda i, j, k: (j // pj, i, j % pj))
    sd = jax.ShapeDtypeStruct((planes, S, N // planes), out_dtype)
    if bias is not None:
        def epi(res, ex, o):
            o[0][...] = (res + ex[0][...]).astype(o[0].dtype)

        out = _matmul(name, grid, NN, a, a_spec, wfull, b_spec, [sd], [o_spec], (tm, tn), extras=(bias,),
                      extra_specs=(pl.BlockSpec((1, tn), lambda i, j, k: (0, j)),), epilogue=epi, job=job)
    else:
        out = _matmul(name, grid, NN, a, a_spec, wfull, b_spec, [sd], [o_spec], (tm, tn), job=job)
    return _first(out, job)


def _mm_fwd_rows(name, a, wrows, resid, gate, job=None):
    S, K = a.shape
    _, N = wrows.shape
    tm = _pick(S, 512, LANES)
    tn = _pick(N, 1024, LANES)
    tk = _pick(K, 2816, LANES)
    grid = (S // tm, N // tn, K // tk)
    a_spec = pl.BlockSpec((tm, tk), lambda i, j, k: (i, k))
    b_spec = pl.BlockSpec((tk, tn), lambda i, j, k: (k, j))
    o_spec = pl.BlockSpec((tm, tn), lambda i, j, k: (i, j))
    g_spec = pl.BlockSpec((1, tn), lambda i, j, k: (0, j))

    def epi(res, ex, o):
        o[0][...] = res
        o[1][...] = ex[0][...] + ex[1][...] * res

    sd = jax.ShapeDtypeStruct((S, N), F32)
    return _matmul(name, grid, NN, a, a_spec, wrows, b_spec, [sd, sd], [o_spec, o_spec], (tm, tn),
                   extras=(resid, gate), extra_specs=(o_spec, g_spec), epilogue=epi, job=job)


def _mm_dgrad_cols(name, dy, wfull, out_dtype=F32, job=None, row_tiles=None, fill_into=None):
    planes, S, npl = dy.shape
    _, K, n = wfull.shape
    tm = _pick(S if row_tiles is None else S // 2, 1024, LANES)
    to = _pick(K, 1024, LANES)
    tk = _pick(n, 2816, LANES)
    per = n // tk
    nk = N_CHIPS * per
    pk = nk // planes
    n_tiles = S // tm
    i0, ni = (0, n_tiles) if row_tiles is None else (int(row_tiles[0] * n_tiles), int(row_tiles[1] * n_tiles))
    grid = (ni, K // to, nk)
    a_spec = pl.BlockSpec((None, tm, tk), lambda i, j, k: (k // pk, i + i0, k % pk))
    b_spec = pl.BlockSpec((None, to, tk), lambda i, j, k: (k // per, j, k % per))
    o_spec = pl.BlockSpec((tm, to), lambda i, j, k: (i + i0, j))
    return _first(_matmul(name, grid, NT, dy, a_spec, wfull, b_spec, [jax.ShapeDtypeStruct((S, K), out_dtype)],
                          [o_spec], (tm, to), job=job, fill_into=fill_into), job)


def _mm_dgrad_rows(name, dy, wrows, out_dtype=F32, job=None):
    S, N = dy.shape
    K, _ = wrows.shape
    tm = _pick(S, 1024, LANES)
    to = _pick(K, 1408, LANES)
    tk = _pick(N, 2048, LANES)
    grid = (S // tm, K // to, N // tk)
    a_spec = pl.BlockSpec((tm, tk), lambda i, j, k: (i, k))
    b_spec = pl.BlockSpec((to, tk), lambda i, j, k: (j, k))
    o_spec = pl.BlockSpec((tm, to), lambda i, j, k: (i, j))
    return _first(_matmul(name, grid, NT, dy, a_spec, wrows, b_spec, [jax.ShapeDtypeStruct((S, K), out_dtype)],
                          [o_spec], (tm, to), job=job), job)


def _mm_wgrad_cols(name, a, dy, n, job=None):
    S, K = a.shape
    planes, _, npl = dy.shape
    N = planes * npl
    to = _pick(K, 1024, LANES)
    tn = _pick(n, 1408, LANES)
    ts = _pick(S, 2048, LANES)
    per = n // tn
    nj = N // tn
    pj = nj // planes
    grid = (K // to, nj, S // ts)
    a_spec = pl.BlockSpec((ts, to), lambda i, j, k: (k, i))
    b_spec = pl.BlockSpec((None, ts, tn), lambda i, j, k: (j // pj, k, j % pj))
    o_spec = pl.BlockSpec((None, to, tn), lambda i, j, k: (j // per, i, j % per))
    return _first(_matmul(name, grid, TN, a, a_spec, dy, b_spec, [jax.ShapeDtypeStruct((N_CHIPS, K, n), F32)],
                          [o_spec], (to, tn), job=job), job)


def _mm_wgrad_rows(name, a, dy, kshard, job=None):
    S, K = a.shape
    _, N = dy.shape
    to = _pick(kshard, 1408, LANES)
    tn = _pick(N, 1024, LANES)
    ts = _pick(S, 2048, LANES)
    per = kshard // to
    grid = (K // to, N // tn, S // ts)
    a_spec = pl.BlockSpec((ts, to), lambda i, j, k: (k, i))
    b_spec = pl.BlockSpec((ts, tn), lambda i, j, k: (k, j))
    o_spec = pl.BlockSpec((None, to, tn), lambda i, j, k: (i // per, i % per, j))
    return _first(_matmul(name, grid, TN, a, a_spec, dy, b_spec, [jax.ShapeDtypeStruct((N_CHIPS, kshard, N), F32)],
                          [o_spec], (to, tn), job=job), job)


def _rows(tm, width, colblk=0):
    return pl.BlockSpec((tm, width), lambda i: (i, colblk))


def _whole(shape):
    nd = len(shape)
    return pl.BlockSpec(shape, lambda i: (0,) * nd)


def _prev_halo(tm, h, width, colblk=0):
    r = tm // h
    return pl.BlockSpec((h, width), lambda i: (jnp.maximum(i * r - 1, 0), colblk))


def _next_halo(tm, h, width, nblk, colblk=0):
    r = tm // h
    return pl.BlockSpec((h, width), lambda i: (jnp.minimum((i + 1) * r, nblk - 1), colblk))


def _accumulate(i, ref, val):
    @pl.when(i == 0)
    def _():
        ref[...] = val

    @pl.when(i > 0)
    def _():
        ref[...] += val


def _fill_rotations(rot, offs):
    n = rot.shape[1]
    for r in sorted({o % SUBLANES for o in offs} - {0}):
        rot[r, 0:n - SUBLANES, :] = rot[0, r:r + n - SUBLANES, :]


def _for_chunks(n_rows, rb, fn):
    def step(j, carry):
        fn(pl.multiple_of(j * rb, rb))
        return carry

    lax.fori_loop(0, n_rows // rb, step, 0)


def _modnorm_fwd(name, x, g, scale, shift):
    S, D = x.shape
    tm = _pick(S, 512, LANES)

    def body(x_ref, g_ref, sc_ref, sh_ref, h_ref):
        h_ref[...] = _modnorm(x_ref[...], g_ref[...], sc_ref[...], sh_ref[...]).astype(BF16)

    vec = _whole((1, D))
    return pl.pallas_call(
        body, name=name, grid=(S // tm,), in_specs=[_rows(tm, D), vec, vec, vec], out_specs=_rows(tm, D),
        out_shape=jax.ShapeDtypeStruct((S, D), BF16), compiler_params=_params(("parallel",)),
    )(x, g, scale, shift)


def _modnorm_bwd(name, x, dh, dx_in, g, scale, shift, job=None):
    S, D = x.shape
    tm = _pick(S, 256, LANES)

    def body(x_ref, dh_ref, dxin_ref, g_ref, sc_ref, sh_ref, dx_ref, dg_ref, dsc_ref, dsh_ref):
        i = pl.program_id(0)
        _, pull = jax.vjp(_modnorm, x_ref[...], g_ref[...], sc_ref[...], sh_ref[...])
        dx, dg, dsc, dsh = pull(dh_ref[...])
        dx_ref[...] = dxin_ref[...] + dx
        _accumulate(i, dg_ref, dg)
        _accumulate(i, dsc_ref, dsc)
        _accumulate(i, dsh_ref, dsh)

    vec = _whole((1, D))
    row = _rows(tm, D)
    vsd = jax.ShapeDtypeStruct((1, D), F32)
    res = _call(body, name, (S // tm,), [row, row, row, vec, vec, vec], [row, vec, vec, vec],
                [jax.ShapeDtypeStruct((S, D), F32), vsd, vsd, vsd], [], ("arbitrary",),
                (x, dh, dx_in, g, scale, shift), job=job)
    return res if job is None else (res[:4], res[4:])


def _conv_fwd(name, proj, w, b, lg, lb, dc):
    S = proj.shape[0]
    K = w.shape[0]
    H = CONV_HALO
    tm = _pick(S, 256, LANES)

    def glu(v):
        return v[:, :dc] * jax.nn.sigmoid(v[:, dc:])

    offs = [H - (K - 1) + k for k in range(K)]
    rb = _pick(tm, 64, SUBLANES)
    lw = min(LANES, dc)

    def body(cur_ref, prev_ref, w_ref, b_ref, lg_ref, lb_ref, conv_ref, act_ref, rot):
        i = pl.program_id(0)
        rot[0, 0:H, :] = jnp.where(i > 0, glu(prev_ref[...]), 0.0)
        rot[0, H:, :] = glu(cur_ref[...])
        _fill_rotations(rot, offs)

        def chunk(row0):
            for l0 in range(0, dc, lw):
                acc = jnp.broadcast_to(b_ref[:, l0:l0 + lw], (rb, lw))
                for k in range(K):
                    a, r = divmod(offs[k], SUBLANES)
                    acc = acc + w_ref[k:k + 1, l0:l0 + lw] * rot[r, pl.ds(row0 + SUBLANES * a, rb), l0:l0 + lw]
                conv_ref[pl.ds(row0, rb), l0:l0 + lw] = acc

        _for_chunks(tm, rb, chunk)
        act_ref[...] = _ln_silu(conv_ref[...], lg_ref[...], lb_ref[...]).astype(BF16)

    vec = _whole((1, dc))
    return pl.pallas_call(
        body, name=name, grid=(S // tm,),
        in_specs=[_rows(tm, 2 * dc), _prev_halo(tm, H, 2 * dc), _whole(w.shape), vec, vec, vec],
        out_specs=[_rows(tm, dc), _rows(tm, dc)],
        out_shape=[jax.ShapeDtypeStruct((S, dc), F32), jax.ShapeDtypeStruct((S, dc), BF16)],
        scratch_shapes=[pltpu.VMEM((SUBLANES, tm + H, dc), F32)],
        compiler_params=_params(("parallel",)),
    )(proj, proj, w, b, lg, lb)


def _conv_bwd_ln(name, conv, dact, lg, lb):
    S, dc = conv.shape
    tm = _pick(S, 256, LANES)

    def body(c_ref, d_ref, lg_ref, lb_ref, dc_ref, dlg_ref, dlb_ref, db_ref):
        i = pl.program_id(0)
        _, pull = jax.vjp(_ln_silu, c_ref[...], lg_ref[...], lb_ref[...])
        dcv, dlg, dlb = pull(d_ref[...])
        dc_ref[...] = dcv
        _accumulate(i, dlg_ref, dlg)
        _accumulate(i, dlb_ref, dlb)
        _accumulate(i, db_ref, jnp.sum(dcv, axis=0, keepdims=True))

    vec = _whole((1, dc))
    row = _rows(tm, dc)
    vsd = jax.ShapeDtypeStruct((1, dc), F32)
    return pl.pallas_call(
        body, name=name, grid=(S // tm,), in_specs=[row, row, vec, vec], out_specs=[row, vec, vec, vec],
        out_shape=[jax.ShapeDtypeStruct((S, dc), F32), vsd, vsd, vsd],
        compiler_params=_params(("arbitrary",)),
    )(conv, dact, lg, lb)


def _conv_bwd(name, dconv, proj, w, dproj, dc, job=None):
    S = proj.shape[0]
    K = w.shape[0]
    H = CONV_HALO
    tm = _pick(S, 256, LANES)
    nt = S // tm

    offs_g = [H - (K - 1) + k for k in range(K)]
    offs_d = [K - 1 - k for k in range(K)]
    rb = _pick(tm, 64, SUBLANES)
    lw = min(LANES, dc)
    kp = -(-K // SUBLANES) * SUBLANES

    def body(d_ref, dn_ref, cur_ref, prev_ref, w_ref, dproj_any, da_ref, dw_ref, dbin_ref, rotg, rotd, accw, dglu_s):
        i = pl.program_id(0)
        pv = prev_ref[...]
        cv = cur_ref[...]
        sig = jax.nn.sigmoid(cv[:, dc:])
        rotg[0, 0:H, :] = jnp.where(i > 0, pv[:, :dc] * jax.nn.sigmoid(pv[:, dc:]), 0.0)
        rotg[0, H:, :] = cv[:, :dc] * sig
        rotd[0, 0:tm, :] = d_ref[...]
        rotd[0, tm:, :] = jnp.where(i < nt - 1, dn_ref[...], 0.0)
        _fill_rotations(rotg, offs_g)
        _fill_rotations(rotd, offs_d)

        @pl.when(i == 0)
        def _():
            accw[...] = jnp.zeros_like(accw)

        def chunk(row0):
            for l0 in range(0, dc, lw):
                lanes = slice(l0, l0 + lw)
                dcur = rotd[0, pl.ds(row0, rb), lanes]
                acc = jnp.zeros((rb, lw), F32)
                for k in range(K):
                    a, r = divmod(offs_d[k], SUBLANES)
                    acc = acc + w_ref[k:k + 1, lanes] * rotd[r, pl.ds(row0 + SUBLANES * a, rb), lanes]
                    a, r = divmod(offs_g[k], SUBLANES)
                    prod = dcur * rotg[r, pl.ds(row0 + SUBLANES * a, rb), lanes]
                    accw[SUBLANES * k:SUBLANES * (k + 1), lanes] += jnp.sum(
                        prod.reshape(rb // SUBLANES, SUBLANES, lw), axis=0)
                dglu_s[pl.ds(row0, rb), lanes] = acc

        _for_chunks(tm, rb, chunk)
        dglu = dglu_s[...]
        da = jnp.concatenate([dglu * sig, dglu * cv[:, :dc] * sig * (1.0 - sig)], axis=1)
        da_ref[...] = da.astype(BF16)
        _accumulate(i, dbin_ref, jnp.sum(da, axis=0, keepdims=True))

        @pl.when(i == nt - 1)
        def _():
            dw_ref[...] = jnp.zeros_like(dw_ref)
            for k in range(K):
                dw_ref[k:k + 1, :] = jnp.sum(accw[SUBLANES * k:SUBLANES * (k + 1), :], axis=0, keepdims=True)

    res = _call(
        body, name, (nt,),
        [_rows(tm, dc), _next_halo(tm, H, dc, S // H), _rows(tm, 2 * dc), _prev_halo(tm, H, 2 * dc),
         _whole(w.shape), pl.BlockSpec(memory_space=pl.ANY)],
        [_rows(tm, 2 * dc), _whole((kp, dc)), _whole((1, 2 * dc))],
        [jax.ShapeDtypeStruct(dproj.shape, BF16), jax.ShapeDtypeStruct((kp, dc), F32),
         jax.ShapeDtypeStruct((1, 2 * dc), F32)],
        [pltpu.VMEM((SUBLANES, tm + H, dc), F32), pltpu.VMEM((SUBLANES, tm + H, dc), F32),
         pltpu.VMEM((SUBLANES * K, dc), F32), pltpu.VMEM((tm, dc), F32)],
        ("arbitrary",), (dconv, dconv, proj, proj, w, dproj), aliases={5: 0}, job=job)
    return res if job is None else (res[:3], res[3:])


def _mix(vln, wsm, bst, out_ref, G, CH, hd):
    for n in range(vln.shape[0] // CH):
        for g in range(G):
            blk = vln[n * CH:(n + 1) * CH, g * hd:(g + 1) * hd].astype(BF16)
            out_ref[n * CH:(n + 1) * CH, g * hd:(g + 1) * hd] = (
                jnp.dot(wsm[g], blk, preferred_element_type=F32) + bst[:, g:g + 1])


def _sgu_fwd(name, proj, a_act, wco, wso, lg, lb, ws, bst, D, ds):
    S = proj.shape[0]
    dc = a_act.shape[1]
    G, CH, _ = ws.shape
    hd = ds // G
    nco = wco.shape[2]
    tm = _pick(S, 256, CH)

    def body(s_ref, gt_ref, a_ref, wco_ref, wso_ref, lg_ref, lb_ref, ws_ref, bst_ref,
             ya_ref, yb_ref, uv_ref, mg_ref, vmix):
        z = _gelu(s_ref[...])
        vln = _layer_norm(z[:, ds:], lg_ref[...], lb_ref[...])
        wsm = [_tril_mask(ws_ref[g]).astype(BF16) for g in range(G)]
        _mix(vln, wsm, bst_ref[...], vmix, G, CH, hd)
        uv = (z[:, :ds] * vmix[...]).astype(BF16)
        uv_ref[...] = uv
        a = a_ref[...]
        for p in range(N_CHIPS):
            ya_ref[:, p * nco:(p + 1) * nco] = jnp.dot(a, wco_ref[p], preferred_element_type=F32)
            yb_ref[:, p * nco:(p + 1) * nco] = jnp.dot(uv, wso_ref[p], preferred_element_type=F32)
        gt = gt_ref[...]
        mg_ref[...] = (jax.nn.sigmoid(gt[:, :D]) * ya_ref[...] + jax.nn.sigmoid(gt[:, D:]) * yb_ref[...]).astype(BF16)

    vec = _whole((1, ds))
    sdf = jax.ShapeDtypeStruct((S, D), F32)
    return pl.pallas_call(
        body, name=name, grid=(S // tm,),
        in_specs=[_rows(tm, 2 * ds, 1), _rows(tm, 2 * D, 1), _rows(tm, dc), _whole(wco.shape), _whole(wso.shape),
                  vec, vec, _whole(ws.shape), _whole(bst.shape)],
        out_specs=[_rows(tm, D), _rows(tm, D), _rows(tm, ds), _rows(tm, D)],
        out_shape=[sdf, sdf, jax.ShapeDtypeStruct((S, ds), BF16), jax.ShapeDtypeStruct((S, D), BF16)],
        scratch_shapes=[pltpu.VMEM((tm, ds), F32)],
        compiler_params=_params(("parallel",)),
    )(proj, proj, a_act, wco, wso, lg, lb, ws, bst)


def _merge_bwd(name, dmerged, proj, ya, yb, D):
    S = proj.shape[0]
    tm = _pick(S, 256, LANES)

    def body(dm_ref, gt_ref, ya_ref, yb_ref, dya_ref, dyb_ref, dg_ref, dbin_ref):
        i = pl.program_id(0)
        dm = dm_ref[...]
        gt = gt_ref[...]
        sa = jax.nn.sigmoid(gt[:, :D])
        sb = jax.nn.sigmoid(gt[:, D:])
        dya_ref[...] = (dm * sa).astype(BF16)
        dyb_ref[...] = (dm * sb).astype(BF16)
        dg = jnp.concatenate([dm * ya_ref[...] * sa * (1.0 - sa), dm * yb_ref[...] * sb * (1.0 - sb)], axis=1)
        dg_ref[...] = dg.astype(BF16)
        _accumulate(i, dbin_ref, jnp.sum(dg, axis=0, keepdims=True))

    row = _rows(tm, D)
    sdb = jax.ShapeDtypeStruct((S, D), BF16)
    return pl.pallas_call(
        body, name=name, grid=(S // tm,),
        in_specs=[row, _rows(tm, 2 * D, 1), row, row],
        out_specs=[row, row, _rows(tm, 2 * D, 1), _whole((1, 2 * D))],
        out_shape=[sdb, sdb, jax.ShapeDtypeStruct((S, 4 * D), BF16), jax.ShapeDtypeStruct((1, 2 * D), F32)],
        compiler_params=_params(("arbitrary",)),
    )(dmerged, proj, ya, yb)


def _sgu_bwd(name, proj, duv, lg, lb, ws, bst, dproj, ds):
    S = proj.shape[0]
    G, CH, _ = ws.shape
    hd = ds // G
    tm = _pick(S, 256, CH)

    def body(s_ref, duv_ref, lg_ref, lb_ref, ws_ref, bst_ref, dproj_any,
             dsin_ref, dws_ref, dbs_ref, dlg_ref, dlb_ref, dbin_ref, vmix, dvln):
        i = pl.program_id(0)
        z, pull_gelu = jax.vjp(_gelu, s_ref[...])
        u = z[:, :ds]
        vln, pull_ln = jax.vjp(_layer_norm, z[:, ds:], lg_ref[...], lb_ref[...])
        wsm = [_tril_mask(ws_ref[g]).astype(BF16) for g in range(G)]
        _mix(vln, wsm, bst_ref[...], vmix, G, CH, hd)
        duv = duv_ref[...]
        du = duv * vmix[...]
        dvmix = duv * u
        for g in range(G):
            dws_g = jnp.zeros((CH, CH), F32)
            dbs_g = jnp.zeros((CH, 1), F32)
            for n in range(tm // CH):
                dblk = dvmix[n * CH:(n + 1) * CH, g * hd:(g + 1) * hd]
                vblk = vln[n * CH:(n + 1) * CH, g * hd:(g + 1) * hd].astype(BF16)
                dvln[n * CH:(n + 1) * CH, g * hd:(g + 1) * hd] = lax.dot_general(
                    wsm[g], dblk.astype(BF16), TN, preferred_element_type=F32)
                dws_g = dws_g + lax.dot_general(dblk.astype(BF16), vblk, NT, preferred_element_type=F32)
                dbs_g = dbs_g + jnp.sum(dblk, axis=1, keepdims=True)
            dws_g = _tril_mask(dws_g)
            dbs_g = jnp.broadcast_to(dbs_g, (CH, LANES))

            @pl.when(i == 0)
            def _():
                dws_ref[g] = dws_g
                dbs_ref[g] = dbs_g

            @pl.when(i > 0)
            def _():
                dws_ref[g] += dws_g
                dbs_ref[g] += dbs_g

        dv, dlg, dlb = pull_ln(dvln[...])
        (dsin,) = pull_gelu(jnp.concatenate([du, dv], axis=1))
        dsin_ref[...] = dsin.astype(BF16)
        _accumulate(i, dlg_ref, dlg)
        _accumulate(i, dlb_ref, dlb)
        _accumulate(i, dbin_ref, jnp.sum(dsin, axis=0, keepdims=True))

    vec = _whole((1, ds))
    vsd = jax.ShapeDtypeStruct((1, ds), F32)
    return pl.pallas_call(
        body, name=name, grid=(S // tm,),
        in_specs=[_rows(tm, 2 * ds, 1), _rows(tm, ds), vec, vec, _whole(ws.shape), _whole(bst.shape),
                  pl.BlockSpec(memory_space=pl.ANY)],
        out_specs=[_rows(tm, 2 * ds, 1), _whole((G, CH, CH)), _whole((G, CH, LANES)), vec, vec, _whole((1, 2 * ds))],
        out_shape=[jax.ShapeDtypeStruct(dproj.shape, BF16), jax.ShapeDtypeStruct((G, CH, CH), F32),
                   jax.ShapeDtypeStruct((G, CH, LANES), F32), vsd, vsd, jax.ShapeDtypeStruct((1, 2 * ds), F32)],
        scratch_shapes=[pltpu.VMEM((tm, ds), F32), pltpu.VMEM((tm, ds), F32)],
        input_output_aliases={6: 0},
        compiler_params=_params(("arbitrary",)),
    )(proj, duv, lg, lb, ws, bst, dproj)


def _silu_mul(val, gt):
    return jax.nn.silu(gt) * val


def _rotation_slots(offs):
    slot = {0: 0}
    for r in sorted({o % SUBLANES for o in offs} - {0}):
        slot[r] = len(slot)
    return slot


def _fill_plane_rotations(rot, slot):
    n = rot.shape[2]
    for r, s in slot.items():
        if r:
            rot[:, s, 0:n - SUBLANES, :] = rot[:, 0, r:r + n - SUBLANES, :]


def _ffn_tiles(S, Fh):
    return _pick(S, 256, LANES), _pick(Fh, 1408, LANES)


def _ffn_fwd(name, up, w, b):
    _, S, Fh = up.shape
    K = w.shape[1]
    H = FFN_HALO
    tm, cw = _ffn_tiles(S, Fh)
    r = tm // H

    offs = [H - (K - 1) + k for k in range(K)]
    slot = _rotation_slots(offs)
    rb = _pick(tm, 64, 2 * SUBLANES)
    lw = min(LANES, cw)

    def body(cur_ref, prev_ref, w_ref, b_ref, act_ref, rot):
        i = pl.program_id(1)
        rot[:, 0, 0:H, :] = jnp.where(i > 0, prev_ref[...], 0.0)
        rot[:, 0, H:, :] = cur_ref[...]
        _fill_plane_rotations(rot, slot)

        def chunk(row0):
            for l0 in range(0, cw, lw):
                lanes = slice(l0, l0 + lw)
                c2 = []
                for pln in range(2):
                    acc = jnp.broadcast_to(b_ref[pln, :, lanes], (rb, lw))
                    for k in range(K):
                        a, rr = divmod(offs[k], SUBLANES)
                        acc = acc + w_ref[pln, k:k + 1, lanes] * rot[pln, slot[rr], pl.ds(row0 + SUBLANES * a, rb), lanes]
                    c2.append(acc)
                act_ref[pl.ds(row0, rb), lanes] = _silu_mul(c2[0], c2[1]).astype(BF16)

        _for_chunks(tm, rb, chunk)

    return pl.pallas_call(
        body, name=name, grid=(Fh // cw, S // tm),
        in_specs=[pl.BlockSpec((2, tm, cw), lambda j, i: (0, i, j)),
                  pl.BlockSpec((2, H, cw), lambda j, i: (0, jnp.maximum(i * r - 1, 0), j)),
                  pl.BlockSpec((2, K, cw), lambda j, i: (0, 0, j)),
                  pl.BlockSpec((2, 1, cw), lambda j, i: (0, 0, j))],
        out_specs=pl.BlockSpec((tm, cw), lambda j, i: (i, j)),
        out_shape=jax.ShapeDtypeStruct((S, Fh), BF16),
        scratch_shapes=[pltpu.VMEM((2, len(slot), tm + H, cw), F32)],
        compiler_params=_params(("parallel", "parallel")),
    )(up, up, w, b)


def _ffn_bwd(name, up, dact, w, b, job=None):
    _, S, Fh = up.shape
    K = w.shape[1]
    H = FFN_HALO
    tm, cw = _ffn_tiles(S, Fh)
    r = tm // H
    nt = S // tm
    nhb = S // H
    te = tm + H

    offs_x = [H - (K - 1) + k for k in range(K)]
    offs_d = [K - 1 - k for k in range(K)]
    slot_x = _rotation_slots(offs_x)
    slot_d = _rotation_slots(offs_d)
    rb = _pick(tm, 64, 2 * SUBLANES)
    rbe = _pick(te, 96, SUBLANES)
    lw = min(LANES, cw)

    def body(cur_ref, prev_ref, next_ref, d_ref, dn_ref, w_ref, b_ref, dup_ref, dwb_ref, rotx, dext, rotd, accw):
        i = pl.program_id(1)
        rotx[:, 0, 0:H, :] = jnp.where(i > 0, prev_ref[...], 0.0)
        rotx[:, 0, H:H + tm, :] = cur_ref[...]
        rotx[:, 0, H + tm:, :] = jnp.where(i < nt - 1, next_ref[...], 0.0)
        dext[0:tm, :] = d_ref[...]
        dext[tm:, :] = jnp.where(i < nt - 1, dn_ref[...], 0.0)
        _fill_plane_rotations(rotx, slot_x)

        def chunk_e(row0):
            for l0 in range(0, cw, lw):
                lanes = slice(l0, l0 + lw)
                c2 = []
                for pln in range(2):
                    acc = jnp.broadcast_to(b_ref[pln, :, lanes], (rbe, lw))
                    for k in range(K):
                        a, rr = divmod(offs_x[k], SUBLANES)
                        acc = acc + w_ref[pln, k:k + 1, lanes] * rotx[pln, slot_x[rr], pl.ds(row0 + SUBLANES * a, rbe), lanes]
                    c2.append(acc)
                _, pull = jax.vjp(_silu_mul, c2[0], c2[1])
                dval, dgt = pull(dext[pl.ds(row0, rbe), lanes])
                rotd[0, 0, pl.ds(row0, rbe), lanes] = dval
                rotd[1, 0, pl.ds(row0, rbe), lanes] = dgt

        _for_chunks(te, rbe, chunk_e)
        _fill_plane_rotations(rotd, slot_d)

        @pl.when(i == 0)
        def _():
            accw[...] = jnp.zeros_like(accw)

        def chunk(row0):
            for l0 in range(0, cw, lw):
                lanes = slice(l0, l0 + lw)
                for pln in range(2):
                    dcur = rotd[pln, 0, pl.ds(row0, rb), lanes]
                    dup = jnp.zeros((rb, lw), F32)
                    for k in range(K):
                        a, rr = divmod(offs_d[k], SUBLANES)
                        dup = dup + w_ref[pln, k:k + 1, lanes] * rotd[pln, slot_d[rr], pl.ds(row0 + SUBLANES * a, rb), lanes]
                        a, rr = divmod(offs_x[k], SUBLANES)
                        prod = dcur * rotx[pln, slot_x[rr], pl.ds(row0 + SUBLANES * a, rb), lanes]
                        accw[pln, SUBLANES * k:SUBLANES * (k + 1), lanes] += jnp.sum(
                            prod.reshape(rb // SUBLANES, SUBLANES, lw), axis=0)
                    accw[pln, SUBLANES * K:SUBLANES * (K + 1), lanes] += jnp.sum(
                        dcur.reshape(rb // SUBLANES, SUBLANES, lw), axis=0)
                    dup_ref[pln, pl.ds(row0, rb), lanes] = dup.astype(BF16)

        _for_chunks(tm, rb, chunk)

        @pl.when(i == nt - 1)
        def _():
            dwb_ref[...] = jnp.zeros_like(dwb_ref)
            for pln in range(2):
                for k in range(K + 1):
                    dwb_ref[pln, k:k + 1, :] = jnp.sum(accw[pln, SUBLANES * k:SUBLANES * (k + 1), :], axis=0,
                                                       keepdims=True)

    res = _call(
        body, name, (Fh // cw, nt),
        [pl.BlockSpec((2, tm, cw), lambda j, i: (0, i, j)),
         pl.BlockSpec((2, H, cw), lambda j, i: (0, jnp.maximum(i * r - 1, 0), j)),
         pl.BlockSpec((2, H, cw), lambda j, i: (0, jnp.minimum((i + 1) * r, nhb - 1), j)),
         pl.BlockSpec((tm, cw), lambda j, i: (i, j)),
         pl.BlockSpec((H, cw), lambda j, i: (jnp.minimum((i + 1) * r, nhb - 1), j)),
         pl.BlockSpec((2, K, cw), lambda j, i: (0, 0, j)),
         pl.BlockSpec((2, 1, cw), lambda j, i: (0, 0, j))],
        [pl.BlockSpec((2, tm, cw), lambda j, i: (0, i, j)), pl.BlockSpec((2, SUBLANES, cw), lambda j, i: (0, 0, j))],
        [jax.ShapeDtypeStruct((2, S, Fh), BF16), jax.ShapeDtypeStruct((2, SUBLANES, Fh), F32)],
        [pltpu.VMEM((2, len(slot_x), tm + 2 * H, cw), F32), pltpu.VMEM((te, cw), F32),
         pltpu.VMEM((2, len(slot_d), te, cw), F32), pltpu.VMEM((2, SUBLANES * (K + 1), cw), F32)],
        ("parallel", "arbitrary"), (up, up, up, dact, dact, w, b), job=job)
    return res if job is None else (res[:2], res[2:])


def _rms(x, g):
    return x * lax.rsqrt(jnp.mean(x * x, axis=-1, keepdims=True) + EPS) * g


def _final(name, x2, target, gf):
    S, D = x2.shape
    tm = _pick(S, 256, LANES)

    def body(x_ref, t_ref, g_ref, dx_ref, dg_ref, loss_ref):
        i = pl.program_id(0)
        y, pull = jax.vjp(_rms, x_ref[...], g_ref[...])
        e = y - t_ref[...]
        dx, dg = pull(e / D)
        dx_ref[...] = dx
        _accumulate(i, dg_ref, dg)
        part = 0.5 * jnp.sum(jnp.mean(jnp.square(e), axis=-1, keepdims=True), axis=0, keepdims=True)
        _accumulate(i, loss_ref, jnp.broadcast_to(part, (SUBLANES, LANES)))

    row = _rows(tm, D)
    return pl.pallas_call(
        body, name=name, grid=(S // tm,), in_specs=[row, row, _whole((1, D))],
        out_specs=[row, _whole((1, D)), _whole((SUBLANES, LANES))],
        out_shape=[jax.ShapeDtypeStruct((S, D), F32), jax.ShapeDtypeStruct((1, D), F32),
                   jax.ShapeDtypeStruct((SUBLANES, LANES), F32)],
        compiler_params=_params(("arbitrary",)),
    )(x2, target, gf)


def _gate_bwd(name, dx, out, gate):
    S, D = dx.shape
    tm = _pick(S, 512, LANES)

    def body(dx_ref, o_ref, g_ref, do_ref, dg_ref):
        i = pl.program_id(0)
        dxv = dx_ref[...]
        do_ref[...] = (dxv * g_ref[...]).astype(BF16)
        _accumulate(i, dg_ref, jnp.sum(dxv * o_ref[...], axis=0, keepdims=True))

    row = _rows(tm, D)
    return pl.pallas_call(
        body, name=name, grid=(S // tm,), in_specs=[row, row, _whole((1, D))],
        out_specs=[row, _whole((1, D))],
        out_shape=[jax.ShapeDtypeStruct((S, D), BF16), jax.ShapeDtypeStruct((1, D), F32)],
        compiler_params=_params(("arbitrary",)),
    )(dx, out, gate)


def _ada_fwd(name, c_pad, w_ada, b_cols):
    nb, D = c_pad.shape
    n = w_ada.shape[1]
    tn = _pick(n, 1024, LANES)

    def body(c_ref, w_ref, b_ref, o_ref):
        o_ref[...] = jnp.dot(jax.nn.silu(c_ref[...]).astype(BF16), w_ref[...].astype(BF16),
                             preferred_element_type=F32) + b_ref[...]

    return pl.pallas_call(
        body, name=name, grid=(n // tn,),
        in_specs=[_whole((nb, D)), pl.BlockSpec((D, tn), lambda j: (0, j)), pl.BlockSpec((1, tn), lambda j: (0, j))],
        out_specs=pl.BlockSpec((nb, tn), lambda j: (0, j)),
        out_shape=jax.ShapeDtypeStruct((nb, n), F32), compiler_params=_params(("parallel",)),
    )(c_pad, w_ada, b_cols)


def _ada_wgrad(name, c_t, dmod_cols):
    D, nb = c_t.shape
    n = dmod_cols.shape[1]
    tr = _pick(D, 256, SUBLANES)

    def body(c_ref, d_ref, o_ref):
        ca = jax.nn.silu(c_ref[...])
        acc = ca[:, 0:1] * d_ref[0:1, :]
        for b in range(1, nb):
            acc = acc + ca[:, b:b + 1] * d_ref[b:b + 1, :]
        o_ref[...] = acc

    return pl.pallas_call(
        body, name=name, grid=(D // tr,),
        in_specs=[pl.BlockSpec((tr, nb), lambda i: (i, 0)), _whole((nb, n))],
        out_specs=pl.BlockSpec((tr, n), lambda i: (i, 0)),
        out_shape=jax.ShapeDtypeStruct((D, n), F32), compiler_params=_params(("parallel",)),
    )(c_t, dmod_cols)


def kernel(x, c, w_ada, b_ada, norm1_g, w_in, b_in, conv_dw_w, conv_dw_b, conv_ln_g, conv_ln_b, w_conv_out, sgu_ln_g, sgu_ln_b, w_spatial, b_spatial, w_sgu_out, w_out, norm2_g, w_up, ffn_dw_w, ffn_dw_b, w_down, final_g, loss_target, m_w_ada, m_b_ada, m_norm1_g, m_w_in, m_b_in, m_conv_dw_w, m_conv_dw_b, m_conv_ln_g, m_conv_ln_b, m_w_conv_out, m_sgu_ln_g, m_sgu_ln_b, m_w_spatial, m_b_spatial, m_w_sgu_out, m_w_out, m_norm2_g, m_w_up, m_ffn_dw_w, m_ffn_dw_b, m_w_down, m_final_g, v_w_ada, v_b_ada, v_norm1_g, v_w_in, v_b_in, v_conv_dw_w, v_conv_dw_b, v_conv_ln_g, v_conv_ln_b, v_w_conv_out, v_sgu_ln_g, v_sgu_ln_b, v_w_spatial, v_b_spatial, v_w_sgu_out, v_w_out, v_norm2_g, v_w_up, v_ffn_dw_w, v_ffn_dw_b, v_w_down, v_final_g):
    S, D = x.shape[1], x.shape[2]
    dc = w_conv_out.shape[1]
    ds = w_sgu_out.shape[1]
    G, CH = w_spatial.shape[1], w_spatial.shape[2]
    KC = conv_dw_w.shape[1]
    KF = ffn_dw_w.shape[1]
    F2 = ffn_dw_b.shape[1]
    Fh = F2 // 2
    n_ada = w_ada.shape[2]
    n_up = w_up.shape[2]
    ax, ay, ac = _axes()
    chip = 2 * ax + ay
    me = 2 * chip + ac
    c_idx = jnp.reshape(ac, (1,)).astype(jnp.int32)
    p_idx = jnp.reshape(chip, (1,)).astype(jnp.int32)

    xs = x[0]
    tgt = loss_target[0]

    g1 = _allgather8(_pack([c[0], conv_dw_w[0], ffn_dw_w[0]]), "gather_small_in")
    parts = [_unpack(g1[2 * q], [(D,), conv_dw_w.shape[1:], ffn_dw_w.shape[1:]]) for q in range(N_CHIPS)]
    c_all = jnp.stack([_unpack(g1[d], [(D,)])[0] for d in range(N_DEV)])
    cw_full = jnp.concatenate([pt[1] for pt in parts], axis=1)
    fw_full = jnp.concatenate([pt[2] for pt in parts], axis=1)

    b_cols = lax.dynamic_slice(b_ada, (0, chip * n_ada), (1, n_ada))
    c_pad = jnp.concatenate([c_all, jnp.zeros_like(c_all)], axis=0)
    mod_blk = _ada_fwd("ada_fwd", c_pad, w_ada[0], b_cols)[:N_DEV]
    g2 = _allgather8(_pack([mod_blk]), "gather_mod")
    mod_all = jnp.concatenate([_unpack(g2[2 * q], [(N_DEV, n_ada)])[0] for q in range(N_CHIPS)], axis=1)
    mod = lax.dynamic_slice(mod_all, (me, 0), (1, 6 * D))
    shift1, scale1, gate1, shift2, scale2, gate2 = [mod[:, k * D:(k + 1) * D] for k in range(6)]

    shards = [w_in[0], w_conv_out[0], w_sgu_out[0], w_out[0], w_up[0], w_down[0]]
    names = ["in", "conv_out", "sgu_out", "out", "up", "down"]
    blk = {nm: _cast_into_block(s, p_idx, "cast_" + nm) for s, nm in zip(shards, names)}
    (win_f,) = _run_job(_job_gather([blk["in"]]), "gather_w_in")
    pc_idx = jnp.concatenate([p_idx, c_idx])

    h1 = _modnorm_fwd("modnorm1", xs, norm1_g, scale1, shift1)
    proj, (wco_f, wso_f, wout_f, wup_f) = _mm_fwd(
        "proj", h1, win_f, bias=b_in, job=_job_gather([blk["conv_out"], blk["sgu_out"], blk["out"], blk["up"]]))
    proj = proj[0]
    wout_r = wout_f.reshape(-1, wout_f.shape[2])
    conv, a_act = _conv_fwd("conv_fwd", proj, cw_full, conv_dw_b, conv_ln_g, conv_ln_b, dc)
    bst = jnp.transpose(b_spatial[0])
    ya, yb, uv, merged = _sgu_fwd("sgu_fwd", proj, a_act, wco_f, wso_f, sgu_ln_g, sgu_ln_b, w_spatial[0], bst, D, ds)
    out1, x1 = _mm_fwd_rows("out1", merged, wout_r, xs, gate1)
    h2 = _modnorm_fwd("modnorm2", x1, norm2_g, scale2, shift2)
    up, (wdown_f,) = _mm_fwd("up", h2, wup_f, planes=2, job=_job_gather([blk["down"]]))
    wdown_r = wdown_f.reshape(-1, wdown_f.shape[2])
    fw2 = jnp.stack([fw_full[:, :Fh], fw_full[:, Fh:]])
    fb2 = jnp.stack([ffn_dw_b[:, :Fh], ffn_dw_b[:, Fh:]])
    act = _ffn_fwd("ffn_fwd", up, fw2, fb2)
    out2, x2 = _mm_fwd_rows("out2", act, wdown_r, x1, gate2)
    dx2, d_final_g, loss_blk = _final("final", x2, tgt, final_g.reshape(1, D))
    loss = lax.psum(loss_blk[0, 0], ("x", "y", "c"))

    def add_cores(nm, g, r1):
        return _add_own_half(g, r1, c_idx, "add_cores_" + nm)

    def add_chips(nm, g, r1, r2):
        return _add_chips(g, r1, r2, pc_idx, "add_chips_" + nm)

    dout2, d_gate2 = _gate_bwd("gate2_bwd", dx2, out2, gate2)
    g_wdown = _mm_wgrad_rows("wgrad_down", act, dout2, w_down.shape[1])
    dact, (r1_down,) = _mm_dgrad_rows("dgrad_down", dout2, wdown_r, job=_job_swap_halves([g_wdown]))
    s_down = add_cores("down", g_wdown, r1_down)
    (dup, d_ffn), (r2_down,) = _ffn_bwd("ffn_bwd", up, dact, fw2, fb2, job=_job_scatter_blocks([s_down]))
    h_down = add_chips("down", g_wdown, r1_down, r2_down)
    g_wup, (sib_down,) = _mm_wgrad_cols("wgrad_up", h2, dup, n_up, job=_job_to_sibling([h_down]))
    dh2, (r1_up,) = _mm_dgrad_cols("dgrad_up", dup, wup_f, job=_job_swap_halves([g_wup]))
    s_up = add_cores("up", g_wup, r1_up)
    dx1, d_norm2, d_scale2, d_shift2 = _modnorm_bwd("modnorm2_bwd", x1, dh2, dx2, norm2_g, scale2, shift2)
    dout1, d_gate1 = _gate_bwd("gate1_bwd", dx1, out1, gate1)
    g_wout = _mm_wgrad_rows("wgrad_out", merged, dout1, w_out.shape[1])
    dmerged, (r1_out,) = _mm_dgrad_rows("dgrad_out", dout1, wout_r, job=_job_swap_halves([g_wout]))
    s_out = add_cores("out", g_wout, r1_out)
    dya, dyb, dproj, dbin_g = _merge_bwd("merge_bwd", dmerged, proj, ya, yb, D)
    g_wco = _mm_wgrad_cols("wgrad_conv_out", a_act, dya[None], w_conv_out.shape[2])
    g_wso = _mm_wgrad_cols("wgrad_sgu_out", uv, dyb[None], w_sgu_out.shape[2])
    da_act, (r1_co, r1_so) = _mm_dgrad_cols("dgrad_conv_out", dya[None], wco_f, job=_job_swap_halves([g_wco, g_wso]))
    s_co = add_cores("conv_out", g_wco, r1_co)
    s_so = add_cores("sgu_out", g_wso, r1_so)
    duv = _mm_dgrad_cols("dgrad_sgu_out", dyb[None], wso_f)
    dproj, d_ws, d_bs, d_sgu_g, d_sgu_b, dbin_s = _sgu_bwd("sgu_bwd", proj, duv, sgu_ln_g, sgu_ln_b, w_spatial[0], bst,
                                                           dproj, ds)
    dconv, d_cln_g, d_cln_b, d_conv_b = _conv_bwd_ln("conv_ln_bwd", conv, da_act, conv_ln_g, conv_ln_b)
    (dproj, d_cw, dbin_a), (r2_up,) = _conv_bwd("conv_bwd", dconv, proj, cw_full, dproj, dc,
                                                job=_job_scatter_blocks([s_up]))
    h_up = add_chips("up", g_wup, r1_up, r2_up)
    g_win, (r2_out, r2_co, r2_so, sib_up) = _mm_wgrad_cols(
        "wgrad_in", h1, dproj[None], w_in.shape[2],
        job=_merge_jobs([_job_scatter_blocks([s_out, s_co, s_so]), _job_to_sibling([h_up])]))
    h_out = add_chips("out", g_wout, r1_out, r2_out)
    h_co = add_chips("conv_out", g_wco, r1_co, r2_co)
    h_so = add_chips("sgu_out", g_wso, r1_so, r2_so)
    dh1, (r1_in, sib_out, sib_co, sib_so) = _mm_dgrad_cols(
        "dgrad_in_a", dproj[None], win_f, row_tiles=(0.0, 0.5),
        job=_merge_jobs([_job_swap_halves([g_win]), _job_to_sibling([h_out, h_co, h_so])]))
    s_in = add_cores("in", g_win, r1_in)
    dh1, (r2_in,) = _mm_dgrad_cols("dgrad_in_b", dproj[None], win_f, row_tiles=(0.5, 0.5), fill_into=dh1,
                                   job=_job_scatter_blocks([s_in]))
    h_in = add_chips("in", g_win, r1_in, r2_in)
    dxs, d_norm1, d_scale1, d_shift1 = _modnorm_bwd("modnorm1_bwd", xs, dh1, dx1, norm1_g, scale1, shift1)
    (sib_in,) = _run_job(_job_to_sibling([h_in]), "exchange_w_in")
    big_halves = {"w_in": (h_in, sib_in), "w_conv_out": (h_co, sib_co), "w_sgu_out": (h_so, sib_so),
                  "w_out": (h_out, sib_out), "w_up": (h_up, sib_up), "w_down": (h_down, sib_down)}

    d_mod = jnp.concatenate([d_shift1, d_scale1, d_gate1, d_shift2, d_scale2, d_gate2], axis=1)
    d_b_in = jnp.concatenate([dbin_a, dbin_s, dbin_g], axis=1)
    d_fw = jnp.concatenate([d_ffn[0, :KF], d_ffn[1, :KF]], axis=1)
    d_fb = jnp.concatenate([d_ffn[0, KF:KF + 1], d_ffn[1, KF:KF + 1]], axis=1)
    small = [d_mod, d_norm1, d_b_in, d_conv_b, d_cln_g, d_cln_b, d_sgu_g, d_sgu_b, d_ws, d_bs[:, :, 0], d_norm2,
             d_fb, d_final_g, d_cw[:KC], d_fw]
    small_shapes = [a.shape for a in small]
    g3 = _allgather8(_pack(small), "gather_small_grads")
    summed = _unpack(_sum8(g3, "sum_small_grads"), small_shapes)
    (g_b_ada, g_norm1, g_b_in, g_conv_b, g_cln_g, g_cln_b, g_sgu_g, g_sgu_b, g_ws, g_bs, g_norm2, g_fb, g_final,
     g_cw_full, g_fw_full) = summed
    n_cw = conv_dw_w.shape[2]
    n_fw = ffn_dw_w.shape[2]
    g_cw = lax.dynamic_slice(g_cw_full, (0, chip * n_cw), (KC, n_cw))
    g_fw = lax.dynamic_slice(g_fw_full, (0, chip * n_fw), (KF, n_fw))
    dmod_all = jnp.stack([_unpack(g3[d], [(6 * D,)])[0] for d in range(N_DEV)])
    dmod_cols = lax.dynamic_slice(dmod_all, (0, chip * n_ada), (N_DEV, n_ada))
    g_wada = _ada_wgrad("ada_wgrad", jnp.transpose(c_all), dmod_cols)

    grads = {
        "w_ada": g_wada[None], "b_ada": g_b_ada, "norm1_g": g_norm1, "b_in": g_b_in,
        "conv_dw_w": g_cw[None], "conv_dw_b": g_conv_b, "conv_ln_g": g_cln_g, "conv_ln_b": g_cln_b,
        "sgu_ln_g": g_sgu_g, "sgu_ln_b": g_sgu_b, "w_spatial": g_ws[None],
        "b_spatial": g_bs[None], "norm2_g": g_norm2, "ffn_dw_w": g_fw[None], "ffn_dw_b": g_fb,
        "final_g": g_final.reshape(D),
    }
    weights = dict(w_ada=w_ada, b_ada=b_ada, norm1_g=norm1_g, w_in=w_in, b_in=b_in, conv_dw_w=conv_dw_w, conv_dw_b=conv_dw_b, conv_ln_g=conv_ln_g, conv_ln_b=conv_ln_b, w_conv_out=w_conv_out, sgu_ln_g=sgu_ln_g, sgu_ln_b=sgu_ln_b, w_spatial=w_spatial, b_spatial=b_spatial, w_sgu_out=w_sgu_out, w_out=w_out, norm2_g=norm2_g, w_up=w_up, ffn_dw_w=ffn_dw_w, ffn_dw_b=ffn_dw_b, w_down=w_down, final_g=final_g)
    m_in = dict(w_ada=m_w_ada, b_ada=m_b_ada, norm1_g=m_norm1_g, w_in=m_w_in, b_in=m_b_in, conv_dw_w=m_conv_dw_w, conv_dw_b=m_conv_dw_b, conv_ln_g=m_conv_ln_g, conv_ln_b=m_conv_ln_b, w_conv_out=m_w_conv_out, sgu_ln_g=m_sgu_ln_g, sgu_ln_b=m_sgu_ln_b, w_spatial=m_w_spatial, b_spatial=m_b_spatial, w_sgu_out=m_w_sgu_out, w_out=m_w_out, norm2_g=m_norm2_g, w_up=m_w_up, ffn_dw_w=m_ffn_dw_w, ffn_dw_b=m_ffn_dw_b, w_down=m_w_down, final_g=m_final_g)
    v_in = dict(w_ada=v_w_ada, b_ada=v_b_ada, norm1_g=v_norm1_g, w_in=v_w_in, b_in=v_b_in, conv_dw_w=v_conv_dw_w, conv_dw_b=v_conv_dw_b, conv_ln_g=v_conv_ln_g, conv_ln_b=v_conv_ln_b, w_conv_out=v_w_conv_out, sgu_ln_g=v_sgu_ln_g, sgu_ln_b=v_sgu_ln_b, w_spatial=v_w_spatial, b_spatial=v_b_spatial, w_sgu_out=v_w_sgu_out, w_out=v_w_out, norm2_g=v_norm2_g, w_up=v_w_up, ffn_dw_w=v_ffn_dw_w, ffn_dw_b=v_ffn_dw_b, w_down=v_w_down, final_g=v_final_g)
    order = list(weights.keys())
    large = ["w_ada", "w_in", "w_conv_out", "w_sgu_out", "w_out", "w_up", "w_down"]
    little = [n for n in order if n not in large]
    delta, new_m, new_v = {}, {}, {}
    for n in large:
        shp = weights[n].shape
        two = (shp[1], shp[2])
        if n in big_halves:
            g_, d_, m_, v_ = _adamw_halves(weights[n].reshape(two), big_halves[n][0], big_halves[n][1],
                                           m_in[n].reshape(two), v_in[n].reshape(two), c_idx, "adamw_" + n)
            grads[n] = g_.reshape(shp)
        else:
            d_, m_, v_ = _adamw(weights[n].reshape(two), grads[n].reshape(two), m_in[n].reshape(two),
                                v_in[n].reshape(two), "adamw_" + n)
        delta[n], new_m[n], new_v[n] = d_.reshape(shp), m_.reshape(shp), v_.reshape(shp)
    shapes = [weights[n].shape for n in little]
    d_, m_, v_ = _adamw(_pack([weights[n] for n in little]), _pack([grads[n] for n in little]),
                        _pack([m_in[n] for n in little]), _pack([v_in[n] for n in little]), "adamw_small")
    for n, dd, mm, vv in zip(little, _unpack(d_, shapes), _unpack(m_, shapes), _unpack(v_, shapes)):
        delta[n], new_m[n], new_v[n] = dd, mm, vv
    grad_out = [grads[n].reshape(weights[n].shape) for n in order]
    return (loss, dxs[None], *grad_out, *[delta[n] for n in order], *[new_m[n] for n in order],
            *[new_v[n] for n in order])
```

```python
import functools

import jax
import jax.numpy as jnp
from jax import lax
from jax.experimental import pallas as pl
from jax.experimental.pallas import tpu as pltpu

F32 = jnp.float32
BF16 = jnp.bfloat16
EPS = 1e-6
MESH = pl.DeviceIdType.MESH
N_CHIPS = 4
N_DEV = 8
LANES = 128
SUBLANES = 8
CONV_HALO = 32
FFN_HALO = 8
VMEM_LIMIT_BYTES = 56 * 1024 * 1024

ADAM_LR = 0.001
ADAM_B1 = 0.9
ADAM_B2 = 0.999
ADAM_EPS = 1e-08
ADAM_WD = 0.01
ADAM_STEP = 10

NN = (((1,), (0,)), ((), ()))
NT = (((1,), (1,)), ((), ()))
TN = (((0,), (0,)), ((), ()))


def _params(sem=None):
    return pltpu.CompilerParams(dimension_semantics=sem, vmem_limit_bytes=VMEM_LIMIT_BYTES)


def _pick(dim, pref, mult):
    best = None
    d = mult
    while d <= min(dim, pref):
        if dim % d == 0:
            best = d
        d += mult
    return dim if best is None else best


def _axes():
    return lax.axis_index("x"), lax.axis_index("y"), lax.axis_index("c")


def _modnorm(x, g, scale, shift):
    r = lax.rsqrt(jnp.mean(x * x, axis=-1, keepdims=True) + EPS)
    return (x * r * g) * (1.0 + scale) + shift


def _layer_norm(x, g, b):
    mu = jnp.mean(x, axis=-1, keepdims=True)
    var = jnp.mean(jnp.square(x - mu), axis=-1, keepdims=True)
    return (x - mu) * lax.rsqrt(var + EPS) * g + b


def _gelu(x):
    return 0.5 * x * (1.0 + lax.erf(x * (0.5 ** 0.5)))


def _ln_silu(x, g, b):
    return jax.nn.silu(_layer_norm(x, g, b))


def _tril_mask(ws):
    n = ws.shape[-1]
    row = lax.broadcasted_iota(jnp.int32, (n, n), 0)
    col = lax.broadcasted_iota(jnp.int32, (n, n), 1)
    return jnp.where(row >= col, ws, 0.0)


def _pack(arrs):
    flat = [a.reshape(-1).astype(F32) for a in arrs]
    total = sum(f.shape[0] for f in flat)
    tile = SUBLANES * LANES
    padded = -(-total // tile) * tile
    if padded > total:
        flat = flat + [jnp.zeros((padded - total,), F32)]
    return jnp.concatenate(flat).reshape(padded // LANES, LANES)


def _unpack(buf, shapes):
    flat = buf.reshape(-1)
    out, off = [], 0
    for s in shapes:
        n = 1
        for d in s:
            n *= d
        out.append(flat[off:off + n].reshape(s))
        off += n
    return out


def _allgather8(buf, name):
    R, L = buf.shape

    def body(in_ref, out_ref, send_sems, recv_sems, local_sem):
        x, y, c = _axes()
        me = 4 * x + 2 * y + c
        mine = pltpu.make_async_copy(in_ref, out_ref.at[me], local_sem)
        mine.start()
        peers = []
        for k in range(1, N_DEV):
            px = 1 - x if k & 4 else x
            py = 1 - y if k & 2 else y
            pc = 1 - c if k & 1 else c
            peers.append((px, py, pc))
        sends = []
        for k, peer in enumerate(peers):
            cp = pltpu.make_async_remote_copy(
                src_ref=in_ref, dst_ref=out_ref.at[me], send_sem=send_sems.at[k], recv_sem=recv_sems.at[k],
                device_id=peer, device_id_type=MESH)
            cp.start()
            sends.append(cp)
        for k, (px, py, pc) in enumerate(peers):
            pltpu.make_async_remote_copy(
                src_ref=in_ref, dst_ref=out_ref.at[4 * px + 2 * py + pc], send_sem=send_sems.at[k],
                recv_sem=recv_sems.at[k], device_id=(px, py, pc), device_id_type=MESH).wait_recv()
        for cp in sends:
            cp.wait_send()
        mine.wait()

    return pl.pallas_call(
        body, name=name,
        out_shape=jax.ShapeDtypeStruct((N_DEV, R, L), buf.dtype),
        in_specs=[pl.BlockSpec(memory_space=pltpu.VMEM)],
        out_specs=pl.BlockSpec(memory_space=pltpu.VMEM),
        scratch_shapes=[pltpu.SemaphoreType.DMA((N_DEV - 1,)), pltpu.SemaphoreType.DMA((N_DEV - 1,)),
                        pltpu.SemaphoreType.DMA],
        compiler_params=pltpu.CompilerParams(vmem_limit_bytes=VMEM_LIMIT_BYTES),
    )(buf)


def _other_chips(x, y):
    return [(1 - x, y), (x, 1 - y), (1 - x, 1 - y)]


def _cast_into_block(shard, p_idx, name):
    K, n = shard.shape
    tr = _pick(K, max(SUBLANES, (1 << 19) // n), 2 * SUBLANES)

    def body(p_ref, s_ref, o_ref):
        o_ref[...] = s_ref[...].astype(BF16)

    return pl.pallas_call(
        body, name=name,
        grid_spec=pltpu.PrefetchScalarGridSpec(
            num_scalar_prefetch=1, grid=(K // tr,),
            in_specs=[pl.BlockSpec((tr, n), lambda i, p: (i, 0))],
            out_specs=pl.BlockSpec((None, tr, n), lambda i, p: (p[0], i, 0))),
        out_shape=jax.ShapeDtypeStruct((N_CHIPS, K, n), BF16),
        compiler_params=_params(("parallel",)),
    )(p_idx, shard)


class _Job:
    def __init__(self, ins, outs, aliases, n_sems, make):
        self.ins, self.outs, self.aliases, self.n_sems, self.make = list(ins), list(outs), list(aliases), n_sems, make


def _merge_jobs(jobs):
    ins, outs, aliases, offs = [], [], [], []
    n_sems = 0
    for jb in jobs:
        offs.append((len(ins), len(outs), n_sems))
        aliases += [(len(ins) + a, len(outs) + b) for a, b in jb.aliases]
        ins += jb.ins
        outs += jb.outs
        n_sems += jb.n_sems

    def make(in_refs, out_refs, send_sems, recv_sems, base=0):
        made = []
        for jb, (oi, oo, os_) in zip(jobs, offs):
            made.append(jb.make(in_refs[oi:oi + len(jb.ins)], out_refs[oo:oo + len(jb.outs)],
                                send_sems, recv_sems, base + os_))

        def start():
            for st, _ in made:
                st()

        def finish():
            for _, fin in made:
                fin()

        return start, finish

    return _Job(ins, outs, aliases, n_sems, make)


def _job_gather(fulls, rel=(0, 1, 2), fresh=False):
    nw = len(fulls)

    def make(in_refs, outs, send_sems, recv_sems, base=0):
        x, y, c = _axes()
        p = 2 * x + y
        chips = _other_chips(x, y)
        srcs = in_refs if fresh else outs

        def rows(w, mine):
            kh = outs[w].shape[1] // 2
            return pl.ds((c if mine else 1 - c) * kh, kh)

        def over_ici(w, j, block):
            qx, qy = chips[j]
            return pltpu.make_async_remote_copy(
                src_ref=srcs[w].at[block, rows(w, True)], dst_ref=outs[w].at[block, rows(w, True)],
                send_sem=send_sems.at[base + 6 * w + j], recv_sem=recv_sems.at[base + 6 * w + j],
                device_id=(qx, qy, c), device_id_type=MESH)

        def over_d2d(w, j, mine):
            qx, qy = chips[j]
            return pltpu.make_async_remote_copy(
                src_ref=outs[w].at[2 * qx + qy, rows(w, mine)], dst_ref=outs[w].at[2 * qx + qy, rows(w, mine)],
                send_sem=send_sems.at[base + 6 * w + 3 + j], recv_sem=recv_sems.at[base + 6 * w + 3 + j],
                device_id=(x, y, 1 - c), device_id_type=MESH)

        def start():
            for w in range(nw):
                for j in rel:
                    over_ici(w, j, p).start()

        def finish():
            for w in range(nw):
                for j in rel:
                    qx, qy = chips[j]
                    over_ici(w, j, 2 * qx + qy).wait_recv()
                    over_d2d(w, j, True).start()
            for w in range(nw):
                for j in rel:
                    over_d2d(w, j, False).wait_recv()
            for w in range(nw):
                for j in rel:
                    over_ici(w, j, p).wait_send()
                    over_d2d(w, j, True).wait_send()

        return start, finish

    return _Job(fulls, [jax.ShapeDtypeStruct(f.shape, f.dtype) for f in fulls],
                [] if fresh else [(w, w) for w in range(nw)], 6 * nw, make)


def _job_simple(ins, outs, n_per, copies_of):
    nw = len(ins)

    def make(in_refs, out_refs, send_sems, recv_sems, base=0):
        x, y, c = _axes()
        copies = []
        for w in range(nw):
            for j, (src, dst, dev) in enumerate(copies_of(w, in_refs[w], out_refs[w], x, y, c)):
                k = base + n_per * w + j
                copies.append(pltpu.make_async_remote_copy(
                    src_ref=src, dst_ref=dst, send_sem=send_sems.at[k], recv_sem=recv_sems.at[k],
                    device_id=dev, device_id_type=MESH))

        def start():
            for cp in copies:
                cp.start()

        def finish():
            for cp in copies:
                cp.wait()

        return start, finish

    return _Job(ins, outs, [], n_per * nw, make)


def _job_swap_halves(grads):
    def copies_of(w, src, dst, x, y, c):
        kh = src.shape[1] // 2
        return [(src.at[:, pl.ds((1 - c) * kh, kh), :], dst, (x, y, 1 - c))]

    outs = [jax.ShapeDtypeStruct((g.shape[0], g.shape[1] // 2, g.shape[2]), g.dtype) for g in grads]
    return _job_simple(grads, outs, 1, copies_of)


def _job_scatter_blocks(sums):
    def copies_of(w, src, dst, x, y, c):
        return [(src.at[2 * qx + qy], dst.at[j], (qx, qy, c)) for j, (qx, qy) in enumerate(_other_chips(x, y))]

    outs = [jax.ShapeDtypeStruct((3,) + s.shape[1:], s.dtype) for s in sums]
    return _job_simple(sums, outs, 3, copies_of)


def _job_to_sibling(arrs):
    def copies_of(w, src, dst, x, y, c):
        return [(src, dst, (x, y, 1 - c))]

    outs = [jax.ShapeDtypeStruct(a.shape, a.dtype) for a in arrs]
    return _job_simple(arrs, outs, 1, copies_of)


def _run_job(job, name):
    ni, no = len(job.ins), len(job.outs)

    def body(*refs):
        start, finish = job.make(refs[:ni], refs[ni:ni + no], refs[-2], refs[-1])
        start()
        finish()

    any_spec = pl.BlockSpec(memory_space=pl.ANY)
    return pl.pallas_call(
        body, name=name, out_shape=job.outs, in_specs=[any_spec] * ni, out_specs=[any_spec] * no,
        input_output_aliases=dict(job.aliases),
        scratch_shapes=[pltpu.SemaphoreType.DMA((job.n_sems,)), pltpu.SemaphoreType.DMA((job.n_sems,))],
    )(*job.ins)


def _call(body, name, grid, in_specs, out_specs, out_shape, scratch_shapes, semantics, args, aliases=None, job=None,
          prefetch=()):
    in_specs, out_specs, out_shape = list(in_specs), list(out_specs), list(out_shape)
    scratch_shapes = list(scratch_shapes)
    n_pre, n_in, n_out, n_scr = len(prefetch), len(args), len(out_shape), len(scratch_shapes)
    all_aliases = {n_pre + a: b for a, b in (aliases or {}).items()}
    if job is None:
        wrapped, extra_in, semantics = body, [], semantics
    else:
        ni, no = len(job.ins), len(job.outs)

        def wrapped(*refs):
            pre, refs = refs[:n_pre], refs[n_pre:]
            ins, cins = refs[:n_in], refs[n_in:n_in + ni]
            outs, couts = refs[n_in + ni:n_in + ni + n_out], refs[n_in + ni + n_out:n_in + ni + n_out + no]
            scr = refs[n_in + ni + n_out + no:n_in + ni + n_out + no + n_scr]
            start, finish = job.make(cins, couts, refs[-2], refs[-1])
            first = functools.reduce(jnp.logical_and, [pl.program_id(a) == 0 for a in range(len(grid))])
            last = functools.reduce(jnp.logical_and, [pl.program_id(a) == grid[a] - 1 for a in range(len(grid))])
            pl.when(first)(start)
            body(*pre, *ins, *outs, *scr)
            pl.when(last)(finish)

        any_spec = pl.BlockSpec(memory_space=pl.ANY)
        for a, b in job.aliases:
            all_aliases[n_pre + n_in + a] = n_out + b
        in_specs, out_specs, out_shape = in_specs + [any_spec] * ni, out_specs + [any_spec] * no, out_shape + job.outs
        scratch_shapes = scratch_shapes + [pltpu.SemaphoreType.DMA((job.n_sems,)), pltpu.SemaphoreType.DMA((job.n_sems,))]
        extra_in, semantics = job.ins, tuple("arbitrary" for _ in grid)
    if n_pre:
        return pl.pallas_call(
            wrapped, name=name,
            grid_spec=pltpu.PrefetchScalarGridSpec(num_scalar_prefetch=n_pre, grid=grid, in_specs=in_specs,
                                                   out_specs=out_specs, scratch_shapes=scratch_shapes),
            out_shape=out_shape, input_output_aliases=all_aliases,
            compiler_params=_params(semantics))(*prefetch, *args, *extra_in)
    return pl.pallas_call(
        wrapped, name=name, grid=grid, in_specs=in_specs, out_specs=out_specs, out_shape=out_shape,
        scratch_shapes=scratch_shapes, input_output_aliases=all_aliases,
        compiler_params=_params(semantics))(*args, *extra_in)


def _add_own_half(g, r, c_idx, name):
    nb, K, n = g.shape
    kh = K // 2
    tr = _pick(kh, max(SUBLANES, (1 << 19) // n), 2 * SUBLANES)
    per = kh // tr

    def body(c_ref, g_ref, r_ref, o_ref):
        o_ref[...] = (g_ref[...] + r_ref[...]).astype(BF16)

    return pl.pallas_call(
        body, name=name,
        grid_spec=pltpu.PrefetchScalarGridSpec(
            num_scalar_prefetch=1, grid=(nb, per),
            in_specs=[pl.BlockSpec((None, tr, n), lambda b, i, c: (b, c[0] * per + i, 0)),
                      pl.BlockSpec((None, tr, n), lambda b, i, c: (b, i, 0))],
            out_specs=pl.BlockSpec((None, tr, n), lambda b, i, c: (b, i, 0))),
        out_shape=jax.ShapeDtypeStruct((nb, kh, n), BF16),
        compiler_params=_params(("parallel", "parallel")),
    )(c_idx, g, r)


def _add_chips(g, r1, r2, pc_idx, name):
    _, K, n = g.shape
    kh = K // 2
    tr = _pick(kh, max(SUBLANES, (1 << 19) // n), 2 * SUBLANES)
    per = kh // tr

    def body(pc_ref, g_ref, r1_ref, r2_ref, o_ref):
        own = g_ref[...] + r1_ref[...]
        o_ref[...] = ((own + r2_ref[0].astype(F32)) + r2_ref[1].astype(F32)) + r2_ref[2].astype(F32)

    return pl.pallas_call(
        body, name=name,
        grid_spec=pltpu.PrefetchScalarGridSpec(
            num_scalar_prefetch=1, grid=(per,),
            in_specs=[pl.BlockSpec((None, tr, n), lambda i, pc: (pc[0], pc[1] * per + i, 0)),
                      pl.BlockSpec((None, tr, n), lambda i, pc: (pc[0], i, 0)),
                      pl.BlockSpec((3, tr, n), lambda i, pc: (0, i, 0))],
            out_specs=pl.BlockSpec((tr, n), lambda i, pc: (i, 0))),
        out_shape=jax.ShapeDtypeStruct((kh, n), F32),
        compiler_params=_params(("parallel",)),
    )(pc_idx, g, r1, r2)


def _sum8(g, name):
    _, R, L = g.shape

    def body(g_ref, o_ref):
        acc = g_ref[0]
        for d in range(1, N_DEV):
            acc = acc + g_ref[d]
        o_ref[...] = acc

    return pl.pallas_call(
        body, name=name, out_shape=jax.ShapeDtypeStruct((R, L), F32),
        in_specs=[pl.BlockSpec(memory_space=pltpu.VMEM)], out_specs=pl.BlockSpec(memory_space=pltpu.VMEM),
        compiler_params=_params(),
    )(g)


def _adamw_math(w, gg, m, v):
    nm = ADAM_B1 * m + (1.0 - ADAM_B1) * gg
    nv = ADAM_B2 * v + (1.0 - ADAM_B2) * jnp.square(gg)
    m_hat = nm / (1.0 - ADAM_B1 ** ADAM_STEP)
    v_hat = nv / (1.0 - ADAM_B2 ** ADAM_STEP)
    return -ADAM_LR * (m_hat / (jnp.sqrt(v_hat) + ADAM_EPS) + ADAM_WD * w), nm, nv


def _adamw(w, g, m, v, name):
    R, C = w.shape
    tr = _pick(R, max(SUBLANES, (1 << 18) // C), SUBLANES)

    def body(w_ref, g_ref, m_ref, v_ref, d_ref, nm_ref, nv_ref):
        d_ref[...], nm_ref[...], nv_ref[...] = _adamw_math(w_ref[...], g_ref[...], m_ref[...], v_ref[...])

    spec = pl.BlockSpec((tr, C), lambda i: (i, 0))
    sd = jax.ShapeDtypeStruct((R, C), F32)
    return pl.pallas_call(
        body, name=name, grid=(R // tr,), in_specs=[spec] * 4, out_specs=[spec] * 3, out_shape=[sd] * 3,
        compiler_params=_params(("parallel",)),
    )(w, g, m, v)


def _adamw_halves(w, g_own, g_sib, m, v, c_idx, name):
    K, n = w.shape
    kh = K // 2
    tr = _pick(kh, max(SUBLANES, (1 << 18) // n), SUBLANES)
    per = kh // tr

    def body(c_ref, w_ref, go_ref, gs_ref, m_ref, v_ref, g_ref, d_ref, nm_ref, nv_ref):
        h = pl.program_id(0)

        def step(gg):
            g_ref[...] = gg
            d_ref[...], nm_ref[...], nv_ref[...] = _adamw_math(w_ref[...], gg, m_ref[...], v_ref[...])

        @pl.when(h == 0)
        def _():
            step(go_ref[...])

        @pl.when(h == 1)
        def _():
            step(gs_ref[...])

    full = pl.BlockSpec((tr, n), lambda h, i, c: (((c[0] + h) % 2) * per + i, 0))
    own = pl.BlockSpec((tr, n), lambda h, i, c: (i * (1 - h), 0))
    sib = pl.BlockSpec((tr, n), lambda h, i, c: (i * h, 0))
    sd = jax.ShapeDtypeStruct((K, n), F32)
    return pl.pallas_call(
        body, name=name,
        grid_spec=pltpu.PrefetchScalarGridSpec(
            num_scalar_prefetch=1, grid=(2, per),
            in_specs=[full, own, sib, full, full], out_specs=[full] * 4),
        out_shape=[sd] * 4,
        compiler_params=_params(("arbitrary", "arbitrary")),
    )(c_idx, w, g_own, g_sib, m, v)


def _matmul(name, grid, dims, a, a_spec, b, b_spec, outs, out_specs, acc_shape,
            extras=(), extra_specs=(), epilogue=None, job=None, fill_into=None, prefetch=()):
    nk = grid[2]
    npre = len(prefetch)
    aliases = None
    if fill_into is not None:
        aliases = {2 + len(extras): 0}
        extras = tuple(extras) + (fill_into,)
        extra_specs = tuple(extra_specs) + (pl.BlockSpec(memory_space=pl.ANY),)
    nex = len(extras)
    nout = len(outs)

    def body(*refs):
        a_ref, b_ref, rest = refs[npre], refs[npre + 1], refs[npre + 2:]
        ex, o = rest[:nex], rest[nex:nex + nout]
        part = lax.dot_general(a_ref[...].astype(BF16), b_ref[...].astype(BF16), dims,
                               preferred_element_type=F32)

        def finish(res):
            if epilogue is None:
                o[0][...] = res.astype(o[0].dtype)
            else:
                epilogue(res, ex, o)

        if nk == 1:
            finish(part)
        else:
            acc = o[0] if in_place else rest[-1]
            k = pl.program_id(2)

            @pl.when(k == 0)
            def _():
                acc[...] = part

            @pl.when(k > 0)
            def _():
                acc[...] += part

            if not in_place:
                @pl.when(k == nk - 1)
                def _():
                    finish(acc[...])

    in_place = epilogue is None and nout == 1 and outs[0].dtype == F32
    res = _call(body, name, grid, [a_spec, b_spec] + list(extra_specs), out_specs, outs,
                [] if nk == 1 or in_place else [pltpu.VMEM(acc_shape, F32)], ("parallel", "parallel", "arbitrary"),
                (a, b, *extras), aliases=aliases, job=job, prefetch=prefetch)
    return res if job is None else (res[:nout], res[nout:])


def _first(res, job):
    return res[0] if job is None else (res[0][0], res[1])


def _mm_fwd(name, a, wfull, out_dtype=F32, planes=1, bias=None, job=None):
    S, K = a.shape
    _, _, n = wfull.shape
    N = N_CHIPS * n
    tm = _pick(S, 1024, LANES)
    tn = _pick(n, 1408, LANES)
    per = n // tn
    nj = N // tn
    pj = nj // planes
    grid = (S // tm, nj, 1)
    a_spec = pl.BlockSpec((tm, K), lambda i, j, k: (i, 0))
    b_spec = pl.BlockSpec((None, K, tn), lambda i, j, k: (j // per, 0, j % per))
    o_spec = pl.BlockSpec((None, tm, tn), lambda i, j, k: (j // pj, i, j % pj))
    sd = jax.ShapeDtypeStruct((planes, S, N // planes), out_dtype)
    if bias is not None:
        def epi(res, ex, o):
            o[0][...] = (res + ex[0][...]).astype(o[0].dtype)

        out = _matmul(name, grid, NN, a, a_spec, wfull, b_spec, [sd], [o_spec], (tm, tn), extras=(bias,),
                      extra_specs=(pl.BlockSpec((1, tn), lambda i, j, k: (0, j)),), epilogue=epi, job=job)
    else:
        out = _matmul(name, grid, NN, a, a_spec, wfull, b_spec, [sd], [o_spec], (tm, tn), job=job)
    return _first(out, job)


def _mm_fwd_block(name, a, wfull, bias, p_idx, mask, into=None, job=None):
    S, K = a.shape
    _, _, n = wfull.shape
    N = N_CHIPS * n
    tm = _pick(S, 1024, LANES)
    tn = _pick(n, 1408, LANES)
    per = n // tn
    grid = (S // tm, per, 1)
    a_spec = pl.BlockSpec((tm, K), lambda i, j, k, p: (i, 0))
    b_spec = pl.BlockSpec((None, K, tn), lambda i, j, k, p: (jnp.bitwise_xor(p[0], mask), 0, j))
    o_spec = pl.BlockSpec((tm, tn), lambda i, j, k, p: (i, jnp.bitwise_xor(p[0], mask) * per + j))
    v_spec = pl.BlockSpec((1, tn), lambda i, j, k, p: (0, jnp.bitwise_xor(p[0], mask) * per + j))

    def epi(res, ex, o):
        o[0][...] = res + ex[0][...]

    return _matmul(name, grid, NN, a, a_spec, wfull, b_spec, [jax.ShapeDtypeStruct((S, N), F32)], [o_spec], (tm, tn),
                   extras=(bias,), extra_specs=(v_spec,), epilogue=epi, job=job, fill_into=into, prefetch=(p_idx,))


def _copy_block(dst, src, p_idx, mask, name):
    _, K, n = dst.shape
    tr = _pick(K, max(SUBLANES, (1 << 19) // n), 2 * SUBLANES)

    def body(p_ref, s_ref, d_any, o_ref):
        o_ref[...] = s_ref[...]

    spec = pl.BlockSpec((None, tr, n), lambda i, p: (jnp.bitwise_xor(p[0], mask), i, 0))
    return _call(body, name, (K // tr,), [spec, pl.BlockSpec(memory_space=pl.ANY)], [spec],
                 [jax.ShapeDtypeStruct(dst.shape, dst.dtype)], [], ("parallel",), (src, dst), aliases={1: 0},
                 prefetch=(p_idx,))[0]


def _mm_fwd_rows(name, a, wrows, resid, gate, job=None):
    S, K = a.shape
    _, N = wrows.shape
    tm = _pick(S, 512, LANES)
    tn = _pick(N, 1024, LANES)
    tk = _pick(K, 2816, LANES)
    grid = (S // tm, N // tn, K // tk)
    a_spec = pl.BlockSpec((tm, tk), lambda i, j, k: (i, k))
    b_spec = pl.BlockSpec((tk, tn), lambda i, j, k: (k, j))
    o_spec = pl.BlockSpec((tm, tn), lambda i, j, k: (i, j))
    g_spec = pl.BlockSpec((1, tn), lambda i, j, k: (0, j))

    def epi(res, ex, o):
        o[0][...] = res
        o[1][...] = ex[0][...] + ex[1][...] * res

    sd = jax.ShapeDtypeStruct((S, N), F32)
    return _matmul(name, grid, NN, a, a_spec, wrows, b_spec, [sd, sd], [o_spec, o_spec], (tm, tn),
                   extras=(resid, gate), extra_specs=(o_spec, g_spec), epilogue=epi, job=job)


def _mm_dgrad_cols(name, dy, wfull, out_dtype=F32, job=None, row_tiles=None, fill_into=None):
    planes, S, npl = dy.shape
    _, K, n = wfull.shape
    tm = _pick(S if row_tiles is None else S // 2, 1024, LANES)
    to = _pick(K, 1024, LANES)
    tk = _pick(n, 2816, LANES)
    per = n // tk
    nk = N_CHIPS * per
    pk = nk // planes
    n_tiles = S // tm
    i0, ni = (0, n_tiles) if row_tiles is None else (int(row_tiles[0] * n_tiles), int(row_tiles[1] * n_tiles))
    grid = (ni, K // to, nk)
    a_spec = pl.BlockSpec((None, tm, tk), lambda i, j, k: (k // pk, i + i0, k % pk))
    b_spec = pl.BlockSpec((None, to, tk), lambda i, j, k: (k // per, j, k % per))
    o_spec = pl.BlockSpec((tm, to), lambda i, j, k: (i + i0, j))
    return _first(_matmul(name, grid, NT, dy, a_spec, wfull, b_spec, [jax.ShapeDtypeStruct((S, K), out_dtype)],
                          [o_spec], (tm, to), job=job, fill_into=fill_into), job)


def _mm_dgrad_rows(name, dy, wrows, out_dtype=F32, job=None):
    S, N = dy.shape
    K, _ = wrows.shape
    tm = _pick(S, 1024, LANES)
    to = _pick(K, 1408, LANES)
    tk = _pick(N, 2048, LANES)
    grid = (S // tm, K // to, N // tk)
    a_spec = pl.BlockSpec((tm, tk), lambda i, j, k: (i, k))
    b_spec = pl.BlockSpec((to, tk), lambda i, j, k: (j, k))
    o_spec = pl.BlockSpec((tm, to), lambda i, j, k: (i, j))
    return _first(_matmul(name, grid, NT, dy, a_spec, wrows, b_spec, [jax.ShapeDtypeStruct((S, K), out_dtype)],
                          [o_spec], (tm, to), job=job), job)


def _mm_wgrad_cols(name, a, dy, n, job=None):
    S, K = a.shape
    planes, _, npl = dy.shape
    N = planes * npl
    to = _pick(K, 1024, LANES)
    tn = _pick(n, 1408, LANES)
    ts = _pick(S, 2048, LANES)
    per = n // tn
    nj = N // tn
    pj = nj // planes
    grid = (K // to, nj, S // ts)
    a_spec = pl.BlockSpec((ts, to), lambda i, j, k: (k, i))
    b_spec = pl.BlockSpec((None, ts, tn), lambda i, j, k: (j // pj, k, j % pj))
    o_spec = pl.BlockSpec((None, to, tn), lambda i, j, k: (j // per, i, j % per))
    return _first(_matmul(name, grid, TN, a, a_spec, dy, b_spec, [jax.ShapeDtypeStruct((N_CHIPS, K, n), F32)],
                          [o_spec], (to, tn), job=job), job)


def _mm_wgrad_rows(name, a, dy, kshard, job=None):
    S, K = a.shape
    _, N = dy.shape
    to = _pick(kshard, 1408, LANES)
    tn = _pick(N, 1024, LANES)
    ts = _pick(S, 2048, LANES)
    per = kshard // to
    grid = (K // to, N // tn, S // ts)
    a_spec = pl.BlockSpec((ts, to), lambda i, j, k: (k, i))
    b_spec = pl.BlockSpec((ts, tn), lambda i, j, k: (k, j))
    o_spec = pl.BlockSpec((None, to, tn), lambda i, j, k: (i // per, i % per, j))
    return _first(_matmul(name, grid, TN, a, a_spec, dy, b_spec, [jax.ShapeDtypeStruct((N_CHIPS, kshard, N), F32)],
                          [o_spec], (to, tn), job=job), job)


def _rows(tm, width, colblk=0):
    return pl.BlockSpec((tm, width), lambda i: (i, colblk))


def _whole(shape):
    nd = len(shape)
    return pl.BlockSpec(shape, lambda i: (0,) * nd)


def _prev_halo(tm, h, width, colblk=0):
    r = tm // h
    return pl.BlockSpec((h, width), lambda i: (jnp.maximum(i * r - 1, 0), colblk))


def _next_halo(tm, h, width, nblk, colblk=0):
    r = tm // h
    return pl.BlockSpec((h, width), lambda i: (jnp.minimum((i + 1) * r, nblk - 1), colblk))


def _accumulate(i, ref, val):
    @pl.when(i == 0)
    def _():
        ref[...] = val

    @pl.when(i > 0)
    def _():
        ref[...] += val


def _fill_rotations(rot, offs):
    n = rot.shape[1]
    for r in sorted({o % SUBLANES for o in offs} - {0}):
        rot[r, 0:n - SUBLANES, :] = rot[0, r:r + n - SUBLANES, :]


def _for_chunks(n_rows, rb, fn):
    def step(j, carry):
        fn(pl.multiple_of(j * rb, rb))
        return carry

    lax.fori_loop(0, n_rows // rb, step, 0)


def _modnorm_fwd(name, x, g, scale, shift):
    S, D = x.shape
    tm = _pick(S, 512, LANES)

    def body(x_ref, g_ref, sc_ref, sh_ref, h_ref):
        h_ref[...] = _modnorm(x_ref[...], g_ref[...], sc_ref[...], sh_ref[...]).astype(BF16)

    vec = _whole((1, D))
    return pl.pallas_call(
        body, name=name, grid=(S // tm,), in_specs=[_rows(tm, D), vec, vec, vec], out_specs=_rows(tm, D),
        out_shape=jax.ShapeDtypeStruct((S, D), BF16), compiler_params=_params(("parallel",)),
    )(x, g, scale, shift)


def _modnorm_bwd(name, x, dh, dx_in, g, scale, shift, gated=None):
    S, D = x.shape
    tm = _pick(S, 256, LANES)

    def body(x_ref, dh_ref, dxin_ref, g_ref, sc_ref, sh_ref, *rest):
        i = pl.program_id(0)
        dx_ref, dg_ref, dsc_ref, dsh_ref = rest[-4:] if gated is None else rest[2:6]
        _, pull = jax.vjp(_modnorm, x_ref[...], g_ref[...], sc_ref[...], sh_ref[...])
        dx, dg, dsc, dsh = pull(dh_ref[...])
        dx = dxin_ref[...] + dx
        dx_ref[...] = dx
        _accumulate(i, dg_ref, dg)
        _accumulate(i, dsc_ref, dsc)
        _accumulate(i, dsh_ref, dsh)
        if gated is not None:
            _gate_bwd_tile(i, dx, rest[0], rest[1], rest[6], rest[7])

    vec = _whole((1, D))
    row = _rows(tm, D)
    vsd = jax.ShapeDtypeStruct((1, D), F32)
    in_specs, args = [row, row, row, vec, vec, vec], (x, dh, dx_in, g, scale, shift)
    out_specs, out_shape = [row, vec, vec, vec], [jax.ShapeDtypeStruct((S, D), F32), vsd, vsd, vsd]
    if gated is not None:
        in_specs, args = in_specs + [row, vec], args + tuple(gated)
        out_specs, out_shape = out_specs + [row, vec], out_shape + [jax.ShapeDtypeStruct((S, D), BF16), vsd]
    return _call(body, name, (S // tm,), in_specs, out_specs, out_shape, [], ("arbitrary",), args)


def _conv_fwd(name, proj, w, b, lg, lb, dc, job=None):
    S = proj.shape[0]
    K = w.shape[0]
    H = CONV_HALO
    tm = _pick(S, 256, LANES)

    def glu(v):
        return v[:, :dc] * jax.nn.sigmoid(v[:, dc:])

    offs = [H - (K - 1) + k for k in range(K)]
    rb = _pick(tm, 64, SUBLANES)
    lw = min(LANES, dc)

    def body(cur_ref, prev_ref, w_ref, b_ref, lg_ref, lb_ref, conv_ref, act_ref, rot):
        i = pl.program_id(0)
        rot[0, 0:H, :] = jnp.where(i > 0, glu(prev_ref[...]), 0.0)
        rot[0, H:, :] = glu(cur_ref[...])
        _fill_rotations(rot, offs)

        def chunk(row0):
            for l0 in range(0, dc, lw):
                acc = jnp.broadcast_to(b_ref[:, l0:l0 + lw], (rb, lw))
                for k in range(K):
                    a, r = divmod(offs[k], SUBLANES)
                    acc = acc + w_ref[k:k + 1, l0:l0 + lw] * rot[r, pl.ds(row0 + SUBLANES * a, rb), l0:l0 + lw]
                conv_ref[pl.ds(row0, rb), l0:l0 + lw] = acc

        _for_chunks(tm, rb, chunk)
        act_ref[...] = _ln_silu(conv_ref[...], lg_ref[...], lb_ref[...]).astype(BF16)

    vec = _whole((1, dc))
    res = _call(body, name, (S // tm,),
                [_rows(tm, 2 * dc), _prev_halo(tm, H, 2 * dc), _whole(w.shape), vec, vec, vec],
                [_rows(tm, dc), _rows(tm, dc)],
                [jax.ShapeDtypeStruct((S, dc), F32), jax.ShapeDtypeStruct((S, dc), BF16)],
                [pltpu.VMEM((SUBLANES, tm + H, dc), F32)], ("parallel",), (proj, proj, w, b, lg, lb), job=job)
    return res if job is None else (res[:2], res[2:])


def _conv_bwd_ln(name, conv, dact, lg, lb):
    S, dc = conv.shape
    tm = _pick(S, 256, LANES)

    def body(c_ref, d_ref, lg_ref, lb_ref, dc_ref, dlg_ref, dlb_ref, db_ref):
        i = pl.program_id(0)
        _, pull = jax.vjp(_ln_silu, c_ref[...], lg_ref[...], lb_ref[...])
        dcv, dlg, dlb = pull(d_ref[...])
        dc_ref[...] = dcv
        _accumulate(i, dlg_ref, dlg)
        _accumulate(i, dlb_ref, dlb)
        _accumulate(i, db_ref, jnp.sum(dcv, axis=0, keepdims=True))

    vec = _whole((1, dc))
    row = _rows(tm, dc)
    vsd = jax.ShapeDtypeStruct((1, dc), F32)
    return pl.pallas_call(
        body, name=name, grid=(S // tm,), in_specs=[row, row, vec, vec], out_specs=[row, vec, vec, vec],
        out_shape=[jax.ShapeDtypeStruct((S, dc), F32), vsd, vsd, vsd],
        compiler_params=_params(("arbitrary",)),
    )(conv, dact, lg, lb)


def _conv_bwd(name, dconv, proj, w, dproj, dc, job=None):
    S = proj.shape[0]
    K = w.shape[0]
    H = CONV_HALO
    tm = _pick(S, 256, LANES)
    nt = S // tm

    offs_g = [H - (K - 1) + k for k in range(K)]
    offs_d = [K - 1 - k for k in range(K)]
    rb = _pick(tm, 64, SUBLANES)
    lw = min(LANES, dc)
    kp = -(-K // SUBLANES) * SUBLANES

    def body(d_ref, dn_ref, cur_ref, prev_ref, w_ref, dproj_any, da_ref, dw_ref, dbin_ref, rotg, rotd, accw, dglu_s):
        i = pl.program_id(0)
        pv = prev_ref[...]
        cv = cur_ref[...]
        sig = jax.nn.sigmoid(cv[:, dc:])
        rotg[0, 0:H, :] = jnp.where(i > 0, pv[:, :dc] * jax.nn.sigmoid(pv[:, dc:]), 0.0)
        rotg[0, H:, :] = cv[:, :dc] * sig
        rotd[0, 0:tm, :] = d_ref[...]
        rotd[0, tm:, :] = jnp.where(i < nt - 1, dn_ref[...], 0.0)
        _fill_rotations(rotg, offs_g)
        _fill_rotations(rotd, offs_d)

        @pl.when(i == 0)
        def _():
            accw[...] = jnp.zeros_like(accw)

        def chunk(row0):
            for l0 in range(0, dc, lw):
                lanes = slice(l0, l0 + lw)
                dcur = rotd[0, pl.ds(row0, rb), lanes]
                acc = jnp.zeros((rb, lw), F32)
                for k in range(K):
                    a, r = divmod(offs_d[k], SUBLANES)
                    acc = acc + w_ref[k:k + 1, lanes] * rotd[r, pl.ds(row0 + SUBLANES * a, rb), lanes]
                    a, r = divmod(offs_g[k], SUBLANES)
                    prod = dcur * rotg[r, pl.ds(row0 + SUBLANES * a, rb), lanes]
                    accw[SUBLANES * k:SUBLANES * (k + 1), lanes] += jnp.sum(
                        prod.reshape(rb // SUBLANES, SUBLANES, lw), axis=0)
                dglu_s[pl.ds(row0, rb), lanes] = acc

        _for_chunks(tm, rb, chunk)
        dglu = dglu_s[...]
        da = jnp.concatenate([dglu * sig, dglu * cv[:, :dc] * sig * (1.0 - sig)], axis=1)
        da_ref[...] = da.astype(BF16)
        _accumulate(i, dbin_ref, jnp.sum(da, axis=0, keepdims=True))

        @pl.when(i == nt - 1)
        def _():
            dw_ref[...] = jnp.zeros_like(dw_ref)
            for k in range(K):
                dw_ref[k:k + 1, :] = jnp.sum(accw[SUBLANES * k:SUBLANES * (k + 1), :], axis=0, keepdims=True)

    res = _call(
        body, name, (nt,),
        [_rows(tm, dc), _next_halo(tm, H, dc, S // H), _rows(tm, 2 * dc), _prev_halo(tm, H, 2 * dc),
         _whole(w.shape), pl.BlockSpec(memory_space=pl.ANY)],
        [_rows(tm, 2 * dc), _whole((kp, dc)), _whole((1, 2 * dc))],
        [jax.ShapeDtypeStruct(dproj.shape, BF16), jax.ShapeDtypeStruct((kp, dc), F32),
         jax.ShapeDtypeStruct((1, 2 * dc), F32)],
        [pltpu.VMEM((SUBLANES, tm + H, dc), F32), pltpu.VMEM((SUBLANES, tm + H, dc), F32),
         pltpu.VMEM((SUBLANES * K, dc), F32), pltpu.VMEM((tm, dc), F32)],
        ("arbitrary",), (dconv, dconv, proj, proj, w, dproj), aliases={5: 0}, job=job)
    return res if job is None else (res[:3], res[3:])


def _mix(vln, wsm, bst, out_ref, G, CH, hd):
    for n in range(vln.shape[0] // CH):
        for g in range(G):
            blk = vln[n * CH:(n + 1) * CH, g * hd:(g + 1) * hd].astype(BF16)
            out_ref[n * CH:(n + 1) * CH, g * hd:(g + 1) * hd] = (
                jnp.dot(wsm[g], blk, preferred_element_type=F32) + bst[:, g:g + 1])


def _sgu_fwd(name, proj, a_act, wco, wso, lg, lb, ws, bst, D, ds, job=None):
    S = proj.shape[0]
    dc = a_act.shape[1]
    G, CH, _ = ws.shape
    hd = ds // G
    nco = wco.shape[2]
    tm = _pick(S, 256, CH)

    def body(s_ref, gt_ref, a_ref, wco_ref, wso_ref, lg_ref, lb_ref, ws_ref, bst_ref,
             ya_ref, yb_ref, uv_ref, mg_ref, vmix):
        z = _gelu(s_ref[...])
        vln = _layer_norm(z[:, ds:], lg_ref[...], lb_ref[...])
        wsm = [_tril_mask(ws_ref[g]).astype(BF16) for g in range(G)]
        _mix(vln, wsm, bst_ref[...], vmix, G, CH, hd)
        uv = (z[:, :ds] * vmix[...]).astype(BF16)
        uv_ref[...] = uv
        a = a_ref[...]
        for p in range(N_CHIPS):
            ya_ref[:, p * nco:(p + 1) * nco] = jnp.dot(a, wco_ref[p], preferred_element_type=F32)
            yb_ref[:, p * nco:(p + 1) * nco] = jnp.dot(uv, wso_ref[p], preferred_element_type=F32)
        gt = gt_ref[...]
        mg_ref[...] = (jax.nn.sigmoid(gt[:, :D]) * ya_ref[...] + jax.nn.sigmoid(gt[:, D:]) * yb_ref[...]).astype(BF16)

    vec = _whole((1, ds))
    sdf = jax.ShapeDtypeStruct((S, D), F32)
    res = _call(body, name, (S // tm,),
                [_rows(tm, 2 * ds, 1), _rows(tm, 2 * D, 1), _rows(tm, dc), _whole(wco.shape), _whole(wso.shape),
                 vec, vec, _whole(ws.shape), _whole(bst.shape)],
                [_rows(tm, D), _rows(tm, D), _rows(tm, ds), _rows(tm, D)],
                [sdf, sdf, jax.ShapeDtypeStruct((S, ds), BF16), jax.ShapeDtypeStruct((S, D), BF16)],
                [pltpu.VMEM((tm, ds), F32)], ("parallel",), (proj, proj, a_act, wco, wso, lg, lb, ws, bst), job=job)
    return res if job is None else (res[:4], res[4:])


def _merge_bwd(name, dmerged, proj, ya, yb, D):
    S = proj.shape[0]
    tm = _pick(S, 256, LANES)

    def body(dm_ref, gt_ref, ya_ref, yb_ref, dya_ref, dyb_ref, dg_ref, dbin_ref):
        i = pl.program_id(0)
        dm = dm_ref[...]
        gt = gt_ref[...]
        sa = jax.nn.sigmoid(gt[:, :D])
        sb = jax.nn.sigmoid(gt[:, D:])
        dya_ref[...] = (dm * sa).astype(BF16)
        dyb_ref[...] = (dm * sb).astype(BF16)
        dg = jnp.concatenate([dm * ya_ref[...] * sa * (1.0 - sa), dm * yb_ref[...] * sb * (1.0 - sb)], axis=1)
        dg_ref[...] = dg.astype(BF16)
        _accumulate(i, dbin_ref, jnp.sum(dg, axis=0, keepdims=True))

    row = _rows(tm, D)
    sdb = jax.ShapeDtypeStruct((S, D), BF16)
    return pl.pallas_call(
        body, name=name, grid=(S // tm,),
        in_specs=[row, _rows(tm, 2 * D, 1), row, row],
        out_specs=[row, row, _rows(tm, 2 * D, 1), _whole((1, 2 * D))],
        out_shape=[sdb, sdb, jax.ShapeDtypeStruct((S, 4 * D), BF16), jax.ShapeDtypeStruct((1, 2 * D), F32)],
        compiler_params=_params(("arbitrary",)),
    )(dmerged, proj, ya, yb)


def _sgu_bwd(name, proj, duv, lg, lb, ws, bst, dproj, ds):
    S = proj.shape[0]
    G, CH, _ = ws.shape
    hd = ds // G
    tm = _pick(S, 256, CH)

    def body(s_ref, duv_ref, lg_ref, lb_ref, ws_ref, bst_ref, dproj_any,
             dsin_ref, dws_ref, dbs_ref, dlg_ref, dlb_ref, dbin_ref, vmix, dvln):
        i = pl.program_id(0)
        z, pull_gelu = jax.vjp(_gelu, s_ref[...])
        u = z[:, :ds]
        vln, pull_ln = jax.vjp(_layer_norm, z[:, ds:], lg_ref[...], lb_ref[...])
        wsm = [_tril_mask(ws_ref[g]).astype(BF16) for g in range(G)]
        _mix(vln, wsm, bst_ref[...], vmix, G, CH, hd)
        duv = duv_ref[...]
        du = duv * vmix[...]
        dvmix = duv * u
        for g in range(G):
            dws_g = jnp.zeros((CH, CH), F32)
            dbs_g = jnp.zeros((CH, 1), F32)
            for n in range(tm // CH):
                dblk = dvmix[n * CH:(n + 1) * CH, g * hd:(g + 1) * hd]
                vblk = vln[n * CH:(n + 1) * CH, g * hd:(g + 1) * hd].astype(BF16)
                dvln[n * CH:(n + 1) * CH, g * hd:(g + 1) * hd] = lax.dot_general(
                    wsm[g], dblk.astype(BF16), TN, preferred_element_type=F32)
                dws_g = dws_g + lax.dot_general(dblk.astype(BF16), vblk, NT, preferred_element_type=F32)
                dbs_g = dbs_g + jnp.sum(dblk, axis=1, keepdims=True)
            dws_g = _tril_mask(dws_g)
            dbs_g = jnp.broadcast_to(dbs_g, (CH, LANES))

            @pl.when(i == 0)
            def _():
                dws_ref[g] = dws_g
                dbs_ref[g] = dbs_g

            @pl.when(i > 0)
            def _():
                dws_ref[g] += dws_g
                dbs_ref[g] += dbs_g

        dv, dlg, dlb = pull_ln(dvln[...])
        (dsin,) = pull_gelu(jnp.concatenate([du, dv], axis=1))
        dsin_ref[...] = dsin.astype(BF16)
        _accumulate(i, dlg_ref, dlg)
        _accumulate(i, dlb_ref, dlb)
        _accumulate(i, dbin_ref, jnp.sum(dsin, axis=0, keepdims=True))

    vec = _whole((1, ds))
    vsd = jax.ShapeDtypeStruct((1, ds), F32)
    return pl.pallas_call(
        body, name=name, grid=(S // tm,),
        in_specs=[_rows(tm, 2 * ds, 1), _rows(tm, ds), vec, vec, _whole(ws.shape), _whole(bst.shape),
                  pl.BlockSpec(memory_space=pl.ANY)],
        out_specs=[_rows(tm, 2 * ds, 1), _whole((G, CH, CH)), _whole((G, CH, LANES)), vec, vec, _whole((1, 2 * ds))],
        out_shape=[jax.ShapeDtypeStruct(dproj.shape, BF16), jax.ShapeDtypeStruct((G, CH, CH), F32),
                   jax.ShapeDtypeStruct((G, CH, LANES), F32), vsd, vsd, jax.ShapeDtypeStruct((1, 2 * ds), F32)],
        scratch_shapes=[pltpu.VMEM((tm, ds), F32), pltpu.VMEM((tm, ds), F32)],
        input_output_aliases={6: 0},
        compiler_params=_params(("arbitrary",)),
    )(proj, duv, lg, lb, ws, bst, dproj)


def _silu_mul(val, gt):
    return jax.nn.silu(gt) * val


def _rotation_slots(offs):
    slot = {0: 0}
    for r in sorted({o % SUBLANES for o in offs} - {0}):
        slot[r] = len(slot)
    return slot


def _fill_plane_rotations(rot, slot):
    n = rot.shape[2]
    for r, s in slot.items():
        if r:
            rot[:, s, 0:n - SUBLANES, :] = rot[:, 0, r:r + n - SUBLANES, :]


def _ffn_tiles(S, Fh):
    return _pick(S, 256, LANES), _pick(Fh, 1408, LANES)


def _ffn_fwd(name, up, w, b):
    _, S, Fh = up.shape
    K = w.shape[1]
    H = FFN_HALO
    tm, cw = _ffn_tiles(S, Fh)
    r = tm // H

    offs = [H - (K - 1) + k for k in range(K)]
    slot = _rotation_slots(offs)
    rb = _pick(tm, 64, 2 * SUBLANES)
    lw = min(LANES, cw)

    def body(cur_ref, prev_ref, w_ref, b_ref, act_ref, rot):
        i = pl.program_id(1)
        rot[:, 0, 0:H, :] = jnp.where(i > 0, prev_ref[...], 0.0)
        rot[:, 0, H:, :] = cur_ref[...]
        _fill_plane_rotations(rot, slot)

        def chunk(row0):
            for l0 in range(0, cw, lw):
                lanes = slice(l0, l0 + lw)
                c2 = []
                for pln in range(2):
                    acc = jnp.broadcast_to(b_ref[pln, :, lanes], (rb, lw))
                    for k in range(K):
                        a, rr = divmod(offs[k], SUBLANES)
                        acc = acc + w_ref[pln, k:k + 1, lanes] * rot[pln, slot[rr], pl.ds(row0 + SUBLANES * a, rb), lanes]
                    c2.append(acc)
                act_ref[pl.ds(row0, rb), lanes] = _silu_mul(c2[0], c2[1]).astype(BF16)

        _for_chunks(tm, rb, chunk)

    return pl.pallas_call(
        body, name=name, grid=(Fh // cw, S // tm),
        in_specs=[pl.BlockSpec((2, tm, cw), lambda j, i: (0, i, j)),
                  pl.BlockSpec((2, H, cw), lambda j, i: (0, jnp.maximum(i * r - 1, 0), j)),
                  pl.BlockSpec((2, K, cw), lambda j, i: (0, 0, j)),
                  pl.BlockSpec((2, 1, cw), lambda j, i: (0, 0, j))],
        out_specs=pl.BlockSpec((tm, cw), lambda j, i: (i, j)),
        out_shape=jax.ShapeDtypeStruct((S, Fh), BF16),
        scratch_shapes=[pltpu.VMEM((2, len(slot), tm + H, cw), F32)],
        compiler_params=_params(("parallel", "parallel")),
    )(up, up, w, b)


def _ffn_bwd(name, up, dact, w, b, job=None):
    _, S, Fh = up.shape
    K = w.shape[1]
    H = FFN_HALO
    tm, cw = _ffn_tiles(S, Fh)
    r = tm // H
    nt = S // tm
    nhb = S // H
    te = tm + H

    offs_x = [H - (K - 1) + k for k in range(K)]
    offs_d = [K - 1 - k for k in range(K)]
    slot_x = _rotation_slots(offs_x)
    slot_d = _rotation_slots(offs_d)
    rb = _pick(tm, 64, 2 * SUBLANES)
    rbe = _pick(te, 96, SUBLANES)
    lw = min(LANES, cw)

    def body(cur_ref, prev_ref, next_ref, d_ref, dn_ref, w_ref, b_ref, dup_ref, dwb_ref, rotx, dext, rotd, accw):
        i = pl.program_id(1)
        rotx[:, 0, 0:H, :] = jnp.where(i > 0, prev_ref[...], 0.0)
        rotx[:, 0, H:H + tm, :] = cur_ref[...]
        rotx[:, 0, H + tm:, :] = jnp.where(i < nt - 1, next_ref[...], 0.0)
        dext[0:tm, :] = d_ref[...]
        dext[tm:, :] = jnp.where(i < nt - 1, dn_ref[...], 0.0)
        _fill_plane_rotations(rotx, slot_x)

        def chunk_e(row0):
            for l0 in range(0, cw, lw):
                lanes = slice(l0, l0 + lw)
                c2 = []
                for pln in range(2):
                    acc = jnp.broadcast_to(b_ref[pln, :, lanes], (rbe, lw))
                    for k in range(K):
                        a, rr = divmod(offs_x[k], SUBLANES)
                        acc = acc + w_ref[pln, k:k + 1, lanes] * rotx[pln, slot_x[rr], pl.ds(row0 + SUBLANES * a, rbe), lanes]
                    c2.append(acc)
                _, pull = jax.vjp(_silu_mul, c2[0], c2[1])
                dval, dgt = pull(dext[pl.ds(row0, rbe), lanes])
                rotd[0, 0, pl.ds(row0, rbe), lanes] = dval
                rotd[1, 0, pl.ds(row0, rbe), lanes] = dgt

        _for_chunks(te, rbe, chunk_e)
        _fill_plane_rotations(rotd, slot_d)

        @pl.when(i == 0)
        def _():
            accw[...] = jnp.zeros_like(accw)

        def chunk(row0):
            for l0 in range(0, cw, lw):
                lanes = slice(l0, l0 + lw)
                for pln in range(2):
                    dcur = rotd[pln, 0, pl.ds(row0, rb), lanes]
                    dup = jnp.zeros((rb, lw), F32)
                    for k in range(K):
                        a, rr = divmod(offs_d[k], SUBLANES)
                        dup = dup + w_ref[pln, k:k + 1, lanes] * rotd[pln, slot_d[rr], pl.ds(row0 + SUBLANES * a, rb), lanes]
                        a, rr = divmod(offs_x[k], SUBLANES)
                        prod = dcur * rotx[pln, slot_x[rr], pl.ds(row0 + SUBLANES * a, rb), lanes]
                        accw[pln, SUBLANES * k:SUBLANES * (k + 1), lanes] += jnp.sum(
                            prod.reshape(rb // SUBLANES, SUBLANES, lw), axis=0)
                    accw[pln, SUBLANES * K:SUBLANES * (K + 1), lanes] += jnp.sum(
                        dcur.reshape(rb // SUBLANES, SUBLANES, lw), axis=0)
                    dup_ref[pln, pl.ds(row0, rb), lanes] = dup.astype(BF16)

        _for_chunks(tm, rb, chunk)

        @pl.when(i == nt - 1)
        def _():
            dwb_ref[...] = jnp.zeros_like(dwb_ref)
            for pln in range(2):
                for k in range(K + 1):
                    dwb_ref[pln, k:k + 1, :] = jnp.sum(accw[pln, SUBLANES * k:SUBLANES * (k + 1), :], axis=0,
                                                       keepdims=True)

    res = _call(
        body, name, (Fh // cw, nt),
        [pl.BlockSpec((2, tm, cw), lambda j, i: (0, i, j)),
         pl.BlockSpec((2, H, cw), lambda j, i: (0, jnp.maximum(i * r - 1, 0), j)),
         pl.BlockSpec((2, H, cw), lambda j, i: (0, jnp.minimum((i + 1) * r, nhb - 1), j)),
         pl.BlockSpec((tm, cw), lambda j, i: (i, j)),
         pl.BlockSpec((H, cw), lambda j, i: (jnp.minimum((i + 1) * r, nhb - 1), j)),
         pl.BlockSpec((2, K, cw), lambda j, i: (0, 0, j)),
         pl.BlockSpec((2, 1, cw), lambda j, i: (0, 0, j))],
        [pl.BlockSpec((2, tm, cw), lambda j, i: (0, i, j)), pl.BlockSpec((2, SUBLANES, cw), lambda j, i: (0, 0, j))],
        [jax.ShapeDtypeStruct((2, S, Fh), BF16), jax.ShapeDtypeStruct((2, SUBLANES, Fh), F32)],
        [pltpu.VMEM((2, len(slot_x), tm + 2 * H, cw), F32), pltpu.VMEM((te, cw), F32),
         pltpu.VMEM((2, len(slot_d), te, cw), F32), pltpu.VMEM((2, SUBLANES * (K + 1), cw), F32)],
        ("parallel", "arbitrary"), (up, up, up, dact, dact, w, b), job=job)
    return res if job is None else (res[:2], res[2:])


def _rms(x, g):
    return x * lax.rsqrt(jnp.mean(x * x, axis=-1, keepdims=True) + EPS) * g


def _gate_bwd_tile(i, dx, out_ref, gate_ref, dout_ref, dgate_ref):
    dout_ref[...] = (dx * gate_ref[...]).astype(BF16)
    _accumulate(i, dgate_ref, jnp.sum(dx * out_ref[...], axis=0, keepdims=True))


def _final(name, x2, target, gf, out, gate):
    S, D = x2.shape
    tm = _pick(S, 256, LANES)

    def body(x_ref, t_ref, g_ref, o_ref, gate_ref, dx_ref, dg_ref, loss_ref, do_ref, dgate_ref):
        i = pl.program_id(0)
        y, pull = jax.vjp(_rms, x_ref[...], g_ref[...])
        e = y - t_ref[...]
        dx, dg = pull(e / D)
        dx_ref[...] = dx
        _accumulate(i, dg_ref, dg)
        part = 0.5 * jnp.sum(jnp.mean(jnp.square(e), axis=-1, keepdims=True), axis=0, keepdims=True)
        _accumulate(i, loss_ref, jnp.broadcast_to(part, (SUBLANES, LANES)))
        _gate_bwd_tile(i, dx, o_ref, gate_ref, do_ref, dgate_ref)

    row = _rows(tm, D)
    vec = _whole((1, D))
    vsd = jax.ShapeDtypeStruct((1, D), F32)
    return pl.pallas_call(
        body, name=name, grid=(S // tm,), in_specs=[row, row, vec, row, vec],
        out_specs=[row, vec, _whole((SUBLANES, LANES)), row, vec],
        out_shape=[jax.ShapeDtypeStruct((S, D), F32), vsd, jax.ShapeDtypeStruct((SUBLANES, LANES), F32),
                   jax.ShapeDtypeStruct((S, D), BF16), vsd],
        compiler_params=_params(("arbitrary",)),
    )(x2, target, gf, out, gate)


def _ada_fwd(name, c_pad, w_ada, b_cols):
    nb, D = c_pad.shape
    n = w_ada.shape[1]
    tn = _pick(n, 1024, LANES)

    def body(c_ref, w_ref, b_ref, o_ref):
        o_ref[...] = jnp.dot(jax.nn.silu(c_ref[...]).astype(BF16), w_ref[...].astype(BF16),
                             preferred_element_type=F32) + b_ref[...]

    return pl.pallas_call(
        body, name=name, grid=(n // tn,),
        in_specs=[_whole((nb, D)), pl.BlockSpec((D, tn), lambda j: (0, j)), pl.BlockSpec((1, tn), lambda j: (0, j))],
        out_specs=pl.BlockSpec((nb, tn), lambda j: (0, j)),
        out_shape=jax.ShapeDtypeStruct((nb, n), F32), compiler_params=_params(("parallel",)),
    )(c_pad, w_ada, b_cols)


def _ada_wgrad(name, c_t, dmod_cols):
    D, nb = c_t.shape
    n = dmod_cols.shape[1]
    tr = _pick(D, 256, SUBLANES)

    def body(c_ref, d_ref, o_ref):
        ca = jax.nn.silu(c_ref[...])
        acc = ca[:, 0:1] * d_ref[0:1, :]
        for b in range(1, nb):
            acc = acc + ca[:, b:b + 1] * d_ref[b:b + 1, :]
        o_ref[...] = acc

    return pl.pallas_call(
        body, name=name, grid=(D // tr,),
        in_specs=[pl.BlockSpec((tr, nb), lambda i: (i, 0)), _whole((nb, n))],
        out_specs=pl.BlockSpec((tr, n), lambda i: (i, 0)),
        out_shape=jax.ShapeDtypeStruct((D, n), F32), compiler_params=_params(("parallel",)),
    )(c_t, dmod_cols)


def kernel(x, c, w_ada, b_ada, norm1_g, w_in, b_in, conv_dw_w, conv_dw_b, conv_ln_g, conv_ln_b, w_conv_out, sgu_ln_g, sgu_ln_b, w_spatial, b_spatial, w_sgu_out, w_out, norm2_g, w_up, ffn_dw_w, ffn_dw_b, w_down, final_g, loss_target, m_w_ada, m_b_ada, m_norm1_g, m_w_in, m_b_in, m_conv_dw_w, m_conv_dw_b, m_conv_ln_g, m_conv_ln_b, m_w_conv_out, m_sgu_ln_g, m_sgu_ln_b, m_w_spatial, m_b_spatial, m_w_sgu_out, m_w_out, m_norm2_g, m_w_up, m_ffn_dw_w, m_ffn_dw_b, m_w_down, m_final_g, v_w_ada, v_b_ada, v_norm1_g, v_w_in, v_b_in, v_conv_dw_w, v_conv_dw_b, v_conv_ln_g, v_conv_ln_b, v_w_conv_out, v_sgu_ln_g, v_sgu_ln_b, v_w_spatial, v_b_spatial, v_w_sgu_out, v_w_out, v_norm2_g, v_w_up, v_ffn_dw_w, v_ffn_dw_b, v_w_down, v_final_g):
    S, D = x.shape[1], x.shape[2]
    dc = w_conv_out.shape[1]
    ds = w_sgu_out.shape[1]
    G, CH = w_spatial.shape[1], w_spatial.shape[2]
    KC = conv_dw_w.shape[1]
    KF = ffn_dw_w.shape[1]
    F2 = ffn_dw_b.shape[1]
    Fh = F2 // 2
    n_ada = w_ada.shape[2]
    n_up = w_up.shape[2]
    ax, ay, ac = _axes()
    chip = 2 * ax + ay
    me = 2 * chip + ac
    c_idx = jnp.reshape(ac, (1,)).astype(jnp.int32)
    p_idx = jnp.reshape(chip, (1,)).astype(jnp.int32)

    xs = x[0]
    tgt = loss_target[0]

    g1 = _allgather8(_pack([c[0], conv_dw_w[0], ffn_dw_w[0]]), "gather_small_in")
    parts = [_unpack(g1[2 * q], [(D,), conv_dw_w.shape[1:], ffn_dw_w.shape[1:]]) for q in range(N_CHIPS)]
    c_all = jnp.stack([_unpack(g1[d], [(D,)])[0] for d in range(N_DEV)])
    cw_full = jnp.concatenate([pt[1] for pt in parts], axis=1)
    fw_full = jnp.concatenate([pt[2] for pt in parts], axis=1)

    b_cols = lax.dynamic_slice(b_ada, (0, chip * n_ada), (1, n_ada))
    c_pad = jnp.concatenate([c_all, jnp.zeros_like(c_all)], axis=0)
    mod_blk = _ada_fwd("ada_fwd", c_pad, w_ada[0], b_cols)[:N_DEV]
    g2 = _allgather8(_pack([mod_blk]), "gather_mod")
    mod_all = jnp.concatenate([_unpack(g2[2 * q], [(N_DEV, n_ada)])[0] for q in range(N_CHIPS)], axis=1)
    mod = lax.dynamic_slice(mod_all, (me, 0), (1, 6 * D))
    shift1, scale1, gate1, shift2, scale2, gate2 = [mod[:, k * D:(k + 1) * D] for k in range(6)]

    shards = [w_in[0], w_conv_out[0], w_sgu_out[0], w_out[0], w_up[0], w_down[0]]
    names = ["in", "conv_out", "sgu_out", "out", "up", "down"]
    blk = {nm: _cast_into_block(s, p_idx, "cast_" + nm) for s, nm in zip(shards, names)}
    pc_idx = jnp.concatenate([p_idx, c_idx])

    h1 = _modnorm_fwd("modnorm1", xs, norm1_g, scale1, shift1)
    (proj,), (win_xy,) = _mm_fwd_block("proj_own", h1, blk["in"], b_in, p_idx, 0,
                                       job=_job_gather([blk["in"]], rel=(0, 1), fresh=True))
    (proj,), (win_d,) = _mm_fwd_block("proj_x", h1, win_xy, b_in, p_idx, 2, into=proj,
                                      job=_job_gather([blk["in"]], rel=(2,), fresh=True))
    (proj,), (wco_f, wso_f) = _mm_fwd_block("proj_y", h1, win_xy, b_in, p_idx, 1, into=proj,
                                            job=_job_gather([blk["conv_out"], blk["sgu_out"]]))
    (proj,), (wout_f,) = _mm_fwd_block("proj_diag", h1, win_d, b_in, p_idx, 3, into=proj,
                                       job=_job_gather([blk["out"]]))
    win_f = _copy_block(_copy_block(win_xy, blk["in"], p_idx, 0, "fill_w_in_own"), win_d, p_idx, 3, "fill_w_in_diag")
    wout_r = wout_f.reshape(-1, wout_f.shape[2])
    (conv, a_act), (wup_xy,) = _conv_fwd("conv_fwd", proj, cw_full, conv_dw_b, conv_ln_g, conv_ln_b, dc,
                                         job=_job_gather([blk["up"]], rel=(0, 1)))
    bst = jnp.transpose(b_spatial[0])
    (ya, yb, uv, merged), (wup_f,) = _sgu_fwd("sgu_fwd", proj, a_act, wco_f, wso_f, sgu_ln_g, sgu_ln_b, w_spatial[0],
                                              bst, D, ds, job=_job_gather([wup_xy], rel=(2,)))
    out1, x1 = _mm_fwd_rows("out1", merged, wout_r, xs, gate1)
    h2 = _modnorm_fwd("modnorm2", x1, norm2_g, scale2, shift2)
    up, (wdown_f,) = _mm_fwd("up", h2, wup_f, planes=2, job=_job_gather([blk["down"]]))
    wdown_r = wdown_f.reshape(-1, wdown_f.shape[2])
    fw2 = jnp.stack([fw_full[:, :Fh], fw_full[:, Fh:]])
    fb2 = jnp.stack([ffn_dw_b[:, :Fh], ffn_dw_b[:, Fh:]])
    act = _ffn_fwd("ffn_fwd", up, fw2, fb2)
    out2, x2 = _mm_fwd_rows("out2", act, wdown_r, x1, gate2)
    dx2, d_final_g, loss_blk, dout2, d_gate2 = _final("final", x2, tgt, final_g.reshape(1, D), out2, gate2)
    loss = lax.psum(loss_blk[0, 0], ("x", "y", "c"))

    def add_cores(nm, g, r1):
        return _add_own_half(g, r1, c_idx, "add_cores_" + nm)

    def add_chips(nm, g, r1, r2):
        return _add_chips(g, r1, r2, pc_idx, "add_chips_" + nm)

    g_wdown = _mm_wgrad_rows("wgrad_down", act, dout2, w_down.shape[1])
    dact, (r1_down,) = _mm_dgrad_rows("dgrad_down", dout2, wdown_r, job=_job_swap_halves([g_wdown]))
    s_down = add_cores("down", g_wdown, r1_down)
    (dup, d_ffn), (r2_down,) = _ffn_bwd("ffn_bwd", up, dact, fw2, fb2, job=_job_scatter_blocks([s_down]))
    h_down = add_chips("down", g_wdown, r1_down, r2_down)
    g_wup, (sib_down,) = _mm_wgrad_cols("wgrad_up", h2, dup, n_up, job=_job_to_sibling([h_down]))
    dh2, (r1_up,) = _mm_dgrad_cols("dgrad_up", dup, wup_f, job=_job_swap_halves([g_wup]))
    s_up = add_cores("up", g_wup, r1_up)
    dx1, d_norm2, d_scale2, d_shift2, dout1, d_gate1 = _modnorm_bwd(
        "modnorm2_bwd", x1, dh2, dx2, norm2_g, scale2, shift2, gated=(out1, gate1))
    g_wout = _mm_wgrad_rows("wgrad_out", merged, dout1, w_out.shape[1])
    dmerged, (r1_out,) = _mm_dgrad_rows("dgrad_out", dout1, wout_r, job=_job_swap_halves([g_wout]))
    s_out = add_cores("out", g_wout, r1_out)
    dya, dyb, dproj, dbin_g = _merge_bwd("merge_bwd", dmerged, proj, ya, yb, D)
    g_wco = _mm_wgrad_cols("wgrad_conv_out", a_act, dya[None], w_conv_out.shape[2])
    g_wso = _mm_wgrad_cols("wgrad_sgu_out", uv, dyb[None], w_sgu_out.shape[2])
    da_act, (r1_co, r1_so) = _mm_dgrad_cols("dgrad_conv_out", dya[None], wco_f, job=_job_swap_halves([g_wco, g_wso]))
    s_co = add_cores("conv_out", g_wco, r1_co)
    s_so = add_cores("sgu_out", g_wso, r1_so)
    duv = _mm_dgrad_cols("dgrad_sgu_out", dyb[None], wso_f)
    dproj, d_ws, d_bs, d_sgu_g, d_sgu_b, dbin_s = _sgu_bwd("sgu_bwd", proj, duv, sgu_ln_g, sgu_ln_b, w_spatial[0], bst,
                                                           dproj, ds)
    dconv, d_cln_g, d_cln_b, d_conv_b = _conv_bwd_ln("conv_ln_bwd", conv, da_act, conv_ln_g, conv_ln_b)
    (dproj, d_cw, dbin_a), (r2_up,) = _conv_bwd("conv_bwd", dconv, proj, cw_full, dproj, dc,
                                                job=_job_scatter_blocks([s_up]))
    h_up = add_chips("up", g_wup, r1_up, r2_up)
    g_win, (r2_out, r2_co, r2_so, sib_up) = _mm_wgrad_cols(
        "wgrad_in", h1, dproj[None], w_in.shape[2],
        job=_merge_jobs([_job_scatter_blocks([s_out, s_co, s_so]), _job_to_sibling([h_up])]))
    h_out = add_chips("out", g_wout, r1_out, r2_out)
    h_co = add_chips("conv_out", g_wco, r1_co, r2_co)
    h_so = add_chips("sgu_out", g_wso, r1_so, r2_so)
    dh1, (r1_in, sib_out, sib_co, sib_so) = _mm_dgrad_cols(
        "dgrad_in_a", dproj[None], win_f, row_tiles=(0.0, 0.5),
        job=_merge_jobs([_job_swap_halves([g_win]), _job_to_sibling([h_out, h_co, h_so])]))
    s_in = add_cores("in", g_win, r1_in)
    dh1, (r2_in,) = _mm_dgrad_cols("dgrad_in_b", dproj[None], win_f, row_tiles=(0.5, 0.5), fill_into=dh1,
                                   job=_job_scatter_blocks([s_in]))
    h_in = add_chips("in", g_win, r1_in, r2_in)
    dxs, d_norm1, d_scale1, d_shift1 = _modnorm_bwd("modnorm1_bwd", xs, dh1, dx1, norm1_g, scale1, shift1)
    (sib_in,) = _run_job(_job_to_sibling([h_in]), "exchange_w_in")
    big_halves = {"w_in": (h_in, sib_in), "w_conv_out": (h_co, sib_co), "w_sgu_out": (h_so, sib_so),
                  "w_out": (h_out, sib_out), "w_up": (h_up, sib_up), "w_down": (h_down, sib_down)}

    d_mod = jnp.concatenate([d_shift1, d_scale1, d_gate1, d_shift2, d_scale2, d_gate2], axis=1)
    d_b_in = jnp.concatenate([dbin_a, dbin_s, dbin_g], axis=1)
    d_fw = jnp.concatenate([d_ffn[0, :KF], d_ffn[1, :KF]], axis=1)
    d_fb = jnp.concatenate([d_ffn[0, KF:KF + 1], d_ffn[1, KF:KF + 1]], axis=1)
    small = [d_mod, d_norm1, d_b_in, d_conv_b, d_cln_g, d_cln_b, d_sgu_g, d_sgu_b, d_ws, d_bs[:, :, 0], d_norm2,
             d_fb, d_final_g, d_cw[:KC], d_fw]
    small_shapes = [a.shape for a in small]
    g3 = _allgather8(_pack(small), "gather_small_grads")
    summed = _unpack(_sum8(g3, "sum_small_grads"), small_shapes)
    (g_b_ada, g_norm1, g_b_in, g_conv_b, g_cln_g, g_cln_b, g_sgu_g, g_sgu_b, g_ws, g_bs, g_norm2, g_fb, g_final,
     g_cw_full, g_fw_full) = summed
    n_cw = conv_dw_w.shape[2]
    n_fw = ffn_dw_w.shape[2]
    g_cw = lax.dynamic_slice(g_cw_full, (0, chip * n_cw), (KC, n_cw))
    g_fw = lax.dynamic_slice(g_fw_full, (0, chip * n_fw), (KF, n_fw))
    dmod_all = jnp.stack([_unpack(g3[d], [(6 * D,)])[0] for d in range(N_DEV)])
    dmod_cols = lax.dynamic_slice(dmod_all, (0, chip * n_ada), (N_DEV, n_ada))
    g_wada = _ada_wgrad("ada_wgrad", jnp.transpose(c_all), dmod_cols)

    grads = {
        "w_ada": g_wada[None], "b_ada": g_b_ada, "norm1_g": g_norm1, "b_in": g_b_in,
        "conv_dw_w": g_cw[None], "conv_dw_b": g_conv_b, "conv_ln_g": g_cln_g, "conv_ln_b": g_cln_b,
        "sgu_ln_g": g_sgu_g, "sgu_ln_b": g_sgu_b, "w_spatial": g_ws[None],
        "b_spatial": g_bs[None], "norm2_g": g_norm2, "ffn_dw_w": g_fw[None], "ffn_dw_b": g_fb,
        "final_g": g_final.reshape(D),
    }
    weights = dict(w_ada=w_ada, b_ada=b_ada, norm1_g=norm1_g, w_in=w_in, b_in=b_in, conv_dw_w=conv_dw_w, conv_dw_b=conv_dw_b, conv_ln_g=conv_ln_g, conv_ln_b=conv_ln_b, w_conv_out=w_conv_out, sgu_ln_g=sgu_ln_g, sgu_ln_b=sgu_ln_b, w_spatial=w_spatial, b_spatial=b_spatial, w_sgu_out=w_sgu_out, w_out=w_out, norm2_g=norm2_g, w_up=w_up, ffn_dw_w=ffn_dw_w, ffn_dw_b=ffn_dw_b, w_down=w_down, final_g=final_g)
    m_in = dict(w_ada=m_w_ada, b_ada=m_b_ada, norm1_g=m_norm1_g, w_in=m_w_in, b_in=m_b_in, conv_dw_w=m_conv_dw_w, conv_dw_b=m_conv_dw_b, conv_ln_g=m_conv_ln_g, conv_ln_b=m_conv_ln_b, w_conv_out=m_w_conv_out, sgu_ln_g=m_sgu_ln_g, sgu_ln_b=m_sgu_ln_b, w_spatial=m_w_spatial, b_spatial=m_b_spatial, w_sgu_out=m_w_sgu_out, w_out=m_w_out, norm2_g=m_norm2_g, w_up=m_w_up, ffn_dw_w=m_ffn_dw_w, ffn_dw_b=m_ffn_dw_b, w_down=m_w_down, final_g=m_final_g)
    v_in = dict(w_ada=v_w_ada, b_ada=v_b_ada, norm1_g=v_norm1_g, w_in=v_w_in, b_in=v_b_in, conv_dw_w=v_conv_dw_w, conv_dw_b=v_conv_dw_b, conv_ln_g=v_conv_ln_g, conv_ln_b=v_conv_ln_b, w_conv_out=v_w_conv_out, sgu_ln_g=v_sgu_ln_g, sgu_ln_b=v_sgu_ln_b, w_spatial=v_w_spatial, b_spatial=v_b_spatial, w_sgu_out=v_w_sgu_out, w_out=v_w_out, norm2_g=v_norm2_g, w_up=v_w_up, ffn_dw_w=v_ffn_dw_w, ffn_dw_b=v_ffn_dw_b, w_down=v_w_down, final_g=v_final_g)
    order = list(weights.keys())
    large = ["w_ada", "w_in", "w_conv_out", "w_sgu_out", "w_out", "w_up", "w_down"]
    little = [n for n in order if n not in large]
    delta, new_m, new_v = {}, {}, {}
    for n in large:
        shp = weights[n].shape
        two = (shp[1], shp[2])
        if n in big_halves:
            g_, d_, m_, v_ = _adamw_halves(weights[n].reshape(two), big_halves[n][0], big_halves[n][1],
                                           m_in[n].reshape(two), v_in[n].reshape(two), c_idx, "adamw_" + n)
            grads[n] = g_.reshape(shp)
        else:
            d_, m_, v_ = _adamw(weights[n].reshape(two), grads[n].reshape(two), m_in[n].reshape(two),
                                v_in[n].reshape(two), "adamw_" + n)
        delta[n], new_m[n], new_v[n] = d_.reshape(shp), m_.reshape(shp), v_.reshape(shp)
    shapes = [weights[n].shape for n in little]
    d_, m_, v_ = _adamw(_pack([weights[n] for n in little]), _pack([grads[n] for n in little]),
                        _pack([m_in[n] for n in little]), _pack([v_in[n] for n in little]), "adamw_small")
    for n, dd, mm, vv in zip(little, _unpack(d_, shapes), _unpack(m_, shapes), _unpack(v_, shapes)):
        delta[n], new_m[n], new_v[n] = dd, mm, vv
    grad_out = [grads[n].reshape(weights[n].shape) for n in order]
    return (loss, dxs[None], *grad_out, *[delta[n] for n in order], *[new_m[n] for n in order],
            *[new_v[n] for n in order])
```

```python
import functools

import jax
import jax.numpy as jnp
from jax import lax
from jax.experimental import pallas as pl
from jax.experimental.pallas import tpu as pltpu

F32 = jnp.float32
BF16 = jnp.bfloat16
EPS = 1e-6
MESH = pl.DeviceIdType.MESH
N_CHIPS = 4
N_DEV = 8
LANES = 128
SUBLANES = 8
CONV_HALO = 32
FFN_HALO = 8
VMEM_LIMIT_BYTES = 56 * 1024 * 1024
WHOLE_WEIGHT_BYTES = 8 * 1024 * 1024

ADAM_LR = 0.001
ADAM_B1 = 0.9
ADAM_B2 = 0.999
ADAM_EPS = 1e-08
ADAM_WD = 0.01
ADAM_STEP = 10

NN = (((1,), (0,)), ((), ()))
NT = (((1,), (1,)), ((), ()))
TN = (((0,), (0,)), ((), ()))


def _params(sem=None):
    return pltpu.CompilerParams(dimension_semantics=sem, vmem_limit_bytes=VMEM_LIMIT_BYTES)


def _pick(dim, pref, mult):
    best = None
    d = mult
    while d <= min(dim, pref):
        if dim % d == 0:
            best = d
        d += mult
    return dim if best is None else best


def _axes():
    return lax.axis_index("x"), lax.axis_index("y"), lax.axis_index("c")


def _modnorm(x, g, scale, shift):
    r = lax.rsqrt(jnp.mean(x * x, axis=-1, keepdims=True) + EPS)
    return (x * r * g) * (1.0 + scale) + shift


def _layer_norm(x, g, b):
    mu = jnp.mean(x, axis=-1, keepdims=True)
    var = jnp.mean(jnp.square(x - mu), axis=-1, keepdims=True)
    return (x - mu) * lax.rsqrt(var + EPS) * g + b


def _gelu(x):
    return 0.5 * x * (1.0 + lax.erf(x * (0.5 ** 0.5)))


def _ln_silu(x, g, b):
    return jax.nn.silu(_layer_norm(x, g, b))


def _tril_mask(ws):
    n = ws.shape[-1]
    row = lax.broadcasted_iota(jnp.int32, (n, n), 0)
    col = lax.broadcasted_iota(jnp.int32, (n, n), 1)
    return jnp.where(row >= col, ws, 0.0)


def _pack(arrs):
    flat = [a.reshape(-1).astype(F32) for a in arrs]
    total = sum(f.shape[0] for f in flat)
    tile = SUBLANES * LANES
    padded = -(-total // tile) * tile
    if padded > total:
        flat = flat + [jnp.zeros((padded - total,), F32)]
    return jnp.concatenate(flat).reshape(padded // LANES, LANES)


def _unpack(buf, shapes):
    flat = buf.reshape(-1)
    out, off = [], 0
    for s in shapes:
        n = 1
        for d in s:
            n *= d
        out.append(flat[off:off + n].reshape(s))
        off += n
    return out


def _allgather8(buf, name):
    R, L = buf.shape

    def body(in_ref, out_ref, send_sems, recv_sems, local_sem):
        x, y, c = _axes()
        me = 4 * x + 2 * y + c
        mine = pltpu.make_async_copy(in_ref, out_ref.at[me], local_sem)
        mine.start()
        peers = []
        for k in range(1, N_DEV):
            px = 1 - x if k & 4 else x
            py = 1 - y if k & 2 else y
            pc = 1 - c if k & 1 else c
            peers.append((px, py, pc))
        sends = []
        for k, peer in enumerate(peers):
            cp = pltpu.make_async_remote_copy(
                src_ref=in_ref, dst_ref=out_ref.at[me], send_sem=send_sems.at[k], recv_sem=recv_sems.at[k],
                device_id=peer, device_id_type=MESH)
            cp.start()
            sends.append(cp)
        for k, (px, py, pc) in enumerate(peers):
            pltpu.make_async_remote_copy(
                src_ref=in_ref, dst_ref=out_ref.at[4 * px + 2 * py + pc], send_sem=send_sems.at[k],
                recv_sem=recv_sems.at[k], device_id=(px, py, pc), device_id_type=MESH).wait_recv()
        for cp in sends:
            cp.wait_send()
        mine.wait()

    return pl.pallas_call(
        body, name=name,
        out_shape=jax.ShapeDtypeStruct((N_DEV, R, L), buf.dtype),
        in_specs=[pl.BlockSpec(memory_space=pltpu.VMEM)],
        out_specs=pl.BlockSpec(memory_space=pltpu.VMEM),
        scratch_shapes=[pltpu.SemaphoreType.DMA((N_DEV - 1,)), pltpu.SemaphoreType.DMA((N_DEV - 1,)),
                        pltpu.SemaphoreType.DMA],
        compiler_params=pltpu.CompilerParams(vmem_limit_bytes=VMEM_LIMIT_BYTES),
    )(buf)


def _other_chips(x, y):
    return [(1 - x, y), (x, 1 - y), (1 - x, 1 - y)]


def _cast_into_block(shard, p_idx, name):
    K, n = shard.shape
    tr = _pick(K, max(SUBLANES, (1 << 19) // n), 2 * SUBLANES)

    def body(p_ref, s_ref, o_ref):
        o_ref[...] = s_ref[...].astype(BF16)

    return pl.pallas_call(
        body, name=name,
        grid_spec=pltpu.PrefetchScalarGridSpec(
            num_scalar_prefetch=1, grid=(K // tr,),
            in_specs=[pl.BlockSpec((tr, n), lambda i, p: (i, 0))],
            out_specs=pl.BlockSpec((None, tr, n), lambda i, p: (p[0], i, 0))),
        out_shape=jax.ShapeDtypeStruct((N_CHIPS, K, n), BF16),
        compiler_params=_params(("parallel",)),
    )(p_idx, shard)


class _Job:
    def __init__(self, ins, outs, aliases, n_sems, make):
        self.ins, self.outs, self.aliases, self.n_sems, self.make = list(ins), list(outs), list(aliases), n_sems, make


def _merge_jobs(jobs):
    ins, outs, aliases, offs = [], [], [], []
    n_sems = 0
    for jb in jobs:
        offs.append((len(ins), len(outs), n_sems))
        aliases += [(len(ins) + a, len(outs) + b) for a, b in jb.aliases]
        ins += jb.ins
        outs += jb.outs
        n_sems += jb.n_sems

    def make(in_refs, out_refs, send_sems, recv_sems, base=0):
        made = []
        for jb, (oi, oo, os_) in zip(jobs, offs):
            made.append(jb.make(in_refs[oi:oi + len(jb.ins)], out_refs[oo:oo + len(jb.outs)],
                                send_sems, recv_sems, base + os_))

        def start():
            for st, _ in made:
                st()

        def finish():
            for _, fin in made:
                fin()

        return start, finish

    return _Job(ins, outs, aliases, n_sems, make)


def _job_gather(fulls, rel=(0, 1, 2), fresh=False):
    nw = len(fulls)

    def make(in_refs, outs, send_sems, recv_sems, base=0):
        x, y, c = _axes()
        p = 2 * x + y
        chips = _other_chips(x, y)
        srcs = in_refs if fresh else outs

        def rows(w, mine):
            kh = outs[w].shape[1] // 2
            return pl.ds((c if mine else 1 - c) * kh, kh)

        def over_ici(w, j, block):
            qx, qy = chips[j]
            return pltpu.make_async_remote_copy(
                src_ref=srcs[w].at[block, rows(w, True)], dst_ref=outs[w].at[block, rows(w, True)],
                send_sem=send_sems.at[base + 6 * w + j], recv_sem=recv_sems.at[base + 6 * w + j],
                device_id=(qx, qy, c), device_id_type=MESH)

        def over_d2d(w, j, mine):
            qx, qy = chips[j]
            return pltpu.make_async_remote_copy(
                src_ref=outs[w].at[2 * qx + qy, rows(w, mine)], dst_ref=outs[w].at[2 * qx + qy, rows(w, mine)],
                send_sem=send_sems.at[base + 6 * w + 3 + j], recv_sem=recv_sems.at[base + 6 * w + 3 + j],
                device_id=(x, y, 1 - c), device_id_type=MESH)

        def start():
            for w in range(nw):
                for j in rel:
                    over_ici(w, j, p).start()

        def finish():
            for w in range(nw):
                for j in rel:
                    qx, qy = chips[j]
                    over_ici(w, j, 2 * qx + qy).wait_recv()
                    over_d2d(w, j, True).start()
            for w in range(nw):
                for j in rel:
                    over_d2d(w, j, False).wait_recv()
            for w in range(nw):
                for j in rel:
                    over_ici(w, j, p).wait_send()
                    over_d2d(w, j, True).wait_send()

        return start, finish

    return _Job(fulls, [jax.ShapeDtypeStruct(f.shape, f.dtype) for f in fulls],
                [] if fresh else [(w, w) for w in range(nw)], 6 * nw, make)


def _job_simple(ins, outs, n_per, copies_of):
    nw = len(ins)

    def make(in_refs, out_refs, send_sems, recv_sems, base=0):
        x, y, c = _axes()
        copies = []
        for w in range(nw):
            for j, (src, dst, dev) in enumerate(copies_of(w, in_refs[w], out_refs[w], x, y, c)):
                k = base + n_per * w + j
                copies.append(pltpu.make_async_remote_copy(
                    src_ref=src, dst_ref=dst, send_sem=send_sems.at[k], recv_sem=recv_sems.at[k],
                    device_id=dev, device_id_type=MESH))

        def start():
            for cp in copies:
                cp.start()

        def finish():
            for cp in copies:
                cp.wait()

        return start, finish

    return _Job(ins, outs, [], n_per * nw, make)


def _job_swap_halves(grads):
    def copies_of(w, src, dst, x, y, c):
        kh = src.shape[1] // 2
        return [(src.at[:, pl.ds((1 - c) * kh, kh), :], dst, (x, y, 1 - c))]

    outs = [jax.ShapeDtypeStruct((g.shape[0], g.shape[1] // 2, g.shape[2]), g.dtype) for g in grads]
    return _job_simple(grads, outs, 1, copies_of)


def _job_scatter_blocks(sums):
    def copies_of(w, src, dst, x, y, c):
        return [(src.at[2 * qx + qy], dst.at[j], (qx, qy, c)) for j, (qx, qy) in enumerate(_other_chips(x, y))]

    outs = [jax.ShapeDtypeStruct((3,) + s.shape[1:], s.dtype) for s in sums]
    return _job_simple(sums, outs, 3, copies_of)


def _job_to_sibling(arrs):
    def copies_of(w, src, dst, x, y, c):
        return [(src, dst, (x, y, 1 - c))]

    outs = [jax.ShapeDtypeStruct(a.shape, a.dtype) for a in arrs]
    return _job_simple(arrs, outs, 1, copies_of)


def _job_allgather8(buf):
    def make(in_refs, out_refs, send_sems, recv_sems, base=0):
        x, y, c = _axes()
        me = 4 * x + 2 * y + c
        src, dst = in_refs[0], out_refs[0]
        peers = [(1 - x if k & 4 else x, 1 - y if k & 2 else y, 1 - c if k & 1 else c) for k in range(1, N_DEV)]
        mine = pltpu.make_async_copy(src, dst.at[me], send_sems.at[base + N_DEV - 1])

        def to(k):
            return pltpu.make_async_remote_copy(
                src_ref=src, dst_ref=dst.at[me], send_sem=send_sems.at[base + k], recv_sem=recv_sems.at[base + k],
                device_id=peers[k], device_id_type=MESH)

        def of(k):
            px, py, pc = peers[k]
            return pltpu.make_async_remote_copy(
                src_ref=src, dst_ref=dst.at[4 * px + 2 * py + pc], send_sem=send_sems.at[base + k],
                recv_sem=recv_sems.at[base + k], device_id=peers[k], device_id_type=MESH)

        def start():
            mine.start()
            for k in range(N_DEV - 1):
                to(k).start()

        def finish():
            for k in range(N_DEV - 1):
                of(k).wait_recv()
                to(k).wait_send()
            mine.wait()

        return start, finish

    return _Job([buf], [jax.ShapeDtypeStruct((N_DEV,) + buf.shape, buf.dtype)], [], N_DEV, make)


def _run_job(job, name):
    ni, no = len(job.ins), len(job.outs)

    def body(*refs):
        start, finish = job.make(refs[:ni], refs[ni:ni + no], refs[-2], refs[-1])
        start()
        finish()

    any_spec = pl.BlockSpec(memory_space=pl.ANY)
    return pl.pallas_call(
        body, name=name, out_shape=job.outs, in_specs=[any_spec] * ni, out_specs=[any_spec] * no,
        input_output_aliases=dict(job.aliases),
        scratch_shapes=[pltpu.SemaphoreType.DMA((job.n_sems,)), pltpu.SemaphoreType.DMA((job.n_sems,))],
    )(*job.ins)


def _call(body, name, grid, in_specs, out_specs, out_shape, scratch_shapes, semantics, args, aliases=None, job=None,
          prefetch=()):
    in_specs, out_specs, out_shape = list(in_specs), list(out_specs), list(out_shape)
    scratch_shapes = list(scratch_shapes)
    n_pre, n_in, n_out, n_scr = len(prefetch), len(args), len(out_shape), len(scratch_shapes)
    all_aliases = {n_pre + a: b for a, b in (aliases or {}).items()}
    if job is None:
        wrapped, extra_in, semantics = body, [], semantics
    else:
        ni, no = len(job.ins), len(job.outs)

        def wrapped(*refs):
            pre, refs = refs[:n_pre], refs[n_pre:]
            ins, cins = refs[:n_in], refs[n_in:n_in + ni]
            outs, couts = refs[n_in + ni:n_in + ni + n_out], refs[n_in + ni + n_out:n_in + ni + n_out + no]
            scr = refs[n_in + ni + n_out + no:n_in + ni + n_out + no + n_scr]
            start, finish = job.make(cins, couts, refs[-2], refs[-1])
            first = functools.reduce(jnp.logical_and, [pl.program_id(a) == 0 for a in range(len(grid))])
            last = functools.reduce(jnp.logical_and, [pl.program_id(a) == grid[a] - 1 for a in range(len(grid))])
            pl.when(first)(start)
            body(*pre, *ins, *outs, *scr)
            pl.when(last)(finish)

        any_spec = pl.BlockSpec(memory_space=pl.ANY)
        for a, b in job.aliases:
            all_aliases[n_pre + n_in + a] = n_out + b
        in_specs, out_specs, out_shape = in_specs + [any_spec] * ni, out_specs + [any_spec] * no, out_shape + job.outs
        scratch_shapes = scratch_shapes + [pltpu.SemaphoreType.DMA((job.n_sems,)), pltpu.SemaphoreType.DMA((job.n_sems,))]
        extra_in, semantics = job.ins, tuple("arbitrary" for _ in grid)
    if n_pre:
        return pl.pallas_call(
            wrapped, name=name,
            grid_spec=pltpu.PrefetchScalarGridSpec(num_scalar_prefetch=n_pre, grid=grid, in_specs=in_specs,
                                                   out_specs=out_specs, scratch_shapes=scratch_shapes),
            out_shape=out_shape, input_output_aliases=all_aliases,
            compiler_params=_params(semantics))(*prefetch, *args, *extra_in)
    return pl.pallas_call(
        wrapped, name=name, grid=grid, in_specs=in_specs, out_specs=out_specs, out_shape=out_shape,
        scratch_shapes=scratch_shapes, input_output_aliases=all_aliases,
        compiler_params=_params(semantics))(*args, *extra_in)


def _add_own_half(g, r, c_idx, name):
    nb, K, n = g.shape
    kh = K // 2
    tr = _pick(kh, max(SUBLANES, (1 << 19) // n), 2 * SUBLANES)
    per = kh // tr

    def body(c_ref, g_ref, r_ref, o_ref):
        o_ref[...] = (g_ref[...] + r_ref[...]).astype(BF16)

    return pl.pallas_call(
        body, name=name,
        grid_spec=pltpu.PrefetchScalarGridSpec(
            num_scalar_prefetch=1, grid=(nb, per),
            in_specs=[pl.BlockSpec((None, tr, n), lambda b, i, c: (b, c[0] * per + i, 0)),
                      pl.BlockSpec((None, tr, n), lambda b, i, c: (b, i, 0))],
            out_specs=pl.BlockSpec((None, tr, n), lambda b, i, c: (b, i, 0))),
        out_shape=jax.ShapeDtypeStruct((nb, kh, n), BF16),
        compiler_params=_params(("parallel", "parallel")),
    )(c_idx, g, r)


def _add_chips(g, r1, r2, pc_idx, name):
    _, K, n = g.shape
    kh = K // 2
    tr = _pick(kh, max(SUBLANES, (1 << 19) // n), 2 * SUBLANES)
    per = kh // tr

    def body(pc_ref, g_ref, r1_ref, r2_ref, o_ref):
        own = g_ref[...] + r1_ref[...]
        o_ref[...] = ((own + r2_ref[0].astype(F32)) + r2_ref[1].astype(F32)) + r2_ref[2].astype(F32)

    return pl.pallas_call(
        body, name=name,
        grid_spec=pltpu.PrefetchScalarGridSpec(
            num_scalar_prefetch=1, grid=(per,),
            in_specs=[pl.BlockSpec((None, tr, n), lambda i, pc: (pc[0], pc[1] * per + i, 0)),
                      pl.BlockSpec((None, tr, n), lambda i, pc: (pc[0], i, 0)),
                      pl.BlockSpec((3, tr, n), lambda i, pc: (0, i, 0))],
            out_specs=pl.BlockSpec((tr, n), lambda i, pc: (i, 0))),
        out_shape=jax.ShapeDtypeStruct((kh, n), F32),
        compiler_params=_params(("parallel",)),
    )(pc_idx, g, r1, r2)


def _sum8(g, name):
    _, R, L = g.shape

    def body(g_ref, o_ref):
        acc = g_ref[0]
        for d in range(1, N_DEV):
            acc = acc + g_ref[d]
        o_ref[...] = acc

    return pl.pallas_call(
        body, name=name, out_shape=jax.ShapeDtypeStruct((R, L), F32),
        in_specs=[pl.BlockSpec(memory_space=pltpu.VMEM)], out_specs=pl.BlockSpec(memory_space=pltpu.VMEM),
        compiler_params=_params(),
    )(g)


def _adamw_math(w, gg, m, v):
    nm = ADAM_B1 * m + (1.0 - ADAM_B1) * gg
    nv = ADAM_B2 * v + (1.0 - ADAM_B2) * jnp.square(gg)
    m_hat = nm / (1.0 - ADAM_B1 ** ADAM_STEP)
    v_hat = nv / (1.0 - ADAM_B2 ** ADAM_STEP)
    return -ADAM_LR * (m_hat / (jnp.sqrt(v_hat) + ADAM_EPS) + ADAM_WD * w), nm, nv


def _adamw(w, g, m, v, name):
    R, C = w.shape
    tr = _pick(R, max(SUBLANES, (1 << 18) // C), SUBLANES)

    def body(w_ref, g_ref, m_ref, v_ref, d_ref, nm_ref, nv_ref):
        d_ref[...], nm_ref[...], nv_ref[...] = _adamw_math(w_ref[...], g_ref[...], m_ref[...], v_ref[...])

    spec = pl.BlockSpec((tr, C), lambda i: (i, 0))
    sd = jax.ShapeDtypeStruct((R, C), F32)
    return pl.pallas_call(
        body, name=name, grid=(R // tr,), in_specs=[spec] * 4, out_specs=[spec] * 3, out_shape=[sd] * 3,
        compiler_params=_params(("parallel",)),
    )(w, g, m, v)


def _adamw_halves(w, g_own, g_sib, m, v, c_idx, name):
    K, n = w.shape
    kh = K // 2
    tr = _pick(kh, max(SUBLANES, (1 << 18) // n), SUBLANES)
    per = kh // tr

    def body(c_ref, w_ref, go_ref, gs_ref, m_ref, v_ref, g_ref, d_ref, nm_ref, nv_ref):
        h = pl.program_id(0)

        def step(gg):
            g_ref[...] = gg
            d_ref[...], nm_ref[...], nv_ref[...] = _adamw_math(w_ref[...], gg, m_ref[...], v_ref[...])

        @pl.when(h == 0)
        def _():
            step(go_ref[...])

        @pl.when(h == 1)
        def _():
            step(gs_ref[...])

    full = pl.BlockSpec((tr, n), lambda h, i, c: (((c[0] + h) % 2) * per + i, 0))
    own = pl.BlockSpec((tr, n), lambda h, i, c: (i * (1 - h), 0))
    sib = pl.BlockSpec((tr, n), lambda h, i, c: (i * h, 0))
    sd = jax.ShapeDtypeStruct((K, n), F32)
    return pl.pallas_call(
        body, name=name,
        grid_spec=pltpu.PrefetchScalarGridSpec(
            num_scalar_prefetch=1, grid=(2, per),
            in_specs=[full, own, sib, full, full], out_specs=[full] * 4),
        out_shape=[sd] * 4,
        compiler_params=_params(("arbitrary", "arbitrary")),
    )(c_idx, w, g_own, g_sib, m, v)


def _matmul(name, grid, dims, a, a_spec, b, b_spec, outs, out_specs, acc_shape,
            extras=(), extra_specs=(), epilogue=None, job=None, fill_into=None, prefetch=()):
    nk = grid[2]
    npre = len(prefetch)
    aliases = None
    if fill_into is not None:
        aliases = {2 + len(extras): 0}
        extras = tuple(extras) + (fill_into,)
        extra_specs = tuple(extra_specs) + (pl.BlockSpec(memory_space=pl.ANY),)
    nex = len(extras)
    nout = len(outs)

    def body(*refs):
        a_ref, b_ref, rest = refs[npre], refs[npre + 1], refs[npre + 2:]
        ex, o = rest[:nex], rest[nex:nex + nout]
        part = lax.dot_general(a_ref[...].astype(BF16), b_ref[...].astype(BF16), dims,
                               preferred_element_type=F32)

        def finish(res):
            if epilogue is None:
                o[0][...] = res.astype(o[0].dtype)
            else:
                epilogue(res, ex, o)

        if nk == 1:
            finish(part)
        else:
            acc = o[0] if in_place else rest[-1]
            k = pl.program_id(2)

            @pl.when(k == 0)
            def _():
                acc[...] = part

            @pl.when(k > 0)
            def _():
                acc[...] += part

            if not in_place:
                @pl.when(k == nk - 1)
                def _():
                    finish(acc[...])

    in_place = epilogue is None and nout == 1 and outs[0].dtype == F32
    res = _call(body, name, grid, [a_spec, b_spec] + list(extra_specs), out_specs, outs,
                [] if nk == 1 or in_place else [pltpu.VMEM(acc_shape, F32)], ("parallel", "parallel", "arbitrary"),
                (a, b, *extras), aliases=aliases, job=job, prefetch=prefetch)
    return res if job is None else (res[:nout], res[nout:])


def _first(res, job):
    return res[0] if job is None else (res[0][0], res[1])


def _mm_fwd(name, a, wfull, out_dtype=F32, planes=1, bias=None, job=None):
    S, K = a.shape
    _, _, n = wfull.shape
    N = N_CHIPS * n
    tm = _pick(S, 1024, LANES)
    tn = _pick(n, 1408, LANES)
    per = n // tn
    nj = N // tn
    pj = nj // planes
    grid = (S // tm, nj, 1)
    a_spec = pl.BlockSpec((tm, K), lambda i, j, k: (i, 0))
    b_spec = pl.BlockSpec((None, K, tn), lambda i, j, k: (j // per, 0, j % per))
    o_spec = pl.BlockSpec((None, tm, tn), lambda i, j, k: (j // pj, i, j % pj))
    sd = jax.ShapeDtypeStruct((planes, S, N // planes), out_dtype)
    if bias is not None:
        def epi(res, ex, o):
            o[0][...] = (res + ex[0][...]).astype(o[0].dtype)

        out = _matmul(name, grid, NN, a, a_spec, wfull, b_spec, [sd], [o_spec], (tm, tn), extras=(bias,),
                      extra_specs=(pl.BlockSpec((1, tn), lambda i, j, k: (0, j)),), epilogue=epi, job=job)
    else:
        out = _matmul(name, grid, NN, a, a_spec, wfull, b_spec, [sd], [o_spec], (tm, tn), job=job)
    return _first(out, job)


def _mm_fwd_block(name, a, wfull, bias, p_idx, mask, into=None, job=None):
    S, K = a.shape
    _, _, n = wfull.shape
    N = N_CHIPS * n
    tm = _pick(S, 1024, LANES)
    tn = _pick(n, 1408, LANES)
    per = n // tn
    grid = (S // tm, per, 1)
    a_spec = pl.BlockSpec((tm, K), lambda i, j, k, p: (i, 0))
    b_spec = pl.BlockSpec((None, K, tn), lambda i, j, k, p: (jnp.bitwise_xor(p[0], mask), 0, j))
    o_spec = pl.BlockSpec((tm, tn), lambda i, j, k, p: (i, jnp.bitwise_xor(p[0], mask) * per + j))
    v_spec = pl.BlockSpec((1, tn), lambda i, j, k, p: (0, jnp.bitwise_xor(p[0], mask) * per + j))

    def epi(res, ex, o):
        o[0][...] = res + ex[0][...]

    return _matmul(name, grid, NN, a, a_spec, wfull, b_spec, [jax.ShapeDtypeStruct((S, N), F32)], [o_spec], (tm, tn),
                   extras=(bias,), extra_specs=(v_spec,), epilogue=epi, job=job, fill_into=into, prefetch=(p_idx,))


def _copy_block(dst, src, p_idx, mask, name):
    _, K, n = dst.shape
    tr = _pick(K, max(SUBLANES, (1 << 19) // n), 2 * SUBLANES)

    def body(p_ref, s_ref, d_any, o_ref):
        o_ref[...] = s_ref[...]

    spec = pl.BlockSpec((None, tr, n), lambda i, p: (jnp.bitwise_xor(p[0], mask), i, 0))
    return _call(body, name, (K // tr,), [spec, pl.BlockSpec(memory_space=pl.ANY)], [spec],
                 [jax.ShapeDtypeStruct(dst.shape, dst.dtype)], [], ("parallel",), (src, dst), aliases={1: 0},
                 prefetch=(p_idx,))[0]


def _mm_fwd_rows(name, a, wrows, resid, gate, job=None):
    S, K = a.shape
    _, N = wrows.shape
    tm = _pick(S, 512, LANES)
    tk = _pick(K, 2816, LANES)
    tn = N if K * N * 2 <= WHOLE_WEIGHT_BYTES else _pick(N, 1024, LANES)
    grid = (S // tm, N // tn, K // tk)
    a_spec = pl.BlockSpec((tm, tk), lambda i, j, k: (i, k))
    if tn == N and tk == K:
        b_spec = pl.BlockSpec((tk, tn), lambda i, j, k: (k, j), pipeline_mode=pl.Buffered(1))
    else:
        b_spec = pl.BlockSpec((tk, tn), lambda i, j, k: (k, j))
    o_spec = pl.BlockSpec((tm, tn), lambda i, j, k: (i, j))
    g_spec = pl.BlockSpec((1, tn), lambda i, j, k: (0, j))

    def epi(res, ex, o):
        o[0][...] = res
        o[1][...] = ex[0][...] + ex[1][...] * res

    sd = jax.ShapeDtypeStruct((S, N), F32)
    return _matmul(name, grid, NN, a, a_spec, wrows, b_spec, [sd, sd], [o_spec, o_spec], (tm, tn),
                   extras=(resid, gate), extra_specs=(o_spec, g_spec), epilogue=epi, job=job)


def _mm_dgrad_cols(name, dy, wfull, out_dtype=F32, job=None, row_tiles=None, fill_into=None):
    planes, S, npl = dy.shape
    _, K, n = wfull.shape
    tm = _pick(S if row_tiles is None else S // 2, 1024, LANES)
    to = _pick(K, 1024, LANES)
    tk = _pick(n, 2816, LANES)
    per = n // tk
    nk = N_CHIPS * per
    pk = nk // planes
    n_tiles = S // tm
    i0, ni = (0, n_tiles) if row_tiles is None else (int(row_tiles[0] * n_tiles), int(row_tiles[1] * n_tiles))
    grid = (ni, K // to, nk)
    a_spec = pl.BlockSpec((None, tm, tk), lambda i, j, k: (k // pk, i + i0, k % pk))
    b_spec = pl.BlockSpec((None, to, tk), lambda i, j, k: (k // per, j, k % per))
    o_spec = pl.BlockSpec((tm, to), lambda i, j, k: (i + i0, j))
    return _first(_matmul(name, grid, NT, dy, a_spec, wfull, b_spec, [jax.ShapeDtypeStruct((S, K), out_dtype)],
                          [o_spec], (tm, to), job=job, fill_into=fill_into), job)


def _mm_dgrad_rows(name, dy, wrows, out_dtype=F32, job=None):
    S, N = dy.shape
    K, _ = wrows.shape
    tm = _pick(S, 1024, LANES)
    to = _pick(K, 1408, LANES)
    tk = _pick(N, 2048, LANES)
    grid = (S // tm, K // to, N // tk)
    a_spec = pl.BlockSpec((tm, tk), lambda i, j, k: (i, k))
    b_spec = pl.BlockSpec((to, tk), lambda i, j, k: (j, k))
    o_spec = pl.BlockSpec((tm, to), lambda i, j, k: (i, j))
    return _first(_matmul(name, grid, NT, dy, a_spec, wrows, b_spec, [jax.ShapeDtypeStruct((S, K), out_dtype)],
                          [o_spec], (tm, to), job=job), job)


def _mm_wgrad_cols(name, a, dy, n, job=None):
    S, K = a.shape
    planes, _, npl = dy.shape
    N = planes * npl
    to = _pick(K, 1024, LANES)
    tn = _pick(n, 1408, LANES)
    ts = _pick(S, 2048, LANES)
    per = n // tn
    nj = N // tn
    pj = nj // planes
    grid = (K // to, nj, S // ts)
    a_spec = pl.BlockSpec((ts, to), lambda i, j, k: (k, i))
    b_spec = pl.BlockSpec((None, ts, tn), lambda i, j, k: (j // pj, k, j % pj))
    o_spec = pl.BlockSpec((None, to, tn), lambda i, j, k: (j // per, i, j % per))
    return _first(_matmul(name, grid, TN, a, a_spec, dy, b_spec, [jax.ShapeDtypeStruct((N_CHIPS, K, n), F32)],
                          [o_spec], (to, tn), job=job), job)


def _mm_wgrad_rows(name, a, dy, kshard, job=None):
    S, K = a.shape
    _, N = dy.shape
    to = _pick(kshard, 1408, LANES)
    tn = _pick(N, 1024, LANES)
    ts = _pick(S, 2048, LANES)
    per = kshard // to
    grid = (K // to, N // tn, S // ts)
    a_spec = pl.BlockSpec((ts, to), lambda i, j, k: (k, i))
    b_spec = pl.BlockSpec((ts, tn), lambda i, j, k: (k, j))
    o_spec = pl.BlockSpec((None, to, tn), lambda i, j, k: (i // per, i % per, j))
    return _first(_matmul(name, grid, TN, a, a_spec, dy, b_spec, [jax.ShapeDtypeStruct((N_CHIPS, kshard, N), F32)],
                          [o_spec], (to, tn), job=job), job)


def _rows(tm, width, colblk=0):
    return pl.BlockSpec((tm, width), lambda i: (i, colblk))


def _whole(shape):
    nd = len(shape)
    return pl.BlockSpec(shape, lambda i: (0,) * nd)


def _prev_halo(tm, h, width, colblk=0):
    r = tm // h
    return pl.BlockSpec((h, width), lambda i: (jnp.maximum(i * r - 1, 0), colblk))


def _next_halo(tm, h, width, nblk, colblk=0):
    r = tm // h
    return pl.BlockSpec((h, width), lambda i: (jnp.minimum((i + 1) * r, nblk - 1), colblk))


def _accumulate(i, ref, val):
    @pl.when(i == 0)
    def _():
        ref[...] = val

    @pl.when(i > 0)
    def _():
        ref[...] += val


def _fill_rotations(rot, offs):
    n = rot.shape[1]
    for r in sorted({o % SUBLANES for o in offs} - {0}):
        rot[r, 0:n - SUBLANES, :] = rot[0, r:r + n - SUBLANES, :]


def _for_chunks(n_rows, rb, fn):
    def step(j, carry):
        fn(pl.multiple_of(j * rb, rb))
        return carry

    lax.fori_loop(0, n_rows // rb, step, 0)


def _modnorm_fwd(name, x, g, scale, shift):
    S, D = x.shape
    tm = _pick(S, 512, LANES)

    def body(x_ref, g_ref, sc_ref, sh_ref, h_ref):
        h_ref[...] = _modnorm(x_ref[...], g_ref[...], sc_ref[...], sh_ref[...]).astype(BF16)

    vec = _whole((1, D))
    return pl.pallas_call(
        body, name=name, grid=(S // tm,), in_specs=[_rows(tm, D), vec, vec, vec], out_specs=_rows(tm, D),
        out_shape=jax.ShapeDtypeStruct((S, D), BF16), compiler_params=_params(("parallel",)),
    )(x, g, scale, shift)


def _modnorm_bwd(name, x, dh, dx_in, g, scale, shift, gated=None):
    S, D = x.shape
    tm = _pick(S, 256, LANES)

    def body(x_ref, dh_ref, dxin_ref, g_ref, sc_ref, sh_ref, *rest):
        i = pl.program_id(0)
        dx_ref, dg_ref, dsc_ref, dsh_ref = rest[-4:] if gated is None else rest[2:6]
        _, pull = jax.vjp(_modnorm, x_ref[...], g_ref[...], sc_ref[...], sh_ref[...])
        dx, dg, dsc, dsh = pull(dh_ref[...])
        dx = dxin_ref[...] + dx
        dx_ref[...] = dx
        _accumulate(i, dg_ref, dg)
        _accumulate(i, dsc_ref, dsc)
        _accumulate(i, dsh_ref, dsh)
        if gated is not None:
            _gate_bwd_tile(i, dx, rest[0], rest[1], rest[6], rest[7])

    vec = _whole((1, D))
    row = _rows(tm, D)
    vsd = jax.ShapeDtypeStruct((1, D), F32)
    in_specs, args = [row, row, row, vec, vec, vec], (x, dh, dx_in, g, scale, shift)
    out_specs, out_shape = [row, vec, vec, vec], [jax.ShapeDtypeStruct((S, D), F32), vsd, vsd, vsd]
    if gated is not None:
        in_specs, args = in_specs + [row, vec], args + tuple(gated)
        out_specs, out_shape = out_specs + [row, vec], out_shape + [jax.ShapeDtypeStruct((S, D), BF16), vsd]
    return _call(body, name, (S // tm,), in_specs, out_specs, out_shape, [], ("arbitrary",), args)


def _conv_fwd(name, proj, w, b, lg, lb, dc, job=None):
    S = proj.shape[0]
    K = w.shape[0]
    H = CONV_HALO
    tm = _pick(S, 256, LANES)

    def glu(v):
        return v[:, :dc] * jax.nn.sigmoid(v[:, dc:])

    offs = [H - (K - 1) + k for k in range(K)]
    rb = _pick(tm, 64, SUBLANES)
    lw = min(LANES, dc)

    def body(cur_ref, prev_ref, w_ref, b_ref, lg_ref, lb_ref, conv_ref, act_ref, rot):
        i = pl.program_id(0)
        rot[0, 0:H, :] = jnp.where(i > 0, glu(prev_ref[...]), 0.0)
        rot[0, H:, :] = glu(cur_ref[...])
        _fill_rotations(rot, offs)

        def chunk(row0):
            for l0 in range(0, dc, lw):
                acc = jnp.broadcast_to(b_ref[:, l0:l0 + lw], (rb, lw))
                for k in range(K):
                    a, r = divmod(offs[k], SUBLANES)
                    acc = acc + w_ref[k:k + 1, l0:l0 + lw] * rot[r, pl.ds(row0 + SUBLANES * a, rb), l0:l0 + lw]
                conv_ref[pl.ds(row0, rb), l0:l0 + lw] = acc

        _for_chunks(tm, rb, chunk)
        act_ref[...] = _ln_silu(conv_ref[...], lg_ref[...], lb_ref[...]).astype(BF16)

    vec = _whole((1, dc))
    res = _call(body, name, (S // tm,),
                [_rows(tm, 2 * dc), _prev_halo(tm, H, 2 * dc), _whole(w.shape), vec, vec, vec],
                [_rows(tm, dc), _rows(tm, dc)],
                [jax.ShapeDtypeStruct((S, dc), F32), jax.ShapeDtypeStruct((S, dc), BF16)],
                [pltpu.VMEM((SUBLANES, tm + H, dc), F32)], ("parallel",), (proj, proj, w, b, lg, lb), job=job)
    return res if job is None else (res[:2], res[2:])


def _conv_bwd_ln(name, conv, dact, lg, lb):
    S, dc = conv.shape
    tm = _pick(S, 256, LANES)

    def body(c_ref, d_ref, lg_ref, lb_ref, dc_ref, dlg_ref, dlb_ref, db_ref):
        i = pl.program_id(0)
        _, pull = jax.vjp(_ln_silu, c_ref[...], lg_ref[...], lb_ref[...])
        dcv, dlg, dlb = pull(d_ref[...])
        dc_ref[...] = dcv
        _accumulate(i, dlg_ref, dlg)
        _accumulate(i, dlb_ref, dlb)
        _accumulate(i, db_ref, jnp.sum(dcv, axis=0, keepdims=True))

    vec = _whole((1, dc))
    row = _rows(tm, dc)
    vsd = jax.ShapeDtypeStruct((1, dc), F32)
    return pl.pallas_call(
        body, name=name, grid=(S // tm,), in_specs=[row, row, vec, vec], out_specs=[row, vec, vec, vec],
        out_shape=[jax.ShapeDtypeStruct((S, dc), F32), vsd, vsd, vsd],
        compiler_params=_params(("arbitrary",)),
    )(conv, dact, lg, lb)


def _conv_bwd(name, dconv, proj, w, dproj, dc, job=None):
    S = proj.shape[0]
    K = w.shape[0]
    H = CONV_HALO
    tm = _pick(S, 256, LANES)
    nt = S // tm

    offs_g = [H - (K - 1) + k for k in range(K)]
    offs_d = [K - 1 - k for k in range(K)]
    rb = _pick(tm, 64, SUBLANES)
    lw = min(LANES, dc)
    kp = -(-K // SUBLANES) * SUBLANES

    def body(d_ref, dn_ref, cur_ref, prev_ref, w_ref, dproj_any, da_ref, dw_ref, dbin_ref, rotg, rotd, accw, dglu_s):
        i = pl.program_id(0)
        pv = prev_ref[...]
        cv = cur_ref[...]
        sig = jax.nn.sigmoid(cv[:, dc:])
        rotg[0, 0:H, :] = jnp.where(i > 0, pv[:, :dc] * jax.nn.sigmoid(pv[:, dc:]), 0.0)
        rotg[0, H:, :] = cv[:, :dc] * sig
        rotd[0, 0:tm, :] = d_ref[...]
        rotd[0, tm:, :] = jnp.where(i < nt - 1, dn_ref[...], 0.0)
        _fill_rotations(rotg, offs_g)
        _fill_rotations(rotd, offs_d)

        @pl.when(i == 0)
        def _():
            accw[...] = jnp.zeros_like(accw)

        def chunk(row0):
            for l0 in range(0, dc, lw):
                lanes = slice(l0, l0 + lw)
                dcur = rotd[0, pl.ds(row0, rb), lanes]
                acc = jnp.zeros((rb, lw), F32)
                for k in range(K):
                    a, r = divmod(offs_d[k], SUBLANES)
                    acc = acc + w_ref[k:k + 1, lanes] * rotd[r, pl.ds(row0 + SUBLANES * a, rb), lanes]
                    a, r = divmod(offs_g[k], SUBLANES)
                    prod = dcur * rotg[r, pl.ds(row0 + SUBLANES * a, rb), lanes]
                    accw[SUBLANES * k:SUBLANES * (k + 1), lanes] += jnp.sum(
                        prod.reshape(rb // SUBLANES, SUBLANES, lw), axis=0)
                dglu_s[pl.ds(row0, rb), lanes] = acc

        _for_chunks(tm, rb, chunk)
        dglu = dglu_s[...]
        da = jnp.concatenate([dglu * sig, dglu * cv[:, :dc] * sig * (1.0 - sig)], axis=1)
        da_ref[...] = da.astype(BF16)
        _accumulate(i, dbin_ref, jnp.sum(da, axis=0, keepdims=True))

        @pl.when(i == nt - 1)
        def _():
            dw_ref[...] = jnp.zeros_like(dw_ref)
            for k in range(K):
                dw_ref[k:k + 1, :] = jnp.sum(accw[SUBLANES * k:SUBLANES * (k + 1), :], axis=0, keepdims=True)

    res = _call(
        body, name, (nt,),
        [_rows(tm, dc), _next_halo(tm, H, dc, S // H), _rows(tm, 2 * dc), _prev_halo(tm, H, 2 * dc),
         _whole(w.shape), pl.BlockSpec(memory_space=pl.ANY)],
        [_rows(tm, 2 * dc), _whole((kp, dc)), _whole((1, 2 * dc))],
        [jax.ShapeDtypeStruct(dproj.shape, BF16), jax.ShapeDtypeStruct((kp, dc), F32),
         jax.ShapeDtypeStruct((1, 2 * dc), F32)],
        [pltpu.VMEM((SUBLANES, tm + H, dc), F32), pltpu.VMEM((SUBLANES, tm + H, dc), F32),
         pltpu.VMEM((SUBLANES * K, dc), F32), pltpu.VMEM((tm, dc), F32)],
        ("arbitrary",), (dconv, dconv, proj, proj, w, dproj), aliases={5: 0}, job=job)
    return res if job is None else (res[:3], res[3:])


def _mix(vln, wsm, bst, out_ref, G, CH, hd):
    for n in range(vln.shape[0] // CH):
        for g in range(G):
            blk = vln[n * CH:(n + 1) * CH, g * hd:(g + 1) * hd].astype(BF16)
            out_ref[n * CH:(n + 1) * CH, g * hd:(g + 1) * hd] = (
                jnp.dot(wsm[g], blk, preferred_element_type=F32) + bst[:, g:g + 1])


def _sgu_fwd(name, proj, a_act, wco, wso, lg, lb, ws, bst, D, ds, job=None):
    S = proj.shape[0]
    dc = a_act.shape[1]
    G, CH, _ = ws.shape
    hd = ds // G
    nco = wco.shape[2]
    tm = _pick(S, 256, CH)

    def body(s_ref, gt_ref, a_ref, wco_ref, wso_ref, lg_ref, lb_ref, ws_ref, bst_ref,
             ya_ref, yb_ref, uv_ref, mg_ref, vmix):
        z = _gelu(s_ref[...])
        vln = _layer_norm(z[:, ds:], lg_ref[...], lb_ref[...])
        wsm = [_tril_mask(ws_ref[g]).astype(BF16) for g in range(G)]
        _mix(vln, wsm, bst_ref[...], vmix, G, CH, hd)
        uv = (z[:, :ds] * vmix[...]).astype(BF16)
        uv_ref[...] = uv
        a = a_ref[...]
        for p in range(N_CHIPS):
            ya_ref[:, p * nco:(p + 1) * nco] = jnp.dot(a, wco_ref[p], preferred_element_type=F32)
            yb_ref[:, p * nco:(p + 1) * nco] = jnp.dot(uv, wso_ref[p], preferred_element_type=F32)
        gt = gt_ref[...]
        mg_ref[...] = (jax.nn.sigmoid(gt[:, :D]) * ya_ref[...] + jax.nn.sigmoid(gt[:, D:]) * yb_ref[...]).astype(BF16)

    vec = _whole((1, ds))
    sdf = jax.ShapeDtypeStruct((S, D), F32)
    res = _call(body, name, (S // tm,),
                [_rows(tm, 2 * ds, 1), _rows(tm, 2 * D, 1), _rows(tm, dc), _whole(wco.shape), _whole(wso.shape),
                 vec, vec, _whole(ws.shape), _whole(bst.shape)],
                [_rows(tm, D), _rows(tm, D), _rows(tm, ds), _rows(tm, D)],
                [sdf, sdf, jax.ShapeDtypeStruct((S, ds), BF16), jax.ShapeDtypeStruct((S, D), BF16)],
                [pltpu.VMEM((tm, ds), F32)], ("parallel",), (proj, proj, a_act, wco, wso, lg, lb, ws, bst), job=job)
    return res if job is None else (res[:4], res[4:])


def _merge_bwd(name, dmerged, proj, ya, yb, D):
    S = proj.shape[0]
    tm = _pick(S, 256, LANES)

    def body(dm_ref, gt_ref, ya_ref, yb_ref, dya_ref, dyb_ref, dg_ref, dbin_ref):
        i = pl.program_id(0)
        dm = dm_ref[...]
        gt = gt_ref[...]
        sa = jax.nn.sigmoid(gt[:, :D])
        sb = jax.nn.sigmoid(gt[:, D:])
        dya_ref[...] = (dm * sa).astype(BF16)
        dyb_ref[...] = (dm * sb).astype(BF16)
        dg = jnp.concatenate([dm * ya_ref[...] * sa * (1.0 - sa), dm * yb_ref[...] * sb * (1.0 - sb)], axis=1)
        dg_ref[...] = dg.astype(BF16)
        _accumulate(i, dbin_ref, jnp.sum(dg, axis=0, keepdims=True))

    row = _rows(tm, D)
    sdb = jax.ShapeDtypeStruct((S, D), BF16)
    return pl.pallas_call(
        body, name=name, grid=(S // tm,),
        in_specs=[row, _rows(tm, 2 * D, 1), row, row],
        out_specs=[row, row, _rows(tm, 2 * D, 1), _whole((1, 2 * D))],
        out_shape=[sdb, sdb, jax.ShapeDtypeStruct((S, 4 * D), BF16), jax.ShapeDtypeStruct((1, 2 * D), F32)],
        compiler_params=_params(("arbitrary",)),
    )(dmerged, proj, ya, yb)


def _sgu_bwd(name, proj, duv, lg, lb, ws, bst, dproj, ds):
    S = proj.shape[0]
    G, CH, _ = ws.shape
    hd = ds // G
    tm = _pick(S, 256, CH)

    def body(s_ref, duv_ref, lg_ref, lb_ref, ws_ref, bst_ref, dproj_any,
             dsin_ref, dws_ref, dbs_ref, dlg_ref, dlb_ref, dbin_ref, vmix, dvln):
        i = pl.program_id(0)
        z, pull_gelu = jax.vjp(_gelu, s_ref[...])
        u = z[:, :ds]
        vln, pull_ln = jax.vjp(_layer_norm, z[:, ds:], lg_ref[...], lb_ref[...])
        wsm = [_tril_mask(ws_ref[g]).astype(BF16) for g in range(G)]
        _mix(vln, wsm, bst_ref[...], vmix, G, CH, hd)
        duv = duv_ref[...]
        du = duv * vmix[...]
        dvmix = duv * u
        for g in range(G):
            dws_g = jnp.zeros((CH, CH), F32)
            dbs_g = jnp.zeros((CH, 1), F32)
            for n in range(tm // CH):
                dblk = dvmix[n * CH:(n + 1) * CH, g * hd:(g + 1) * hd]
                vblk = vln[n * CH:(n + 1) * CH, g * hd:(g + 1) * hd].astype(BF16)
                dvln[n * CH:(n + 1) * CH, g * hd:(g + 1) * hd] = lax.dot_general(
                    wsm[g], dblk.astype(BF16), TN, preferred_element_type=F32)
                dws_g = dws_g + lax.dot_general(dblk.astype(BF16), vblk, NT, preferred_element_type=F32)
                dbs_g = dbs_g + jnp.sum(dblk, axis=1, keepdims=True)
            dws_g = _tril_mask(dws_g)
            dbs_g = jnp.broadcast_to(dbs_g, (CH, LANES))

            @pl.when(i == 0)
            def _():
                dws_ref[g] = dws_g
                dbs_ref[g] = dbs_g

            @pl.when(i > 0)
            def _():
                dws_ref[g] += dws_g
                dbs_ref[g] += dbs_g

        dv, dlg, dlb = pull_ln(dvln[...])
        (dsin,) = pull_gelu(jnp.concatenate([du, dv], axis=1))
        dsin_ref[...] = dsin.astype(BF16)
        _accumulate(i, dlg_ref, dlg)
        _accumulate(i, dlb_ref, dlb)
        _accumulate(i, dbin_ref, jnp.sum(dsin, axis=0, keepdims=True))

    vec = _whole((1, ds))
    vsd = jax.ShapeDtypeStruct((1, ds), F32)
    return pl.pallas_call(
        body, name=name, grid=(S // tm,),
        in_specs=[_rows(tm, 2 * ds, 1), _rows(tm, ds), vec, vec, _whole(ws.shape), _whole(bst.shape),
                  pl.BlockSpec(memory_space=pl.ANY)],
        out_specs=[_rows(tm, 2 * ds, 1), _whole((G, CH, CH)), _whole((G, CH, LANES)), vec, vec, _whole((1, 2 * ds))],
        out_shape=[jax.ShapeDtypeStruct(dproj.shape, BF16), jax.ShapeDtypeStruct((G, CH, CH), F32),
                   jax.ShapeDtypeStruct((G, CH, LANES), F32), vsd, vsd, jax.ShapeDtypeStruct((1, 2 * ds), F32)],
        scratch_shapes=[pltpu.VMEM((tm, ds), F32), pltpu.VMEM((tm, ds), F32)],
        input_output_aliases={6: 0},
        compiler_params=_params(("arbitrary",)),
    )(proj, duv, lg, lb, ws, bst, dproj)


def _silu_mul(val, gt):
    return jax.nn.silu(gt) * val


def _rotation_slots(offs):
    slot = {0: 0}
    for r in sorted({o % SUBLANES for o in offs} - {0}):
        slot[r] = len(slot)
    return slot


def _fill_plane_rotations(rot, slot):
    n = rot.shape[2]
    for r, s in slot.items():
        if r:
            rot[:, s, 0:n - SUBLANES, :] = rot[:, 0, r:r + n - SUBLANES, :]


def _ffn_tiles(S, Fh):
    return _pick(S, 256, LANES), _pick(Fh, 1408, LANES)


def _ffn_fwd(name, up, w, b):
    _, S, Fh = up.shape
    K = w.shape[1]
    H = FFN_HALO
    tm, cw = _ffn_tiles(S, Fh)
    r = tm // H

    offs = [H - (K - 1) + k for k in range(K)]
    slot = _rotation_slots(offs)
    rb = _pick(tm, 64, 2 * SUBLANES)
    lw = min(LANES, cw)

    def body(cur_ref, prev_ref, w_ref, b_ref, act_ref, rot):
        i = pl.program_id(1)
        rot[:, 0, 0:H, :] = jnp.where(i > 0, prev_ref[...], 0.0)
        rot[:, 0, H:, :] = cur_ref[...]
        _fill_plane_rotations(rot, slot)

        def chunk(row0):
            for l0 in range(0, cw, lw):
                lanes = slice(l0, l0 + lw)
                c2 = []
                for pln in range(2):
                    acc = jnp.broadcast_to(b_ref[pln, :, lanes], (rb, lw))
                    for k in range(K):
                        a, rr = divmod(offs[k], SUBLANES)
                        acc = acc + w_ref[pln, k:k + 1, lanes] * rot[pln, slot[rr], pl.ds(row0 + SUBLANES * a, rb), lanes]
                    c2.append(acc)
                act_ref[pl.ds(row0, rb), lanes] = _silu_mul(c2[0], c2[1]).astype(BF16)

        _for_chunks(tm, rb, chunk)

    return pl.pallas_call(
        body, name=name, grid=(Fh // cw, S // tm),
        in_specs=[pl.BlockSpec((2, tm, cw), lambda j, i: (0, i, j)),
                  pl.BlockSpec((2, H, cw), lambda j, i: (0, jnp.maximum(i * r - 1, 0), j)),
                  pl.BlockSpec((2, K, cw), lambda j, i: (0, 0, j)),
                  pl.BlockSpec((2, 1, cw), lambda j, i: (0, 0, j))],
        out_specs=pl.BlockSpec((tm, cw), lambda j, i: (i, j)),
        out_shape=jax.ShapeDtypeStruct((S, Fh), BF16),
        scratch_shapes=[pltpu.VMEM((2, len(slot), tm + H, cw), F32)],
        compiler_params=_params(("parallel", "parallel")),
    )(up, up, w, b)


def _ffn_bwd(name, up, dact, w, b, job=None):
    _, S, Fh = up.shape
    K = w.shape[1]
    H = FFN_HALO
    tm, cw = _ffn_tiles(S, Fh)
    r = tm // H
    nt = S // tm
    nhb = S // H
    te = tm + H

    offs_x = [H - (K - 1) + k for k in range(K)]
    offs_d = [K - 1 - k for k in range(K)]
    slot_x = _rotation_slots(offs_x)
    slot_d = _rotation_slots(offs_d)
    rb = _pick(tm, 64, 2 * SUBLANES)
    rbe = _pick(te, 96, SUBLANES)
    lw = min(LANES, cw)

    def body(cur_ref, prev_ref, next_ref, d_ref, dn_ref, w_ref, b_ref, dup_ref, dwb_ref, rotx, dext, rotd, accw):
        i = pl.program_id(1)
        rotx[:, 0, 0:H, :] = jnp.where(i > 0, prev_ref[...], 0.0)
        rotx[:, 0, H:H + tm, :] = cur_ref[...]
        rotx[:, 0, H + tm:, :] = jnp.where(i < nt - 1, next_ref[...], 0.0)
        dext[0:tm, :] = d_ref[...]
        dext[tm:, :] = jnp.where(i < nt - 1, dn_ref[...], 0.0)
        _fill_plane_rotations(rotx, slot_x)

        def chunk_e(row0):
            for l0 in range(0, cw, lw):
                lanes = slice(l0, l0 + lw)
                c2 = []
                for pln in range(2):
                    acc = jnp.broadcast_to(b_ref[pln, :, lanes], (rbe, lw))
                    for k in range(K):
                        a, rr = divmod(offs_x[k], SUBLANES)
                        acc = acc + w_ref[pln, k:k + 1, lanes] * rotx[pln, slot_x[rr], pl.ds(row0 + SUBLANES * a, rbe), lanes]
                    c2.append(acc)
                _, pull = jax.vjp(_silu_mul, c2[0], c2[1])
                dval, dgt = pull(dext[pl.ds(row0, rbe), lanes])
                rotd[0, 0, pl.ds(row0, rbe), lanes] = dval
                rotd[1, 0, pl.ds(row0, rbe), lanes] = dgt

        _for_chunks(te, rbe, chunk_e)
        _fill_plane_rotations(rotd, slot_d)

        @pl.when(i == 0)
        def _():
            accw[...] = jnp.zeros_like(accw)

        def chunk(row0):
            for l0 in range(0, cw, lw):
                lanes = slice(l0, l0 + lw)
                for pln in range(2):
                    dcur = rotd[pln, 0, pl.ds(row0, rb), lanes]
                    dup = jnp.zeros((rb, lw), F32)
                    for k in range(K):
                        a, rr = divmod(offs_d[k], SUBLANES)
                        dup = dup + w_ref[pln, k:k + 1, lanes] * rotd[pln, slot_d[rr], pl.ds(row0 + SUBLANES * a, rb), lanes]
                        a, rr = divmod(offs_x[k], SUBLANES)
                        prod = dcur * rotx[pln, slot_x[rr], pl.ds(row0 + SUBLANES * a, rb), lanes]
                        accw[pln, SUBLANES * k:SUBLANES * (k + 1), lanes] += jnp.sum(
                            prod.reshape(rb // SUBLANES, SUBLANES, lw), axis=0)
                    accw[pln, SUBLANES * K:SUBLANES * (K + 1), lanes] += jnp.sum(
                        dcur.reshape(rb // SUBLANES, SUBLANES, lw), axis=0)
                    dup_ref[pln, pl.ds(row0, rb), lanes] = dup.astype(BF16)

        _for_chunks(tm, rb, chunk)

        @pl.when(i == nt - 1)
        def _():
            dwb_ref[...] = jnp.zeros_like(dwb_ref)
            for pln in range(2):
                for k in range(K + 1):
                    dwb_ref[pln, k:k + 1, :] = jnp.sum(accw[pln, SUBLANES * k:SUBLANES * (k + 1), :], axis=0,
                                                       keepdims=True)

    res = _call(
        body, name, (Fh // cw, nt),
        [pl.BlockSpec((2, tm, cw), lambda j, i: (0, i, j)),
         pl.BlockSpec((2, H, cw), lambda j, i: (0, jnp.maximum(i * r - 1, 0), j)),
         pl.BlockSpec((2, H, cw), lambda j, i: (0, jnp.minimum((i + 1) * r, nhb - 1), j)),
         pl.BlockSpec((tm, cw), lambda j, i: (i, j)),
         pl.BlockSpec((H, cw), lambda j, i: (jnp.minimum((i + 1) * r, nhb - 1), j)),
         pl.BlockSpec((2, K, cw), lambda j, i: (0, 0, j)),
         pl.BlockSpec((2, 1, cw), lambda j, i: (0, 0, j))],
        [pl.BlockSpec((2, tm, cw), lambda j, i: (0, i, j)), pl.BlockSpec((2, SUBLANES, cw), lambda j, i: (0, 0, j))],
        [jax.ShapeDtypeStruct((2, S, Fh), BF16), jax.ShapeDtypeStruct((2, SUBLANES, Fh), F32)],
        [pltpu.VMEM((2, len(slot_x), tm + 2 * H, cw), F32), pltpu.VMEM((te, cw), F32),
         pltpu.VMEM((2, len(slot_d), te, cw), F32), pltpu.VMEM((2, SUBLANES * (K + 1), cw), F32)],
        ("parallel", "arbitrary"), (up, up, up, dact, dact, w, b), job=job)
    return res if job is None else (res[:2], res[2:])


def _rms(x, g):
    return x * lax.rsqrt(jnp.mean(x * x, axis=-1, keepdims=True) + EPS) * g


def _gate_bwd_tile(i, dx, out_ref, gate_ref, dout_ref, dgate_ref):
    dout_ref[...] = (dx * gate_ref[...]).astype(BF16)
    _accumulate(i, dgate_ref, jnp.sum(dx * out_ref[...], axis=0, keepdims=True))


def _final(name, x2, target, gf, out, gate):
    S, D = x2.shape
    tm = _pick(S, 256, LANES)

    def body(x_ref, t_ref, g_ref, o_ref, gate_ref, dx_ref, dg_ref, loss_ref, do_ref, dgate_ref):
        i = pl.program_id(0)
        y, pull = jax.vjp(_rms, x_ref[...], g_ref[...])
        e = y - t_ref[...]
        dx, dg = pull(e / D)
        dx_ref[...] = dx
        _accumulate(i, dg_ref, dg)
        part = 0.5 * jnp.sum(jnp.mean(jnp.square(e), axis=-1, keepdims=True), axis=0, keepdims=True)
        _accumulate(i, loss_ref, jnp.broadcast_to(part, (SUBLANES, LANES)))
        _gate_bwd_tile(i, dx, o_ref, gate_ref, do_ref, dgate_ref)

    row = _rows(tm, D)
    vec = _whole((1, D))
    vsd = jax.ShapeDtypeStruct((1, D), F32)
    return pl.pallas_call(
        body, name=name, grid=(S // tm,), in_specs=[row, row, vec, row, vec],
        out_specs=[row, vec, _whole((SUBLANES, LANES)), row, vec],
        out_shape=[jax.ShapeDtypeStruct((S, D), F32), vsd, jax.ShapeDtypeStruct((SUBLANES, LANES), F32),
                   jax.ShapeDtypeStruct((S, D), BF16), vsd],
        compiler_params=_params(("arbitrary",)),
    )(x2, target, gf, out, gate)


def _ada_fwd(name, c_pad, w_ada, b_cols):
    nb, D = c_pad.shape
    n = w_ada.shape[1]
    tn = _pick(n, 1024, LANES)

    def body(c_ref, w_ref, b_ref, o_ref):
        o_ref[...] = jnp.dot(jax.nn.silu(c_ref[...]).astype(BF16), w_ref[...].astype(BF16),
                             preferred_element_type=F32) + b_ref[...]

    return pl.pallas_call(
        body, name=name, grid=(n // tn,),
        in_specs=[_whole((nb, D)), pl.BlockSpec((D, tn), lambda j: (0, j)), pl.BlockSpec((1, tn), lambda j: (0, j))],
        out_specs=pl.BlockSpec((nb, tn), lambda j: (0, j)),
        out_shape=jax.ShapeDtypeStruct((nb, n), F32), compiler_params=_params(("parallel",)),
    )(c_pad, w_ada, b_cols)


def _ada_wgrad(name, c_t, dmod_cols):
    D, nb = c_t.shape
    n = dmod_cols.shape[1]
    tr = _pick(D, 256, SUBLANES)

    def body(c_ref, d_ref, o_ref):
        ca = jax.nn.silu(c_ref[...])
        acc = ca[:, 0:1] * d_ref[0:1, :]
        for b in range(1, nb):
            acc = acc + ca[:, b:b + 1] * d_ref[b:b + 1, :]
        o_ref[...] = acc

    return pl.pallas_call(
        body, name=name, grid=(D // tr,),
        in_specs=[pl.BlockSpec((tr, nb), lambda i: (i, 0)), _whole((nb, n))],
        out_specs=pl.BlockSpec((tr, n), lambda i: (i, 0)),
        out_shape=jax.ShapeDtypeStruct((D, n), F32), compiler_params=_params(("parallel",)),
    )(c_t, dmod_cols)


def kernel(x, c, w_ada, b_ada, norm1_g, w_in, b_in, conv_dw_w, conv_dw_b, conv_ln_g, conv_ln_b, w_conv_out, sgu_ln_g, sgu_ln_b, w_spatial, b_spatial, w_sgu_out, w_out, norm2_g, w_up, ffn_dw_w, ffn_dw_b, w_down, final_g, loss_target, m_w_ada, m_b_ada, m_norm1_g, m_w_in, m_b_in, m_conv_dw_w, m_conv_dw_b, m_conv_ln_g, m_conv_ln_b, m_w_conv_out, m_sgu_ln_g, m_sgu_ln_b, m_w_spatial, m_b_spatial, m_w_sgu_out, m_w_out, m_norm2_g, m_w_up, m_ffn_dw_w, m_ffn_dw_b, m_w_down, m_final_g, v_w_ada, v_b_ada, v_norm1_g, v_w_in, v_b_in, v_conv_dw_w, v_conv_dw_b, v_conv_ln_g, v_conv_ln_b, v_w_conv_out, v_sgu_ln_g, v_sgu_ln_b, v_w_spatial, v_b_spatial, v_w_sgu_out, v_w_out, v_norm2_g, v_w_up, v_ffn_dw_w, v_ffn_dw_b, v_w_down, v_final_g):
    S, D = x.shape[1], x.shape[2]
    dc = w_conv_out.shape[1]
    ds = w_sgu_out.shape[1]
    G, CH = w_spatial.shape[1], w_spatial.shape[2]
    KC = conv_dw_w.shape[1]
    KF = ffn_dw_w.shape[1]
    F2 = ffn_dw_b.shape[1]
    Fh = F2 // 2
    n_ada = w_ada.shape[2]
    n_up = w_up.shape[2]
    ax, ay, ac = _axes()
    chip = 2 * ax + ay
    me = 2 * chip + ac
    c_idx = jnp.reshape(ac, (1,)).astype(jnp.int32)
    p_idx = jnp.reshape(chip, (1,)).astype(jnp.int32)

    xs = x[0]
    tgt = loss_target[0]

    g1 = _allgather8(_pack([c[0], conv_dw_w[0], ffn_dw_w[0]]), "gather_small_in")
    parts = [_unpack(g1[2 * q], [(D,), conv_dw_w.shape[1:], ffn_dw_w.shape[1:]]) for q in range(N_CHIPS)]
    c_all = jnp.stack([_unpack(g1[d], [(D,)])[0] for d in range(N_DEV)])
    cw_full = jnp.concatenate([pt[1] for pt in parts], axis=1)
    fw_full = jnp.concatenate([pt[2] for pt in parts], axis=1)

    b_cols = lax.dynamic_slice(b_ada, (0, chip * n_ada), (1, n_ada))
    c_pad = jnp.concatenate([c_all, jnp.zeros_like(c_all)], axis=0)
    mod_blk = _ada_fwd("ada_fwd", c_pad, w_ada[0], b_cols)[:N_DEV]
    g2 = _allgather8(_pack([mod_blk]), "gather_mod")
    mod_all = jnp.concatenate([_unpack(g2[2 * q], [(N_DEV, n_ada)])[0] for q in range(N_CHIPS)], axis=1)
    mod = lax.dynamic_slice(mod_all, (me, 0), (1, 6 * D))
    shift1, scale1, gate1, shift2, scale2, gate2 = [mod[:, k * D:(k + 1) * D] for k in range(6)]

    shards = [w_in[0], w_conv_out[0], w_sgu_out[0], w_out[0], w_up[0], w_down[0]]
    names = ["in", "conv_out", "sgu_out", "out", "up", "down"]
    blk = {nm: _cast_into_block(s, p_idx, "cast_" + nm) for s, nm in zip(shards, names)}
    pc_idx = jnp.concatenate([p_idx, c_idx])

    h1 = _modnorm_fwd("modnorm1", xs, norm1_g, scale1, shift1)
    (proj,), (win_xy,) = _mm_fwd_block("proj_own", h1, blk["in"], b_in, p_idx, 0,
                                       job=_job_gather([blk["in"]], rel=(0, 1), fresh=True))
    (proj,), (win_d,) = _mm_fwd_block("proj_x", h1, win_xy, b_in, p_idx, 2, into=proj,
                                      job=_job_gather([blk["in"]], rel=(2,), fresh=True))
    (proj,), (wco_f, wso_f) = _mm_fwd_block("proj_y", h1, win_xy, b_in, p_idx, 1, into=proj,
                                            job=_job_gather([blk["conv_out"], blk["sgu_out"]]))
    (proj,), (wout_f,) = _mm_fwd_block("proj_diag", h1, win_d, b_in, p_idx, 3, into=proj,
                                       job=_job_gather([blk["out"]]))
    win_f = _copy_block(_copy_block(win_xy, blk["in"], p_idx, 0, "fill_w_in_own"), win_d, p_idx, 3, "fill_w_in_diag")
    wout_r = wout_f.reshape(-1, wout_f.shape[2])
    (conv, a_act), (wup_xy,) = _conv_fwd("conv_fwd", proj, cw_full, conv_dw_b, conv_ln_g, conv_ln_b, dc,
                                         job=_job_gather([blk["up"]], rel=(0, 1)))
    bst = jnp.transpose(b_spatial[0])
    (ya, yb, uv, merged), (wup_f,) = _sgu_fwd("sgu_fwd", proj, a_act, wco_f, wso_f, sgu_ln_g, sgu_ln_b, w_spatial[0],
                                              bst, D, ds, job=_job_gather([wup_xy], rel=(2,)))
    out1, x1 = _mm_fwd_rows("out1", merged, wout_r, xs, gate1)
    h2 = _modnorm_fwd("modnorm2", x1, norm2_g, scale2, shift2)
    up, (wdown_f,) = _mm_fwd("up", h2, wup_f, planes=2, job=_job_gather([blk["down"]]))
    wdown_r = wdown_f.reshape(-1, wdown_f.shape[2])
    fw2 = jnp.stack([fw_full[:, :Fh], fw_full[:, Fh:]])
    fb2 = jnp.stack([ffn_dw_b[:, :Fh], ffn_dw_b[:, Fh:]])
    act = _ffn_fwd("ffn_fwd", up, fw2, fb2)
    out2, x2 = _mm_fwd_rows("out2", act, wdown_r, x1, gate2)
    dx2, d_final_g, loss_blk, dout2, d_gate2 = _final("final", x2, tgt, final_g.reshape(1, D), out2, gate2)
    loss = lax.psum(loss_blk[0, 0], ("x", "y", "c"))

    def add_cores(nm, g, r1):
        return _add_own_half(g, r1, c_idx, "add_cores_" + nm)

    def add_chips(nm, g, r1, r2):
        return _add_chips(g, r1, r2, pc_idx, "add_chips_" + nm)

    g_wdown = _mm_wgrad_rows("wgrad_down", act, dout2, w_down.shape[1])
    dact, (r1_down,) = _mm_dgrad_rows("dgrad_down", dout2, wdown_r, job=_job_swap_halves([g_wdown]))
    s_down = add_cores("down", g_wdown, r1_down)
    (dup, d_ffn), (r2_down,) = _ffn_bwd("ffn_bwd", up, dact, fw2, fb2, job=_job_scatter_blocks([s_down]))
    h_down = add_chips("down", g_wdown, r1_down, r2_down)
    g_wup = _mm_wgrad_cols("wgrad_up", h2, dup, n_up)
    dh2, (r1_up,) = _mm_dgrad_cols("dgrad_up", dup, wup_f, job=_job_swap_halves([g_wup]))
    s_up = add_cores("up", g_wup, r1_up)
    dx1, d_norm2, d_scale2, d_shift2, dout1, d_gate1 = _modnorm_bwd(
        "modnorm2_bwd", x1, dh2, dx2, norm2_g, scale2, shift2, gated=(out1, gate1))
    g_wout = _mm_wgrad_rows("wgrad_out", merged, dout1, w_out.shape[1])
    dmerged = _mm_dgrad_rows("dgrad_out", dout1, wout_r)
    dya, dyb, dproj, dbin_g = _merge_bwd("merge_bwd", dmerged, proj, ya, yb, D)
    g_wco = _mm_wgrad_cols("wgrad_conv_out", a_act, dya[None], w_conv_out.shape[2])
    g_wso = _mm_wgrad_cols("wgrad_sgu_out", uv, dyb[None], w_sgu_out.shape[2])
    da_act, (r1_out, r1_co, r1_so) = _mm_dgrad_cols("dgrad_conv_out", dya[None], wco_f,
                                                    job=_job_swap_halves([g_wout, g_wco, g_wso]))
    s_out = add_cores("out", g_wout, r1_out)
    s_co = add_cores("conv_out", g_wco, r1_co)
    s_so = add_cores("sgu_out", g_wso, r1_so)
    duv = _mm_dgrad_cols("dgrad_sgu_out", dyb[None], wso_f)
    dproj, d_ws, d_bs, d_sgu_g, d_sgu_b, dbin_s = _sgu_bwd("sgu_bwd", proj, duv, sgu_ln_g, sgu_ln_b, w_spatial[0], bst,
                                                           dproj, ds)
    dconv, d_cln_g, d_cln_b, d_conv_b = _conv_bwd_ln("conv_ln_bwd", conv, da_act, conv_ln_g, conv_ln_b)
    d_fw = jnp.concatenate([d_ffn[0, :KF], d_ffn[1, :KF]], axis=1)
    d_fb = jnp.concatenate([d_ffn[0, KF:KF + 1], d_ffn[1, KF:KF + 1]], axis=1)
    early = [d_sgu_g, d_sgu_b, d_ws, d_bs[:, :, 0], d_norm2, d_fb, d_final_g, d_fw]
    (dproj, d_cw, dbin_a), (r2_up, g_early) = _conv_bwd(
        "conv_bwd", dconv, proj, cw_full, dproj, dc,
        job=_merge_jobs([_job_scatter_blocks([s_up]), _job_allgather8(_pack(early))]))
    h_up = add_chips("up", g_wup, r1_up, r2_up)
    g_win, (r2_out, r2_co, r2_so, sib_up, sib_down) = _mm_wgrad_cols(
        "wgrad_in", h1, dproj[None], w_in.shape[2],
        job=_merge_jobs([_job_scatter_blocks([s_out, s_co, s_so]), _job_to_sibling([h_up, h_down])]))
    h_out = add_chips("out", g_wout, r1_out, r2_out)
    h_co = add_chips("conv_out", g_wco, r1_co, r2_co)
    h_so = add_chips("sgu_out", g_wso, r1_so, r2_so)
    dh1, (r1_in, sib_out, sib_co, sib_so) = _mm_dgrad_cols(
        "dgrad_in_a", dproj[None], win_f, row_tiles=(0.0, 0.5),
        job=_merge_jobs([_job_swap_halves([g_win]), _job_to_sibling([h_out, h_co, h_so])]))
    s_in = add_cores("in", g_win, r1_in)
    dh1, (r2_in,) = _mm_dgrad_cols("dgrad_in_b", dproj[None], win_f, row_tiles=(0.5, 0.5), fill_into=dh1,
                                   job=_job_scatter_blocks([s_in]))
    h_in = add_chips("in", g_win, r1_in, r2_in)
    dxs, d_norm1, d_scale1, d_shift1 = _modnorm_bwd("modnorm1_bwd", xs, dh1, dx1, norm1_g, scale1, shift1)
    (sib_in,) = _run_job(_job_to_sibling([h_in]), "exchange_w_in")
    big_halves = {"w_in": (h_in, sib_in), "w_conv_out": (h_co, sib_co), "w_sgu_out": (h_so, sib_so),
                  "w_out": (h_out, sib_out), "w_up": (h_up, sib_up), "w_down": (h_down, sib_down)}

    d_mod = jnp.concatenate([d_shift1, d_scale1, d_gate1, d_shift2, d_scale2, d_gate2], axis=1)
    d_b_in = jnp.concatenate([dbin_a, dbin_s, dbin_g], axis=1)
    late = [d_mod, d_norm1, d_b_in, d_conv_b, d_cln_g, d_cln_b, d_cw[:KC]]
    g3 = _allgather8(_pack(late), "gather_small_grads")
    g_b_ada, g_norm1, g_b_in, g_conv_b, g_cln_g, g_cln_b, g_cw_full = _unpack(
        _sum8(g3, "sum_small_grads"), [a.shape for a in late])
    g_sgu_g, g_sgu_b, g_ws, g_bs, g_norm2, g_fb, g_final, g_fw_full = _unpack(
        _sum8(g_early, "sum_early_grads"), [a.shape for a in early])
    n_cw = conv_dw_w.shape[2]
    n_fw = ffn_dw_w.shape[2]
    g_cw = lax.dynamic_slice(g_cw_full, (0, chip * n_cw), (KC, n_cw))
    g_fw = lax.dynamic_slice(g_fw_full, (0, chip * n_fw), (KF, n_fw))
    dmod_all = jnp.stack([_unpack(g3[d], [(6 * D,)])[0] for d in range(N_DEV)])
    dmod_cols = lax.dynamic_slice(dmod_all, (0, chip * n_ada), (N_DEV, n_ada))
    g_wada = _ada_wgrad("ada_wgrad", jnp.transpose(c_all), dmod_cols)

    grads = {
        "w_ada": g_wada[None], "b_ada": g_b_ada, "norm1_g": g_norm1, "b_in": g_b_in,
        "conv_dw_w": g_cw[None], "conv_dw_b": g_conv_b, "conv_ln_g": g_cln_g, "conv_ln_b": g_cln_b,
        "sgu_ln_g": g_sgu_g, "sgu_ln_b": g_sgu_b, "w_spatial": g_ws[None],
        "b_spatial": g_bs[None], "norm2_g": g_norm2, "ffn_dw_w": g_fw[None], "ffn_dw_b": g_fb,
        "final_g": g_final.reshape(D),
    }
    weights = dict(w_ada=w_ada, b_ada=b_ada, norm1_g=norm1_g, w_in=w_in, b_in=b_in, conv_dw_w=conv_dw_w, conv_dw_b=conv_dw_b, conv_ln_g=conv_ln_g, conv_ln_b=conv_ln_b, w_conv_out=w_conv_out, sgu_ln_g=sgu_ln_g, sgu_ln_b=sgu_ln_b, w_spatial=w_spatial, b_spatial=b_spatial, w_sgu_out=w_sgu_out, w_out=w_out, norm2_g=norm2_g, w_up=w_up, ffn_dw_w=ffn_dw_w, ffn_dw_b=ffn_dw_b, w_down=w_down, final_g=final_g)
    m_in = dict(w_ada=m_w_ada, b_ada=m_b_ada, norm1_g=m_norm1_g, w_in=m_w_in, b_in=m_b_in, conv_dw_w=m_conv_dw_w, conv_dw_b=m_conv_dw_b, conv_ln_g=m_conv_ln_g, conv_ln_b=m_conv_ln_b, w_conv_out=m_w_conv_out, sgu_ln_g=m_sgu_ln_g, sgu_ln_b=m_sgu_ln_b, w_spatial=m_w_spatial, b_spatial=m_b_spatial, w_sgu_out=m_w_sgu_out, w_out=m_w_out, norm2_g=m_norm2_g, w_up=m_w_up, ffn_dw_w=m_ffn_dw_w, ffn_dw_b=m_ffn_dw_b, w_down=m_w_down, final_g=m_final_g)
    v_in = dict(w_ada=v_w_ada, b_ada=v_b_ada, norm1_g=v_norm1_g, w_in=v_w_in, b_in=v_b_in, conv_dw_w=v_conv_dw_w, conv_dw_b=v_conv_dw_b, conv_ln_g=v_conv_ln_g, conv_ln_b=v_conv_ln_b, w_conv_out=v_w_conv_out, sgu_ln_g=v_sgu_ln_g, sgu_ln_b=v_sgu_ln_b, w_spatial=v_w_spatial, b_spatial=v_b_spatial, w_sgu_out=v_w_sgu_out, w_out=v_w_out, norm2_g=v_norm2_g, w_up=v_w_up, ffn_dw_w=v_ffn_dw_w, ffn_dw_b=v_ffn_dw_b, w_down=v_w_down, final_g=v_final_g)
    order = list(weights.keys())
    large = ["w_ada", "w_in", "w_conv_out", "w_sgu_out", "w_out", "w_up", "w_down"]
    little = [n for n in order if n not in large]
    delta, new_m, new_v = {}, {}, {}
    for n in large:
        shp = weights[n].shape
        two = (shp[1], shp[2])
        if n in big_halves:
            g_, d_, m_, v_ = _adamw_halves(weights[n].reshape(two), big_halves[n][0], big_halves[n][1],
                                           m_in[n].reshape(two), v_in[n].reshape(two), c_idx, "adamw_" + n)
            grads[n] = g_.reshape(shp)
        else:
            d_, m_, v_ = _adamw(weights[n].reshape(two), grads[n].reshape(two), m_in[n].reshape(two),
                                v_in[n].reshape(two), "adamw_" + n)
        delta[n], new_m[n], new_v[n] = d_.reshape(shp), m_.reshape(shp), v_.reshape(shp)
    shapes = [weights[n].shape for n in little]
    d_, m_, v_ = _adamw(_pack([weights[n] for n in little]), _pack([grads[n] for n in little]),
                        _pack([m_in[n] for n in little]), _pack([v_in[n] for n in little]), "adamw_small")
    for n, dd, mm, vv in zip(little, _unpack(d_, shapes), _unpack(m_, shapes), _unpack(v_, shapes)):
        delta[n], new_m[n], new_v[n] = dd, mm, vv
    grad_out = [grads[n].reshape(weights[n].shape) for n in order]
    return (loss, dxs[None], *grad_out, *[delta[n] for n in order], *[new_m[n] for n in order],
            *[new_v[n] for n in order])
```

```python
import functools

import jax
import jax.numpy as jnp
from jax import lax
from jax.experimental import pallas as pl
from jax.experimental.pallas import tpu as pltpu

F32 = jnp.float32
BF16 = jnp.bfloat16
EPS = 1e-6
MESH = pl.DeviceIdType.MESH
N_CHIPS = 4
N_DEV = 8
LANES = 128
SUBLANES = 8
CONV_HALO = 32
FFN_HALO = 8
CONV_CHUNK_ROWS = 32
VMEM_LIMIT_BYTES = 56 * 1024 * 1024
WHOLE_WEIGHT_BYTES = 8 * 1024 * 1024

ADAM_LR = 0.001
ADAM_B1 = 0.9
ADAM_B2 = 0.999
ADAM_EPS = 1e-08
ADAM_WD = 0.01
ADAM_STEP = 10

NN = (((1,), (0,)), ((), ()))
NT = (((1,), (1,)), ((), ()))
TN = (((0,), (0,)), ((), ()))


def _params(sem=None):
    return pltpu.CompilerParams(dimension_semantics=sem, vmem_limit_bytes=VMEM_LIMIT_BYTES)


def _pick(dim, pref, mult):
    best = None
    d = mult
    while d <= min(dim, pref):
        if dim % d == 0:
            best = d
        d += mult
    return dim if best is None else best


def _axes():
    return lax.axis_index("x"), lax.axis_index("y"), lax.axis_index("c")


def _modnorm(x, g, scale, shift):
    r = lax.rsqrt(jnp.mean(x * x, axis=-1, keepdims=True) + EPS)
    return (x * r * g) * (1.0 + scale) + shift


def _layer_norm(x, g, b):
    mu = jnp.mean(x, axis=-1, keepdims=True)
    var = jnp.mean(jnp.square(x - mu), axis=-1, keepdims=True)
    return (x - mu) * lax.rsqrt(var + EPS) * g + b


def _gelu(x):
    return 0.5 * x * (1.0 + lax.erf(x * (0.5 ** 0.5)))


def _ln_silu(x, g, b):
    return jax.nn.silu(_layer_norm(x, g, b))


def _tril_mask(ws):
    n = ws.shape[-1]
    row = lax.broadcasted_iota(jnp.int32, (n, n), 0)
    col = lax.broadcasted_iota(jnp.int32, (n, n), 1)
    return jnp.where(row >= col, ws, 0.0)


def _pack(arrs):
    flat = [a.reshape(-1).astype(F32) for a in arrs]
    total = sum(f.shape[0] for f in flat)
    tile = SUBLANES * LANES
    padded = -(-total // tile) * tile
    if padded > total:
        flat = flat + [jnp.zeros((padded - total,), F32)]
    return jnp.concatenate(flat).reshape(padded // LANES, LANES)


def _unpack(buf, shapes):
    flat = buf.reshape(-1)
    out, off = [], 0
    for s in shapes:
        n = 1
        for d in s:
            n *= d
        out.append(flat[off:off + n].reshape(s))
        off += n
    return out


def _allgather8(buf, name):
    R, L = buf.shape

    def body(in_ref, out_ref, send_sems, recv_sems, local_sem):
        x, y, c = _axes()
        me = 4 * x + 2 * y + c
        mine = pltpu.make_async_copy(in_ref, out_ref.at[me], local_sem)
        mine.start()
        peers = []
        for k in range(1, N_DEV):
            px = 1 - x if k & 4 else x
            py = 1 - y if k & 2 else y
            pc = 1 - c if k & 1 else c
            peers.append((px, py, pc))
        sends = []
        for k, peer in enumerate(peers):
            cp = pltpu.make_async_remote_copy(
                src_ref=in_ref, dst_ref=out_ref.at[me], send_sem=send_sems.at[k], recv_sem=recv_sems.at[k],
                device_id=peer, device_id_type=MESH)
            cp.start()
            sends.append(cp)
        for k, (px, py, pc) in enumerate(peers):
            pltpu.make_async_remote_copy(
                src_ref=in_ref, dst_ref=out_ref.at[4 * px + 2 * py + pc], send_sem=send_sems.at[k],
                recv_sem=recv_sems.at[k], device_id=(px, py, pc), device_id_type=MESH).wait_recv()
        for cp in sends:
            cp.wait_send()
        mine.wait()

    return pl.pallas_call(
        body, name=name,
        out_shape=jax.ShapeDtypeStruct((N_DEV, R, L), buf.dtype),
        in_specs=[pl.BlockSpec(memory_space=pltpu.VMEM)],
        out_specs=pl.BlockSpec(memory_space=pltpu.VMEM),
        scratch_shapes=[pltpu.SemaphoreType.DMA((N_DEV - 1,)), pltpu.SemaphoreType.DMA((N_DEV - 1,)),
                        pltpu.SemaphoreType.DMA],
        compiler_params=pltpu.CompilerParams(vmem_limit_bytes=VMEM_LIMIT_BYTES),
    )(buf)


def _other_chips(x, y):
    return [(1 - x, y), (x, 1 - y), (1 - x, 1 - y)]


def _cast_into_block(shard, p_idx, name):
    K, n = shard.shape
    tr = _pick(K, max(SUBLANES, (1 << 19) // n), 2 * SUBLANES)

    def body(p_ref, s_ref, o_ref):
        o_ref[...] = s_ref[...].astype(BF16)

    return pl.pallas_call(
        body, name=name,
        grid_spec=pltpu.PrefetchScalarGridSpec(
            num_scalar_prefetch=1, grid=(K // tr,),
            in_specs=[pl.BlockSpec((tr, n), lambda i, p: (i, 0))],
            out_specs=pl.BlockSpec((None, tr, n), lambda i, p: (p[0], i, 0))),
        out_shape=jax.ShapeDtypeStruct((N_CHIPS, K, n), BF16),
        compiler_params=_params(("parallel",)),
    )(p_idx, shard)


class _Job:
    def __init__(self, ins, outs, aliases, n_sems, make):
        self.ins, self.outs, self.aliases, self.n_sems, self.make = list(ins), list(outs), list(aliases), n_sems, make


def _merge_jobs(jobs):
    ins, outs, aliases, offs = [], [], [], []
    n_sems = 0
    for jb in jobs:
        offs.append((len(ins), len(outs), n_sems))
        aliases += [(len(ins) + a, len(outs) + b) for a, b in jb.aliases]
        ins += jb.ins
        outs += jb.outs
        n_sems += jb.n_sems

    def make(in_refs, out_refs, send_sems, recv_sems, base=0):
        made = []
        for jb, (oi, oo, os_) in zip(jobs, offs):
            made.append(jb.make(in_refs[oi:oi + len(jb.ins)], out_refs[oo:oo + len(jb.outs)],
                                send_sems, recv_sems, base + os_))

        def start():
            for st, _ in made:
                st()

        def finish():
            for _, fin in made:
                fin()

        return start, finish

    return _Job(ins, outs, aliases, n_sems, make)


def _job_gather(fulls, rel=(0, 1, 2), fresh=False):
    nw = len(fulls)

    def make(in_refs, outs, send_sems, recv_sems, base=0):
        x, y, c = _axes()
        p = 2 * x + y
        chips = _other_chips(x, y)
        srcs = in_refs if fresh else outs

        def rows(w, mine):
            kh = outs[w].shape[1] // 2
            return pl.ds((c if mine else 1 - c) * kh, kh)

        def over_ici(w, j, block):
            qx, qy = chips[j]
            return pltpu.make_async_remote_copy(
                src_ref=srcs[w].at[block, rows(w, True)], dst_ref=outs[w].at[block, rows(w, True)],
                send_sem=send_sems.at[base + 6 * w + j], recv_sem=recv_sems.at[base + 6 * w + j],
                device_id=(qx, qy, c), device_id_type=MESH)

        def over_d2d(w, j, mine):
            qx, qy = chips[j]
            return pltpu.make_async_remote_copy(
                src_ref=outs[w].at[2 * qx + qy, rows(w, mine)], dst_ref=outs[w].at[2 * qx + qy, rows(w, mine)],
                send_sem=send_sems.at[base + 6 * w + 3 + j], recv_sem=recv_sems.at[base + 6 * w + 3 + j],
                device_id=(x, y, 1 - c), device_id_type=MESH)

        def start():
            for w in range(nw):
                for j in rel:
                    over_ici(w, j, p).start()

        def finish():
            for w in range(nw):
                for j in rel:
                    qx, qy = chips[j]
                    over_ici(w, j, 2 * qx + qy).wait_recv()
                    over_d2d(w, j, True).start()
            for w in range(nw):
                for j in rel:
                    over_d2d(w, j, False).wait_recv()
            for w in range(nw):
                for j in rel:
                    over_ici(w, j, p).wait_send()
                    over_d2d(w, j, True).wait_send()

        return start, finish

    return _Job(fulls, [jax.ShapeDtypeStruct(f.shape, f.dtype) for f in fulls],
                [] if fresh else [(w, w) for w in range(nw)], 6 * nw, make)


def _job_simple(ins, outs, n_per, copies_of):
    nw = len(ins)

    def make(in_refs, out_refs, send_sems, recv_sems, base=0):
        x, y, c = _axes()
        copies = []
        for w in range(nw):
            for j, (src, dst, dev) in enumerate(copies_of(w, in_refs[w], out_refs[w], x, y, c)):
                k = base + n_per * w + j
                copies.append(pltpu.make_async_remote_copy(
                    src_ref=src, dst_ref=dst, send_sem=send_sems.at[k], recv_sem=recv_sems.at[k],
                    device_id=dev, device_id_type=MESH))

        def start():
            for cp in copies:
                cp.start()

        def finish():
            for cp in copies:
                cp.wait()

        return start, finish

    return _Job(ins, outs, [], n_per * nw, make)


def _job_swap_halves(grads):
    def copies_of(w, src, dst, x, y, c):
        kh = src.shape[1] // 2
        return [(src.at[:, pl.ds((1 - c) * kh, kh), :], dst, (x, y, 1 - c))]

    outs = [jax.ShapeDtypeStruct((g.shape[0], g.shape[1] // 2, g.shape[2]), g.dtype) for g in grads]
    return _job_simple(grads, outs, 1, copies_of)


def _job_scatter_blocks(sums):
    def copies_of(w, src, dst, x, y, c):
        return [(src.at[2 * qx + qy], dst.at[j], (qx, qy, c)) for j, (qx, qy) in enumerate(_other_chips(x, y))]

    outs = [jax.ShapeDtypeStruct((3,) + s.shape[1:], s.dtype) for s in sums]
    return _job_simple(sums, outs, 3, copies_of)


def _job_to_sibling(arrs):
    def copies_of(w, src, dst, x, y, c):
        return [(src, dst, (x, y, 1 - c))]

    outs = [jax.ShapeDtypeStruct(a.shape, a.dtype) for a in arrs]
    return _job_simple(arrs, outs, 1, copies_of)


def _job_allgather8(buf):
    def make(in_refs, out_refs, send_sems, recv_sems, base=0):
        x, y, c = _axes()
        me = 4 * x + 2 * y + c
        src, dst = in_refs[0], out_refs[0]
        peers = [(1 - x if k & 4 else x, 1 - y if k & 2 else y, 1 - c if k & 1 else c) for k in range(1, N_DEV)]
        mine = pltpu.make_async_copy(src, dst.at[me], send_sems.at[base + N_DEV - 1])

        def to(k):
            return pltpu.make_async_remote_copy(
                src_ref=src, dst_ref=dst.at[me], send_sem=send_sems.at[base + k], recv_sem=recv_sems.at[base + k],
                device_id=peers[k], device_id_type=MESH)

        def of(k):
            px, py, pc = peers[k]
            return pltpu.make_async_remote_copy(
                src_ref=src, dst_ref=dst.at[4 * px + 2 * py + pc], send_sem=send_sems.at[base + k],
                recv_sem=recv_sems.at[base + k], device_id=peers[k], device_id_type=MESH)

        def start():
            mine.start()
            for k in range(N_DEV - 1):
                to(k).start()

        def finish():
            for k in range(N_DEV - 1):
                of(k).wait_recv()
                to(k).wait_send()
            mine.wait()

        return start, finish

    return _Job([buf], [jax.ShapeDtypeStruct((N_DEV,) + buf.shape, buf.dtype)], [], N_DEV, make)


def _run_job(job, name):
    ni, no = len(job.ins), len(job.outs)

    def body(*refs):
        start, finish = job.make(refs[:ni], refs[ni:ni + no], refs[-2], refs[-1])
        start()
        finish()

    any_spec = pl.BlockSpec(memory_space=pl.ANY)
    return pl.pallas_call(
        body, name=name, out_shape=job.outs, in_specs=[any_spec] * ni, out_specs=[any_spec] * no,
        input_output_aliases=dict(job.aliases),
        scratch_shapes=[pltpu.SemaphoreType.DMA((job.n_sems,)), pltpu.SemaphoreType.DMA((job.n_sems,))],
    )(*job.ins)


def _call(body, name, grid, in_specs, out_specs, out_shape, scratch_shapes, semantics, args, aliases=None, job=None,
          prefetch=()):
    in_specs, out_specs, out_shape = list(in_specs), list(out_specs), list(out_shape)
    scratch_shapes = list(scratch_shapes)
    n_pre, n_in, n_out, n_scr = len(prefetch), len(args), len(out_shape), len(scratch_shapes)
    all_aliases = {n_pre + a: b for a, b in (aliases or {}).items()}
    if job is None:
        wrapped, extra_in, semantics = body, [], semantics
    else:
        ni, no = len(job.ins), len(job.outs)

        def wrapped(*refs):
            pre, refs = refs[:n_pre], refs[n_pre:]
            ins, cins = refs[:n_in], refs[n_in:n_in + ni]
            outs, couts = refs[n_in + ni:n_in + ni + n_out], refs[n_in + ni + n_out:n_in + ni + n_out + no]
            scr = refs[n_in + ni + n_out + no:n_in + ni + n_out + no + n_scr]
            start, finish = job.make(cins, couts, refs[-2], refs[-1])
            first = functools.reduce(jnp.logical_and, [pl.program_id(a) == 0 for a in range(len(grid))])
            last = functools.reduce(jnp.logical_and, [pl.program_id(a) == grid[a] - 1 for a in range(len(grid))])
            pl.when(first)(start)
            body(*pre, *ins, *outs, *scr)
            pl.when(last)(finish)

        any_spec = pl.BlockSpec(memory_space=pl.ANY)
        for a, b in job.aliases:
            all_aliases[n_pre + n_in + a] = n_out + b
        in_specs, out_specs, out_shape = in_specs + [any_spec] * ni, out_specs + [any_spec] * no, out_shape + job.outs
        scratch_shapes = scratch_shapes + [pltpu.SemaphoreType.DMA((job.n_sems,)), pltpu.SemaphoreType.DMA((job.n_sems,))]
        extra_in, semantics = job.ins, tuple("arbitrary" for _ in grid)
    if n_pre:
        return pl.pallas_call(
            wrapped, name=name,
            grid_spec=pltpu.PrefetchScalarGridSpec(num_scalar_prefetch=n_pre, grid=grid, in_specs=in_specs,
                                                   out_specs=out_specs, scratch_shapes=scratch_shapes),
            out_shape=out_shape, input_output_aliases=all_aliases,
            compiler_params=_params(semantics))(*prefetch, *args, *extra_in)
    return pl.pallas_call(
        wrapped, name=name, grid=grid, in_specs=in_specs, out_specs=out_specs, out_shape=out_shape,
        scratch_shapes=scratch_shapes, input_output_aliases=all_aliases,
        compiler_params=_params(semantics))(*args, *extra_in)


def _add_own_half(g, r, c_idx, name):
    nb, K, n = g.shape
    kh = K // 2
    tr = _pick(kh, max(SUBLANES, (1 << 19) // n), 2 * SUBLANES)
    per = kh // tr

    def body(c_ref, g_ref, r_ref, o_ref):
        o_ref[...] = (g_ref[...] + r_ref[...]).astype(BF16)

    return pl.pallas_call(
        body, name=name,
        grid_spec=pltpu.PrefetchScalarGridSpec(
            num_scalar_prefetch=1, grid=(nb, per),
            in_specs=[pl.BlockSpec((None, tr, n), lambda b, i, c: (b, c[0] * per + i, 0)),
                      pl.BlockSpec((None, tr, n), lambda b, i, c: (b, i, 0))],
            out_specs=pl.BlockSpec((None, tr, n), lambda b, i, c: (b, i, 0))),
        out_shape=jax.ShapeDtypeStruct((nb, kh, n), BF16),
        compiler_params=_params(("parallel", "parallel")),
    )(c_idx, g, r)


def _add_chips(g, r1, r2, pc_idx, name):
    _, K, n = g.shape
    kh = K // 2
    tr = _pick(kh, max(SUBLANES, (1 << 19) // n), 2 * SUBLANES)
    per = kh // tr

    def body(pc_ref, g_ref, r1_ref, r2_ref, o_ref):
        own = g_ref[...] + r1_ref[...]
        o_ref[...] = ((own + r2_ref[0].astype(F32)) + r2_ref[1].astype(F32)) + r2_ref[2].astype(F32)

    return pl.pallas_call(
        body, name=name,
        grid_spec=pltpu.PrefetchScalarGridSpec(
            num_scalar_prefetch=1, grid=(per,),
            in_specs=[pl.BlockSpec((None, tr, n), lambda i, pc: (pc[0], pc[1] * per + i, 0)),
                      pl.BlockSpec((None, tr, n), lambda i, pc: (pc[0], i, 0)),
                      pl.BlockSpec((3, tr, n), lambda i, pc: (0, i, 0))],
            out_specs=pl.BlockSpec((tr, n), lambda i, pc: (i, 0))),
        out_shape=jax.ShapeDtypeStruct((kh, n), F32),
        compiler_params=_params(("parallel",)),
    )(pc_idx, g, r1, r2)


def _sum8(g, name):
    _, R, L = g.shape

    def body(g_ref, o_ref):
        acc = g_ref[0]
        for d in range(1, N_DEV):
            acc = acc + g_ref[d]
        o_ref[...] = acc

    return pl.pallas_call(
        body, name=name, out_shape=jax.ShapeDtypeStruct((R, L), F32),
        in_specs=[pl.BlockSpec(memory_space=pltpu.VMEM)], out_specs=pl.BlockSpec(memory_space=pltpu.VMEM),
        compiler_params=_params(),
    )(g)


def _adamw_math(w, gg, m, v):
    nm = ADAM_B1 * m + (1.0 - ADAM_B1) * gg
    nv = ADAM_B2 * v + (1.0 - ADAM_B2) * jnp.square(gg)
    m_hat = nm / (1.0 - ADAM_B1 ** ADAM_STEP)
    v_hat = nv / (1.0 - ADAM_B2 ** ADAM_STEP)
    return -ADAM_LR * (m_hat / (jnp.sqrt(v_hat) + ADAM_EPS) + ADAM_WD * w), nm, nv


def _adamw(w, g, m, v, name):
    R, C = w.shape
    tr = _pick(R, max(SUBLANES, (1 << 18) // C), SUBLANES)

    def body(w_ref, g_ref, m_ref, v_ref, d_ref, nm_ref, nv_ref):
        d_ref[...], nm_ref[...], nv_ref[...] = _adamw_math(w_ref[...], g_ref[...], m_ref[...], v_ref[...])

    spec = pl.BlockSpec((tr, C), lambda i: (i, 0))
    sd = jax.ShapeDtypeStruct((R, C), F32)
    return pl.pallas_call(
        body, name=name, grid=(R // tr,), in_specs=[spec] * 4, out_specs=[spec] * 3, out_shape=[sd] * 3,
        compiler_params=_params(("parallel",)),
    )(w, g, m, v)


def _adamw_halves(w, g_own, g_sib, m, v, c_idx, name):
    K, n = w.shape
    kh = K // 2
    tr = _pick(kh, max(SUBLANES, (1 << 18) // n), SUBLANES)
    per = kh // tr

    def body(c_ref, w_ref, go_ref, gs_ref, m_ref, v_ref, g_ref, d_ref, nm_ref, nv_ref):
        h = pl.program_id(0)

        def step(gg):
            g_ref[...] = gg
            d_ref[...], nm_ref[...], nv_ref[...] = _adamw_math(w_ref[...], gg, m_ref[...], v_ref[...])

        @pl.when(h == 0)
        def _():
            step(go_ref[...])

        @pl.when(h == 1)
        def _():
            step(gs_ref[...])

    full = pl.BlockSpec((tr, n), lambda h, i, c: (((c[0] + h) % 2) * per + i, 0))
    own = pl.BlockSpec((tr, n), lambda h, i, c: (i * (1 - h), 0))
    sib = pl.BlockSpec((tr, n), lambda h, i, c: (i * h, 0))
    sd = jax.ShapeDtypeStruct((K, n), F32)
    return pl.pallas_call(
        body, name=name,
        grid_spec=pltpu.PrefetchScalarGridSpec(
            num_scalar_prefetch=1, grid=(2, per),
            in_specs=[full, own, sib, full, full], out_specs=[full] * 4),
        out_shape=[sd] * 4,
        compiler_params=_params(("arbitrary", "arbitrary")),
    )(c_idx, w, g_own, g_sib, m, v)


def _matmul(name, grid, dims, a, a_spec, b, b_spec, outs, out_specs, acc_shape,
            extras=(), extra_specs=(), epilogue=None, job=None, fill_into=None, prefetch=()):
    nk = grid[2]
    npre = len(prefetch)
    aliases = None
    if fill_into is not None:
        aliases = {2 + len(extras): 0}
        extras = tuple(extras) + (fill_into,)
        extra_specs = tuple(extra_specs) + (pl.BlockSpec(memory_space=pl.ANY),)
    nex = len(extras)
    nout = len(outs)

    def body(*refs):
        a_ref, b_ref, rest = refs[npre], refs[npre + 1], refs[npre + 2:]
        ex, o = rest[:nex], rest[nex:nex + nout]
        part = lax.dot_general(a_ref[...].astype(BF16), b_ref[...].astype(BF16), dims,
                               preferred_element_type=F32)

        def finish(res):
            if epilogue is None:
                o[0][...] = res.astype(o[0].dtype)
            else:
                epilogue(res, ex, o)

        if nk == 1:
            finish(part)
        else:
            acc = o[0] if in_place else rest[-1]
            k = pl.program_id(2)

            @pl.when(k == 0)
            def _():
                acc[...] = part

            @pl.when(k > 0)
            def _():
                acc[...] += part

            if not in_place:
                @pl.when(k == nk - 1)
                def _():
                    finish(acc[...])

    in_place = epilogue is None and nout == 1 and outs[0].dtype == F32
    res = _call(body, name, grid, [a_spec, b_spec] + list(extra_specs), out_specs, outs,
                [] if nk == 1 or in_place else [pltpu.VMEM(acc_shape, F32)], ("parallel", "parallel", "arbitrary"),
                (a, b, *extras), aliases=aliases, job=job, prefetch=prefetch)
    return res if job is None else (res[:nout], res[nout:])


def _first(res, job):
    return res[0] if job is None else (res[0][0], res[1])


def _mm_fwd(name, a, wfull, out_dtype=F32, planes=1, bias=None, job=None):
    S, K = a.shape
    _, _, n = wfull.shape
    N = N_CHIPS * n
    tm = _pick(S, 1024, LANES)
    tn = _pick(n, 1408, LANES)
    per = n // tn
    nj = N // tn
    pj = nj // planes
    grid = (S // tm, nj, 1)
    a_spec = pl.BlockSpec((tm, K), lambda i, j, k: (i, 0))
    b_spec = pl.BlockSpec((None, K, tn), lambda i, j, k: (j // per, 0, j % per))
    o_spec = pl.BlockSpec((None, tm, tn), lambda i, j, k: (j // pj, i, j % pj))
    sd = jax.ShapeDtypeStruct((planes, S, N // planes), out_dtype)
    if bias is not None:
        def epi(res, ex, o):
            o[0][...] = (res + ex[0][...]).astype(o[0].dtype)

        out = _matmul(name, grid, NN, a, a_spec, wfull, b_spec, [sd], [o_spec], (tm, tn), extras=(bias,),
                      extra_specs=(pl.BlockSpec((1, tn), lambda i, j, k: (0, j)),), epilogue=epi, job=job)
    else:
        out = _matmul(name, grid, NN, a, a_spec, wfull, b_spec, [sd], [o_spec], (tm, tn), job=job)
    return _first(out, job)


def _mm_fwd_block(name, a, wfull, bias, p_idx, mask, into=None, job=None):
    S, K = a.shape
    _, _, n = wfull.shape
    N = N_CHIPS * n
    tm = _pick(S, 1024, LANES)
    tn = _pick(n, 1408, LANES)
    per = n // tn
    grid = (S // tm, per, 1)
    a_spec = pl.BlockSpec((tm, K), lambda i, j, k, p: (i, 0))
    b_spec = pl.BlockSpec((None, K, tn), lambda i, j, k, p: (jnp.bitwise_xor(p[0], mask), 0, j))
    o_spec = pl.BlockSpec((tm, tn), lambda i, j, k, p: (i, jnp.bitwise_xor(p[0], mask) * per + j))
    v_spec = pl.BlockSpec((1, tn), lambda i, j, k, p: (0, jnp.bitwise_xor(p[0], mask) * per + j))

    def epi(res, ex, o):
        o[0][...] = res + ex[0][...]

    return _matmul(name, grid, NN, a, a_spec, wfull, b_spec, [jax.ShapeDtypeStruct((S, N), F32)], [o_spec], (tm, tn),
                   extras=(bias,), extra_specs=(v_spec,), epilogue=epi, job=job, fill_into=into, prefetch=(p_idx,))


def _copy_block(dst, src, p_idx, mask, name):
    _, K, n = dst.shape
    tr = _pick(K, max(SUBLANES, (1 << 19) // n), 2 * SUBLANES)

    def body(p_ref, s_ref, d_any, o_ref):
        o_ref[...] = s_ref[...]

    spec = pl.BlockSpec((None, tr, n), lambda i, p: (jnp.bitwise_xor(p[0], mask), i, 0))
    return _call(body, name, (K // tr,), [spec, pl.BlockSpec(memory_space=pl.ANY)], [spec],
                 [jax.ShapeDtypeStruct(dst.shape, dst.dtype)], [], ("parallel",), (src, dst), aliases={1: 0},
                 prefetch=(p_idx,))[0]


def _mm_fwd_rows(name, a, wrows, resid, gate, job=None):
    S, K = a.shape
    _, N = wrows.shape
    tm = _pick(S, 512, LANES)
    tk = _pick(K, 2816, LANES)
    tn = N if K * N * 2 <= WHOLE_WEIGHT_BYTES else _pick(N, 1024, LANES)
    grid = (S // tm, N // tn, K // tk)
    a_spec = pl.BlockSpec((tm, tk), lambda i, j, k: (i, k))
    if tn == N and tk == K:
        b_spec = pl.BlockSpec((tk, tn), lambda i, j, k: (k, j), pipeline_mode=pl.Buffered(1))
    else:
        b_spec = pl.BlockSpec((tk, tn), lambda i, j, k: (k, j))
    o_spec = pl.BlockSpec((tm, tn), lambda i, j, k: (i, j))
    g_spec = pl.BlockSpec((1, tn), lambda i, j, k: (0, j))

    def epi(res, ex, o):
        o[0][...] = res
        o[1][...] = ex[0][...] + ex[1][...] * res

    sd = jax.ShapeDtypeStruct((S, N), F32)
    return _matmul(name, grid, NN, a, a_spec, wrows, b_spec, [sd, sd], [o_spec, o_spec], (tm, tn),
                   extras=(resid, gate), extra_specs=(o_spec, g_spec), epilogue=epi, job=job)


def _mm_dgrad_cols(name, dy, wfull, out_dtype=F32, job=None, row_tiles=None, fill_into=None):
    planes, S, npl = dy.shape
    _, K, n = wfull.shape
    tm = _pick(S if row_tiles is None else S // 2, 1024, LANES)
    to = _pick(K, 1024, LANES)
    tk = _pick(n, 2816, LANES)
    per = n // tk
    nk = N_CHIPS * per
    pk = nk // planes
    n_tiles = S // tm
    i0, ni = (0, n_tiles) if row_tiles is None else (int(row_tiles[0] * n_tiles), int(row_tiles[1] * n_tiles))
    grid = (ni, K // to, nk)
    a_spec = pl.BlockSpec((None, tm, tk), lambda i, j, k: (k // pk, i + i0, k % pk))
    b_spec = pl.BlockSpec((None, to, tk), lambda i, j, k: (k // per, j, k % per))
    o_spec = pl.BlockSpec((tm, to), lambda i, j, k: (i + i0, j))
    return _first(_matmul(name, grid, NT, dy, a_spec, wfull, b_spec, [jax.ShapeDtypeStruct((S, K), out_dtype)],
                          [o_spec], (tm, to), job=job, fill_into=fill_into), job)


def _mm_dgrad_rows(name, dy, wrows, out_dtype=F32, job=None):
    S, N = dy.shape
    K, _ = wrows.shape
    tm = _pick(S, 1024, LANES)
    to = _pick(K, 1408, LANES)
    tk = _pick(N, 2048, LANES)
    grid = (S // tm, K // to, N // tk)
    a_spec = pl.BlockSpec((tm, tk), lambda i, j, k: (i, k))
    b_spec = pl.BlockSpec((to, tk), lambda i, j, k: (j, k))
    o_spec = pl.BlockSpec((tm, to), lambda i, j, k: (i, j))
    return _first(_matmul(name, grid, NT, dy, a_spec, wrows, b_spec, [jax.ShapeDtypeStruct((S, K), out_dtype)],
                          [o_spec], (tm, to), job=job), job)


def _mm_wgrad_cols(name, a, dy, n, job=None):
    S, K = a.shape
    planes, _, npl = dy.shape
    N = planes * npl
    to = _pick(K, 1024, LANES)
    tn = _pick(n, 1408, LANES)
    ts = _pick(S, 2048, LANES)
    per = n // tn
    nj = N // tn
    pj = nj // planes
    grid = (K // to, nj, S // ts)
    a_spec = pl.BlockSpec((ts, to), lambda i, j, k: (k, i))
    b_spec = pl.BlockSpec((None, ts, tn), lambda i, j, k: (j // pj, k, j % pj))
    o_spec = pl.BlockSpec((None, to, tn), lambda i, j, k: (j // per, i, j % per))
    return _first(_matmul(name, grid, TN, a, a_spec, dy, b_spec, [jax.ShapeDtypeStruct((N_CHIPS, K, n), F32)],
                          [o_spec], (to, tn), job=job), job)


def _mm_wgrad_rows(name, a, dy, kshard, job=None):
    S, K = a.shape
    _, N = dy.shape
    to = _pick(kshard, 1408, LANES)
    tn = _pick(N, 1024, LANES)
    ts = _pick(S, 2048, LANES)
    per = kshard // to
    grid = (K // to, N // tn, S // ts)
    a_spec = pl.BlockSpec((ts, to), lambda i, j, k: (k, i))
    b_spec = pl.BlockSpec((ts, tn), lambda i, j, k: (k, j))
    o_spec = pl.BlockSpec((None, to, tn), lambda i, j, k: (i // per, i % per, j))
    return _first(_matmul(name, grid, TN, a, a_spec, dy, b_spec, [jax.ShapeDtypeStruct((N_CHIPS, kshard, N), F32)],
                          [o_spec], (to, tn), job=job), job)


def _rows(tm, width, colblk=0):
    return pl.BlockSpec((tm, width), lambda i: (i, colblk))


def _whole(shape):
    nd = len(shape)
    return pl.BlockSpec(shape, lambda i: (0,) * nd)


def _prev_halo(tm, h, width, colblk=0):
    r = tm // h
    return pl.BlockSpec((h, width), lambda i: (jnp.maximum(i * r - 1, 0), colblk))


def _next_halo(tm, h, width, nblk, colblk=0):
    r = tm // h
    return pl.BlockSpec((h, width), lambda i: (jnp.minimum((i + 1) * r, nblk - 1), colblk))


def _accumulate(i, ref, val):
    @pl.when(i == 0)
    def _():
        ref[...] = val

    @pl.when(i > 0)
    def _():
        ref[...] += val


def _fill_rotations(rot, offs):
    n = rot.shape[1]
    for r in sorted({o % SUBLANES for o in offs} - {0}):
        rot[r, 0:n - SUBLANES, :] = rot[0, r:r + n - SUBLANES, :]


def _tap_windows(rot, offs, row0, rb, lanes):
    by_res = {}
    for k, o in enumerate(offs):
        by_res.setdefault(o % SUBLANES, []).append((o // SUBLANES, k))
    for r, taps in by_res.items():
        lo = min(a for a, _ in taps)
        hi = max(a for a, _ in taps)
        win = rot[r, pl.ds(row0 + SUBLANES * lo, rb + SUBLANES * (hi - lo)), lanes]
        for a, k in taps:
            yield k, win[SUBLANES * (a - lo):SUBLANES * (a - lo) + rb, :]


def _for_chunks(n_rows, rb, fn):
    def step(j, carry):
        fn(pl.multiple_of(j * rb, rb))
        return carry

    lax.fori_loop(0, n_rows // rb, step, 0)


def _modnorm_fwd(name, x, g, scale, shift):
    S, D = x.shape
    tm = _pick(S, 512, LANES)

    def body(x_ref, g_ref, sc_ref, sh_ref, h_ref):
        h_ref[...] = _modnorm(x_ref[...], g_ref[...], sc_ref[...], sh_ref[...]).astype(BF16)

    vec = _whole((1, D))
    return pl.pallas_call(
        body, name=name, grid=(S // tm,), in_specs=[_rows(tm, D), vec, vec, vec], out_specs=_rows(tm, D),
        out_shape=jax.ShapeDtypeStruct((S, D), BF16), compiler_params=_params(("parallel",)),
    )(x, g, scale, shift)


def _modnorm_bwd(name, x, dh, dx_in, g, scale, shift, gated=None):
    S, D = x.shape
    tm = _pick(S, 256, LANES)

    def body(x_ref, dh_ref, dxin_ref, g_ref, sc_ref, sh_ref, *rest):
        i = pl.program_id(0)
        dx_ref, dg_ref, dsc_ref, dsh_ref = rest[-4:] if gated is None else rest[2:6]
        _, pull = jax.vjp(_modnorm, x_ref[...], g_ref[...], sc_ref[...], sh_ref[...])
        dx, dg, dsc, dsh = pull(dh_ref[...])
        dx = dxin_ref[...] + dx
        dx_ref[...] = dx
        _accumulate(i, dg_ref, dg)
        _accumulate(i, dsc_ref, dsc)
        _accumulate(i, dsh_ref, dsh)
        if gated is not None:
            _gate_bwd_tile(i, dx, rest[0], rest[1], rest[6], rest[7])

    vec = _whole((1, D))
    row = _rows(tm, D)
    vsd = jax.ShapeDtypeStruct((1, D), F32)
    in_specs, args = [row, row, row, vec, vec, vec], (x, dh, dx_in, g, scale, shift)
    out_specs, out_shape = [row, vec, vec, vec], [jax.ShapeDtypeStruct((S, D), F32), vsd, vsd, vsd]
    if gated is not None:
        in_specs, args = in_specs + [row, vec], args + tuple(gated)
        out_specs, out_shape = out_specs + [row, vec], out_shape + [jax.ShapeDtypeStruct((S, D), BF16), vsd]
    return _call(body, name, (S // tm,), in_specs, out_specs, out_shape, [], ("arbitrary",), args)


def _conv_fwd(name, proj, w, b, lg, lb, dc, job=None):
    S = proj.shape[0]
    K = w.shape[0]
    H = CONV_HALO
    tm = _pick(S, 256, LANES)

    def glu(v):
        return v[:, :dc] * jax.nn.sigmoid(v[:, dc:])

    offs = [H - (K - 1) + k for k in range(K)]
    rb = _pick(tm, CONV_CHUNK_ROWS, SUBLANES)
    lw = min(LANES, dc)

    def body(cur_ref, prev_ref, w_ref, b_ref, lg_ref, lb_ref, conv_ref, act_ref, rot):
        i = pl.program_id(0)
        rot[0, 0:H, :] = jnp.where(i > 0, glu(prev_ref[...]), 0.0)
        rot[0, H:, :] = glu(cur_ref[...])
        _fill_rotations(rot, offs)

        def chunk(row0):
            for l0 in range(0, dc, lw):
                lanes = slice(l0, l0 + lw)
                acc = jnp.broadcast_to(b_ref[:, lanes], (rb, lw))
                for k, win in _tap_windows(rot, offs, row0, rb, lanes):
                    acc = acc + w_ref[k:k + 1, lanes] * win
                conv_ref[pl.ds(row0, rb), lanes] = acc

        _for_chunks(tm, rb, chunk)
        act_ref[...] = _ln_silu(conv_ref[...], lg_ref[...], lb_ref[...]).astype(BF16)

    vec = _whole((1, dc))
    res = _call(body, name, (S // tm,),
                [_rows(tm, 2 * dc), _prev_halo(tm, H, 2 * dc), _whole(w.shape), vec, vec, vec],
                [_rows(tm, dc), _rows(tm, dc)],
                [jax.ShapeDtypeStruct((S, dc), F32), jax.ShapeDtypeStruct((S, dc), BF16)],
                [pltpu.VMEM((SUBLANES, tm + H, dc), F32)], ("parallel",), (proj, proj, w, b, lg, lb), job=job)
    return res if job is None else (res[:2], res[2:])


def _conv_bwd_ln(name, conv, dact, lg, lb):
    S, dc = conv.shape
    tm = _pick(S, 256, LANES)

    def body(c_ref, d_ref, lg_ref, lb_ref, dc_ref, dlg_ref, dlb_ref, db_ref):
        i = pl.program_id(0)
        _, pull = jax.vjp(_ln_silu, c_ref[...], lg_ref[...], lb_ref[...])
        dcv, dlg, dlb = pull(d_ref[...])
        dc_ref[...] = dcv
        _accumulate(i, dlg_ref, dlg)
        _accumulate(i, dlb_ref, dlb)
        _accumulate(i, db_ref, jnp.sum(dcv, axis=0, keepdims=True))

    vec = _whole((1, dc))
    row = _rows(tm, dc)
    vsd = jax.ShapeDtypeStruct((1, dc), F32)
    return pl.pallas_call(
        body, name=name, grid=(S // tm,), in_specs=[row, row, vec, vec], out_specs=[row, vec, vec, vec],
        out_shape=[jax.ShapeDtypeStruct((S, dc), F32), vsd, vsd, vsd],
        compiler_params=_params(("arbitrary",)),
    )(conv, dact, lg, lb)


def _conv_bwd(name, dconv, proj, w, dproj, dc, job=None):
    S = proj.shape[0]
    K = w.shape[0]
    H = CONV_HALO
    tm = _pick(S, 256, LANES)
    nt = S // tm

    offs_g = [H - (K - 1) + k for k in range(K)]
    offs_d = [K - 1 - k for k in range(K)]
    rb = _pick(tm, CONV_CHUNK_ROWS, SUBLANES)
    lw = min(LANES, dc)
    kp = -(-K // SUBLANES) * SUBLANES

    def body(d_ref, dn_ref, cur_ref, prev_ref, w_ref, dproj_any, da_ref, dw_ref, dbin_ref, rotg, rotd, accw, dglu_s):
        i = pl.program_id(0)
        pv = prev_ref[...]
        cv = cur_ref[...]
        sig = jax.nn.sigmoid(cv[:, dc:])
        rotg[0, 0:H, :] = jnp.where(i > 0, pv[:, :dc] * jax.nn.sigmoid(pv[:, dc:]), 0.0)
        rotg[0, H:, :] = cv[:, :dc] * sig
        rotd[0, 0:tm, :] = d_ref[...]
        rotd[0, tm:, :] = jnp.where(i < nt - 1, dn_ref[...], 0.0)
        _fill_rotations(rotg, offs_g)
        _fill_rotations(rotd, offs_d)

        @pl.when(i == 0)
        def _():
            accw[...] = jnp.zeros_like(accw)

        for l0 in range(0, dc, lw):
            lanes = slice(l0, l0 + lw)

            def chunk(j, sums, lanes=lanes):
                row0 = pl.multiple_of(j * rb, rb)
                dcur = rotd[0, pl.ds(row0, rb), lanes]
                acc = jnp.zeros((rb, lw), F32)
                for k, win in _tap_windows(rotd, offs_d, row0, rb, lanes):
                    acc = acc + w_ref[k:k + 1, lanes] * win
                new = list(sums)
                for k, win in _tap_windows(rotg, offs_g, row0, rb, lanes):
                    new[k] = sums[k] + jnp.sum((dcur * win).reshape(rb // SUBLANES, SUBLANES, lw), axis=0)
                dglu_s[pl.ds(row0, rb), lanes] = acc
                return tuple(new)

            sums = lax.fori_loop(0, tm // rb, chunk, tuple(jnp.zeros((SUBLANES, lw), F32) for _ in range(K)))
            for k in range(K):
                accw[SUBLANES * k:SUBLANES * (k + 1), lanes] += sums[k]
        dglu = dglu_s[...]
        da = jnp.concatenate([dglu * sig, dglu * cv[:, :dc] * sig * (1.0 - sig)], axis=1)
        da_ref[...] = da.astype(BF16)
        _accumulate(i, dbin_ref, jnp.sum(da, axis=0, keepdims=True))

        @pl.when(i == nt - 1)
        def _():
            dw_ref[...] = jnp.zeros_like(dw_ref)
            for k in range(K):
                dw_ref[k:k + 1, :] = jnp.sum(accw[SUBLANES * k:SUBLANES * (k + 1), :], axis=0, keepdims=True)

    res = _call(
        body, name, (nt,),
        [_rows(tm, dc), _next_halo(tm, H, dc, S // H), _rows(tm, 2 * dc), _prev_halo(tm, H, 2 * dc),
         _whole(w.shape), pl.BlockSpec(memory_space=pl.ANY)],
        [_rows(tm, 2 * dc), _whole((kp, dc)), _whole((1, 2 * dc))],
        [jax.ShapeDtypeStruct(dproj.shape, BF16), jax.ShapeDtypeStruct((kp, dc), F32),
         jax.ShapeDtypeStruct((1, 2 * dc), F32)],
        [pltpu.VMEM((SUBLANES, tm + H, dc), F32), pltpu.VMEM((SUBLANES, tm + H, dc), F32),
         pltpu.VMEM((SUBLANES * K, dc), F32), pltpu.VMEM((tm, dc), F32)],
        ("arbitrary",), (dconv, dconv, proj, proj, w, dproj), aliases={5: 0}, job=job)
    return res if job is None else (res[:3], res[3:])


def _mix(vln, wsm, bst, out_ref, G, CH, hd):
    for n in range(vln.shape[0] // CH):
        for g in range(G):
            blk = vln[n * CH:(n + 1) * CH, g * hd:(g + 1) * hd].astype(BF16)
            out_ref[n * CH:(n + 1) * CH, g * hd:(g + 1) * hd] = (
                jnp.dot(wsm[g], blk, preferred_element_type=F32) + bst[:, g:g + 1])


def _sgu_fwd(name, proj, a_act, wco, wso, lg, lb, ws, bst, D, ds, job=None):
    S = proj.shape[0]
    dc = a_act.shape[1]
    G, CH, _ = ws.shape
    hd = ds // G
    nco = wco.shape[2]
    tm = _pick(S, 256, CH)

    def body(s_ref, gt_ref, a_ref, wco_ref, wso_ref, lg_ref, lb_ref, ws_ref, bst_ref,
             ya_ref, yb_ref, uv_ref, mg_ref, vmix):
        z = _gelu(s_ref[...])
        vln = _layer_norm(z[:, ds:], lg_ref[...], lb_ref[...])
        wsm = [_tril_mask(ws_ref[g]).astype(BF16) for g in range(G)]
        _mix(vln, wsm, bst_ref[...], vmix, G, CH, hd)
        uv = (z[:, :ds] * vmix[...]).astype(BF16)
        uv_ref[...] = uv
        a = a_ref[...]
        for p in range(N_CHIPS):
            ya_ref[:, p * nco:(p + 1) * nco] = jnp.dot(a, wco_ref[p], preferred_element_type=F32)
            yb_ref[:, p * nco:(p + 1) * nco] = jnp.dot(uv, wso_ref[p], preferred_element_type=F32)
        gt = gt_ref[...]
        mg_ref[...] = (jax.nn.sigmoid(gt[:, :D]) * ya_ref[...] + jax.nn.sigmoid(gt[:, D:]) * yb_ref[...]).astype(BF16)

    vec = _whole((1, ds))
    sdf = jax.ShapeDtypeStruct((S, D), F32)
    res = _call(body, name, (S // tm,),
                [_rows(tm, 2 * ds, 1), _rows(tm, 2 * D, 1), _rows(tm, dc), _whole(wco.shape), _whole(wso.shape),
                 vec, vec, _whole(ws.shape), _whole(bst.shape)],
                [_rows(tm, D), _rows(tm, D), _rows(tm, ds), _rows(tm, D)],
                [sdf, sdf, jax.ShapeDtypeStruct((S, ds), BF16), jax.ShapeDtypeStruct((S, D), BF16)],
                [pltpu.VMEM((tm, ds), F32)], ("parallel",), (proj, proj, a_act, wco, wso, lg, lb, ws, bst), job=job)
    return res if job is None else (res[:4], res[4:])


def _merge_bwd(name, dmerged, proj, ya, yb, D):
    S = proj.shape[0]
    tm = _pick(S, 256, LANES)

    def body(dm_ref, gt_ref, ya_ref, yb_ref, dya_ref, dyb_ref, dg_ref, dbin_ref):
        i = pl.program_id(0)
        dm = dm_ref[...]
        gt = gt_ref[...]
        sa = jax.nn.sigmoid(gt[:, :D])
        sb = jax.nn.sigmoid(gt[:, D:])
        dya_ref[...] = (dm * sa).astype(BF16)
        dyb_ref[...] = (dm * sb).astype(BF16)
        dg = jnp.concatenate([dm * ya_ref[...] * sa * (1.0 - sa), dm * yb_ref[...] * sb * (1.0 - sb)], axis=1)
        dg_ref[...] = dg.astype(BF16)
        _accumulate(i, dbin_ref, jnp.sum(dg, axis=0, keepdims=True))

    row = _rows(tm, D)
    sdb = jax.ShapeDtypeStruct((S, D), BF16)
    return pl.pallas_call(
        body, name=name, grid=(S // tm,),
        in_specs=[row, _rows(tm, 2 * D, 1), row, row],
        out_specs=[row, row, _rows(tm, 2 * D, 1), _whole((1, 2 * D))],
        out_shape=[sdb, sdb, jax.ShapeDtypeStruct((S, 4 * D), BF16), jax.ShapeDtypeStruct((1, 2 * D), F32)],
        compiler_params=_params(("arbitrary",)),
    )(dmerged, proj, ya, yb)


def _sgu_bwd(name, proj, duv, lg, lb, ws, bst, dproj, ds):
    S = proj.shape[0]
    G, CH, _ = ws.shape
    hd = ds // G
    tm = _pick(S, 256, CH)

    def body(s_ref, duv_ref, lg_ref, lb_ref, ws_ref, bst_ref, dproj_any,
             dsin_ref, dws_ref, dbs_ref, dlg_ref, dlb_ref, dbin_ref, vmix, dvln):
        i = pl.program_id(0)
        z, pull_gelu = jax.vjp(_gelu, s_ref[...])
        u = z[:, :ds]
        vln, pull_ln = jax.vjp(_layer_norm, z[:, ds:], lg_ref[...], lb_ref[...])
        wsm = [_tril_mask(ws_ref[g]).astype(BF16) for g in range(G)]
        _mix(vln, wsm, bst_ref[...], vmix, G, CH, hd)
        duv = duv_ref[...]
        du = duv * vmix[...]
        dvmix = duv * u
        for g in range(G):
            dws_g = jnp.zeros((CH, CH), F32)
            dbs_g = jnp.zeros((CH, 1), F32)
            for n in range(tm // CH):
                dblk = dvmix[n * CH:(n + 1) * CH, g * hd:(g + 1) * hd]
                vblk = vln[n * CH:(n + 1) * CH, g * hd:(g + 1) * hd].astype(BF16)
                dvln[n * CH:(n + 1) * CH, g * hd:(g + 1) * hd] = lax.dot_general(
                    wsm[g], dblk.astype(BF16), TN, preferred_element_type=F32)
                dws_g = dws_g + lax.dot_general(dblk.astype(BF16), vblk, NT, preferred_element_type=F32)
                dbs_g = dbs_g + jnp.sum(dblk, axis=1, keepdims=True)
            dws_g = _tril_mask(dws_g)
            dbs_g = jnp.broadcast_to(dbs_g, (CH, LANES))

            @pl.when(i == 0)
            def _():
                dws_ref[g] = dws_g
                dbs_ref[g] = dbs_g

            @pl.when(i > 0)
            def _():
                dws_ref[g] += dws_g
                dbs_ref[g] += dbs_g

        dv, dlg, dlb = pull_ln(dvln[...])
        (dsin,) = pull_gelu(jnp.concatenate([du, dv], axis=1))
        dsin_ref[...] = dsin.astype(BF16)
        _accumulate(i, dlg_ref, dlg)
        _accumulate(i, dlb_ref, dlb)
        _accumulate(i, dbin_ref, jnp.sum(dsin, axis=0, keepdims=True))

    vec = _whole((1, ds))
    vsd = jax.ShapeDtypeStruct((1, ds), F32)
    return pl.pallas_call(
        body, name=name, grid=(S // tm,),
        in_specs=[_rows(tm, 2 * ds, 1), _rows(tm, ds), vec, vec, _whole(ws.shape), _whole(bst.shape),
                  pl.BlockSpec(memory_space=pl.ANY)],
        out_specs=[_rows(tm, 2 * ds, 1), _whole((G, CH, CH)), _whole((G, CH, LANES)), vec, vec, _whole((1, 2 * ds))],
        out_shape=[jax.ShapeDtypeStruct(dproj.shape, BF16), jax.ShapeDtypeStruct((G, CH, CH), F32),
                   jax.ShapeDtypeStruct((G, CH, LANES), F32), vsd, vsd, jax.ShapeDtypeStruct((1, 2 * ds), F32)],
        scratch_shapes=[pltpu.VMEM((tm, ds), F32), pltpu.VMEM((tm, ds), F32)],
        input_output_aliases={6: 0},
        compiler_params=_params(("arbitrary",)),
    )(proj, duv, lg, lb, ws, bst, dproj)


def _silu_mul(val, gt):
    return jax.nn.silu(gt) * val


def _rotation_slots(offs):
    slot = {0: 0}
    for r in sorted({o % SUBLANES for o in offs} - {0}):
        slot[r] = len(slot)
    return slot


def _fill_plane_rotations(rot, slot):
    n = rot.shape[2]
    for r, s in slot.items():
        if r:
            rot[:, s, 0:n - SUBLANES, :] = rot[:, 0, r:r + n - SUBLANES, :]


def _ffn_tiles(S, Fh):
    return _pick(S, 256, LANES), _pick(Fh, 1408, LANES)


def _ffn_fwd(name, up, w, b):
    _, S, Fh = up.shape
    K = w.shape[1]
    H = FFN_HALO
    tm, cw = _ffn_tiles(S, Fh)
    r = tm // H

    offs = [H - (K - 1) + k for k in range(K)]
    slot = _rotation_slots(offs)
    rb = _pick(tm, 64, 2 * SUBLANES)
    lw = min(LANES, cw)

    def body(cur_ref, prev_ref, w_ref, b_ref, act_ref, rot):
        i = pl.program_id(1)
        rot[:, 0, 0:H, :] = jnp.where(i > 0, prev_ref[...], 0.0)
        rot[:, 0, H:, :] = cur_ref[...]
        _fill_plane_rotations(rot, slot)

        def chunk(row0):
            for l0 in range(0, cw, lw):
                lanes = slice(l0, l0 + lw)
                c2 = []
                for pln in range(2):
                    acc = jnp.broadcast_to(b_ref[pln, :, lanes], (rb, lw))
                    for k in range(K):
                        a, rr = divmod(offs[k], SUBLANES)
                        acc = acc + w_ref[pln, k:k + 1, lanes] * rot[pln, slot[rr], pl.ds(row0 + SUBLANES * a, rb), lanes]
                    c2.append(acc)
                act_ref[pl.ds(row0, rb), lanes] = _silu_mul(c2[0], c2[1]).astype(BF16)

        _for_chunks(tm, rb, chunk)

    return pl.pallas_call(
        body, name=name, grid=(Fh // cw, S // tm),
        in_specs=[pl.BlockSpec((2, tm, cw), lambda j, i: (0, i, j)),
                  pl.BlockSpec((2, H, cw), lambda j, i: (0, jnp.maximum(i * r - 1, 0), j)),
                  pl.BlockSpec((2, K, cw), lambda j, i: (0, 0, j)),
                  pl.BlockSpec((2, 1, cw), lambda j, i: (0, 0, j))],
        out_specs=pl.BlockSpec((tm, cw), lambda j, i: (i, j)),
        out_shape=jax.ShapeDtypeStruct((S, Fh), BF16),
        scratch_shapes=[pltpu.VMEM((2, len(slot), tm + H, cw), F32)],
        compiler_params=_params(("parallel", "parallel")),
    )(up, up, w, b)


def _ffn_bwd(name, up, dact, w, b, job=None):
    _, S, Fh = up.shape
    K = w.shape[1]
    H = FFN_HALO
    tm, cw = _ffn_tiles(S, Fh)
    r = tm // H
    nt = S // tm
    nhb = S // H
    te = tm + H

    offs_x = [H - (K - 1) + k for k in range(K)]
    offs_d = [K - 1 - k for k in range(K)]
    slot_x = _rotation_slots(offs_x)
    slot_d = _rotation_slots(offs_d)
    rb = _pick(tm, 64, 2 * SUBLANES)
    rbe = _pick(te, 96, SUBLANES)
    lw = min(LANES, cw)

    def body(cur_ref, prev_ref, next_ref, d_ref, dn_ref, w_ref, b_ref, dup_ref, dwb_ref, rotx, dext, rotd, accw):
        i = pl.program_id(1)
        rotx[:, 0, 0:H, :] = jnp.where(i > 0, prev_ref[...], 0.0)
        rotx[:, 0, H:H + tm, :] = cur_ref[...]
        rotx[:, 0, H + tm:, :] = jnp.where(i < nt - 1, next_ref[...], 0.0)
        dext[0:tm, :] = d_ref[...]
        dext[tm:, :] = jnp.where(i < nt - 1, dn_ref[...], 0.0)
        _fill_plane_rotations(rotx, slot_x)

        def chunk_e(row0):
            for l0 in range(0, cw, lw):
                lanes = slice(l0, l0 + lw)
                c2 = []
                for pln in range(2):
                    acc = jnp.broadcast_to(b_ref[pln, :, lanes], (rbe, lw))
                    for k in range(K):
                        a, rr = divmod(offs_x[k], SUBLANES)
                        acc = acc + w_ref[pln, k:k + 1, lanes] * rotx[pln, slot_x[rr], pl.ds(row0 + SUBLANES * a, rbe), lanes]
                    c2.append(acc)
                _, pull = jax.vjp(_silu_mul, c2[0], c2[1])
                dval, dgt = pull(dext[pl.ds(row0, rbe), lanes])
                rotd[0, 0, pl.ds(row0, rbe), lanes] = dval
                rotd[1, 0, pl.ds(row0, rbe), lanes] = dgt

        _for_chunks(te, rbe, chunk_e)
        _fill_plane_rotations(rotd, slot_d)

        @pl.when(i == 0)
        def _():
            accw[...] = jnp.zeros_like(accw)

        def chunk(row0):
            for l0 in range(0, cw, lw):
                lanes = slice(l0, l0 + lw)
                for pln in range(2):
                    dcur = rotd[pln, 0, pl.ds(row0, rb), lanes]
                    dup = jnp.zeros((rb, lw), F32)
                    for k in range(K):
                        a, rr = divmod(offs_d[k], SUBLANES)
                        dup = dup + w_ref[pln, k:k + 1, lanes] * rotd[pln, slot_d[rr], pl.ds(row0 + SUBLANES * a, rb), lanes]
                        a, rr = divmod(offs_x[k], SUBLANES)
                        prod = dcur * rotx[pln, slot_x[rr], pl.ds(row0 + SUBLANES * a, rb), lanes]
                        accw[pln, SUBLANES * k:SUBLANES * (k + 1), lanes] += jnp.sum(
                            prod.reshape(rb // SUBLANES, SUBLANES, lw), axis=0)
                    accw[pln, SUBLANES * K:SUBLANES * (K + 1), lanes] += jnp.sum(
                        dcur.reshape(rb // SUBLANES, SUBLANES, lw), axis=0)
                    dup_ref[pln, pl.ds(row0, rb), lanes] = dup.astype(BF16)

        _for_chunks(tm, rb, chunk)

        @pl.when(i == nt - 1)
        def _():
            dwb_ref[...] = jnp.zeros_like(dwb_ref)
            for pln in range(2):
                for k in range(K + 1):
                    dwb_ref[pln, k:k + 1, :] = jnp.sum(accw[pln, SUBLANES * k:SUBLANES * (k + 1), :], axis=0,
                                                       keepdims=True)

    res = _call(
        body, name, (Fh // cw, nt),
        [pl.BlockSpec((2, tm, cw), lambda j, i: (0, i, j)),
         pl.BlockSpec((2, H, cw), lambda j, i: (0, jnp.maximum(i * r - 1, 0), j)),
         pl.BlockSpec((2, H, cw), lambda j, i: (0, jnp.minimum((i + 1) * r, nhb - 1), j)),
         pl.BlockSpec((tm, cw), lambda j, i: (i, j)),
         pl.BlockSpec((H, cw), lambda j, i: (jnp.minimum((i + 1) * r, nhb - 1), j)),
         pl.BlockSpec((2, K, cw), lambda j, i: (0, 0, j)),
         pl.BlockSpec((2, 1, cw), lambda j, i: (0, 0, j))],
        [pl.BlockSpec((2, tm, cw), lambda j, i: (0, i, j)), pl.BlockSpec((2, SUBLANES, cw), lambda j, i: (0, 0, j))],
        [jax.ShapeDtypeStruct((2, S, Fh), BF16), jax.ShapeDtypeStruct((2, SUBLANES, Fh), F32)],
        [pltpu.VMEM((2, len(slot_x), tm + 2 * H, cw), F32), pltpu.VMEM((te, cw), F32),
         pltpu.VMEM((2, len(slot_d), te, cw), F32), pltpu.VMEM((2, SUBLANES * (K + 1), cw), F32)],
        ("parallel", "arbitrary"), (up, up, up, dact, dact, w, b), job=job)
    return res if job is None else (res[:2], res[2:])


def _rms(x, g):
    return x * lax.rsqrt(jnp.mean(x * x, axis=-1, keepdims=True) + EPS) * g


def _gate_bwd_tile(i, dx, out_ref, gate_ref, dout_ref, dgate_ref):
    dout_ref[...] = (dx * gate_ref[...]).astype(BF16)
    _accumulate(i, dgate_ref, jnp.sum(dx * out_ref[...], axis=0, keepdims=True))


def _final(name, x2, target, gf, out, gate):
    S, D = x2.shape
    tm = _pick(S, 256, LANES)

    def body(x_ref, t_ref, g_ref, o_ref, gate_ref, dx_ref, dg_ref, loss_ref, do_ref, dgate_ref):
        i = pl.program_id(0)
        y, pull = jax.vjp(_rms, x_ref[...], g_ref[...])
        e = y - t_ref[...]
        dx, dg = pull(e / D)
        dx_ref[...] = dx
        _accumulate(i, dg_ref, dg)
        part = 0.5 * jnp.sum(jnp.mean(jnp.square(e), axis=-1, keepdims=True), axis=0, keepdims=True)
        _accumulate(i, loss_ref, jnp.broadcast_to(part, (SUBLANES, LANES)))
        _gate_bwd_tile(i, dx, o_ref, gate_ref, do_ref, dgate_ref)

    row = _rows(tm, D)
    vec = _whole((1, D))
    vsd = jax.ShapeDtypeStruct((1, D), F32)
    return pl.pallas_call(
        body, name=name, grid=(S // tm,), in_specs=[row, row, vec, row, vec],
        out_specs=[row, vec, _whole((SUBLANES, LANES)), row, vec],
        out_shape=[jax.ShapeDtypeStruct((S, D), F32), vsd, jax.ShapeDtypeStruct((SUBLANES, LANES), F32),
                   jax.ShapeDtypeStruct((S, D), BF16), vsd],
        compiler_params=_params(("arbitrary",)),
    )(x2, target, gf, out, gate)


def _ada_fwd(name, c_pad, w_ada, b_cols):
    nb, D = c_pad.shape
    n = w_ada.shape[1]
    tn = _pick(n, 1024, LANES)

    def body(c_ref, w_ref, b_ref, o_ref):
        o_ref[...] = jnp.dot(jax.nn.silu(c_ref[...]).astype(BF16), w_ref[...].astype(BF16),
                             preferred_element_type=F32) + b_ref[...]

    return pl.pallas_call(
        body, name=name, grid=(n // tn,),
        in_specs=[_whole((nb, D)), pl.BlockSpec((D, tn), lambda j: (0, j)), pl.BlockSpec((1, tn), lambda j: (0, j))],
        out_specs=pl.BlockSpec((nb, tn), lambda j: (0, j)),
        out_shape=jax.ShapeDtypeStruct((nb, n), F32), compiler_params=_params(("parallel",)),
    )(c_pad, w_ada, b_cols)


def _ada_wgrad(name, c_t, dmod_cols):
    D, nb = c_t.shape
    n = dmod_cols.shape[1]
    tr = _pick(D, 256, SUBLANES)

    def body(c_ref, d_ref, o_ref):
        ca = jax.nn.silu(c_ref[...])
        acc = ca[:, 0:1] * d_ref[0:1, :]
        for b in range(1, nb):
            acc = acc + ca[:, b:b + 1] * d_ref[b:b + 1, :]
        o_ref[...] = acc

    return pl.pallas_call(
        body, name=name, grid=(D // tr,),
        in_specs=[pl.BlockSpec((tr, nb), lambda i: (i, 0)), _whole((nb, n))],
        out_specs=pl.BlockSpec((tr, n), lambda i: (i, 0)),
        out_shape=jax.ShapeDtypeStruct((D, n), F32), compiler_params=_params(("parallel",)),
    )(c_t, dmod_cols)


def kernel(x, c, w_ada, b_ada, norm1_g, w_in, b_in, conv_dw_w, conv_dw_b, conv_ln_g, conv_ln_b, w_conv_out, sgu_ln_g, sgu_ln_b, w_spatial, b_spatial, w_sgu_out, w_out, norm2_g, w_up, ffn_dw_w, ffn_dw_b, w_down, final_g, loss_target, m_w_ada, m_b_ada, m_norm1_g, m_w_in, m_b_in, m_conv_dw_w, m_conv_dw_b, m_conv_ln_g, m_conv_ln_b, m_w_conv_out, m_sgu_ln_g, m_sgu_ln_b, m_w_spatial, m_b_spatial, m_w_sgu_out, m_w_out, m_norm2_g, m_w_up, m_ffn_dw_w, m_ffn_dw_b, m_w_down, m_final_g, v_w_ada, v_b_ada, v_norm1_g, v_w_in, v_b_in, v_conv_dw_w, v_conv_dw_b, v_conv_ln_g, v_conv_ln_b, v_w_conv_out, v_sgu_ln_g, v_sgu_ln_b, v_w_spatial, v_b_spatial, v_w_sgu_out, v_w_out, v_norm2_g, v_w_up, v_ffn_dw_w, v_ffn_dw_b, v_w_down, v_final_g):
    S, D = x.shape[1], x.shape[2]
    dc = w_conv_out.shape[1]
    ds = w_sgu_out.shape[1]
    G, CH = w_spatial.shape[1], w_spatial.shape[2]
    KC = conv_dw_w.shape[1]
    KF = ffn_dw_w.shape[1]
    F2 = ffn_dw_b.shape[1]
    Fh = F2 // 2
    n_ada = w_ada.shape[2]
    n_up = w_up.shape[2]
    ax, ay, ac = _axes()
    chip = 2 * ax + ay
    me = 2 * chip + ac
    c_idx = jnp.reshape(ac, (1,)).astype(jnp.int32)
    p_idx = jnp.reshape(chip, (1,)).astype(jnp.int32)

    xs = x[0]
    tgt = loss_target[0]

    g1 = _allgather8(_pack([c[0], conv_dw_w[0], ffn_dw_w[0]]), "gather_small_in")
    parts = [_unpack(g1[2 * q], [(D,), conv_dw_w.shape[1:], ffn_dw_w.shape[1:]]) for q in range(N_CHIPS)]
    c_all = jnp.stack([_unpack(g1[d], [(D,)])[0] for d in range(N_DEV)])
    cw_full = jnp.concatenate([pt[1] for pt in parts], axis=1)
    fw_full = jnp.concatenate([pt[2] for pt in parts], axis=1)

    b_cols = lax.dynamic_slice(b_ada, (0, chip * n_ada), (1, n_ada))
    c_pad = jnp.concatenate([c_all, jnp.zeros_like(c_all)], axis=0)
    mod_blk = _ada_fwd("ada_fwd", c_pad, w_ada[0], b_cols)[:N_DEV]
    g2 = _allgather8(_pack([mod_blk]), "gather_mod")
    mod_all = jnp.concatenate([_unpack(g2[2 * q], [(N_DEV, n_ada)])[0] for q in range(N_CHIPS)], axis=1)
    mod = lax.dynamic_slice(mod_all, (me, 0), (1, 6 * D))
    shift1, scale1, gate1, shift2, scale2, gate2 = [mod[:, k * D:(k + 1) * D] for k in range(6)]

    shards = [w_in[0], w_conv_out[0], w_sgu_out[0], w_out[0], w_up[0], w_down[0]]
    names = ["in", "conv_out", "sgu_out", "out", "up", "down"]
    blk = {nm: _cast_into_block(s, p_idx, "cast_" + nm) for s, nm in zip(shards, names)}
    pc_idx = jnp.concatenate([p_idx, c_idx])

    h1 = _modnorm_fwd("modnorm1", xs, norm1_g, scale1, shift1)
    (proj,), (win_xy,) = _mm_fwd_block("proj_own", h1, blk["in"], b_in, p_idx, 0,
                                       job=_job_gather([blk["in"]], rel=(0, 1), fresh=True))
    (proj,), (win_d,) = _mm_fwd_block("proj_x", h1, win_xy, b_in, p_idx, 2, into=proj,
                                      job=_job_gather([blk["in"]], rel=(2,), fresh=True))
    (proj,), (wco_f, wso_f) = _mm_fwd_block("proj_y", h1, win_xy, b_in, p_idx, 1, into=proj,
                                            job=_job_gather([blk["conv_out"], blk["sgu_out"]]))
    (proj,), (wout_f,) = _mm_fwd_block("proj_diag", h1, win_d, b_in, p_idx, 3, into=proj,
                                       job=_job_gather([blk["out"]]))
    win_f = _copy_block(_copy_block(win_xy, blk["in"], p_idx, 0, "fill_w_in_own"), win_d, p_idx, 3, "fill_w_in_diag")
    wout_r = wout_f.reshape(-1, wout_f.shape[2])
    (conv, a_act), (wup_xy,) = _conv_fwd("conv_fwd", proj, cw_full, conv_dw_b, conv_ln_g, conv_ln_b, dc,
                                         job=_job_gather([blk["up"]], rel=(0, 1)))
    bst = jnp.transpose(b_spatial[0])
    (ya, yb, uv, merged), (wup_f,) = _sgu_fwd("sgu_fwd", proj, a_act, wco_f, wso_f, sgu_ln_g, sgu_ln_b, w_spatial[0],
                                              bst, D, ds, job=_job_gather([wup_xy], rel=(2,)))
    out1, x1 = _mm_fwd_rows("out1", merged, wout_r, xs, gate1)
    h2 = _modnorm_fwd("modnorm2", x1, norm2_g, scale2, shift2)
    up, (wdown_f,) = _mm_fwd("up", h2, wup_f, planes=2, job=_job_gather([blk["down"]]))
    wdown_r = wdown_f.reshape(-1, wdown_f.shape[2])
    fw2 = jnp.stack([fw_full[:, :Fh], fw_full[:, Fh:]])
    fb2 = jnp.stack([ffn_dw_b[:, :Fh], ffn_dw_b[:, Fh:]])
    act = _ffn_fwd("ffn_fwd", up, fw2, fb2)
    out2, x2 = _mm_fwd_rows("out2", act, wdown_r, x1, gate2)
    dx2, d_final_g, loss_blk, dout2, d_gate2 = _final("final", x2, tgt, final_g.reshape(1, D), out2, gate2)
    loss = lax.psum(loss_blk[0, 0], ("x", "y", "c"))

    def add_cores(nm, g, r1):
        return _add_own_half(g, r1, c_idx, "add_cores_" + nm)

    def add_chips(nm, g, r1, r2):
        return _add_chips(g, r1, r2, pc_idx, "add_chips_" + nm)

    g_wdown = _mm_wgrad_rows("wgrad_down", act, dout2, w_down.shape[1])
    dact, (r1_down,) = _mm_dgrad_rows("dgrad_down", dout2, wdown_r, job=_job_swap_halves([g_wdown]))
    s_down = add_cores("down", g_wdown, r1_down)
    (dup, d_ffn), (r2_down,) = _ffn_bwd("ffn_bwd", up, dact, fw2, fb2, job=_job_scatter_blocks([s_down]))
    h_down = add_chips("down", g_wdown, r1_down, r2_down)
    g_wup = _mm_wgrad_cols("wgrad_up", h2, dup, n_up)
    dh2, (r1_up,) = _mm_dgrad_cols("dgrad_up", dup, wup_f, job=_job_swap_halves([g_wup]))
    s_up = add_cores("up", g_wup, r1_up)
    dx1, d_norm2, d_scale2, d_shift2, dout1, d_gate1 = _modnorm_bwd(
        "modnorm2_bwd", x1, dh2, dx2, norm2_g, scale2, shift2, gated=(out1, gate1))
    g_wout = _mm_wgrad_rows("wgrad_out", merged, dout1, w_out.shape[1])
    dmerged = _mm_dgrad_rows("dgrad_out", dout1, wout_r)
    dya, dyb, dproj, dbin_g = _merge_bwd("merge_bwd", dmerged, proj, ya, yb, D)
    g_wco = _mm_wgrad_cols("wgrad_conv_out", a_act, dya[None], w_conv_out.shape[2])
    g_wso = _mm_wgrad_cols("wgrad_sgu_out", uv, dyb[None], w_sgu_out.shape[2])
    da_act, (r1_out, r1_co, r1_so) = _mm_dgrad_cols("dgrad_conv_out", dya[None], wco_f,
                                                    job=_job_swap_halves([g_wout, g_wco, g_wso]))
    s_out = add_cores("out", g_wout, r1_out)
    s_co = add_cores("conv_out", g_wco, r1_co)
    s_so = add_cores("sgu_out", g_wso, r1_so)
    duv = _mm_dgrad_cols("dgrad_sgu_out", dyb[None], wso_f)
    dproj, d_ws, d_bs, d_sgu_g, d_sgu_b, dbin_s = _sgu_bwd("sgu_bwd", proj, duv, sgu_ln_g, sgu_ln_b, w_spatial[0], bst,
                                                           dproj, ds)
    dconv, d_cln_g, d_cln_b, d_conv_b = _conv_bwd_ln("conv_ln_bwd", conv, da_act, conv_ln_g, conv_ln_b)
    d_fw = jnp.concatenate([d_ffn[0, :KF], d_ffn[1, :KF]], axis=1)
    d_fb = jnp.concatenate([d_ffn[0, KF:KF + 1], d_ffn[1, KF:KF + 1]], axis=1)
    early = [d_sgu_g, d_sgu_b, d_ws, d_bs[:, :, 0], d_norm2, d_fb, d_final_g, d_fw]
    (dproj, d_cw, dbin_a), (r2_up, g_early) = _conv_bwd(
        "conv_bwd", dconv, proj, cw_full, dproj, dc,
        job=_merge_jobs([_job_scatter_blocks([s_up]), _job_allgather8(_pack(early))]))
    h_up = add_chips("up", g_wup, r1_up, r2_up)
    g_win, (r2_out, r2_co, r2_so, sib_up, sib_down) = _mm_wgrad_cols(
        "wgrad_in", h1, dproj[None], w_in.shape[2],
        job=_merge_jobs([_job_scatter_blocks([s_out, s_co, s_so]), _job_to_sibling([h_up, h_down])]))
    h_out = add_chips("out", g_wout, r1_out, r2_out)
    h_co = add_chips("conv_out", g_wco, r1_co, r2_co)
    h_so = add_chips("sgu_out", g_wso, r1_so, r2_so)
    dh1, (r1_in, sib_out, sib_co, sib_so) = _mm_dgrad_cols(
        "dgrad_in_a", dproj[None], win_f, row_tiles=(0.0, 0.5),
        job=_merge_jobs([_job_swap_halves([g_win]), _job_to_sibling([h_out, h_co, h_so])]))
    s_in = add_cores("in", g_win, r1_in)
    dh1, (r2_in,) = _mm_dgrad_cols("dgrad_in_b", dproj[None], win_f, row_tiles=(0.5, 0.5), fill_into=dh1,
                                   job=_job_scatter_blocks([s_in]))
    h_in = add_chips("in", g_win, r1_in, r2_in)
    dxs, d_norm1, d_scale1, d_shift1 = _modnorm_bwd("modnorm1_bwd", xs, dh1, dx1, norm1_g, scale1, shift1)
    (sib_in,) = _run_job(_job_to_sibling([h_in]), "exchange_w_in")
    big_halves = {"w_in": (h_in, sib_in), "w_conv_out": (h_co, sib_co), "w_sgu_out": (h_so, sib_so),
                  "w_out": (h_out, sib_out), "w_up": (h_up, sib_up), "w_down": (h_down, sib_down)}

    d_mod = jnp.concatenate([d_shift1, d_scale1, d_gate1, d_shift2, d_scale2, d_gate2], axis=1)
    d_b_in = jnp.concatenate([dbin_a, dbin_s, dbin_g], axis=1)
    late = [d_mod, d_norm1, d_b_in, d_conv_b, d_cln_g, d_cln_b, d_cw[:KC]]
    g3 = _allgather8(_pack(late), "gather_small_grads")
    g_b_ada, g_norm1, g_b_in, g_conv_b, g_cln_g, g_cln_b, g_cw_full = _unpack(
        _sum8(g3, "sum_small_grads"), [a.shape for a in late])
    g_sgu_g, g_sgu_b, g_ws, g_bs, g_norm2, g_fb, g_final, g_fw_full = _unpack(
        _sum8(g_early, "sum_early_grads"), [a.shape for a in early])
    n_cw = conv_dw_w.shape[2]
    n_fw = ffn_dw_w.shape[2]
    g_cw = lax.dynamic_slice(g_cw_full, (0, chip * n_cw), (KC, n_cw))
    g_fw = lax.dynamic_slice(g_fw_full, (0, chip * n_fw), (KF, n_fw))
    dmod_all = jnp.stack([_unpack(g3[d], [(6 * D,)])[0] for d in range(N_DEV)])
    dmod_cols = lax.dynamic_slice(dmod_all, (0, chip * n_ada), (N_DEV, n_ada))
    g_wada = _ada_wgrad("ada_wgrad", jnp.transpose(c_all), dmod_cols)

    grads = {
        "w_ada": g_wada[None], "b_ada": g_b_ada, "norm1_g": g_norm1, "b_in": g_b_in,
        "conv_dw_w": g_cw[None], "conv_dw_b": g_conv_b, "conv_ln_g": g_cln_g, "conv_ln_b": g_cln_b,
        "sgu_ln_g": g_sgu_g, "sgu_ln_b": g_sgu_b, "w_spatial": g_ws[None],
        "b_spatial": g_bs[None], "norm2_g": g_norm2, "ffn_dw_w": g_fw[None], "ffn_dw_b": g_fb,
        "final_g": g_final.reshape(D),
    }
    weights = dict(w_ada=w_ada, b_ada=b_ada, norm1_g=norm1_g, w_in=w_in, b_in=b_in, conv_dw_w=conv_dw_w, conv_dw_b=conv_dw_b, conv_ln_g=conv_ln_g, conv_ln_b=conv_ln_b, w_conv_out=w_conv_out, sgu_ln_g=sgu_ln_g, sgu_ln_b=sgu_ln_b, w_spatial=w_spatial, b_spatial=b_spatial, w_sgu_out=w_sgu_out, w_out=w_out, norm2_g=norm2_g, w_up=w_up, ffn_dw_w=ffn_dw_w, ffn_dw_b=ffn_dw_b, w_down=w_down, final_g=final_g)
    m_in = dict(w_ada=m_w_ada, b_ada=m_b_ada, norm1_g=m_norm1_g, w_in=m_w_in, b_in=m_b_in, conv_dw_w=m_conv_dw_w, conv_dw_b=m_conv_dw_b, conv_ln_g=m_conv_ln_g, conv_ln_b=m_conv_ln_b, w_conv_out=m_w_conv_out, sgu_ln_g=m_sgu_ln_g, sgu_ln_b=m_sgu_ln_b, w_spatial=m_w_spatial, b_spatial=m_b_spatial, w_sgu_out=m_w_sgu_out, w_out=m_w_out, norm2_g=m_norm2_g, w_up=m_w_up, ffn_dw_w=m_ffn_dw_w, ffn_dw_b=m_ffn_dw_b, w_down=m_w_down, final_g=m_final_g)
    v_in = dict(w_ada=v_w_ada, b_ada=v_b_ada, norm1_g=v_norm1_g, w_in=v_w_in, b_in=v_b_in, conv_dw_w=v_conv_dw_w, conv_dw_b=v_conv_dw_b, conv_ln_g=v_conv_ln_g, conv_ln_b=v_conv_ln_b, w_conv_out=v_w_conv_out, sgu_ln_g=v_sgu_ln_g, sgu_ln_b=v_sgu_ln_b, w_spatial=v_w_spatial, b_spatial=v_b_spatial, w_sgu_out=v_w_sgu_out, w_out=v_w_out, norm2_g=v_norm2_g, w_up=v_w_up, ffn_dw_w=v_ffn_dw_w, ffn_dw_b=v_ffn_dw_b, w_down=v_w_down, final_g=v_final_g)
    order = list(weights.keys())
    large = ["w_ada", "w_in", "w_conv_out", "w_sgu_out", "w_out", "w_up", "w_down"]
    little = [n for n in order if n not in large]
    delta, new_m, new_v = {}, {}, {}
    for n in large:
        shp = weights[n].shape
        two = (shp[1], shp[2])
        if n in big_halves:
            g_, d_, m_, v_ = _adamw_halves(weights[n].reshape(two), big_halves[n][0], big_halves[n][1],
                                           m_in[n].reshape(two), v_in[n].reshape(two), c_idx, "adamw_" + n)
            grads[n] = g_.reshape(shp)
        else:
            d_, m_, v_ = _adamw(weights[n].reshape(two), grads[n].reshape(two), m_in[n].reshape(two),
                                v_in[n].reshape(two), "adamw_" + n)
        delta[n], new_m[n], new_v[n] = d_.reshape(shp), m_.reshape(shp), v_.reshape(shp)
    shapes = [weights[n].shape for n in little]
    d_, m_, v_ = _adamw(_pack([weights[n] for n in little]), _pack([grads[n] for n in little]),
                        _pack([m_in[n] for n in little]), _pack([v_in[n] for n in little]), "adamw_small")
    for n, dd, mm, vv in zip(little, _unpack(d_, shapes), _unpack(m_, shapes), _unpack(v_, shapes)):
        delta[n], new_m[n], new_v[n] = dd, mm, vv
    grad_out = [grads[n].reshape(weights[n].shape) for n in order]
    return (loss, dxs[None], *grad_out, *[delta[n] for n in order], *[new_m[n] for n in order],
            *[new_v[n] for n in order])
```

```python
import functools

import jax
import jax.numpy as jnp
from jax import lax
from jax.experimental import pallas as pl
from jax.experimental.pallas import tpu as pltpu

F32 = jnp.float32
BF16 = jnp.bfloat16
EPS = 1e-6
MESH = pl.DeviceIdType.MESH
N_CHIPS = 4
N_DEV = 8
LANES = 128
SUBLANES = 8
CONV_HALO = 32
FFN_HALO = 8
CONV_CHUNK_ROWS = 32
VMEM_LIMIT_BYTES = 56 * 1024 * 1024
WHOLE_WEIGHT_BYTES = 8 * 1024 * 1024

ADAM_LR = 0.001
ADAM_B1 = 0.9
ADAM_B2 = 0.999
ADAM_EPS = 1e-08
ADAM_WD = 0.01
ADAM_STEP = 10

NN = (((1,), (0,)), ((), ()))
NT = (((1,), (1,)), ((), ()))
TN = (((0,), (0,)), ((), ()))


def _params(sem=None):
    return pltpu.CompilerParams(dimension_semantics=sem, vmem_limit_bytes=VMEM_LIMIT_BYTES)


def _pick(dim, pref, mult):
    best = None
    d = mult
    while d <= min(dim, pref):
        if dim % d == 0:
            best = d
        d += mult
    return dim if best is None else best


def _axes():
    return lax.axis_index("x"), lax.axis_index("y"), lax.axis_index("c")


def _modnorm(x, g, scale, shift):
    r = lax.rsqrt(jnp.mean(x * x, axis=-1, keepdims=True) + EPS)
    return (x * r * g) * (1.0 + scale) + shift


def _layer_norm(x, g, b):
    mu = jnp.mean(x, axis=-1, keepdims=True)
    var = jnp.mean(jnp.square(x - mu), axis=-1, keepdims=True)
    return (x - mu) * lax.rsqrt(var + EPS) * g + b


def _gelu(x):
    return 0.5 * x * (1.0 + lax.erf(x * (0.5 ** 0.5)))


def _ln_silu(x, g, b):
    return jax.nn.silu(_layer_norm(x, g, b))


def _tril_mask(ws):
    n = ws.shape[-1]
    row = lax.broadcasted_iota(jnp.int32, (n, n), 0)
    col = lax.broadcasted_iota(jnp.int32, (n, n), 1)
    return jnp.where(row >= col, ws, 0.0)


def _pack(arrs):
    flat = [a.reshape(-1).astype(F32) for a in arrs]
    total = sum(f.shape[0] for f in flat)
    tile = SUBLANES * LANES
    padded = -(-total // tile) * tile
    if padded > total:
        flat = flat + [jnp.zeros((padded - total,), F32)]
    return jnp.concatenate(flat).reshape(padded // LANES, LANES)


def _unpack(buf, shapes):
    flat = buf.reshape(-1)
    out, off = [], 0
    for s in shapes:
        n = 1
        for d in s:
            n *= d
        out.append(flat[off:off + n].reshape(s))
        off += n
    return out


def _allgather8(buf, name):
    R, L = buf.shape

    def body(in_ref, out_ref, send_sems, recv_sems, local_sem):
        x, y, c = _axes()
        me = 4 * x + 2 * y + c
        mine = pltpu.make_async_copy(in_ref, out_ref.at[me], local_sem)
        mine.start()
        peers = []
        for k in range(1, N_DEV):
            px = 1 - x if k & 4 else x
            py = 1 - y if k & 2 else y
            pc = 1 - c if k & 1 else c
            peers.append((px, py, pc))
        sends = []
        for k, peer in enumerate(peers):
            cp = pltpu.make_async_remote_copy(
                src_ref=in_ref, dst_ref=out_ref.at[me], send_sem=send_sems.at[k], recv_sem=recv_sems.at[k],
                device_id=peer, device_id_type=MESH)
            cp.start()
            sends.append(cp)
        for k, (px, py, pc) in enumerate(peers):
            pltpu.make_async_remote_copy(
                src_ref=in_ref, dst_ref=out_ref.at[4 * px + 2 * py + pc], send_sem=send_sems.at[k],
                recv_sem=recv_sems.at[k], device_id=(px, py, pc), device_id_type=MESH).wait_recv()
        for cp in sends:
            cp.wait_send()
        mine.wait()

    return pl.pallas_call(
        body, name=name,
        out_shape=jax.ShapeDtypeStruct((N_DEV, R, L), buf.dtype),
        in_specs=[pl.BlockSpec(memory_space=pltpu.VMEM)],
        out_specs=pl.BlockSpec(memory_space=pltpu.VMEM),
        scratch_shapes=[pltpu.SemaphoreType.DMA((N_DEV - 1,)), pltpu.SemaphoreType.DMA((N_DEV - 1,)),
                        pltpu.SemaphoreType.DMA],
        compiler_params=pltpu.CompilerParams(vmem_limit_bytes=VMEM_LIMIT_BYTES),
    )(buf)


def _other_chips(x, y):
    return [(1 - x, y), (x, 1 - y), (1 - x, 1 - y)]


def _cast_into_block(shard, p_idx, name):
    K, n = shard.shape
    tr = _pick(K, max(SUBLANES, (1 << 19) // n), 2 * SUBLANES)

    def body(p_ref, s_ref, o_ref):
        o_ref[...] = s_ref[...].astype(BF16)

    return pl.pallas_call(
        body, name=name,
        grid_spec=pltpu.PrefetchScalarGridSpec(
            num_scalar_prefetch=1, grid=(K // tr,),
            in_specs=[pl.BlockSpec((tr, n), lambda i, p: (i, 0))],
            out_specs=pl.BlockSpec((None, tr, n), lambda i, p: (p[0], i, 0))),
        out_shape=jax.ShapeDtypeStruct((N_CHIPS, K, n), BF16),
        compiler_params=_params(("parallel",)),
    )(p_idx, shard)


class _Job:
    def __init__(self, ins, outs, aliases, n_sems, make):
        self.ins, self.outs, self.aliases, self.n_sems, self.make = list(ins), list(outs), list(aliases), n_sems, make


def _merge_jobs(jobs):
    ins, outs, aliases, offs = [], [], [], []
    n_sems = 0
    for jb in jobs:
        offs.append((len(ins), len(outs), n_sems))
        aliases += [(len(ins) + a, len(outs) + b) for a, b in jb.aliases]
        ins += jb.ins
        outs += jb.outs
        n_sems += jb.n_sems

    def make(in_refs, out_refs, send_sems, recv_sems, base=0):
        made = []
        for jb, (oi, oo, os_) in zip(jobs, offs):
            made.append(jb.make(in_refs[oi:oi + len(jb.ins)], out_refs[oo:oo + len(jb.outs)],
                                send_sems, recv_sems, base + os_))

        def start():
            for st, _ in made:
                st()

        def finish():
            for _, fin in made:
                fin()

        return start, finish

    return _Job(ins, outs, aliases, n_sems, make)


def _job_gather(fulls, rel=(0, 1, 2), fresh=False):
    nw = len(fulls)

    def make(in_refs, outs, send_sems, recv_sems, base=0):
        x, y, c = _axes()
        p = 2 * x + y
        chips = _other_chips(x, y)
        srcs = in_refs if fresh else outs

        def rows(w, mine):
            kh = outs[w].shape[1] // 2
            return pl.ds((c if mine else 1 - c) * kh, kh)

        def over_ici(w, j, block):
            qx, qy = chips[j]
            return pltpu.make_async_remote_copy(
                src_ref=srcs[w].at[block, rows(w, True)], dst_ref=outs[w].at[block, rows(w, True)],
                send_sem=send_sems.at[base + 6 * w + j], recv_sem=recv_sems.at[base + 6 * w + j],
                device_id=(qx, qy, c), device_id_type=MESH)

        def over_d2d(w, j, mine):
            qx, qy = chips[j]
            return pltpu.make_async_remote_copy(
                src_ref=outs[w].at[2 * qx + qy, rows(w, mine)], dst_ref=outs[w].at[2 * qx + qy, rows(w, mine)],
                send_sem=send_sems.at[base + 6 * w + 3 + j], recv_sem=recv_sems.at[base + 6 * w + 3 + j],
                device_id=(x, y, 1 - c), device_id_type=MESH)

        def start():
            for w in range(nw):
                for j in rel:
                    over_ici(w, j, p).start()

        def finish():
            for w in range(nw):
                for j in rel:
                    qx, qy = chips[j]
                    over_ici(w, j, 2 * qx + qy).wait_recv()
                    over_d2d(w, j, True).start()
            for w in range(nw):
                for j in rel:
                    over_d2d(w, j, False).wait_recv()
            for w in range(nw):
                for j in rel:
                    over_ici(w, j, p).wait_send()
                    over_d2d(w, j, True).wait_send()

        return start, finish

    return _Job(fulls, [jax.ShapeDtypeStruct(f.shape, f.dtype) for f in fulls],
                [] if fresh else [(w, w) for w in range(nw)], 6 * nw, make)


def _job_simple(ins, outs, n_per, copies_of):
    nw = len(ins)

    def make(in_refs, out_refs, send_sems, recv_sems, base=0):
        x, y, c = _axes()
        copies = []
        for w in range(nw):
            for j, (src, dst, dev) in enumerate(copies_of(w, in_refs[w], out_refs[w], x, y, c)):
                k = base + n_per * w + j
                copies.append(pltpu.make_async_remote_copy(
                    src_ref=src, dst_ref=dst, send_sem=send_sems.at[k], recv_sem=recv_sems.at[k],
                    device_id=dev, device_id_type=MESH))

        def start():
            for cp in copies:
                cp.start()

        def finish():
            for cp in copies:
                cp.wait()

        return start, finish

    return _Job(ins, outs, [], n_per * nw, make)


def _job_swap_halves(grads):
    def copies_of(w, src, dst, x, y, c):
        kh = src.shape[1] // 2
        return [(src.at[:, pl.ds((1 - c) * kh, kh), :], dst, (x, y, 1 - c))]

    outs = [jax.ShapeDtypeStruct((g.shape[0], g.shape[1] // 2, g.shape[2]), g.dtype) for g in grads]
    return _job_simple(grads, outs, 1, copies_of)


def _job_scatter_blocks(sums):
    def copies_of(w, src, dst, x, y, c):
        return [(src.at[2 * qx + qy], dst.at[j], (qx, qy, c)) for j, (qx, qy) in enumerate(_other_chips(x, y))]

    outs = [jax.ShapeDtypeStruct((3,) + s.shape[1:], s.dtype) for s in sums]
    return _job_simple(sums, outs, 3, copies_of)


def _job_to_sibling(arrs):
    def copies_of(w, src, dst, x, y, c):
        return [(src, dst, (x, y, 1 - c))]

    outs = [jax.ShapeDtypeStruct(a.shape, a.dtype) for a in arrs]
    return _job_simple(arrs, outs, 1, copies_of)


def _job_allgather8(buf):
    def make(in_refs, out_refs, send_sems, recv_sems, base=0):
        x, y, c = _axes()
        me = 4 * x + 2 * y + c
        src, dst = in_refs[0], out_refs[0]
        peers = [(1 - x if k & 4 else x, 1 - y if k & 2 else y, 1 - c if k & 1 else c) for k in range(1, N_DEV)]
        mine = pltpu.make_async_copy(src, dst.at[me], send_sems.at[base + N_DEV - 1])

        def to(k):
            return pltpu.make_async_remote_copy(
                src_ref=src, dst_ref=dst.at[me], send_sem=send_sems.at[base + k], recv_sem=recv_sems.at[base + k],
                device_id=peers[k], device_id_type=MESH)

        def of(k):
            px, py, pc = peers[k]
            return pltpu.make_async_remote_copy(
                src_ref=src, dst_ref=dst.at[4 * px + 2 * py + pc], send_sem=send_sems.at[base + k],
                recv_sem=recv_sems.at[base + k], device_id=peers[k], device_id_type=MESH)

        def start():
            mine.start()
            for k in range(N_DEV - 1):
                to(k).start()

        def finish():
            for k in range(N_DEV - 1):
                of(k).wait_recv()
                to(k).wait_send()
            mine.wait()

        return start, finish

    return _Job([buf], [jax.ShapeDtypeStruct((N_DEV,) + buf.shape, buf.dtype)], [], N_DEV, make)


def _run_job(job, name):
    ni, no = len(job.ins), len(job.outs)

    def body(*refs):
        start, finish = job.make(refs[:ni], refs[ni:ni + no], refs[-2], refs[-1])
        start()
        finish()

    any_spec = pl.BlockSpec(memory_space=pl.ANY)
    return pl.pallas_call(
        body, name=name, out_shape=job.outs, in_specs=[any_spec] * ni, out_specs=[any_spec] * no,
        input_output_aliases=dict(job.aliases),
        scratch_shapes=[pltpu.SemaphoreType.DMA((job.n_sems,)), pltpu.SemaphoreType.DMA((job.n_sems,))],
    )(*job.ins)


def _call(body, name, grid, in_specs, out_specs, out_shape, scratch_shapes, semantics, args, aliases=None, job=None,
          prefetch=()):
    in_specs, out_specs, out_shape = list(in_specs), list(out_specs), list(out_shape)
    scratch_shapes = list(scratch_shapes)
    n_pre, n_in, n_out, n_scr = len(prefetch), len(args), len(out_shape), len(scratch_shapes)
    all_aliases = {n_pre + a: b for a, b in (aliases or {}).items()}
    if job is None:
        wrapped, extra_in, semantics = body, [], semantics
    else:
        ni, no = len(job.ins), len(job.outs)

        def wrapped(*refs):
            pre, refs = refs[:n_pre], refs[n_pre:]
            ins, cins = refs[:n_in], refs[n_in:n_in + ni]
            outs, couts = refs[n_in + ni:n_in + ni + n_out], refs[n_in + ni + n_out:n_in + ni + n_out + no]
            scr = refs[n_in + ni + n_out + no:n_in + ni + n_out + no + n_scr]
            start, finish = job.make(cins, couts, refs[-2], refs[-1])
            first = functools.reduce(jnp.logical_and, [pl.program_id(a) == 0 for a in range(len(grid))])
            last = functools.reduce(jnp.logical_and, [pl.program_id(a) == grid[a] - 1 for a in range(len(grid))])
            pl.when(first)(start)
            body(*pre, *ins, *outs, *scr)
            pl.when(last)(finish)

        any_spec = pl.BlockSpec(memory_space=pl.ANY)
        for a, b in job.aliases:
            all_aliases[n_pre + n_in + a] = n_out + b
        in_specs, out_specs, out_shape = in_specs + [any_spec] * ni, out_specs + [any_spec] * no, out_shape + job.outs
        scratch_shapes = scratch_shapes + [pltpu.SemaphoreType.DMA((job.n_sems,)), pltpu.SemaphoreType.DMA((job.n_sems,))]
        extra_in, semantics = job.ins, tuple("arbitrary" for _ in grid)
    if n_pre:
        return pl.pallas_call(
            wrapped, name=name,
            grid_spec=pltpu.PrefetchScalarGridSpec(num_scalar_prefetch=n_pre, grid=grid, in_specs=in_specs,
                                                   out_specs=out_specs, scratch_shapes=scratch_shapes),
            out_shape=out_shape, input_output_aliases=all_aliases,
            compiler_params=_params(semantics))(*prefetch, *args, *extra_in)
    return pl.pallas_call(
        wrapped, name=name, grid=grid, in_specs=in_specs, out_specs=out_specs, out_shape=out_shape,
        scratch_shapes=scratch_shapes, input_output_aliases=all_aliases,
        compiler_params=_params(semantics))(*args, *extra_in)


def _add_own_half(g, r, c_idx, name):
    nb, K, n = g.shape
    kh = K // 2
    tr = _pick(kh, max(SUBLANES, (1 << 19) // n), 2 * SUBLANES)
    per = kh // tr

    def body(c_ref, g_ref, r_ref, o_ref):
        o_ref[...] = (g_ref[...] + r_ref[...]).astype(BF16)

    return pl.pallas_call(
        body, name=name,
        grid_spec=pltpu.PrefetchScalarGridSpec(
            num_scalar_prefetch=1, grid=(nb, per),
            in_specs=[pl.BlockSpec((None, tr, n), lambda b, i, c: (b, c[0] * per + i, 0)),
                      pl.BlockSpec((None, tr, n), lambda b, i, c: (b, i, 0))],
            out_specs=pl.BlockSpec((None, tr, n), lambda b, i, c: (b, i, 0))),
        out_shape=jax.ShapeDtypeStruct((nb, kh, n), BF16),
        compiler_params=_params(("parallel", "parallel")),
    )(c_idx, g, r)


def _add_chips(g, r1, r2, pc_idx, name):
    _, K, n = g.shape
    kh = K // 2
    tr = _pick(kh, max(SUBLANES, (1 << 19) // n), 2 * SUBLANES)
    per = kh // tr

    def body(pc_ref, g_ref, r1_ref, r2_ref, o_ref):
        own = g_ref[...] + r1_ref[...]
        o_ref[...] = ((own + r2_ref[0].astype(F32)) + r2_ref[1].astype(F32)) + r2_ref[2].astype(F32)

    return pl.pallas_call(
        body, name=name,
        grid_spec=pltpu.PrefetchScalarGridSpec(
            num_scalar_prefetch=1, grid=(per,),
            in_specs=[pl.BlockSpec((None, tr, n), lambda i, pc: (pc[0], pc[1] * per + i, 0)),
                      pl.BlockSpec((None, tr, n), lambda i, pc: (pc[0], i, 0)),
                      pl.BlockSpec((3, tr, n), lambda i, pc: (0, i, 0))],
            out_specs=pl.BlockSpec((tr, n), lambda i, pc: (i, 0))),
        out_shape=jax.ShapeDtypeStruct((kh, n), F32),
        compiler_params=_params(("parallel",)),
    )(pc_idx, g, r1, r2)


def _sum8(g, name):
    _, R, L = g.shape

    def body(g_ref, o_ref):
        acc = g_ref[0]
        for d in range(1, N_DEV):
            acc = acc + g_ref[d]
        o_ref[...] = acc

    return pl.pallas_call(
        body, name=name, out_shape=jax.ShapeDtypeStruct((R, L), F32),
        in_specs=[pl.BlockSpec(memory_space=pltpu.VMEM)], out_specs=pl.BlockSpec(memory_space=pltpu.VMEM),
        compiler_params=_params(),
    )(g)


def _adamw_math(w, gg, m, v):
    nm = ADAM_B1 * m + (1.0 - ADAM_B1) * gg
    nv = ADAM_B2 * v + (1.0 - ADAM_B2) * jnp.square(gg)
    m_hat = nm / (1.0 - ADAM_B1 ** ADAM_STEP)
    v_hat = nv / (1.0 - ADAM_B2 ** ADAM_STEP)
    return -ADAM_LR * (m_hat / (jnp.sqrt(v_hat) + ADAM_EPS) + ADAM_WD * w), nm, nv


def _adamw(w, g, m, v, name):
    R, C = w.shape
    tr = _pick(R, max(SUBLANES, (1 << 18) // C), SUBLANES)

    def body(w_ref, g_ref, m_ref, v_ref, d_ref, nm_ref, nv_ref):
        d_ref[...], nm_ref[...], nv_ref[...] = _adamw_math(w_ref[...], g_ref[...], m_ref[...], v_ref[...])

    spec = pl.BlockSpec((tr, C), lambda i: (i, 0))
    sd = jax.ShapeDtypeStruct((R, C), F32)
    return pl.pallas_call(
        body, name=name, grid=(R // tr,), in_specs=[spec] * 4, out_specs=[spec] * 3, out_shape=[sd] * 3,
        compiler_params=_params(("parallel",)),
    )(w, g, m, v)


def _adamw_halves(w, g_own, g_sib, m, v, c_idx, name):
    K, n = w.shape
    kh = K // 2
    tr = _pick(kh, max(SUBLANES, (1 << 18) // n), SUBLANES)
    per = kh // tr

    def body(c_ref, w_ref, go_ref, gs_ref, m_ref, v_ref, g_ref, d_ref, nm_ref, nv_ref):
        h = pl.program_id(0)

        def step(gg):
            g_ref[...] = gg
            d_ref[...], nm_ref[...], nv_ref[...] = _adamw_math(w_ref[...], gg, m_ref[...], v_ref[...])

        @pl.when(h == 0)
        def _():
            step(go_ref[...])

        @pl.when(h == 1)
        def _():
            step(gs_ref[...])

    full = pl.BlockSpec((tr, n), lambda h, i, c: (((c[0] + h) % 2) * per + i, 0))
    own = pl.BlockSpec((tr, n), lambda h, i, c: (i * (1 - h), 0))
    sib = pl.BlockSpec((tr, n), lambda h, i, c: (i * h, 0))
    sd = jax.ShapeDtypeStruct((K, n), F32)
    return pl.pallas_call(
        body, name=name,
        grid_spec=pltpu.PrefetchScalarGridSpec(
            num_scalar_prefetch=1, grid=(2, per),
            in_specs=[full, own, sib, full, full], out_specs=[full] * 4),
        out_shape=[sd] * 4,
        compiler_params=_params(("arbitrary", "arbitrary")),
    )(c_idx, w, g_own, g_sib, m, v)


def _matmul(name, grid, dims, a, a_spec, b, b_spec, outs, out_specs, acc_shape,
            extras=(), extra_specs=(), epilogue=None, job=None, fill_into=None, prefetch=()):
    nk = grid[2]
    npre = len(prefetch)
    aliases = None
    if fill_into is not None:
        aliases = {2 + len(extras): 0}
        extras = tuple(extras) + (fill_into,)
        extra_specs = tuple(extra_specs) + (pl.BlockSpec(memory_space=pl.ANY),)
    nex = len(extras)
    nout = len(outs)

    def body(*refs):
        a_ref, b_ref, rest = refs[npre], refs[npre + 1], refs[npre + 2:]
        ex, o = rest[:nex], rest[nex:nex + nout]
        part = lax.dot_general(a_ref[...].astype(BF16), b_ref[...].astype(BF16), dims,
                               preferred_element_type=F32)

        def finish(res):
            if epilogue is None:
                o[0][...] = res.astype(o[0].dtype)
            else:
                epilogue(res, ex, o)

        if nk == 1:
            finish(part)
        else:
            acc = o[0] if in_place else rest[-1]
            k = pl.program_id(2)

            @pl.when(k == 0)
            def _():
                acc[...] = part

            @pl.when(k > 0)
            def _():
                acc[...] += part

            if not in_place:
                @pl.when(k == nk - 1)
                def _():
                    finish(acc[...])

    in_place = epilogue is None and nout == 1 and outs[0].dtype == F32
    res = _call(body, name, grid, [a_spec, b_spec] + list(extra_specs), out_specs, outs,
                [] if nk == 1 or in_place else [pltpu.VMEM(acc_shape, F32)], ("parallel", "parallel", "arbitrary"),
                (a, b, *extras), aliases=aliases, job=job, prefetch=prefetch)
    return res if job is None else (res[:nout], res[nout:])


def _first(res, job):
    return res[0] if job is None else (res[0][0], res[1])


def _mm_fwd(name, a, wfull, out_dtype=F32, planes=1, bias=None, job=None):
    S, K = a.shape
    _, _, n = wfull.shape
    N = N_CHIPS * n
    tm = _pick(S, 1024, LANES)
    tn = _pick(n, 1408, LANES)
    per = n // tn
    nj = N // tn
    pj = nj // planes
    grid = (S // tm, nj, 1)
    a_spec = pl.BlockSpec((tm, K), lambda i, j, k: (i, 0))
    b_spec = pl.BlockSpec((None, K, tn), lambda i, j, k: (j // per, 0, j % per))
    o_spec = pl.BlockSpec((None, tm, tn), lambda i, j, k: (j // pj, i, j % pj))
    sd = jax.ShapeDtypeStruct((planes, S, N // planes), out_dtype)
    if bias is not None:
        def epi(res, ex, o):
            o[0][...] = (res + ex[0][...]).astype(o[0].dtype)

        out = _matmul(name, grid, NN, a, a_spec, wfull, b_spec, [sd], [o_spec], (tm, tn), extras=(bias,),
                      extra_specs=(pl.BlockSpec((1, tn), lambda i, j, k: (0, j)),), epilogue=epi, job=job)
    else:
        out = _matmul(name, grid, NN, a, a_spec, wfull, b_spec, [sd], [o_spec], (tm, tn), job=job)
    return _first(out, job)


def _mm_fwd_block(name, a, wfull, bias, p_idx, mask, into=None, job=None):
    S, K = a.shape
    _, _, n = wfull.shape
    N = N_CHIPS * n
    tm = _pick(S, 1024, LANES)
    tn = _pick(n, 1408, LANES)
    per = n // tn
    grid = (S // tm, per, 1)
    a_spec = pl.BlockSpec((tm, K), lambda i, j, k, p: (i, 0))
    b_spec = pl.BlockSpec((None, K, tn), lambda i, j, k, p: (jnp.bitwise_xor(p[0], mask), 0, j))
    o_spec = pl.BlockSpec((tm, tn), lambda i, j, k, p: (i, jnp.bitwise_xor(p[0], mask) * per + j))
    v_spec = pl.BlockSpec((1, tn), lambda i, j, k, p: (0, jnp.bitwise_xor(p[0], mask) * per + j))

    def epi(res, ex, o):
        o[0][...] = res + ex[0][...]

    return _matmul(name, grid, NN, a, a_spec, wfull, b_spec, [jax.ShapeDtypeStruct((S, N), F32)], [o_spec], (tm, tn),
                   extras=(bias,), extra_specs=(v_spec,), epilogue=epi, job=job, fill_into=into, prefetch=(p_idx,))


def _copy_block(dst, src, p_idx, mask, name):
    _, K, n = dst.shape
    tr = _pick(K, max(SUBLANES, (1 << 19) // n), 2 * SUBLANES)

    def body(p_ref, s_ref, d_any, o_ref):
        o_ref[...] = s_ref[...]

    spec = pl.BlockSpec((None, tr, n), lambda i, p: (jnp.bitwise_xor(p[0], mask), i, 0))
    return _call(body, name, (K // tr,), [spec, pl.BlockSpec(memory_space=pl.ANY)], [spec],
                 [jax.ShapeDtypeStruct(dst.shape, dst.dtype)], [], ("parallel",), (src, dst), aliases={1: 0},
                 prefetch=(p_idx,))[0]


def _mm_fwd_rows(name, a, wrows, resid, gate, job=None):
    S, K = a.shape
    _, N = wrows.shape
    tm = _pick(S, 512, LANES)
    tk = _pick(K, 2816, LANES)
    tn = N if K * N * 2 <= WHOLE_WEIGHT_BYTES else _pick(N, 1024, LANES)
    grid = (S // tm, N // tn, K // tk)
    a_spec = pl.BlockSpec((tm, tk), lambda i, j, k: (i, k))
    if tn == N and tk == K:
        b_spec = pl.BlockSpec((tk, tn), lambda i, j, k: (k, j), pipeline_mode=pl.Buffered(1))
    else:
        b_spec = pl.BlockSpec((tk, tn), lambda i, j, k: (k, j))
    o_spec = pl.BlockSpec((tm, tn), lambda i, j, k: (i, j))
    g_spec = pl.BlockSpec((1, tn), lambda i, j, k: (0, j))

    def epi(res, ex, o):
        o[0][...] = res
        o[1][...] = ex[0][...] + ex[1][...] * res

    sd = jax.ShapeDtypeStruct((S, N), F32)
    return _matmul(name, grid, NN, a, a_spec, wrows, b_spec, [sd, sd], [o_spec, o_spec], (tm, tn),
                   extras=(resid, gate), extra_specs=(o_spec, g_spec), epilogue=epi, job=job)


def _mm_dgrad_cols(name, dy, wfull, out_dtype=F32, job=None, row_tiles=None, fill_into=None):
    planes, S, npl = dy.shape
    _, K, n = wfull.shape
    tm = _pick(S if row_tiles is None else S // 2, 1024, LANES)
    to = _pick(K, 1024, LANES)
    tk = _pick(n, 2816, LANES)
    per = n // tk
    nk = N_CHIPS * per
    pk = nk // planes
    n_tiles = S // tm
    i0, ni = (0, n_tiles) if row_tiles is None else (int(row_tiles[0] * n_tiles), int(row_tiles[1] * n_tiles))
    grid = (ni, K // to, nk)
    a_spec = pl.BlockSpec((None, tm, tk), lambda i, j, k: (k // pk, i + i0, k % pk))
    b_spec = pl.BlockSpec((None, to, tk), lambda i, j, k: (k // per, j, k % per))
    o_spec = pl.BlockSpec((tm, to), lambda i, j, k: (i + i0, j))
    return _first(_matmul(name, grid, NT, dy, a_spec, wfull, b_spec, [jax.ShapeDtypeStruct((S, K), out_dtype)],
                          [o_spec], (tm, to), job=job, fill_into=fill_into), job)


def _mm_dgrad_rows(name, dy, wrows, out_dtype=F32, job=None):
    S, N = dy.shape
    K, _ = wrows.shape
    tm = _pick(S, 1024, LANES)
    to = _pick(K, 1408, LANES)
    tk = _pick(N, 2048, LANES)
    grid = (S // tm, K // to, N // tk)
    a_spec = pl.BlockSpec((tm, tk), lambda i, j, k: (i, k))
    b_spec = pl.BlockSpec((to, tk), lambda i, j, k: (j, k))
    o_spec = pl.BlockSpec((tm, to), lambda i, j, k: (i, j))
    return _first(_matmul(name, grid, NT, dy, a_spec, wrows, b_spec, [jax.ShapeDtypeStruct((S, K), out_dtype)],
                          [o_spec], (tm, to), job=job), job)


def _mm_wgrad_cols(name, a, dy, n, job=None):
    S, K = a.shape
    planes, _, npl = dy.shape
    N = planes * npl
    to = _pick(K, 1024, LANES)
    tn = _pick(n, 1408, LANES)
    ts = _pick(S, 2048, LANES)
    per = n // tn
    nj = N // tn
    pj = nj // planes
    grid = (K // to, nj, S // ts)
    a_spec = pl.BlockSpec((ts, to), lambda i, j, k: (k, i))
    b_spec = pl.BlockSpec((None, ts, tn), lambda i, j, k: (j // pj, k, j % pj))
    o_spec = pl.BlockSpec((None, to, tn), lambda i, j, k: (j // per, i, j % per))
    return _first(_matmul(name, grid, TN, a, a_spec, dy, b_spec, [jax.ShapeDtypeStruct((N_CHIPS, K, n), F32)],
                          [o_spec], (to, tn), job=job), job)


def _mm_wgrad_rows(name, a, dy, kshard, job=None):
    S, K = a.shape
    _, N = dy.shape
    to = _pick(kshard, 1408, LANES)
    tn = _pick(N, 1024, LANES)
    ts = _pick(S, 2048, LANES)
    per = kshard // to
    grid = (K // to, N // tn, S // ts)
    a_spec = pl.BlockSpec((ts, to), lambda i, j, k: (k, i))
    b_spec = pl.BlockSpec((ts, tn), lambda i, j, k: (k, j))
    o_spec = pl.BlockSpec((None, to, tn), lambda i, j, k: (i // per, i % per, j))
    return _first(_matmul(name, grid, TN, a, a_spec, dy, b_spec, [jax.ShapeDtypeStruct((N_CHIPS, kshard, N), F32)],
                          [o_spec], (to, tn), job=job), job)


def _rows(tm, width, colblk=0):
    return pl.BlockSpec((tm, width), lambda i: (i, colblk))


def _whole(shape):
    nd = len(shape)
    return pl.BlockSpec(shape, lambda i: (0,) * nd)


def _prev_halo(tm, h, width, colblk=0):
    r = tm // h
    return pl.BlockSpec((h, width), lambda i: (jnp.maximum(i * r - 1, 0), colblk))


def _next_halo(tm, h, width, nblk, colblk=0):
    r = tm // h
    return pl.BlockSpec((h, width), lambda i: (jnp.minimum((i + 1) * r, nblk - 1), colblk))


def _accumulate(i, ref, val):
    @pl.when(i == 0)
    def _():
        ref[...] = val

    @pl.when(i > 0)
    def _():
        ref[...] += val


def _fill_rotations(rot, offs):
    n = rot.shape[1]
    for r in sorted({o % SUBLANES for o in offs} - {0}):
        rot[r, 0:n - SUBLANES, :] = rot[0, r:r + n - SUBLANES, :]


def _tap_windows(rot, offs, row0, rb, lanes):
    by_res = {}
    for k, o in enumerate(offs):
        by_res.setdefault(o % SUBLANES, []).append((o // SUBLANES, k))
    for r, taps in by_res.items():
        lo = min(a for a, _ in taps)
        hi = max(a for a, _ in taps)
        win = rot[r, pl.ds(row0 + SUBLANES * lo, rb + SUBLANES * (hi - lo)), lanes]
        for a, k in taps:
            yield k, win[SUBLANES * (a - lo):SUBLANES * (a - lo) + rb, :]


def _for_chunks(n_rows, rb, fn):
    def step(j, carry):
        fn(pl.multiple_of(j * rb, rb))
        return carry

    lax.fori_loop(0, n_rows // rb, step, 0)


def _modnorm_fwd(name, x, g, scale, shift):
    S, D = x.shape
    tm = _pick(S, 512, LANES)

    def body(x_ref, g_ref, sc_ref, sh_ref, h_ref):
        h_ref[...] = _modnorm(x_ref[...], g_ref[...], sc_ref[...], sh_ref[...]).astype(BF16)

    vec = _whole((1, D))
    return pl.pallas_call(
        body, name=name, grid=(S // tm,), in_specs=[_rows(tm, D), vec, vec, vec], out_specs=_rows(tm, D),
        out_shape=jax.ShapeDtypeStruct((S, D), BF16), compiler_params=_params(("parallel",)),
    )(x, g, scale, shift)


def _modnorm_bwd(name, x, dh, dx_in, g, scale, shift, gated=None):
    S, D = x.shape
    tm = _pick(S, 256, LANES)

    def body(x_ref, dh_ref, dxin_ref, g_ref, sc_ref, sh_ref, *rest):
        i = pl.program_id(0)
        dx_ref, dg_ref, dsc_ref, dsh_ref = rest[-4:] if gated is None else rest[2:6]
        _, pull = jax.vjp(_modnorm, x_ref[...], g_ref[...], sc_ref[...], sh_ref[...])
        dx, dg, dsc, dsh = pull(dh_ref[...])
        dx = dxin_ref[...] + dx
        dx_ref[...] = dx
        _accumulate(i, dg_ref, dg)
        _accumulate(i, dsc_ref, dsc)
        _accumulate(i, dsh_ref, dsh)
        if gated is not None:
            _gate_bwd_tile(i, dx, rest[0], rest[1], rest[6], rest[7])

    vec = _whole((1, D))
    row = _rows(tm, D)
    vsd = jax.ShapeDtypeStruct((1, D), F32)
    in_specs, args = [row, row, row, vec, vec, vec], (x, dh, dx_in, g, scale, shift)
    out_specs, out_shape = [row, vec, vec, vec], [jax.ShapeDtypeStruct((S, D), F32), vsd, vsd, vsd]
    if gated is not None:
        in_specs, args = in_specs + [row, vec], args + tuple(gated)
        out_specs, out_shape = out_specs + [row, vec], out_shape + [jax.ShapeDtypeStruct((S, D), BF16), vsd]
    return _call(body, name, (S // tm,), in_specs, out_specs, out_shape, [], ("arbitrary",), args)


def _conv_fwd(name, proj, w, b, lg, lb, dc, job=None):
    S = proj.shape[0]
    K = w.shape[0]
    H = CONV_HALO
    tm = _pick(S, 256, LANES)

    def glu(v):
        return v[:, :dc] * jax.nn.sigmoid(v[:, dc:])

    offs = [H - (K - 1) + k for k in range(K)]
    rb = _pick(tm, CONV_CHUNK_ROWS, SUBLANES)
    lw = min(LANES, dc)

    def body(cur_ref, prev_ref, w_ref, b_ref, lg_ref, lb_ref, conv_ref, act_ref, rot):
        i = pl.program_id(0)
        rot[0, 0:H, :] = jnp.where(i > 0, glu(prev_ref[...]), 0.0)
        rot[0, H:, :] = glu(cur_ref[...])
        _fill_rotations(rot, offs)

        def chunk(row0):
            for l0 in range(0, dc, lw):
                lanes = slice(l0, l0 + lw)
                acc = jnp.broadcast_to(b_ref[:, lanes], (rb, lw))
                for k, win in _tap_windows(rot, offs, row0, rb, lanes):
                    acc = acc + w_ref[k:k + 1, lanes] * win
                conv_ref[pl.ds(row0, rb), lanes] = acc

        _for_chunks(tm, rb, chunk)
        act_ref[...] = _ln_silu(conv_ref[...], lg_ref[...], lb_ref[...]).astype(BF16)

    vec = _whole((1, dc))
    res = _call(body, name, (S // tm,),
                [_rows(tm, 2 * dc), _prev_halo(tm, H, 2 * dc), _whole(w.shape), vec, vec, vec],
                [_rows(tm, dc), _rows(tm, dc)],
                [jax.ShapeDtypeStruct((S, dc), F32), jax.ShapeDtypeStruct((S, dc), BF16)],
                [pltpu.VMEM((SUBLANES, tm + H, dc), F32)], ("parallel",), (proj, proj, w, b, lg, lb), job=job)
    return res if job is None else (res[:2], res[2:])


def _conv_bwd_ln(name, conv, dact, lg, lb):
    S, dc = conv.shape
    tm = _pick(S, 256, LANES)

    def body(c_ref, d_ref, lg_ref, lb_ref, dc_ref, dlg_ref, dlb_ref, db_ref):
        i = pl.program_id(0)
        _, pull = jax.vjp(_ln_silu, c_ref[...], lg_ref[...], lb_ref[...])
        dcv, dlg, dlb = pull(d_ref[...])
        dc_ref[...] = dcv
        _accumulate(i, dlg_ref, dlg)
        _accumulate(i, dlb_ref, dlb)
        _accumulate(i, db_ref, jnp.sum(dcv, axis=0, keepdims=True))

    vec = _whole((1, dc))
    row = _rows(tm, dc)
    vsd = jax.ShapeDtypeStruct((1, dc), F32)
    return pl.pallas_call(
        body, name=name, grid=(S // tm,), in_specs=[row, row, vec, vec], out_specs=[row, vec, vec, vec],
        out_shape=[jax.ShapeDtypeStruct((S, dc), F32), vsd, vsd, vsd],
        compiler_params=_params(("arbitrary",)),
    )(conv, dact, lg, lb)


def _conv_bwd(name, dconv, proj, w, dproj, dc, job=None):
    S = proj.shape[0]
    K = w.shape[0]
    H = CONV_HALO
    tm = _pick(S, 256, LANES)
    nt = S // tm

    offs_g = [H - (K - 1) + k for k in range(K)]
    offs_d = [K - 1 - k for k in range(K)]
    rb = _pick(tm, CONV_CHUNK_ROWS, SUBLANES)
    lw = min(LANES, dc)
    kp = -(-K // SUBLANES) * SUBLANES

    def body(d_ref, dn_ref, cur_ref, prev_ref, w_ref, dproj_any, da_ref, dw_ref, dbin_ref, rotg, rotd, accw, dglu_s):
        i = pl.program_id(0)
        pv = prev_ref[...]
        cv = cur_ref[...]
        sig = jax.nn.sigmoid(cv[:, dc:])
        rotg[0, 0:H, :] = jnp.where(i > 0, pv[:, :dc] * jax.nn.sigmoid(pv[:, dc:]), 0.0)
        rotg[0, H:, :] = cv[:, :dc] * sig
        rotd[0, 0:tm, :] = d_ref[...]
        rotd[0, tm:, :] = jnp.where(i < nt - 1, dn_ref[...], 0.0)
        _fill_rotations(rotg, offs_g)
        _fill_rotations(rotd, offs_d)

        @pl.when(i == 0)
        def _():
            accw[...] = jnp.zeros_like(accw)

        for l0 in range(0, dc, lw):
            lanes = slice(l0, l0 + lw)

            def chunk(j, sums, lanes=lanes):
                row0 = pl.multiple_of(j * rb, rb)
                dcur = rotd[0, pl.ds(row0, rb), lanes]
                acc = jnp.zeros((rb, lw), F32)
                for k, win in _tap_windows(rotd, offs_d, row0, rb, lanes):
                    acc = acc + w_ref[k:k + 1, lanes] * win
                new = list(sums)
                for k, win in _tap_windows(rotg, offs_g, row0, rb, lanes):
                    new[k] = sums[k] + jnp.sum((dcur * win).reshape(rb // SUBLANES, SUBLANES, lw), axis=0)
                dglu_s[pl.ds(row0, rb), lanes] = acc
                return tuple(new)

            sums = lax.fori_loop(0, tm // rb, chunk, tuple(jnp.zeros((SUBLANES, lw), F32) for _ in range(K)))
            for k in range(K):
                accw[SUBLANES * k:SUBLANES * (k + 1), lanes] += sums[k]
        dglu = dglu_s[...]
        da = jnp.concatenate([dglu * sig, dglu * cv[:, :dc] * sig * (1.0 - sig)], axis=1)
        da_ref[...] = da.astype(BF16)
        _accumulate(i, dbin_ref, jnp.sum(da, axis=0, keepdims=True))

        @pl.when(i == nt - 1)
        def _():
            dw_ref[...] = jnp.zeros_like(dw_ref)
            for k in range(K):
                dw_ref[k:k + 1, :] = jnp.sum(accw[SUBLANES * k:SUBLANES * (k + 1), :], axis=0, keepdims=True)

    res = _call(
        body, name, (nt,),
        [_rows(tm, dc), _next_halo(tm, H, dc, S // H), _rows(tm, 2 * dc), _prev_halo(tm, H, 2 * dc),
         _whole(w.shape), pl.BlockSpec(memory_space=pl.ANY)],
        [_rows(tm, 2 * dc), _whole((kp, dc)), _whole((1, 2 * dc))],
        [jax.ShapeDtypeStruct(dproj.shape, BF16), jax.ShapeDtypeStruct((kp, dc), F32),
         jax.ShapeDtypeStruct((1, 2 * dc), F32)],
        [pltpu.VMEM((SUBLANES, tm + H, dc), F32), pltpu.VMEM((SUBLANES, tm + H, dc), F32),
         pltpu.VMEM((SUBLANES * K, dc), F32), pltpu.VMEM((tm, dc), F32)],
        ("arbitrary",), (dconv, dconv, proj, proj, w, dproj), aliases={5: 0}, job=job)
    return res if job is None else (res[:3], res[3:])


def _mix(vln, wsm, bst, out_ref, G, CH, hd):
    for n in range(vln.shape[0] // CH):
        for g in range(G):
            blk = vln[n * CH:(n + 1) * CH, g * hd:(g + 1) * hd].astype(BF16)
            out_ref[n * CH:(n + 1) * CH, g * hd:(g + 1) * hd] = (
                jnp.dot(wsm[g], blk, preferred_element_type=F32) + bst[:, g:g + 1])


def _blocks_to_columns(full, name):
    nb, K, n = full.shape
    tr = _pick(K, max(SUBLANES, (1 << 19) // n), 2 * SUBLANES)

    def body(s_ref, o_ref):
        o_ref[...] = s_ref[...]

    return pl.pallas_call(
        body, name=name, grid=(nb, K // tr),
        in_specs=[pl.BlockSpec((None, tr, n), lambda p, i: (p, i, 0))],
        out_specs=pl.BlockSpec((tr, n), lambda p, i: (i, p)),
        out_shape=jax.ShapeDtypeStruct((K, nb * n), full.dtype),
        compiler_params=_params(("parallel", "parallel")))(full)


def _sgu_fwd(name, proj, a_act, wco, wso, lg, lb, ws, bst, D, ds, job=None):
    S = proj.shape[0]
    dc = a_act.shape[1]
    G, CH, _ = ws.shape
    hd = ds // G
    tm = _pick(S, 256, CH)

    def body(s_ref, gt_ref, a_ref, wco_ref, wso_ref, lg_ref, lb_ref, ws_ref, bst_ref,
             ya_ref, yb_ref, uv_ref, mg_ref, vmix):
        z = _gelu(s_ref[...])
        vln = _layer_norm(z[:, ds:], lg_ref[...], lb_ref[...])
        wsm = [_tril_mask(ws_ref[g]).astype(BF16) for g in range(G)]
        _mix(vln, wsm, bst_ref[...], vmix, G, CH, hd)
        uv = (z[:, :ds] * vmix[...]).astype(BF16)
        uv_ref[...] = uv
        ya = jnp.dot(a_ref[...], wco_ref[...], preferred_element_type=F32)
        yb = jnp.dot(uv, wso_ref[...], preferred_element_type=F32)
        ya_ref[...] = ya.astype(BF16)
        yb_ref[...] = yb.astype(BF16)
        gt = gt_ref[...]
        mg_ref[...] = (jax.nn.sigmoid(gt[:, :D]) * ya + jax.nn.sigmoid(gt[:, D:]) * yb).astype(BF16)

    vec = _whole((1, ds))
    sdb = jax.ShapeDtypeStruct((S, D), BF16)
    res = _call(body, name, (S // tm,),
                [_rows(tm, 2 * ds, 1), _rows(tm, 2 * D, 1), _rows(tm, dc), _whole(wco.shape), _whole(wso.shape),
                 vec, vec, _whole(ws.shape), _whole(bst.shape)],
                [_rows(tm, D), _rows(tm, D), _rows(tm, ds), _rows(tm, D)],
                [sdb, sdb, jax.ShapeDtypeStruct((S, ds), BF16), sdb],
                [pltpu.VMEM((tm, ds), F32)], ("parallel",), (proj, proj, a_act, wco, wso, lg, lb, ws, bst), job=job)
    return res if job is None else (res[:4], res[4:])


def _merge_bwd(name, dmerged, proj, ya, yb, D):
    S = proj.shape[0]
    tm = _pick(S, 256, LANES)

    def body(dm_ref, gt_ref, ya_ref, yb_ref, dya_ref, dyb_ref, dg_ref, dbin_ref):
        i = pl.program_id(0)
        dm = dm_ref[...]
        gt = gt_ref[...]
        sa = jax.nn.sigmoid(gt[:, :D])
        sb = jax.nn.sigmoid(gt[:, D:])
        dya_ref[...] = (dm * sa).astype(BF16)
        dyb_ref[...] = (dm * sb).astype(BF16)
        ya = ya_ref[...].astype(F32)
        yb = yb_ref[...].astype(F32)
        dg = jnp.concatenate([dm * ya * sa * (1.0 - sa), dm * yb * sb * (1.0 - sb)], axis=1)
        dg_ref[...] = dg.astype(BF16)
        _accumulate(i, dbin_ref, jnp.sum(dg, axis=0, keepdims=True))

    row = _rows(tm, D)
    sdb = jax.ShapeDtypeStruct((S, D), BF16)
    return pl.pallas_call(
        body, name=name, grid=(S // tm,),
        in_specs=[row, _rows(tm, 2 * D, 1), row, row],
        out_specs=[row, row, _rows(tm, 2 * D, 1), _whole((1, 2 * D))],
        out_shape=[sdb, sdb, jax.ShapeDtypeStruct((S, 4 * D), BF16), jax.ShapeDtypeStruct((1, 2 * D), F32)],
        compiler_params=_params(("arbitrary",)),
    )(dmerged, proj, ya, yb)


def _sgu_bwd(name, proj, duv, lg, lb, ws, bst, dproj, ds):
    S = proj.shape[0]
    G, CH, _ = ws.shape
    hd = ds // G
    tm = _pick(S, 256, CH)

    def body(s_ref, duv_ref, lg_ref, lb_ref, ws_ref, bst_ref, dproj_any,
             dsin_ref, dws_ref, dbs_ref, dlg_ref, dlb_ref, dbin_ref, vmix, dvln):
        i = pl.program_id(0)
        z, pull_gelu = jax.vjp(_gelu, s_ref[...])
        u = z[:, :ds]
        vln, pull_ln = jax.vjp(_layer_norm, z[:, ds:], lg_ref[...], lb_ref[...])
        wsm = [_tril_mask(ws_ref[g]).astype(BF16) for g in range(G)]
        _mix(vln, wsm, bst_ref[...], vmix, G, CH, hd)
        duv = duv_ref[...]
        du = duv * vmix[...]
        dvmix = duv * u
        for g in range(G):
            dws_g = jnp.zeros((CH, CH), F32)
            dbs_g = jnp.zeros((CH, 1), F32)
            for n in range(tm // CH):
                dblk = dvmix[n * CH:(n + 1) * CH, g * hd:(g + 1) * hd]
                vblk = vln[n * CH:(n + 1) * CH, g * hd:(g + 1) * hd].astype(BF16)
                dvln[n * CH:(n + 1) * CH, g * hd:(g + 1) * hd] = lax.dot_general(
                    wsm[g], dblk.astype(BF16), TN, preferred_element_type=F32)
                dws_g = dws_g + lax.dot_general(dblk.astype(BF16), vblk, NT, preferred_element_type=F32)
                dbs_g = dbs_g + jnp.sum(dblk, axis=1, keepdims=True)
            dws_g = _tril_mask(dws_g)
            dbs_g = jnp.broadcast_to(dbs_g, (CH, LANES))

            @pl.when(i == 0)
            def _():
                dws_ref[g] = dws_g
                dbs_ref[g] = dbs_g

            @pl.when(i > 0)
            def _():
                dws_ref[g] += dws_g
                dbs_ref[g] += dbs_g

        dv, dlg, dlb = pull_ln(dvln[...])
        (dsin,) = pull_gelu(jnp.concatenate([du, dv], axis=1))
        dsin_ref[...] = dsin.astype(BF16)
        _accumulate(i, dlg_ref, dlg)
        _accumulate(i, dlb_ref, dlb)
        _accumulate(i, dbin_ref, jnp.sum(dsin, axis=0, keepdims=True))

    vec = _whole((1, ds))
    vsd = jax.ShapeDtypeStruct((1, ds), F32)
    return pl.pallas_call(
        body, name=name, grid=(S // tm,),
        in_specs=[_rows(tm, 2 * ds, 1), _rows(tm, ds), vec, vec, _whole(ws.shape), _whole(bst.shape),
                  pl.BlockSpec(memory_space=pl.ANY)],
        out_specs=[_rows(tm, 2 * ds, 1), _whole((G, CH, CH)), _whole((G, CH, LANES)), vec, vec, _whole((1, 2 * ds))],
        out_shape=[jax.ShapeDtypeStruct(dproj.shape, BF16), jax.ShapeDtypeStruct((G, CH, CH), F32),
                   jax.ShapeDtypeStruct((G, CH, LANES), F32), vsd, vsd, jax.ShapeDtypeStruct((1, 2 * ds), F32)],
        scratch_shapes=[pltpu.VMEM((tm, ds), F32), pltpu.VMEM((tm, ds), F32)],
        input_output_aliases={6: 0},
        compiler_params=_params(("arbitrary",)),
    )(proj, duv, lg, lb, ws, bst, dproj)


def _silu_mul(val, gt):
    return jax.nn.silu(gt) * val


def _rotation_slots(offs):
    slot = {0: 0}
    for r in sorted({o % SUBLANES for o in offs} - {0}):
        slot[r] = len(slot)
    return slot


def _fill_plane_rotations(rot, slot):
    n = rot.shape[2]
    for r, s in slot.items():
        if r:
            rot[:, s, 0:n - SUBLANES, :] = rot[:, 0, r:r + n - SUBLANES, :]


def _ffn_tiles(S, Fh):
    return _pick(S, 256, LANES), _pick(Fh, 1408, LANES)


def _ffn_fwd(name, up, w, b):
    _, S, Fh = up.shape
    K = w.shape[1]
    H = FFN_HALO
    tm, cw = _ffn_tiles(S, Fh)
    r = tm // H

    offs = [H - (K - 1) + k for k in range(K)]
    slot = _rotation_slots(offs)
    rb = _pick(tm, 64, 2 * SUBLANES)
    lw = min(LANES, cw)

    def body(cur_ref, prev_ref, w_ref, b_ref, act_ref, rot):
        i = pl.program_id(1)
        rot[:, 0, 0:H, :] = jnp.where(i > 0, prev_ref[...], 0.0)
        rot[:, 0, H:, :] = cur_ref[...]
        _fill_plane_rotations(rot, slot)

        def chunk(row0):
            for l0 in range(0, cw, lw):
                lanes = slice(l0, l0 + lw)
                c2 = []
                for pln in range(2):
                    acc = jnp.broadcast_to(b_ref[pln, :, lanes], (rb, lw))
                    for k in range(K):
                        a, rr = divmod(offs[k], SUBLANES)
                        acc = acc + w_ref[pln, k:k + 1, lanes] * rot[pln, slot[rr], pl.ds(row0 + SUBLANES * a, rb), lanes]
                    c2.append(acc)
                act_ref[pl.ds(row0, rb), lanes] = _silu_mul(c2[0], c2[1]).astype(BF16)

        _for_chunks(tm, rb, chunk)

    return pl.pallas_call(
        body, name=name, grid=(Fh // cw, S // tm),
        in_specs=[pl.BlockSpec((2, tm, cw), lambda j, i: (0, i, j)),
                  pl.BlockSpec((2, H, cw), lambda j, i: (0, jnp.maximum(i * r - 1, 0), j)),
                  pl.BlockSpec((2, K, cw), lambda j, i: (0, 0, j)),
                  pl.BlockSpec((2, 1, cw), lambda j, i: (0, 0, j))],
        out_specs=pl.BlockSpec((tm, cw), lambda j, i: (i, j)),
        out_shape=jax.ShapeDtypeStruct((S, Fh), BF16),
        scratch_shapes=[pltpu.VMEM((2, len(slot), tm + H, cw), F32)],
        compiler_params=_params(("parallel", "parallel")),
    )(up, up, w, b)


def _ffn_bwd(name, up, dact, w, b, job=None):
    _, S, Fh = up.shape
    K = w.shape[1]
    H = FFN_HALO
    tm, cw = _ffn_tiles(S, Fh)
    r = tm // H
    nt = S // tm
    nhb = S // H
    te = tm + H

    offs_x = [H - (K - 1) + k for k in range(K)]
    offs_d = [K - 1 - k for k in range(K)]
    slot_x = _rotation_slots(offs_x)
    slot_d = _rotation_slots(offs_d)
    rb = _pick(tm, 64, 2 * SUBLANES)
    rbe = _pick(te, 96, SUBLANES)
    lw = min(LANES, cw)

    def body(cur_ref, prev_ref, next_ref, d_ref, dn_ref, w_ref, b_ref, dup_ref, dwb_ref, rotx, dext, rotd, accw):
        i = pl.program_id(1)
        rotx[:, 0, 0:H, :] = jnp.where(i > 0, prev_ref[...], 0.0)
        rotx[:, 0, H:H + tm, :] = cur_ref[...]
        rotx[:, 0, H + tm:, :] = jnp.where(i < nt - 1, next_ref[...], 0.0)
        dext[0:tm, :] = d_ref[...]
        dext[tm:, :] = jnp.where(i < nt - 1, dn_ref[...], 0.0)
        _fill_plane_rotations(rotx, slot_x)

        def chunk_e(row0):
            for l0 in range(0, cw, lw):
                lanes = slice(l0, l0 + lw)
                c2 = []
                for pln in range(2):
                    acc = jnp.broadcast_to(b_ref[pln, :, lanes], (rbe, lw))
                    for k in range(K):
                        a, rr = divmod(offs_x[k], SUBLANES)
                        acc = acc + w_ref[pln, k:k + 1, lanes] * rotx[pln, slot_x[rr], pl.ds(row0 + SUBLANES * a, rbe), lanes]
                    c2.append(acc)
                _, pull = jax.vjp(_silu_mul, c2[0], c2[1])
                dval, dgt = pull(dext[pl.ds(row0, rbe), lanes])
                rotd[0, 0, pl.ds(row0, rbe), lanes] = dval
                rotd[1, 0, pl.ds(row0, rbe), lanes] = dgt

        _for_chunks(te, rbe, chunk_e)
        _fill_plane_rotations(rotd, slot_d)

        @pl.when(i == 0)
        def _():
            accw[...] = jnp.zeros_like(accw)

        def chunk(row0):
            for l0 in range(0, cw, lw):
                lanes = slice(l0, l0 + lw)
                for pln in range(2):
                    dcur = rotd[pln, 0, pl.ds(row0, rb), lanes]
                    dup = jnp.zeros((rb, lw), F32)
                    for k in range(K):
                        a, rr = divmod(offs_d[k], SUBLANES)
                        dup = dup + w_ref[pln, k:k + 1, lanes] * rotd[pln, slot_d[rr], pl.ds(row0 + SUBLANES * a, rb), lanes]
                        a, rr = divmod(offs_x[k], SUBLANES)
                        prod = dcur * rotx[pln, slot_x[rr], pl.ds(row0 + SUBLANES * a, rb), lanes]
                        accw[pln, SUBLANES * k:SUBLANES * (k + 1), lanes] += jnp.sum(
                            prod.reshape(rb // SUBLANES, SUBLANES, lw), axis=0)
                    accw[pln, SUBLANES * K:SUBLANES * (K + 1), lanes] += jnp.sum(
                        dcur.reshape(rb // SUBLANES, SUBLANES, lw), axis=0)
                    dup_ref[pln, pl.ds(row0, rb), lanes] = dup.astype(BF16)

        _for_chunks(tm, rb, chunk)

        @pl.when(i == nt - 1)
        def _():
            dwb_ref[...] = jnp.zeros_like(dwb_ref)
            for pln in range(2):
                for k in range(K + 1):
                    dwb_ref[pln, k:k + 1, :] = jnp.sum(accw[pln, SUBLANES * k:SUBLANES * (k + 1), :], axis=0,
                                                       keepdims=True)

    res = _call(
        body, name, (Fh // cw, nt),
        [pl.BlockSpec((2, tm, cw), lambda j, i: (0, i, j)),
         pl.BlockSpec((2, H, cw), lambda j, i: (0, jnp.maximum(i * r - 1, 0), j)),
         pl.BlockSpec((2, H, cw), lambda j, i: (0, jnp.minimum((i + 1) * r, nhb - 1), j)),
         pl.BlockSpec((tm, cw), lambda j, i: (i, j)),
         pl.BlockSpec((H, cw), lambda j, i: (jnp.minimum((i + 1) * r, nhb - 1), j)),
         pl.BlockSpec((2, K, cw), lambda j, i: (0, 0, j)),
         pl.BlockSpec((2, 1, cw), lambda j, i: (0, 0, j))],
        [pl.BlockSpec((2, tm, cw), lambda j, i: (0, i, j)), pl.BlockSpec((2, SUBLANES, cw), lambda j, i: (0, 0, j))],
        [jax.ShapeDtypeStruct((2, S, Fh), BF16), jax.ShapeDtypeStruct((2, SUBLANES, Fh), F32)],
        [pltpu.VMEM((2, len(slot_x), tm + 2 * H, cw), F32), pltpu.VMEM((te, cw), F32),
         pltpu.VMEM((2, len(slot_d), te, cw), F32), pltpu.VMEM((2, SUBLANES * (K + 1), cw), F32)],
        ("parallel", "arbitrary"), (up, up, up, dact, dact, w, b), job=job)
    return res if job is None else (res[:2], res[2:])


def _rms(x, g):
    return x * lax.rsqrt(jnp.mean(x * x, axis=-1, keepdims=True) + EPS) * g


def _gate_bwd_tile(i, dx, out_ref, gate_ref, dout_ref, dgate_ref):
    dout_ref[...] = (dx * gate_ref[...]).astype(BF16)
    _accumulate(i, dgate_ref, jnp.sum(dx * out_ref[...], axis=0, keepdims=True))


def _final(name, x2, target, gf, out, gate):
    S, D = x2.shape
    tm = _pick(S, 256, LANES)

    def body(x_ref, t_ref, g_ref, o_ref, gate_ref, dx_ref, dg_ref, loss_ref, do_ref, dgate_ref):
        i = pl.program_id(0)
        y, pull = jax.vjp(_rms, x_ref[...], g_ref[...])
        e = y - t_ref[...]
        dx, dg = pull(e / D)
        dx_ref[...] = dx
        _accumulate(i, dg_ref, dg)
        part = 0.5 * jnp.sum(jnp.mean(jnp.square(e), axis=-1, keepdims=True), axis=0, keepdims=True)
        _accumulate(i, loss_ref, jnp.broadcast_to(part, (SUBLANES, LANES)))
        _gate_bwd_tile(i, dx, o_ref, gate_ref, do_ref, dgate_ref)

    row = _rows(tm, D)
    vec = _whole((1, D))
    vsd = jax.ShapeDtypeStruct((1, D), F32)
    return pl.pallas_call(
        body, name=name, grid=(S // tm,), in_specs=[row, row, vec, row, vec],
        out_specs=[row, vec, _whole((SUBLANES, LANES)), row, vec],
        out_shape=[jax.ShapeDtypeStruct((S, D), F32), vsd, jax.ShapeDtypeStruct((SUBLANES, LANES), F32),
                   jax.ShapeDtypeStruct((S, D), BF16), vsd],
        compiler_params=_params(("arbitrary",)),
    )(x2, target, gf, out, gate)


def _ada_fwd(name, c_pad, w_ada, b_cols):
    nb, D = c_pad.shape
    n = w_ada.shape[1]
    tn = _pick(n, 1024, LANES)

    def body(c_ref, w_ref, b_ref, o_ref):
        o_ref[...] = jnp.dot(jax.nn.silu(c_ref[...]).astype(BF16), w_ref[...].astype(BF16),
                             preferred_element_type=F32) + b_ref[...]

    return pl.pallas_call(
        body, name=name, grid=(n // tn,),
        in_specs=[_whole((nb, D)), pl.BlockSpec((D, tn), lambda j: (0, j)), pl.BlockSpec((1, tn), lambda j: (0, j))],
        out_specs=pl.BlockSpec((nb, tn), lambda j: (0, j)),
        out_shape=jax.ShapeDtypeStruct((nb, n), F32), compiler_params=_params(("parallel",)),
    )(c_pad, w_ada, b_cols)


def _ada_wgrad(name, c_t, dmod_cols):
    D, nb = c_t.shape
    n = dmod_cols.shape[1]
    tr = _pick(D, 256, SUBLANES)

    def body(c_ref, d_ref, o_ref):
        ca = jax.nn.silu(c_ref[...])
        acc = ca[:, 0:1] * d_ref[0:1, :]
        for b in range(1, nb):
            acc = acc + ca[:, b:b + 1] * d_ref[b:b + 1, :]
        o_ref[...] = acc

    return pl.pallas_call(
        body, name=name, grid=(D // tr,),
        in_specs=[pl.BlockSpec((tr, nb), lambda i: (i, 0)), _whole((nb, n))],
        out_specs=pl.BlockSpec((tr, n), lambda i: (i, 0)),
        out_shape=jax.ShapeDtypeStruct((D, n), F32), compiler_params=_params(("parallel",)),
    )(c_t, dmod_cols)


def kernel(x, c, w_ada, b_ada, norm1_g, w_in, b_in, conv_dw_w, conv_dw_b, conv_ln_g, conv_ln_b, w_conv_out, sgu_ln_g, sgu_ln_b, w_spatial, b_spatial, w_sgu_out, w_out, norm2_g, w_up, ffn_dw_w, ffn_dw_b, w_down, final_g, loss_target, m_w_ada, m_b_ada, m_norm1_g, m_w_in, m_b_in, m_conv_dw_w, m_conv_dw_b, m_conv_ln_g, m_conv_ln_b, m_w_conv_out, m_sgu_ln_g, m_sgu_ln_b, m_w_spatial, m_b_spatial, m_w_sgu_out, m_w_out, m_norm2_g, m_w_up, m_ffn_dw_w, m_ffn_dw_b, m_w_down, m_final_g, v_w_ada, v_b_ada, v_norm1_g, v_w_in, v_b_in, v_conv_dw_w, v_conv_dw_b, v_conv_ln_g, v_conv_ln_b, v_w_conv_out, v_sgu_ln_g, v_sgu_ln_b, v_w_spatial, v_b_spatial, v_w_sgu_out, v_w_out, v_norm2_g, v_w_up, v_ffn_dw_w, v_ffn_dw_b, v_w_down, v_final_g):
    S, D = x.shape[1], x.shape[2]
    dc = w_conv_out.shape[1]
    ds = w_sgu_out.shape[1]
    G, CH = w_spatial.shape[1], w_spatial.shape[2]
    KC = conv_dw_w.shape[1]
    KF = ffn_dw_w.shape[1]
    F2 = ffn_dw_b.shape[1]
    Fh = F2 // 2
    n_ada = w_ada.shape[2]
    n_up = w_up.shape[2]
    ax, ay, ac = _axes()
    chip = 2 * ax + ay
    me = 2 * chip + ac
    c_idx = jnp.reshape(ac, (1,)).astype(jnp.int32)
    p_idx = jnp.reshape(chip, (1,)).astype(jnp.int32)

    xs = x[0]
    tgt = loss_target[0]

    g1 = _allgather8(_pack([c[0], conv_dw_w[0], ffn_dw_w[0]]), "gather_small_in")
    n_cw, n_fw = conv_dw_w.shape[2], ffn_dw_w.shape[2]
    flat1 = g1.reshape(N_DEV, -1)

    def from_chips(off, rows, cols):
        shards4 = flat1[0::2, off:off + rows * cols].reshape(N_CHIPS, rows, cols)
        return jnp.transpose(shards4, (1, 0, 2)).reshape(rows, N_CHIPS * cols)

    c_all = flat1[:, :D]
    cw_full = from_chips(D, KC, n_cw)
    fw_full = from_chips(D + KC * n_cw, KF, n_fw)

    b_cols = lax.dynamic_slice(b_ada, (0, chip * n_ada), (1, n_ada))
    c_pad = jnp.concatenate([c_all, jnp.zeros_like(c_all)], axis=0)
    mod_blk = _ada_fwd("ada_fwd", c_pad, w_ada[0], b_cols)[:N_DEV]
    g2 = _allgather8(_pack([mod_blk]), "gather_mod")
    mod4 = g2.reshape(N_DEV, -1)[0::2, :N_DEV * n_ada].reshape(N_CHIPS, N_DEV, n_ada)
    mod_all = jnp.transpose(mod4, (1, 0, 2)).reshape(N_DEV, N_CHIPS * n_ada)
    mod = lax.dynamic_slice(mod_all, (me, 0), (1, 6 * D))
    shift1, scale1, gate1, shift2, scale2, gate2 = [mod[:, k * D:(k + 1) * D] for k in range(6)]

    shards = [w_in[0], w_conv_out[0], w_sgu_out[0], w_out[0], w_up[0], w_down[0]]
    names = ["in", "conv_out", "sgu_out", "out", "up", "down"]
    blk = {nm: _cast_into_block(s, p_idx, "cast_" + nm) for s, nm in zip(shards, names)}
    pc_idx = jnp.concatenate([p_idx, c_idx])

    h1 = _modnorm_fwd("modnorm1", xs, norm1_g, scale1, shift1)
    (proj,), (win_xy,) = _mm_fwd_block("proj_own", h1, blk["in"], b_in, p_idx, 0,
                                       job=_job_gather([blk["in"]], rel=(0, 1), fresh=True))
    (proj,), (win_d,) = _mm_fwd_block("proj_x", h1, win_xy, b_in, p_idx, 2, into=proj,
                                      job=_job_gather([blk["in"]], rel=(2,), fresh=True))
    (proj,), (wco_f, wso_f) = _mm_fwd_block("proj_y", h1, win_xy, b_in, p_idx, 1, into=proj,
                                            job=_job_gather([blk["conv_out"], blk["sgu_out"]]))
    (proj,), (wout_f,) = _mm_fwd_block("proj_diag", h1, win_d, b_in, p_idx, 3, into=proj,
                                       job=_job_gather([blk["out"]]))
    win_f = _copy_block(_copy_block(win_xy, blk["in"], p_idx, 0, "fill_w_in_own"), win_d, p_idx, 3, "fill_w_in_diag")
    wout_r = wout_f.reshape(-1, wout_f.shape[2])
    wco_c = _blocks_to_columns(wco_f, "columns_conv_out")
    wso_c = _blocks_to_columns(wso_f, "columns_sgu_out")
    (conv, a_act), (wup_xy,) = _conv_fwd("conv_fwd", proj, cw_full, conv_dw_b, conv_ln_g, conv_ln_b, dc,
                                         job=_job_gather([blk["up"]], rel=(0, 1)))
    bst = jnp.transpose(b_spatial[0])
    (ya, yb, uv, merged), (wup_f,) = _sgu_fwd("sgu_fwd", proj, a_act, wco_c, wso_c, sgu_ln_g, sgu_ln_b, w_spatial[0],
                                              bst, D, ds, job=_job_gather([wup_xy], rel=(2,)))
    out1, x1 = _mm_fwd_rows("out1", merged, wout_r, xs, gate1)
    h2 = _modnorm_fwd("modnorm2", x1, norm2_g, scale2, shift2)
    up, (wdown_f,) = _mm_fwd("up", h2, wup_f, planes=2, job=_job_gather([blk["down"]]))
    wdown_r = wdown_f.reshape(-1, wdown_f.shape[2])
    fw2 = jnp.stack([fw_full[:, :Fh], fw_full[:, Fh:]])
    fb2 = jnp.stack([ffn_dw_b[:, :Fh], ffn_dw_b[:, Fh:]])
    act = _ffn_fwd("ffn_fwd", up, fw2, fb2)
    out2, x2 = _mm_fwd_rows("out2", act, wdown_r, x1, gate2)
    dx2, d_final_g, loss_blk, dout2, d_gate2 = _final("final", x2, tgt, final_g.reshape(1, D), out2, gate2)
    loss = lax.psum(loss_blk[0, 0], ("x", "y", "c"))

    def add_cores(nm, g, r1):
        return _add_own_half(g, r1, c_idx, "add_cores_" + nm)

    def add_chips(nm, g, r1, r2):
        return _add_chips(g, r1, r2, pc_idx, "add_chips_" + nm)

    g_wdown = _mm_wgrad_rows("wgrad_down", act, dout2, w_down.shape[1])
    dact, (r1_down,) = _mm_dgrad_rows("dgrad_down", dout2, wdown_r, job=_job_swap_halves([g_wdown]))
    s_down = add_cores("down", g_wdown, r1_down)
    (dup, d_ffn), (r2_down,) = _ffn_bwd("ffn_bwd", up, dact, fw2, fb2, job=_job_scatter_blocks([s_down]))
    h_down = add_chips("down", g_wdown, r1_down, r2_down)
    g_wup = _mm_wgrad_cols("wgrad_up", h2, dup, n_up)
    dh2, (r1_up,) = _mm_dgrad_cols("dgrad_up", dup, wup_f, job=_job_swap_halves([g_wup]))
    s_up = add_cores("up", g_wup, r1_up)
    dx1, d_norm2, d_scale2, d_shift2, dout1, d_gate1 = _modnorm_bwd(
        "modnorm2_bwd", x1, dh2, dx2, norm2_g, scale2, shift2, gated=(out1, gate1))
    g_wout = _mm_wgrad_rows("wgrad_out", merged, dout1, w_out.shape[1])
    dmerged = _mm_dgrad_rows("dgrad_out", dout1, wout_r)
    dya, dyb, dproj, dbin_g = _merge_bwd("merge_bwd", dmerged, proj, ya, yb, D)
    g_wco = _mm_wgrad_cols("wgrad_conv_out", a_act, dya[None], w_conv_out.shape[2])
    g_wso = _mm_wgrad_cols("wgrad_sgu_out", uv, dyb[None], w_sgu_out.shape[2])
    da_act, (r1_out, r1_co, r1_so) = _mm_dgrad_rows("dgrad_conv_out", dya, wco_c,
                                                    job=_job_swap_halves([g_wout, g_wco, g_wso]))
    s_out = add_cores("out", g_wout, r1_out)
    s_co = add_cores("conv_out", g_wco, r1_co)
    s_so = add_cores("sgu_out", g_wso, r1_so)
    duv = _mm_dgrad_rows("dgrad_sgu_out", dyb, wso_c)
    dproj, d_ws, d_bs, d_sgu_g, d_sgu_b, dbin_s = _sgu_bwd("sgu_bwd", proj, duv, sgu_ln_g, sgu_ln_b, w_spatial[0], bst,
                                                           dproj, ds)
    dconv, d_cln_g, d_cln_b, d_conv_b = _conv_bwd_ln("conv_ln_bwd", conv, da_act, conv_ln_g, conv_ln_b)
    d_fw = jnp.concatenate([d_ffn[0, :KF], d_ffn[1, :KF]], axis=1)
    d_fb = jnp.concatenate([d_ffn[0, KF:KF + 1], d_ffn[1, KF:KF + 1]], axis=1)
    early = [d_sgu_g, d_sgu_b, d_ws, d_bs[:, :, 0], d_norm2, d_fb, d_final_g, d_fw]
    (dproj, d_cw, dbin_a), (r2_up, g_early) = _conv_bwd(
        "conv_bwd", dconv, proj, cw_full, dproj, dc,
        job=_merge_jobs([_job_scatter_blocks([s_up]), _job_allgather8(_pack(early))]))
    h_up = add_chips("up", g_wup, r1_up, r2_up)
    g_win, (r2_out, r2_co, r2_so, sib_up, sib_down) = _mm_wgrad_cols(
        "wgrad_in", h1, dproj[None], w_in.shape[2],
        job=_merge_jobs([_job_scatter_blocks([s_out, s_co, s_so]), _job_to_sibling([h_up, h_down])]))
    h_out = add_chips("out", g_wout, r1_out, r2_out)
    h_co = add_chips("conv_out", g_wco, r1_co, r2_co)
    h_so = add_chips("sgu_out", g_wso, r1_so, r2_so)
    dh1, (r1_in, sib_out, sib_co, sib_so) = _mm_dgrad_cols(
        "dgrad_in_a", dproj[None], win_f, row_tiles=(0.0, 0.5),
        job=_merge_jobs([_job_swap_halves([g_win]), _job_to_sibling([h_out, h_co, h_so])]))
    s_in = add_cores("in", g_win, r1_in)
    dh1, (r2_in,) = _mm_dgrad_cols("dgrad_in_b", dproj[None], win_f, row_tiles=(0.5, 0.5), fill_into=dh1,
                                   job=_job_scatter_blocks([s_in]))
    h_in = add_chips("in", g_win, r1_in, r2_in)
    dxs, d_norm1, d_scale1, d_shift1 = _modnorm_bwd("modnorm1_bwd", xs, dh1, dx1, norm1_g, scale1, shift1)

    d_mod = jnp.concatenate([d_shift1, d_scale1, d_gate1, d_shift2, d_scale2, d_gate2], axis=1)
    d_b_in = jnp.concatenate([dbin_a, dbin_s, dbin_g], axis=1)
    late = [d_mod, d_norm1, d_b_in, d_conv_b, d_cln_g, d_cln_b, d_cw[:KC]]
    g3, sib_in = _run_job(_merge_jobs([_job_allgather8(_pack(late)), _job_to_sibling([h_in])]), "gather_small_grads")
    big_halves = {"w_in": (h_in, sib_in), "w_conv_out": (h_co, sib_co), "w_sgu_out": (h_so, sib_so),
                  "w_out": (h_out, sib_out), "w_up": (h_up, sib_up), "w_down": (h_down, sib_down)}
    g_b_ada, g_norm1, g_b_in, g_conv_b, g_cln_g, g_cln_b, g_cw_full = _unpack(
        _sum8(g3, "sum_small_grads"), [a.shape for a in late])
    g_sgu_g, g_sgu_b, g_ws, g_bs, g_norm2, g_fb, g_final, g_fw_full = _unpack(
        _sum8(g_early, "sum_early_grads"), [a.shape for a in early])
    g_cw = lax.dynamic_slice(g_cw_full, (0, chip * n_cw), (KC, n_cw))
    g_fw = lax.dynamic_slice(g_fw_full, (0, chip * n_fw), (KF, n_fw))
    dmod_all = g3.reshape(N_DEV, -1)[:, :6 * D]
    dmod_cols = lax.dynamic_slice(dmod_all, (0, chip * n_ada), (N_DEV, n_ada))
    g_wada = _ada_wgrad("ada_wgrad", jnp.transpose(c_all), dmod_cols)

    grads = {
        "w_ada": g_wada[None], "b_ada": g_b_ada, "norm1_g": g_norm1, "b_in": g_b_in,
        "conv_dw_w": g_cw[None], "conv_dw_b": g_conv_b, "conv_ln_g": g_cln_g, "conv_ln_b": g_cln_b,
        "sgu_ln_g": g_sgu_g, "sgu_ln_b": g_sgu_b, "w_spatial": g_ws[None],
        "b_spatial": g_bs[None], "norm2_g": g_norm2, "ffn_dw_w": g_fw[None], "ffn_dw_b": g_fb,
        "final_g": g_final.reshape(D),
    }
    weights = dict(w_ada=w_ada, b_ada=b_ada, norm1_g=norm1_g, w_in=w_in, b_in=b_in, conv_dw_w=conv_dw_w, conv_dw_b=conv_dw_b, conv_ln_g=conv_ln_g, conv_ln_b=conv_ln_b, w_conv_out=w_conv_out, sgu_ln_g=sgu_ln_g, sgu_ln_b=sgu_ln_b, w_spatial=w_spatial, b_spatial=b_spatial, w_sgu_out=w_sgu_out, w_out=w_out, norm2_g=norm2_g, w_up=w_up, ffn_dw_w=ffn_dw_w, ffn_dw_b=ffn_dw_b, w_down=w_down, final_g=final_g)
    m_in = dict(w_ada=m_w_ada, b_ada=m_b_ada, norm1_g=m_norm1_g, w_in=m_w_in, b_in=m_b_in, conv_dw_w=m_conv_dw_w, conv_dw_b=m_conv_dw_b, conv_ln_g=m_conv_ln_g, conv_ln_b=m_conv_ln_b, w_conv_out=m_w_conv_out, sgu_ln_g=m_sgu_ln_g, sgu_ln_b=m_sgu_ln_b, w_spatial=m_w_spatial, b_spatial=m_b_spatial, w_sgu_out=m_w_sgu_out, w_out=m_w_out, norm2_g=m_norm2_g, w_up=m_w_up, ffn_dw_w=m_ffn_dw_w, ffn_dw_b=m_ffn_dw_b, w_down=m_w_down, final_g=m_final_g)
    v_in = dict(w_ada=v_w_ada, b_ada=v_b_ada, norm1_g=v_norm1_g, w_in=v_w_in, b_in=v_b_in, conv_dw_w=v_conv_dw_w, conv_dw_b=v_conv_dw_b, conv_ln_g=v_conv_ln_g, conv_ln_b=v_conv_ln_b, w_conv_out=v_w_conv_out, sgu_ln_g=v_sgu_ln_g, sgu_ln_b=v_sgu_ln_b, w_spatial=v_w_spatial, b_spatial=v_b_spatial, w_sgu_out=v_w_sgu_out, w_out=v_w_out, norm2_g=v_norm2_g, w_up=v_w_up, ffn_dw_w=v_ffn_dw_w, ffn_dw_b=v_ffn_dw_b, w_down=v_w_down, final_g=v_final_g)
    order = list(weights.keys())
    large = ["w_ada", "w_in", "w_conv_out", "w_sgu_out", "w_out", "w_up", "w_down"]
    little = [n for n in order if n not in large]
    delta, new_m, new_v = {}, {}, {}
    for n in large:
        shp = weights[n].shape
        two = (shp[1], shp[2])
        if n in big_halves:
            g_, d_, m_, v_ = _adamw_halves(weights[n].reshape(two), big_halves[n][0], big_halves[n][1],
                                           m_in[n].reshape(two), v_in[n].reshape(two), c_idx, "adamw_" + n)
            grads[n] = g_.reshape(shp)
        else:
            d_, m_, v_ = _adamw(weights[n].reshape(two), grads[n].reshape(two), m_in[n].reshape(two),
                                v_in[n].reshape(two), "adamw_" + n)
        delta[n], new_m[n], new_v[n] = d_.reshape(shp), m_.reshape(shp), v_.reshape(shp)
    shapes = [weights[n].shape for n in little]
    d_, m_, v_ = _adamw(_pack([weights[n] for n in little]), _pack([grads[n] for n in little]),
                        _pack([m_in[n] for n in little]), _pack([v_in[n] for n in little]), "adamw_small")
    for n, dd, mm, vv in zip(little, _unpack(d_, shapes), _unpack(m_, shapes), _unpack(v_, shapes)):
        delta[n], new_m[n], new_v[n] = dd, mm, vv
    grad_out = [grads[n].reshape(weights[n].shape) for n in order]
    return (loss, dxs[None], *grad_out, *[delta[n] for n in order], *[new_m[n] for n in order],
            *[new_v[n] for n in order])
```

```python
import functools

import jax
import jax.numpy as jnp
from jax import lax
from jax.experimental import pallas as pl
from jax.experimental.pallas import tpu as pltpu

F32 = jnp.float32
BF16 = jnp.bfloat16
EPS = 1e-6
MESH = pl.DeviceIdType.MESH
N_CHIPS = 4
N_DEV = 8
LANES = 128
SUBLANES = 8
CONV_HALO = 32
FFN_HALO = 8
CONV_CHUNK_ROWS = 32
VMEM_LIMIT_BYTES = 56 * 1024 * 1024
WHOLE_WEIGHT_BYTES = 8 * 1024 * 1024

ADAM_LR = 0.001
ADAM_B1 = 0.9
ADAM_B2 = 0.999
ADAM_EPS = 1e-08
ADAM_WD = 0.01
ADAM_STEP = 10

NN = (((1,), (0,)), ((), ()))
NT = (((1,), (1,)), ((), ()))
TN = (((0,), (0,)), ((), ()))


def _params(sem=None):
    return pltpu.CompilerParams(dimension_semantics=sem, vmem_limit_bytes=VMEM_LIMIT_BYTES)


def _pick(dim, pref, mult):
    best = None
    d = mult
    while d <= min(dim, pref):
        if dim % d == 0:
            best = d
        d += mult
    return dim if best is None else best


def _axes():
    return lax.axis_index("x"), lax.axis_index("y"), lax.axis_index("c")


def _modnorm(x, g, scale, shift):
    r = lax.rsqrt(jnp.mean(x * x, axis=-1, keepdims=True) + EPS)
    return (x * r * g) * (1.0 + scale) + shift


def _layer_norm(x, g, b):
    mu = jnp.mean(x, axis=-1, keepdims=True)
    var = jnp.mean(jnp.square(x - mu), axis=-1, keepdims=True)
    return (x - mu) * lax.rsqrt(var + EPS) * g + b


def _gelu(x):
    return 0.5 * x * (1.0 + lax.erf(x * (0.5 ** 0.5)))


def _ln_silu(x, g, b):
    return jax.nn.silu(_layer_norm(x, g, b))


def _tril_mask(ws):
    n = ws.shape[-1]
    row = lax.broadcasted_iota(jnp.int32, (n, n), 0)
    col = lax.broadcasted_iota(jnp.int32, (n, n), 1)
    return jnp.where(row >= col, ws, 0.0)


def _pack(arrs):
    flat = [a.reshape(-1).astype(F32) for a in arrs]
    total = sum(f.shape[0] for f in flat)
    tile = SUBLANES * LANES
    padded = -(-total // tile) * tile
    if padded > total:
        flat = flat + [jnp.zeros((padded - total,), F32)]
    return jnp.concatenate(flat).reshape(padded // LANES, LANES)


def _unpack(buf, shapes):
    flat = buf.reshape(-1)
    out, off = [], 0
    for s in shapes:
        n = 1
        for d in s:
            n *= d
        out.append(flat[off:off + n].reshape(s))
        off += n
    return out


def _allgather8(buf, name):
    R, L = buf.shape

    def body(in_ref, out_ref, send_sems, recv_sems, local_sem):
        x, y, c = _axes()
        me = 4 * x + 2 * y + c
        mine = pltpu.make_async_copy(in_ref, out_ref.at[me], local_sem)
        mine.start()
        peers = []
        for k in range(1, N_DEV):
            px = 1 - x if k & 4 else x
            py = 1 - y if k & 2 else y
            pc = 1 - c if k & 1 else c
            peers.append((px, py, pc))
        sends = []
        for k, peer in enumerate(peers):
            cp = pltpu.make_async_remote_copy(
                src_ref=in_ref, dst_ref=out_ref.at[me], send_sem=send_sems.at[k], recv_sem=recv_sems.at[k],
                device_id=peer, device_id_type=MESH)
            cp.start()
            sends.append(cp)
        for k, (px, py, pc) in enumerate(peers):
            pltpu.make_async_remote_copy(
                src_ref=in_ref, dst_ref=out_ref.at[4 * px + 2 * py + pc], send_sem=send_sems.at[k],
                recv_sem=recv_sems.at[k], device_id=(px, py, pc), device_id_type=MESH).wait_recv()
        for cp in sends:
            cp.wait_send()
        mine.wait()

    return pl.pallas_call(
        body, name=name,
        out_shape=jax.ShapeDtypeStruct((N_DEV, R, L), buf.dtype),
        in_specs=[pl.BlockSpec(memory_space=pltpu.VMEM)],
        out_specs=pl.BlockSpec(memory_space=pltpu.VMEM),
        scratch_shapes=[pltpu.SemaphoreType.DMA((N_DEV - 1,)), pltpu.SemaphoreType.DMA((N_DEV - 1,)),
                        pltpu.SemaphoreType.DMA],
        compiler_params=pltpu.CompilerParams(vmem_limit_bytes=VMEM_LIMIT_BYTES),
    )(buf)


def _other_chips(x, y):
    return [(1 - x, y), (x, 1 - y), (1 - x, 1 - y)]


def _cast_into_block(shard, p_idx, name):
    K, n = shard.shape
    tr = _pick(K, max(SUBLANES, (1 << 19) // n), 2 * SUBLANES)

    def body(p_ref, s_ref, o_ref):
        o_ref[...] = s_ref[...].astype(BF16)

    return pl.pallas_call(
        body, name=name,
        grid_spec=pltpu.PrefetchScalarGridSpec(
            num_scalar_prefetch=1, grid=(K // tr,),
            in_specs=[pl.BlockSpec((tr, n), lambda i, p: (i, 0))],
            out_specs=pl.BlockSpec((None, tr, n), lambda i, p: (p[0], i, 0))),
        out_shape=jax.ShapeDtypeStruct((N_CHIPS, K, n), BF16),
        compiler_params=_params(("parallel",)),
    )(p_idx, shard)


class _Job:
    def __init__(self, ins, outs, aliases, n_sems, make):
        self.ins, self.outs, self.aliases, self.n_sems, self.make = list(ins), list(outs), list(aliases), n_sems, make


def _merge_jobs(jobs):
    ins, outs, aliases, offs = [], [], [], []
    n_sems = 0
    for jb in jobs:
        offs.append((len(ins), len(outs), n_sems))
        aliases += [(len(ins) + a, len(outs) + b) for a, b in jb.aliases]
        ins += jb.ins
        outs += jb.outs
        n_sems += jb.n_sems

    def make(in_refs, out_refs, send_sems, recv_sems, base=0):
        made = []
        for jb, (oi, oo, os_) in zip(jobs, offs):
            made.append(jb.make(in_refs[oi:oi + len(jb.ins)], out_refs[oo:oo + len(jb.outs)],
                                send_sems, recv_sems, base + os_))

        def start():
            for st, _ in made:
                st()

        def finish():
            for _, fin in made:
                fin()

        return start, finish

    return _Job(ins, outs, aliases, n_sems, make)


def _job_gather(fulls, rel=(0, 1, 2), fresh=False):
    nw = len(fulls)

    def make(in_refs, outs, send_sems, recv_sems, base=0):
        x, y, c = _axes()
        p = 2 * x + y
        chips = _other_chips(x, y)
        srcs = in_refs if fresh else outs

        def rows(w, mine):
            kh = outs[w].shape[1] // 2
            return pl.ds((c if mine else 1 - c) * kh, kh)

        def over_ici(w, j, block):
            qx, qy = chips[j]
            return pltpu.make_async_remote_copy(
                src_ref=srcs[w].at[block, rows(w, True)], dst_ref=outs[w].at[block, rows(w, True)],
                send_sem=send_sems.at[base + 6 * w + j], recv_sem=recv_sems.at[base + 6 * w + j],
                device_id=(qx, qy, c), device_id_type=MESH)

        def over_d2d(w, j, mine):
            qx, qy = chips[j]
            return pltpu.make_async_remote_copy(
                src_ref=outs[w].at[2 * qx + qy, rows(w, mine)], dst_ref=outs[w].at[2 * qx + qy, rows(w, mine)],
                send_sem=send_sems.at[base + 6 * w + 3 + j], recv_sem=recv_sems.at[base + 6 * w + 3 + j],
                device_id=(x, y, 1 - c), device_id_type=MESH)

        def start():
            for w in range(nw):
                for j in rel:
                    over_ici(w, j, p).start()

        def finish():
            for w in range(nw):
                for j in rel:
                    qx, qy = chips[j]
                    over_ici(w, j, 2 * qx + qy).wait_recv()
                    over_d2d(w, j, True).start()
            for w in range(nw):
                for j in rel:
                    over_d2d(w, j, False).wait_recv()
            for w in range(nw):
                for j in rel:
                    over_ici(w, j, p).wait_send()
                    over_d2d(w, j, True).wait_send()

        return start, finish

    return _Job(fulls, [jax.ShapeDtypeStruct(f.shape, f.dtype) for f in fulls],
                [] if fresh else [(w, w) for w in range(nw)], 6 * nw, make)


def _job_simple(ins, outs, n_per, copies_of):
    nw = len(ins)

    def make(in_refs, out_refs, send_sems, recv_sems, base=0):
        x, y, c = _axes()
        copies = []
        for w in range(nw):
            for j, (src, dst, dev) in enumerate(copies_of(w, in_refs[w], out_refs[w], x, y, c)):
                k = base + n_per * w + j
                copies.append(pltpu.make_async_remote_copy(
                    src_ref=src, dst_ref=dst, send_sem=send_sems.at[k], recv_sem=recv_sems.at[k],
                    device_id=dev, device_id_type=MESH))

        def start():
            for cp in copies:
                cp.start()

        def finish():
            for cp in copies:
                cp.wait()

        return start, finish

    return _Job(ins, outs, [], n_per * nw, make)


def _job_swap_halves(grads):
    def copies_of(w, src, dst, x, y, c):
        kh = src.shape[1] // 2
        return [(src.at[:, pl.ds((1 - c) * kh, kh), :], dst, (x, y, 1 - c))]

    outs = [jax.ShapeDtypeStruct((g.shape[0], g.shape[1] // 2, g.shape[2]), g.dtype) for g in grads]
    return _job_simple(grads, outs, 1, copies_of)


def _job_scatter_blocks(sums):
    def copies_of(w, src, dst, x, y, c):
        return [(src.at[2 * qx + qy], dst.at[j], (qx, qy, c)) for j, (qx, qy) in enumerate(_other_chips(x, y))]

    outs = [jax.ShapeDtypeStruct((3,) + s.shape[1:], s.dtype) for s in sums]
    return _job_simple(sums, outs, 3, copies_of)


def _job_to_sibling(arrs):
    def copies_of(w, src, dst, x, y, c):
        return [(src, dst, (x, y, 1 - c))]

    outs = [jax.ShapeDtypeStruct(a.shape, a.dtype) for a in arrs]
    return _job_simple(arrs, outs, 1, copies_of)


def _job_allgather8(buf):
    def make(in_refs, out_refs, send_sems, recv_sems, base=0):
        x, y, c = _axes()
        me = 4 * x + 2 * y + c
        src, dst = in_refs[0], out_refs[0]
        peers = [(1 - x if k & 4 else x, 1 - y if k & 2 else y, 1 - c if k & 1 else c) for k in range(1, N_DEV)]
        mine = pltpu.make_async_copy(src, dst.at[me], send_sems.at[base + N_DEV - 1])

        def to(k):
            return pltpu.make_async_remote_copy(
                src_ref=src, dst_ref=dst.at[me], send_sem=send_sems.at[base + k], recv_sem=recv_sems.at[base + k],
                device_id=peers[k], device_id_type=MESH)

        def of(k):
            px, py, pc = peers[k]
            return pltpu.make_async_remote_copy(
                src_ref=src, dst_ref=dst.at[4 * px + 2 * py + pc], send_sem=send_sems.at[base + k],
                recv_sem=recv_sems.at[base + k], device_id=peers[k], device_id_type=MESH)

        def start():
            mine.start()
            for k in range(N_DEV - 1):
                to(k).start()

        def finish():
            for k in range(N_DEV - 1):
                of(k).wait_recv()
                to(k).wait_send()
            mine.wait()

        return start, finish

    return _Job([buf], [jax.ShapeDtypeStruct((N_DEV,) + buf.shape, buf.dtype)], [], N_DEV, make)


def _run_job(job, name):
    ni, no = len(job.ins), len(job.outs)

    def body(*refs):
        start, finish = job.make(refs[:ni], refs[ni:ni + no], refs[-2], refs[-1])
        start()
        finish()

    any_spec = pl.BlockSpec(memory_space=pl.ANY)
    return pl.pallas_call(
        body, name=name, out_shape=job.outs, in_specs=[any_spec] * ni, out_specs=[any_spec] * no,
        input_output_aliases=dict(job.aliases),
        scratch_shapes=[pltpu.SemaphoreType.DMA((job.n_sems,)), pltpu.SemaphoreType.DMA((job.n_sems,))],
    )(*job.ins)


def _call(body, name, grid, in_specs, out_specs, out_shape, scratch_shapes, semantics, args, aliases=None, job=None,
          prefetch=()):
    in_specs, out_specs, out_shape = list(in_specs), list(out_specs), list(out_shape)
    scratch_shapes = list(scratch_shapes)
    n_pre, n_in, n_out, n_scr = len(prefetch), len(args), len(out_shape), len(scratch_shapes)
    all_aliases = {n_pre + a: b for a, b in (aliases or {}).items()}
    if job is None:
        wrapped, extra_in, semantics = body, [], semantics
    else:
        ni, no = len(job.ins), len(job.outs)

        def wrapped(*refs):
            pre, refs = refs[:n_pre], refs[n_pre:]
            ins, cins = refs[:n_in], refs[n_in:n_in + ni]
            outs, couts = refs[n_in + ni:n_in + ni + n_out], refs[n_in + ni + n_out:n_in + ni + n_out + no]
            scr = refs[n_in + ni + n_out + no:n_in + ni + n_out + no + n_scr]
            start, finish = job.make(cins, couts, refs[-2], refs[-1])
            first = functools.reduce(jnp.logical_and, [pl.program_id(a) == 0 for a in range(len(grid))])
            last = functools.reduce(jnp.logical_and, [pl.program_id(a) == grid[a] - 1 for a in range(len(grid))])
            pl.when(first)(start)
            body(*pre, *ins, *outs, *scr)
            pl.when(last)(finish)

        any_spec = pl.BlockSpec(memory_space=pl.ANY)
        for a, b in job.aliases:
            all_aliases[n_pre + n_in + a] = n_out + b
        in_specs, out_specs, out_shape = in_specs + [any_spec] * ni, out_specs + [any_spec] * no, out_shape + job.outs
        scratch_shapes = scratch_shapes + [pltpu.SemaphoreType.DMA((job.n_sems,)), pltpu.SemaphoreType.DMA((job.n_sems,))]
        extra_in, semantics = job.ins, tuple("arbitrary" for _ in grid)
    if n_pre:
        return pl.pallas_call(
            wrapped, name=name,
            grid_spec=pltpu.PrefetchScalarGridSpec(num_scalar_prefetch=n_pre, grid=grid, in_specs=in_specs,
                                                   out_specs=out_specs, scratch_shapes=scratch_shapes),
            out_shape=out_shape, input_output_aliases=all_aliases,
            compiler_params=_params(semantics))(*prefetch, *args, *extra_in)
    return pl.pallas_call(
        wrapped, name=name, grid=grid, in_specs=in_specs, out_specs=out_specs, out_shape=out_shape,
        scratch_shapes=scratch_shapes, input_output_aliases=all_aliases,
        compiler_params=_params(semantics))(*args, *extra_in)


def _add_own_half(g, r, c_idx, name):
    nb, K, n = g.shape
    kh = K // 2
    tr = _pick(kh, max(SUBLANES, (1 << 19) // n), 2 * SUBLANES)
    per = kh // tr

    def body(c_ref, g_ref, r_ref, o_ref):
        o_ref[...] = (g_ref[...] + r_ref[...]).astype(BF16)

    return pl.pallas_call(
        body, name=name,
        grid_spec=pltpu.PrefetchScalarGridSpec(
            num_scalar_prefetch=1, grid=(nb, per),
            in_specs=[pl.BlockSpec((None, tr, n), lambda b, i, c: (b, c[0] * per + i, 0)),
                      pl.BlockSpec((None, tr, n), lambda b, i, c: (b, i, 0))],
            out_specs=pl.BlockSpec((None, tr, n), lambda b, i, c: (b, i, 0))),
        out_shape=jax.ShapeDtypeStruct((nb, kh, n), BF16),
        compiler_params=_params(("parallel", "parallel")),
    )(c_idx, g, r)


def _add_chips(g, r1, r2, pc_idx, name):
    _, K, n = g.shape
    kh = K // 2
    tr = _pick(kh, max(SUBLANES, (1 << 19) // n), 2 * SUBLANES)
    per = kh // tr

    def body(pc_ref, g_ref, r1_ref, r2_ref, o_ref):
        own = g_ref[...] + r1_ref[...]
        o_ref[...] = ((own + r2_ref[0].astype(F32)) + r2_ref[1].astype(F32)) + r2_ref[2].astype(F32)

    return pl.pallas_call(
        body, name=name,
        grid_spec=pltpu.PrefetchScalarGridSpec(
            num_scalar_prefetch=1, grid=(per,),
            in_specs=[pl.BlockSpec((None, tr, n), lambda i, pc: (pc[0], pc[1] * per + i, 0)),
                      pl.BlockSpec((None, tr, n), lambda i, pc: (pc[0], i, 0)),
                      pl.BlockSpec((3, tr, n), lambda i, pc: (0, i, 0))],
            out_specs=pl.BlockSpec((tr, n), lambda i, pc: (i, 0))),
        out_shape=jax.ShapeDtypeStruct((kh, n), F32),
        compiler_params=_params(("parallel",)),
    )(pc_idx, g, r1, r2)


def _sum8(g, name):
    _, R, L = g.shape

    def body(g_ref, o_ref):
        acc = g_ref[0]
        for d in range(1, N_DEV):
            acc = acc + g_ref[d]
        o_ref[...] = acc

    return pl.pallas_call(
        body, name=name, out_shape=jax.ShapeDtypeStruct((R, L), F32),
        in_specs=[pl.BlockSpec(memory_space=pltpu.VMEM)], out_specs=pl.BlockSpec(memory_space=pltpu.VMEM),
        compiler_params=_params(),
    )(g)


def _adamw_math(w, gg, m, v):
    nm = ADAM_B1 * m + (1.0 - ADAM_B1) * gg
    nv = ADAM_B2 * v + (1.0 - ADAM_B2) * jnp.square(gg)
    m_hat = nm / (1.0 - ADAM_B1 ** ADAM_STEP)
    v_hat = nv / (1.0 - ADAM_B2 ** ADAM_STEP)
    return -ADAM_LR * (m_hat / (jnp.sqrt(v_hat) + ADAM_EPS) + ADAM_WD * w), nm, nv


def _adamw(w, g, m, v, name):
    R, C = w.shape
    tr = _pick(R, max(SUBLANES, (1 << 18) // C), SUBLANES)

    def body(w_ref, g_ref, m_ref, v_ref, d_ref, nm_ref, nv_ref):
        d_ref[...], nm_ref[...], nv_ref[...] = _adamw_math(w_ref[...], g_ref[...], m_ref[...], v_ref[...])

    spec = pl.BlockSpec((tr, C), lambda i: (i, 0))
    sd = jax.ShapeDtypeStruct((R, C), F32)
    return pl.pallas_call(
        body, name=name, grid=(R // tr,), in_specs=[spec] * 4, out_specs=[spec] * 3, out_shape=[sd] * 3,
        compiler_params=_params(("parallel",)),
    )(w, g, m, v)


def _adamw_halves(w, g_own, g_sib, m, v, c_idx, name):
    K, n = w.shape
    kh = K // 2
    tr = _pick(kh, max(SUBLANES, (1 << 18) // n), SUBLANES)
    per = kh // tr

    def body(c_ref, w_ref, go_ref, gs_ref, m_ref, v_ref, g_ref, d_ref, nm_ref, nv_ref):
        h = pl.program_id(0)

        def step(gg):
            g_ref[...] = gg
            d_ref[...], nm_ref[...], nv_ref[...] = _adamw_math(w_ref[...], gg, m_ref[...], v_ref[...])

        @pl.when(h == 0)
        def _():
            step(go_ref[...])

        @pl.when(h == 1)
        def _():
            step(gs_ref[...])

    full = pl.BlockSpec((tr, n), lambda h, i, c: (((c[0] + h) % 2) * per + i, 0))
    own = pl.BlockSpec((tr, n), lambda h, i, c: (i * (1 - h), 0))
    sib = pl.BlockSpec((tr, n), lambda h, i, c: (i * h, 0))
    sd = jax.ShapeDtypeStruct((K, n), F32)
    return pl.pallas_call(
        body, name=name,
        grid_spec=pltpu.PrefetchScalarGridSpec(
            num_scalar_prefetch=1, grid=(2, per),
            in_specs=[full, own, sib, full, full], out_specs=[full] * 4),
        out_shape=[sd] * 4,
        compiler_params=_params(("arbitrary", "arbitrary")),
    )(c_idx, w, g_own, g_sib, m, v)


def _matmul(name, grid, dims, a, a_spec, b, b_spec, outs, out_specs, acc_shape,
            extras=(), extra_specs=(), epilogue=None, job=None, fill_into=None, prefetch=()):
    nk = grid[2]
    npre = len(prefetch)
    aliases = None
    if fill_into is not None:
        aliases = {2 + len(extras): 0}
        extras = tuple(extras) + (fill_into,)
        extra_specs = tuple(extra_specs) + (pl.BlockSpec(memory_space=pl.ANY),)
    nex = len(extras)
    nout = len(outs)

    def body(*refs):
        a_ref, b_ref, rest = refs[npre], refs[npre + 1], refs[npre + 2:]
        ex, o = rest[:nex], rest[nex:nex + nout]
        part = lax.dot_general(a_ref[...].astype(BF16), b_ref[...].astype(BF16), dims,
                               preferred_element_type=F32)

        def finish(res):
            if epilogue is None:
                o[0][...] = res.astype(o[0].dtype)
            else:
                epilogue(res, ex, o)

        if nk == 1:
            finish(part)
        else:
            acc = o[0] if in_place else rest[-1]
            k = pl.program_id(2)

            @pl.when(k == 0)
            def _():
                acc[...] = part

            @pl.when(k > 0)
            def _():
                acc[...] += part

            if not in_place:
                @pl.when(k == nk - 1)
                def _():
                    finish(acc[...])

    in_place = epilogue is None and nout == 1 and outs[0].dtype == F32
    res = _call(body, name, grid, [a_spec, b_spec] + list(extra_specs), out_specs, outs,
                [] if nk == 1 or in_place else [pltpu.VMEM(acc_shape, F32)], ("parallel", "parallel", "arbitrary"),
                (a, b, *extras), aliases=aliases, job=job, prefetch=prefetch)
    return res if job is None else (res[:nout], res[nout:])


def _first(res, job):
    return res[0] if job is None else (res[0][0], res[1])


def _mm_fwd(name, a, wfull, out_dtype=F32, planes=1, bias=None, job=None):
    S, K = a.shape
    _, _, n = wfull.shape
    N = N_CHIPS * n
    tm = _pick(S, 1024, LANES)
    tn = _pick(n, 1408, LANES)
    per = n // tn
    nj = N // tn
    pj = nj // planes
    grid = (S // tm, nj, 1)
    a_spec = pl.BlockSpec((tm, K), lambda i, j, k: (i, 0))
    b_spec = pl.BlockSpec((None, K, tn), lambda i, j, k: (j // per, 0, j % per))
    o_spec = pl.BlockSpec((None, tm, tn), lambda i, j, k: (j // pj, i, j % pj))
    sd = jax.ShapeDtypeStruct((planes, S, N // planes), out_dtype)
    if bias is not None:
        def epi(res, ex, o):
            o[0][...] = (res + ex[0][...]).astype(o[0].dtype)

        out = _matmul(name, grid, NN, a, a_spec, wfull, b_spec, [sd], [o_spec], (tm, tn), extras=(bias,),
                      extra_specs=(pl.BlockSpec((1, tn), lambda i, j, k: (0, j)),), epilogue=epi, job=job)
    else:
        out = _matmul(name, grid, NN, a, a_spec, wfull, b_spec, [sd], [o_spec], (tm, tn), job=job)
    return _first(out, job)


def _mm_fwd_block(name, a, wfull, bias, p_idx, mask, into=None, job=None):
    S, K = a.shape
    _, _, n = wfull.shape
    N = N_CHIPS * n
    tm = _pick(S, 1024, LANES)
    tn = _pick(n, 1408, LANES)
    per = n // tn
    grid = (S // tm, per, 1)
    a_spec = pl.BlockSpec((tm, K), lambda i, j, k, p: (i, 0))
    b_spec = pl.BlockSpec((None, K, tn), lambda i, j, k, p: (jnp.bitwise_xor(p[0], mask), 0, j))
    o_spec = pl.BlockSpec((tm, tn), lambda i, j, k, p: (i, jnp.bitwise_xor(p[0], mask) * per + j))
    v_spec = pl.BlockSpec((1, tn), lambda i, j, k, p: (0, jnp.bitwise_xor(p[0], mask) * per + j))

    def epi(res, ex, o):
        o[0][...] = res + ex[0][...]

    return _matmul(name, grid, NN, a, a_spec, wfull, b_spec, [jax.ShapeDtypeStruct((S, N), F32)], [o_spec], (tm, tn),
                   extras=(bias,), extra_specs=(v_spec,), epilogue=epi, job=job, fill_into=into, prefetch=(p_idx,))


def _copy_block(dst, src, p_idx, mask, name):
    _, K, n = dst.shape
    tr = _pick(K, max(SUBLANES, (1 << 19) // n), 2 * SUBLANES)

    def body(p_ref, s_ref, d_any, o_ref):
        o_ref[...] = s_ref[...]

    spec = pl.BlockSpec((None, tr, n), lambda i, p: (jnp.bitwise_xor(p[0], mask), i, 0))
    return _call(body, name, (K // tr,), [spec, pl.BlockSpec(memory_space=pl.ANY)], [spec],
                 [jax.ShapeDtypeStruct(dst.shape, dst.dtype)], [], ("parallel",), (src, dst), aliases={1: 0},
                 prefetch=(p_idx,))[0]


def _mm_fwd_rows(name, a, wrows, resid, gate, job=None):
    S, K = a.shape
    _, N = wrows.shape
    whole = K * N * 2 <= WHOLE_WEIGHT_BYTES
    tm = _pick(S, 512 if whole else 1024, LANES)
    tk = K if whole else _pick(K, 2816, LANES)
    tn = N if whole else _pick(N, 1024, LANES)
    grid = (S // tm, N // tn, K // tk)
    a_spec = pl.BlockSpec((tm, tk), lambda i, j, k: (i, k))
    if tn == N and tk == K:
        b_spec = pl.BlockSpec((tk, tn), lambda i, j, k: (k, j), pipeline_mode=pl.Buffered(1))
    else:
        b_spec = pl.BlockSpec((tk, tn), lambda i, j, k: (k, j))
    o_spec = pl.BlockSpec((tm, tn), lambda i, j, k: (i, j))
    g_spec = pl.BlockSpec((1, tn), lambda i, j, k: (0, j))

    def epi(res, ex, o):
        o[0][...] = res.astype(BF16)
        o[1][...] = ex[0][...] + ex[1][...] * res

    return _matmul(name, grid, NN, a, a_spec, wrows, b_spec,
                   [jax.ShapeDtypeStruct((S, N), BF16), jax.ShapeDtypeStruct((S, N), F32)], [o_spec, o_spec], (tm, tn),
                   extras=(resid, gate), extra_specs=(o_spec, g_spec), epilogue=epi, job=job)


def _mm_dgrad_cols(name, dy, wfull, out_dtype=F32, job=None, row_tiles=None, fill_into=None):
    planes, S, npl = dy.shape
    _, K, n = wfull.shape
    tm = _pick(S if row_tiles is None else S // 2, 1024, LANES)
    to = _pick(K, 1024, LANES)
    tk = _pick(n, 2816, LANES)
    per = n // tk
    nk = N_CHIPS * per
    pk = nk // planes
    n_tiles = S // tm
    i0, ni = (0, n_tiles) if row_tiles is None else (int(row_tiles[0] * n_tiles), int(row_tiles[1] * n_tiles))
    grid = (ni, K // to, nk)
    a_spec = pl.BlockSpec((None, tm, tk), lambda i, j, k: (k // pk, i + i0, k % pk))
    b_spec = pl.BlockSpec((None, to, tk), lambda i, j, k: (k // per, j, k % per))
    o_spec = pl.BlockSpec((tm, to), lambda i, j, k: (i + i0, j))
    return _first(_matmul(name, grid, NT, dy, a_spec, wfull, b_spec, [jax.ShapeDtypeStruct((S, K), out_dtype)],
                          [o_spec], (tm, to), job=job, fill_into=fill_into), job)


def _mm_dgrad_rows(name, dy, wrows, out_dtype=F32, job=None):
    S, N = dy.shape
    K, _ = wrows.shape
    tm = _pick(S, 1024, LANES)
    to = _pick(K, 1408, LANES)
    tk = _pick(N, 2048, LANES)
    grid = (S // tm, K // to, N // tk)
    a_spec = pl.BlockSpec((tm, tk), lambda i, j, k: (i, k))
    b_spec = pl.BlockSpec((to, tk), lambda i, j, k: (j, k))
    o_spec = pl.BlockSpec((tm, to), lambda i, j, k: (i, j))
    return _first(_matmul(name, grid, NT, dy, a_spec, wrows, b_spec, [jax.ShapeDtypeStruct((S, K), out_dtype)],
                          [o_spec], (tm, to), job=job), job)


def _mm_wgrad_cols(name, a, dy, n, job=None):
    S, K = a.shape
    planes, _, npl = dy.shape
    N = planes * npl
    to = _pick(K, 1024, LANES)
    tn = _pick(n, 1408, LANES)
    ts = _pick(S, 2048, LANES)
    per = n // tn
    nj = N // tn
    pj = nj // planes
    grid = (K // to, nj, S // ts)
    a_spec = pl.BlockSpec((ts, to), lambda i, j, k: (k, i))
    b_spec = pl.BlockSpec((None, ts, tn), lambda i, j, k: (j // pj, k, j % pj))
    o_spec = pl.BlockSpec((None, to, tn), lambda i, j, k: (j // per, i, j % per))
    return _first(_matmul(name, grid, TN, a, a_spec, dy, b_spec, [jax.ShapeDtypeStruct((N_CHIPS, K, n), F32)],
                          [o_spec], (to, tn), job=job), job)


def _mm_wgrad_rows(name, a, dy, kshard, job=None):
    S, K = a.shape
    _, N = dy.shape
    to = _pick(kshard, 1408, LANES)
    tn = _pick(N, 1024, LANES)
    ts = _pick(S, 2048, LANES)
    per = kshard // to
    grid = (K // to, N // tn, S // ts)
    a_spec = pl.BlockSpec((ts, to), lambda i, j, k: (k, i))
    b_spec = pl.BlockSpec((ts, tn), lambda i, j, k: (k, j))
    o_spec = pl.BlockSpec((None, to, tn), lambda i, j, k: (i // per, i % per, j))
    return _first(_matmul(name, grid, TN, a, a_spec, dy, b_spec, [jax.ShapeDtypeStruct((N_CHIPS, kshard, N), F32)],
                          [o_spec], (to, tn), job=job), job)


def _rows(tm, width, colblk=0):
    return pl.BlockSpec((tm, width), lambda i: (i, colblk))


def _whole(shape):
    nd = len(shape)
    return pl.BlockSpec(shape, lambda i: (0,) * nd)


def _prev_halo(tm, h, width, colblk=0):
    r = tm // h
    return pl.BlockSpec((h, width), lambda i: (jnp.maximum(i * r - 1, 0), colblk))


def _next_halo(tm, h, width, nblk, colblk=0):
    r = tm // h
    return pl.BlockSpec((h, width), lambda i: (jnp.minimum((i + 1) * r, nblk - 1), colblk))


def _accumulate(i, ref, val):
    @pl.when(i == 0)
    def _():
        ref[...] = val

    @pl.when(i > 0)
    def _():
        ref[...] += val


def _fill_rotations(rot, offs):
    n = rot.shape[1]
    for r in sorted({o % SUBLANES for o in offs} - {0}):
        rot[r, 0:n - SUBLANES, :] = rot[0, r:r + n - SUBLANES, :]


def _tap_windows(rot, offs, row0, rb, lanes):
    by_res = {}
    for k, o in enumerate(offs):
        by_res.setdefault(o % SUBLANES, []).append((o // SUBLANES, k))
    for r, taps in by_res.items():
        lo = min(a for a, _ in taps)
        hi = max(a for a, _ in taps)
        win = rot[r, pl.ds(row0 + SUBLANES * lo, rb + SUBLANES * (hi - lo)), lanes]
        for a, k in taps:
            yield k, win[SUBLANES * (a - lo):SUBLANES * (a - lo) + rb, :]


def _for_chunks(n_rows, rb, fn):
    def step(j, carry):
        fn(pl.multiple_of(j * rb, rb))
        return carry

    lax.fori_loop(0, n_rows // rb, step, 0)


def _modnorm_fwd(name, x, g, scale, shift):
    S, D = x.shape
    tm = _pick(S, 512, LANES)

    def body(x_ref, g_ref, sc_ref, sh_ref, h_ref):
        h_ref[...] = _modnorm(x_ref[...], g_ref[...], sc_ref[...], sh_ref[...]).astype(BF16)

    vec = _whole((1, D))
    return pl.pallas_call(
        body, name=name, grid=(S // tm,), in_specs=[_rows(tm, D), vec, vec, vec], out_specs=_rows(tm, D),
        out_shape=jax.ShapeDtypeStruct((S, D), BF16), compiler_params=_params(("parallel",)),
    )(x, g, scale, shift)


def _modnorm_bwd(name, x, dh, dx_in, g, scale, shift, gated=None):
    S, D = x.shape
    tm = _pick(S, 256, LANES)

    def body(x_ref, dh_ref, dxin_ref, g_ref, sc_ref, sh_ref, *rest):
        i = pl.program_id(0)
        dx_ref, dg_ref, dsc_ref, dsh_ref = rest[-4:] if gated is None else rest[2:6]
        _, pull = jax.vjp(_modnorm, x_ref[...], g_ref[...], sc_ref[...], sh_ref[...])
        dx, dg, dsc, dsh = pull(dh_ref[...])
        dx = dxin_ref[...] + dx
        dx_ref[...] = dx
        _accumulate(i, dg_ref, dg)
        _accumulate(i, dsc_ref, dsc)
        _accumulate(i, dsh_ref, dsh)
        if gated is not None:
            _gate_bwd_tile(i, dx, rest[0], rest[1], rest[6], rest[7])

    vec = _whole((1, D))
    row = _rows(tm, D)
    vsd = jax.ShapeDtypeStruct((1, D), F32)
    in_specs, args = [row, row, row, vec, vec, vec], (x, dh, dx_in, g, scale, shift)
    out_specs, out_shape = [row, vec, vec, vec], [jax.ShapeDtypeStruct((S, D), F32), vsd, vsd, vsd]
    if gated is not None:
        in_specs, args = in_specs + [row, vec], args + tuple(gated)
        out_specs, out_shape = out_specs + [row, vec], out_shape + [jax.ShapeDtypeStruct((S, D), BF16), vsd]
    return _call(body, name, (S // tm,), in_specs, out_specs, out_shape, [], ("arbitrary",), args)


def _conv_fwd(name, proj, w, b, lg, lb, dc, job=None):
    S = proj.shape[0]
    K = w.shape[0]
    H = CONV_HALO
    tm = _pick(S, 256, LANES)

    def glu(v):
        return v[:, :dc] * jax.nn.sigmoid(v[:, dc:])

    offs = [H - (K - 1) + k for k in range(K)]
    rb = _pick(tm, CONV_CHUNK_ROWS, SUBLANES)
    lw = min(LANES, dc)

    def body(cur_ref, prev_ref, w_ref, b_ref, lg_ref, lb_ref, conv_ref, act_ref, rot):
        i = pl.program_id(0)
        rot[0, 0:H, :] = jnp.where(i > 0, glu(prev_ref[...]), 0.0)
        rot[0, H:, :] = glu(cur_ref[...])
        _fill_rotations(rot, offs)

        def chunk(row0):
            for l0 in range(0, dc, lw):
                lanes = slice(l0, l0 + lw)
                acc = jnp.broadcast_to(b_ref[:, lanes], (rb, lw))
                for k, win in _tap_windows(rot, offs, row0, rb, lanes):
                    acc = acc + w_ref[k:k + 1, lanes] * win
                conv_ref[pl.ds(row0, rb), lanes] = acc

        _for_chunks(tm, rb, chunk)
        act_ref[...] = _ln_silu(conv_ref[...], lg_ref[...], lb_ref[...]).astype(BF16)

    vec = _whole((1, dc))
    res = _call(body, name, (S // tm,),
                [_rows(tm, 2 * dc), _prev_halo(tm, H, 2 * dc), _whole(w.shape), vec, vec, vec],
                [_rows(tm, dc), _rows(tm, dc)],
                [jax.ShapeDtypeStruct((S, dc), F32), jax.ShapeDtypeStruct((S, dc), BF16)],
                [pltpu.VMEM((SUBLANES, tm + H, dc), F32)], ("parallel",), (proj, proj, w, b, lg, lb), job=job)
    return res if job is None else (res[:2], res[2:])


def _conv_bwd_ln(name, conv, dact, lg, lb):
    S, dc = conv.shape
    tm = _pick(S, 256, LANES)

    def body(c_ref, d_ref, lg_ref, lb_ref, dc_ref, dlg_ref, dlb_ref, db_ref):
        i = pl.program_id(0)
        _, pull = jax.vjp(_ln_silu, c_ref[...], lg_ref[...], lb_ref[...])
        dcv, dlg, dlb = pull(d_ref[...])
        dc_ref[...] = dcv
        _accumulate(i, dlg_ref, dlg)
        _accumulate(i, dlb_ref, dlb)
        _accumulate(i, db_ref, jnp.sum(dcv, axis=0, keepdims=True))

    vec = _whole((1, dc))
    row = _rows(tm, dc)
    vsd = jax.ShapeDtypeStruct((1, dc), F32)
    return pl.pallas_call(
        body, name=name, grid=(S // tm,), in_specs=[row, row, vec, vec], out_specs=[row, vec, vec, vec],
        out_shape=[jax.ShapeDtypeStruct((S, dc), F32), vsd, vsd, vsd],
        compiler_params=_params(("arbitrary",)),
    )(conv, dact, lg, lb)


def _conv_bwd(name, dconv, proj, w, dproj, dc, job=None):
    S = proj.shape[0]
    K = w.shape[0]
    H = CONV_HALO
    tm = _pick(S, 256, LANES)
    nt = S // tm

    offs_g = [H - (K - 1) + k for k in range(K)]
    offs_d = [K - 1 - k for k in range(K)]
    rb = _pick(tm, CONV_CHUNK_ROWS, SUBLANES)
    lw = min(LANES, dc)
    kp = -(-K // SUBLANES) * SUBLANES

    def body(d_ref, dn_ref, cur_ref, prev_ref, w_ref, dproj_any, da_ref, dw_ref, dbin_ref, rotg, rotd, accw, dglu_s):
        i = pl.program_id(0)
        pv = prev_ref[...]
        cv = cur_ref[...]
        sig = jax.nn.sigmoid(cv[:, dc:])
        rotg[0, 0:H, :] = jnp.where(i > 0, pv[:, :dc] * jax.nn.sigmoid(pv[:, dc:]), 0.0)
        rotg[0, H:, :] = cv[:, :dc] * sig
        rotd[0, 0:tm, :] = d_ref[...]
        rotd[0, tm:, :] = jnp.where(i < nt - 1, dn_ref[...], 0.0)
        _fill_rotations(rotg, offs_g)
        _fill_rotations(rotd, offs_d)

        @pl.when(i == 0)
        def _():
            accw[...] = jnp.zeros_like(accw)

        for l0 in range(0, dc, lw):
            lanes = slice(l0, l0 + lw)

            def chunk(j, sums, lanes=lanes):
                row0 = pl.multiple_of(j * rb, rb)
                dcur = rotd[0, pl.ds(row0, rb), lanes]
                acc = jnp.zeros((rb, lw), F32)
                for k, win in _tap_windows(rotd, offs_d, row0, rb, lanes):
                    acc = acc + w_ref[k:k + 1, lanes] * win
                new = list(sums)
                for k, win in _tap_windows(rotg, offs_g, row0, rb, lanes):
                    new[k] = sums[k] + jnp.sum((dcur * win).reshape(rb // SUBLANES, SUBLANES, lw), axis=0)
                dglu_s[pl.ds(row0, rb), lanes] = acc
                return tuple(new)

            sums = lax.fori_loop(0, tm // rb, chunk, tuple(jnp.zeros((SUBLANES, lw), F32) for _ in range(K)))
            for k in range(K):
                accw[SUBLANES * k:SUBLANES * (k + 1), lanes] += sums[k]
        dglu = dglu_s[...]
        da = jnp.concatenate([dglu * sig, dglu * cv[:, :dc] * sig * (1.0 - sig)], axis=1)
        da_ref[...] = da.astype(BF16)
        _accumulate(i, dbin_ref, jnp.sum(da, axis=0, keepdims=True))

        @pl.when(i == nt - 1)
        def _():
            dw_ref[...] = jnp.zeros_like(dw_ref)
            for k in range(K):
                dw_ref[k:k + 1, :] = jnp.sum(accw[SUBLANES * k:SUBLANES * (k + 1), :], axis=0, keepdims=True)

    res = _call(
        body, name, (nt,),
        [_rows(tm, dc), _next_halo(tm, H, dc, S // H), _rows(tm, 2 * dc), _prev_halo(tm, H, 2 * dc),
         _whole(w.shape), pl.BlockSpec(memory_space=pl.ANY)],
        [_rows(tm, 2 * dc), _whole((kp, dc)), _whole((1, 2 * dc))],
        [jax.ShapeDtypeStruct(dproj.shape, BF16), jax.ShapeDtypeStruct((kp, dc), F32),
         jax.ShapeDtypeStruct((1, 2 * dc), F32)],
        [pltpu.VMEM((SUBLANES, tm + H, dc), F32), pltpu.VMEM((SUBLANES, tm + H, dc), F32),
         pltpu.VMEM((SUBLANES * K, dc), F32), pltpu.VMEM((tm, dc), F32)],
        ("arbitrary",), (dconv, dconv, proj, proj, w, dproj), aliases={5: 0}, job=job)
    return res if job is None else (res[:3], res[3:])


def _mix(vln, wsm, bst, out_ref, G, CH, hd):
    for n in range(vln.shape[0] // CH):
        for g in range(G):
            blk = vln[n * CH:(n + 1) * CH, g * hd:(g + 1) * hd].astype(BF16)
            out_ref[n * CH:(n + 1) * CH, g * hd:(g + 1) * hd] = (
                jnp.dot(wsm[g], blk, preferred_element_type=F32) + bst[:, g:g + 1])


def _blocks_to_columns(full, name):
    nb, K, n = full.shape
    tr = _pick(K, max(SUBLANES, (1 << 19) // n), 2 * SUBLANES)

    def body(s_ref, o_ref):
        o_ref[...] = s_ref[...]

    return pl.pallas_call(
        body, name=name, grid=(nb, K // tr),
        in_specs=[pl.BlockSpec((None, tr, n), lambda p, i: (p, i, 0))],
        out_specs=pl.BlockSpec((tr, n), lambda p, i: (i, p)),
        out_shape=jax.ShapeDtypeStruct((K, nb * n), full.dtype),
        compiler_params=_params(("parallel", "parallel")))(full)


def _sgu_fwd(name, proj, a_act, wco, wso, lg, lb, ws, bst, D, ds, job=None):
    S = proj.shape[0]
    dc = a_act.shape[1]
    G, CH, _ = ws.shape
    hd = ds // G
    tm = _pick(S, 256, CH)

    def body(s_ref, gt_ref, a_ref, wco_ref, wso_ref, lg_ref, lb_ref, ws_ref, bst_ref,
             ya_ref, yb_ref, uv_ref, mg_ref, vmix):
        z = _gelu(s_ref[...])
        vln = _layer_norm(z[:, ds:], lg_ref[...], lb_ref[...])
        wsm = [_tril_mask(ws_ref[g]).astype(BF16) for g in range(G)]
        _mix(vln, wsm, bst_ref[...], vmix, G, CH, hd)
        uv = (z[:, :ds] * vmix[...]).astype(BF16)
        uv_ref[...] = uv
        ya = jnp.dot(a_ref[...], wco_ref[...], preferred_element_type=F32)
        yb = jnp.dot(uv, wso_ref[...], preferred_element_type=F32)
        ya_ref[...] = ya.astype(BF16)
        yb_ref[...] = yb.astype(BF16)
        gt = gt_ref[...]
        mg_ref[...] = (jax.nn.sigmoid(gt[:, :D]) * ya + jax.nn.sigmoid(gt[:, D:]) * yb).astype(BF16)

    vec = _whole((1, ds))
    sdb = jax.ShapeDtypeStruct((S, D), BF16)
    res = _call(body, name, (S // tm,),
                [_rows(tm, 2 * ds, 1), _rows(tm, 2 * D, 1), _rows(tm, dc), _whole(wco.shape), _whole(wso.shape),
                 vec, vec, _whole(ws.shape), _whole(bst.shape)],
                [_rows(tm, D), _rows(tm, D), _rows(tm, ds), _rows(tm, D)],
                [sdb, sdb, jax.ShapeDtypeStruct((S, ds), BF16), sdb],
                [pltpu.VMEM((tm, ds), F32)], ("parallel",), (proj, proj, a_act, wco, wso, lg, lb, ws, bst), job=job)
    return res if job is None else (res[:4], res[4:])


def _merge_bwd(name, dmerged, proj, ya, yb, D):
    S = proj.shape[0]
    tm = _pick(S, 256, LANES)

    def body(dm_ref, gt_ref, ya_ref, yb_ref, dya_ref, dyb_ref, dg_ref, dbin_ref):
        i = pl.program_id(0)
        dm = dm_ref[...]
        gt = gt_ref[...]
        sa = jax.nn.sigmoid(gt[:, :D])
        sb = jax.nn.sigmoid(gt[:, D:])
        dya_ref[...] = (dm * sa).astype(BF16)
        dyb_ref[...] = (dm * sb).astype(BF16)
        ya = ya_ref[...].astype(F32)
        yb = yb_ref[...].astype(F32)
        dg = jnp.concatenate([dm * ya * sa * (1.0 - sa), dm * yb * sb * (1.0 - sb)], axis=1)
        dg_ref[...] = dg.astype(BF16)
        _accumulate(i, dbin_ref, jnp.sum(dg, axis=0, keepdims=True))

    row = _rows(tm, D)
    sdb = jax.ShapeDtypeStruct((S, D), BF16)
    return pl.pallas_call(
        body, name=name, grid=(S // tm,),
        in_specs=[row, _rows(tm, 2 * D, 1), row, row],
        out_specs=[row, row, _rows(tm, 2 * D, 1), _whole((1, 2 * D))],
        out_shape=[sdb, sdb, jax.ShapeDtypeStruct((S, 4 * D), BF16), jax.ShapeDtypeStruct((1, 2 * D), F32)],
        compiler_params=_params(("arbitrary",)),
    )(dmerged, proj, ya, yb)


def _sgu_bwd(name, proj, duv, lg, lb, ws, bst, dproj, ds):
    S = proj.shape[0]
    G, CH, _ = ws.shape
    hd = ds // G
    tm = _pick(S, 256, CH)

    def body(s_ref, duv_ref, lg_ref, lb_ref, ws_ref, bst_ref, dproj_any,
             dsin_ref, dws_ref, dbs_ref, dlg_ref, dlb_ref, dbin_ref, vmix, dvln):
        i = pl.program_id(0)
        z, pull_gelu = jax.vjp(_gelu, s_ref[...])
        u = z[:, :ds]
        vln, pull_ln = jax.vjp(_layer_norm, z[:, ds:], lg_ref[...], lb_ref[...])
        wsm = [_tril_mask(ws_ref[g]).astype(BF16) for g in range(G)]
        _mix(vln, wsm, bst_ref[...], vmix, G, CH, hd)
        duv = duv_ref[...]
        du = duv * vmix[...]
        dvmix = duv * u
        for g in range(G):
            dws_g = jnp.zeros((CH, CH), F32)
            dbs_g = jnp.zeros((CH, 1), F32)
            for n in range(tm // CH):
                dblk = dvmix[n * CH:(n + 1) * CH, g * hd:(g + 1) * hd]
                vblk = vln[n * CH:(n + 1) * CH, g * hd:(g + 1) * hd].astype(BF16)
                dvln[n * CH:(n + 1) * CH, g * hd:(g + 1) * hd] = lax.dot_general(
                    wsm[g], dblk.astype(BF16), TN, preferred_element_type=F32)
                dws_g = dws_g + lax.dot_general(dblk.astype(BF16), vblk, NT, preferred_element_type=F32)
                dbs_g = dbs_g + jnp.sum(dblk, axis=1, keepdims=True)
            dws_g = _tril_mask(dws_g)
            dbs_g = jnp.broadcast_to(dbs_g, (CH, LANES))

            @pl.when(i == 0)
            def _():
                dws_ref[g] = dws_g
                dbs_ref[g] = dbs_g

            @pl.when(i > 0)
            def _():
                dws_ref[g] += dws_g
                dbs_ref[g] += dbs_g

        dv, dlg, dlb = pull_ln(dvln[...])
        (dsin,) = pull_gelu(jnp.concatenate([du, dv], axis=1))
        dsin_ref[...] = dsin.astype(BF16)
        _accumulate(i, dlg_ref, dlg)
        _accumulate(i, dlb_ref, dlb)
        _accumulate(i, dbin_ref, jnp.sum(dsin, axis=0, keepdims=True))

    vec = _whole((1, ds))
    vsd = jax.ShapeDtypeStruct((1, ds), F32)
    return pl.pallas_call(
        body, name=name, grid=(S // tm,),
        in_specs=[_rows(tm, 2 * ds, 1), _rows(tm, ds), vec, vec, _whole(ws.shape), _whole(bst.shape),
                  pl.BlockSpec(memory_space=pl.ANY)],
        out_specs=[_rows(tm, 2 * ds, 1), _whole((G, CH, CH)), _whole((G, CH, LANES)), vec, vec, _whole((1, 2 * ds))],
        out_shape=[jax.ShapeDtypeStruct(dproj.shape, BF16), jax.ShapeDtypeStruct((G, CH, CH), F32),
                   jax.ShapeDtypeStruct((G, CH, LANES), F32), vsd, vsd, jax.ShapeDtypeStruct((1, 2 * ds), F32)],
        scratch_shapes=[pltpu.VMEM((tm, ds), F32), pltpu.VMEM((tm, ds), F32)],
        input_output_aliases={6: 0},
        compiler_params=_params(("arbitrary",)),
    )(proj, duv, lg, lb, ws, bst, dproj)


def _silu_mul(val, gt):
    return jax.nn.silu(gt) * val


def _rotation_slots(offs):
    slot = {0: 0}
    for r in sorted({o % SUBLANES for o in offs} - {0}):
        slot[r] = len(slot)
    return slot


def _fill_plane_rotations(rot, slot):
    n = rot.shape[2]
    for r, s in slot.items():
        if r:
            rot[:, s, 0:n - SUBLANES, :] = rot[:, 0, r:r + n - SUBLANES, :]


def _ffn_tiles(S, Fh):
    return _pick(S, 256, LANES), _pick(Fh, 1408, LANES)


def _ffn_fwd(name, up, w, b):
    _, S, Fh = up.shape
    K = w.shape[1]
    H = FFN_HALO
    tm, cw = _ffn_tiles(S, Fh)
    r = tm // H

    offs = [H - (K - 1) + k for k in range(K)]
    slot = _rotation_slots(offs)
    rb = _pick(tm, 64, 2 * SUBLANES)
    lw = min(LANES, cw)

    def body(cur_ref, prev_ref, w_ref, b_ref, act_ref, rot):
        i = pl.program_id(1)
        rot[:, 0, 0:H, :] = jnp.where(i > 0, prev_ref[...], 0.0)
        rot[:, 0, H:, :] = cur_ref[...]
        _fill_plane_rotations(rot, slot)

        def chunk(row0):
            for l0 in range(0, cw, lw):
                lanes = slice(l0, l0 + lw)
                c2 = []
                for pln in range(2):
                    acc = jnp.broadcast_to(b_ref[pln, :, lanes], (rb, lw))
                    for k in range(K):
                        a, rr = divmod(offs[k], SUBLANES)
                        acc = acc + w_ref[pln, k:k + 1, lanes] * rot[pln, slot[rr], pl.ds(row0 + SUBLANES * a, rb), lanes]
                    c2.append(acc)
                act_ref[pl.ds(row0, rb), lanes] = _silu_mul(c2[0], c2[1]).astype(BF16)

        _for_chunks(tm, rb, chunk)

    return pl.pallas_call(
        body, name=name, grid=(Fh // cw, S // tm),
        in_specs=[pl.BlockSpec((2, tm, cw), lambda j, i: (0, i, j)),
                  pl.BlockSpec((2, H, cw), lambda j, i: (0, jnp.maximum(i * r - 1, 0), j)),
                  pl.BlockSpec((2, K, cw), lambda j, i: (0, 0, j)),
                  pl.BlockSpec((2, 1, cw), lambda j, i: (0, 0, j))],
        out_specs=pl.BlockSpec((tm, cw), lambda j, i: (i, j)),
        out_shape=jax.ShapeDtypeStruct((S, Fh), BF16),
        scratch_shapes=[pltpu.VMEM((2, len(slot), tm + H, cw), F32)],
        compiler_params=_params(("parallel", "parallel")),
    )(up, up, w, b)


def _ffn_bwd(name, up, dact, w, b, job=None):
    _, S, Fh = up.shape
    K = w.shape[1]
    H = FFN_HALO
    tm, cw = _ffn_tiles(S, Fh)
    r = tm // H
    nt = S // tm
    nhb = S // H
    te = tm + H

    offs_x = [H - (K - 1) + k for k in range(K)]
    offs_d = [K - 1 - k for k in range(K)]
    slot_x = _rotation_slots(offs_x)
    slot_d = _rotation_slots(offs_d)
    rb = _pick(tm, 64, 2 * SUBLANES)
    rbe = _pick(te, 96, SUBLANES)
    lw = min(LANES, cw)

    def body(cur_ref, prev_ref, next_ref, d_ref, dn_ref, w_ref, b_ref, dup_ref, dwb_ref, rotx, dext, rotd, accw):
        i = pl.program_id(1)
        rotx[:, 0, 0:H, :] = jnp.where(i > 0, prev_ref[...], 0.0)
        rotx[:, 0, H:H + tm, :] = cur_ref[...]
        rotx[:, 0, H + tm:, :] = jnp.where(i < nt - 1, next_ref[...], 0.0)
        dext[0:tm, :] = d_ref[...]
        dext[tm:, :] = jnp.where(i < nt - 1, dn_ref[...], 0.0)
        _fill_plane_rotations(rotx, slot_x)

        def chunk_e(row0):
            for l0 in range(0, cw, lw):
                lanes = slice(l0, l0 + lw)
                c2 = []
                for pln in range(2):
                    acc = jnp.broadcast_to(b_ref[pln, :, lanes], (rbe, lw))
                    for k in range(K):
                        a, rr = divmod(offs_x[k], SUBLANES)
                        acc = acc + w_ref[pln, k:k + 1, lanes] * rotx[pln, slot_x[rr], pl.ds(row0 + SUBLANES * a, rbe), lanes]
                    c2.append(acc)
                _, pull = jax.vjp(_silu_mul, c2[0], c2[1])
                dval, dgt = pull(dext[pl.ds(row0, rbe), lanes])
                rotd[0, 0, pl.ds(row0, rbe), lanes] = dval
                rotd[1, 0, pl.ds(row0, rbe), lanes] = dgt

        _for_chunks(te, rbe, chunk_e)
        _fill_plane_rotations(rotd, slot_d)

        @pl.when(i == 0)
        def _():
            accw[...] = jnp.zeros_like(accw)

        def chunk(row0):
            for l0 in range(0, cw, lw):
                lanes = slice(l0, l0 + lw)
                for pln in range(2):
                    dcur = rotd[pln, 0, pl.ds(row0, rb), lanes]
                    dup = jnp.zeros((rb, lw), F32)
                    for k in range(K):
                        a, rr = divmod(offs_d[k], SUBLANES)
                        dup = dup + w_ref[pln, k:k + 1, lanes] * rotd[pln, slot_d[rr], pl.ds(row0 + SUBLANES * a, rb), lanes]
                        a, rr = divmod(offs_x[k], SUBLANES)
                        prod = dcur * rotx[pln, slot_x[rr], pl.ds(row0 + SUBLANES * a, rb), lanes]
                        accw[pln, SUBLANES * k:SUBLANES * (k + 1), lanes] += jnp.sum(
                            prod.reshape(rb // SUBLANES, SUBLANES, lw), axis=0)
                    accw[pln, SUBLANES * K:SUBLANES * (K + 1), lanes] += jnp.sum(
                        dcur.reshape(rb // SUBLANES, SUBLANES, lw), axis=0)
                    dup_ref[pln, pl.ds(row0, rb), lanes] = dup.astype(BF16)

        _for_chunks(tm, rb, chunk)

        @pl.when(i == nt - 1)
        def _():
            dwb_ref[...] = jnp.zeros_like(dwb_ref)
            for pln in range(2):
                for k in range(K + 1):
                    dwb_ref[pln, k:k + 1, :] = jnp.sum(accw[pln, SUBLANES * k:SUBLANES * (k + 1), :], axis=0,
                                                       keepdims=True)

    res = _call(
        body, name, (Fh // cw, nt),
        [pl.BlockSpec((2, tm, cw), lambda j, i: (0, i, j)),
         pl.BlockSpec((2, H, cw), lambda j, i: (0, jnp.maximum(i * r - 1, 0), j)),
         pl.BlockSpec((2, H, cw), lambda j, i: (0, jnp.minimum((i + 1) * r, nhb - 1), j)),
         pl.BlockSpec((tm, cw), lambda j, i: (i, j)),
         pl.BlockSpec((H, cw), lambda j, i: (jnp.minimum((i + 1) * r, nhb - 1), j)),
         pl.BlockSpec((2, K, cw), lambda j, i: (0, 0, j)),
         pl.BlockSpec((2, 1, cw), lambda j, i: (0, 0, j))],
        [pl.BlockSpec((2, tm, cw), lambda j, i: (0, i, j)), pl.BlockSpec((2, SUBLANES, cw), lambda j, i: (0, 0, j))],
        [jax.ShapeDtypeStruct((2, S, Fh), BF16), jax.ShapeDtypeStruct((2, SUBLANES, Fh), F32)],
        [pltpu.VMEM((2, len(slot_x), tm + 2 * H, cw), F32), pltpu.VMEM((te, cw), F32),
         pltpu.VMEM((2, len(slot_d), te, cw), F32), pltpu.VMEM((2, SUBLANES * (K + 1), cw), F32)],
        ("parallel", "arbitrary"), (up, up, up, dact, dact, w, b), job=job)
    return res if job is None else (res[:2], res[2:])


def _rms(x, g):
    return x * lax.rsqrt(jnp.mean(x * x, axis=-1, keepdims=True) + EPS) * g


def _gate_bwd_tile(i, dx, out_ref, gate_ref, dout_ref, dgate_ref):
    dout_ref[...] = (dx * gate_ref[...]).astype(BF16)
    _accumulate(i, dgate_ref, jnp.sum(dx * out_ref[...].astype(F32), axis=0, keepdims=True))


def _final(name, x2, target, gf, out, gate):
    S, D = x2.shape
    tm = _pick(S, 256, LANES)

    def body(x_ref, t_ref, g_ref, o_ref, gate_ref, dx_ref, dg_ref, loss_ref, do_ref, dgate_ref):
        i = pl.program_id(0)
        y, pull = jax.vjp(_rms, x_ref[...], g_ref[...])
        e = y - t_ref[...]
        dx, dg = pull(e / D)
        dx_ref[...] = dx
        _accumulate(i, dg_ref, dg)
        part = 0.5 * jnp.sum(jnp.mean(jnp.square(e), axis=-1, keepdims=True), axis=0, keepdims=True)
        _accumulate(i, loss_ref, jnp.broadcast_to(part, (SUBLANES, LANES)))
        _gate_bwd_tile(i, dx, o_ref, gate_ref, do_ref, dgate_ref)

    row = _rows(tm, D)
    vec = _whole((1, D))
    vsd = jax.ShapeDtypeStruct((1, D), F32)
    return pl.pallas_call(
        body, name=name, grid=(S // tm,), in_specs=[row, row, vec, row, vec],
        out_specs=[row, vec, _whole((SUBLANES, LANES)), row, vec],
        out_shape=[jax.ShapeDtypeStruct((S, D), F32), vsd, jax.ShapeDtypeStruct((SUBLANES, LANES), F32),
                   jax.ShapeDtypeStruct((S, D), BF16), vsd],
        compiler_params=_params(("arbitrary",)),
    )(x2, target, gf, out, gate)


def _ada_fwd(name, c_pad, w_ada, b_cols):
    nb, D = c_pad.shape
    n = w_ada.shape[1]
    tn = _pick(n, 1024, LANES)

    def body(c_ref, w_ref, b_ref, o_ref):
        o_ref[...] = jnp.dot(jax.nn.silu(c_ref[...]).astype(BF16), w_ref[...].astype(BF16),
                             preferred_element_type=F32) + b_ref[...]

    return pl.pallas_call(
        body, name=name, grid=(n // tn,),
        in_specs=[_whole((nb, D)), pl.BlockSpec((D, tn), lambda j: (0, j)), pl.BlockSpec((1, tn), lambda j: (0, j))],
        out_specs=pl.BlockSpec((nb, tn), lambda j: (0, j)),
        out_shape=jax.ShapeDtypeStruct((nb, n), F32), compiler_params=_params(("parallel",)),
    )(c_pad, w_ada, b_cols)


def _ada_wgrad(name, c_t, dmod_cols):
    D, nb = c_t.shape
    n = dmod_cols.shape[1]
    tr = _pick(D, 256, SUBLANES)

    def body(c_ref, d_ref, o_ref):
        ca = jax.nn.silu(c_ref[...])
        acc = ca[:, 0:1] * d_ref[0:1, :]
        for b in range(1, nb):
            acc = acc + ca[:, b:b + 1] * d_ref[b:b + 1, :]
        o_ref[...] = acc

    return pl.pallas_call(
        body, name=name, grid=(D // tr,),
        in_specs=[pl.BlockSpec((tr, nb), lambda i: (i, 0)), _whole((nb, n))],
        out_specs=pl.BlockSpec((tr, n), lambda i: (i, 0)),
        out_shape=jax.ShapeDtypeStruct((D, n), F32), compiler_params=_params(("parallel",)),
    )(c_t, dmod_cols)


def kernel(x, c, w_ada, b_ada, norm1_g, w_in, b_in, conv_dw_w, conv_dw_b, conv_ln_g, conv_ln_b, w_conv_out, sgu_ln_g, sgu_ln_b, w_spatial, b_spatial, w_sgu_out, w_out, norm2_g, w_up, ffn_dw_w, ffn_dw_b, w_down, final_g, loss_target, m_w_ada, m_b_ada, m_norm1_g, m_w_in, m_b_in, m_conv_dw_w, m_conv_dw_b, m_conv_ln_g, m_conv_ln_b, m_w_conv_out, m_sgu_ln_g, m_sgu_ln_b, m_w_spatial, m_b_spatial, m_w_sgu_out, m_w_out, m_norm2_g, m_w_up, m_ffn_dw_w, m_ffn_dw_b, m_w_down, m_final_g, v_w_ada, v_b_ada, v_norm1_g, v_w_in, v_b_in, v_conv_dw_w, v_conv_dw_b, v_conv_ln_g, v_conv_ln_b, v_w_conv_out, v_sgu_ln_g, v_sgu_ln_b, v_w_spatial, v_b_spatial, v_w_sgu_out, v_w_out, v_norm2_g, v_w_up, v_ffn_dw_w, v_ffn_dw_b, v_w_down, v_final_g):
    S, D = x.shape[1], x.shape[2]
    dc = w_conv_out.shape[1]
    ds = w_sgu_out.shape[1]
    G, CH = w_spatial.shape[1], w_spatial.shape[2]
    KC = conv_dw_w.shape[1]
    KF = ffn_dw_w.shape[1]
    F2 = ffn_dw_b.shape[1]
    Fh = F2 // 2
    n_ada = w_ada.shape[2]
    n_up = w_up.shape[2]
    ax, ay, ac = _axes()
    chip = 2 * ax + ay
    me = 2 * chip + ac
    c_idx = jnp.reshape(ac, (1,)).astype(jnp.int32)
    p_idx = jnp.reshape(chip, (1,)).astype(jnp.int32)

    xs = x[0]
    tgt = loss_target[0]

    g1 = _allgather8(_pack([c[0], conv_dw_w[0], ffn_dw_w[0]]), "gather_small_in")
    n_cw, n_fw = conv_dw_w.shape[2], ffn_dw_w.shape[2]
    parts = [_unpack(g1[2 * q], [(D,), conv_dw_w.shape[1:], ffn_dw_w.shape[1:]]) for q in range(N_CHIPS)]
    c_all = jnp.stack([_unpack(g1[d], [(D,)])[0] for d in range(N_DEV)])
    cw_full = jnp.concatenate([pt[1] for pt in parts], axis=1)
    fw_full = jnp.concatenate([pt[2] for pt in parts], axis=1)

    b_cols = lax.dynamic_slice(b_ada, (0, chip * n_ada), (1, n_ada))
    c_pad = jnp.concatenate([c_all, jnp.zeros_like(c_all)], axis=0)
    mod_blk = _ada_fwd("ada_fwd", c_pad, w_ada[0], b_cols)[:N_DEV]
    g2 = _allgather8(_pack([mod_blk]), "gather_mod")
    mod_all = jnp.concatenate([_unpack(g2[2 * q], [(N_DEV, n_ada)])[0] for q in range(N_CHIPS)], axis=1)
    mod = lax.dynamic_slice(mod_all, (me, 0), (1, 6 * D))
    shift1, scale1, gate1, shift2, scale2, gate2 = [mod[:, k * D:(k + 1) * D] for k in range(6)]

    shards = [w_in[0], w_conv_out[0], w_sgu_out[0], w_out[0], w_up[0], w_down[0]]
    names = ["in", "conv_out", "sgu_out", "out", "up", "down"]
    blk = {nm: _cast_into_block(s, p_idx, "cast_" + nm) for s, nm in zip(shards, names)}
    pc_idx = jnp.concatenate([p_idx, c_idx])

    h1 = _modnorm_fwd("modnorm1", xs, norm1_g, scale1, shift1)
    (proj,), (win_xy,) = _mm_fwd_block("proj_own", h1, blk["in"], b_in, p_idx, 0,
                                       job=_job_gather([blk["in"]], rel=(0, 1), fresh=True))
    (proj,), (win_d,) = _mm_fwd_block("proj_x", h1, win_xy, b_in, p_idx, 2, into=proj,
                                      job=_job_gather([blk["in"]], rel=(2,), fresh=True))
    (proj,), (wco_f, wso_f) = _mm_fwd_block("proj_y", h1, win_xy, b_in, p_idx, 1, into=proj,
                                            job=_job_gather([blk["conv_out"], blk["sgu_out"]]))
    (proj,), (wout_f,) = _mm_fwd_block("proj_diag", h1, win_d, b_in, p_idx, 3, into=proj,
                                       job=_job_gather([blk["out"]]))
    win_f = _copy_block(_copy_block(win_xy, blk["in"], p_idx, 0, "fill_w_in_own"), win_d, p_idx, 3, "fill_w_in_diag")
    wout_r = wout_f.reshape(-1, wout_f.shape[2])
    wco_c = _blocks_to_columns(wco_f, "columns_conv_out")
    wso_c = _blocks_to_columns(wso_f, "columns_sgu_out")
    (conv, a_act), (wup_xy,) = _conv_fwd("conv_fwd", proj, cw_full, conv_dw_b, conv_ln_g, conv_ln_b, dc,
                                         job=_job_gather([blk["up"]], rel=(0, 1)))
    bst = jnp.transpose(b_spatial[0])
    (ya, yb, uv, merged), (wup_f,) = _sgu_fwd("sgu_fwd", proj, a_act, wco_c, wso_c, sgu_ln_g, sgu_ln_b, w_spatial[0],
                                              bst, D, ds, job=_job_gather([wup_xy], rel=(2,)))
    out1, x1 = _mm_fwd_rows("out1", merged, wout_r, xs, gate1)
    h2 = _modnorm_fwd("modnorm2", x1, norm2_g, scale2, shift2)
    up, (wdown_f,) = _mm_fwd("up", h2, wup_f, planes=2, job=_job_gather([blk["down"]]))
    wdown_r = wdown_f.reshape(-1, wdown_f.shape[2])
    fw2 = jnp.stack([fw_full[:, :Fh], fw_full[:, Fh:]])
    fb2 = jnp.stack([ffn_dw_b[:, :Fh], ffn_dw_b[:, Fh:]])
    act = _ffn_fwd("ffn_fwd", up, fw2, fb2)
    out2, x2 = _mm_fwd_rows("out2", act, wdown_r, x1, gate2)
    dx2, d_final_g, loss_blk, dout2, d_gate2 = _final("final", x2, tgt, final_g.reshape(1, D), out2, gate2)
    loss = lax.psum(loss_blk[0, 0], ("x", "y", "c"))

    def add_cores(nm, g, r1):
        return _add_own_half(g, r1, c_idx, "add_cores_" + nm)

    def add_chips(nm, g, r1, r2):
        return _add_chips(g, r1, r2, pc_idx, "add_chips_" + nm)

    g_wdown = _mm_wgrad_rows("wgrad_down", act, dout2, w_down.shape[1])
    dact, (r1_down,) = _mm_dgrad_rows("dgrad_down", dout2, wdown_r, job=_job_swap_halves([g_wdown]))
    s_down = add_cores("down", g_wdown, r1_down)
    (dup, d_ffn), (r2_down,) = _ffn_bwd("ffn_bwd", up, dact, fw2, fb2, job=_job_scatter_blocks([s_down]))
    h_down = add_chips("down", g_wdown, r1_down, r2_down)
    g_wup = _mm_wgrad_cols("wgrad_up", h2, dup, n_up)
    dh2, (r1_up,) = _mm_dgrad_cols("dgrad_up", dup, wup_f, job=_job_swap_halves([g_wup]))
    s_up = add_cores("up", g_wup, r1_up)
    dx1, d_norm2, d_scale2, d_shift2, dout1, d_gate1 = _modnorm_bwd(
        "modnorm2_bwd", x1, dh2, dx2, norm2_g, scale2, shift2, gated=(out1, gate1))
    g_wout = _mm_wgrad_rows("wgrad_out", merged, dout1, w_out.shape[1])
    dmerged = _mm_dgrad_rows("dgrad_out", dout1, wout_r)
    dya, dyb, dproj, dbin_g = _merge_bwd("merge_bwd", dmerged, proj, ya, yb, D)
    g_wco = _mm_wgrad_cols("wgrad_conv_out", a_act, dya[None], w_conv_out.shape[2])
    g_wso = _mm_wgrad_cols("wgrad_sgu_out", uv, dyb[None], w_sgu_out.shape[2])
    da_act, (r1_out, r1_co, r1_so) = _mm_dgrad_rows("dgrad_conv_out", dya, wco_c,
                                                    job=_job_swap_halves([g_wout, g_wco, g_wso]))
    s_out = add_cores("out", g_wout, r1_out)
    s_co = add_cores("conv_out", g_wco, r1_co)
    s_so = add_cores("sgu_out", g_wso, r1_so)
    duv = _mm_dgrad_rows("dgrad_sgu_out", dyb, wso_c)
    dproj, d_ws, d_bs, d_sgu_g, d_sgu_b, dbin_s = _sgu_bwd("sgu_bwd", proj, duv, sgu_ln_g, sgu_ln_b, w_spatial[0], bst,
                                                           dproj, ds)
    dconv, d_cln_g, d_cln_b, d_conv_b = _conv_bwd_ln("conv_ln_bwd", conv, da_act, conv_ln_g, conv_ln_b)
    d_fw = jnp.concatenate([d_ffn[0, :KF], d_ffn[1, :KF]], axis=1)
    d_fb = jnp.concatenate([d_ffn[0, KF:KF + 1], d_ffn[1, KF:KF + 1]], axis=1)
    early = [d_sgu_g, d_sgu_b, d_ws, d_bs[:, :, 0], d_norm2, d_fb, d_final_g, d_fw]
    (dproj, d_cw, dbin_a), (r2_up, g_early) = _conv_bwd(
        "conv_bwd", dconv, proj, cw_full, dproj, dc,
        job=_merge_jobs([_job_scatter_blocks([s_up]), _job_allgather8(_pack(early))]))
    h_up = add_chips("up", g_wup, r1_up, r2_up)
    g_win, (r2_out, r2_co, r2_so, sib_up, sib_down) = _mm_wgrad_cols(
        "wgrad_in", h1, dproj[None], w_in.shape[2],
        job=_merge_jobs([_job_scatter_blocks([s_out, s_co, s_so]), _job_to_sibling([h_up, h_down])]))
    h_out = add_chips("out", g_wout, r1_out, r2_out)
    h_co = add_chips("conv_out", g_wco, r1_co, r2_co)
    h_so = add_chips("sgu_out", g_wso, r1_so, r2_so)
    dh1, (r1_in, sib_out, sib_co, sib_so) = _mm_dgrad_cols(
        "dgrad_in_a", dproj[None], win_f, row_tiles=(0.0, 0.5),
        job=_merge_jobs([_job_swap_halves([g_win]), _job_to_sibling([h_out, h_co, h_so])]))
    s_in = add_cores("in", g_win, r1_in)
    dh1, (r2_in,) = _mm_dgrad_cols("dgrad_in_b", dproj[None], win_f, row_tiles=(0.5, 0.5), fill_into=dh1,
                                   job=_job_scatter_blocks([s_in]))
    h_in = add_chips("in", g_win, r1_in, r2_in)
    dxs, d_norm1, d_scale1, d_shift1 = _modnorm_bwd("modnorm1_bwd", xs, dh1, dx1, norm1_g, scale1, shift1)

    d_mod = jnp.concatenate([d_shift1, d_scale1, d_gate1, d_shift2, d_scale2, d_gate2], axis=1)
    d_b_in = jnp.concatenate([dbin_a, dbin_s, dbin_g], axis=1)
    late = [d_mod, d_norm1, d_b_in, d_conv_b, d_cln_g, d_cln_b, d_cw[:KC]]
    g3, sib_in = _run_job(_merge_jobs([_job_allgather8(_pack(late)), _job_to_sibling([h_in])]), "gather_small_grads")
    big_halves = {"w_in": (h_in, sib_in), "w_conv_out": (h_co, sib_co), "w_sgu_out": (h_so, sib_so),
                  "w_out": (h_out, sib_out), "w_up": (h_up, sib_up), "w_down": (h_down, sib_down)}
    g_b_ada, g_norm1, g_b_in, g_conv_b, g_cln_g, g_cln_b, g_cw_full = _unpack(
        _sum8(g3, "sum_small_grads"), [a.shape for a in late])
    g_sgu_g, g_sgu_b, g_ws, g_bs, g_norm2, g_fb, g_final, g_fw_full = _unpack(
        _sum8(g_early, "sum_early_grads"), [a.shape for a in early])
    g_cw = lax.dynamic_slice(g_cw_full, (0, chip * n_cw), (KC, n_cw))
    g_fw = lax.dynamic_slice(g_fw_full, (0, chip * n_fw), (KF, n_fw))
    dmod_all = g3.reshape(N_DEV, -1)[:, :6 * D]
    dmod_cols = lax.dynamic_slice(dmod_all, (0, chip * n_ada), (N_DEV, n_ada))
    g_wada = _ada_wgrad("ada_wgrad", jnp.transpose(c_all), dmod_cols)

    grads = {
        "w_ada": g_wada[None], "b_ada": g_b_ada, "norm1_g": g_norm1, "b_in": g_b_in,
        "conv_dw_w": g_cw[None], "conv_dw_b": g_conv_b, "conv_ln_g": g_cln_g, "conv_ln_b": g_cln_b,
        "sgu_ln_g": g_sgu_g, "sgu_ln_b": g_sgu_b, "w_spatial": g_ws[None],
        "b_spatial": g_bs[None], "norm2_g": g_norm2, "ffn_dw_w": g_fw[None], "ffn_dw_b": g_fb,
        "final_g": g_final.reshape(D),
    }
    weights = dict(w_ada=w_ada, b_ada=b_ada, norm1_g=norm1_g, w_in=w_in, b_in=b_in, conv_dw_w=conv_dw_w, conv_dw_b=conv_dw_b, conv_ln_g=conv_ln_g, conv_ln_b=conv_ln_b, w_conv_out=w_conv_out, sgu_ln_g=sgu_ln_g, sgu_ln_b=sgu_ln_b, w_spatial=w_spatial, b_spatial=b_spatial, w_sgu_out=w_sgu_out, w_out=w_out, norm2_g=norm2_g, w_up=w_up, ffn_dw_w=ffn_dw_w, ffn_dw_b=ffn_dw_b, w_down=w_down, final_g=final_g)
    m_in = dict(w_ada=m_w_ada, b_ada=m_b_ada, norm1_g=m_norm1_g, w_in=m_w_in, b_in=m_b_in, conv_dw_w=m_conv_dw_w, conv_dw_b=m_conv_dw_b, conv_ln_g=m_conv_ln_g, conv_ln_b=m_conv_ln_b, w_conv_out=m_w_conv_out, sgu_ln_g=m_sgu_ln_g, sgu_ln_b=m_sgu_ln_b, w_spatial=m_w_spatial, b_spatial=m_b_spatial, w_sgu_out=m_w_sgu_out, w_out=m_w_out, norm2_g=m_norm2_g, w_up=m_w_up, ffn_dw_w=m_ffn_dw_w, ffn_dw_b=m_ffn_dw_b, w_down=m_w_down, final_g=m_final_g)
    v_in = dict(w_ada=v_w_ada, b_ada=v_b_ada, norm1_g=v_norm1_g, w_in=v_w_in, b_in=v_b_in, conv_dw_w=v_conv_dw_w, conv_dw_b=v_conv_dw_b, conv_ln_g=v_conv_ln_g, conv_ln_b=v_conv_ln_b, w_conv_out=v_w_conv_out, sgu_ln_g=v_sgu_ln_g, sgu_ln_b=v_sgu_ln_b, w_spatial=v_w_spatial, b_spatial=v_b_spatial, w_sgu_out=v_w_sgu_out, w_out=v_w_out, norm2_g=v_norm2_g, w_up=v_w_up, ffn_dw_w=v_ffn_dw_w, ffn_dw_b=v_ffn_dw_b, w_down=v_w_down, final_g=v_final_g)
    order = list(weights.keys())
    large = ["w_ada", "w_in", "w_conv_out", "w_sgu_out", "w_out", "w_up", "w_down"]
    little = [n for n in order if n not in large]
    delta, new_m, new_v = {}, {}, {}
    for n in large:
        shp = weights[n].shape
        two = (shp[1], shp[2])
        if n in big_halves:
            g_, d_, m_, v_ = _adamw_halves(weights[n].reshape(two), big_halves[n][0], big_halves[n][1],
                                           m_in[n].reshape(two), v_in[n].reshape(two), c_idx, "adamw_" + n)
            grads[n] = g_.reshape(shp)
        else:
            d_, m_, v_ = _adamw(weights[n].reshape(two), grads[n].reshape(two), m_in[n].reshape(two),
                                v_in[n].reshape(two), "adamw_" + n)
        delta[n], new_m[n], new_v[n] = d_.reshape(shp), m_.reshape(shp), v_.reshape(shp)
    shapes = [weights[n].shape for n in little]
    d_, m_, v_ = _adamw(_pack([weights[n] for n in little]), _pack([grads[n] for n in little]),
                        _pack([m_in[n] for n in little]), _pack([v_in[n] for n in little]), "adamw_small")
    for n, dd, mm, vv in zip(little, _unpack(d_, shapes), _unpack(m_, shapes), _unpack(v_, shapes)):
        delta[n], new_m[n], new_v[n] = dd, mm, vv
    grad_out = [grads[n].reshape(weights[n].shape) for n in order]
    return (loss, dxs[None], *grad_out, *[delta[n] for n in order], *[new_m[n] for n in order],
            *[new_v[n] for n in order])
```

```python
import functools

import jax
import jax.numpy as jnp
from jax import lax
from jax.experimental import pallas as pl
from jax.experimental.pallas import tpu as pltpu

F32 = jnp.float32
BF16 = jnp.bfloat16
EPS = 1e-6
MESH = pl.DeviceIdType.MESH
N_CHIPS = 4
N_DEV = 8
LANES = 128
SUBLANES = 8
CONV_HALO = 32
FFN_HALO = 8
CONV_CHUNK_ROWS = 32
VMEM_LIMIT_BYTES = 56 * 1024 * 1024
WHOLE_WEIGHT_BYTES = 8 * 1024 * 1024

ADAM_LR = 0.001
ADAM_B1 = 0.9
ADAM_B2 = 0.999
ADAM_EPS = 1e-08
ADAM_WD = 0.01
ADAM_STEP = 10

NN = (((1,), (0,)), ((), ()))
NT = (((1,), (1,)), ((), ()))
TN = (((0,), (0,)), ((), ()))


def _params(sem=None):
    return pltpu.CompilerParams(dimension_semantics=sem, vmem_limit_bytes=VMEM_LIMIT_BYTES)


def _pick(dim, pref, mult):
    best = None
    d = mult
    while d <= min(dim, pref):
        if dim % d == 0:
            best = d
        d += mult
    return dim if best is None else best


def _axes():
    return lax.axis_index("x"), lax.axis_index("y"), lax.axis_index("c")


def _modnorm(x, g, scale, shift):
    r = lax.rsqrt(jnp.mean(x * x, axis=-1, keepdims=True) + EPS)
    return (x * r * g) * (1.0 + scale) + shift


def _layer_norm(x, g, b):
    mu = jnp.mean(x, axis=-1, keepdims=True)
    var = jnp.mean(jnp.square(x - mu), axis=-1, keepdims=True)
    return (x - mu) * lax.rsqrt(var + EPS) * g + b


def _gelu(x):
    return 0.5 * x * (1.0 + lax.erf(x * (0.5 ** 0.5)))


def _ln_silu(x, g, b):
    return jax.nn.silu(_layer_norm(x, g, b))


def _tril_mask(ws):
    n = ws.shape[-1]
    row = lax.broadcasted_iota(jnp.int32, (n, n), 0)
    col = lax.broadcasted_iota(jnp.int32, (n, n), 1)
    return jnp.where(row >= col, ws, 0.0)


def _pack(arrs):
    flat = [a.reshape(-1).astype(F32) for a in arrs]
    total = sum(f.shape[0] for f in flat)
    tile = SUBLANES * LANES
    padded = -(-total // tile) * tile
    if padded > total:
        flat = flat + [jnp.zeros((padded - total,), F32)]
    return jnp.concatenate(flat).reshape(padded // LANES, LANES)


def _unpack(buf, shapes):
    flat = buf.reshape(-1)
    out, off = [], 0
    for s in shapes:
        n = 1
        for d in s:
            n *= d
        out.append(flat[off:off + n].reshape(s))
        off += n
    return out


def _allgather8(buf, name):
    R, L = buf.shape

    def body(in_ref, out_ref, send_sems, recv_sems, local_sem):
        x, y, c = _axes()
        me = 4 * x + 2 * y + c
        mine = pltpu.make_async_copy(in_ref, out_ref.at[me], local_sem)
        mine.start()
        peers = []
        for k in range(1, N_DEV):
            px = 1 - x if k & 4 else x
            py = 1 - y if k & 2 else y
            pc = 1 - c if k & 1 else c
            peers.append((px, py, pc))
        sends = []
        for k, peer in enumerate(peers):
            cp = pltpu.make_async_remote_copy(
                src_ref=in_ref, dst_ref=out_ref.at[me], send_sem=send_sems.at[k], recv_sem=recv_sems.at[k],
                device_id=peer, device_id_type=MESH)
            cp.start()
            sends.append(cp)
        for k, (px, py, pc) in enumerate(peers):
            pltpu.make_async_remote_copy(
                src_ref=in_ref, dst_ref=out_ref.at[4 * px + 2 * py + pc], send_sem=send_sems.at[k],
                recv_sem=recv_sems.at[k], device_id=(px, py, pc), device_id_type=MESH).wait_recv()
        for cp in sends:
            cp.wait_send()
        mine.wait()

    return pl.pallas_call(
        body, name=name,
        out_shape=jax.ShapeDtypeStruct((N_DEV, R, L), buf.dtype),
        in_specs=[pl.BlockSpec(memory_space=pltpu.VMEM)],
        out_specs=pl.BlockSpec(memory_space=pltpu.VMEM),
        scratch_shapes=[pltpu.SemaphoreType.DMA((N_DEV - 1,)), pltpu.SemaphoreType.DMA((N_DEV - 1,)),
                        pltpu.SemaphoreType.DMA],
        compiler_params=pltpu.CompilerParams(vmem_limit_bytes=VMEM_LIMIT_BYTES),
    )(buf)


def _other_chips(x, y):
    return [(1 - x, y), (x, 1 - y), (1 - x, 1 - y)]


def _cast_into_block(shard, p_idx, name):
    K, n = shard.shape
    tr = _pick(K, max(SUBLANES, (1 << 19) // n), 2 * SUBLANES)

    def body(p_ref, s_ref, o_ref):
        o_ref[...] = s_ref[...].astype(BF16)

    return pl.pallas_call(
        body, name=name,
        grid_spec=pltpu.PrefetchScalarGridSpec(
            num_scalar_prefetch=1, grid=(K // tr,),
            in_specs=[pl.BlockSpec((tr, n), lambda i, p: (i, 0))],
            out_specs=pl.BlockSpec((None, tr, n), lambda i, p: (p[0], i, 0))),
        out_shape=jax.ShapeDtypeStruct((N_CHIPS, K, n), BF16),
        compiler_params=_params(("parallel",)),
    )(p_idx, shard)


class _Job:
    def __init__(self, ins, outs, aliases, n_sems, make):
        self.ins, self.outs, self.aliases, self.n_sems, self.make = list(ins), list(outs), list(aliases), n_sems, make


def _merge_jobs(jobs):
    ins, outs, aliases, offs = [], [], [], []
    n_sems = 0
    for jb in jobs:
        offs.append((len(ins), len(outs), n_sems))
        aliases += [(len(ins) + a, len(outs) + b) for a, b in jb.aliases]
        ins += jb.ins
        outs += jb.outs
        n_sems += jb.n_sems

    def make(in_refs, out_refs, send_sems, recv_sems, base=0):
        made = []
        for jb, (oi, oo, os_) in zip(jobs, offs):
            made.append(jb.make(in_refs[oi:oi + len(jb.ins)], out_refs[oo:oo + len(jb.outs)],
                                send_sems, recv_sems, base + os_))

        def start():
            for st, _ in made:
                st()

        def finish():
            for _, fin in made:
                fin()

        return start, finish

    return _Job(ins, outs, aliases, n_sems, make)


def _job_gather(fulls, rel=(0, 1, 2), fresh=False):
    nw = len(fulls)

    def make(in_refs, outs, send_sems, recv_sems, base=0):
        x, y, c = _axes()
        p = 2 * x + y
        chips = _other_chips(x, y)
        srcs = in_refs if fresh else outs

        def rows(w, mine):
            kh = outs[w].shape[1] // 2
            return pl.ds((c if mine else 1 - c) * kh, kh)

        def over_ici(w, j, block):
            qx, qy = chips[j]
            return pltpu.make_async_remote_copy(
                src_ref=srcs[w].at[block, rows(w, True)], dst_ref=outs[w].at[block, rows(w, True)],
                send_sem=send_sems.at[base + 6 * w + j], recv_sem=recv_sems.at[base + 6 * w + j],
                device_id=(qx, qy, c), device_id_type=MESH)

        def over_d2d(w, j, mine):
            qx, qy = chips[j]
            return pltpu.make_async_remote_copy(
                src_ref=outs[w].at[2 * qx + qy, rows(w, mine)], dst_ref=outs[w].at[2 * qx + qy, rows(w, mine)],
                send_sem=send_sems.at[base + 6 * w + 3 + j], recv_sem=recv_sems.at[base + 6 * w + 3 + j],
                device_id=(x, y, 1 - c), device_id_type=MESH)

        def start():
            for w in range(nw):
                for j in rel:
                    over_ici(w, j, p).start()

        def finish():
            for w in range(nw):
                for j in rel:
                    qx, qy = chips[j]
                    over_ici(w, j, 2 * qx + qy).wait_recv()
                    over_d2d(w, j, True).start()
            for w in range(nw):
                for j in rel:
                    over_d2d(w, j, False).wait_recv()
            for w in range(nw):
                for j in rel:
                    over_ici(w, j, p).wait_send()
                    over_d2d(w, j, True).wait_send()

        return start, finish

    return _Job(fulls, [jax.ShapeDtypeStruct(f.shape, f.dtype) for f in fulls],
                [] if fresh else [(w, w) for w in range(nw)], 6 * nw, make)


def _job_simple(ins, outs, n_per, copies_of):
    nw = len(ins)

    def make(in_refs, out_refs, send_sems, recv_sems, base=0):
        x, y, c = _axes()
        copies = []
        for w in range(nw):
            for j, (src, dst, dev) in enumerate(copies_of(w, in_refs[w], out_refs[w], x, y, c)):
                k = base + n_per * w + j
                copies.append(pltpu.make_async_remote_copy(
                    src_ref=src, dst_ref=dst, send_sem=send_sems.at[k], recv_sem=recv_sems.at[k],
                    device_id=dev, device_id_type=MESH))

        def start():
            for cp in copies:
                cp.start()

        def finish():
            for cp in copies:
                cp.wait()

        return start, finish

    return _Job(ins, outs, [], n_per * nw, make)


def _job_swap_halves(grads):
    def copies_of(w, src, dst, x, y, c):
        kh = src.shape[1] // 2
        return [(src.at[:, pl.ds((1 - c) * kh, kh), :], dst, (x, y, 1 - c))]

    outs = [jax.ShapeDtypeStruct((g.shape[0], g.shape[1] // 2, g.shape[2]), g.dtype) for g in grads]
    return _job_simple(grads, outs, 1, copies_of)


def _job_scatter_blocks(sums):
    def copies_of(w, src, dst, x, y, c):
        return [(src.at[2 * qx + qy], dst.at[j], (qx, qy, c)) for j, (qx, qy) in enumerate(_other_chips(x, y))]

    outs = [jax.ShapeDtypeStruct((3,) + s.shape[1:], s.dtype) for s in sums]
    return _job_simple(sums, outs, 3, copies_of)


def _job_to_sibling(arrs):
    def copies_of(w, src, dst, x, y, c):
        return [(src, dst, (x, y, 1 - c))]

    outs = [jax.ShapeDtypeStruct(a.shape, a.dtype) for a in arrs]
    return _job_simple(arrs, outs, 1, copies_of)


def _job_allgather8(buf):
    def make(in_refs, out_refs, send_sems, recv_sems, base=0):
        x, y, c = _axes()
        me = 4 * x + 2 * y + c
        src, dst = in_refs[0], out_refs[0]
        peers = [(1 - x if k & 4 else x, 1 - y if k & 2 else y, 1 - c if k & 1 else c) for k in range(1, N_DEV)]
        mine = pltpu.make_async_copy(src, dst.at[me], send_sems.at[base + N_DEV - 1])

        def to(k):
            return pltpu.make_async_remote_copy(
                src_ref=src, dst_ref=dst.at[me], send_sem=send_sems.at[base + k], recv_sem=recv_sems.at[base + k],
                device_id=peers[k], device_id_type=MESH)

        def of(k):
            px, py, pc = peers[k]
            return pltpu.make_async_remote_copy(
                src_ref=src, dst_ref=dst.at[4 * px + 2 * py + pc], send_sem=send_sems.at[base + k],
                recv_sem=recv_sems.at[base + k], device_id=peers[k], device_id_type=MESH)

        def start():
            mine.start()
            for k in range(N_DEV - 1):
                to(k).start()

        def finish():
            for k in range(N_DEV - 1):
                of(k).wait_recv()
                to(k).wait_send()
            mine.wait()

        return start, finish

    return _Job([buf], [jax.ShapeDtypeStruct((N_DEV,) + buf.shape, buf.dtype)], [], N_DEV, make)


def _run_job(job, name):
    ni, no = len(job.ins), len(job.outs)

    def body(*refs):
        start, finish = job.make(refs[:ni], refs[ni:ni + no], refs[-2], refs[-1])
        start()
        finish()

    any_spec = pl.BlockSpec(memory_space=pl.ANY)
    return pl.pallas_call(
        body, name=name, out_shape=job.outs, in_specs=[any_spec] * ni, out_specs=[any_spec] * no,
        input_output_aliases=dict(job.aliases),
        scratch_shapes=[pltpu.SemaphoreType.DMA((job.n_sems,)), pltpu.SemaphoreType.DMA((job.n_sems,))],
    )(*job.ins)


def _call(body, name, grid, in_specs, out_specs, out_shape, scratch_shapes, semantics, args, aliases=None, job=None,
          prefetch=()):
    in_specs, out_specs, out_shape = list(in_specs), list(out_specs), list(out_shape)
    scratch_shapes = list(scratch_shapes)
    n_pre, n_in, n_out, n_scr = len(prefetch), len(args), len(out_shape), len(scratch_shapes)
    all_aliases = {n_pre + a: b for a, b in (aliases or {}).items()}
    if job is None:
        wrapped, extra_in, semantics = body, [], semantics
    else:
        ni, no = len(job.ins), len(job.outs)

        def wrapped(*refs):
            pre, refs = refs[:n_pre], refs[n_pre:]
            ins, cins = refs[:n_in], refs[n_in:n_in + ni]
            outs, couts = refs[n_in + ni:n_in + ni + n_out], refs[n_in + ni + n_out:n_in + ni + n_out + no]
            scr = refs[n_in + ni + n_out + no:n_in + ni + n_out + no + n_scr]
            start, finish = job.make(cins, couts, refs[-2], refs[-1])
            first = functools.reduce(jnp.logical_and, [pl.program_id(a) == 0 for a in range(len(grid))])
            last = functools.reduce(jnp.logical_and, [pl.program_id(a) == grid[a] - 1 for a in range(len(grid))])
            pl.when(first)(start)
            body(*pre, *ins, *outs, *scr)
            pl.when(last)(finish)

        any_spec = pl.BlockSpec(memory_space=pl.ANY)
        for a, b in job.aliases:
            all_aliases[n_pre + n_in + a] = n_out + b
        in_specs, out_specs, out_shape = in_specs + [any_spec] * ni, out_specs + [any_spec] * no, out_shape + job.outs
        scratch_shapes = scratch_shapes + [pltpu.SemaphoreType.DMA((job.n_sems,)), pltpu.SemaphoreType.DMA((job.n_sems,))]
        extra_in, semantics = job.ins, tuple("arbitrary" for _ in grid)
    if n_pre:
        return pl.pallas_call(
            wrapped, name=name,
            grid_spec=pltpu.PrefetchScalarGridSpec(num_scalar_prefetch=n_pre, grid=grid, in_specs=in_specs,
                                                   out_specs=out_specs, scratch_shapes=scratch_shapes),
            out_shape=out_shape, input_output_aliases=all_aliases,
            compiler_params=_params(semantics))(*prefetch, *args, *extra_in)
    return pl.pallas_call(
        wrapped, name=name, grid=grid, in_specs=in_specs, out_specs=out_specs, out_shape=out_shape,
        scratch_shapes=scratch_shapes, input_output_aliases=all_aliases,
        compiler_params=_params(semantics))(*args, *extra_in)


def _add_own_half(g, r, pc_idx, name):
    nb, K, n = g.shape
    kh = K // 2
    tr = _pick(kh, max(SUBLANES, (1 << 19) // n), 2 * SUBLANES)
    per = kh // tr

    def body(pc_ref, g_ref, r_ref, o_ref):
        o_ref[...] = (g_ref[...] + r_ref[...]).astype(BF16)

    def other(b, pc):
        return jnp.bitwise_xor(pc[0], b + 1)

    return pl.pallas_call(
        body, name=name,
        grid_spec=pltpu.PrefetchScalarGridSpec(
            num_scalar_prefetch=1, grid=(nb - 1, per),
            in_specs=[pl.BlockSpec((None, tr, n), lambda b, i, pc: (other(b, pc), pc[1] * per + i, 0)),
                      pl.BlockSpec((None, tr, n), lambda b, i, pc: (other(b, pc), i, 0))],
            out_specs=pl.BlockSpec((None, tr, n), lambda b, i, pc: (other(b, pc), i, 0))),
        out_shape=jax.ShapeDtypeStruct((nb, kh, n), BF16),
        compiler_params=_params(("parallel", "parallel")),
    )(pc_idx, g, r)


def _add_chips(g, r1, r2, pc_idx, name):
    _, K, n = g.shape
    kh = K // 2
    tr = _pick(kh, max(SUBLANES, (1 << 19) // n), 2 * SUBLANES)
    per = kh // tr

    def body(pc_ref, g_ref, r1_ref, r2_ref, o_ref):
        own = g_ref[...] + r1_ref[...]
        o_ref[...] = ((own + r2_ref[0].astype(F32)) + r2_ref[1].astype(F32)) + r2_ref[2].astype(F32)

    return pl.pallas_call(
        body, name=name,
        grid_spec=pltpu.PrefetchScalarGridSpec(
            num_scalar_prefetch=1, grid=(per,),
            in_specs=[pl.BlockSpec((None, tr, n), lambda i, pc: (pc[0], pc[1] * per + i, 0)),
                      pl.BlockSpec((None, tr, n), lambda i, pc: (pc[0], i, 0)),
                      pl.BlockSpec((3, tr, n), lambda i, pc: (0, i, 0))],
            out_specs=pl.BlockSpec((tr, n), lambda i, pc: (i, 0))),
        out_shape=jax.ShapeDtypeStruct((kh, n), F32),
        compiler_params=_params(("parallel",)),
    )(pc_idx, g, r1, r2)


def _sum8(g, name):
    _, R, L = g.shape

    def body(g_ref, o_ref):
        acc = g_ref[0]
        for d in range(1, N_DEV):
            acc = acc + g_ref[d]
        o_ref[...] = acc

    return pl.pallas_call(
        body, name=name, out_shape=jax.ShapeDtypeStruct((R, L), F32),
        in_specs=[pl.BlockSpec(memory_space=pltpu.VMEM)], out_specs=pl.BlockSpec(memory_space=pltpu.VMEM),
        compiler_params=_params(),
    )(g)


def _adamw_math(w, gg, m, v):
    nm = ADAM_B1 * m + (1.0 - ADAM_B1) * gg
    nv = ADAM_B2 * v + (1.0 - ADAM_B2) * jnp.square(gg)
    m_hat = nm / (1.0 - ADAM_B1 ** ADAM_STEP)
    v_hat = nv / (1.0 - ADAM_B2 ** ADAM_STEP)
    return -ADAM_LR * (m_hat / (jnp.sqrt(v_hat) + ADAM_EPS) + ADAM_WD * w), nm, nv


def _adamw(w, g, m, v, name):
    R, C = w.shape
    tr = _pick(R, max(SUBLANES, (1 << 18) // C), SUBLANES)

    def body(w_ref, g_ref, m_ref, v_ref, d_ref, nm_ref, nv_ref):
        d_ref[...], nm_ref[...], nv_ref[...] = _adamw_math(w_ref[...], g_ref[...], m_ref[...], v_ref[...])

    spec = pl.BlockSpec((tr, C), lambda i: (i, 0))
    sd = jax.ShapeDtypeStruct((R, C), F32)
    return pl.pallas_call(
        body, name=name, grid=(R // tr,), in_specs=[spec] * 4, out_specs=[spec] * 3, out_shape=[sd] * 3,
        compiler_params=_params(("parallel",)),
    )(w, g, m, v)


def _adamw_halves(w, g_own, g_sib, m, v, c_idx, name):
    K, n = w.shape
    kh = K // 2
    tr = _pick(kh, max(SUBLANES, (1 << 18) // n), SUBLANES)
    per = kh // tr

    def body(c_ref, w_ref, go_ref, gs_ref, m_ref, v_ref, g_ref, d_ref, nm_ref, nv_ref):
        h = pl.program_id(0)

        def step(gg):
            g_ref[...] = gg
            d_ref[...], nm_ref[...], nv_ref[...] = _adamw_math(w_ref[...], gg, m_ref[...], v_ref[...])

        @pl.when(h == 0)
        def _():
            step(go_ref[...])

        @pl.when(h == 1)
        def _():
            step(gs_ref[...])

    full = pl.BlockSpec((tr, n), lambda h, i, c: (((c[0] + h) % 2) * per + i, 0))
    own = pl.BlockSpec((tr, n), lambda h, i, c: (i * (1 - h), 0))
    sib = pl.BlockSpec((tr, n), lambda h, i, c: (i * h, 0))
    sd = jax.ShapeDtypeStruct((K, n), F32)
    return pl.pallas_call(
        body, name=name,
        grid_spec=pltpu.PrefetchScalarGridSpec(
            num_scalar_prefetch=1, grid=(2, per),
            in_specs=[full, own, sib, full, full], out_specs=[full] * 4),
        out_shape=[sd] * 4,
        compiler_params=_params(("arbitrary", "arbitrary")),
    )(c_idx, w, g_own, g_sib, m, v)


def _matmul(name, grid, dims, a, a_spec, b, b_spec, outs, out_specs, acc_shape,
            extras=(), extra_specs=(), epilogue=None, job=None, fill_into=None, prefetch=()):
    nk = grid[2]
    npre = len(prefetch)
    aliases = None
    if fill_into is not None:
        aliases = {2 + len(extras): 0}
        extras = tuple(extras) + (fill_into,)
        extra_specs = tuple(extra_specs) + (pl.BlockSpec(memory_space=pl.ANY),)
    nex = len(extras)
    nout = len(outs)

    def body(*refs):
        a_ref, b_ref, rest = refs[npre], refs[npre + 1], refs[npre + 2:]
        ex, o = rest[:nex], rest[nex:nex + nout]
        part = lax.dot_general(a_ref[...].astype(BF16), b_ref[...].astype(BF16), dims,
                               preferred_element_type=F32)

        def finish(res):
            if epilogue is None:
                o[0][...] = res.astype(o[0].dtype)
            else:
                epilogue(res, ex, o)

        if nk == 1:
            finish(part)
        else:
            acc = o[0] if in_place else rest[-1]
            k = pl.program_id(2)

            @pl.when(k == 0)
            def _():
                acc[...] = part

            @pl.when(k > 0)
            def _():
                acc[...] += part

            if not in_place:
                @pl.when(k == nk - 1)
                def _():
                    finish(acc[...])

    in_place = epilogue is None and nout == 1 and outs[0].dtype == F32
    res = _call(body, name, grid, [a_spec, b_spec] + list(extra_specs), out_specs, outs,
                [] if nk == 1 or in_place else [pltpu.VMEM(acc_shape, F32)], ("parallel", "parallel", "arbitrary"),
                (a, b, *extras), aliases=aliases, job=job, prefetch=prefetch)
    return res if job is None else (res[:nout], res[nout:])


def _first(res, job):
    return res[0] if job is None else (res[0][0], res[1])


def _mm_fwd(name, a, wfull, out_dtype=F32, planes=1, bias=None, job=None):
    S, K = a.shape
    _, _, n = wfull.shape
    N = N_CHIPS * n
    tm = _pick(S, 1024, LANES)
    tn = _pick(n, 1408, LANES)
    per = n // tn
    nj = N // tn
    pj = nj // planes
    grid = (S // tm, nj, 1)
    a_spec = pl.BlockSpec((tm, K), lambda i, j, k: (i, 0))
    b_spec = pl.BlockSpec((None, K, tn), lambda i, j, k: (j // per, 0, j % per))
    o_spec = pl.BlockSpec((None, tm, tn), lambda i, j, k: (j // pj, i, j % pj))
    sd = jax.ShapeDtypeStruct((planes, S, N // planes), out_dtype)
    if bias is not None:
        def epi(res, ex, o):
            o[0][...] = (res + ex[0][...]).astype(o[0].dtype)

        out = _matmul(name, grid, NN, a, a_spec, wfull, b_spec, [sd], [o_spec], (tm, tn), extras=(bias,),
                      extra_specs=(pl.BlockSpec((1, tn), lambda i, j, k: (0, j)),), epilogue=epi, job=job)
    else:
        out = _matmul(name, grid, NN, a, a_spec, wfull, b_spec, [sd], [o_spec], (tm, tn), job=job)
    return _first(out, job)


def _mm_fwd_block(name, a, wfull, bias, p_idx, mask, into=None, job=None):
    S, K = a.shape
    _, _, n = wfull.shape
    N = N_CHIPS * n
    tm = _pick(S, 1024, LANES)
    tn = _pick(n, 1408, LANES)
    per = n // tn
    grid = (S // tm, per, 1)
    a_spec = pl.BlockSpec((tm, K), lambda i, j, k, p: (i, 0))
    b_spec = pl.BlockSpec((None, K, tn), lambda i, j, k, p: (jnp.bitwise_xor(p[0], mask), 0, j))
    o_spec = pl.BlockSpec((tm, tn), lambda i, j, k, p: (i, jnp.bitwise_xor(p[0], mask) * per + j))
    v_spec = pl.BlockSpec((1, tn), lambda i, j, k, p: (0, jnp.bitwise_xor(p[0], mask) * per + j))

    def epi(res, ex, o):
        o[0][...] = res + ex[0][...]

    return _matmul(name, grid, NN, a, a_spec, wfull, b_spec, [jax.ShapeDtypeStruct((S, N), F32)], [o_spec], (tm, tn),
                   extras=(bias,), extra_specs=(v_spec,), epilogue=epi, job=job, fill_into=into, prefetch=(p_idx,))


def _copy_block(dst, src, p_idx, mask, name):
    _, K, n = dst.shape
    tr = _pick(K, max(SUBLANES, (1 << 19) // n), 2 * SUBLANES)

    def body(p_ref, s_ref, d_any, o_ref):
        o_ref[...] = s_ref[...]

    spec = pl.BlockSpec((None, tr, n), lambda i, p: (jnp.bitwise_xor(p[0], mask), i, 0))
    return _call(body, name, (K // tr,), [spec, pl.BlockSpec(memory_space=pl.ANY)], [spec],
                 [jax.ShapeDtypeStruct(dst.shape, dst.dtype)], [], ("parallel",), (src, dst), aliases={1: 0},
                 prefetch=(p_idx,))[0]


def _mm_fwd_rows(name, a, wrows, resid, gate, job=None):
    S, K = a.shape
    _, N = wrows.shape
    whole = K * N * 2 <= WHOLE_WEIGHT_BYTES
    tm = _pick(S, 512 if whole else 1024, LANES)
    tk = K if whole else _pick(K, 2816, LANES)
    tn = N if whole else _pick(N, 1024, LANES)
    grid = (S // tm, N // tn, K // tk)
    a_spec = pl.BlockSpec((tm, tk), lambda i, j, k: (i, k))
    if tn == N and tk == K:
        b_spec = pl.BlockSpec((tk, tn), lambda i, j, k: (k, j), pipeline_mode=pl.Buffered(1))
    else:
        b_spec = pl.BlockSpec((tk, tn), lambda i, j, k: (k, j))
    o_spec = pl.BlockSpec((tm, tn), lambda i, j, k: (i, j))
    g_spec = pl.BlockSpec((1, tn), lambda i, j, k: (0, j))

    def epi(res, ex, o):
        o[0][...] = res.astype(BF16)
        o[1][...] = ex[0][...] + ex[1][...] * res

    return _matmul(name, grid, NN, a, a_spec, wrows, b_spec,
                   [jax.ShapeDtypeStruct((S, N), BF16), jax.ShapeDtypeStruct((S, N), F32)], [o_spec, o_spec], (tm, tn),
                   extras=(resid, gate), extra_specs=(o_spec, g_spec), epilogue=epi, job=job)


def _mm_dgrad_cols(name, dy, wfull, out_dtype=F32, job=None, row_tiles=None, fill_into=None):
    planes, S, npl = dy.shape
    _, K, n = wfull.shape
    tm = _pick(S if row_tiles is None else S // 2, 1024, LANES)
    to = _pick(K, 1024, LANES)
    tk = _pick(n, 2816, LANES)
    per = n // tk
    nk = N_CHIPS * per
    pk = nk // planes
    n_tiles = S // tm
    i0, ni = (0, n_tiles) if row_tiles is None else (int(row_tiles[0] * n_tiles), int(row_tiles[1] * n_tiles))
    grid = (ni, K // to, nk)
    a_spec = pl.BlockSpec((None, tm, tk), lambda i, j, k: (k // pk, i + i0, k % pk))
    b_spec = pl.BlockSpec((None, to, tk), lambda i, j, k: (k // per, j, k % per))
    o_spec = pl.BlockSpec((tm, to), lambda i, j, k: (i + i0, j))
    return _first(_matmul(name, grid, NT, dy, a_spec, wfull, b_spec, [jax.ShapeDtypeStruct((S, K), out_dtype)],
                          [o_spec], (tm, to), job=job, fill_into=fill_into), job)


def _mm_dgrad_rows(name, dy, wrows, out_dtype=F32, job=None):
    S, N = dy.shape
    K, _ = wrows.shape
    tm = _pick(S, 1024, LANES)
    to = _pick(K, 1408, LANES)
    tk = _pick(N, 2048, LANES)
    grid = (S // tm, K // to, N // tk)
    a_spec = pl.BlockSpec((tm, tk), lambda i, j, k: (i, k))
    b_spec = pl.BlockSpec((to, tk), lambda i, j, k: (j, k))
    o_spec = pl.BlockSpec((tm, to), lambda i, j, k: (i, j))
    return _first(_matmul(name, grid, NT, dy, a_spec, wrows, b_spec, [jax.ShapeDtypeStruct((S, K), out_dtype)],
                          [o_spec], (tm, to), job=job), job)


def _mm_wgrad_cols(name, a, dy, n, job=None):
    S, K = a.shape
    planes, _, npl = dy.shape
    N = planes * npl
    to = _pick(K, 1024, LANES)
    tn = _pick(n, 1408, LANES)
    ts = _pick(S, 2048, LANES)
    per = n // tn
    nj = N // tn
    pj = nj // planes
    grid = (K // to, nj, S // ts)
    a_spec = pl.BlockSpec((ts, to), lambda i, j, k: (k, i))
    b_spec = pl.BlockSpec((None, ts, tn), lambda i, j, k: (j // pj, k, j % pj))
    o_spec = pl.BlockSpec((None, to, tn), lambda i, j, k: (j // per, i, j % per))
    return _first(_matmul(name, grid, TN, a, a_spec, dy, b_spec, [jax.ShapeDtypeStruct((N_CHIPS, K, n), F32)],
                          [o_spec], (to, tn), job=job), job)


def _mm_wgrad_rows(name, a, dy, kshard, job=None):
    S, K = a.shape
    _, N = dy.shape
    to = _pick(kshard, 1408, LANES)
    tn = _pick(N, 1024, LANES)
    ts = _pick(S, 2048, LANES)
    per = kshard // to
    grid = (K // to, N // tn, S // ts)
    a_spec = pl.BlockSpec((ts, to), lambda i, j, k: (k, i))
    b_spec = pl.BlockSpec((ts, tn), lambda i, j, k: (k, j))
    o_spec = pl.BlockSpec((None, to, tn), lambda i, j, k: (i // per, i % per, j))
    return _first(_matmul(name, grid, TN, a, a_spec, dy, b_spec, [jax.ShapeDtypeStruct((N_CHIPS, kshard, N), F32)],
                          [o_spec], (to, tn), job=job), job)


def _rows(tm, width, colblk=0):
    return pl.BlockSpec((tm, width), lambda i: (i, colblk))


def _whole(shape):
    nd = len(shape)
    return pl.BlockSpec(shape, lambda i: (0,) * nd)


def _prev_halo(tm, h, width, colblk=0):
    r = tm // h
    return pl.BlockSpec((h, width), lambda i: (jnp.maximum(i * r - 1, 0), colblk))


def _next_halo(tm, h, width, nblk, colblk=0):
    r = tm // h
    return pl.BlockSpec((h, width), lambda i: (jnp.minimum((i + 1) * r, nblk - 1), colblk))


def _accumulate(i, ref, val):
    @pl.when(i == 0)
    def _():
        ref[...] = val

    @pl.when(i > 0)
    def _():
        ref[...] += val


def _fill_rotations(rot, offs):
    n = rot.shape[1]
    for r in sorted({o % SUBLANES for o in offs} - {0}):
        rot[r, 0:n - SUBLANES, :] = rot[0, r:r + n - SUBLANES, :]


def _tap_windows(rot, offs, row0, rb, lanes):
    by_res = {}
    for k, o in enumerate(offs):
        by_res.setdefault(o % SUBLANES, []).append((o // SUBLANES, k))
    for r, taps in by_res.items():
        lo = min(a for a, _ in taps)
        hi = max(a for a, _ in taps)
        win = rot[r, pl.ds(row0 + SUBLANES * lo, rb + SUBLANES * (hi - lo)), lanes]
        for a, k in taps:
            yield k, win[SUBLANES * (a - lo):SUBLANES * (a - lo) + rb, :]


def _for_chunks(n_rows, rb, fn):
    def step(j, carry):
        fn(pl.multiple_of(j * rb, rb))
        return carry

    lax.fori_loop(0, n_rows // rb, step, 0)


def _modnorm_fwd(name, x, g, scale, shift):
    S, D = x.shape
    tm = _pick(S, 512, LANES)

    def body(x_ref, g_ref, sc_ref, sh_ref, h_ref):
        h_ref[...] = _modnorm(x_ref[...], g_ref[...], sc_ref[...], sh_ref[...]).astype(BF16)

    vec = _whole((1, D))
    return pl.pallas_call(
        body, name=name, grid=(S // tm,), in_specs=[_rows(tm, D), vec, vec, vec], out_specs=_rows(tm, D),
        out_shape=jax.ShapeDtypeStruct((S, D), BF16), compiler_params=_params(("parallel",)),
    )(x, g, scale, shift)


def _modnorm_bwd(name, x, dh, dx_in, g, scale, shift, gated=None):
    S, D = x.shape
    tm = _pick(S, 256, LANES)

    def body(x_ref, dh_ref, dxin_ref, g_ref, sc_ref, sh_ref, *rest):
        i = pl.program_id(0)
        dx_ref, dg_ref, dsc_ref, dsh_ref = rest[-4:] if gated is None else rest[2:6]
        _, pull = jax.vjp(_modnorm, x_ref[...], g_ref[...], sc_ref[...], sh_ref[...])
        dx, dg, dsc, dsh = pull(dh_ref[...])
        dx = dxin_ref[...] + dx
        dx_ref[...] = dx
        _accumulate(i, dg_ref, dg)
        _accumulate(i, dsc_ref, dsc)
        _accumulate(i, dsh_ref, dsh)
        if gated is not None:
            _gate_bwd_tile(i, dx, rest[0], rest[1], rest[6], rest[7])

    vec = _whole((1, D))
    row = _rows(tm, D)
    vsd = jax.ShapeDtypeStruct((1, D), F32)
    in_specs, args = [row, row, row, vec, vec, vec], (x, dh, dx_in, g, scale, shift)
    out_specs, out_shape = [row, vec, vec, vec], [jax.ShapeDtypeStruct((S, D), F32), vsd, vsd, vsd]
    if gated is not None:
        in_specs, args = in_specs + [row, vec], args + tuple(gated)
        out_specs, out_shape = out_specs + [row, vec], out_shape + [jax.ShapeDtypeStruct((S, D), BF16), vsd]
    return _call(body, name, (S // tm,), in_specs, out_specs, out_shape, [], ("arbitrary",), args)


def _conv_fwd(name, proj, w, b, lg, lb, dc, job=None):
    S = proj.shape[0]
    K = w.shape[0]
    H = CONV_HALO
    tm = _pick(S, 256, LANES)

    def glu(v):
        return v[:, :dc] * jax.nn.sigmoid(v[:, dc:])

    offs = [H - (K - 1) + k for k in range(K)]
    rb = _pick(tm, CONV_CHUNK_ROWS, SUBLANES)
    lw = min(LANES, dc)

    def body(cur_ref, prev_ref, w_ref, b_ref, lg_ref, lb_ref, conv_ref, act_ref, rot):
        i = pl.program_id(0)
        rot[0, 0:H, :] = jnp.where(i > 0, glu(prev_ref[...]), 0.0)
        rot[0, H:, :] = glu(cur_ref[...])
        _fill_rotations(rot, offs)

        def chunk(row0):
            for l0 in range(0, dc, lw):
                lanes = slice(l0, l0 + lw)
                acc = jnp.broadcast_to(b_ref[:, lanes], (rb, lw))
                for k, win in _tap_windows(rot, offs, row0, rb, lanes):
                    acc = acc + w_ref[k:k + 1, lanes] * win
                conv_ref[pl.ds(row0, rb), lanes] = acc

        _for_chunks(tm, rb, chunk)
        act_ref[...] = _ln_silu(conv_ref[...], lg_ref[...], lb_ref[...]).astype(BF16)

    vec = _whole((1, dc))
    res = _call(body, name, (S // tm,),
                [_rows(tm, 2 * dc), _prev_halo(tm, H, 2 * dc), _whole(w.shape), vec, vec, vec],
                [_rows(tm, dc), _rows(tm, dc)],
                [jax.ShapeDtypeStruct((S, dc), F32), jax.ShapeDtypeStruct((S, dc), BF16)],
                [pltpu.VMEM((SUBLANES, tm + H, dc), F32)], ("parallel",), (proj, proj, w, b, lg, lb), job=job)
    return res if job is None else (res[:2], res[2:])


def _conv_bwd_ln(name, conv, dact, lg, lb):
    S, dc = conv.shape
    tm = _pick(S, 256, LANES)

    def body(c_ref, d_ref, lg_ref, lb_ref, dc_ref, dlg_ref, dlb_ref, db_ref):
        i = pl.program_id(0)
        _, pull = jax.vjp(_ln_silu, c_ref[...], lg_ref[...], lb_ref[...])
        dcv, dlg, dlb = pull(d_ref[...])
        dc_ref[...] = dcv
        _accumulate(i, dlg_ref, dlg)
        _accumulate(i, dlb_ref, dlb)
        _accumulate(i, db_ref, jnp.sum(dcv, axis=0, keepdims=True))

    vec = _whole((1, dc))
    row = _rows(tm, dc)
    vsd = jax.ShapeDtypeStruct((1, dc), F32)
    return pl.pallas_call(
        body, name=name, grid=(S // tm,), in_specs=[row, row, vec, vec], out_specs=[row, vec, vec, vec],
        out_shape=[jax.ShapeDtypeStruct((S, dc), F32), vsd, vsd, vsd],
        compiler_params=_params(("arbitrary",)),
    )(conv, dact, lg, lb)


def _conv_bwd(name, dconv, proj, w, dproj, dc, job=None):
    S = proj.shape[0]
    K = w.shape[0]
    H = CONV_HALO
    tm = _pick(S, 256, LANES)
    nt = S // tm

    offs_g = [H - (K - 1) + k for k in range(K)]
    offs_d = [K - 1 - k for k in range(K)]
    rb = _pick(tm, CONV_CHUNK_ROWS, SUBLANES)
    lw = min(LANES, dc)
    kp = -(-K // SUBLANES) * SUBLANES

    def body(d_ref, dn_ref, cur_ref, prev_ref, w_ref, dproj_any, da_ref, dw_ref, dbin_ref, rotg, rotd, accw, dglu_s):
        i = pl.program_id(0)
        pv = prev_ref[...]
        cv = cur_ref[...]
        sig = jax.nn.sigmoid(cv[:, dc:])
        rotg[0, 0:H, :] = jnp.where(i > 0, pv[:, :dc] * jax.nn.sigmoid(pv[:, dc:]), 0.0)
        rotg[0, H:, :] = cv[:, :dc] * sig
        rotd[0, 0:tm, :] = d_ref[...]
        rotd[0, tm:, :] = jnp.where(i < nt - 1, dn_ref[...], 0.0)
        _fill_rotations(rotg, offs_g)
        _fill_rotations(rotd, offs_d)

        @pl.when(i == 0)
        def _():
            accw[...] = jnp.zeros_like(accw)

        for l0 in range(0, dc, lw):
            lanes = slice(l0, l0 + lw)

            def chunk(j, sums, lanes=lanes):
                row0 = pl.multiple_of(j * rb, rb)
                dcur = rotd[0, pl.ds(row0, rb), lanes]
                acc = jnp.zeros((rb, lw), F32)
                for k, win in _tap_windows(rotd, offs_d, row0, rb, lanes):
                    acc = acc + w_ref[k:k + 1, lanes] * win
                new = list(sums)
                for k, win in _tap_windows(rotg, offs_g, row0, rb, lanes):
                    new[k] = sums[k] + jnp.sum((dcur * win).reshape(rb // SUBLANES, SUBLANES, lw), axis=0)
                dglu_s[pl.ds(row0, rb), lanes] = acc
                return tuple(new)

            sums = lax.fori_loop(0, tm // rb, chunk, tuple(jnp.zeros((SUBLANES, lw), F32) for _ in range(K)))
            for k in range(K):
                accw[SUBLANES * k:SUBLANES * (k + 1), lanes] += sums[k]
        dglu = dglu_s[...]
        da = jnp.concatenate([dglu * sig, dglu * cv[:, :dc] * sig * (1.0 - sig)], axis=1)
        da_ref[...] = da.astype(BF16)
        _accumulate(i, dbin_ref, jnp.sum(da, axis=0, keepdims=True))

        @pl.when(i == nt - 1)
        def _():
            dw_ref[...] = jnp.zeros_like(dw_ref)
            for k in range(K):
                dw_ref[k:k + 1, :] = jnp.sum(accw[SUBLANES * k:SUBLANES * (k + 1), :], axis=0, keepdims=True)

    res = _call(
        body, name, (nt,),
        [_rows(tm, dc), _next_halo(tm, H, dc, S // H), _rows(tm, 2 * dc), _prev_halo(tm, H, 2 * dc),
         _whole(w.shape), pl.BlockSpec(memory_space=pl.ANY)],
        [_rows(tm, 2 * dc), _whole((kp, dc)), _whole((1, 2 * dc))],
        [jax.ShapeDtypeStruct(dproj.shape, BF16), jax.ShapeDtypeStruct((kp, dc), F32),
         jax.ShapeDtypeStruct((1, 2 * dc), F32)],
        [pltpu.VMEM((SUBLANES, tm + H, dc), F32), pltpu.VMEM((SUBLANES, tm + H, dc), F32),
         pltpu.VMEM((SUBLANES * K, dc), F32), pltpu.VMEM((tm, dc), F32)],
        ("arbitrary",), (dconv, dconv, proj, proj, w, dproj), aliases={5: 0}, job=job)
    return res if job is None else (res[:3], res[3:])


def _mix(vln, wsm, bst, out_ref, G, CH, hd):
    for n in range(vln.shape[0] // CH):
        for g in range(G):
            blk = vln[n * CH:(n + 1) * CH, g * hd:(g + 1) * hd].astype(BF16)
            out_ref[n * CH:(n + 1) * CH, g * hd:(g + 1) * hd] = (
                jnp.dot(wsm[g], blk, preferred_element_type=F32) + bst[:, g:g + 1])


def _blocks_to_columns(full, name):
    nb, K, n = full.shape
    tr = _pick(K, max(SUBLANES, (1 << 19) // n), 2 * SUBLANES)

    def body(s_ref, o_ref):
        o_ref[...] = s_ref[...]

    return pl.pallas_call(
        body, name=name, grid=(nb, K // tr),
        in_specs=[pl.BlockSpec((None, tr, n), lambda p, i: (p, i, 0))],
        out_specs=pl.BlockSpec((tr, n), lambda p, i: (i, p)),
        out_shape=jax.ShapeDtypeStruct((K, nb * n), full.dtype),
        compiler_params=_params(("parallel", "parallel")))(full)


def _sgu_fwd(name, proj, a_act, wco, wso, lg, lb, ws, bst, D, ds, job=None):
    S = proj.shape[0]
    dc = a_act.shape[1]
    G, CH, _ = ws.shape
    hd = ds // G
    tm = _pick(S, 256, CH)

    def body(s_ref, gt_ref, a_ref, wco_ref, wso_ref, lg_ref, lb_ref, ws_ref, bst_ref,
             ya_ref, yb_ref, uv_ref, mg_ref, vmix):
        z = _gelu(s_ref[...])
        vln = _layer_norm(z[:, ds:], lg_ref[...], lb_ref[...])
        wsm = [_tril_mask(ws_ref[g]).astype(BF16) for g in range(G)]
        _mix(vln, wsm, bst_ref[...], vmix, G, CH, hd)
        uv = (z[:, :ds] * vmix[...]).astype(BF16)
        uv_ref[...] = uv
        ya = jnp.dot(a_ref[...], wco_ref[...], preferred_element_type=F32)
        yb = jnp.dot(uv, wso_ref[...], preferred_element_type=F32)
        ya_ref[...] = ya.astype(BF16)
        yb_ref[...] = yb.astype(BF16)
        gt = gt_ref[...]
        mg_ref[...] = (jax.nn.sigmoid(gt[:, :D]) * ya + jax.nn.sigmoid(gt[:, D:]) * yb).astype(BF16)

    vec = _whole((1, ds))
    sdb = jax.ShapeDtypeStruct((S, D), BF16)
    res = _call(body, name, (S // tm,),
                [_rows(tm, 2 * ds, 1), _rows(tm, 2 * D, 1), _rows(tm, dc), _whole(wco.shape), _whole(wso.shape),
                 vec, vec, _whole(ws.shape), _whole(bst.shape)],
                [_rows(tm, D), _rows(tm, D), _rows(tm, ds), _rows(tm, D)],
                [sdb, sdb, jax.ShapeDtypeStruct((S, ds), BF16), sdb],
                [pltpu.VMEM((tm, ds), F32)], ("parallel",), (proj, proj, a_act, wco, wso, lg, lb, ws, bst), job=job)
    return res if job is None else (res[:4], res[4:])


def _merge_bwd(name, dmerged, proj, ya, yb, D):
    S = proj.shape[0]
    tm = _pick(S, 256, LANES)

    def body(dm_ref, gt_ref, ya_ref, yb_ref, dya_ref, dyb_ref, dg_ref, dbin_ref):
        i = pl.program_id(0)
        dm = dm_ref[...]
        gt = gt_ref[...]
        sa = jax.nn.sigmoid(gt[:, :D])
        sb = jax.nn.sigmoid(gt[:, D:])
        dya_ref[...] = (dm * sa).astype(BF16)
        dyb_ref[...] = (dm * sb).astype(BF16)
        ya = ya_ref[...].astype(F32)
        yb = yb_ref[...].astype(F32)
        dg = jnp.concatenate([dm * ya * sa * (1.0 - sa), dm * yb * sb * (1.0 - sb)], axis=1)
        dg_ref[...] = dg.astype(BF16)
        _accumulate(i, dbin_ref, jnp.sum(dg, axis=0, keepdims=True))

    row = _rows(tm, D)
    sdb = jax.ShapeDtypeStruct((S, D), BF16)
    return pl.pallas_call(
        body, name=name, grid=(S // tm,),
        in_specs=[row, _rows(tm, 2 * D, 1), row, row],
        out_specs=[row, row, _rows(tm, 2 * D, 1), _whole((1, 2 * D))],
        out_shape=[sdb, sdb, jax.ShapeDtypeStruct((S, 4 * D), BF16), jax.ShapeDtypeStruct((1, 2 * D), F32)],
        compiler_params=_params(("arbitrary",)),
    )(dmerged, proj, ya, yb)


def _sgu_bwd(name, proj, duv, lg, lb, ws, bst, dproj, ds):
    S = proj.shape[0]
    G, CH, _ = ws.shape
    hd = ds // G
    tm = _pick(S, 256, CH)

    def body(s_ref, duv_ref, lg_ref, lb_ref, ws_ref, bst_ref, dproj_any,
             dsin_ref, dws_ref, dbs_ref, dlg_ref, dlb_ref, dbin_ref, vmix, dvln):
        i = pl.program_id(0)
        z, pull_gelu = jax.vjp(_gelu, s_ref[...])
        u = z[:, :ds]
        vln, pull_ln = jax.vjp(_layer_norm, z[:, ds:], lg_ref[...], lb_ref[...])
        wsm = [_tril_mask(ws_ref[g]).astype(BF16) for g in range(G)]
        _mix(vln, wsm, bst_ref[...], vmix, G, CH, hd)
        duv = duv_ref[...]
        du = duv * vmix[...]
        dvmix = duv * u
        for g in range(G):
            dws_g = jnp.zeros((CH, CH), F32)
            dbs_g = jnp.zeros((CH, 1), F32)
            for n in range(tm // CH):
                dblk = dvmix[n * CH:(n + 1) * CH, g * hd:(g + 1) * hd]
                vblk = vln[n * CH:(n + 1) * CH, g * hd:(g + 1) * hd].astype(BF16)
                dvln[n * CH:(n + 1) * CH, g * hd:(g + 1) * hd] = lax.dot_general(
                    wsm[g], dblk.astype(BF16), TN, preferred_element_type=F32)
                dws_g = dws_g + lax.dot_general(dblk.astype(BF16), vblk, NT, preferred_element_type=F32)
                dbs_g = dbs_g + jnp.sum(dblk, axis=1, keepdims=True)
            dws_g = _tril_mask(dws_g)
            dbs_g = jnp.broadcast_to(dbs_g, (CH, LANES))

            @pl.when(i == 0)
            def _():
                dws_ref[g] = dws_g
                dbs_ref[g] = dbs_g

            @pl.when(i > 0)
            def _():
                dws_ref[g] += dws_g
                dbs_ref[g] += dbs_g

        dv, dlg, dlb = pull_ln(dvln[...])
        (dsin,) = pull_gelu(jnp.concatenate([du, dv], axis=1))
        dsin_ref[...] = dsin.astype(BF16)
        _accumulate(i, dlg_ref, dlg)
        _accumulate(i, dlb_ref, dlb)
        _accumulate(i, dbin_ref, jnp.sum(dsin, axis=0, keepdims=True))

    vec = _whole((1, ds))
    vsd = jax.ShapeDtypeStruct((1, ds), F32)
    return pl.pallas_call(
        body, name=name, grid=(S // tm,),
        in_specs=[_rows(tm, 2 * ds, 1), _rows(tm, ds), vec, vec, _whole(ws.shape), _whole(bst.shape),
                  pl.BlockSpec(memory_space=pl.ANY)],
        out_specs=[_rows(tm, 2 * ds, 1), _whole((G, CH, CH)), _whole((G, CH, LANES)), vec, vec, _whole((1, 2 * ds))],
        out_shape=[jax.ShapeDtypeStruct(dproj.shape, BF16), jax.ShapeDtypeStruct((G, CH, CH), F32),
                   jax.ShapeDtypeStruct((G, CH, LANES), F32), vsd, vsd, jax.ShapeDtypeStruct((1, 2 * ds), F32)],
        scratch_shapes=[pltpu.VMEM((tm, ds), F32), pltpu.VMEM((tm, ds), F32)],
        input_output_aliases={6: 0},
        compiler_params=_params(("arbitrary",)),
    )(proj, duv, lg, lb, ws, bst, dproj)


def _silu_mul(val, gt):
    return jax.nn.silu(gt) * val


def _rotation_slots(offs):
    slot = {0: 0}
    for r in sorted({o % SUBLANES for o in offs} - {0}):
        slot[r] = len(slot)
    return slot


def _fill_plane_rotations(rot, slot):
    n = rot.shape[2]
    for r, s in slot.items():
        if r:
            rot[:, s, 0:n - SUBLANES, :] = rot[:, 0, r:r + n - SUBLANES, :]


def _ffn_tiles(S, Fh, rows=256):
    return _pick(S, rows, LANES), _pick(Fh, 1408, LANES)


def _ffn_fwd(name, up, w, b):
    _, S, Fh = up.shape
    K = w.shape[1]
    H = FFN_HALO
    tm, cw = _ffn_tiles(S, Fh, rows=512)
    r = tm // H

    offs = [H - (K - 1) + k for k in range(K)]
    slot = _rotation_slots(offs)
    rb = _pick(tm, 64, 2 * SUBLANES)
    lw = min(LANES, cw)

    def body(cur_ref, prev_ref, w_ref, b_ref, act_ref, rot):
        i = pl.program_id(1)
        rot[:, 0, 0:H, :] = jnp.where(i > 0, prev_ref[...], 0.0)
        rot[:, 0, H:, :] = cur_ref[...]
        _fill_plane_rotations(rot, slot)

        def chunk(row0):
            for l0 in range(0, cw, lw):
                lanes = slice(l0, l0 + lw)
                c2 = []
                for pln in range(2):
                    acc = jnp.broadcast_to(b_ref[pln, :, lanes], (rb, lw))
                    for k in range(K):
                        a, rr = divmod(offs[k], SUBLANES)
                        acc = acc + w_ref[pln, k:k + 1, lanes] * rot[pln, slot[rr], pl.ds(row0 + SUBLANES * a, rb), lanes]
                    c2.append(acc)
                act_ref[pl.ds(row0, rb), lanes] = _silu_mul(c2[0], c2[1]).astype(BF16)

        _for_chunks(tm, rb, chunk)

    return pl.pallas_call(
        body, name=name, grid=(Fh // cw, S // tm),
        in_specs=[pl.BlockSpec((2, tm, cw), lambda j, i: (0, i, j)),
                  pl.BlockSpec((2, H, cw), lambda j, i: (0, jnp.maximum(i * r - 1, 0), j)),
                  pl.BlockSpec((2, K, cw), lambda j, i: (0, 0, j)),
                  pl.BlockSpec((2, 1, cw), lambda j, i: (0, 0, j))],
        out_specs=pl.BlockSpec((tm, cw), lambda j, i: (i, j)),
        out_shape=jax.ShapeDtypeStruct((S, Fh), BF16),
        scratch_shapes=[pltpu.VMEM((2, len(slot), tm + H, cw), F32)],
        compiler_params=_params(("parallel", "parallel")),
    )(up, up, w, b)


def _ffn_bwd(name, up, dact, w, b, job=None):
    _, S, Fh = up.shape
    K = w.shape[1]
    H = FFN_HALO
    tm, cw = _ffn_tiles(S, Fh)
    r = tm // H
    nt = S // tm
    nhb = S // H
    te = tm + H

    offs_x = [H - (K - 1) + k for k in range(K)]
    offs_d = [K - 1 - k for k in range(K)]
    slot_x = _rotation_slots(offs_x)
    slot_d = _rotation_slots(offs_d)
    rb = _pick(tm, 64, 2 * SUBLANES)
    rbe = _pick(te, 96, SUBLANES)
    lw = min(LANES, cw)

    def body(cur_ref, prev_ref, next_ref, d_ref, dn_ref, w_ref, b_ref, dup_ref, dwb_ref, rotx, dext, rotd, accw):
        i = pl.program_id(1)
        rotx[:, 0, 0:H, :] = jnp.where(i > 0, prev_ref[...], 0.0)
        rotx[:, 0, H:H + tm, :] = cur_ref[...]
        rotx[:, 0, H + tm:, :] = jnp.where(i < nt - 1, next_ref[...], 0.0)
        dext[0:tm, :] = d_ref[...]
        dext[tm:, :] = jnp.where(i < nt - 1, dn_ref[...], 0.0)
        _fill_plane_rotations(rotx, slot_x)

        def chunk_e(row0):
            for l0 in range(0, cw, lw):
                lanes = slice(l0, l0 + lw)
                c2 = []
                for pln in range(2):
                    acc = jnp.broadcast_to(b_ref[pln, :, lanes], (rbe, lw))
                    for k in range(K):
                        a, rr = divmod(offs_x[k], SUBLANES)
                        acc = acc + w_ref[pln, k:k + 1, lanes] * rotx[pln, slot_x[rr], pl.ds(row0 + SUBLANES * a, rbe), lanes]
                    c2.append(acc)
                _, pull = jax.vjp(_silu_mul, c2[0], c2[1])
                dval, dgt = pull(dext[pl.ds(row0, rbe), lanes])
                rotd[0, 0, pl.ds(row0, rbe), lanes] = dval
                rotd[1, 0, pl.ds(row0, rbe), lanes] = dgt

        _for_chunks(te, rbe, chunk_e)
        _fill_plane_rotations(rotd, slot_d)

        @pl.when(i == 0)
        def _():
            accw[...] = jnp.zeros_like(accw)

        def chunk(row0):
            for l0 in range(0, cw, lw):
                lanes = slice(l0, l0 + lw)
                for pln in range(2):
                    dcur = rotd[pln, 0, pl.ds(row0, rb), lanes]
                    dup = jnp.zeros((rb, lw), F32)
                    for k in range(K):
                        a, rr = divmod(offs_d[k], SUBLANES)
                        dup = dup + w_ref[pln, k:k + 1, lanes] * rotd[pln, slot_d[rr], pl.ds(row0 + SUBLANES * a, rb), lanes]
                        a, rr = divmod(offs_x[k], SUBLANES)
                        prod = dcur * rotx[pln, slot_x[rr], pl.ds(row0 + SUBLANES * a, rb), lanes]
                        accw[pln, SUBLANES * k:SUBLANES * (k + 1), lanes] += jnp.sum(
                            prod.reshape(rb // SUBLANES, SUBLANES, lw), axis=0)
                    accw[pln, SUBLANES * K:SUBLANES * (K + 1), lanes] += jnp.sum(
                        dcur.reshape(rb // SUBLANES, SUBLANES, lw), axis=0)
                    dup_ref[pln, pl.ds(row0, rb), lanes] = dup.astype(BF16)

        _for_chunks(tm, rb, chunk)

        @pl.when(i == nt - 1)
        def _():
            dwb_ref[...] = jnp.zeros_like(dwb_ref)
            for pln in range(2):
                for k in range(K + 1):
                    dwb_ref[pln, k:k + 1, :] = jnp.sum(accw[pln, SUBLANES * k:SUBLANES * (k + 1), :], axis=0,
                                                       keepdims=True)

    res = _call(
        body, name, (Fh // cw, nt),
        [pl.BlockSpec((2, tm, cw), lambda j, i: (0, i, j)),
         pl.BlockSpec((2, H, cw), lambda j, i: (0, jnp.maximum(i * r - 1, 0), j)),
         pl.BlockSpec((2, H, cw), lambda j, i: (0, jnp.minimum((i + 1) * r, nhb - 1), j)),
         pl.BlockSpec((tm, cw), lambda j, i: (i, j)),
         pl.BlockSpec((H, cw), lambda j, i: (jnp.minimum((i + 1) * r, nhb - 1), j)),
         pl.BlockSpec((2, K, cw), lambda j, i: (0, 0, j)),
         pl.BlockSpec((2, 1, cw), lambda j, i: (0, 0, j))],
        [pl.BlockSpec((2, tm, cw), lambda j, i: (0, i, j)), pl.BlockSpec((2, SUBLANES, cw), lambda j, i: (0, 0, j))],
        [jax.ShapeDtypeStruct((2, S, Fh), BF16), jax.ShapeDtypeStruct((2, SUBLANES, Fh), F32)],
        [pltpu.VMEM((2, len(slot_x), tm + 2 * H, cw), F32), pltpu.VMEM((te, cw), F32),
         pltpu.VMEM((2, len(slot_d), te, cw), F32), pltpu.VMEM((2, SUBLANES * (K + 1), cw), F32)],
        ("parallel", "arbitrary"), (up, up, up, dact, dact, w, b), job=job)
    return res if job is None else (res[:2], res[2:])


def _rms(x, g):
    return x * lax.rsqrt(jnp.mean(x * x, axis=-1, keepdims=True) + EPS) * g


def _gate_bwd_tile(i, dx, out_ref, gate_ref, dout_ref, dgate_ref):
    dout_ref[...] = (dx * gate_ref[...]).astype(BF16)
    _accumulate(i, dgate_ref, jnp.sum(dx * out_ref[...].astype(F32), axis=0, keepdims=True))


def _final(name, x2, target, gf, out, gate):
    S, D = x2.shape
    tm = _pick(S, 256, LANES)

    def body(x_ref, t_ref, g_ref, o_ref, gate_ref, dx_ref, dg_ref, loss_ref, do_ref, dgate_ref):
        i = pl.program_id(0)
        y, pull = jax.vjp(_rms, x_ref[...], g_ref[...])
        e = y - t_ref[...]
        dx, dg = pull(e / D)
        dx_ref[...] = dx
        _accumulate(i, dg_ref, dg)
        part = 0.5 * jnp.sum(jnp.mean(jnp.square(e), axis=-1, keepdims=True), axis=0, keepdims=True)
        _accumulate(i, loss_ref, jnp.broadcast_to(part, (SUBLANES, LANES)))
        _gate_bwd_tile(i, dx, o_ref, gate_ref, do_ref, dgate_ref)

    row = _rows(tm, D)
    vec = _whole((1, D))
    vsd = jax.ShapeDtypeStruct((1, D), F32)
    return pl.pallas_call(
        body, name=name, grid=(S // tm,), in_specs=[row, row, vec, row, vec],
        out_specs=[row, vec, _whole((SUBLANES, LANES)), row, vec],
        out_shape=[jax.ShapeDtypeStruct((S, D), F32), vsd, jax.ShapeDtypeStruct((SUBLANES, LANES), F32),
                   jax.ShapeDtypeStruct((S, D), BF16), vsd],
        compiler_params=_params(("arbitrary",)),
    )(x2, target, gf, out, gate)


def _ada_fwd(name, c_pad, w_ada, b_cols):
    nb, D = c_pad.shape
    n = w_ada.shape[1]
    tn = _pick(n, 1024, LANES)

    def body(c_ref, w_ref, b_ref, o_ref):
        o_ref[...] = jnp.dot(jax.nn.silu(c_ref[...]).astype(BF16), w_ref[...].astype(BF16),
                             preferred_element_type=F32) + b_ref[...]

    return pl.pallas_call(
        body, name=name, grid=(n // tn,),
        in_specs=[_whole((nb, D)), pl.BlockSpec((D, tn), lambda j: (0, j)), pl.BlockSpec((1, tn), lambda j: (0, j))],
        out_specs=pl.BlockSpec((nb, tn), lambda j: (0, j)),
        out_shape=jax.ShapeDtypeStruct((nb, n), F32), compiler_params=_params(("parallel",)),
    )(c_pad, w_ada, b_cols)


def _ada_wgrad(name, c_t, dmod_cols):
    D, nb = c_t.shape
    n = dmod_cols.shape[1]
    tr = _pick(D, 256, SUBLANES)

    def body(c_ref, d_ref, o_ref):
        ca = jax.nn.silu(c_ref[...])
        acc = ca[:, 0:1] * d_ref[0:1, :]
        for b in range(1, nb):
            acc = acc + ca[:, b:b + 1] * d_ref[b:b + 1, :]
        o_ref[...] = acc

    return pl.pallas_call(
        body, name=name, grid=(D // tr,),
        in_specs=[pl.BlockSpec((tr, nb), lambda i: (i, 0)), _whole((nb, n))],
        out_specs=pl.BlockSpec((tr, n), lambda i: (i, 0)),
        out_shape=jax.ShapeDtypeStruct((D, n), F32), compiler_params=_params(("parallel",)),
    )(c_t, dmod_cols)


def kernel(x, c, w_ada, b_ada, norm1_g, w_in, b_in, conv_dw_w, conv_dw_b, conv_ln_g, conv_ln_b, w_conv_out, sgu_ln_g, sgu_ln_b, w_spatial, b_spatial, w_sgu_out, w_out, norm2_g, w_up, ffn_dw_w, ffn_dw_b, w_down, final_g, loss_target, m_w_ada, m_b_ada, m_norm1_g, m_w_in, m_b_in, m_conv_dw_w, m_conv_dw_b, m_conv_ln_g, m_conv_ln_b, m_w_conv_out, m_sgu_ln_g, m_sgu_ln_b, m_w_spatial, m_b_spatial, m_w_sgu_out, m_w_out, m_norm2_g, m_w_up, m_ffn_dw_w, m_ffn_dw_b, m_w_down, m_final_g, v_w_ada, v_b_ada, v_norm1_g, v_w_in, v_b_in, v_conv_dw_w, v_conv_dw_b, v_conv_ln_g, v_conv_ln_b, v_w_conv_out, v_sgu_ln_g, v_sgu_ln_b, v_w_spatial, v_b_spatial, v_w_sgu_out, v_w_out, v_norm2_g, v_w_up, v_ffn_dw_w, v_ffn_dw_b, v_w_down, v_final_g):
    S, D = x.shape[1], x.shape[2]
    dc = w_conv_out.shape[1]
    ds = w_sgu_out.shape[1]
    G, CH = w_spatial.shape[1], w_spatial.shape[2]
    KC = conv_dw_w.shape[1]
    KF = ffn_dw_w.shape[1]
    F2 = ffn_dw_b.shape[1]
    Fh = F2 // 2
    n_ada = w_ada.shape[2]
    n_up = w_up.shape[2]
    ax, ay, ac = _axes()
    chip = 2 * ax + ay
    me = 2 * chip + ac
    c_idx = jnp.reshape(ac, (1,)).astype(jnp.int32)
    p_idx = jnp.reshape(chip, (1,)).astype(jnp.int32)

    xs = x[0]
    tgt = loss_target[0]

    g1 = _allgather8(_pack([c[0], conv_dw_w[0], ffn_dw_w[0]]), "gather_small_in")
    n_cw, n_fw = conv_dw_w.shape[2], ffn_dw_w.shape[2]
    parts = [_unpack(g1[2 * q], [(D,), conv_dw_w.shape[1:], ffn_dw_w.shape[1:]]) for q in range(N_CHIPS)]
    c_all = jnp.stack([_unpack(g1[d], [(D,)])[0] for d in range(N_DEV)])
    cw_full = jnp.concatenate([pt[1] for pt in parts], axis=1)
    fw_full = jnp.concatenate([pt[2] for pt in parts], axis=1)

    b_cols = lax.dynamic_slice(b_ada, (0, chip * n_ada), (1, n_ada))
    c_pad = jnp.concatenate([c_all, jnp.zeros_like(c_all)], axis=0)
    mod_blk = _ada_fwd("ada_fwd", c_pad, w_ada[0], b_cols)[:N_DEV]
    g2 = _allgather8(_pack([mod_blk]), "gather_mod")
    mod_all = jnp.concatenate([_unpack(g2[2 * q], [(N_DEV, n_ada)])[0] for q in range(N_CHIPS)], axis=1)
    mod = lax.dynamic_slice(mod_all, (me, 0), (1, 6 * D))
    shift1, scale1, gate1, shift2, scale2, gate2 = [mod[:, k * D:(k + 1) * D] for k in range(6)]

    shards = [w_in[0], w_conv_out[0], w_sgu_out[0], w_out[0], w_up[0], w_down[0]]
    names = ["in", "conv_out", "sgu_out", "out", "up", "down"]
    blk = {nm: _cast_into_block(s, p_idx, "cast_" + nm) for s, nm in zip(shards, names)}
    pc_idx = jnp.concatenate([p_idx, c_idx])

    h1 = _modnorm_fwd("modnorm1", xs, norm1_g, scale1, shift1)
    (proj,), (win_xy,) = _mm_fwd_block("proj_own", h1, blk["in"], b_in, p_idx, 0,
                                       job=_job_gather([blk["in"]], rel=(0, 1), fresh=True))
    (proj,), (win_d,) = _mm_fwd_block("proj_x", h1, win_xy, b_in, p_idx, 2, into=proj,
                                      job=_job_gather([blk["in"]], rel=(2,), fresh=True))
    (proj,), (wco_f, wso_f) = _mm_fwd_block("proj_y", h1, win_xy, b_in, p_idx, 1, into=proj,
                                            job=_job_gather([blk["conv_out"], blk["sgu_out"]]))
    (proj,), (wout_f,) = _mm_fwd_block("proj_diag", h1, win_d, b_in, p_idx, 3, into=proj,
                                       job=_job_gather([blk["out"]]))
    win_f = _copy_block(_copy_block(win_xy, blk["in"], p_idx, 0, "fill_w_in_own"), win_d, p_idx, 3, "fill_w_in_diag")
    wout_r = wout_f.reshape(-1, wout_f.shape[2])
    wco_c = _blocks_to_columns(wco_f, "columns_conv_out")
    wso_c = _blocks_to_columns(wso_f, "columns_sgu_out")
    (conv, a_act), (wup_xy,) = _conv_fwd("conv_fwd", proj, cw_full, conv_dw_b, conv_ln_g, conv_ln_b, dc,
                                         job=_job_gather([blk["up"]], rel=(0, 1)))
    bst = jnp.transpose(b_spatial[0])
    (ya, yb, uv, merged), (wup_f,) = _sgu_fwd("sgu_fwd", proj, a_act, wco_c, wso_c, sgu_ln_g, sgu_ln_b, w_spatial[0],
                                              bst, D, ds, job=_job_gather([wup_xy], rel=(2,)))
    out1, x1 = _mm_fwd_rows("out1", merged, wout_r, xs, gate1)
    h2 = _modnorm_fwd("modnorm2", x1, norm2_g, scale2, shift2)
    up, (wdown_f,) = _mm_fwd("up", h2, wup_f, planes=2, job=_job_gather([blk["down"]]))
    wdown_r = wdown_f.reshape(-1, wdown_f.shape[2])
    fw2 = jnp.stack([fw_full[:, :Fh], fw_full[:, Fh:]])
    fb2 = jnp.stack([ffn_dw_b[:, :Fh], ffn_dw_b[:, Fh:]])
    act = _ffn_fwd("ffn_fwd", up, fw2, fb2)
    out2, x2 = _mm_fwd_rows("out2", act, wdown_r, x1, gate2)
    dx2, d_final_g, loss_blk, dout2, d_gate2 = _final("final", x2, tgt, final_g.reshape(1, D), out2, gate2)
    loss = lax.psum(loss_blk[0, 0], ("x", "y", "c"))

    def add_cores(nm, g, r1):
        return _add_own_half(g, r1, pc_idx, "add_cores_" + nm)

    def add_chips(nm, g, r1, r2):
        return _add_chips(g, r1, r2, pc_idx, "add_chips_" + nm)

    g_wdown = _mm_wgrad_rows("wgrad_down", act, dout2, w_down.shape[1])
    dact, (r1_down,) = _mm_dgrad_rows("dgrad_down", dout2, wdown_r, job=_job_swap_halves([g_wdown]))
    s_down = add_cores("down", g_wdown, r1_down)
    (dup, d_ffn), (r2_down,) = _ffn_bwd("ffn_bwd", up, dact, fw2, fb2, job=_job_scatter_blocks([s_down]))
    h_down = add_chips("down", g_wdown, r1_down, r2_down)
    g_wup = _mm_wgrad_cols("wgrad_up", h2, dup, n_up)
    dh2, (r1_up,) = _mm_dgrad_cols("dgrad_up", dup, wup_f, job=_job_swap_halves([g_wup]))
    s_up = add_cores("up", g_wup, r1_up)
    dx1, d_norm2, d_scale2, d_shift2, dout1, d_gate1 = _modnorm_bwd(
        "modnorm2_bwd", x1, dh2, dx2, norm2_g, scale2, shift2, gated=(out1, gate1))
    g_wout = _mm_wgrad_rows("wgrad_out", merged, dout1, w_out.shape[1])
    dmerged = _mm_dgrad_rows("dgrad_out", dout1, wout_r)
    dya, dyb, dproj, dbin_g = _merge_bwd("merge_bwd", dmerged, proj, ya, yb, D)
    g_wco = _mm_wgrad_cols("wgrad_conv_out", a_act, dya[None], w_conv_out.shape[2])
    g_wso = _mm_wgrad_cols("wgrad_sgu_out", uv, dyb[None], w_sgu_out.shape[2])
    da_act, (r1_out, r1_co, r1_so) = _mm_dgrad_rows("dgrad_conv_out", dya, wco_c,
                                                    job=_job_swap_halves([g_wout, g_wco, g_wso]))
    s_out = add_cores("out", g_wout, r1_out)
    s_co = add_cores("conv_out", g_wco, r1_co)
    s_so = add_cores("sgu_out", g_wso, r1_so)
    duv = _mm_dgrad_rows("dgrad_sgu_out", dyb, wso_c)
    dproj, d_ws, d_bs, d_sgu_g, d_sgu_b, dbin_s = _sgu_bwd("sgu_bwd", proj, duv, sgu_ln_g, sgu_ln_b, w_spatial[0], bst,
                                                           dproj, ds)
    dconv, d_cln_g, d_cln_b, d_conv_b = _conv_bwd_ln("conv_ln_bwd", conv, da_act, conv_ln_g, conv_ln_b)
    d_fw = jnp.concatenate([d_ffn[0, :KF], d_ffn[1, :KF]], axis=1)
    d_fb = jnp.concatenate([d_ffn[0, KF:KF + 1], d_ffn[1, KF:KF + 1]], axis=1)
    early = [d_sgu_g, d_sgu_b, d_ws, d_bs[:, :, 0], d_norm2, d_fb, d_final_g, d_fw]
    (dproj, d_cw, dbin_a), (r2_up, g_early) = _conv_bwd(
        "conv_bwd", dconv, proj, cw_full, dproj, dc,
        job=_merge_jobs([_job_scatter_blocks([s_up]), _job_allgather8(_pack(early))]))
    h_up = add_chips("up", g_wup, r1_up, r2_up)
    g_win, (r2_out, r2_co, r2_so, sib_up, sib_down) = _mm_wgrad_cols(
        "wgrad_in", h1, dproj[None], w_in.shape[2],
        job=_merge_jobs([_job_scatter_blocks([s_out, s_co, s_so]), _job_to_sibling([h_up, h_down])]))
    h_out = add_chips("out", g_wout, r1_out, r2_out)
    h_co = add_chips("conv_out", g_wco, r1_co, r2_co)
    h_so = add_chips("sgu_out", g_wso, r1_so, r2_so)
    dh1, (r1_in, sib_out, sib_co, sib_so) = _mm_dgrad_cols(
        "dgrad_in_a", dproj[None], win_f, row_tiles=(0.0, 0.5),
        job=_merge_jobs([_job_swap_halves([g_win]), _job_to_sibling([h_out, h_co, h_so])]))
    s_in = add_cores("in", g_win, r1_in)
    dh1, (r2_in,) = _mm_dgrad_cols("dgrad_in_b", dproj[None], win_f, row_tiles=(0.5, 0.5), fill_into=dh1,
                                   job=_job_scatter_blocks([s_in]))
    h_in = add_chips("in", g_win, r1_in, r2_in)
    dxs, d_norm1, d_scale1, d_shift1 = _modnorm_bwd("modnorm1_bwd", xs, dh1, dx1, norm1_g, scale1, shift1)

    d_mod = jnp.concatenate([d_shift1, d_scale1, d_gate1, d_shift2, d_scale2, d_gate2], axis=1)
    d_b_in = jnp.concatenate([dbin_a, dbin_s, dbin_g], axis=1)
    late = [d_mod, d_norm1, d_b_in, d_conv_b, d_cln_g, d_cln_b, d_cw[:KC]]
    g3, sib_in = _run_job(_merge_jobs([_job_allgather8(_pack(late)), _job_to_sibling([h_in])]), "gather_small_grads")
    big_halves = {"w_in": (h_in, sib_in), "w_conv_out": (h_co, sib_co), "w_sgu_out": (h_so, sib_so),
                  "w_out": (h_out, sib_out), "w_up": (h_up, sib_up), "w_down": (h_down, sib_down)}
    g_b_ada, g_norm1, g_b_in, g_conv_b, g_cln_g, g_cln_b, g_cw_full = _unpack(
        _sum8(g3, "sum_small_grads"), [a.shape for a in late])
    g_sgu_g, g_sgu_b, g_ws, g_bs, g_norm2, g_fb, g_final, g_fw_full = _unpack(
        _sum8(g_early, "sum_early_grads"), [a.shape for a in early])
    g_cw = lax.dynamic_slice(g_cw_full, (0, chip * n_cw), (KC, n_cw))
    g_fw = lax.dynamic_slice(g_fw_full, (0, chip * n_fw), (KF, n_fw))
    dmod_all = g3.reshape(N_DEV, -1)[:, :6 * D]
    dmod_cols = lax.dynamic_slice(dmod_all, (0, chip * n_ada), (N_DEV, n_ada))
    g_wada = _ada_wgrad("ada_wgrad", jnp.transpose(c_all), dmod_cols)

    grads = {
        "w_ada": g_wada[None], "b_ada": g_b_ada, "norm1_g": g_norm1, "b_in": g_b_in,
        "conv_dw_w": g_cw[None], "conv_dw_b": g_conv_b, "conv_ln_g": g_cln_g, "conv_ln_b": g_cln_b,
        "sgu_ln_g": g_sgu_g, "sgu_ln_b": g_sgu_b, "w_spatial": g_ws[None],
        "b_spatial": g_bs[None], "norm2_g": g_norm2, "ffn_dw_w": g_fw[None], "ffn_dw_b": g_fb,
        "final_g": g_final.reshape(D),
    }
    weights = dict(w_ada=w_ada, b_ada=b_ada, norm1_g=norm1_g, w_in=w_in, b_in=b_in, conv_dw_w=conv_dw_w, conv_dw_b=conv_dw_b, conv_ln_g=conv_ln_g, conv_ln_b=conv_ln_b, w_conv_out=w_conv_out, sgu_ln_g=sgu_ln_g, sgu_ln_b=sgu_ln_b, w_spatial=w_spatial, b_spatial=b_spatial, w_sgu_out=w_sgu_out, w_out=w_out, norm2_g=norm2_g, w_up=w_up, ffn_dw_w=ffn_dw_w, ffn_dw_b=ffn_dw_b, w_down=w_down, final_g=final_g)
    m_in = dict(w_ada=m_w_ada, b_ada=m_b_ada, norm1_g=m_norm1_g, w_in=m_w_in, b_in=m_b_in, conv_dw_w=m_conv_dw_w, conv_dw_b=m_conv_dw_b, conv_ln_g=m_conv_ln_g, conv_ln_b=m_conv_ln_b, w_conv_out=m_w_conv_out, sgu_ln_g=m_sgu_ln_g, sgu_ln_b=m_sgu_ln_b, w_spatial=m_w_spatial, b_spatial=m_b_spatial, w_sgu_out=m_w_sgu_out, w_out=m_w_out, norm2_g=m_norm2_g, w_up=m_w_up, ffn_dw_w=m_ffn_dw_w, ffn_dw_b=m_ffn_dw_b, w_down=m_w_down, final_g=m_final_g)
    v_in = dict(w_ada=v_w_ada, b_ada=v_b_ada, norm1_g=v_norm1_g, w_in=v_w_in, b_in=v_b_in, conv_dw_w=v_conv_dw_w, conv_dw_b=v_conv_dw_b, conv_ln_g=v_conv_ln_g, conv_ln_b=v_conv_ln_b, w_conv_out=v_w_conv_out, sgu_ln_g=v_sgu_ln_g, sgu_ln_b=v_sgu_ln_b, w_spatial=v_w_spatial, b_spatial=v_b_spatial, w_sgu_out=v_w_sgu_out, w_out=v_w_out, norm2_g=v_norm2_g, w_up=v_w_up, ffn_dw_w=v_ffn_dw_w, ffn_dw_b=v_ffn_dw_b, w_down=v_w_down, final_g=v_final_g)
    order = list(weights.keys())
    large = ["w_ada", "w_in", "w_conv_out", "w_sgu_out", "w_out", "w_up", "w_down"]
    little = [n for n in order if n not in large]
    delta, new_m, new_v = {}, {}, {}
    for n in large:
        shp = weights[n].shape
        two = (shp[1], shp[2])
        if n in big_halves:
            g_, d_, m_, v_ = _adamw_halves(weights[n].reshape(two), big_halves[n][0], big_halves[n][1],
                                           m_in[n].reshape(two), v_in[n].reshape(two), c_idx, "adamw_" + n)
            grads[n] = g_.reshape(shp)
        else:
            d_, m_, v_ = _adamw(weights[n].reshape(two), grads[n].reshape(two), m_in[n].reshape(two),
                                v_in[n].reshape(two), "adamw_" + n)
        delta[n], new_m[n], new_v[n] = d_.reshape(shp), m_.reshape(shp), v_.reshape(shp)
    shapes = [weights[n].shape for n in little]
    d_, m_, v_ = _adamw(_pack([weights[n] for n in little]), _pack([grads[n] for n in little]),
                        _pack([m_in[n] for n in little]), _pack([v_in[n] for n in little]), "adamw_small")
    for n, dd, mm, vv in zip(little, _unpack(d_, shapes), _unpack(m_, shapes), _unpack(v_, shapes)):
        delta[n], new_m[n], new_v[n] = dd, mm, vv
    grad_out = [grads[n].reshape(weights[n].shape) for n in order]
    return (loss, dxs[None], *grad_out, *[delta[n] for n in order], *[new_m[n] for n in order],
            *[new_v[n] for n in order])
```

```python
import functools

import jax
import jax.numpy as jnp
from jax import lax
from jax.experimental import pallas as pl
from jax.experimental.pallas import tpu as pltpu

F32 = jnp.float32
BF16 = jnp.bfloat16
EPS = 1e-6
MESH = pl.DeviceIdType.MESH
N_CHIPS = 4
N_DEV = 8
LANES = 128
SUBLANES = 8
CONV_HALO = 32
FFN_HALO = 8
CONV_CHUNK_ROWS = 32
VMEM_LIMIT_BYTES = 56 * 1024 * 1024
WHOLE_WEIGHT_BYTES = 8 * 1024 * 1024
SMALL_TILE_BYTES = 2 * 1024 * 1024

ADAM_LR = 0.001
ADAM_B1 = 0.9
ADAM_B2 = 0.999
ADAM_EPS = 1e-08
ADAM_WD = 0.01
ADAM_STEP = 10

NN = (((1,), (0,)), ((), ()))
NT = (((1,), (1,)), ((), ()))
TN = (((0,), (0,)), ((), ()))


def _params(sem=None):
    return pltpu.CompilerParams(dimension_semantics=sem, vmem_limit_bytes=VMEM_LIMIT_BYTES)


def _pick(dim, pref, mult):
    best = None
    d = mult
    while d <= min(dim, pref):
        if dim % d == 0:
            best = d
        d += mult
    return dim if best is None else best


def _axes():
    return lax.axis_index("x"), lax.axis_index("y"), lax.axis_index("c")


def _modnorm(x, g, scale, shift):
    r = lax.rsqrt(jnp.mean(x * x, axis=-1, keepdims=True) + EPS)
    return (x * r * g) * (1.0 + scale) + shift


def _layer_norm(x, g, b):
    mu = jnp.mean(x, axis=-1, keepdims=True)
    var = jnp.mean(jnp.square(x - mu), axis=-1, keepdims=True)
    return (x - mu) * lax.rsqrt(var + EPS) * g + b


def _gelu(x):
    return 0.5 * x * (1.0 + lax.erf(x * (0.5 ** 0.5)))


def _ln_silu(x, g, b):
    return jax.nn.silu(_layer_norm(x, g, b))


def _tril_mask(ws):
    n = ws.shape[-1]
    row = lax.broadcasted_iota(jnp.int32, (n, n), 0)
    col = lax.broadcasted_iota(jnp.int32, (n, n), 1)
    return jnp.where(row >= col, ws, 0.0)


def _pack(arrs):
    flat = [a.reshape(-1).astype(F32) for a in arrs]
    total = sum(f.shape[0] for f in flat)
    tile = SUBLANES * LANES
    padded = -(-total // tile) * tile
    if padded > total:
        flat = flat + [jnp.zeros((padded - total,), F32)]
    return jnp.concatenate(flat).reshape(padded // LANES, LANES)


def _unpack(buf, shapes):
    flat = buf.reshape(-1)
    out, off = [], 0
    for s in shapes:
        n = 1
        for d in s:
            n *= d
        out.append(flat[off:off + n].reshape(s))
        off += n
    return out


def _allgather8(buf, name):
    R, L = buf.shape

    def body(in_ref, out_ref, send_sems, recv_sems, local_sem):
        x, y, c = _axes()
        me = 4 * x + 2 * y + c
        mine = pltpu.make_async_copy(in_ref, out_ref.at[me], local_sem)
        mine.start()
        peers = []
        for k in range(1, N_DEV):
            px = 1 - x if k & 4 else x
            py = 1 - y if k & 2 else y
            pc = 1 - c if k & 1 else c
            peers.append((px, py, pc))
        sends = []
        for k, peer in enumerate(peers):
            cp = pltpu.make_async_remote_copy(
                src_ref=in_ref, dst_ref=out_ref.at[me], send_sem=send_sems.at[k], recv_sem=recv_sems.at[k],
                device_id=peer, device_id_type=MESH)
            cp.start()
            sends.append(cp)
        for k, (px, py, pc) in enumerate(peers):
            pltpu.make_async_remote_copy(
                src_ref=in_ref, dst_ref=out_ref.at[4 * px + 2 * py + pc], send_sem=send_sems.at[k],
                recv_sem=recv_sems.at[k], device_id=(px, py, pc), device_id_type=MESH).wait_recv()
        for cp in sends:
            cp.wait_send()
        mine.wait()

    return pl.pallas_call(
        body, name=name,
        out_shape=jax.ShapeDtypeStruct((N_DEV, R, L), buf.dtype),
        in_specs=[pl.BlockSpec(memory_space=pltpu.VMEM)],
        out_specs=pl.BlockSpec(memory_space=pltpu.VMEM),
        scratch_shapes=[pltpu.SemaphoreType.DMA((N_DEV - 1,)), pltpu.SemaphoreType.DMA((N_DEV - 1,)),
                        pltpu.SemaphoreType.DMA],
        compiler_params=pltpu.CompilerParams(vmem_limit_bytes=VMEM_LIMIT_BYTES),
    )(buf)


def _other_chips(x, y):
    return [(1 - x, y), (x, 1 - y), (1 - x, 1 - y)]


def _cast_into_block(shard, p_idx, name):
    K, n = shard.shape
    tr = _pick(K, max(SUBLANES, (1 << 19) // n), 2 * SUBLANES)

    def body(p_ref, s_ref, o_ref):
        o_ref[...] = s_ref[...].astype(BF16)

    return pl.pallas_call(
        body, name=name,
        grid_spec=pltpu.PrefetchScalarGridSpec(
            num_scalar_prefetch=1, grid=(K // tr,),
            in_specs=[pl.BlockSpec((tr, n), lambda i, p: (i, 0))],
            out_specs=pl.BlockSpec((None, tr, n), lambda i, p: (p[0], i, 0))),
        out_shape=jax.ShapeDtypeStruct((N_CHIPS, K, n), BF16),
        compiler_params=_params(("parallel",)),
    )(p_idx, shard)


class _Job:
    def __init__(self, ins, outs, aliases, n_sems, make):
        self.ins, self.outs, self.aliases, self.n_sems, self.make = list(ins), list(outs), list(aliases), n_sems, make


def _merge_jobs(jobs):
    ins, outs, aliases, offs = [], [], [], []
    n_sems = 0
    for jb in jobs:
        offs.append((len(ins), len(outs), n_sems))
        aliases += [(len(ins) + a, len(outs) + b) for a, b in jb.aliases]
        ins += jb.ins
        outs += jb.outs
        n_sems += jb.n_sems

    def make(in_refs, out_refs, send_sems, recv_sems, base=0):
        made = []
        for jb, (oi, oo, os_) in zip(jobs, offs):
            made.append(jb.make(in_refs[oi:oi + len(jb.ins)], out_refs[oo:oo + len(jb.outs)],
                                send_sems, recv_sems, base + os_))

        def start():
            for st, _ in made:
                st()

        def finish():
            for _, fin in made:
                fin()

        return start, finish

    return _Job(ins, outs, aliases, n_sems, make)


def _job_gather(fulls, rel=(0, 1, 2), fresh=False):
    nw = len(fulls)

    def make(in_refs, outs, send_sems, recv_sems, base=0):
        x, y, c = _axes()
        p = 2 * x + y
        chips = _other_chips(x, y)
        srcs = in_refs if fresh else outs

        def rows(w, mine):
            kh = outs[w].shape[1] // 2
            return pl.ds((c if mine else 1 - c) * kh, kh)

        def over_ici(w, j, block):
            qx, qy = chips[j]
            return pltpu.make_async_remote_copy(
                src_ref=srcs[w].at[block, rows(w, True)], dst_ref=outs[w].at[block, rows(w, True)],
                send_sem=send_sems.at[base + 6 * w + j], recv_sem=recv_sems.at[base + 6 * w + j],
                device_id=(qx, qy, c), device_id_type=MESH)

        def over_d2d(w, j, mine):
            qx, qy = chips[j]
            return pltpu.make_async_remote_copy(
                src_ref=outs[w].at[2 * qx + qy, rows(w, mine)], dst_ref=outs[w].at[2 * qx + qy, rows(w, mine)],
                send_sem=send_sems.at[base + 6 * w + 3 + j], recv_sem=recv_sems.at[base + 6 * w + 3 + j],
                device_id=(x, y, 1 - c), device_id_type=MESH)

        def start():
            for w in range(nw):
                for j in rel:
                    over_ici(w, j, p).start()

        def finish():
            for w in range(nw):
                for j in rel:
                    qx, qy = chips[j]
                    over_ici(w, j, 2 * qx + qy).wait_recv()
                    over_d2d(w, j, True).start()
            for w in range(nw):
                for j in rel:
                    over_d2d(w, j, False).wait_recv()
            for w in range(nw):
                for j in rel:
                    over_ici(w, j, p).wait_send()
                    over_d2d(w, j, True).wait_send()

        return start, finish

    return _Job(fulls, [jax.ShapeDtypeStruct(f.shape, f.dtype) for f in fulls],
                [] if fresh else [(w, w) for w in range(nw)], 6 * nw, make)


def _job_simple(ins, outs, n_per, copies_of):
    nw = len(ins)

    def make(in_refs, out_refs, send_sems, recv_sems, base=0):
        x, y, c = _axes()
        copies = []
        for w in range(nw):
            for j, (src, dst, dev) in enumerate(copies_of(w, in_refs[w], out_refs[w], x, y, c)):
                k = base + n_per * w + j
                copies.append(pltpu.make_async_remote_copy(
                    src_ref=src, dst_ref=dst, send_sem=send_sems.at[k], recv_sem=recv_sems.at[k],
                    device_id=dev, device_id_type=MESH))

        def start():
            for cp in copies:
                cp.start()

        def finish():
            for cp in copies:
                cp.wait()

        return start, finish

    return _Job(ins, outs, [], n_per * nw, make)


def _job_swap_halves(grads):
    def copies_of(w, src, dst, x, y, c):
        kh = src.shape[1] // 2
        return [(src.at[:, pl.ds((1 - c) * kh, kh), :], dst, (x, y, 1 - c))]

    outs = [jax.ShapeDtypeStruct((g.shape[0], g.shape[1] // 2, g.shape[2]), g.dtype) for g in grads]
    return _job_simple(grads, outs, 1, copies_of)


def _job_scatter_blocks(sums):
    def copies_of(w, src, dst, x, y, c):
        return [(src.at[2 * qx + qy], dst.at[j], (qx, qy, c)) for j, (qx, qy) in enumerate(_other_chips(x, y))]

    outs = [jax.ShapeDtypeStruct((3,) + s.shape[1:], s.dtype) for s in sums]
    return _job_simple(sums, outs, 3, copies_of)


def _job_to_sibling(arrs):
    def copies_of(w, src, dst, x, y, c):
        return [(src, dst, (x, y, 1 - c))]

    outs = [jax.ShapeDtypeStruct(a.shape, a.dtype) for a in arrs]
    return _job_simple(arrs, outs, 1, copies_of)


def _job_allgather8(buf):
    def make(in_refs, out_refs, send_sems, recv_sems, base=0):
        x, y, c = _axes()
        me = 4 * x + 2 * y + c
        src, dst = in_refs[0], out_refs[0]
        peers = [(1 - x if k & 4 else x, 1 - y if k & 2 else y, 1 - c if k & 1 else c) for k in range(1, N_DEV)]
        mine = pltpu.make_async_copy(src, dst.at[me], send_sems.at[base + N_DEV - 1])

        def to(k):
            return pltpu.make_async_remote_copy(
                src_ref=src, dst_ref=dst.at[me], send_sem=send_sems.at[base + k], recv_sem=recv_sems.at[base + k],
                device_id=peers[k], device_id_type=MESH)

        def of(k):
            px, py, pc = peers[k]
            return pltpu.make_async_remote_copy(
                src_ref=src, dst_ref=dst.at[4 * px + 2 * py + pc], send_sem=send_sems.at[base + k],
                recv_sem=recv_sems.at[base + k], device_id=peers[k], device_id_type=MESH)

        def start():
            mine.start()
            for k in range(N_DEV - 1):
                to(k).start()

        def finish():
            for k in range(N_DEV - 1):
                of(k).wait_recv()
                to(k).wait_send()
            mine.wait()

        return start, finish

    return _Job([buf], [jax.ShapeDtypeStruct((N_DEV,) + buf.shape, buf.dtype)], [], N_DEV, make)


def _run_job(job, name):
    ni, no = len(job.ins), len(job.outs)

    def body(*refs):
        start, finish = job.make(refs[:ni], refs[ni:ni + no], refs[-2], refs[-1])
        start()
        finish()

    any_spec = pl.BlockSpec(memory_space=pl.ANY)
    return pl.pallas_call(
        body, name=name, out_shape=job.outs, in_specs=[any_spec] * ni, out_specs=[any_spec] * no,
        input_output_aliases=dict(job.aliases),
        scratch_shapes=[pltpu.SemaphoreType.DMA((job.n_sems,)), pltpu.SemaphoreType.DMA((job.n_sems,))],
    )(*job.ins)


def _call(body, name, grid, in_specs, out_specs, out_shape, scratch_shapes, semantics, args, aliases=None, job=None,
          prefetch=()):
    in_specs, out_specs, out_shape = list(in_specs), list(out_specs), list(out_shape)
    scratch_shapes = list(scratch_shapes)
    n_pre, n_in, n_out, n_scr = len(prefetch), len(args), len(out_shape), len(scratch_shapes)
    all_aliases = {n_pre + a: b for a, b in (aliases or {}).items()}
    if job is None:
        wrapped, extra_in, semantics = body, [], semantics
    else:
        ni, no = len(job.ins), len(job.outs)

        def wrapped(*refs):
            pre, refs = refs[:n_pre], refs[n_pre:]
            ins, cins = refs[:n_in], refs[n_in:n_in + ni]
            outs, couts = refs[n_in + ni:n_in + ni + n_out], refs[n_in + ni + n_out:n_in + ni + n_out + no]
            scr = refs[n_in + ni + n_out + no:n_in + ni + n_out + no + n_scr]
            start, finish = job.make(cins, couts, refs[-2], refs[-1])
            first = functools.reduce(jnp.logical_and, [pl.program_id(a) == 0 for a in range(len(grid))])
            last = functools.reduce(jnp.logical_and, [pl.program_id(a) == grid[a] - 1 for a in range(len(grid))])
            pl.when(first)(start)
            body(*pre, *ins, *outs, *scr)
            pl.when(last)(finish)

        any_spec = pl.BlockSpec(memory_space=pl.ANY)
        for a, b in job.aliases:
            all_aliases[n_pre + n_in + a] = n_out + b
        in_specs, out_specs, out_shape = in_specs + [any_spec] * ni, out_specs + [any_spec] * no, out_shape + job.outs
        scratch_shapes = scratch_shapes + [pltpu.SemaphoreType.DMA((job.n_sems,)), pltpu.SemaphoreType.DMA((job.n_sems,))]
        extra_in, semantics = job.ins, tuple("arbitrary" for _ in grid)
    if n_pre:
        return pl.pallas_call(
            wrapped, name=name,
            grid_spec=pltpu.PrefetchScalarGridSpec(num_scalar_prefetch=n_pre, grid=grid, in_specs=in_specs,
                                                   out_specs=out_specs, scratch_shapes=scratch_shapes),
            out_shape=out_shape, input_output_aliases=all_aliases,
            compiler_params=_params(semantics))(*prefetch, *args, *extra_in)
    return pl.pallas_call(
        wrapped, name=name, grid=grid, in_specs=in_specs, out_specs=out_specs, out_shape=out_shape,
        scratch_shapes=scratch_shapes, input_output_aliases=all_aliases,
        compiler_params=_params(semantics))(*args, *extra_in)


def _add_own_half(g, r, pc_idx, name):
    nb, K, n = g.shape
    kh = K // 2
    tr = _pick(kh, max(SUBLANES, (1 << 19) // n), 2 * SUBLANES)
    per = kh // tr

    def body(pc_ref, g_ref, r_ref, o_ref):
        o_ref[...] = (g_ref[...] + r_ref[...]).astype(BF16)

    def other(b, pc):
        return jnp.bitwise_xor(pc[0], b + 1)

    return pl.pallas_call(
        body, name=name,
        grid_spec=pltpu.PrefetchScalarGridSpec(
            num_scalar_prefetch=1, grid=(nb - 1, per),
            in_specs=[pl.BlockSpec((None, tr, n), lambda b, i, pc: (other(b, pc), pc[1] * per + i, 0)),
                      pl.BlockSpec((None, tr, n), lambda b, i, pc: (other(b, pc), i, 0))],
            out_specs=pl.BlockSpec((None, tr, n), lambda b, i, pc: (other(b, pc), i, 0))),
        out_shape=jax.ShapeDtypeStruct((nb, kh, n), BF16),
        compiler_params=_params(("parallel", "parallel")),
    )(pc_idx, g, r)


def _add_chips(g, r1, r2, pc_idx, name):
    _, K, n = g.shape
    kh = K // 2
    tr = _pick(kh, max(SUBLANES, (1 << 19) // n), 2 * SUBLANES)
    per = kh // tr

    def body(pc_ref, g_ref, r1_ref, r2_ref, o_ref):
        own = g_ref[...] + r1_ref[...]
        o_ref[...] = ((own + r2_ref[0].astype(F32)) + r2_ref[1].astype(F32)) + r2_ref[2].astype(F32)

    return pl.pallas_call(
        body, name=name,
        grid_spec=pltpu.PrefetchScalarGridSpec(
            num_scalar_prefetch=1, grid=(per,),
            in_specs=[pl.BlockSpec((None, tr, n), lambda i, pc: (pc[0], pc[1] * per + i, 0)),
                      pl.BlockSpec((None, tr, n), lambda i, pc: (pc[0], i, 0)),
                      pl.BlockSpec((3, tr, n), lambda i, pc: (0, i, 0))],
            out_specs=pl.BlockSpec((tr, n), lambda i, pc: (i, 0))),
        out_shape=jax.ShapeDtypeStruct((kh, n), F32),
        compiler_params=_params(("parallel",)),
    )(pc_idx, g, r1, r2)


def _sum8(g, name):
    _, R, L = g.shape

    def body(g_ref, o_ref):
        acc = g_ref[0]
        for d in range(1, N_DEV):
            acc = acc + g_ref[d]
        o_ref[...] = acc

    return pl.pallas_call(
        body, name=name, out_shape=jax.ShapeDtypeStruct((R, L), F32),
        in_specs=[pl.BlockSpec(memory_space=pltpu.VMEM)], out_specs=pl.BlockSpec(memory_space=pltpu.VMEM),
        compiler_params=_params(),
    )(g)


def _adamw_math(w, gg, m, v):
    nm = ADAM_B1 * m + (1.0 - ADAM_B1) * gg
    nv = ADAM_B2 * v + (1.0 - ADAM_B2) * jnp.square(gg)
    m_hat = nm / (1.0 - ADAM_B1 ** ADAM_STEP)
    v_hat = nv / (1.0 - ADAM_B2 ** ADAM_STEP)
    return -ADAM_LR * (m_hat / (jnp.sqrt(v_hat) + ADAM_EPS) + ADAM_WD * w), nm, nv


def _adamw(w, g, m, v, name):
    R, C = w.shape
    tr = _pick(R, max(SUBLANES, (1 << 18) // C), SUBLANES)

    def body(w_ref, g_ref, m_ref, v_ref, d_ref, nm_ref, nv_ref):
        d_ref[...], nm_ref[...], nv_ref[...] = _adamw_math(w_ref[...], g_ref[...], m_ref[...], v_ref[...])

    spec = pl.BlockSpec((tr, C), lambda i: (i, 0))
    sd = jax.ShapeDtypeStruct((R, C), F32)
    return pl.pallas_call(
        body, name=name, grid=(R // tr,), in_specs=[spec] * 4, out_specs=[spec] * 3, out_shape=[sd] * 3,
        compiler_params=_params(("parallel",)),
    )(w, g, m, v)


def _adamw_halves(w, g_own, g_sib, m, v, c_idx, name):
    K, n = w.shape
    kh = K // 2
    tr = _pick(kh, max(SUBLANES, (1 << 18) // n), SUBLANES)
    per = kh // tr

    def body(c_ref, w_ref, go_ref, gs_ref, m_ref, v_ref, g_ref, d_ref, nm_ref, nv_ref):
        h = pl.program_id(0)

        def step(gg):
            g_ref[...] = gg
            d_ref[...], nm_ref[...], nv_ref[...] = _adamw_math(w_ref[...], gg, m_ref[...], v_ref[...])

        @pl.when(h == 0)
        def _():
            step(go_ref[...])

        @pl.when(h == 1)
        def _():
            step(gs_ref[...])

    full = pl.BlockSpec((tr, n), lambda h, i, c: (((c[0] + h) % 2) * per + i, 0))
    own = pl.BlockSpec((tr, n), lambda h, i, c: (i * (1 - h), 0))
    sib = pl.BlockSpec((tr, n), lambda h, i, c: (i * h, 0))
    sd = jax.ShapeDtypeStruct((K, n), F32)
    return pl.pallas_call(
        body, name=name,
        grid_spec=pltpu.PrefetchScalarGridSpec(
            num_scalar_prefetch=1, grid=(2, per),
            in_specs=[full, own, sib, full, full], out_specs=[full] * 4),
        out_shape=[sd] * 4,
        compiler_params=_params(("arbitrary", "arbitrary")),
    )(c_idx, w, g_own, g_sib, m, v)


def _matmul(name, grid, dims, a, a_spec, b, b_spec, outs, out_specs, acc_shape,
            extras=(), extra_specs=(), epilogue=None, job=None, fill_into=None, prefetch=()):
    nk = grid[2]
    npre = len(prefetch)
    aliases = None
    if fill_into is not None:
        aliases = {2 + len(extras): 0}
        extras = tuple(extras) + (fill_into,)
        extra_specs = tuple(extra_specs) + (pl.BlockSpec(memory_space=pl.ANY),)
    nex = len(extras)
    nout = len(outs)

    def body(*refs):
        a_ref, b_ref, rest = refs[npre], refs[npre + 1], refs[npre + 2:]
        ex, o = rest[:nex], rest[nex:nex + nout]
        part = lax.dot_general(a_ref[...].astype(BF16), b_ref[...].astype(BF16), dims,
                               preferred_element_type=F32)

        def finish(res):
            if epilogue is None:
                o[0][...] = res.astype(o[0].dtype)
            else:
                epilogue(res, ex, o)

        if nk == 1:
            finish(part)
        else:
            acc = o[0] if in_place else rest[-1]
            k = pl.program_id(2)

            @pl.when(k == 0)
            def _():
                acc[...] = part

            @pl.when(k > 0)
            def _():
                acc[...] += part

            if not in_place:
                @pl.when(k == nk - 1)
                def _():
                    finish(acc[...])

    in_place = epilogue is None and nout == 1 and outs[0].dtype == F32
    res = _call(body, name, grid, [a_spec, b_spec] + list(extra_specs), out_specs, outs,
                [] if nk == 1 or in_place else [pltpu.VMEM(acc_shape, F32)], ("parallel", "parallel", "arbitrary"),
                (a, b, *extras), aliases=aliases, job=job, prefetch=prefetch)
    return res if job is None else (res[:nout], res[nout:])


def _first(res, job):
    return res[0] if job is None else (res[0][0], res[1])


def _mm_fwd(name, a, wfull, out_dtype=F32, planes=1, bias=None, job=None):
    S, K = a.shape
    _, _, n = wfull.shape
    N = N_CHIPS * n
    tm = _pick(S, 1024, LANES)
    tn = _pick(n, 1408, LANES)
    per = n // tn
    nj = N // tn
    pj = nj // planes
    grid = (S // tm, nj, 1)
    a_spec = pl.BlockSpec((tm, K), lambda i, j, k: (i, 0))
    b_spec = pl.BlockSpec((None, K, tn), lambda i, j, k: (j // per, 0, j % per))
    o_spec = pl.BlockSpec((None, tm, tn), lambda i, j, k: (j // pj, i, j % pj))
    sd = jax.ShapeDtypeStruct((planes, S, N // planes), out_dtype)
    if bias is not None:
        def epi(res, ex, o):
            o[0][...] = (res + ex[0][...]).astype(o[0].dtype)

        out = _matmul(name, grid, NN, a, a_spec, wfull, b_spec, [sd], [o_spec], (tm, tn), extras=(bias,),
                      extra_specs=(pl.BlockSpec((1, tn), lambda i, j, k: (0, j)),), epilogue=epi, job=job)
    else:
        out = _matmul(name, grid, NN, a, a_spec, wfull, b_spec, [sd], [o_spec], (tm, tn), job=job)
    return _first(out, job)


def _mm_fwd_block(name, a, wfull, bias, p_idx, mask, into=None, job=None):
    S, K = a.shape
    _, _, n = wfull.shape
    N = N_CHIPS * n
    tm = _pick(S, 1024, LANES)
    tn = _pick(n, 1408, LANES)
    per = n // tn
    grid = (S // tm, per, 1)
    a_spec = pl.BlockSpec((tm, K), lambda i, j, k, p: (i, 0))
    b_spec = pl.BlockSpec((None, K, tn), lambda i, j, k, p: (jnp.bitwise_xor(p[0], mask), 0, j))
    o_spec = pl.BlockSpec((tm, tn), lambda i, j, k, p: (i, jnp.bitwise_xor(p[0], mask) * per + j))
    v_spec = pl.BlockSpec((1, tn), lambda i, j, k, p: (0, jnp.bitwise_xor(p[0], mask) * per + j))

    def epi(res, ex, o):
        o[0][...] = res + ex[0][...]

    return _matmul(name, grid, NN, a, a_spec, wfull, b_spec, [jax.ShapeDtypeStruct((S, N), F32)], [o_spec], (tm, tn),
                   extras=(bias,), extra_specs=(v_spec,), epilogue=epi, job=job, fill_into=into, prefetch=(p_idx,))


def _copy_block(dst, src, p_idx, mask, name):
    _, K, n = dst.shape
    tr = _pick(K, max(SUBLANES, (1 << 19) // n), 2 * SUBLANES)

    def body(p_ref, s_ref, d_any, o_ref):
        o_ref[...] = s_ref[...]

    spec = pl.BlockSpec((None, tr, n), lambda i, p: (jnp.bitwise_xor(p[0], mask), i, 0))
    return _call(body, name, (K // tr,), [spec, pl.BlockSpec(memory_space=pl.ANY)], [spec],
                 [jax.ShapeDtypeStruct(dst.shape, dst.dtype)], [], ("parallel",), (src, dst), aliases={1: 0},
                 prefetch=(p_idx,))[0]


def _mm_fwd_rows(name, a, wrows, resid, gate, job=None):
    S, K = a.shape
    _, N = wrows.shape
    whole = K * N * 2 <= WHOLE_WEIGHT_BYTES
    tm = _pick(S, 512 if whole else 1024, LANES)
    tk = K if whole else _pick(K, 2816, LANES)
    tn = N if whole else _pick(N, 1024, LANES)
    grid = (S // tm, N // tn, K // tk)
    a_spec = pl.BlockSpec((tm, tk), lambda i, j, k: (i, k))
    if tn == N and tk == K:
        b_spec = pl.BlockSpec((tk, tn), lambda i, j, k: (k, j), pipeline_mode=pl.Buffered(1))
    else:
        b_spec = pl.BlockSpec((tk, tn), lambda i, j, k: (k, j))
    o_spec = pl.BlockSpec((tm, tn), lambda i, j, k: (i, j))
    g_spec = pl.BlockSpec((1, tn), lambda i, j, k: (0, j))

    def epi(res, ex, o):
        o[0][...] = res.astype(BF16)
        o[1][...] = ex[0][...] + ex[1][...] * res

    return _matmul(name, grid, NN, a, a_spec, wrows, b_spec,
                   [jax.ShapeDtypeStruct((S, N), BF16), jax.ShapeDtypeStruct((S, N), F32)], [o_spec, o_spec], (tm, tn),
                   extras=(resid, gate), extra_specs=(o_spec, g_spec), epilogue=epi, job=job)


def _mm_dgrad_cols(name, dy, wfull, out_dtype=F32, job=None, row_tiles=None, fill_into=None):
    planes, S, npl = dy.shape
    _, K, n = wfull.shape
    tm = _pick(S if row_tiles is None else S // 2, 1024, LANES)
    to = _pick(K, 1024, LANES)
    tk = _pick(n, 2816, LANES)
    per = n // tk
    nk = N_CHIPS * per
    pk = nk // planes
    n_tiles = S // tm
    i0, ni = (0, n_tiles) if row_tiles is None else (int(row_tiles[0] * n_tiles), int(row_tiles[1] * n_tiles))
    grid = (ni, K // to, nk)
    a_spec = pl.BlockSpec((None, tm, tk), lambda i, j, k: (k // pk, i + i0, k % pk))
    b_spec = pl.BlockSpec((None, to, tk), lambda i, j, k: (k // per, j, k % per))
    o_spec = pl.BlockSpec((tm, to), lambda i, j, k: (i + i0, j))
    return _first(_matmul(name, grid, NT, dy, a_spec, wfull, b_spec, [jax.ShapeDtypeStruct((S, K), out_dtype)],
                          [o_spec], (tm, to), job=job, fill_into=fill_into), job)


def _mm_dgrad_rows(name, dy, wrows, out_dtype=F32, job=None):
    S, N = dy.shape
    K, _ = wrows.shape
    tm = _pick(S, 1024, LANES)
    to = _pick(K, 1408, LANES)
    tk = _pick(N, 2048, LANES)
    grid = (S // tm, K // to, N // tk)
    a_spec = pl.BlockSpec((tm, tk), lambda i, j, k: (i, k))
    b_spec = pl.BlockSpec((to, tk), lambda i, j, k: (j, k))
    o_spec = pl.BlockSpec((tm, to), lambda i, j, k: (i, j))
    return _first(_matmul(name, grid, NT, dy, a_spec, wrows, b_spec, [jax.ShapeDtypeStruct((S, K), out_dtype)],
                          [o_spec], (tm, to), job=job), job)


def _mm_wgrad_cols(name, a, dy, n, job=None):
    S, K = a.shape
    planes, _, npl = dy.shape
    N = planes * npl
    to = _pick(K, 1024, LANES)
    tn = _pick(n, 1408, LANES)
    ts = _pick(S, 4096 if to * tn * 4 <= SMALL_TILE_BYTES else 2048, LANES)
    per = n // tn
    nj = N // tn
    pj = nj // planes
    grid = (K // to, nj, S // ts)
    a_spec = pl.BlockSpec((ts, to), lambda i, j, k: (k, i))
    b_spec = pl.BlockSpec((None, ts, tn), lambda i, j, k: (j // pj, k, j % pj))
    o_spec = pl.BlockSpec((None, to, tn), lambda i, j, k: (j // per, i, j % per))
    return _first(_matmul(name, grid, TN, a, a_spec, dy, b_spec, [jax.ShapeDtypeStruct((N_CHIPS, K, n), F32)],
                          [o_spec], (to, tn), job=job), job)


def _mm_wgrad_rows(name, a, dy, kshard, job=None):
    S, K = a.shape
    _, N = dy.shape
    to = _pick(kshard, 1408, LANES)
    tn = N if to * N * 4 <= 2 * SMALL_TILE_BYTES else _pick(N, 1024, LANES)
    ts = _pick(S, 2048, LANES)
    per = kshard // to
    grid = (K // to, N // tn, S // ts)
    a_spec = pl.BlockSpec((ts, to), lambda i, j, k: (k, i))
    b_spec = pl.BlockSpec((ts, tn), lambda i, j, k: (k, j))
    o_spec = pl.BlockSpec((None, to, tn), lambda i, j, k: (i // per, i % per, j))
    return _first(_matmul(name, grid, TN, a, a_spec, dy, b_spec, [jax.ShapeDtypeStruct((N_CHIPS, kshard, N), F32)],
                          [o_spec], (to, tn), job=job), job)


def _rows(tm, width, colblk=0):
    return pl.BlockSpec((tm, width), lambda i: (i, colblk))


def _whole(shape):
    nd = len(shape)
    return pl.BlockSpec(shape, lambda i: (0,) * nd)


def _prev_halo(tm, h, width, colblk=0):
    r = tm // h
    return pl.BlockSpec((h, width), lambda i: (jnp.maximum(i * r - 1, 0), colblk))


def _next_halo(tm, h, width, nblk, colblk=0):
    r = tm // h
    return pl.BlockSpec((h, width), lambda i: (jnp.minimum((i + 1) * r, nblk - 1), colblk))


def _accumulate(i, ref, val):
    @pl.when(i == 0)
    def _():
        ref[...] = val

    @pl.when(i > 0)
    def _():
        ref[...] += val


def _fill_rotations(rot, offs):
    n = rot.shape[1]
    for r in sorted({o % SUBLANES for o in offs} - {0}):
        rot[r, 0:n - SUBLANES, :] = rot[0, r:r + n - SUBLANES, :]


def _tap_windows(rot, offs, row0, rb, lanes):
    by_res = {}
    for k, o in enumerate(offs):
        by_res.setdefault(o % SUBLANES, []).append((o // SUBLANES, k))
    for r, taps in by_res.items():
        lo = min(a for a, _ in taps)
        hi = max(a for a, _ in taps)
        win = rot[r, pl.ds(row0 + SUBLANES * lo, rb + SUBLANES * (hi - lo)), lanes]
        for a, k in taps:
            yield k, win[SUBLANES * (a - lo):SUBLANES * (a - lo) + rb, :]


def _for_chunks(n_rows, rb, fn):
    def step(j, carry):
        fn(pl.multiple_of(j * rb, rb))
        return carry

    lax.fori_loop(0, n_rows // rb, step, 0)


def _modnorm_fwd(name, x, g, scale, shift):
    S, D = x.shape
    tm = _pick(S, 1024, LANES)

    def body(x_ref, g_ref, sc_ref, sh_ref, h_ref):
        h_ref[...] = _modnorm(x_ref[...], g_ref[...], sc_ref[...], sh_ref[...]).astype(BF16)

    vec = _whole((1, D))
    return pl.pallas_call(
        body, name=name, grid=(S // tm,), in_specs=[_rows(tm, D), vec, vec, vec], out_specs=_rows(tm, D),
        out_shape=jax.ShapeDtypeStruct((S, D), BF16), compiler_params=_params(("parallel",)),
    )(x, g, scale, shift)


def _modnorm_bwd(name, x, dh, dx_in, g, scale, shift, gated=None):
    S, D = x.shape
    tm = _pick(S, 256, LANES)

    def body(x_ref, dh_ref, dxin_ref, g_ref, sc_ref, sh_ref, *rest):
        i = pl.program_id(0)
        dx_ref, dg_ref, dsc_ref, dsh_ref = rest[-4:] if gated is None else rest[2:6]
        _, pull = jax.vjp(_modnorm, x_ref[...], g_ref[...], sc_ref[...], sh_ref[...])
        dx, dg, dsc, dsh = pull(dh_ref[...])
        dx = dxin_ref[...] + dx
        dx_ref[...] = dx
        _accumulate(i, dg_ref, dg)
        _accumulate(i, dsc_ref, dsc)
        _accumulate(i, dsh_ref, dsh)
        if gated is not None:
            _gate_bwd_tile(i, dx, rest[0], rest[1], rest[6], rest[7])

    vec = _whole((1, D))
    row = _rows(tm, D)
    vsd = jax.ShapeDtypeStruct((1, D), F32)
    in_specs, args = [row, row, row, vec, vec, vec], (x, dh, dx_in, g, scale, shift)
    out_specs, out_shape = [row, vec, vec, vec], [jax.ShapeDtypeStruct((S, D), F32), vsd, vsd, vsd]
    if gated is not None:
        in_specs, args = in_specs + [row, vec], args + tuple(gated)
        out_specs, out_shape = out_specs + [row, vec], out_shape + [jax.ShapeDtypeStruct((S, D), BF16), vsd]
    return _call(body, name, (S // tm,), in_specs, out_specs, out_shape, [], ("arbitrary",), args)


def _conv_fwd(name, proj, w, b, lg, lb, dc, job=None):
    S = proj.shape[0]
    K = w.shape[0]
    H = CONV_HALO
    tm = _pick(S, 256, LANES)

    def glu(v):
        return v[:, :dc] * jax.nn.sigmoid(v[:, dc:])

    offs = [H - (K - 1) + k for k in range(K)]
    rb = _pick(tm, CONV_CHUNK_ROWS, SUBLANES)
    lw = min(LANES, dc)

    def body(cur_ref, prev_ref, w_ref, b_ref, lg_ref, lb_ref, conv_ref, act_ref, rot):
        i = pl.program_id(0)
        rot[0, 0:H, :] = jnp.where(i > 0, glu(prev_ref[...]), 0.0)
        rot[0, H:, :] = glu(cur_ref[...])
        _fill_rotations(rot, offs)

        def chunk(row0):
            for l0 in range(0, dc, lw):
                lanes = slice(l0, l0 + lw)
                acc = jnp.broadcast_to(b_ref[:, lanes], (rb, lw))
                for k, win in _tap_windows(rot, offs, row0, rb, lanes):
                    acc = acc + w_ref[k:k + 1, lanes] * win
                conv_ref[pl.ds(row0, rb), lanes] = acc

        _for_chunks(tm, rb, chunk)
        act_ref[...] = _ln_silu(conv_ref[...], lg_ref[...], lb_ref[...]).astype(BF16)

    vec = _whole((1, dc))
    res = _call(body, name, (S // tm,),
                [_rows(tm, 2 * dc), _prev_halo(tm, H, 2 * dc), _whole(w.shape), vec, vec, vec],
                [_rows(tm, dc), _rows(tm, dc)],
                [jax.ShapeDtypeStruct((S, dc), F32), jax.ShapeDtypeStruct((S, dc), BF16)],
                [pltpu.VMEM((SUBLANES, tm + H, dc), F32)], ("parallel",), (proj, proj, w, b, lg, lb), job=job)
    return res if job is None else (res[:2], res[2:])


def _conv_bwd_ln(name, conv, dact, lg, lb):
    S, dc = conv.shape
    tm = _pick(S, 256, LANES)

    def body(c_ref, d_ref, lg_ref, lb_ref, dc_ref, dlg_ref, dlb_ref, db_ref):
        i = pl.program_id(0)
        _, pull = jax.vjp(_ln_silu, c_ref[...], lg_ref[...], lb_ref[...])
        dcv, dlg, dlb = pull(d_ref[...])
        dc_ref[...] = dcv
        _accumulate(i, dlg_ref, dlg)
        _accumulate(i, dlb_ref, dlb)
        _accumulate(i, db_ref, jnp.sum(dcv, axis=0, keepdims=True))

    vec = _whole((1, dc))
    row = _rows(tm, dc)
    vsd = jax.ShapeDtypeStruct((1, dc), F32)
    return pl.pallas_call(
        body, name=name, grid=(S // tm,), in_specs=[row, row, vec, vec], out_specs=[row, vec, vec, vec],
        out_shape=[jax.ShapeDtypeStruct((S, dc), F32), vsd, vsd, vsd],
        compiler_params=_params(("arbitrary",)),
    )(conv, dact, lg, lb)


def _conv_bwd(name, dconv, proj, w, dproj, dc, job=None):
    S = proj.shape[0]
    K = w.shape[0]
    H = CONV_HALO
    tm = _pick(S, 256, LANES)
    nt = S // tm

    offs_g = [H - (K - 1) + k for k in range(K)]
    offs_d = [K - 1 - k for k in range(K)]
    rb = _pick(tm, CONV_CHUNK_ROWS, SUBLANES)
    lw = min(LANES, dc)
    kp = -(-K // SUBLANES) * SUBLANES

    def body(d_ref, dn_ref, cur_ref, prev_ref, w_ref, dproj_any, da_ref, dw_ref, dbin_ref, rotg, rotd, accw, dglu_s):
        i = pl.program_id(0)
        pv = prev_ref[...]
        cv = cur_ref[...]
        sig = jax.nn.sigmoid(cv[:, dc:])
        rotg[0, 0:H, :] = jnp.where(i > 0, pv[:, :dc] * jax.nn.sigmoid(pv[:, dc:]), 0.0)
        rotg[0, H:, :] = cv[:, :dc] * sig
        rotd[0, 0:tm, :] = d_ref[...]
        rotd[0, tm:, :] = jnp.where(i < nt - 1, dn_ref[...], 0.0)
        _fill_rotations(rotg, offs_g)
        _fill_rotations(rotd, offs_d)

        @pl.when(i == 0)
        def _():
            accw[...] = jnp.zeros_like(accw)

        for l0 in range(0, dc, lw):
            lanes = slice(l0, l0 + lw)

            def chunk(j, sums, lanes=lanes):
                row0 = pl.multiple_of(j * rb, rb)
                dcur = rotd[0, pl.ds(row0, rb), lanes]
                acc = jnp.zeros((rb, lw), F32)
                for k, win in _tap_windows(rotd, offs_d, row0, rb, lanes):
                    acc = acc + w_ref[k:k + 1, lanes] * win
                new = list(sums)
                for k, win in _tap_windows(rotg, offs_g, row0, rb, lanes):
                    new[k] = sums[k] + jnp.sum((dcur * win).reshape(rb // SUBLANES, SUBLANES, lw), axis=0)
                dglu_s[pl.ds(row0, rb), lanes] = acc
                return tuple(new)

            sums = lax.fori_loop(0, tm // rb, chunk, tuple(jnp.zeros((SUBLANES, lw), F32) for _ in range(K)))
            for k in range(K):
                accw[SUBLANES * k:SUBLANES * (k + 1), lanes] += sums[k]
        dglu = dglu_s[...]
        da = jnp.concatenate([dglu * sig, dglu * cv[:, :dc] * sig * (1.0 - sig)], axis=1)
        da_ref[...] = da.astype(BF16)
        _accumulate(i, dbin_ref, jnp.sum(da, axis=0, keepdims=True))

        @pl.when(i == nt - 1)
        def _():
            dw_ref[...] = jnp.zeros_like(dw_ref)
            for k in range(K):
                dw_ref[k:k + 1, :] = jnp.sum(accw[SUBLANES * k:SUBLANES * (k + 1), :], axis=0, keepdims=True)

    res = _call(
        body, name, (nt,),
        [_rows(tm, dc), _next_halo(tm, H, dc, S // H), _rows(tm, 2 * dc), _prev_halo(tm, H, 2 * dc),
         _whole(w.shape), pl.BlockSpec(memory_space=pl.ANY)],
        [_rows(tm, 2 * dc), _whole((kp, dc)), _whole((1, 2 * dc))],
        [jax.ShapeDtypeStruct(dproj.shape, BF16), jax.ShapeDtypeStruct((kp, dc), F32),
         jax.ShapeDtypeStruct((1, 2 * dc), F32)],
        [pltpu.VMEM((SUBLANES, tm + H, dc), F32), pltpu.VMEM((SUBLANES, tm + H, dc), F32),
         pltpu.VMEM((SUBLANES * K, dc), F32), pltpu.VMEM((tm, dc), F32)],
        ("arbitrary",), (dconv, dconv, proj, proj, w, dproj), aliases={5: 0}, job=job)
    return res if job is None else (res[:3], res[3:])


def _mix(vln, wsm, bst, out_ref, G, CH, hd):
    for n in range(vln.shape[0] // CH):
        for g in range(G):
            blk = vln[n * CH:(n + 1) * CH, g * hd:(g + 1) * hd].astype(BF16)
            out_ref[n * CH:(n + 1) * CH, g * hd:(g + 1) * hd] = (
                jnp.dot(wsm[g], blk, preferred_element_type=F32) + bst[:, g:g + 1])


def _blocks_to_columns(full, name):
    nb, K, n = full.shape
    tr = _pick(K, max(SUBLANES, (1 << 19) // n), 2 * SUBLANES)

    def body(s_ref, o_ref):
        o_ref[...] = s_ref[...]

    return pl.pallas_call(
        body, name=name, grid=(nb, K // tr),
        in_specs=[pl.BlockSpec((None, tr, n), lambda p, i: (p, i, 0))],
        out_specs=pl.BlockSpec((tr, n), lambda p, i: (i, p)),
        out_shape=jax.ShapeDtypeStruct((K, nb * n), full.dtype),
        compiler_params=_params(("parallel", "parallel")))(full)


def _sgu_fwd(name, proj, a_act, wco, wso, lg, lb, ws, bst, D, ds, job=None):
    S = proj.shape[0]
    dc = a_act.shape[1]
    G, CH, _ = ws.shape
    hd = ds // G
    tm = _pick(S, 256, CH)

    def body(s_ref, gt_ref, a_ref, wco_ref, wso_ref, lg_ref, lb_ref, ws_ref, bst_ref,
             ya_ref, yb_ref, uv_ref, mg_ref, vmix):
        z = _gelu(s_ref[...])
        vln = _layer_norm(z[:, ds:], lg_ref[...], lb_ref[...])
        wsm = [_tril_mask(ws_ref[g]).astype(BF16) for g in range(G)]
        _mix(vln, wsm, bst_ref[...], vmix, G, CH, hd)
        uv = (z[:, :ds] * vmix[...]).astype(BF16)
        uv_ref[...] = uv
        ya = jnp.dot(a_ref[...], wco_ref[...], preferred_element_type=F32)
        yb = jnp.dot(uv, wso_ref[...], preferred_element_type=F32)
        ya_ref[...] = ya.astype(BF16)
        yb_ref[...] = yb.astype(BF16)
        gt = gt_ref[...]
        mg_ref[...] = (jax.nn.sigmoid(gt[:, :D]) * ya + jax.nn.sigmoid(gt[:, D:]) * yb).astype(BF16)

    vec = _whole((1, ds))
    sdb = jax.ShapeDtypeStruct((S, D), BF16)
    res = _call(body, name, (S // tm,),
                [_rows(tm, 2 * ds, 1), _rows(tm, 2 * D, 1), _rows(tm, dc), _whole(wco.shape), _whole(wso.shape),
                 vec, vec, _whole(ws.shape), _whole(bst.shape)],
                [_rows(tm, D), _rows(tm, D), _rows(tm, ds), _rows(tm, D)],
                [sdb, sdb, jax.ShapeDtypeStruct((S, ds), BF16), sdb],
                [pltpu.VMEM((tm, ds), F32)], ("parallel",), (proj, proj, a_act, wco, wso, lg, lb, ws, bst), job=job)
    return res if job is None else (res[:4], res[4:])


def _merge_bwd(name, dmerged, proj, ya, yb, D):
    S = proj.shape[0]
    tm = _pick(S, 256, LANES)

    def body(dm_ref, gt_ref, ya_ref, yb_ref, dya_ref, dyb_ref, dg_ref, dbin_ref):
        i = pl.program_id(0)
        dm = dm_ref[...]
        gt = gt_ref[...]
        sa = jax.nn.sigmoid(gt[:, :D])
        sb = jax.nn.sigmoid(gt[:, D:])
        dya_ref[...] = (dm * sa).astype(BF16)
        dyb_ref[...] = (dm * sb).astype(BF16)
        ya = ya_ref[...].astype(F32)
        yb = yb_ref[...].astype(F32)
        dg = jnp.concatenate([dm * ya * sa * (1.0 - sa), dm * yb * sb * (1.0 - sb)], axis=1)
        dg_ref[...] = dg.astype(BF16)
        _accumulate(i, dbin_ref, jnp.sum(dg, axis=0, keepdims=True))

    row = _rows(tm, D)
    sdb = jax.ShapeDtypeStruct((S, D), BF16)
    return pl.pallas_call(
        body, name=name, grid=(S // tm,),
        in_specs=[row, _rows(tm, 2 * D, 1), row, row],
        out_specs=[row, row, _rows(tm, 2 * D, 1), _whole((1, 2 * D))],
        out_shape=[sdb, sdb, jax.ShapeDtypeStruct((S, 4 * D), BF16), jax.ShapeDtypeStruct((1, 2 * D), F32)],
        compiler_params=_params(("arbitrary",)),
    )(dmerged, proj, ya, yb)


def _sgu_bwd(name, proj, duv, lg, lb, ws, bst, dproj, ds):
    S = proj.shape[0]
    G, CH, _ = ws.shape
    hd = ds // G
    tm = _pick(S, 256, CH)

    def body(s_ref, duv_ref, lg_ref, lb_ref, ws_ref, bst_ref, dproj_any,
             dsin_ref, dws_ref, dbs_ref, dlg_ref, dlb_ref, dbin_ref, vmix, dvln):
        i = pl.program_id(0)
        z, pull_gelu = jax.vjp(_gelu, s_ref[...])
        u = z[:, :ds]
        vln, pull_ln = jax.vjp(_layer_norm, z[:, ds:], lg_ref[...], lb_ref[...])
        wsm = [_tril_mask(ws_ref[g]).astype(BF16) for g in range(G)]
        _mix(vln, wsm, bst_ref[...], vmix, G, CH, hd)
        duv = duv_ref[...]
        du = duv * vmix[...]
        dvmix = duv * u
        for g in range(G):
            dws_g = jnp.zeros((CH, CH), F32)
            dbs_g = jnp.zeros((CH, 1), F32)
            for n in range(tm // CH):
                dblk = dvmix[n * CH:(n + 1) * CH, g * hd:(g + 1) * hd]
                vblk = vln[n * CH:(n + 1) * CH, g * hd:(g + 1) * hd].astype(BF16)
                dvln[n * CH:(n + 1) * CH, g * hd:(g + 1) * hd] = lax.dot_general(
                    wsm[g], dblk.astype(BF16), TN, preferred_element_type=F32)
                dws_g = dws_g + lax.dot_general(dblk.astype(BF16), vblk, NT, preferred_element_type=F32)
                dbs_g = dbs_g + jnp.sum(dblk, axis=1, keepdims=True)
            dws_g = _tril_mask(dws_g)
            dbs_g = jnp.broadcast_to(dbs_g, (CH, LANES))

            @pl.when(i == 0)
            def _():
                dws_ref[g] = dws_g
                dbs_ref[g] = dbs_g

            @pl.when(i > 0)
            def _():
                dws_ref[g] += dws_g
                dbs_ref[g] += dbs_g

        dv, dlg, dlb = pull_ln(dvln[...])
        (dsin,) = pull_gelu(jnp.concatenate([du, dv], axis=1))
        dsin_ref[...] = dsin.astype(BF16)
        _accumulate(i, dlg_ref, dlg)
        _accumulate(i, dlb_ref, dlb)
        _accumulate(i, dbin_ref, jnp.sum(dsin, axis=0, keepdims=True))

    vec = _whole((1, ds))
    vsd = jax.ShapeDtypeStruct((1, ds), F32)
    return pl.pallas_call(
        body, name=name, grid=(S // tm,),
        in_specs=[_rows(tm, 2 * ds, 1), _rows(tm, ds), vec, vec, _whole(ws.shape), _whole(bst.shape),
                  pl.BlockSpec(memory_space=pl.ANY)],
        out_specs=[_rows(tm, 2 * ds, 1), _whole((G, CH, CH)), _whole((G, CH, LANES)), vec, vec, _whole((1, 2 * ds))],
        out_shape=[jax.ShapeDtypeStruct(dproj.shape, BF16), jax.ShapeDtypeStruct((G, CH, CH), F32),
                   jax.ShapeDtypeStruct((G, CH, LANES), F32), vsd, vsd, jax.ShapeDtypeStruct((1, 2 * ds), F32)],
        scratch_shapes=[pltpu.VMEM((tm, ds), F32), pltpu.VMEM((tm, ds), F32)],
        input_output_aliases={6: 0},
        compiler_params=_params(("arbitrary",)),
    )(proj, duv, lg, lb, ws, bst, dproj)


def _silu_mul(val, gt):
    return jax.nn.silu(gt) * val


def _rotation_slots(offs):
    slot = {0: 0}
    for r in sorted({o % SUBLANES for o in offs} - {0}):
        slot[r] = len(slot)
    return slot


def _fill_plane_rotations(rot, slot):
    n = rot.shape[2]
    for r, s in slot.items():
        if r:
            rot[:, s, 0:n - SUBLANES, :] = rot[:, 0, r:r + n - SUBLANES, :]


def _ffn_tiles(S, Fh, rows=256):
    return _pick(S, rows, LANES), _pick(Fh, 1408, LANES)


def _ffn_fwd(name, up, w, b):
    _, S, Fh = up.shape
    K = w.shape[1]
    H = FFN_HALO
    tm, cw = _ffn_tiles(S, Fh, rows=512)
    r = tm // H

    offs = [H - (K - 1) + k for k in range(K)]
    slot = _rotation_slots(offs)
    rb = _pick(tm, 64, 2 * SUBLANES)
    lw = min(LANES, cw)

    def body(cur_ref, prev_ref, w_ref, b_ref, act_ref, rot):
        i = pl.program_id(1)
        rot[:, 0, 0:H, :] = jnp.where(i > 0, prev_ref[...], 0.0)
        rot[:, 0, H:, :] = cur_ref[...]
        _fill_plane_rotations(rot, slot)

        def chunk(row0):
            for l0 in range(0, cw, lw):
                lanes = slice(l0, l0 + lw)
                c2 = []
                for pln in range(2):
                    acc = jnp.broadcast_to(b_ref[pln, :, lanes], (rb, lw))
                    for k in range(K):
                        a, rr = divmod(offs[k], SUBLANES)
                        acc = acc + w_ref[pln, k:k + 1, lanes] * rot[pln, slot[rr], pl.ds(row0 + SUBLANES * a, rb), lanes]
                    c2.append(acc)
                act_ref[pl.ds(row0, rb), lanes] = _silu_mul(c2[0], c2[1]).astype(BF16)

        _for_chunks(tm, rb, chunk)

    return pl.pallas_call(
        body, name=name, grid=(Fh // cw, S // tm),
        in_specs=[pl.BlockSpec((2, tm, cw), lambda j, i: (0, i, j)),
                  pl.BlockSpec((2, H, cw), lambda j, i: (0, jnp.maximum(i * r - 1, 0), j)),
                  pl.BlockSpec((2, K, cw), lambda j, i: (0, 0, j)),
                  pl.BlockSpec((2, 1, cw), lambda j, i: (0, 0, j))],
        out_specs=pl.BlockSpec((tm, cw), lambda j, i: (i, j)),
        out_shape=jax.ShapeDtypeStruct((S, Fh), BF16),
        scratch_shapes=[pltpu.VMEM((2, len(slot), tm + H, cw), F32)],
        compiler_params=_params(("parallel", "parallel")),
    )(up, up, w, b)


def _ffn_bwd(name, up, dact, w, b, job=None):
    _, S, Fh = up.shape
    K = w.shape[1]
    H = FFN_HALO
    tm, cw = _ffn_tiles(S, Fh)
    r = tm // H
    nt = S // tm
    nhb = S // H
    te = tm + H

    offs_x = [H - (K - 1) + k for k in range(K)]
    offs_d = [K - 1 - k for k in range(K)]
    slot_x = _rotation_slots(offs_x)
    slot_d = _rotation_slots(offs_d)
    rb = _pick(tm, 64, 2 * SUBLANES)
    rbe = _pick(te, 96, SUBLANES)
    lw = min(LANES, cw)

    def body(cur_ref, prev_ref, next_ref, d_ref, dn_ref, w_ref, b_ref, dup_ref, dwb_ref, rotx, dext, rotd, accw):
        i = pl.program_id(1)
        rotx[:, 0, 0:H, :] = jnp.where(i > 0, prev_ref[...], 0.0)
        rotx[:, 0, H:H + tm, :] = cur_ref[...]
        rotx[:, 0, H + tm:, :] = jnp.where(i < nt - 1, next_ref[...], 0.0)
        dext[0:tm, :] = d_ref[...]
        dext[tm:, :] = jnp.where(i < nt - 1, dn_ref[...], 0.0)
        _fill_plane_rotations(rotx, slot_x)

        def chunk_e(row0):
            for l0 in range(0, cw, lw):
                lanes = slice(l0, l0 + lw)
                c2 = []
                for pln in range(2):
                    acc = jnp.broadcast_to(b_ref[pln, :, lanes], (rbe, lw))
                    for k in range(K):
                        a, rr = divmod(offs_x[k], SUBLANES)
                        acc = acc + w_ref[pln, k:k + 1, lanes] * rotx[pln, slot_x[rr], pl.ds(row0 + SUBLANES * a, rbe), lanes]
                    c2.append(acc)
                _, pull = jax.vjp(_silu_mul, c2[0], c2[1])
                dval, dgt = pull(dext[pl.ds(row0, rbe), lanes])
                rotd[0, 0, pl.ds(row0, rbe), lanes] = dval
                rotd[1, 0, pl.ds(row0, rbe), lanes] = dgt

        _for_chunks(te, rbe, chunk_e)
        _fill_plane_rotations(rotd, slot_d)

        @pl.when(i == 0)
        def _():
            accw[...] = jnp.zeros_like(accw)

        def chunk(row0):
            for l0 in range(0, cw, lw):
                lanes = slice(l0, l0 + lw)
                for pln in range(2):
                    dcur = rotd[pln, 0, pl.ds(row0, rb), lanes]
                    dup = jnp.zeros((rb, lw), F32)
                    for k in range(K):
                        a, rr = divmod(offs_d[k], SUBLANES)
                        dup = dup + w_ref[pln, k:k + 1, lanes] * rotd[pln, slot_d[rr], pl.ds(row0 + SUBLANES * a, rb), lanes]
                        a, rr = divmod(offs_x[k], SUBLANES)
                        prod = dcur * rotx[pln, slot_x[rr], pl.ds(row0 + SUBLANES * a, rb), lanes]
                        accw[pln, SUBLANES * k:SUBLANES * (k + 1), lanes] += jnp.sum(
                            prod.reshape(rb // SUBLANES, SUBLANES, lw), axis=0)
                    accw[pln, SUBLANES * K:SUBLANES * (K + 1), lanes] += jnp.sum(
                        dcur.reshape(rb // SUBLANES, SUBLANES, lw), axis=0)
                    dup_ref[pln, pl.ds(row0, rb), lanes] = dup.astype(BF16)

        _for_chunks(tm, rb, chunk)

        @pl.when(i == nt - 1)
        def _():
            dwb_ref[...] = jnp.zeros_like(dwb_ref)
            for pln in range(2):
                for k in range(K + 1):
                    dwb_ref[pln, k:k + 1, :] = jnp.sum(accw[pln, SUBLANES * k:SUBLANES * (k + 1), :], axis=0,
                                                       keepdims=True)

    res = _call(
        body, name, (Fh // cw, nt),
        [pl.BlockSpec((2, tm, cw), lambda j, i: (0, i, j)),
         pl.BlockSpec((2, H, cw), lambda j, i: (0, jnp.maximum(i * r - 1, 0), j)),
         pl.BlockSpec((2, H, cw), lambda j, i: (0, jnp.minimum((i + 1) * r, nhb - 1), j)),
         pl.BlockSpec((tm, cw), lambda j, i: (i, j)),
         pl.BlockSpec((H, cw), lambda j, i: (jnp.minimum((i + 1) * r, nhb - 1), j)),
         pl.BlockSpec((2, K, cw), lambda j, i: (0, 0, j)),
         pl.BlockSpec((2, 1, cw), lambda j, i: (0, 0, j))],
        [pl.BlockSpec((2, tm, cw), lambda j, i: (0, i, j)), pl.BlockSpec((2, SUBLANES, cw), lambda j, i: (0, 0, j))],
        [jax.ShapeDtypeStruct((2, S, Fh), BF16), jax.ShapeDtypeStruct((2, SUBLANES, Fh), F32)],
        [pltpu.VMEM((2, len(slot_x), tm + 2 * H, cw), F32), pltpu.VMEM((te, cw), F32),
         pltpu.VMEM((2, len(slot_d), te, cw), F32), pltpu.VMEM((2, SUBLANES * (K + 1), cw), F32)],
        ("parallel", "arbitrary"), (up, up, up, dact, dact, w, b), job=job)
    return res if job is None else (res[:2], res[2:])


def _rms(x, g):
    return x * lax.rsqrt(jnp.mean(x * x, axis=-1, keepdims=True) + EPS) * g


def _gate_bwd_tile(i, dx, out_ref, gate_ref, dout_ref, dgate_ref):
    dout_ref[...] = (dx * gate_ref[...]).astype(BF16)
    _accumulate(i, dgate_ref, jnp.sum(dx * out_ref[...].astype(F32), axis=0, keepdims=True))


def _final(name, x2, target, gf, out, gate):
    S, D = x2.shape
    tm = _pick(S, 256, LANES)

    def body(x_ref, t_ref, g_ref, o_ref, gate_ref, dx_ref, dg_ref, loss_ref, do_ref, dgate_ref):
        i = pl.program_id(0)
        y, pull = jax.vjp(_rms, x_ref[...], g_ref[...])
        e = y - t_ref[...]
        dx, dg = pull(e / D)
        dx_ref[...] = dx
        _accumulate(i, dg_ref, dg)
        part = 0.5 * jnp.sum(jnp.mean(jnp.square(e), axis=-1, keepdims=True), axis=0, keepdims=True)
        _accumulate(i, loss_ref, jnp.broadcast_to(part, (SUBLANES, LANES)))
        _gate_bwd_tile(i, dx, o_ref, gate_ref, do_ref, dgate_ref)

    row = _rows(tm, D)
    vec = _whole((1, D))
    vsd = jax.ShapeDtypeStruct((1, D), F32)
    return pl.pallas_call(
        body, name=name, grid=(S // tm,), in_specs=[row, row, vec, row, vec],
        out_specs=[row, vec, _whole((SUBLANES, LANES)), row, vec],
        out_shape=[jax.ShapeDtypeStruct((S, D), F32), vsd, jax.ShapeDtypeStruct((SUBLANES, LANES), F32),
                   jax.ShapeDtypeStruct((S, D), BF16), vsd],
        compiler_params=_params(("arbitrary",)),
    )(x2, target, gf, out, gate)


def _ada_fwd(name, c_pad, w_ada, b_cols):
    nb, D = c_pad.shape
    n = w_ada.shape[1]
    tn = _pick(n, 1024, LANES)

    def body(c_ref, w_ref, b_ref, o_ref):
        o_ref[...] = jnp.dot(jax.nn.silu(c_ref[...]).astype(BF16), w_ref[...].astype(BF16),
                             preferred_element_type=F32) + b_ref[...]

    return pl.pallas_call(
        body, name=name, grid=(n // tn,),
        in_specs=[_whole((nb, D)), pl.BlockSpec((D, tn), lambda j: (0, j)), pl.BlockSpec((1, tn), lambda j: (0, j))],
        out_specs=pl.BlockSpec((nb, tn), lambda j: (0, j)),
        out_shape=jax.ShapeDtypeStruct((nb, n), F32), compiler_params=_params(("parallel",)),
    )(c_pad, w_ada, b_cols)


def _ada_wgrad(name, c_t, dmod_cols):
    D, nb = c_t.shape
    n = dmod_cols.shape[1]
    tr = _pick(D, 256, SUBLANES)

    def body(c_ref, d_ref, o_ref):
        ca = jax.nn.silu(c_ref[...])
        acc = ca[:, 0:1] * d_ref[0:1, :]
        for b in range(1, nb):
            acc = acc + ca[:, b:b + 1] * d_ref[b:b + 1, :]
        o_ref[...] = acc

    return pl.pallas_call(
        body, name=name, grid=(D // tr,),
        in_specs=[pl.BlockSpec((tr, nb), lambda i: (i, 0)), _whole((nb, n))],
        out_specs=pl.BlockSpec((tr, n), lambda i: (i, 0)),
        out_shape=jax.ShapeDtypeStruct((D, n), F32), compiler_params=_params(("parallel",)),
    )(c_t, dmod_cols)


def kernel(x, c, w_ada, b_ada, norm1_g, w_in, b_in, conv_dw_w, conv_dw_b, conv_ln_g, conv_ln_b, w_conv_out, sgu_ln_g, sgu_ln_b, w_spatial, b_spatial, w_sgu_out, w_out, norm2_g, w_up, ffn_dw_w, ffn_dw_b, w_down, final_g, loss_target, m_w_ada, m_b_ada, m_norm1_g, m_w_in, m_b_in, m_conv_dw_w, m_conv_dw_b, m_conv_ln_g, m_conv_ln_b, m_w_conv_out, m_sgu_ln_g, m_sgu_ln_b, m_w_spatial, m_b_spatial, m_w_sgu_out, m_w_out, m_norm2_g, m_w_up, m_ffn_dw_w, m_ffn_dw_b, m_w_down, m_final_g, v_w_ada, v_b_ada, v_norm1_g, v_w_in, v_b_in, v_conv_dw_w, v_conv_dw_b, v_conv_ln_g, v_conv_ln_b, v_w_conv_out, v_sgu_ln_g, v_sgu_ln_b, v_w_spatial, v_b_spatial, v_w_sgu_out, v_w_out, v_norm2_g, v_w_up, v_ffn_dw_w, v_ffn_dw_b, v_w_down, v_final_g):
    S, D = x.shape[1], x.shape[2]
    dc = w_conv_out.shape[1]
    ds = w_sgu_out.shape[1]
    G, CH = w_spatial.shape[1], w_spatial.shape[2]
    KC = conv_dw_w.shape[1]
    KF = ffn_dw_w.shape[1]
    F2 = ffn_dw_b.shape[1]
    Fh = F2 // 2
    n_ada = w_ada.shape[2]
    n_up = w_up.shape[2]
    ax, ay, ac = _axes()
    chip = 2 * ax + ay
    me = 2 * chip + ac
    c_idx = jnp.reshape(ac, (1,)).astype(jnp.int32)
    p_idx = jnp.reshape(chip, (1,)).astype(jnp.int32)

    xs = x[0]
    tgt = loss_target[0]

    g1 = _allgather8(_pack([c[0], conv_dw_w[0], ffn_dw_w[0]]), "gather_small_in")
    n_cw, n_fw = conv_dw_w.shape[2], ffn_dw_w.shape[2]
    parts = [_unpack(g1[2 * q], [(D,), conv_dw_w.shape[1:], ffn_dw_w.shape[1:]]) for q in range(N_CHIPS)]
    c_all = jnp.stack([_unpack(g1[d], [(D,)])[0] for d in range(N_DEV)])
    cw_full = jnp.concatenate([pt[1] for pt in parts], axis=1)
    fw_full = jnp.concatenate([pt[2] for pt in parts], axis=1)

    b_cols = lax.dynamic_slice(b_ada, (0, chip * n_ada), (1, n_ada))
    c_pad = jnp.concatenate([c_all, jnp.zeros_like(c_all)], axis=0)
    mod_blk = _ada_fwd("ada_fwd", c_pad, w_ada[0], b_cols)[:N_DEV]
    g2 = _allgather8(_pack([mod_blk]), "gather_mod")
    mod_all = jnp.concatenate([_unpack(g2[2 * q], [(N_DEV, n_ada)])[0] for q in range(N_CHIPS)], axis=1)
    mod = lax.dynamic_slice(mod_all, (me, 0), (1, 6 * D))
    shift1, scale1, gate1, shift2, scale2, gate2 = [mod[:, k * D:(k + 1) * D] for k in range(6)]

    shards = [w_in[0], w_conv_out[0], w_sgu_out[0], w_out[0], w_up[0], w_down[0]]
    names = ["in", "conv_out", "sgu_out", "out", "up", "down"]
    blk = {nm: _cast_into_block(s, p_idx, "cast_" + nm) for s, nm in zip(shards, names)}
    pc_idx = jnp.concatenate([p_idx, c_idx])

    h1 = _modnorm_fwd("modnorm1", xs, norm1_g, scale1, shift1)
    (proj,), (win_xy,) = _mm_fwd_block("proj_own", h1, blk["in"], b_in, p_idx, 0,
                                       job=_job_gather([blk["in"]], rel=(0, 1), fresh=True))
    (proj,), (win_d,) = _mm_fwd_block("proj_x", h1, win_xy, b_in, p_idx, 2, into=proj,
                                      job=_job_gather([blk["in"]], rel=(2,), fresh=True))
    (proj,), (wco_f, wso_f) = _mm_fwd_block("proj_y", h1, win_xy, b_in, p_idx, 1, into=proj,
                                            job=_job_gather([blk["conv_out"], blk["sgu_out"]]))
    (proj,), (wout_f,) = _mm_fwd_block("proj_diag", h1, win_d, b_in, p_idx, 3, into=proj,
                                       job=_job_gather([blk["out"]]))
    win_f = _copy_block(_copy_block(win_xy, blk["in"], p_idx, 0, "fill_w_in_own"), win_d, p_idx, 3, "fill_w_in_diag")
    wout_r = wout_f.reshape(-1, wout_f.shape[2])
    wco_c = _blocks_to_columns(wco_f, "columns_conv_out")
    wso_c = _blocks_to_columns(wso_f, "columns_sgu_out")
    (conv, a_act), (wup_xy,) = _conv_fwd("conv_fwd", proj, cw_full, conv_dw_b, conv_ln_g, conv_ln_b, dc,
                                         job=_job_gather([blk["up"]], rel=(0, 1)))
    bst = jnp.transpose(b_spatial[0])
    (ya, yb, uv, merged), (wup_f,) = _sgu_fwd("sgu_fwd", proj, a_act, wco_c, wso_c, sgu_ln_g, sgu_ln_b, w_spatial[0],
                                              bst, D, ds, job=_job_gather([wup_xy], rel=(2,)))
    out1, x1 = _mm_fwd_rows("out1", merged, wout_r, xs, gate1)
    h2 = _modnorm_fwd("modnorm2", x1, norm2_g, scale2, shift2)
    up, (wdown_f,) = _mm_fwd("up", h2, wup_f, planes=2, job=_job_gather([blk["down"]]))
    wdown_r = wdown_f.reshape(-1, wdown_f.shape[2])
    fw2 = jnp.stack([fw_full[:, :Fh], fw_full[:, Fh:]])
    fb2 = jnp.stack([ffn_dw_b[:, :Fh], ffn_dw_b[:, Fh:]])
    act = _ffn_fwd("ffn_fwd", up, fw2, fb2)
    out2, x2 = _mm_fwd_rows("out2", act, wdown_r, x1, gate2)
    dx2, d_final_g, loss_blk, dout2, d_gate2 = _final("final", x2, tgt, final_g.reshape(1, D), out2, gate2)
    loss = lax.psum(loss_blk[0, 0], ("x", "y", "c"))

    def add_cores(nm, g, r1):
        return _add_own_half(g, r1, pc_idx, "add_cores_" + nm)

    def add_chips(nm, g, r1, r2):
        return _add_chips(g, r1, r2, pc_idx, "add_chips_" + nm)

    g_wdown = _mm_wgrad_rows("wgrad_down", act, dout2, w_down.shape[1])
    dact, (r1_down,) = _mm_dgrad_rows("dgrad_down", dout2, wdown_r, job=_job_swap_halves([g_wdown]))
    s_down = add_cores("down", g_wdown, r1_down)
    (dup, d_ffn), (r2_down,) = _ffn_bwd("ffn_bwd", up, dact, fw2, fb2, job=_job_scatter_blocks([s_down]))
    h_down = add_chips("down", g_wdown, r1_down, r2_down)
    g_wup = _mm_wgrad_cols("wgrad_up", h2, dup, n_up)
    dh2, (r1_up,) = _mm_dgrad_cols("dgrad_up", dup, wup_f, job=_job_swap_halves([g_wup]))
    s_up = add_cores("up", g_wup, r1_up)
    dx1, d_norm2, d_scale2, d_shift2, dout1, d_gate1 = _modnorm_bwd(
        "modnorm2_bwd", x1, dh2, dx2, norm2_g, scale2, shift2, gated=(out1, gate1))
    g_wout = _mm_wgrad_rows("wgrad_out", merged, dout1, w_out.shape[1])
    dmerged = _mm_dgrad_rows("dgrad_out", dout1, wout_r)
    dya, dyb, dproj, dbin_g = _merge_bwd("merge_bwd", dmerged, proj, ya, yb, D)
    g_wco = _mm_wgrad_cols("wgrad_conv_out", a_act, dya[None], w_conv_out.shape[2])
    g_wso = _mm_wgrad_cols("wgrad_sgu_out", uv, dyb[None], w_sgu_out.shape[2])
    da_act, (r1_out, r1_co, r1_so) = _mm_dgrad_rows("dgrad_conv_out", dya, wco_c,
                                                    job=_job_swap_halves([g_wout, g_wco, g_wso]))
    s_out = add_cores("out", g_wout, r1_out)
    s_co = add_cores("conv_out", g_wco, r1_co)
    s_so = add_cores("sgu_out", g_wso, r1_so)
    duv = _mm_dgrad_rows("dgrad_sgu_out", dyb, wso_c)
    dproj, d_ws, d_bs, d_sgu_g, d_sgu_b, dbin_s = _sgu_bwd("sgu_bwd", proj, duv, sgu_ln_g, sgu_ln_b, w_spatial[0], bst,
                                                           dproj, ds)
    dconv, d_cln_g, d_cln_b, d_conv_b = _conv_bwd_ln("conv_ln_bwd", conv, da_act, conv_ln_g, conv_ln_b)
    d_fw = jnp.concatenate([d_ffn[0, :KF], d_ffn[1, :KF]], axis=1)
    d_fb = jnp.concatenate([d_ffn[0, KF:KF + 1], d_ffn[1, KF:KF + 1]], axis=1)
    early = [d_sgu_g, d_sgu_b, d_ws, d_bs[:, :, 0], d_norm2, d_fb, d_final_g, d_fw]
    (dproj, d_cw, dbin_a), (r2_up, g_early) = _conv_bwd(
        "conv_bwd", dconv, proj, cw_full, dproj, dc,
        job=_merge_jobs([_job_scatter_blocks([s_up]), _job_allgather8(_pack(early))]))
    h_up = add_chips("up", g_wup, r1_up, r2_up)
    g_win, (r2_out, r2_co, r2_so, sib_up, sib_down) = _mm_wgrad_cols(
        "wgrad_in", h1, dproj[None], w_in.shape[2],
        job=_merge_jobs([_job_scatter_blocks([s_out, s_co, s_so]), _job_to_sibling([h_up, h_down])]))
    h_out = add_chips("out", g_wout, r1_out, r2_out)
    h_co = add_chips("conv_out", g_wco, r1_co, r2_co)
    h_so = add_chips("sgu_out", g_wso, r1_so, r2_so)
    dh1, (r1_in, sib_out, sib_co, sib_so) = _mm_dgrad_cols(
        "dgrad_in_a", dproj[None], win_f, row_tiles=(0.0, 0.5),
        job=_merge_jobs([_job_swap_halves([g_win]), _job_to_sibling([h_out, h_co, h_so])]))
    s_in = add_cores("in", g_win, r1_in)
    dh1, (r2_in,) = _mm_dgrad_cols("dgrad_in_b", dproj[None], win_f, row_tiles=(0.5, 0.5), fill_into=dh1,
                                   job=_job_scatter_blocks([s_in]))
    h_in = add_chips("in", g_win, r1_in, r2_in)
    dxs, d_norm1, d_scale1, d_shift1 = _modnorm_bwd("modnorm1_bwd", xs, dh1, dx1, norm1_g, scale1, shift1)

    d_mod = jnp.concatenate([d_shift1, d_scale1, d_gate1, d_shift2, d_scale2, d_gate2], axis=1)
    d_b_in = jnp.concatenate([dbin_a, dbin_s, dbin_g], axis=1)
    late = [d_mod, d_norm1, d_b_in, d_conv_b, d_cln_g, d_cln_b, d_cw[:KC]]
    g3, sib_in = _run_job(_merge_jobs([_job_allgather8(_pack(late)), _job_to_sibling([h_in])]), "gather_small_grads")
    big_halves = {"w_in": (h_in, sib_in), "w_conv_out": (h_co, sib_co), "w_sgu_out": (h_so, sib_so),
                  "w_out": (h_out, sib_out), "w_up": (h_up, sib_up), "w_down": (h_down, sib_down)}
    g_b_ada, g_norm1, g_b_in, g_conv_b, g_cln_g, g_cln_b, g_cw_full = _unpack(
        _sum8(g3, "sum_small_grads"), [a.shape for a in late])
    g_sgu_g, g_sgu_b, g_ws, g_bs, g_norm2, g_fb, g_final, g_fw_full = _unpack(
        _sum8(g_early, "sum_early_grads"), [a.shape for a in early])
    g_cw = lax.dynamic_slice(g_cw_full, (0, chip * n_cw), (KC, n_cw))
    g_fw = lax.dynamic_slice(g_fw_full, (0, chip * n_fw), (KF, n_fw))
    dmod_all = g3.reshape(N_DEV, -1)[:, :6 * D]
    dmod_cols = lax.dynamic_slice(dmod_all, (0, chip * n_ada), (N_DEV, n_ada))
    g_wada = _ada_wgrad("ada_wgrad", jnp.transpose(c_all), dmod_cols)

    grads = {
        "w_ada": g_wada[None], "b_ada": g_b_ada, "norm1_g": g_norm1, "b_in": g_b_in,
        "conv_dw_w": g_cw[None], "conv_dw_b": g_conv_b, "conv_ln_g": g_cln_g, "conv_ln_b": g_cln_b,
        "sgu_ln_g": g_sgu_g, "sgu_ln_b": g_sgu_b, "w_spatial": g_ws[None],
        "b_spatial": g_bs[None], "norm2_g": g_norm2, "ffn_dw_w": g_fw[None], "ffn_dw_b": g_fb,
        "final_g": g_final.reshape(D),
    }
    weights = dict(w_ada=w_ada, b_ada=b_ada, norm1_g=norm1_g, w_in=w_in, b_in=b_in, conv_dw_w=conv_dw_w, conv_dw_b=conv_dw_b, conv_ln_g=conv_ln_g, conv_ln_b=conv_ln_b, w_conv_out=w_conv_out, sgu_ln_g=sgu_ln_g, sgu_ln_b=sgu_ln_b, w_spatial=w_spatial, b_spatial=b_spatial, w_sgu_out=w_sgu_out, w_out=w_out, norm2_g=norm2_g, w_up=w_up, ffn_dw_w=ffn_dw_w, ffn_dw_b=ffn_dw_b, w_down=w_down, final_g=final_g)
    m_in = dict(w_ada=m_w_ada, b_ada=m_b_ada, norm1_g=m_norm1_g, w_in=m_w_in, b_in=m_b_in, conv_dw_w=m_conv_dw_w, conv_dw_b=m_conv_dw_b, conv_ln_g=m_conv_ln_g, conv_ln_b=m_conv_ln_b, w_conv_out=m_w_conv_out, sgu_ln_g=m_sgu_ln_g, sgu_ln_b=m_sgu_ln_b, w_spatial=m_w_spatial, b_spatial=m_b_spatial, w_sgu_out=m_w_sgu_out, w_out=m_w_out, norm2_g=m_norm2_g, w_up=m_w_up, ffn_dw_w=m_ffn_dw_w, ffn_dw_b=m_ffn_dw_b, w_down=m_w_down, final_g=m_final_g)
    v_in = dict(w_ada=v_w_ada, b_ada=v_b_ada, norm1_g=v_norm1_g, w_in=v_w_in, b_in=v_b_in, conv_dw_w=v_conv_dw_w, conv_dw_b=v_conv_dw_b, conv_ln_g=v_conv_ln_g, conv_ln_b=v_conv_ln_b, w_conv_out=v_w_conv_out, sgu_ln_g=v_sgu_ln_g, sgu_ln_b=v_sgu_ln_b, w_spatial=v_w_spatial, b_spatial=v_b_spatial, w_sgu_out=v_w_sgu_out, w_out=v_w_out, norm2_g=v_norm2_g, w_up=v_w_up, ffn_dw_w=v_ffn_dw_w, ffn_dw_b=v_ffn_dw_b, w_down=v_w_down, final_g=v_final_g)
    order = list(weights.keys())
    large = ["w_ada", "w_in", "w_conv_out", "w_sgu_out", "w_out", "w_up", "w_down"]
    little = [n for n in order if n not in large]
    delta, new_m, new_v = {}, {}, {}
    for n in large:
        shp = weights[n].shape
        two = (shp[1], shp[2])
        if n in big_halves:
            g_, d_, m_, v_ = _adamw_halves(weights[n].reshape(two), big_halves[n][0], big_halves[n][1],
                                           m_in[n].reshape(two), v_in[n].reshape(two), c_idx, "adamw_" + n)
            grads[n] = g_.reshape(shp)
        else:
            d_, m_, v_ = _adamw(weights[n].reshape(two), grads[n].reshape(two), m_in[n].reshape(two),
                                v_in[n].reshape(two), "adamw_" + n)
        delta[n], new_m[n], new_v[n] = d_.reshape(shp), m_.reshape(shp), v_.reshape(shp)
    shapes = [weights[n].shape for n in little]
    d_, m_, v_ = _adamw(_pack([weights[n] for n in little]), _pack([grads[n] for n in little]),
                        _pack([m_in[n] for n in little]), _pack([v_in[n] for n in little]), "adamw_small")
    for n, dd, mm, vv in zip(little, _unpack(d_, shapes), _unpack(m_, shapes), _unpack(v_, shapes)):
        delta[n], new_m[n], new_v[n] = dd, mm, vv
    grad_out = [grads[n].reshape(weights[n].shape) for n in order]
    return (loss, dxs[None], *grad_out, *[delta[n] for n in order], *[new_m[n] for n in order],
            *[new_v[n] for n in order])
```

```python
import functools

import jax
import jax.numpy as jnp
from jax import lax
from jax.experimental import pallas as pl
from jax.experimental.pallas import tpu as pltpu

F32 = jnp.float32
BF16 = jnp.bfloat16
EPS = 1e-6
MESH = pl.DeviceIdType.MESH
N_CHIPS = 4
N_DEV = 8
LANES = 128
SUBLANES = 8
CONV_HALO = 32
FFN_HALO = 8
CONV_CHUNK_ROWS = 32
VMEM_LIMIT_BYTES = 56 * 1024 * 1024
WHOLE_WEIGHT_BYTES = 8 * 1024 * 1024
SMALL_TILE_BYTES = 2 * 1024 * 1024

ADAM_LR = 0.001
ADAM_B1 = 0.9
ADAM_B2 = 0.999
ADAM_EPS = 1e-08
ADAM_WD = 0.01
ADAM_STEP = 10

NN = (((1,), (0,)), ((), ()))
NT = (((1,), (1,)), ((), ()))
TN = (((0,), (0,)), ((), ()))


def _params(sem=None):
    return pltpu.CompilerParams(dimension_semantics=sem, vmem_limit_bytes=VMEM_LIMIT_BYTES)


def _pick(dim, pref, mult):
    best = None
    d = mult
    while d <= min(dim, pref):
        if dim % d == 0:
            best = d
        d += mult
    return dim if best is None else best


def _axes():
    return lax.axis_index("x"), lax.axis_index("y"), lax.axis_index("c")


def _modnorm(x, g, scale, shift):
    r = lax.rsqrt(jnp.mean(x * x, axis=-1, keepdims=True) + EPS)
    return (x * r * g) * (1.0 + scale) + shift


def _layer_norm(x, g, b):
    mu = jnp.mean(x, axis=-1, keepdims=True)
    var = jnp.mean(jnp.square(x - mu), axis=-1, keepdims=True)
    return (x - mu) * lax.rsqrt(var + EPS) * g + b


def _gelu(x):
    return 0.5 * x * (1.0 + lax.erf(x * (0.5 ** 0.5)))


def _ln_silu(x, g, b):
    return jax.nn.silu(_layer_norm(x, g, b))


def _tril_mask(ws):
    n = ws.shape[-1]
    row = lax.broadcasted_iota(jnp.int32, (n, n), 0)
    col = lax.broadcasted_iota(jnp.int32, (n, n), 1)
    return jnp.where(row >= col, ws, 0.0)


def _pack(arrs):
    flat = [a.reshape(-1).astype(F32) for a in arrs]
    total = sum(f.shape[0] for f in flat)
    tile = SUBLANES * LANES
    padded = -(-total // tile) * tile
    if padded > total:
        flat = flat + [jnp.zeros((padded - total,), F32)]
    return jnp.concatenate(flat).reshape(padded // LANES, LANES)


def _unpack(buf, shapes):
    flat = buf.reshape(-1)
    out, off = [], 0
    for s in shapes:
        n = 1
        for d in s:
            n *= d
        out.append(flat[off:off + n].reshape(s))
        off += n
    return out


def _allgather8(buf, name):
    R, L = buf.shape

    def body(in_ref, out_ref, send_sems, recv_sems, local_sem):
        x, y, c = _axes()
        me = 4 * x + 2 * y + c
        mine = pltpu.make_async_copy(in_ref, out_ref.at[me], local_sem)
        mine.start()
        peers = []
        for k in range(1, N_DEV):
            px = 1 - x if k & 4 else x
            py = 1 - y if k & 2 else y
            pc = 1 - c if k & 1 else c
            peers.append((px, py, pc))
        sends = []
        for k, peer in enumerate(peers):
            cp = pltpu.make_async_remote_copy(
                src_ref=in_ref, dst_ref=out_ref.at[me], send_sem=send_sems.at[k], recv_sem=recv_sems.at[k],
                device_id=peer, device_id_type=MESH)
            cp.start()
            sends.append(cp)
        for k, (px, py, pc) in enumerate(peers):
            pltpu.make_async_remote_copy(
                src_ref=in_ref, dst_ref=out_ref.at[4 * px + 2 * py + pc], send_sem=send_sems.at[k],
                recv_sem=recv_sems.at[k], device_id=(px, py, pc), device_id_type=MESH).wait_recv()
        for cp in sends:
            cp.wait_send()
        mine.wait()

    return pl.pallas_call(
        body, name=name,
        out_shape=jax.ShapeDtypeStruct((N_DEV, R, L), buf.dtype),
        in_specs=[pl.BlockSpec(memory_space=pltpu.VMEM)],
        out_specs=pl.BlockSpec(memory_space=pltpu.VMEM),
        scratch_shapes=[pltpu.SemaphoreType.DMA((N_DEV - 1,)), pltpu.SemaphoreType.DMA((N_DEV - 1,)),
                        pltpu.SemaphoreType.DMA],
        compiler_params=pltpu.CompilerParams(vmem_limit_bytes=VMEM_LIMIT_BYTES),
    )(buf)


def _other_chips(x, y):
    return [(1 - x, y), (x, 1 - y), (1 - x, 1 - y)]


def _cast_into_block(shard, p_idx, name):
    K, n = shard.shape
    tr = _pick(K, max(SUBLANES, (1 << 19) // n), 2 * SUBLANES)

    def body(p_ref, s_ref, o_ref):
        o_ref[...] = s_ref[...].astype(BF16)

    return pl.pallas_call(
        body, name=name,
        grid_spec=pltpu.PrefetchScalarGridSpec(
            num_scalar_prefetch=1, grid=(K // tr,),
            in_specs=[pl.BlockSpec((tr, n), lambda i, p: (i, 0))],
            out_specs=pl.BlockSpec((None, tr, n), lambda i, p: (p[0], i, 0))),
        out_shape=jax.ShapeDtypeStruct((N_CHIPS, K, n), BF16),
        compiler_params=_params(("parallel",)),
    )(p_idx, shard)


class _Job:
    def __init__(self, ins, outs, aliases, n_sems, make):
        self.ins, self.outs, self.aliases, self.n_sems, self.make = list(ins), list(outs), list(aliases), n_sems, make


def _merge_jobs(jobs):
    ins, outs, aliases, offs = [], [], [], []
    n_sems = 0
    for jb in jobs:
        offs.append((len(ins), len(outs), n_sems))
        aliases += [(len(ins) + a, len(outs) + b) for a, b in jb.aliases]
        ins += jb.ins
        outs += jb.outs
        n_sems += jb.n_sems

    def make(in_refs, out_refs, send_sems, recv_sems, base=0):
        made = []
        for jb, (oi, oo, os_) in zip(jobs, offs):
            made.append(jb.make(in_refs[oi:oi + len(jb.ins)], out_refs[oo:oo + len(jb.outs)],
                                send_sems, recv_sems, base + os_))

        def start():
            for st, _ in made:
                st()

        def finish():
            for _, fin in made:
                fin()

        return start, finish

    return _Job(ins, outs, aliases, n_sems, make)


def _job_gather(fulls, rel=(0, 1, 2), fresh=False):
    nw = len(fulls)

    def make(in_refs, outs, send_sems, recv_sems, base=0):
        x, y, c = _axes()
        p = 2 * x + y
        chips = _other_chips(x, y)
        srcs = in_refs if fresh else outs

        def rows(w, mine):
            kh = outs[w].shape[1] // 2
            return pl.ds((c if mine else 1 - c) * kh, kh)

        def over_ici(w, j, block):
            qx, qy = chips[j]
            return pltpu.make_async_remote_copy(
                src_ref=srcs[w].at[block, rows(w, True)], dst_ref=outs[w].at[block, rows(w, True)],
                send_sem=send_sems.at[base + 6 * w + j], recv_sem=recv_sems.at[base + 6 * w + j],
                device_id=(qx, qy, c), device_id_type=MESH)

        def over_d2d(w, j, mine):
            qx, qy = chips[j]
            return pltpu.make_async_remote_copy(
                src_ref=outs[w].at[2 * qx + qy, rows(w, mine)], dst_ref=outs[w].at[2 * qx + qy, rows(w, mine)],
                send_sem=send_sems.at[base + 6 * w + 3 + j], recv_sem=recv_sems.at[base + 6 * w + 3 + j],
                device_id=(x, y, 1 - c), device_id_type=MESH)

        def start():
            for w in range(nw):
                for j in rel:
                    over_ici(w, j, p).start()

        def finish():
            for w in range(nw):
                for j in rel:
                    qx, qy = chips[j]
                    over_ici(w, j, 2 * qx + qy).wait_recv()
                    over_d2d(w, j, True).start()
            for w in range(nw):
                for j in rel:
                    over_d2d(w, j, False).wait_recv()
            for w in range(nw):
                for j in rel:
                    over_ici(w, j, p).wait_send()
                    over_d2d(w, j, True).wait_send()

        return start, finish

    return _Job(fulls, [jax.ShapeDtypeStruct(f.shape, f.dtype) for f in fulls],
                [] if fresh else [(w, w) for w in range(nw)], 6 * nw, make)


def _job_simple(ins, outs, n_per, copies_of):
    nw = len(ins)

    def make(in_refs, out_refs, send_sems, recv_sems, base=0):
        x, y, c = _axes()
        copies = []
        for w in range(nw):
            for j, (src, dst, dev) in enumerate(copies_of(w, in_refs[w], out_refs[w], x, y, c)):
                k = base + n_per * w + j
                copies.append(pltpu.make_async_remote_copy(
                    src_ref=src, dst_ref=dst, send_sem=send_sems.at[k], recv_sem=recv_sems.at[k],
                    device_id=dev, device_id_type=MESH))

        def start():
            for cp in copies:
                cp.start()

        def finish():
            for cp in copies:
                cp.wait()

        return start, finish

    return _Job(ins, outs, [], n_per * nw, make)


def _job_swap_halves(grads):
    def copies_of(w, src, dst, x, y, c):
        kh = src.shape[1] // 2
        return [(src.at[:, pl.ds((1 - c) * kh, kh), :], dst, (x, y, 1 - c))]

    outs = [jax.ShapeDtypeStruct((g.shape[0], g.shape[1] // 2, g.shape[2]), g.dtype) for g in grads]
    return _job_simple(grads, outs, 1, copies_of)


def _job_scatter_blocks(sums):
    def copies_of(w, src, dst, x, y, c):
        return [(src.at[2 * qx + qy], dst.at[j], (qx, qy, c)) for j, (qx, qy) in enumerate(_other_chips(x, y))]

    outs = [jax.ShapeDtypeStruct((3,) + s.shape[1:], s.dtype) for s in sums]
    return _job_simple(sums, outs, 3, copies_of)


def _job_to_sibling(arrs):
    def copies_of(w, src, dst, x, y, c):
        return [(src, dst, (x, y, 1 - c))]

    outs = [jax.ShapeDtypeStruct(a.shape, a.dtype) for a in arrs]
    return _job_simple(arrs, outs, 1, copies_of)


def _job_allgather8(buf):
    def make(in_refs, out_refs, send_sems, recv_sems, base=0):
        x, y, c = _axes()
        me = 4 * x + 2 * y + c
        src, dst = in_refs[0], out_refs[0]
        peers = [(1 - x if k & 4 else x, 1 - y if k & 2 else y, 1 - c if k & 1 else c) for k in range(1, N_DEV)]
        mine = pltpu.make_async_copy(src, dst.at[me], send_sems.at[base + N_DEV - 1])

        def to(k):
            return pltpu.make_async_remote_copy(
                src_ref=src, dst_ref=dst.at[me], send_sem=send_sems.at[base + k], recv_sem=recv_sems.at[base + k],
                device_id=peers[k], device_id_type=MESH)

        def of(k):
            px, py, pc = peers[k]
            return pltpu.make_async_remote_copy(
                src_ref=src, dst_ref=dst.at[4 * px + 2 * py + pc], send_sem=send_sems.at[base + k],
                recv_sem=recv_sems.at[base + k], device_id=peers[k], device_id_type=MESH)

        def start():
            mine.start()
            for k in range(N_DEV - 1):
                to(k).start()

        def finish():
            for k in range(N_DEV - 1):
                of(k).wait_recv()
                to(k).wait_send()
            mine.wait()

        return start, finish

    return _Job([buf], [jax.ShapeDtypeStruct((N_DEV,) + buf.shape, buf.dtype)], [], N_DEV, make)


def _run_job(job, name):
    ni, no = len(job.ins), len(job.outs)

    def body(*refs):
        start, finish = job.make(refs[:ni], refs[ni:ni + no], refs[-2], refs[-1])
        start()
        finish()

    any_spec = pl.BlockSpec(memory_space=pl.ANY)
    return pl.pallas_call(
        body, name=name, out_shape=job.outs, in_specs=[any_spec] * ni, out_specs=[any_spec] * no,
        input_output_aliases=dict(job.aliases),
        scratch_shapes=[pltpu.SemaphoreType.DMA((job.n_sems,)), pltpu.SemaphoreType.DMA((job.n_sems,))],
    )(*job.ins)


def _call(body, name, grid, in_specs, out_specs, out_shape, scratch_shapes, semantics, args, aliases=None, job=None,
          prefetch=()):
    in_specs, out_specs, out_shape = list(in_specs), list(out_specs), list(out_shape)
    scratch_shapes = list(scratch_shapes)
    n_pre, n_in, n_out, n_scr = len(prefetch), len(args), len(out_shape), len(scratch_shapes)
    all_aliases = {n_pre + a: b for a, b in (aliases or {}).items()}
    if job is None:
        wrapped, extra_in, semantics = body, [], semantics
    else:
        ni, no = len(job.ins), len(job.outs)

        def wrapped(*refs):
            pre, refs = refs[:n_pre], refs[n_pre:]
            ins, cins = refs[:n_in], refs[n_in:n_in + ni]
            outs, couts = refs[n_in + ni:n_in + ni + n_out], refs[n_in + ni + n_out:n_in + ni + n_out + no]
            scr = refs[n_in + ni + n_out + no:n_in + ni + n_out + no + n_scr]
            start, finish = job.make(cins, couts, refs[-2], refs[-1])
            first = functools.reduce(jnp.logical_and, [pl.program_id(a) == 0 for a in range(len(grid))])
            last = functools.reduce(jnp.logical_and, [pl.program_id(a) == grid[a] - 1 for a in range(len(grid))])
            pl.when(first)(start)
            body(*pre, *ins, *outs, *scr)
            pl.when(last)(finish)

        any_spec = pl.BlockSpec(memory_space=pl.ANY)
        for a, b in job.aliases:
            all_aliases[n_pre + n_in + a] = n_out + b
        in_specs, out_specs, out_shape = in_specs + [any_spec] * ni, out_specs + [any_spec] * no, out_shape + job.outs
        scratch_shapes = scratch_shapes + [pltpu.SemaphoreType.DMA((job.n_sems,)), pltpu.SemaphoreType.DMA((job.n_sems,))]
        extra_in, semantics = job.ins, tuple("arbitrary" for _ in grid)
    if n_pre:
        return pl.pallas_call(
            wrapped, name=name,
            grid_spec=pltpu.PrefetchScalarGridSpec(num_scalar_prefetch=n_pre, grid=grid, in_specs=in_specs,
                                                   out_specs=out_specs, scratch_shapes=scratch_shapes),
            out_shape=out_shape, input_output_aliases=all_aliases,
            compiler_params=_params(semantics))(*prefetch, *args, *extra_in)
    return pl.pallas_call(
        wrapped, name=name, grid=grid, in_specs=in_specs, out_specs=out_specs, out_shape=out_shape,
        scratch_shapes=scratch_shapes, input_output_aliases=all_aliases,
        compiler_params=_params(semantics))(*args, *extra_in)


def _add_own_half(g, r, pc_idx, name):
    nb, K, n = g.shape
    kh = K // 2
    tr = _pick(kh, max(SUBLANES, (1 << 19) // n), 2 * SUBLANES)
    per = kh // tr

    def body(pc_ref, g_ref, r_ref, o_ref):
        o_ref[...] = (g_ref[...] + r_ref[...]).astype(BF16)

    def other(b, pc):
        return jnp.bitwise_xor(pc[0], b + 1)

    return pl.pallas_call(
        body, name=name,
        grid_spec=pltpu.PrefetchScalarGridSpec(
            num_scalar_prefetch=1, grid=(nb - 1, per),
            in_specs=[pl.BlockSpec((None, tr, n), lambda b, i, pc: (other(b, pc), pc[1] * per + i, 0)),
                      pl.BlockSpec((None, tr, n), lambda b, i, pc: (other(b, pc), i, 0))],
            out_specs=pl.BlockSpec((None, tr, n), lambda b, i, pc: (other(b, pc), i, 0))),
        out_shape=jax.ShapeDtypeStruct((nb, kh, n), BF16),
        compiler_params=_params(("parallel", "parallel")),
    )(pc_idx, g, r)


def _add_chips(g, r1, r2, pc_idx, name):
    _, K, n = g.shape
    kh = K // 2
    tr = _pick(kh, max(SUBLANES, (1 << 19) // n), 2 * SUBLANES)
    per = kh // tr

    def body(pc_ref, g_ref, r1_ref, r2_ref, o_ref):
        own = g_ref[...] + r1_ref[...]
        o_ref[...] = ((own + r2_ref[0].astype(F32)) + r2_ref[1].astype(F32)) + r2_ref[2].astype(F32)

    return pl.pallas_call(
        body, name=name,
        grid_spec=pltpu.PrefetchScalarGridSpec(
            num_scalar_prefetch=1, grid=(per,),
            in_specs=[pl.BlockSpec((None, tr, n), lambda i, pc: (pc[0], pc[1] * per + i, 0)),
                      pl.BlockSpec((None, tr, n), lambda i, pc: (pc[0], i, 0)),
                      pl.BlockSpec((3, tr, n), lambda i, pc: (0, i, 0))],
            out_specs=pl.BlockSpec((tr, n), lambda i, pc: (i, 0))),
        out_shape=jax.ShapeDtypeStruct((kh, n), F32),
        compiler_params=_params(("parallel",)),
    )(pc_idx, g, r1, r2)


def _sum8(g, name):
    _, R, L = g.shape

    def body(g_ref, o_ref):
        acc = g_ref[0]
        for d in range(1, N_DEV):
            acc = acc + g_ref[d]
        o_ref[...] = acc

    return pl.pallas_call(
        body, name=name, out_shape=jax.ShapeDtypeStruct((R, L), F32),
        in_specs=[pl.BlockSpec(memory_space=pltpu.VMEM)], out_specs=pl.BlockSpec(memory_space=pltpu.VMEM),
        compiler_params=_params(),
    )(g)


def _adamw_math(w, gg, m, v):
    nm = ADAM_B1 * m + (1.0 - ADAM_B1) * gg
    nv = ADAM_B2 * v + (1.0 - ADAM_B2) * jnp.square(gg)
    m_hat = nm / (1.0 - ADAM_B1 ** ADAM_STEP)
    v_hat = nv / (1.0 - ADAM_B2 ** ADAM_STEP)
    return -ADAM_LR * (m_hat / (jnp.sqrt(v_hat) + ADAM_EPS) + ADAM_WD * w), nm, nv


def _adamw(w, g, m, v, name):
    R, C = w.shape
    tr = _pick(R, max(SUBLANES, (1 << 18) // C), SUBLANES)

    def body(w_ref, g_ref, m_ref, v_ref, d_ref, nm_ref, nv_ref):
        d_ref[...], nm_ref[...], nv_ref[...] = _adamw_math(w_ref[...], g_ref[...], m_ref[...], v_ref[...])

    spec = pl.BlockSpec((tr, C), lambda i: (i, 0))
    sd = jax.ShapeDtypeStruct((R, C), F32)
    return pl.pallas_call(
        body, name=name, grid=(R // tr,), in_specs=[spec] * 4, out_specs=[spec] * 3, out_shape=[sd] * 3,
        compiler_params=_params(("parallel",)),
    )(w, g, m, v)


def _adamw_halves(w, g_own, g_sib, m, v, c_idx, name):
    K, n = w.shape
    kh = K // 2
    tr = _pick(kh, max(SUBLANES, (1 << 18) // n), SUBLANES)
    per = kh // tr

    def body(c_ref, w_ref, go_ref, gs_ref, m_ref, v_ref, g_ref, d_ref, nm_ref, nv_ref):
        h = pl.program_id(0)

        def step(gg):
            g_ref[...] = gg
            d_ref[...], nm_ref[...], nv_ref[...] = _adamw_math(w_ref[...], gg, m_ref[...], v_ref[...])

        @pl.when(h == 0)
        def _():
            step(go_ref[...])

        @pl.when(h == 1)
        def _():
            step(gs_ref[...])

    full = pl.BlockSpec((tr, n), lambda h, i, c: (((c[0] + h) % 2) * per + i, 0))
    own = pl.BlockSpec((tr, n), lambda h, i, c: (i * (1 - h), 0))
    sib = pl.BlockSpec((tr, n), lambda h, i, c: (i * h, 0))
    sd = jax.ShapeDtypeStruct((K, n), F32)
    return pl.pallas_call(
        body, name=name,
        grid_spec=pltpu.PrefetchScalarGridSpec(
            num_scalar_prefetch=1, grid=(2, per),
            in_specs=[full, own, sib, full, full], out_specs=[full] * 4),
        out_shape=[sd] * 4,
        compiler_params=_params(("arbitrary", "arbitrary")),
    )(c_idx, w, g_own, g_sib, m, v)


def _matmul(name, grid, dims, a, a_spec, b, b_spec, outs, out_specs, acc_shape,
            extras=(), extra_specs=(), epilogue=None, job=None, fill_into=None, prefetch=()):
    nk = grid[2]
    npre = len(prefetch)
    aliases = None
    if fill_into is not None:
        aliases = {2 + len(extras): 0}
        extras = tuple(extras) + (fill_into,)
        extra_specs = tuple(extra_specs) + (pl.BlockSpec(memory_space=pl.ANY),)
    nex = len(extras)
    nout = len(outs)

    def body(*refs):
        a_ref, b_ref, rest = refs[npre], refs[npre + 1], refs[npre + 2:]
        ex, o = rest[:nex], rest[nex:nex + nout]
        part = lax.dot_general(a_ref[...].astype(BF16), b_ref[...].astype(BF16), dims,
                               preferred_element_type=F32)

        def finish(res):
            if epilogue is None:
                o[0][...] = res.astype(o[0].dtype)
            else:
                epilogue(res, ex, o)

        if nk == 1:
            finish(part)
        else:
            acc = o[0] if in_place else rest[-1]
            k = pl.program_id(2)

            @pl.when(k == 0)
            def _():
                acc[...] = part

            @pl.when(k > 0)
            def _():
                acc[...] += part

            if not in_place:
                @pl.when(k == nk - 1)
                def _():
                    finish(acc[...])

    in_place = epilogue is None and nout == 1 and outs[0].dtype == F32
    res = _call(body, name, grid, [a_spec, b_spec] + list(extra_specs), out_specs, outs,
                [] if nk == 1 or in_place else [pltpu.VMEM(acc_shape, F32)], ("parallel", "parallel", "arbitrary"),
                (a, b, *extras), aliases=aliases, job=job, prefetch=prefetch)
    return res if job is None else (res[:nout], res[nout:])


def _first(res, job):
    return res[0] if job is None else (res[0][0], res[1])


def _mm_fwd(name, a, wfull, out_dtype=F32, planes=1, bias=None, job=None):
    S, K = a.shape
    _, _, n = wfull.shape
    N = N_CHIPS * n
    tm = _pick(S, 1024, LANES)
    tn = _pick(n, 1408, LANES)
    per = n // tn
    nj = N // tn
    pj = nj // planes
    grid = (S // tm, nj, 1)
    a_spec = pl.BlockSpec((tm, K), lambda i, j, k: (i, 0))
    b_spec = pl.BlockSpec((None, K, tn), lambda i, j, k: (j // per, 0, j % per))
    o_spec = pl.BlockSpec((None, tm, tn), lambda i, j, k: (j // pj, i, j % pj))
    sd = jax.ShapeDtypeStruct((planes, S, N // planes), out_dtype)
    if bias is not None:
        def epi(res, ex, o):
            o[0][...] = (res + ex[0][...]).astype(o[0].dtype)

        out = _matmul(name, grid, NN, a, a_spec, wfull, b_spec, [sd], [o_spec], (tm, tn), extras=(bias,),
                      extra_specs=(pl.BlockSpec((1, tn), lambda i, j, k: (0, j)),), epilogue=epi, job=job)
    else:
        out = _matmul(name, grid, NN, a, a_spec, wfull, b_spec, [sd], [o_spec], (tm, tn), job=job)
    return _first(out, job)


def _mm_fwd_block(name, a, wfull, bias, p_idx, mask, into=None, job=None):
    S, K = a.shape
    _, _, n = wfull.shape
    N = N_CHIPS * n
    tm = _pick(S, 1024, LANES)
    tn = _pick(n, 1408, LANES)
    per = n // tn
    grid = (S // tm, per, 1)
    a_spec = pl.BlockSpec((tm, K), lambda i, j, k, p: (i, 0))
    b_spec = pl.BlockSpec((None, K, tn), lambda i, j, k, p: (jnp.bitwise_xor(p[0], mask), 0, j))
    o_spec = pl.BlockSpec((tm, tn), lambda i, j, k, p: (i, jnp.bitwise_xor(p[0], mask) * per + j))
    v_spec = pl.BlockSpec((1, tn), lambda i, j, k, p: (0, jnp.bitwise_xor(p[0], mask) * per + j))

    def epi(res, ex, o):
        o[0][...] = res + ex[0][...]

    return _matmul(name, grid, NN, a, a_spec, wfull, b_spec, [jax.ShapeDtypeStruct((S, N), F32)], [o_spec], (tm, tn),
                   extras=(bias,), extra_specs=(v_spec,), epilogue=epi, job=job, fill_into=into, prefetch=(p_idx,))


def _copy_block(dst, src, p_idx, mask, name):
    _, K, n = dst.shape
    tr = _pick(K, max(SUBLANES, (1 << 19) // n), 2 * SUBLANES)

    def body(p_ref, s_ref, d_any, o_ref):
        o_ref[...] = s_ref[...]

    spec = pl.BlockSpec((None, tr, n), lambda i, p: (jnp.bitwise_xor(p[0], mask), i, 0))
    return _call(body, name, (K // tr,), [spec, pl.BlockSpec(memory_space=pl.ANY)], [spec],
                 [jax.ShapeDtypeStruct(dst.shape, dst.dtype)], [], ("parallel",), (src, dst), aliases={1: 0},
                 prefetch=(p_idx,))[0]


def _mm_fwd_rows(name, a, wrows, resid, gate, job=None):
    S, K = a.shape
    _, N = wrows.shape
    whole = K * N * 2 <= WHOLE_WEIGHT_BYTES
    tm = _pick(S, 512 if whole else 1024, LANES)
    tk = K if whole else _pick(K, 2816, LANES)
    tn = N if whole else _pick(N, 1024, LANES)
    grid = (S // tm, N // tn, K // tk)
    a_spec = pl.BlockSpec((tm, tk), lambda i, j, k: (i, k))
    if tn == N and tk == K:
        b_spec = pl.BlockSpec((tk, tn), lambda i, j, k: (k, j), pipeline_mode=pl.Buffered(1))
    else:
        b_spec = pl.BlockSpec((tk, tn), lambda i, j, k: (k, j))
    o_spec = pl.BlockSpec((tm, tn), lambda i, j, k: (i, j))
    g_spec = pl.BlockSpec((1, tn), lambda i, j, k: (0, j))

    def epi(res, ex, o):
        o[0][...] = res.astype(BF16)
        o[1][...] = ex[0][...] + ex[1][...] * res

    return _matmul(name, grid, NN, a, a_spec, wrows, b_spec,
                   [jax.ShapeDtypeStruct((S, N), BF16), jax.ShapeDtypeStruct((S, N), F32)], [o_spec, o_spec], (tm, tn),
                   extras=(resid, gate), extra_specs=(o_spec, g_spec), epilogue=epi, job=job)


def _mm_dgrad_cols(name, dy, wfull, out_dtype=F32, job=None, row_tiles=None, fill_into=None):
    planes, S, npl = dy.shape
    _, K, n = wfull.shape
    tm = _pick(S if row_tiles is None else S // 2, 1024, LANES)
    to = _pick(K, 1024, LANES)
    tk = _pick(n, 2816, LANES)
    per = n // tk
    nk = N_CHIPS * per
    pk = nk // planes
    n_tiles = S // tm
    i0, ni = (0, n_tiles) if row_tiles is None else (int(row_tiles[0] * n_tiles), int(row_tiles[1] * n_tiles))
    grid = (ni, K // to, nk)
    a_spec = pl.BlockSpec((None, tm, tk), lambda i, j, k: (k // pk, i + i0, k % pk))
    b_spec = pl.BlockSpec((None, to, tk), lambda i, j, k: (k // per, j, k % per))
    o_spec = pl.BlockSpec((tm, to), lambda i, j, k: (i + i0, j))
    return _first(_matmul(name, grid, NT, dy, a_spec, wfull, b_spec, [jax.ShapeDtypeStruct((S, K), out_dtype)],
                          [o_spec], (tm, to), job=job, fill_into=fill_into), job)


def _mm_dgrad_rows(name, dy, wrows, out_dtype=F32, job=None):
    S, N = dy.shape
    K, _ = wrows.shape
    tm = _pick(S, 1024, LANES)
    to = _pick(K, 1408, LANES)
    tk = _pick(N, 2048, LANES)
    grid = (S // tm, K // to, N // tk)
    a_spec = pl.BlockSpec((tm, tk), lambda i, j, k: (i, k))
    b_spec = pl.BlockSpec((to, tk), lambda i, j, k: (j, k))
    o_spec = pl.BlockSpec((tm, to), lambda i, j, k: (i, j))
    return _first(_matmul(name, grid, NT, dy, a_spec, wrows, b_spec, [jax.ShapeDtypeStruct((S, K), out_dtype)],
                          [o_spec], (tm, to), job=job), job)


def _mm_wgrad_cols(name, a, dy, n, job=None):
    S, K = a.shape
    planes, _, npl = dy.shape
    N = planes * npl
    to = _pick(K, 1024, LANES)
    tn = _pick(n, 1408, LANES)
    ts = _pick(S, 4096 if to * tn * 4 <= SMALL_TILE_BYTES else 2048, LANES)
    per = n // tn
    nj = N // tn
    pj = nj // planes
    grid = (K // to, nj, S // ts)
    a_spec = pl.BlockSpec((ts, to), lambda i, j, k: (k, i))
    b_spec = pl.BlockSpec((None, ts, tn), lambda i, j, k: (j // pj, k, j % pj))
    o_spec = pl.BlockSpec((None, to, tn), lambda i, j, k: (j // per, i, j % per))
    return _first(_matmul(name, grid, TN, a, a_spec, dy, b_spec, [jax.ShapeDtypeStruct((N_CHIPS, K, n), F32)],
                          [o_spec], (to, tn), job=job), job)


def _mm_wgrad_rows(name, a, dy, kshard, job=None):
    S, K = a.shape
    _, N = dy.shape
    to = _pick(kshard, 1408, LANES)
    tn = N if to * N * 4 <= 2 * SMALL_TILE_BYTES else _pick(N, 1024, LANES)
    ts = _pick(S, 2048, LANES)
    per = kshard // to
    grid = (K // to, N // tn, S // ts)
    a_spec = pl.BlockSpec((ts, to), lambda i, j, k: (k, i))
    b_spec = pl.BlockSpec((ts, tn), lambda i, j, k: (k, j))
    o_spec = pl.BlockSpec((None, to, tn), lambda i, j, k: (i // per, i % per, j))
    return _first(_matmul(name, grid, TN, a, a_spec, dy, b_spec, [jax.ShapeDtypeStruct((N_CHIPS, kshard, N), F32)],
                          [o_spec], (to, tn), job=job), job)


def _rows(tm, width, colblk=0):
    return pl.BlockSpec((tm, width), lambda i: (i, colblk))


def _whole(shape):
    nd = len(shape)
    return pl.BlockSpec(shape, lambda i: (0,) * nd)


def _prev_halo(tm, h, width, colblk=0):
    r = tm // h
    return pl.BlockSpec((h, width), lambda i: (jnp.maximum(i * r - 1, 0), colblk))


def _next_halo(tm, h, width, nblk, colblk=0):
    r = tm // h
    return pl.BlockSpec((h, width), lambda i: (jnp.minimum((i + 1) * r, nblk - 1), colblk))


def _accumulate(i, ref, val):
    @pl.when(i == 0)
    def _():
        ref[...] = val

    @pl.when(i > 0)
    def _():
        ref[...] += val


def _fill_rotations(rot, offs):
    n = rot.shape[1]
    for r in sorted({o % SUBLANES for o in offs} - {0}):
        rot[r, 0:n - SUBLANES, :] = rot[0, r:r + n - SUBLANES, :]


def _tap_windows(rot, offs, row0, rb, lanes):
    by_res = {}
    for k, o in enumerate(offs):
        by_res.setdefault(o % SUBLANES, []).append((o // SUBLANES, k))
    for r, taps in by_res.items():
        lo = min(a for a, _ in taps)
        hi = max(a for a, _ in taps)
        win = rot[r, pl.ds(row0 + SUBLANES * lo, rb + SUBLANES * (hi - lo)), lanes]
        for a, k in taps:
            yield k, win[SUBLANES * (a - lo):SUBLANES * (a - lo) + rb, :]


def _for_chunks(n_rows, rb, fn):
    def step(j, carry):
        fn(pl.multiple_of(j * rb, rb))
        return carry

    lax.fori_loop(0, n_rows // rb, step, 0)


def _modnorm_fwd(name, x, g, scale, shift):
    S, D = x.shape
    tm = _pick(S, 1024, LANES)

    def body(x_ref, g_ref, sc_ref, sh_ref, h_ref):
        h_ref[...] = _modnorm(x_ref[...], g_ref[...], sc_ref[...], sh_ref[...]).astype(BF16)

    vec = _whole((1, D))
    return pl.pallas_call(
        body, name=name, grid=(S // tm,), in_specs=[_rows(tm, D), vec, vec, vec], out_specs=_rows(tm, D),
        out_shape=jax.ShapeDtypeStruct((S, D), BF16), compiler_params=_params(("parallel",)),
    )(x, g, scale, shift)


def _modnorm_bwd(name, x, dh, dx_in, g, scale, shift, gated=None):
    S, D = x.shape
    tm = _pick(S, 256, LANES)

    def body(x_ref, dh_ref, dxin_ref, g_ref, sc_ref, sh_ref, *rest):
        i = pl.program_id(0)
        dx_ref, dg_ref, dsc_ref, dsh_ref = rest[-4:] if gated is None else rest[2:6]
        _, pull = jax.vjp(_modnorm, x_ref[...], g_ref[...], sc_ref[...], sh_ref[...])
        dx, dg, dsc, dsh = pull(dh_ref[...])
        dx = dxin_ref[...] + dx
        dx_ref[...] = dx
        _accumulate(i, dg_ref, dg)
        _accumulate(i, dsc_ref, dsc)
        _accumulate(i, dsh_ref, dsh)
        if gated is not None:
            _gate_bwd_tile(i, dx, rest[0], rest[1], rest[6], rest[7])

    vec = _whole((1, D))
    row = _rows(tm, D)
    vsd = jax.ShapeDtypeStruct((1, D), F32)
    in_specs, args = [row, row, row, vec, vec, vec], (x, dh, dx_in, g, scale, shift)
    out_specs, out_shape = [row, vec, vec, vec], [jax.ShapeDtypeStruct((S, D), F32), vsd, vsd, vsd]
    if gated is not None:
        in_specs, args = in_specs + [row, vec], args + tuple(gated)
        out_specs, out_shape = out_specs + [row, vec], out_shape + [jax.ShapeDtypeStruct((S, D), BF16), vsd]
    return _call(body, name, (S // tm,), in_specs, out_specs, out_shape, [], ("arbitrary",), args)


def _conv_fwd(name, proj, w, b, lg, lb, dc, job=None):
    S = proj.shape[0]
    K = w.shape[0]
    H = CONV_HALO
    tm = _pick(S, 512, LANES)

    def glu(v):
        return v[:, :dc] * jax.nn.sigmoid(v[:, dc:])

    offs = [H - (K - 1) + k for k in range(K)]
    rb = _pick(tm, CONV_CHUNK_ROWS, SUBLANES)
    lw = min(LANES, dc)

    def body(cur_ref, prev_ref, w_ref, b_ref, lg_ref, lb_ref, conv_ref, act_ref, rot):
        i = pl.program_id(0)
        rot[0, 0:H, :] = jnp.where(i > 0, glu(prev_ref[...]), 0.0)
        rot[0, H:, :] = glu(cur_ref[...])
        _fill_rotations(rot, offs)

        def chunk(row0):
            for l0 in range(0, dc, lw):
                lanes = slice(l0, l0 + lw)
                acc = jnp.broadcast_to(b_ref[:, lanes], (rb, lw))
                for k, win in _tap_windows(rot, offs, row0, rb, lanes):
                    acc = acc + w_ref[k:k + 1, lanes] * win
                conv_ref[pl.ds(row0, rb), lanes] = acc

        _for_chunks(tm, rb, chunk)
        act_ref[...] = _ln_silu(conv_ref[...], lg_ref[...], lb_ref[...]).astype(BF16)

    vec = _whole((1, dc))
    res = _call(body, name, (S // tm,),
                [_rows(tm, 2 * dc), _prev_halo(tm, H, 2 * dc), _whole(w.shape), vec, vec, vec],
                [_rows(tm, dc), _rows(tm, dc)],
                [jax.ShapeDtypeStruct((S, dc), F32), jax.ShapeDtypeStruct((S, dc), BF16)],
                [pltpu.VMEM((SUBLANES, tm + H, dc), F32)], ("parallel",), (proj, proj, w, b, lg, lb), job=job)
    return res if job is None else (res[:2], res[2:])


def _conv_bwd_ln(name, conv, dact, lg, lb):
    S, dc = conv.shape
    tm = _pick(S, 256, LANES)

    def body(c_ref, d_ref, lg_ref, lb_ref, dc_ref, dlg_ref, dlb_ref, db_ref):
        i = pl.program_id(0)
        _, pull = jax.vjp(_ln_silu, c_ref[...], lg_ref[...], lb_ref[...])
        dcv, dlg, dlb = pull(d_ref[...])
        dc_ref[...] = dcv
        _accumulate(i, dlg_ref, dlg)
        _accumulate(i, dlb_ref, dlb)
        _accumulate(i, db_ref, jnp.sum(dcv, axis=0, keepdims=True))

    vec = _whole((1, dc))
    row = _rows(tm, dc)
    vsd = jax.ShapeDtypeStruct((1, dc), F32)
    return pl.pallas_call(
        body, name=name, grid=(S // tm,), in_specs=[row, row, vec, vec], out_specs=[row, vec, vec, vec],
        out_shape=[jax.ShapeDtypeStruct((S, dc), F32), vsd, vsd, vsd],
        compiler_params=_params(("arbitrary",)),
    )(conv, dact, lg, lb)


def _conv_bwd(name, dconv, proj, w, dproj, dc, job=None):
    S = proj.shape[0]
    K = w.shape[0]
    H = CONV_HALO
    tm = _pick(S, 256, LANES)
    nt = S // tm

    offs_g = [H - (K - 1) + k for k in range(K)]
    offs_d = [K - 1 - k for k in range(K)]
    rb = _pick(tm, CONV_CHUNK_ROWS, SUBLANES)
    lw = min(LANES, dc)
    kp = -(-K // SUBLANES) * SUBLANES

    def body(d_ref, dn_ref, cur_ref, prev_ref, w_ref, dproj_any, da_ref, dw_ref, dbin_ref, rotg, rotd, accw, dglu_s):
        i = pl.program_id(0)
        pv = prev_ref[...]
        cv = cur_ref[...]
        sig = jax.nn.sigmoid(cv[:, dc:])
        rotg[0, 0:H, :] = jnp.where(i > 0, pv[:, :dc] * jax.nn.sigmoid(pv[:, dc:]), 0.0)
        rotg[0, H:, :] = cv[:, :dc] * sig
        rotd[0, 0:tm, :] = d_ref[...]
        rotd[0, tm:, :] = jnp.where(i < nt - 1, dn_ref[...], 0.0)
        _fill_rotations(rotg, offs_g)
        _fill_rotations(rotd, offs_d)

        @pl.when(i == 0)
        def _():
            accw[...] = jnp.zeros_like(accw)

        for l0 in range(0, dc, lw):
            lanes = slice(l0, l0 + lw)

            def chunk(j, sums, lanes=lanes):
                row0 = pl.multiple_of(j * rb, rb)
                dcur = rotd[0, pl.ds(row0, rb), lanes]
                acc = jnp.zeros((rb, lw), F32)
                for k, win in _tap_windows(rotd, offs_d, row0, rb, lanes):
                    acc = acc + w_ref[k:k + 1, lanes] * win
                new = list(sums)
                for k, win in _tap_windows(rotg, offs_g, row0, rb, lanes):
                    new[k] = sums[k] + jnp.sum((dcur * win).reshape(rb // SUBLANES, SUBLANES, lw), axis=0)
                dglu_s[pl.ds(row0, rb), lanes] = acc
                return tuple(new)

            sums = lax.fori_loop(0, tm // rb, chunk, tuple(jnp.zeros((SUBLANES, lw), F32) for _ in range(K)))
            for k in range(K):
                accw[SUBLANES * k:SUBLANES * (k + 1), lanes] += sums[k]
        dglu = dglu_s[...]
        da = jnp.concatenate([dglu * sig, dglu * cv[:, :dc] * sig * (1.0 - sig)], axis=1)
        da_ref[...] = da.astype(BF16)
        _accumulate(i, dbin_ref, jnp.sum(da, axis=0, keepdims=True))

        @pl.when(i == nt - 1)
        def _():
            dw_ref[...] = jnp.zeros_like(dw_ref)
            for k in range(K):
                dw_ref[k:k + 1, :] = jnp.sum(accw[SUBLANES * k:SUBLANES * (k + 1), :], axis=0, keepdims=True)

    res = _call(
        body, name, (nt,),
        [_rows(tm, dc), _next_halo(tm, H, dc, S // H), _rows(tm, 2 * dc), _prev_halo(tm, H, 2 * dc),
         _whole(w.shape), pl.BlockSpec(memory_space=pl.ANY)],
        [_rows(tm, 2 * dc), _whole((kp, dc)), _whole((1, 2 * dc))],
        [jax.ShapeDtypeStruct(dproj.shape, BF16), jax.ShapeDtypeStruct((kp, dc), F32),
         jax.ShapeDtypeStruct((1, 2 * dc), F32)],
        [pltpu.VMEM((SUBLANES, tm + H, dc), F32), pltpu.VMEM((SUBLANES, tm + H, dc), F32),
         pltpu.VMEM((SUBLANES * K, dc), F32), pltpu.VMEM((tm, dc), F32)],
        ("arbitrary",), (dconv, dconv, proj, proj, w, dproj), aliases={5: 0}, job=job)
    return res if job is None else (res[:3], res[3:])


def _mix(vln, wsm, bst, out_ref, G, CH, hd):
    for n in range(vln.shape[0] // CH):
        for g in range(G):
            blk = vln[n * CH:(n + 1) * CH, g * hd:(g + 1) * hd].astype(BF16)
            out_ref[n * CH:(n + 1) * CH, g * hd:(g + 1) * hd] = (
                jnp.dot(wsm[g], blk, preferred_element_type=F32) + bst[:, g:g + 1])


def _blocks_to_columns(full, name):
    nb, K, n = full.shape
    tr = _pick(K, max(SUBLANES, (1 << 19) // n), 2 * SUBLANES)

    def body(s_ref, o_ref):
        o_ref[...] = s_ref[...]

    return pl.pallas_call(
        body, name=name, grid=(nb, K // tr),
        in_specs=[pl.BlockSpec((None, tr, n), lambda p, i: (p, i, 0))],
        out_specs=pl.BlockSpec((tr, n), lambda p, i: (i, p)),
        out_shape=jax.ShapeDtypeStruct((K, nb * n), full.dtype),
        compiler_params=_params(("parallel", "parallel")))(full)


def _sgu_fwd(name, proj, a_act, wco, wso, lg, lb, ws, bst, D, ds, job=None):
    S = proj.shape[0]
    dc = a_act.shape[1]
    G, CH, _ = ws.shape
    hd = ds // G
    tm = _pick(S, 256, CH)

    def body(s_ref, gt_ref, a_ref, wco_ref, wso_ref, lg_ref, lb_ref, ws_ref, bst_ref,
             ya_ref, yb_ref, uv_ref, mg_ref, vmix):
        z = _gelu(s_ref[...])
        vln = _layer_norm(z[:, ds:], lg_ref[...], lb_ref[...])
        wsm = [_tril_mask(ws_ref[g]).astype(BF16) for g in range(G)]
        _mix(vln, wsm, bst_ref[...], vmix, G, CH, hd)
        uv = (z[:, :ds] * vmix[...]).astype(BF16)
        uv_ref[...] = uv
        ya = jnp.dot(a_ref[...], wco_ref[...], preferred_element_type=F32)
        yb = jnp.dot(uv, wso_ref[...], preferred_element_type=F32)
        ya_ref[...] = ya.astype(BF16)
        yb_ref[...] = yb.astype(BF16)
        gt = gt_ref[...]
        mg_ref[...] = (jax.nn.sigmoid(gt[:, :D]) * ya + jax.nn.sigmoid(gt[:, D:]) * yb).astype(BF16)

    vec = _whole((1, ds))
    sdb = jax.ShapeDtypeStruct((S, D), BF16)
    res = _call(body, name, (S // tm,),
                [_rows(tm, 2 * ds, 1), _rows(tm, 2 * D, 1), _rows(tm, dc), _whole(wco.shape), _whole(wso.shape),
                 vec, vec, _whole(ws.shape), _whole(bst.shape)],
                [_rows(tm, D), _rows(tm, D), _rows(tm, ds), _rows(tm, D)],
                [sdb, sdb, jax.ShapeDtypeStruct((S, ds), BF16), sdb],
                [pltpu.VMEM((tm, ds), F32)], ("parallel",), (proj, proj, a_act, wco, wso, lg, lb, ws, bst), job=job)
    return res if job is None else (res[:4], res[4:])


def _merge_bwd(name, dmerged, proj, ya, yb, D):
    S = proj.shape[0]
    tm = _pick(S, 256, LANES)

    def body(dm_ref, gt_ref, ya_ref, yb_ref, dya_ref, dyb_ref, dg_ref, dbin_ref):
        i = pl.program_id(0)
        dm = dm_ref[...]
        gt = gt_ref[...]
        sa = jax.nn.sigmoid(gt[:, :D])
        sb = jax.nn.sigmoid(gt[:, D:])
        dya_ref[...] = (dm * sa).astype(BF16)
        dyb_ref[...] = (dm * sb).astype(BF16)
        ya = ya_ref[...].astype(F32)
        yb = yb_ref[...].astype(F32)
        dg = jnp.concatenate([dm * ya * sa * (1.0 - sa), dm * yb * sb * (1.0 - sb)], axis=1)
        dg_ref[...] = dg.astype(BF16)
        _accumulate(i, dbin_ref, jnp.sum(dg, axis=0, keepdims=True))

    row = _rows(tm, D)
    sdb = jax.ShapeDtypeStruct((S, D), BF16)
    return pl.pallas_call(
        body, name=name, grid=(S // tm,),
        in_specs=[row, _rows(tm, 2 * D, 1), row, row],
        out_specs=[row, row, _rows(tm, 2 * D, 1), _whole((1, 2 * D))],
        out_shape=[sdb, sdb, jax.ShapeDtypeStruct((S, 4 * D), BF16), jax.ShapeDtypeStruct((1, 2 * D), F32)],
        compiler_params=_params(("arbitrary",)),
    )(dmerged, proj, ya, yb)


def _sgu_bwd(name, proj, duv, lg, lb, ws, bst, dproj, ds):
    S = proj.shape[0]
    G, CH, _ = ws.shape
    hd = ds // G
    tm = _pick(S, 256, CH)

    def body(s_ref, duv_ref, lg_ref, lb_ref, ws_ref, bst_ref, dproj_any,
             dsin_ref, dws_ref, dbs_ref, dlg_ref, dlb_ref, dbin_ref, vmix, dvln):
        i = pl.program_id(0)
        z, pull_gelu = jax.vjp(_gelu, s_ref[...])
        u = z[:, :ds]
        vln, pull_ln = jax.vjp(_layer_norm, z[:, ds:], lg_ref[...], lb_ref[...])
        wsm = [_tril_mask(ws_ref[g]).astype(BF16) for g in range(G)]
        _mix(vln, wsm, bst_ref[...], vmix, G, CH, hd)
        duv = duv_ref[...]
        du = duv * vmix[...]
        dvmix = duv * u
        for g in range(G):
            dws_g = jnp.zeros((CH, CH), F32)
            dbs_g = jnp.zeros((CH, 1), F32)
            for n in range(tm // CH):
                dblk = dvmix[n * CH:(n + 1) * CH, g * hd:(g + 1) * hd]
                vblk = vln[n * CH:(n + 1) * CH, g * hd:(g + 1) * hd].astype(BF16)
                dvln[n * CH:(n + 1) * CH, g * hd:(g + 1) * hd] = lax.dot_general(
                    wsm[g], dblk.astype(BF16), TN, preferred_element_type=F32)
                dws_g = dws_g + lax.dot_general(dblk.astype(BF16), vblk, NT, preferred_element_type=F32)
                dbs_g = dbs_g + jnp.sum(dblk, axis=1, keepdims=True)
            dws_g = _tril_mask(dws_g)
            dbs_g = jnp.broadcast_to(dbs_g, (CH, LANES))

            @pl.when(i == 0)
            def _():
                dws_ref[g] = dws_g
                dbs_ref[g] = dbs_g

            @pl.when(i > 0)
            def _():
                dws_ref[g] += dws_g
                dbs_ref[g] += dbs_g

        dv, dlg, dlb = pull_ln(dvln[...])
        (dsin,) = pull_gelu(jnp.concatenate([du, dv], axis=1))
        dsin_ref[...] = dsin.astype(BF16)
        _accumulate(i, dlg_ref, dlg)
        _accumulate(i, dlb_ref, dlb)
        _accumulate(i, dbin_ref, jnp.sum(dsin, axis=0, keepdims=True))

    vec = _whole((1, ds))
    vsd = jax.ShapeDtypeStruct((1, ds), F32)
    return pl.pallas_call(
        body, name=name, grid=(S // tm,),
        in_specs=[_rows(tm, 2 * ds, 1), _rows(tm, ds), vec, vec, _whole(ws.shape), _whole(bst.shape),
                  pl.BlockSpec(memory_space=pl.ANY)],
        out_specs=[_rows(tm, 2 * ds, 1), _whole((G, CH, CH)), _whole((G, CH, LANES)), vec, vec, _whole((1, 2 * ds))],
        out_shape=[jax.ShapeDtypeStruct(dproj.shape, BF16), jax.ShapeDtypeStruct((G, CH, CH), F32),
                   jax.ShapeDtypeStruct((G, CH, LANES), F32), vsd, vsd, jax.ShapeDtypeStruct((1, 2 * ds), F32)],
        scratch_shapes=[pltpu.VMEM((tm, ds), F32), pltpu.VMEM((tm, ds), F32)],
        input_output_aliases={6: 0},
        compiler_params=_params(("arbitrary",)),
    )(proj, duv, lg, lb, ws, bst, dproj)


def _silu_mul(val, gt):
    return jax.nn.silu(gt) * val


def _rotation_slots(offs):
    slot = {0: 0}
    for r in sorted({o % SUBLANES for o in offs} - {0}):
        slot[r] = len(slot)
    return slot


def _fill_plane_rotations(rot, slot):
    n = rot.shape[2]
    for r, s in slot.items():
        if r:
            rot[:, s, 0:n - SUBLANES, :] = rot[:, 0, r:r + n - SUBLANES, :]


def _ffn_tiles(S, Fh, rows=256):
    return _pick(S, rows, LANES), _pick(Fh, 1408, LANES)


def _ffn_fwd(name, up, w, b):
    _, S, Fh = up.shape
    K = w.shape[1]
    H = FFN_HALO
    tm, cw = _ffn_tiles(S, Fh, rows=512)
    r = tm // H

    offs = [H - (K - 1) + k for k in range(K)]
    slot = _rotation_slots(offs)
    rb = _pick(tm, 64, 2 * SUBLANES)
    lw = min(LANES, cw)

    def body(cur_ref, prev_ref, w_ref, b_ref, act_ref, rot):
        i = pl.program_id(1)
        rot[:, 0, 0:H, :] = jnp.where(i > 0, prev_ref[...], 0.0)
        rot[:, 0, H:, :] = cur_ref[...]
        _fill_plane_rotations(rot, slot)

        def chunk(row0):
            for l0 in range(0, cw, lw):
                lanes = slice(l0, l0 + lw)
                c2 = []
                for pln in range(2):
                    acc = jnp.broadcast_to(b_ref[pln, :, lanes], (rb, lw))
                    for k in range(K):
                        a, rr = divmod(offs[k], SUBLANES)
                        acc = acc + w_ref[pln, k:k + 1, lanes] * rot[pln, slot[rr], pl.ds(row0 + SUBLANES * a, rb), lanes]
                    c2.append(acc)
                act_ref[pl.ds(row0, rb), lanes] = _silu_mul(c2[0], c2[1]).astype(BF16)

        _for_chunks(tm, rb, chunk)

    return pl.pallas_call(
        body, name=name, grid=(Fh // cw, S // tm),
        in_specs=[pl.BlockSpec((2, tm, cw), lambda j, i: (0, i, j)),
                  pl.BlockSpec((2, H, cw), lambda j, i: (0, jnp.maximum(i * r - 1, 0), j)),
                  pl.BlockSpec((2, K, cw), lambda j, i: (0, 0, j)),
                  pl.BlockSpec((2, 1, cw), lambda j, i: (0, 0, j))],
        out_specs=pl.BlockSpec((tm, cw), lambda j, i: (i, j)),
        out_shape=jax.ShapeDtypeStruct((S, Fh), BF16),
        scratch_shapes=[pltpu.VMEM((2, len(slot), tm + H, cw), F32)],
        compiler_params=_params(("parallel", "parallel")),
    )(up, up, w, b)


def _ffn_bwd(name, up, dact, w, b, job=None):
    _, S, Fh = up.shape
    K = w.shape[1]
    H = FFN_HALO
    tm, cw = _ffn_tiles(S, Fh)
    r = tm // H
    nt = S // tm
    nhb = S // H
    te = tm + H

    offs_x = [H - (K - 1) + k for k in range(K)]
    offs_d = [K - 1 - k for k in range(K)]
    slot_x = _rotation_slots(offs_x)
    slot_d = _rotation_slots(offs_d)
    rb = _pick(tm, 64, 2 * SUBLANES)
    rbe = _pick(te, 96, SUBLANES)
    lw = min(LANES, cw)

    def body(cur_ref, prev_ref, next_ref, d_ref, dn_ref, w_ref, b_ref, dup_ref, dwb_ref, rotx, dext, rotd, accw):
        i = pl.program_id(1)
        rotx[:, 0, 0:H, :] = jnp.where(i > 0, prev_ref[...], 0.0)
        rotx[:, 0, H:H + tm, :] = cur_ref[...]
        rotx[:, 0, H + tm:, :] = jnp.where(i < nt - 1, next_ref[...], 0.0)
        dext[0:tm, :] = d_ref[...]
        dext[tm:, :] = jnp.where(i < nt - 1, dn_ref[...], 0.0)
        _fill_plane_rotations(rotx, slot_x)

        def chunk_e(row0):
            for l0 in range(0, cw, lw):
                lanes = slice(l0, l0 + lw)
                c2 = []
                for pln in range(2):
                    acc = jnp.broadcast_to(b_ref[pln, :, lanes], (rbe, lw))
                    for k in range(K):
                        a, rr = divmod(offs_x[k], SUBLANES)
                        acc = acc + w_ref[pln, k:k + 1, lanes] * rotx[pln, slot_x[rr], pl.ds(row0 + SUBLANES * a, rbe), lanes]
                    c2.append(acc)
                _, pull = jax.vjp(_silu_mul, c2[0], c2[1])
                dval, dgt = pull(dext[pl.ds(row0, rbe), lanes])
                rotd[0, 0, pl.ds(row0, rbe), lanes] = dval
                rotd[1, 0, pl.ds(row0, rbe), lanes] = dgt

        _for_chunks(te, rbe, chunk_e)
        _fill_plane_rotations(rotd, slot_d)

        @pl.when(i == 0)
        def _():
            accw[...] = jnp.zeros_like(accw)

        def chunk(row0):
            for l0 in range(0, cw, lw):
                lanes = slice(l0, l0 + lw)
                for pln in range(2):
                    dcur = rotd[pln, 0, pl.ds(row0, rb), lanes]
                    dup = jnp.zeros((rb, lw), F32)
                    for k in range(K):
                        a, rr = divmod(offs_d[k], SUBLANES)
                        dup = dup + w_ref[pln, k:k + 1, lanes] * rotd[pln, slot_d[rr], pl.ds(row0 + SUBLANES * a, rb), lanes]
                        a, rr = divmod(offs_x[k], SUBLANES)
                        prod = dcur * rotx[pln, slot_x[rr], pl.ds(row0 + SUBLANES * a, rb), lanes]
                        accw[pln, SUBLANES * k:SUBLANES * (k + 1), lanes] += jnp.sum(
                            prod.reshape(rb // SUBLANES, SUBLANES, lw), axis=0)
                    accw[pln, SUBLANES * K:SUBLANES * (K + 1), lanes] += jnp.sum(
                        dcur.reshape(rb // SUBLANES, SUBLANES, lw), axis=0)
                    dup_ref[pln, pl.ds(row0, rb), lanes] = dup.astype(BF16)

        _for_chunks(tm, rb, chunk)

        @pl.when(i == nt - 1)
        def _():
            dwb_ref[...] = jnp.zeros_like(dwb_ref)
            for pln in range(2):
                for k in range(K + 1):
                    dwb_ref[pln, k:k + 1, :] = jnp.sum(accw[pln, SUBLANES * k:SUBLANES * (k + 1), :], axis=0,
                                                       keepdims=True)

    res = _call(
        body, name, (Fh // cw, nt),
        [pl.BlockSpec((2, tm, cw), lambda j, i: (0, i, j)),
         pl.BlockSpec((2, H, cw), lambda j, i: (0, jnp.maximum(i * r - 1, 0), j)),
         pl.BlockSpec((2, H, cw), lambda j, i: (0, jnp.minimum((i + 1) * r, nhb - 1), j)),
         pl.BlockSpec((tm, cw), lambda j, i: (i, j)),
         pl.BlockSpec((H, cw), lambda j, i: (jnp.minimum((i + 1) * r, nhb - 1), j)),
         pl.BlockSpec((2, K, cw), lambda j, i: (0, 0, j)),
         pl.BlockSpec((2, 1, cw), lambda j, i: (0, 0, j))],
        [pl.BlockSpec((2, tm, cw), lambda j, i: (0, i, j)), pl.BlockSpec((2, SUBLANES, cw), lambda j, i: (0, 0, j))],
        [jax.ShapeDtypeStruct((2, S, Fh), BF16), jax.ShapeDtypeStruct((2, SUBLANES, Fh), F32)],
        [pltpu.VMEM((2, len(slot_x), tm + 2 * H, cw), F32), pltpu.VMEM((te, cw), F32),
         pltpu.VMEM((2, len(slot_d), te, cw), F32), pltpu.VMEM((2, SUBLANES * (K + 1), cw), F32)],
        ("parallel", "arbitrary"), (up, up, up, dact, dact, w, b), job=job)
    return res if job is None else (res[:2], res[2:])


def _rms(x, g):
    return x * lax.rsqrt(jnp.mean(x * x, axis=-1, keepdims=True) + EPS) * g


def _gate_bwd_tile(i, dx, out_ref, gate_ref, dout_ref, dgate_ref):
    dout_ref[...] = (dx * gate_ref[...]).astype(BF16)
    _accumulate(i, dgate_ref, jnp.sum(dx * out_ref[...].astype(F32), axis=0, keepdims=True))


def _final(name, x2, target, gf, out, gate):
    S, D = x2.shape
    tm = _pick(S, 256, LANES)

    def body(x_ref, t_ref, g_ref, o_ref, gate_ref, dx_ref, dg_ref, loss_ref, do_ref, dgate_ref):
        i = pl.program_id(0)
        y, pull = jax.vjp(_rms, x_ref[...], g_ref[...])
        e = y - t_ref[...]
        dx, dg = pull(e / D)
        dx_ref[...] = dx
        _accumulate(i, dg_ref, dg)
        part = 0.5 * jnp.sum(jnp.mean(jnp.square(e), axis=-1, keepdims=True), axis=0, keepdims=True)
        _accumulate(i, loss_ref, jnp.broadcast_to(part, (SUBLANES, LANES)))
        _gate_bwd_tile(i, dx, o_ref, gate_ref, do_ref, dgate_ref)

    row = _rows(tm, D)
    vec = _whole((1, D))
    vsd = jax.ShapeDtypeStruct((1, D), F32)
    return pl.pallas_call(
        body, name=name, grid=(S // tm,), in_specs=[row, row, vec, row, vec],
        out_specs=[row, vec, _whole((SUBLANES, LANES)), row, vec],
        out_shape=[jax.ShapeDtypeStruct((S, D), F32), vsd, jax.ShapeDtypeStruct((SUBLANES, LANES), F32),
                   jax.ShapeDtypeStruct((S, D), BF16), vsd],
        compiler_params=_params(("arbitrary",)),
    )(x2, target, gf, out, gate)


def _ada_fwd(name, c_pad, w_ada, b_cols):
    nb, D = c_pad.shape
    n = w_ada.shape[1]
    tn = _pick(n, 1024, LANES)

    def body(c_ref, w_ref, b_ref, o_ref):
        o_ref[...] = jnp.dot(jax.nn.silu(c_ref[...]).astype(BF16), w_ref[...].astype(BF16),
                             preferred_element_type=F32) + b_ref[...]

    return pl.pallas_call(
        body, name=name, grid=(n // tn,),
        in_specs=[_whole((nb, D)), pl.BlockSpec((D, tn), lambda j: (0, j)), pl.BlockSpec((1, tn), lambda j: (0, j))],
        out_specs=pl.BlockSpec((nb, tn), lambda j: (0, j)),
        out_shape=jax.ShapeDtypeStruct((nb, n), F32), compiler_params=_params(("parallel",)),
    )(c_pad, w_ada, b_cols)


def _ada_wgrad(name, c_t, dmod_cols):
    D, nb = c_t.shape
    n = dmod_cols.shape[1]
    tr = _pick(D, 256, SUBLANES)

    def body(c_ref, d_ref, o_ref):
        ca = jax.nn.silu(c_ref[...])
        acc = ca[:, 0:1] * d_ref[0:1, :]
        for b in range(1, nb):
            acc = acc + ca[:, b:b + 1] * d_ref[b:b + 1, :]
        o_ref[...] = acc

    return pl.pallas_call(
        body, name=name, grid=(D // tr,),
        in_specs=[pl.BlockSpec((tr, nb), lambda i: (i, 0)), _whole((nb, n))],
        out_specs=pl.BlockSpec((tr, n), lambda i: (i, 0)),
        out_shape=jax.ShapeDtypeStruct((D, n), F32), compiler_params=_params(("parallel",)),
    )(c_t, dmod_cols)


def kernel(x, c, w_ada, b_ada, norm1_g, w_in, b_in, conv_dw_w, conv_dw_b, conv_ln_g, conv_ln_b, w_conv_out, sgu_ln_g, sgu_ln_b, w_spatial, b_spatial, w_sgu_out, w_out, norm2_g, w_up, ffn_dw_w, ffn_dw_b, w_down, final_g, loss_target, m_w_ada, m_b_ada, m_norm1_g, m_w_in, m_b_in, m_conv_dw_w, m_conv_dw_b, m_conv_ln_g, m_conv_ln_b, m_w_conv_out, m_sgu_ln_g, m_sgu_ln_b, m_w_spatial, m_b_spatial, m_w_sgu_out, m_w_out, m_norm2_g, m_w_up, m_ffn_dw_w, m_ffn_dw_b, m_w_down, m_final_g, v_w_ada, v_b_ada, v_norm1_g, v_w_in, v_b_in, v_conv_dw_w, v_conv_dw_b, v_conv_ln_g, v_conv_ln_b, v_w_conv_out, v_sgu_ln_g, v_sgu_ln_b, v_w_spatial, v_b_spatial, v_w_sgu_out, v_w_out, v_norm2_g, v_w_up, v_ffn_dw_w, v_ffn_dw_b, v_w_down, v_final_g):
    S, D = x.shape[1], x.shape[2]
    dc = w_conv_out.shape[1]
    ds = w_sgu_out.shape[1]
    G, CH = w_spatial.shape[1], w_spatial.shape[2]
    KC = conv_dw_w.shape[1]
    KF = ffn_dw_w.shape[1]
    F2 = ffn_dw_b.shape[1]
    Fh = F2 // 2
    n_ada = w_ada.shape[2]
    n_up = w_up.shape[2]
    ax, ay, ac = _axes()
    chip = 2 * ax + ay
    me = 2 * chip + ac
    c_idx = jnp.reshape(ac, (1,)).astype(jnp.int32)
    p_idx = jnp.reshape(chip, (1,)).astype(jnp.int32)

    xs = x[0]
    tgt = loss_target[0]

    g1 = _allgather8(_pack([c[0], conv_dw_w[0], ffn_dw_w[0]]), "gather_small_in")
    n_cw, n_fw = conv_dw_w.shape[2], ffn_dw_w.shape[2]
    parts = [_unpack(g1[2 * q], [(D,), conv_dw_w.shape[1:], ffn_dw_w.shape[1:]]) for q in range(N_CHIPS)]
    c_all = jnp.stack([_unpack(g1[d], [(D,)])[0] for d in range(N_DEV)])
    cw_full = jnp.concatenate([pt[1] for pt in parts], axis=1)
    fw_full = jnp.concatenate([pt[2] for pt in parts], axis=1)

    b_cols = lax.dynamic_slice(b_ada, (0, chip * n_ada), (1, n_ada))
    c_pad = jnp.concatenate([c_all, jnp.zeros_like(c_all)], axis=0)
    mod_blk = _ada_fwd("ada_fwd", c_pad, w_ada[0], b_cols)[:N_DEV]
    g2 = _allgather8(_pack([mod_blk]), "gather_mod")
    mod_all = jnp.concatenate([_unpack(g2[2 * q], [(N_DEV, n_ada)])[0] for q in range(N_CHIPS)], axis=1)
    mod = lax.dynamic_slice(mod_all, (me, 0), (1, 6 * D))
    shift1, scale1, gate1, shift2, scale2, gate2 = [mod[:, k * D:(k + 1) * D] for k in range(6)]

    shards = [w_in[0], w_conv_out[0], w_sgu_out[0], w_out[0], w_up[0], w_down[0]]
    names = ["in", "conv_out", "sgu_out", "out", "up", "down"]
    blk = {nm: _cast_into_block(s, p_idx, "cast_" + nm) for s, nm in zip(shards, names)}
    pc_idx = jnp.concatenate([p_idx, c_idx])

    h1 = _modnorm_fwd("modnorm1", xs, norm1_g, scale1, shift1)
    (proj,), (win_xy,) = _mm_fwd_block("proj_own", h1, blk["in"], b_in, p_idx, 0,
                                       job=_job_gather([blk["in"]], rel=(0, 1), fresh=True))
    (proj,), (win_d,) = _mm_fwd_block("proj_x", h1, win_xy, b_in, p_idx, 2, into=proj,
                                      job=_job_gather([blk["in"]], rel=(2,), fresh=True))
    (proj,), (wco_f, wso_f) = _mm_fwd_block("proj_y", h1, win_xy, b_in, p_idx, 1, into=proj,
                                            job=_job_gather([blk["conv_out"], blk["sgu_out"]]))
    (proj,), (wout_f,) = _mm_fwd_block("proj_diag", h1, win_d, b_in, p_idx, 3, into=proj,
                                       job=_job_gather([blk["out"]]))
    win_f = _copy_block(_copy_block(win_xy, blk["in"], p_idx, 0, "fill_w_in_own"), win_d, p_idx, 3, "fill_w_in_diag")
    wout_r = wout_f.reshape(-1, wout_f.shape[2])
    wco_c = _blocks_to_columns(wco_f, "columns_conv_out")
    wso_c = _blocks_to_columns(wso_f, "columns_sgu_out")
    (conv, a_act), (wup_xy,) = _conv_fwd("conv_fwd", proj, cw_full, conv_dw_b, conv_ln_g, conv_ln_b, dc,
                                         job=_job_gather([blk["up"]], rel=(0, 1)))
    bst = jnp.transpose(b_spatial[0])
    (ya, yb, uv, merged), (wup_f,) = _sgu_fwd("sgu_fwd", proj, a_act, wco_c, wso_c, sgu_ln_g, sgu_ln_b, w_spatial[0],
                                              bst, D, ds, job=_job_gather([wup_xy], rel=(2,)))
    out1, x1 = _mm_fwd_rows("out1", merged, wout_r, xs, gate1)
    h2 = _modnorm_fwd("modnorm2", x1, norm2_g, scale2, shift2)
    up, (wdown_f,) = _mm_fwd("up", h2, wup_f, planes=2, job=_job_gather([blk["down"]]))
    wdown_r = wdown_f.reshape(-1, wdown_f.shape[2])
    fw2 = jnp.stack([fw_full[:, :Fh], fw_full[:, Fh:]])
    fb2 = jnp.stack([ffn_dw_b[:, :Fh], ffn_dw_b[:, Fh:]])
    act = _ffn_fwd("ffn_fwd", up, fw2, fb2)
    out2, x2 = _mm_fwd_rows("out2", act, wdown_r, x1, gate2)
    dx2, d_final_g, loss_blk, dout2, d_gate2 = _final("final", x2, tgt, final_g.reshape(1, D), out2, gate2)
    loss = lax.psum(loss_blk[0, 0], ("x", "y", "c"))

    def add_cores(nm, g, r1):
        return _add_own_half(g, r1, pc_idx, "add_cores_" + nm)

    def add_chips(nm, g, r1, r2):
        return _add_chips(g, r1, r2, pc_idx, "add_chips_" + nm)

    g_wdown = _mm_wgrad_rows("wgrad_down", act, dout2, w_down.shape[1])
    dact, (r1_down,) = _mm_dgrad_rows("dgrad_down", dout2, wdown_r, job=_job_swap_halves([g_wdown]))
    s_down = add_cores("down", g_wdown, r1_down)
    (dup, d_ffn), (r2_down,) = _ffn_bwd("ffn_bwd", up, dact, fw2, fb2, job=_job_scatter_blocks([s_down]))
    h_down = add_chips("down", g_wdown, r1_down, r2_down)
    g_wup = _mm_wgrad_cols("wgrad_up", h2, dup, n_up)
    dh2, (r1_up,) = _mm_dgrad_cols("dgrad_up", dup, wup_f, job=_job_swap_halves([g_wup]))
    s_up = add_cores("up", g_wup, r1_up)
    dx1, d_norm2, d_scale2, d_shift2, dout1, d_gate1 = _modnorm_bwd(
        "modnorm2_bwd", x1, dh2, dx2, norm2_g, scale2, shift2, gated=(out1, gate1))
    g_wout = _mm_wgrad_rows("wgrad_out", merged, dout1, w_out.shape[1])
    dmerged = _mm_dgrad_rows("dgrad_out", dout1, wout_r)
    dya, dyb, dproj, dbin_g = _merge_bwd("merge_bwd", dmerged, proj, ya, yb, D)
    g_wco = _mm_wgrad_cols("wgrad_conv_out", a_act, dya[None], w_conv_out.shape[2])
    g_wso = _mm_wgrad_cols("wgrad_sgu_out", uv, dyb[None], w_sgu_out.shape[2])
    da_act, (r1_out, r1_co, r1_so) = _mm_dgrad_rows("dgrad_conv_out", dya, wco_c,
                                                    job=_job_swap_halves([g_wout, g_wco, g_wso]))
    s_out = add_cores("out", g_wout, r1_out)
    s_co = add_cores("conv_out", g_wco, r1_co)
    s_so = add_cores("sgu_out", g_wso, r1_so)
    duv = _mm_dgrad_rows("dgrad_sgu_out", dyb, wso_c)
    dproj, d_ws, d_bs, d_sgu_g, d_sgu_b, dbin_s = _sgu_bwd("sgu_bwd", proj, duv, sgu_ln_g, sgu_ln_b, w_spatial[0], bst,
                                                           dproj, ds)
    dconv, d_cln_g, d_cln_b, d_conv_b = _conv_bwd_ln("conv_ln_bwd", conv, da_act, conv_ln_g, conv_ln_b)
    d_fw = jnp.concatenate([d_ffn[0, :KF], d_ffn[1, :KF]], axis=1)
    d_fb = jnp.concatenate([d_ffn[0, KF:KF + 1], d_ffn[1, KF:KF + 1]], axis=1)
    early = [d_sgu_g, d_sgu_b, d_ws, d_bs[:, :, 0], d_norm2, d_fb, d_final_g, d_fw]
    (dproj, d_cw, dbin_a), (r2_up, g_early) = _conv_bwd(
        "conv_bwd", dconv, proj, cw_full, dproj, dc,
        job=_merge_jobs([_job_scatter_blocks([s_up]), _job_allgather8(_pack(early))]))
    h_up = add_chips("up", g_wup, r1_up, r2_up)
    g_win, (r2_out, r2_co, r2_so, sib_up, sib_down) = _mm_wgrad_cols(
        "wgrad_in", h1, dproj[None], w_in.shape[2],
        job=_merge_jobs([_job_scatter_blocks([s_out, s_co, s_so]), _job_to_sibling([h_up, h_down])]))
    h_out = add_chips("out", g_wout, r1_out, r2_out)
    h_co = add_chips("conv_out", g_wco, r1_co, r2_co)
    h_so = add_chips("sgu_out", g_wso, r1_so, r2_so)
    dh1, (r1_in, sib_out, sib_co, sib_so) = _mm_dgrad_cols(
        "dgrad_in_a", dproj[None], win_f, row_tiles=(0.0, 0.5),
        job=_merge_jobs([_job_swap_halves([g_win]), _job_to_sibling([h_out, h_co, h_so])]))
    s_in = add_cores("in", g_win, r1_in)
    dh1, (r2_in,) = _mm_dgrad_cols("dgrad_in_b", dproj[None], win_f, row_tiles=(0.5, 0.5), fill_into=dh1,
                                   job=_job_scatter_blocks([s_in]))
    h_in = add_chips("in", g_win, r1_in, r2_in)
    dxs, d_norm1, d_scale1, d_shift1 = _modnorm_bwd("modnorm1_bwd", xs, dh1, dx1, norm1_g, scale1, shift1)

    d_mod = jnp.concatenate([d_shift1, d_scale1, d_gate1, d_shift2, d_scale2, d_gate2], axis=1)
    d_b_in = jnp.concatenate([dbin_a, dbin_s, dbin_g], axis=1)
    late = [d_mod, d_norm1, d_b_in, d_conv_b, d_cln_g, d_cln_b, d_cw[:KC]]
    g3, sib_in = _run_job(_merge_jobs([_job_allgather8(_pack(late)), _job_to_sibling([h_in])]), "gather_small_grads")
    big_halves = {"w_in": (h_in, sib_in), "w_conv_out": (h_co, sib_co), "w_sgu_out": (h_so, sib_so),
                  "w_out": (h_out, sib_out), "w_up": (h_up, sib_up), "w_down": (h_down, sib_down)}
    g_b_ada, g_norm1, g_b_in, g_conv_b, g_cln_g, g_cln_b, g_cw_full = _unpack(
        _sum8(g3, "sum_small_grads"), [a.shape for a in late])
    g_sgu_g, g_sgu_b, g_ws, g_bs, g_norm2, g_fb, g_final, g_fw_full = _unpack(
        _sum8(g_early, "sum_early_grads"), [a.shape for a in early])
    g_cw = lax.dynamic_slice(g_cw_full, (0, chip * n_cw), (KC, n_cw))
    g_fw = lax.dynamic_slice(g_fw_full, (0, chip * n_fw), (KF, n_fw))
    dmod_all = g3.reshape(N_DEV, -1)[:, :6 * D]
    dmod_cols = lax.dynamic_slice(dmod_all, (0, chip * n_ada), (N_DEV, n_ada))
    g_wada = _ada_wgrad("ada_wgrad", jnp.transpose(c_all), dmod_cols)

    grads = {
        "w_ada": g_wada[None], "b_ada": g_b_ada, "norm1_g": g_norm1, "b_in": g_b_in,
        "conv_dw_w": g_cw[None], "conv_dw_b": g_conv_b, "conv_ln_g": g_cln_g, "conv_ln_b": g_cln_b,
        "sgu_ln_g": g_sgu_g, "sgu_ln_b": g_sgu_b, "w_spatial": g_ws[None],
        "b_spatial": g_bs[None], "norm2_g": g_norm2, "ffn_dw_w": g_fw[None], "ffn_dw_b": g_fb,
        "final_g": g_final.reshape(D),
    }
    weights = dict(w_ada=w_ada, b_ada=b_ada, norm1_g=norm1_g, w_in=w_in, b_in=b_in, conv_dw_w=conv_dw_w, conv_dw_b=conv_dw_b, conv_ln_g=conv_ln_g, conv_ln_b=conv_ln_b, w_conv_out=w_conv_out, sgu_ln_g=sgu_ln_g, sgu_ln_b=sgu_ln_b, w_spatial=w_spatial, b_spatial=b_spatial, w_sgu_out=w_sgu_out, w_out=w_out, norm2_g=norm2_g, w_up=w_up, ffn_dw_w=ffn_dw_w, ffn_dw_b=ffn_dw_b, w_down=w_down, final_g=final_g)
    m_in = dict(w_ada=m_w_ada, b_ada=m_b_ada, norm1_g=m_norm1_g, w_in=m_w_in, b_in=m_b_in, conv_dw_w=m_conv_dw_w, conv_dw_b=m_conv_dw_b, conv_ln_g=m_conv_ln_g, conv_ln_b=m_conv_ln_b, w_conv_out=m_w_conv_out, sgu_ln_g=m_sgu_ln_g, sgu_ln_b=m_sgu_ln_b, w_spatial=m_w_spatial, b_spatial=m_b_spatial, w_sgu_out=m_w_sgu_out, w_out=m_w_out, norm2_g=m_norm2_g, w_up=m_w_up, ffn_dw_w=m_ffn_dw_w, ffn_dw_b=m_ffn_dw_b, w_down=m_w_down, final_g=m_final_g)
    v_in = dict(w_ada=v_w_ada, b_ada=v_b_ada, norm1_g=v_norm1_g, w_in=v_w_in, b_in=v_b_in, conv_dw_w=v_conv_dw_w, conv_dw_b=v_conv_dw_b, conv_ln_g=v_conv_ln_g, conv_ln_b=v_conv_ln_b, w_conv_out=v_w_conv_out, sgu_ln_g=v_sgu_ln_g, sgu_ln_b=v_sgu_ln_b, w_spatial=v_w_spatial, b_spatial=v_b_spatial, w_sgu_out=v_w_sgu_out, w_out=v_w_out, norm2_g=v_norm2_g, w_up=v_w_up, ffn_dw_w=v_ffn_dw_w, ffn_dw_b=v_ffn_dw_b, w_down=v_w_down, final_g=v_final_g)
    order = list(weights.keys())
    large = ["w_ada", "w_in", "w_conv_out", "w_sgu_out", "w_out", "w_up", "w_down"]
    little = [n for n in order if n not in large]
    delta, new_m, new_v = {}, {}, {}
    for n in large:
        shp = weights[n].shape
        two = (shp[1], shp[2])
        if n in big_halves:
            g_, d_, m_, v_ = _adamw_halves(weights[n].reshape(two), big_halves[n][0], big_halves[n][1],
                                           m_in[n].reshape(two), v_in[n].reshape(two), c_idx, "adamw_" + n)
            grads[n] = g_.reshape(shp)
        else:
            d_, m_, v_ = _adamw(weights[n].reshape(two), grads[n].reshape(two), m_in[n].reshape(two),
                                v_in[n].reshape(two), "adamw_" + n)
        delta[n], new_m[n], new_v[n] = d_.reshape(shp), m_.reshape(shp), v_.reshape(shp)
    shapes = [weights[n].shape for n in little]
    d_, m_, v_ = _adamw(_pack([weights[n] for n in little]), _pack([grads[n] for n in little]),
                        _pack([m_in[n] for n in little]), _pack([v_in[n] for n in little]), "adamw_small")
    for n, dd, mm, vv in zip(little, _unpack(d_, shapes), _unpack(m_, shapes), _unpack(v_, shapes)):
        delta[n], new_m[n], new_v[n] = dd, mm, vv
    grad_out = [grads[n].reshape(weights[n].shape) for n in order]
    return (loss, dxs[None], *grad_out, *[delta[n] for n in order], *[new_m[n] for n in order],
            *[new_v[n] for n in order])
```

```python
import functools

import jax
import jax.numpy as jnp
from jax import lax
from jax.experimental import pallas as pl
from jax.experimental.pallas import tpu as pltpu

F32 = jnp.float32
BF16 = jnp.bfloat16
EPS = 1e-6
MESH = pl.DeviceIdType.MESH
N_CHIPS = 4
N_DEV = 8
LANES = 128
SUBLANES = 8
CONV_HALO = 32
FFN_HALO = 8
CONV_CHUNK_ROWS = 32
VMEM_LIMIT_BYTES = 56 * 1024 * 1024
WHOLE_WEIGHT_BYTES = 8 * 1024 * 1024
SMALL_TILE_BYTES = 2 * 1024 * 1024

ADAM_LR = 0.001
ADAM_B1 = 0.9
ADAM_B2 = 0.999
ADAM_EPS = 1e-08
ADAM_WD = 0.01
ADAM_STEP = 10

NN = (((1,), (0,)), ((), ()))
NT = (((1,), (1,)), ((), ()))
TN = (((0,), (0,)), ((), ()))


def _params(sem=None):
    return pltpu.CompilerParams(dimension_semantics=sem, vmem_limit_bytes=VMEM_LIMIT_BYTES)


def _pick(dim, pref, mult):
    best = None
    d = mult
    while d <= min(dim, pref):
        if dim % d == 0:
            best = d
        d += mult
    return dim if best is None else best


def _axes():
    return lax.axis_index("x"), lax.axis_index("y"), lax.axis_index("c")


def _modnorm(x, g, scale, shift):
    r = lax.rsqrt(jnp.mean(x * x, axis=-1, keepdims=True) + EPS)
    return (x * r * g) * (1.0 + scale) + shift


def _layer_norm(x, g, b):
    mu = jnp.mean(x, axis=-1, keepdims=True)
    var = jnp.mean(jnp.square(x - mu), axis=-1, keepdims=True)
    return (x - mu) * lax.rsqrt(var + EPS) * g + b


def _gelu(x):
    return 0.5 * x * (1.0 + lax.erf(x * (0.5 ** 0.5)))


def _ln_silu(x, g, b):
    return jax.nn.silu(_layer_norm(x, g, b))


def _tril_mask(ws):
    n = ws.shape[-1]
    row = lax.broadcasted_iota(jnp.int32, (n, n), 0)
    col = lax.broadcasted_iota(jnp.int32, (n, n), 1)
    return jnp.where(row >= col, ws, 0.0)


def _pack(arrs):
    flat = [a.reshape(-1).astype(F32) for a in arrs]
    total = sum(f.shape[0] for f in flat)
    tile = SUBLANES * LANES
    padded = -(-total // tile) * tile
    if padded > total:
        flat = flat + [jnp.zeros((padded - total,), F32)]
    return jnp.concatenate(flat).reshape(padded // LANES, LANES)


def _unpack(buf, shapes):
    flat = buf.reshape(-1)
    out, off = [], 0
    for s in shapes:
        n = 1
        for d in s:
            n *= d
        out.append(flat[off:off + n].reshape(s))
        off += n
    return out


def _allgather8(buf, name):
    R, L = buf.shape

    def body(in_ref, out_ref, send_sems, recv_sems, local_sem):
        x, y, c = _axes()
        me = 4 * x + 2 * y + c
        mine = pltpu.make_async_copy(in_ref, out_ref.at[me], local_sem)
        mine.start()
        peers = []
        for k in range(1, N_DEV):
            px = 1 - x if k & 4 else x
            py = 1 - y if k & 2 else y
            pc = 1 - c if k & 1 else c
            peers.append((px, py, pc))
        sends = []
        for k, peer in enumerate(peers):
            cp = pltpu.make_async_remote_copy(
                src_ref=in_ref, dst_ref=out_ref.at[me], send_sem=send_sems.at[k], recv_sem=recv_sems.at[k],
                device_id=peer, device_id_type=MESH)
            cp.start()
            sends.append(cp)
        for k, (px, py, pc) in enumerate(peers):
            pltpu.make_async_remote_copy(
                src_ref=in_ref, dst_ref=out_ref.at[4 * px + 2 * py + pc], send_sem=send_sems.at[k],
                recv_sem=recv_sems.at[k], device_id=(px, py, pc), device_id_type=MESH).wait_recv()
        for cp in sends:
            cp.wait_send()
        mine.wait()

    return pl.pallas_call(
        body, name=name,
        out_shape=jax.ShapeDtypeStruct((N_DEV, R, L), buf.dtype),
        in_specs=[pl.BlockSpec(memory_space=pltpu.VMEM)],
        out_specs=pl.BlockSpec(memory_space=pltpu.VMEM),
        scratch_shapes=[pltpu.SemaphoreType.DMA((N_DEV - 1,)), pltpu.SemaphoreType.DMA((N_DEV - 1,)),
                        pltpu.SemaphoreType.DMA],
        compiler_params=pltpu.CompilerParams(vmem_limit_bytes=VMEM_LIMIT_BYTES),
    )(buf)


def _other_chips(x, y):
    return [(1 - x, y), (x, 1 - y), (1 - x, 1 - y)]


def _cast_into_block(shard, p_idx, name):
    K, n = shard.shape
    tr = _pick(K, max(SUBLANES, (1 << 19) // n), 2 * SUBLANES)

    def body(p_ref, s_ref, o_ref):
        o_ref[...] = s_ref[...].astype(BF16)

    return pl.pallas_call(
        body, name=name,
        grid_spec=pltpu.PrefetchScalarGridSpec(
            num_scalar_prefetch=1, grid=(K // tr,),
            in_specs=[pl.BlockSpec((tr, n), lambda i, p: (i, 0))],
            out_specs=pl.BlockSpec((None, tr, n), lambda i, p: (p[0], i, 0))),
        out_shape=jax.ShapeDtypeStruct((N_CHIPS, K, n), BF16),
        compiler_params=_params(("parallel",)),
    )(p_idx, shard)


class _Job:
    def __init__(self, ins, outs, aliases, n_sems, make):
        self.ins, self.outs, self.aliases, self.n_sems, self.make = list(ins), list(outs), list(aliases), n_sems, make


def _merge_jobs(jobs):
    ins, outs, aliases, offs = [], [], [], []
    n_sems = 0
    for jb in jobs:
        offs.append((len(ins), len(outs), n_sems))
        aliases += [(len(ins) + a, len(outs) + b) for a, b in jb.aliases]
        ins += jb.ins
        outs += jb.outs
        n_sems += jb.n_sems

    def make(in_refs, out_refs, send_sems, recv_sems, base=0):
        made = []
        for jb, (oi, oo, os_) in zip(jobs, offs):
            made.append(jb.make(in_refs[oi:oi + len(jb.ins)], out_refs[oo:oo + len(jb.outs)],
                                send_sems, recv_sems, base + os_))

        def start():
            for st, _ in made:
                st()

        def finish():
            for _, fin in made:
                fin()

        return start, finish

    return _Job(ins, outs, aliases, n_sems, make)


def _job_gather(fulls, rel=(0, 1, 2), fresh=False):
    nw = len(fulls)

    def make(in_refs, outs, send_sems, recv_sems, base=0):
        x, y, c = _axes()
        p = 2 * x + y
        chips = _other_chips(x, y)
        srcs = in_refs if fresh else outs

        def rows(w, mine):
            kh = outs[w].shape[1] // 2
            return pl.ds((c if mine else 1 - c) * kh, kh)

        def over_ici(w, j, block):
            qx, qy = chips[j]
            return pltpu.make_async_remote_copy(
                src_ref=srcs[w].at[block, rows(w, True)], dst_ref=outs[w].at[block, rows(w, True)],
                send_sem=send_sems.at[base + 6 * w + j], recv_sem=recv_sems.at[base + 6 * w + j],
                device_id=(qx, qy, c), device_id_type=MESH)

        def over_d2d(w, j, mine):
            qx, qy = chips[j]
            return pltpu.make_async_remote_copy(
                src_ref=outs[w].at[2 * qx + qy, rows(w, mine)], dst_ref=outs[w].at[2 * qx + qy, rows(w, mine)],
                send_sem=send_sems.at[base + 6 * w + 3 + j], recv_sem=recv_sems.at[base + 6 * w + 3 + j],
                device_id=(x, y, 1 - c), device_id_type=MESH)

        def start():
            for w in range(nw):
                for j in rel:
                    over_ici(w, j, p).start()

        def finish():
            for w in range(nw):
                for j in rel:
                    qx, qy = chips[j]
                    over_ici(w, j, 2 * qx + qy).wait_recv()
                    over_d2d(w, j, True).start()
            for w in range(nw):
                for j in rel:
                    over_d2d(w, j, False).wait_recv()
            for w in range(nw):
                for j in rel:
                    over_ici(w, j, p).wait_send()
                    over_d2d(w, j, True).wait_send()

        return start, finish

    return _Job(fulls, [jax.ShapeDtypeStruct(f.shape, f.dtype) for f in fulls],
                [] if fresh else [(w, w) for w in range(nw)], 6 * nw, make)


def _job_simple(ins, outs, n_per, copies_of):
    nw = len(ins)

    def make(in_refs, out_refs, send_sems, recv_sems, base=0):
        x, y, c = _axes()
        copies = []
        for w in range(nw):
            for j, (src, dst, dev) in enumerate(copies_of(w, in_refs[w], out_refs[w], x, y, c)):
                k = base + n_per * w + j
                copies.append(pltpu.make_async_remote_copy(
                    src_ref=src, dst_ref=dst, send_sem=send_sems.at[k], recv_sem=recv_sems.at[k],
                    device_id=dev, device_id_type=MESH))

        def start():
            for cp in copies:
                cp.start()

        def finish():
            for cp in copies:
                cp.wait()

        return start, finish

    return _Job(ins, outs, [], n_per * nw, make)


def _job_swap_halves(grads):
    def copies_of(w, src, dst, x, y, c):
        kh = src.shape[1] // 2
        return [(src.at[:, pl.ds((1 - c) * kh, kh), :], dst, (x, y, 1 - c))]

    outs = [jax.ShapeDtypeStruct((g.shape[0], g.shape[1] // 2, g.shape[2]), g.dtype) for g in grads]
    return _job_simple(grads, outs, 1, copies_of)


def _job_scatter_blocks(sums):
    def copies_of(w, src, dst, x, y, c):
        return [(src.at[2 * qx + qy], dst.at[j], (qx, qy, c)) for j, (qx, qy) in enumerate(_other_chips(x, y))]

    outs = [jax.ShapeDtypeStruct((3,) + s.shape[1:], s.dtype) for s in sums]
    return _job_simple(sums, outs, 3, copies_of)


def _job_to_sibling(arrs):
    def copies_of(w, src, dst, x, y, c):
        return [(src, dst, (x, y, 1 - c))]

    outs = [jax.ShapeDtypeStruct(a.shape, a.dtype) for a in arrs]
    return _job_simple(arrs, outs, 1, copies_of)


def _job_allgather8(buf):
    def make(in_refs, out_refs, send_sems, recv_sems, base=0):
        x, y, c = _axes()
        me = 4 * x + 2 * y + c
        src, dst = in_refs[0], out_refs[0]
        peers = [(1 - x if k & 4 else x, 1 - y if k & 2 else y, 1 - c if k & 1 else c) for k in range(1, N_DEV)]
        mine = pltpu.make_async_copy(src, dst.at[me], send_sems.at[base + N_DEV - 1])

        def to(k):
            return pltpu.make_async_remote_copy(
                src_ref=src, dst_ref=dst.at[me], send_sem=send_sems.at[base + k], recv_sem=recv_sems.at[base + k],
                device_id=peers[k], device_id_type=MESH)

        def of(k):
            px, py, pc = peers[k]
            return pltpu.make_async_remote_copy(
                src_ref=src, dst_ref=dst.at[4 * px + 2 * py + pc], send_sem=send_sems.at[base + k],
                recv_sem=recv_sems.at[base + k], device_id=peers[k], device_id_type=MESH)

        def start():
            mine.start()
            for k in range(N_DEV - 1):
                to(k).start()

        def finish():
            for k in range(N_DEV - 1):
                of(k).wait_recv()
                to(k).wait_send()
            mine.wait()

        return start, finish

    return _Job([buf], [jax.ShapeDtypeStruct((N_DEV,) + buf.shape, buf.dtype)], [], N_DEV, make)


def _run_job(job, name):
    ni, no = len(job.ins), len(job.outs)

    def body(*refs):
        start, finish = job.make(refs[:ni], refs[ni:ni + no], refs[-2], refs[-1])
        start()
        finish()

    any_spec = pl.BlockSpec(memory_space=pl.ANY)
    return pl.pallas_call(
        body, name=name, out_shape=job.outs, in_specs=[any_spec] * ni, out_specs=[any_spec] * no,
        input_output_aliases=dict(job.aliases),
        scratch_shapes=[pltpu.SemaphoreType.DMA((job.n_sems,)), pltpu.SemaphoreType.DMA((job.n_sems,))],
    )(*job.ins)


def _call(body, name, grid, in_specs, out_specs, out_shape, scratch_shapes, semantics, args, aliases=None, job=None,
          prefetch=()):
    in_specs, out_specs, out_shape = list(in_specs), list(out_specs), list(out_shape)
    scratch_shapes = list(scratch_shapes)
    n_pre, n_in, n_out, n_scr = len(prefetch), len(args), len(out_shape), len(scratch_shapes)
    all_aliases = {n_pre + a: b for a, b in (aliases or {}).items()}
    if job is None:
        wrapped, extra_in, semantics = body, [], semantics
    else:
        ni, no = len(job.ins), len(job.outs)

        def wrapped(*refs):
            pre, refs = refs[:n_pre], refs[n_pre:]
            ins, cins = refs[:n_in], refs[n_in:n_in + ni]
            outs, couts = refs[n_in + ni:n_in + ni + n_out], refs[n_in + ni + n_out:n_in + ni + n_out + no]
            scr = refs[n_in + ni + n_out + no:n_in + ni + n_out + no + n_scr]
            start, finish = job.make(cins, couts, refs[-2], refs[-1])
            first = functools.reduce(jnp.logical_and, [pl.program_id(a) == 0 for a in range(len(grid))])
            last = functools.reduce(jnp.logical_and, [pl.program_id(a) == grid[a] - 1 for a in range(len(grid))])
            pl.when(first)(start)
            body(*pre, *ins, *outs, *scr)
            pl.when(last)(finish)

        any_spec = pl.BlockSpec(memory_space=pl.ANY)
        for a, b in job.aliases:
            all_aliases[n_pre + n_in + a] = n_out + b
        in_specs, out_specs, out_shape = in_specs + [any_spec] * ni, out_specs + [any_spec] * no, out_shape + job.outs
        scratch_shapes = scratch_shapes + [pltpu.SemaphoreType.DMA((job.n_sems,)), pltpu.SemaphoreType.DMA((job.n_sems,))]
        extra_in, semantics = job.ins, tuple("arbitrary" for _ in grid)
    if n_pre:
        return pl.pallas_call(
            wrapped, name=name,
            grid_spec=pltpu.PrefetchScalarGridSpec(num_scalar_prefetch=n_pre, grid=grid, in_specs=in_specs,
                                                   out_specs=out_specs, scratch_shapes=scratch_shapes),
            out_shape=out_shape, input_output_aliases=all_aliases,
            compiler_params=_params(semantics))(*prefetch, *args, *extra_in)
    return pl.pallas_call(
        wrapped, name=name, grid=grid, in_specs=in_specs, out_specs=out_specs, out_shape=out_shape,
        scratch_shapes=scratch_shapes, input_output_aliases=all_aliases,
        compiler_params=_params(semantics))(*args, *extra_in)


def _add_own_half(g, r, pc_idx, name):
    nb, K, n = g.shape
    kh = K // 2
    tr = _pick(kh, max(SUBLANES, (1 << 19) // n), 2 * SUBLANES)
    per = kh // tr

    def body(pc_ref, g_ref, r_ref, o_ref):
        o_ref[...] = (g_ref[...] + r_ref[...]).astype(BF16)

    def other(b, pc):
        return jnp.bitwise_xor(pc[0], b + 1)

    return pl.pallas_call(
        body, name=name,
        grid_spec=pltpu.PrefetchScalarGridSpec(
            num_scalar_prefetch=1, grid=(nb - 1, per),
            in_specs=[pl.BlockSpec((None, tr, n), lambda b, i, pc: (other(b, pc), pc[1] * per + i, 0)),
                      pl.BlockSpec((None, tr, n), lambda b, i, pc: (other(b, pc), i, 0))],
            out_specs=pl.BlockSpec((None, tr, n), lambda b, i, pc: (other(b, pc), i, 0))),
        out_shape=jax.ShapeDtypeStruct((nb, kh, n), BF16),
        compiler_params=_params(("parallel", "parallel")),
    )(pc_idx, g, r)


def _add_chips(g, r1, r2, pc_idx, name):
    _, K, n = g.shape
    kh = K // 2
    tr = _pick(kh, max(SUBLANES, (1 << 19) // n), 2 * SUBLANES)
    per = kh // tr

    def body(pc_ref, g_ref, r1_ref, r2_ref, o_ref):
        own = g_ref[...] + r1_ref[...]
        o_ref[...] = ((own + r2_ref[0].astype(F32)) + r2_ref[1].astype(F32)) + r2_ref[2].astype(F32)

    return pl.pallas_call(
        body, name=name,
        grid_spec=pltpu.PrefetchScalarGridSpec(
            num_scalar_prefetch=1, grid=(per,),
            in_specs=[pl.BlockSpec((None, tr, n), lambda i, pc: (pc[0], pc[1] * per + i, 0)),
                      pl.BlockSpec((None, tr, n), lambda i, pc: (pc[0], i, 0)),
                      pl.BlockSpec((3, tr, n), lambda i, pc: (0, i, 0))],
            out_specs=pl.BlockSpec((tr, n), lambda i, pc: (i, 0))),
        out_shape=jax.ShapeDtypeStruct((kh, n), F32),
        compiler_params=_params(("parallel",)),
    )(pc_idx, g, r1, r2)


def _sum8(g, name):
    _, R, L = g.shape

    def body(g_ref, o_ref):
        acc = g_ref[0]
        for d in range(1, N_DEV):
            acc = acc + g_ref[d]
        o_ref[...] = acc

    return pl.pallas_call(
        body, name=name, out_shape=jax.ShapeDtypeStruct((R, L), F32),
        in_specs=[pl.BlockSpec(memory_space=pltpu.VMEM)], out_specs=pl.BlockSpec(memory_space=pltpu.VMEM),
        compiler_params=_params(),
    )(g)


def _adamw_math(w, gg, m, v):
    nm = ADAM_B1 * m + (1.0 - ADAM_B1) * gg
    nv = ADAM_B2 * v + (1.0 - ADAM_B2) * jnp.square(gg)
    m_hat = nm / (1.0 - ADAM_B1 ** ADAM_STEP)
    v_hat = nv / (1.0 - ADAM_B2 ** ADAM_STEP)
    return -ADAM_LR * (m_hat / (jnp.sqrt(v_hat) + ADAM_EPS) + ADAM_WD * w), nm, nv


def _adamw(w, g, m, v, name):
    R, C = w.shape
    tr = _pick(R, max(SUBLANES, (1 << 18) // C), SUBLANES)

    def body(w_ref, g_ref, m_ref, v_ref, d_ref, nm_ref, nv_ref):
        d_ref[...], nm_ref[...], nv_ref[...] = _adamw_math(w_ref[...], g_ref[...], m_ref[...], v_ref[...])

    spec = pl.BlockSpec((tr, C), lambda i: (i, 0))
    sd = jax.ShapeDtypeStruct((R, C), F32)
    return pl.pallas_call(
        body, name=name, grid=(R // tr,), in_specs=[spec] * 4, out_specs=[spec] * 3, out_shape=[sd] * 3,
        compiler_params=_params(("parallel",)),
    )(w, g, m, v)


def _adamw_halves(w, g_own, g_sib, m, v, c_idx, name):
    K, n = w.shape
    kh = K // 2
    tr = _pick(kh, max(SUBLANES, (1 << 18) // n), SUBLANES)
    per = kh // tr

    def body(c_ref, w_ref, go_ref, gs_ref, m_ref, v_ref, g_ref, d_ref, nm_ref, nv_ref):
        h = pl.program_id(0)

        def step(gg):
            g_ref[...] = gg
            d_ref[...], nm_ref[...], nv_ref[...] = _adamw_math(w_ref[...], gg, m_ref[...], v_ref[...])

        @pl.when(h == 0)
        def _():
            step(go_ref[...])

        @pl.when(h == 1)
        def _():
            step(gs_ref[...])

    full = pl.BlockSpec((tr, n), lambda h, i, c: (((c[0] + h) % 2) * per + i, 0))
    own = pl.BlockSpec((tr, n), lambda h, i, c: (i * (1 - h), 0))
    sib = pl.BlockSpec((tr, n), lambda h, i, c: (i * h, 0))
    sd = jax.ShapeDtypeStruct((K, n), F32)
    return pl.pallas_call(
        body, name=name,
        grid_spec=pltpu.PrefetchScalarGridSpec(
            num_scalar_prefetch=1, grid=(2, per),
            in_specs=[full, own, sib, full, full], out_specs=[full] * 4),
        out_shape=[sd] * 4,
        compiler_params=_params(("arbitrary", "arbitrary")),
    )(c_idx, w, g_own, g_sib, m, v)


def _matmul(name, grid, dims, a, a_spec, b, b_spec, outs, out_specs, acc_shape,
            extras=(), extra_specs=(), epilogue=None, job=None, fill_into=None, prefetch=()):
    nk = grid[2]
    npre = len(prefetch)
    aliases = None
    if fill_into is not None:
        aliases = {2 + len(extras): 0}
        extras = tuple(extras) + (fill_into,)
        extra_specs = tuple(extra_specs) + (pl.BlockSpec(memory_space=pl.ANY),)
    nex = len(extras)
    nout = len(outs)

    def body(*refs):
        a_ref, b_ref, rest = refs[npre], refs[npre + 1], refs[npre + 2:]
        ex, o = rest[:nex], rest[nex:nex + nout]
        part = lax.dot_general(a_ref[...].astype(BF16), b_ref[...].astype(BF16), dims,
                               preferred_element_type=F32)

        def finish(res):
            if epilogue is None:
                o[0][...] = res.astype(o[0].dtype)
            else:
                epilogue(res, ex, o)

        if nk == 1:
            finish(part)
        else:
            acc = o[0] if in_place else rest[-1]
            k = pl.program_id(2)

            @pl.when(k == 0)
            def _():
                acc[...] = part

            @pl.when(k > 0)
            def _():
                acc[...] += part

            if not in_place:
                @pl.when(k == nk - 1)
                def _():
                    finish(acc[...])

    in_place = epilogue is None and nout == 1 and outs[0].dtype == F32
    res = _call(body, name, grid, [a_spec, b_spec] + list(extra_specs), out_specs, outs,
                [] if nk == 1 or in_place else [pltpu.VMEM(acc_shape, F32)], ("parallel", "parallel", "arbitrary"),
                (a, b, *extras), aliases=aliases, job=job, prefetch=prefetch)
    return res if job is None else (res[:nout], res[nout:])


def _first(res, job):
    return res[0] if job is None else (res[0][0], res[1])


def _mm_fwd(name, a, wfull, out_dtype=F32, planes=1, bias=None, job=None):
    S, K = a.shape
    _, _, n = wfull.shape
    N = N_CHIPS * n
    tm = _pick(S, 1024, LANES)
    tn = _pick(n, 1408, LANES)
    per = n // tn
    nj = N // tn
    pj = nj // planes
    grid = (S // tm, nj, 1)
    a_spec = pl.BlockSpec((tm, K), lambda i, j, k: (i, 0))
    b_spec = pl.BlockSpec((None, K, tn), lambda i, j, k: (j // per, 0, j % per))
    o_spec = pl.BlockSpec((None, tm, tn), lambda i, j, k: (j // pj, i, j % pj))
    sd = jax.ShapeDtypeStruct((planes, S, N // planes), out_dtype)
    if bias is not None:
        def epi(res, ex, o):
            o[0][...] = (res + ex[0][...]).astype(o[0].dtype)

        out = _matmul(name, grid, NN, a, a_spec, wfull, b_spec, [sd], [o_spec], (tm, tn), extras=(bias,),
                      extra_specs=(pl.BlockSpec((1, tn), lambda i, j, k: (0, j)),), epilogue=epi, job=job)
    else:
        out = _matmul(name, grid, NN, a, a_spec, wfull, b_spec, [sd], [o_spec], (tm, tn), job=job)
    return _first(out, job)


def _mm_fwd_block(name, a, wfull, bias, p_idx, mask, into=None, job=None):
    S, K = a.shape
    _, _, n = wfull.shape
    N = N_CHIPS * n
    tm = _pick(S, 1024, LANES)
    tn = _pick(n, 1408, LANES)
    per = n // tn
    grid = (S // tm, per, 1)
    a_spec = pl.BlockSpec((tm, K), lambda i, j, k, p: (i, 0))
    b_spec = pl.BlockSpec((None, K, tn), lambda i, j, k, p: (jnp.bitwise_xor(p[0], mask), 0, j))
    o_spec = pl.BlockSpec((tm, tn), lambda i, j, k, p: (i, jnp.bitwise_xor(p[0], mask) * per + j))
    v_spec = pl.BlockSpec((1, tn), lambda i, j, k, p: (0, jnp.bitwise_xor(p[0], mask) * per + j))

    def epi(res, ex, o):
        o[0][...] = res + ex[0][...]

    return _matmul(name, grid, NN, a, a_spec, wfull, b_spec, [jax.ShapeDtypeStruct((S, N), F32)], [o_spec], (tm, tn),
                   extras=(bias,), extra_specs=(v_spec,), epilogue=epi, job=job, fill_into=into, prefetch=(p_idx,))


def _copy_block(dst, src, p_idx, mask, name):
    _, K, n = dst.shape
    tr = _pick(K, max(SUBLANES, (1 << 19) // n), 2 * SUBLANES)

    def body(p_ref, s_ref, d_any, o_ref):
        o_ref[...] = s_ref[...]

    spec = pl.BlockSpec((None, tr, n), lambda i, p: (jnp.bitwise_xor(p[0], mask), i, 0))
    return _call(body, name, (K // tr,), [spec, pl.BlockSpec(memory_space=pl.ANY)], [spec],
                 [jax.ShapeDtypeStruct(dst.shape, dst.dtype)], [], ("parallel",), (src, dst), aliases={1: 0},
                 prefetch=(p_idx,))[0]


def _mm_fwd_rows(name, a, wrows, resid, gate, job=None):
    S, K = a.shape
    _, N = wrows.shape
    whole = K * N * 2 <= WHOLE_WEIGHT_BYTES
    tm = _pick(S, 512 if whole else 1024, LANES)
    tk = K if whole else _pick(K, 2816, LANES)
    tn = N if whole else _pick(N, 1024, LANES)
    grid = (S // tm, N // tn, K // tk)
    a_spec = pl.BlockSpec((tm, tk), lambda i, j, k: (i, k))
    if tn == N and tk == K:
        b_spec = pl.BlockSpec((tk, tn), lambda i, j, k: (k, j), pipeline_mode=pl.Buffered(1))
    else:
        b_spec = pl.BlockSpec((tk, tn), lambda i, j, k: (k, j))
    o_spec = pl.BlockSpec((tm, tn), lambda i, j, k: (i, j))
    g_spec = pl.BlockSpec((1, tn), lambda i, j, k: (0, j))

    def epi(res, ex, o):
        o[0][...] = res.astype(BF16)
        o[1][...] = ex[0][...] + ex[1][...] * res

    return _matmul(name, grid, NN, a, a_spec, wrows, b_spec,
                   [jax.ShapeDtypeStruct((S, N), BF16), jax.ShapeDtypeStruct((S, N), F32)], [o_spec, o_spec], (tm, tn),
                   extras=(resid, gate), extra_specs=(o_spec, g_spec), epilogue=epi, job=job)


def _mm_dgrad_cols(name, dy, wfull, out_dtype=F32, job=None, row_tiles=None, fill_into=None):
    planes, S, npl = dy.shape
    _, K, n = wfull.shape
    tm = _pick(S if row_tiles is None else S // 2, 1024, LANES)
    to = _pick(K, 1024, LANES)
    tk = _pick(n, 2816, LANES)
    per = n // tk
    nk = N_CHIPS * per
    pk = nk // planes
    n_tiles = S // tm
    i0, ni = (0, n_tiles) if row_tiles is None else (int(row_tiles[0] * n_tiles), int(row_tiles[1] * n_tiles))
    grid = (ni, K // to, nk)
    a_spec = pl.BlockSpec((None, tm, tk), lambda i, j, k: (k // pk, i + i0, k % pk))
    b_spec = pl.BlockSpec((None, to, tk), lambda i, j, k: (k // per, j, k % per))
    o_spec = pl.BlockSpec((tm, to), lambda i, j, k: (i + i0, j))
    return _first(_matmul(name, grid, NT, dy, a_spec, wfull, b_spec, [jax.ShapeDtypeStruct((S, K), out_dtype)],
                          [o_spec], (tm, to), job=job, fill_into=fill_into), job)


def _mm_dgrad_rows(name, dy, wrows, out_dtype=F32, job=None):
    S, N = dy.shape
    K, _ = wrows.shape
    tm = _pick(S, 1024, LANES)
    to = _pick(K, 1408, LANES)
    tk = _pick(N, 2048, LANES)
    grid = (S // tm, K // to, N // tk)
    a_spec = pl.BlockSpec((tm, tk), lambda i, j, k: (i, k))
    b_spec = pl.BlockSpec((to, tk), lambda i, j, k: (j, k))
    o_spec = pl.BlockSpec((tm, to), lambda i, j, k: (i, j))
    return _first(_matmul(name, grid, NT, dy, a_spec, wrows, b_spec, [jax.ShapeDtypeStruct((S, K), out_dtype)],
                          [o_spec], (tm, to), job=job), job)


def _mm_wgrad_cols(name, a, dy, n, job=None):
    S, K = a.shape
    planes, _, npl = dy.shape
    N = planes * npl
    to = _pick(K, 1024, LANES)
    tn = _pick(n, 1408, LANES)
    ts = _pick(S, 4096 if to * tn * 4 <= SMALL_TILE_BYTES else 2048, LANES)
    per = n // tn
    nj = N // tn
    pj = nj // planes
    grid = (K // to, nj, S // ts)
    a_spec = pl.BlockSpec((ts, to), lambda i, j, k: (k, i))
    b_spec = pl.BlockSpec((None, ts, tn), lambda i, j, k: (j // pj, k, j % pj))
    o_spec = pl.BlockSpec((None, to, tn), lambda i, j, k: (j // per, i, j % per))
    return _first(_matmul(name, grid, TN, a, a_spec, dy, b_spec, [jax.ShapeDtypeStruct((N_CHIPS, K, n), F32)],
                          [o_spec], (to, tn), job=job), job)


def _mm_wgrad_rows(name, a, dy, kshard, job=None):
    S, K = a.shape
    _, N = dy.shape
    to = _pick(kshard, 1408, LANES)
    tn = N if to * N * 4 <= 2 * SMALL_TILE_BYTES else _pick(N, 1024, LANES)
    ts = _pick(S, 2048, LANES)
    per = kshard // to
    grid = (K // to, N // tn, S // ts)
    a_spec = pl.BlockSpec((ts, to), lambda i, j, k: (k, i))
    b_spec = pl.BlockSpec((ts, tn), lambda i, j, k: (k, j))
    o_spec = pl.BlockSpec((None, to, tn), lambda i, j, k: (i // per, i % per, j))
    return _first(_matmul(name, grid, TN, a, a_spec, dy, b_spec, [jax.ShapeDtypeStruct((N_CHIPS, kshard, N), F32)],
                          [o_spec], (to, tn), job=job), job)


def _rows(tm, width, colblk=0):
    return pl.BlockSpec((tm, width), lambda i: (i, colblk))


def _whole(shape):
    nd = len(shape)
    return pl.BlockSpec(shape, lambda i: (0,) * nd)


def _prev_halo(tm, h, width, colblk=0):
    r = tm // h
    return pl.BlockSpec((h, width), lambda i: (jnp.maximum(i * r - 1, 0), colblk))


def _next_halo(tm, h, width, nblk, colblk=0):
    r = tm // h
    return pl.BlockSpec((h, width), lambda i: (jnp.minimum((i + 1) * r, nblk - 1), colblk))


def _accumulate(i, ref, val):
    @pl.when(i == 0)
    def _():
        ref[...] = val

    @pl.when(i > 0)
    def _():
        ref[...] += val


def _fill_rotations(rot, offs):
    n = rot.shape[1]
    for r in sorted({o % SUBLANES for o in offs} - {0}):
        rot[r, 0:n - SUBLANES, :] = rot[0, r:r + n - SUBLANES, :]


def _tap_windows(rot, offs, row0, rb, lanes):
    by_res = {}
    for k, o in enumerate(offs):
        by_res.setdefault(o % SUBLANES, []).append((o // SUBLANES, k))
    for r, taps in by_res.items():
        lo = min(a for a, _ in taps)
        hi = max(a for a, _ in taps)
        win = rot[r, pl.ds(row0 + SUBLANES * lo, rb + SUBLANES * (hi - lo)), lanes]
        for a, k in taps:
            yield k, win[SUBLANES * (a - lo):SUBLANES * (a - lo) + rb, :]


def _for_chunks(n_rows, rb, fn):
    def step(j, carry):
        fn(pl.multiple_of(j * rb, rb))
        return carry

    lax.fori_loop(0, n_rows // rb, step, 0)


def _modnorm_fwd(name, x, g, scale, shift):
    S, D = x.shape
    tm = _pick(S, 1024, LANES)

    def body(x_ref, g_ref, sc_ref, sh_ref, h_ref):
        h_ref[...] = _modnorm(x_ref[...], g_ref[...], sc_ref[...], sh_ref[...]).astype(BF16)

    vec = _whole((1, D))
    return pl.pallas_call(
        body, name=name, grid=(S // tm,), in_specs=[_rows(tm, D), vec, vec, vec], out_specs=_rows(tm, D),
        out_shape=jax.ShapeDtypeStruct((S, D), BF16), compiler_params=_params(("parallel",)),
    )(x, g, scale, shift)


def _modnorm_bwd(name, x, dh, dx_in, g, scale, shift, gated=None):
    S, D = x.shape
    tm = _pick(S, 256, LANES)

    rb = _pick(tm, 32, 2 * SUBLANES)

    def body(x_ref, dh_ref, dxin_ref, g_ref, sc_ref, sh_ref, *rest):
        i = pl.program_id(0)
        dx_ref, dg_ref, dsc_ref, dsh_ref = rest[-4:] if gated is None else rest[2:6]
        sums = [dg_ref, dsc_ref, dsh_ref] + ([] if gated is None else [rest[7]])

        @pl.when(i == 0)
        def _():
            for ref in sums:
                ref[...] = jnp.zeros_like(ref)

        def chunk(row0):
            rows = pl.ds(row0, rb)
            _, pull = jax.vjp(_modnorm, x_ref[rows, :], g_ref[...], sc_ref[...], sh_ref[...])
            dx, dg, dsc, dsh = pull(dh_ref[rows, :])
            dx = dxin_ref[rows, :] + dx
            dx_ref[rows, :] = dx
            dg_ref[...] += dg
            dsc_ref[...] += dsc
            dsh_ref[...] += dsh
            if gated is not None:
                rest[6][rows, :] = (dx * rest[1][...]).astype(BF16)
                rest[7][...] += jnp.sum(dx * rest[0][rows, :].astype(F32), axis=0, keepdims=True)

        _for_chunks(tm, rb, chunk)

    vec = _whole((1, D))
    row = _rows(tm, D)
    vsd = jax.ShapeDtypeStruct((1, D), F32)
    in_specs, args = [row, row, row, vec, vec, vec], (x, dh, dx_in, g, scale, shift)
    out_specs, out_shape = [row, vec, vec, vec], [jax.ShapeDtypeStruct((S, D), F32), vsd, vsd, vsd]
    if gated is not None:
        in_specs, args = in_specs + [row, vec], args + tuple(gated)
        out_specs, out_shape = out_specs + [row, vec], out_shape + [jax.ShapeDtypeStruct((S, D), BF16), vsd]
    return _call(body, name, (S // tm,), in_specs, out_specs, out_shape, [], ("arbitrary",), args)


def _conv_fwd(name, proj, w, b, lg, lb, dc, job=None):
    S = proj.shape[0]
    K = w.shape[0]
    H = CONV_HALO
    tm = _pick(S, 256, LANES)

    def glu(v):
        return v[:, :dc] * jax.nn.sigmoid(v[:, dc:])

    offs = [H - (K - 1) + k for k in range(K)]
    rb = _pick(tm, CONV_CHUNK_ROWS, SUBLANES)
    lw = min(LANES, dc)

    def body(cur_ref, prev_ref, w_ref, b_ref, lg_ref, lb_ref, conv_ref, act_ref, rot):
        i = pl.program_id(0)
        rot[0, 0:H, :] = jnp.where(i > 0, glu(prev_ref[...]), 0.0)
        rot[0, H:, :] = glu(cur_ref[...])
        _fill_rotations(rot, offs)

        def chunk(row0):
            for l0 in range(0, dc, lw):
                lanes = slice(l0, l0 + lw)
                acc = jnp.broadcast_to(b_ref[:, lanes], (rb, lw))
                for k, win in _tap_windows(rot, offs, row0, rb, lanes):
                    acc = acc + w_ref[k:k + 1, lanes] * win
                conv_ref[pl.ds(row0, rb), lanes] = acc

        _for_chunks(tm, rb, chunk)
        act_ref[...] = _ln_silu(conv_ref[...], lg_ref[...], lb_ref[...]).astype(BF16)

    vec = _whole((1, dc))
    res = _call(body, name, (S // tm,),
                [_rows(tm, 2 * dc), _prev_halo(tm, H, 2 * dc), _whole(w.shape), vec, vec, vec],
                [_rows(tm, dc), _rows(tm, dc)],
                [jax.ShapeDtypeStruct((S, dc), F32), jax.ShapeDtypeStruct((S, dc), BF16)],
                [pltpu.VMEM((SUBLANES, tm + H, dc), F32)], ("parallel",), (proj, proj, w, b, lg, lb), job=job)
    return res if job is None else (res[:2], res[2:])


def _conv_bwd_ln(name, conv, dact, lg, lb):
    S, dc = conv.shape
    tm = _pick(S, 256, LANES)

    def body(c_ref, d_ref, lg_ref, lb_ref, dc_ref, dlg_ref, dlb_ref, db_ref):
        i = pl.program_id(0)
        _, pull = jax.vjp(_ln_silu, c_ref[...], lg_ref[...], lb_ref[...])
        dcv, dlg, dlb = pull(d_ref[...])
        dc_ref[...] = dcv
        _accumulate(i, dlg_ref, dlg)
        _accumulate(i, dlb_ref, dlb)
        _accumulate(i, db_ref, jnp.sum(dcv, axis=0, keepdims=True))

    vec = _whole((1, dc))
    row = _rows(tm, dc)
    vsd = jax.ShapeDtypeStruct((1, dc), F32)
    return pl.pallas_call(
        body, name=name, grid=(S // tm,), in_specs=[row, row, vec, vec], out_specs=[row, vec, vec, vec],
        out_shape=[jax.ShapeDtypeStruct((S, dc), F32), vsd, vsd, vsd],
        compiler_params=_params(("arbitrary",)),
    )(conv, dact, lg, lb)


def _conv_bwd(name, dconv, proj, w, dproj, dc, job=None):
    S = proj.shape[0]
    K = w.shape[0]
    H = CONV_HALO
    tm = _pick(S, 256, LANES)
    nt = S // tm

    offs_g = [H - (K - 1) + k for k in range(K)]
    offs_d = [K - 1 - k for k in range(K)]
    rb = _pick(tm, CONV_CHUNK_ROWS, SUBLANES)
    lw = min(LANES, dc)
    kp = -(-K // SUBLANES) * SUBLANES

    def body(d_ref, dn_ref, cur_ref, prev_ref, w_ref, dproj_any, da_ref, dw_ref, dbin_ref, rotg, rotd, accw, dglu_s):
        i = pl.program_id(0)
        pv = prev_ref[...]
        cv = cur_ref[...]
        sig = jax.nn.sigmoid(cv[:, dc:])
        rotg[0, 0:H, :] = jnp.where(i > 0, pv[:, :dc] * jax.nn.sigmoid(pv[:, dc:]), 0.0)
        rotg[0, H:, :] = cv[:, :dc] * sig
        rotd[0, 0:tm, :] = d_ref[...]
        rotd[0, tm:, :] = jnp.where(i < nt - 1, dn_ref[...], 0.0)
        _fill_rotations(rotg, offs_g)
        _fill_rotations(rotd, offs_d)

        @pl.when(i == 0)
        def _():
            accw[...] = jnp.zeros_like(accw)

        for l0 in range(0, dc, lw):
            lanes = slice(l0, l0 + lw)

            def chunk(j, sums, lanes=lanes):
                row0 = pl.multiple_of(j * rb, rb)
                dcur = rotd[0, pl.ds(row0, rb), lanes]
                acc = jnp.zeros((rb, lw), F32)
                for k, win in _tap_windows(rotd, offs_d, row0, rb, lanes):
                    acc = acc + w_ref[k:k + 1, lanes] * win
                new = list(sums)
                for k, win in _tap_windows(rotg, offs_g, row0, rb, lanes):
                    new[k] = sums[k] + jnp.sum((dcur * win).reshape(rb // SUBLANES, SUBLANES, lw), axis=0)
                dglu_s[pl.ds(row0, rb), lanes] = acc
                return tuple(new)

            sums = lax.fori_loop(0, tm // rb, chunk, tuple(jnp.zeros((SUBLANES, lw), F32) for _ in range(K)))
            for k in range(K):
                accw[SUBLANES * k:SUBLANES * (k + 1), lanes] += sums[k]
        dglu = dglu_s[...]
        da = jnp.concatenate([dglu * sig, dglu * cv[:, :dc] * sig * (1.0 - sig)], axis=1)
        da_ref[...] = da.astype(BF16)
        _accumulate(i, dbin_ref, jnp.sum(da, axis=0, keepdims=True))

        @pl.when(i == nt - 1)
        def _():
            dw_ref[...] = jnp.zeros_like(dw_ref)
            for k in range(K):
                dw_ref[k:k + 1, :] = jnp.sum(accw[SUBLANES * k:SUBLANES * (k + 1), :], axis=0, keepdims=True)

    res = _call(
        body, name, (nt,),
        [_rows(tm, dc), _next_halo(tm, H, dc, S // H), _rows(tm, 2 * dc), _prev_halo(tm, H, 2 * dc),
         _whole(w.shape), pl.BlockSpec(memory_space=pl.ANY)],
        [_rows(tm, 2 * dc), _whole((kp, dc)), _whole((1, 2 * dc))],
        [jax.ShapeDtypeStruct(dproj.shape, BF16), jax.ShapeDtypeStruct((kp, dc), F32),
         jax.ShapeDtypeStruct((1, 2 * dc), F32)],
        [pltpu.VMEM((SUBLANES, tm + H, dc), F32), pltpu.VMEM((SUBLANES, tm + H, dc), F32),
         pltpu.VMEM((SUBLANES * K, dc), F32), pltpu.VMEM((tm, dc), F32)],
        ("arbitrary",), (dconv, dconv, proj, proj, w, dproj), aliases={5: 0}, job=job)
    return res if job is None else (res[:3], res[3:])


def _mix(vln, wsm, bst, out_ref, G, CH, hd):
    for n in range(vln.shape[0] // CH):
        for g in range(G):
            blk = vln[n * CH:(n + 1) * CH, g * hd:(g + 1) * hd].astype(BF16)
            out_ref[n * CH:(n + 1) * CH, g * hd:(g + 1) * hd] = (
                jnp.dot(wsm[g], blk, preferred_element_type=F32) + bst[:, g:g + 1])


def _blocks_to_columns(full, name):
    nb, K, n = full.shape
    tr = _pick(K, max(SUBLANES, (1 << 19) // n), 2 * SUBLANES)

    def body(s_ref, o_ref):
        o_ref[...] = s_ref[...]

    return pl.pallas_call(
        body, name=name, grid=(nb, K // tr),
        in_specs=[pl.BlockSpec((None, tr, n), lambda p, i: (p, i, 0))],
        out_specs=pl.BlockSpec((tr, n), lambda p, i: (i, p)),
        out_shape=jax.ShapeDtypeStruct((K, nb * n), full.dtype),
        compiler_params=_params(("parallel", "parallel")))(full)


def _sgu_fwd(name, proj, a_act, wco, wso, lg, lb, ws, bst, D, ds, job=None):
    S = proj.shape[0]
    dc = a_act.shape[1]
    G, CH, _ = ws.shape
    hd = ds // G
    tm = _pick(S, 256, CH)

    def body(s_ref, gt_ref, a_ref, wco_ref, wso_ref, lg_ref, lb_ref, ws_ref, bst_ref,
             ya_ref, yb_ref, uv_ref, mg_ref, vmix):
        z = _gelu(s_ref[...])
        vln = _layer_norm(z[:, ds:], lg_ref[...], lb_ref[...])
        wsm = [_tril_mask(ws_ref[g]).astype(BF16) for g in range(G)]
        _mix(vln, wsm, bst_ref[...], vmix, G, CH, hd)
        uv = (z[:, :ds] * vmix[...]).astype(BF16)
        uv_ref[...] = uv
        ya = jnp.dot(a_ref[...], wco_ref[...], preferred_element_type=F32)
        yb = jnp.dot(uv, wso_ref[...], preferred_element_type=F32)
        ya_ref[...] = ya.astype(BF16)
        yb_ref[...] = yb.astype(BF16)
        gt = gt_ref[...]
        mg_ref[...] = (jax.nn.sigmoid(gt[:, :D]) * ya + jax.nn.sigmoid(gt[:, D:]) * yb).astype(BF16)

    vec = _whole((1, ds))
    sdb = jax.ShapeDtypeStruct((S, D), BF16)
    res = _call(body, name, (S // tm,),
                [_rows(tm, 2 * ds, 1), _rows(tm, 2 * D, 1), _rows(tm, dc), _whole(wco.shape), _whole(wso.shape),
                 vec, vec, _whole(ws.shape), _whole(bst.shape)],
                [_rows(tm, D), _rows(tm, D), _rows(tm, ds), _rows(tm, D)],
                [sdb, sdb, jax.ShapeDtypeStruct((S, ds), BF16), sdb],
                [pltpu.VMEM((tm, ds), F32)], ("parallel",), (proj, proj, a_act, wco, wso, lg, lb, ws, bst), job=job)
    return res if job is None else (res[:4], res[4:])


def _merge_bwd(name, dmerged, proj, ya, yb, D):
    S = proj.shape[0]
    tm = _pick(S, 256, LANES)

    def body(dm_ref, gt_ref, ya_ref, yb_ref, dya_ref, dyb_ref, dg_ref, dbin_ref):
        i = pl.program_id(0)
        dm = dm_ref[...]
        gt = gt_ref[...]
        sa = jax.nn.sigmoid(gt[:, :D])
        sb = jax.nn.sigmoid(gt[:, D:])
        dya_ref[...] = (dm * sa).astype(BF16)
        dyb_ref[...] = (dm * sb).astype(BF16)
        ya = ya_ref[...].astype(F32)
        yb = yb_ref[...].astype(F32)
        dg = jnp.concatenate([dm * ya * sa * (1.0 - sa), dm * yb * sb * (1.0 - sb)], axis=1)
        dg_ref[...] = dg.astype(BF16)
        _accumulate(i, dbin_ref, jnp.sum(dg, axis=0, keepdims=True))

    row = _rows(tm, D)
    sdb = jax.ShapeDtypeStruct((S, D), BF16)
    return pl.pallas_call(
        body, name=name, grid=(S // tm,),
        in_specs=[row, _rows(tm, 2 * D, 1), row, row],
        out_specs=[row, row, _rows(tm, 2 * D, 1), _whole((1, 2 * D))],
        out_shape=[sdb, sdb, jax.ShapeDtypeStruct((S, 4 * D), BF16), jax.ShapeDtypeStruct((1, 2 * D), F32)],
        compiler_params=_params(("arbitrary",)),
    )(dmerged, proj, ya, yb)


def _sgu_bwd(name, proj, duv, lg, lb, ws, bst, dproj, ds):
    S = proj.shape[0]
    G, CH, _ = ws.shape
    hd = ds // G
    tm = _pick(S, 256, CH)

    def body(s_ref, duv_ref, lg_ref, lb_ref, ws_ref, bst_ref, dproj_any,
             dsin_ref, dws_ref, dbs_ref, dlg_ref, dlb_ref, dbin_ref, vmix, dvln):
        i = pl.program_id(0)
        z, pull_gelu = jax.vjp(_gelu, s_ref[...])
        u = z[:, :ds]
        vln, pull_ln = jax.vjp(_layer_norm, z[:, ds:], lg_ref[...], lb_ref[...])
        wsm = [_tril_mask(ws_ref[g]).astype(BF16) for g in range(G)]
        _mix(vln, wsm, bst_ref[...], vmix, G, CH, hd)
        duv = duv_ref[...]
        du = duv * vmix[...]
        dvmix = duv * u
        for g in range(G):
            dws_g = jnp.zeros((CH, CH), F32)
            dbs_g = jnp.zeros((CH, 1), F32)
            for n in range(tm // CH):
                dblk = dvmix[n * CH:(n + 1) * CH, g * hd:(g + 1) * hd]
                vblk = vln[n * CH:(n + 1) * CH, g * hd:(g + 1) * hd].astype(BF16)
                dvln[n * CH:(n + 1) * CH, g * hd:(g + 1) * hd] = lax.dot_general(
                    wsm[g], dblk.astype(BF16), TN, preferred_element_type=F32)
                dws_g = dws_g + lax.dot_general(dblk.astype(BF16), vblk, NT, preferred_element_type=F32)
                dbs_g = dbs_g + jnp.sum(dblk, axis=1, keepdims=True)
            dws_g = _tril_mask(dws_g)
            dbs_g = jnp.broadcast_to(dbs_g, (CH, LANES))

            @pl.when(i == 0)
            def _():
                dws_ref[g] = dws_g
                dbs_ref[g] = dbs_g

            @pl.when(i > 0)
            def _():
                dws_ref[g] += dws_g
                dbs_ref[g] += dbs_g

        dv, dlg, dlb = pull_ln(dvln[...])
        (dsin,) = pull_gelu(jnp.concatenate([du, dv], axis=1))
        dsin_ref[...] = dsin.astype(BF16)
        _accumulate(i, dlg_ref, dlg)
        _accumulate(i, dlb_ref, dlb)
        _accumulate(i, dbin_ref, jnp.sum(dsin, axis=0, keepdims=True))

    vec = _whole((1, ds))
    vsd = jax.ShapeDtypeStruct((1, ds), F32)
    return pl.pallas_call(
        body, name=name, grid=(S // tm,),
        in_specs=[_rows(tm, 2 * ds, 1), _rows(tm, ds), vec, vec, _whole(ws.shape), _whole(bst.shape),
                  pl.BlockSpec(memory_space=pl.ANY)],
        out_specs=[_rows(tm, 2 * ds, 1), _whole((G, CH, CH)), _whole((G, CH, LANES)), vec, vec, _whole((1, 2 * ds))],
        out_shape=[jax.ShapeDtypeStruct(dproj.shape, BF16), jax.ShapeDtypeStruct((G, CH, CH), F32),
                   jax.ShapeDtypeStruct((G, CH, LANES), F32), vsd, vsd, jax.ShapeDtypeStruct((1, 2 * ds), F32)],
        scratch_shapes=[pltpu.VMEM((tm, ds), F32), pltpu.VMEM((tm, ds), F32)],
        input_output_aliases={6: 0},
        compiler_params=_params(("arbitrary",)),
    )(proj, duv, lg, lb, ws, bst, dproj)


def _silu_mul(val, gt):
    return jax.nn.silu(gt) * val


def _rotation_slots(offs):
    slot = {0: 0}
    for r in sorted({o % SUBLANES for o in offs} - {0}):
        slot[r] = len(slot)
    return slot


def _fill_plane_rotations(rot, slot):
    n = rot.shape[2]
    for r, s in slot.items():
        if r:
            rot[:, s, 0:n - SUBLANES, :] = rot[:, 0, r:r + n - SUBLANES, :]


def _ffn_tiles(S, Fh, rows=256):
    return _pick(S, rows, LANES), _pick(Fh, 1408, LANES)


def _ffn_fwd(name, up, w, b):
    _, S, Fh = up.shape
    K = w.shape[1]
    H = FFN_HALO
    tm, cw = _ffn_tiles(S, Fh, rows=512)
    r = tm // H

    offs = [H - (K - 1) + k for k in range(K)]
    slot = _rotation_slots(offs)
    rb = _pick(tm, 64, 2 * SUBLANES)
    lw = min(LANES, cw)

    def body(cur_ref, prev_ref, w_ref, b_ref, act_ref, rot):
        i = pl.program_id(1)
        rot[:, 0, 0:H, :] = jnp.where(i > 0, prev_ref[...], 0.0)
        rot[:, 0, H:, :] = cur_ref[...]
        _fill_plane_rotations(rot, slot)

        def chunk(row0):
            for l0 in range(0, cw, lw):
                lanes = slice(l0, l0 + lw)
                c2 = []
                for pln in range(2):
                    acc = jnp.broadcast_to(b_ref[pln, :, lanes], (rb, lw))
                    for k in range(K):
                        a, rr = divmod(offs[k], SUBLANES)
                        acc = acc + w_ref[pln, k:k + 1, lanes] * rot[pln, slot[rr], pl.ds(row0 + SUBLANES * a, rb), lanes]
                    c2.append(acc)
                act_ref[pl.ds(row0, rb), lanes] = _silu_mul(c2[0], c2[1]).astype(BF16)

        _for_chunks(tm, rb, chunk)

    return pl.pallas_call(
        body, name=name, grid=(Fh // cw, S // tm),
        in_specs=[pl.BlockSpec((2, tm, cw), lambda j, i: (0, i, j)),
                  pl.BlockSpec((2, H, cw), lambda j, i: (0, jnp.maximum(i * r - 1, 0), j)),
                  pl.BlockSpec((2, K, cw), lambda j, i: (0, 0, j)),
                  pl.BlockSpec((2, 1, cw), lambda j, i: (0, 0, j))],
        out_specs=pl.BlockSpec((tm, cw), lambda j, i: (i, j)),
        out_shape=jax.ShapeDtypeStruct((S, Fh), BF16),
        scratch_shapes=[pltpu.VMEM((2, len(slot), tm + H, cw), F32)],
        compiler_params=_params(("parallel", "parallel")),
    )(up, up, w, b)


def _ffn_bwd(name, up, dact, w, b, job=None):
    _, S, Fh = up.shape
    K = w.shape[1]
    H = FFN_HALO
    tm, cw = _ffn_tiles(S, Fh)
    r = tm // H
    nt = S // tm
    nhb = S // H
    te = tm + H

    offs_x = [H - (K - 1) + k for k in range(K)]
    offs_d = [K - 1 - k for k in range(K)]
    slot_x = _rotation_slots(offs_x)
    slot_d = _rotation_slots(offs_d)
    rb = _pick(tm, 64, 2 * SUBLANES)
    rbe = _pick(te, 96, SUBLANES)
    lw = min(LANES, cw)

    def body(cur_ref, prev_ref, next_ref, d_ref, dn_ref, w_ref, b_ref, dup_ref, dwb_ref, rotx, dext, rotd, accw):
        i = pl.program_id(1)
        rotx[:, 0, 0:H, :] = jnp.where(i > 0, prev_ref[...], 0.0)
        rotx[:, 0, H:H + tm, :] = cur_ref[...]
        rotx[:, 0, H + tm:, :] = jnp.where(i < nt - 1, next_ref[...], 0.0)
        dext[0:tm, :] = d_ref[...]
        dext[tm:, :] = jnp.where(i < nt - 1, dn_ref[...], 0.0)
        _fill_plane_rotations(rotx, slot_x)

        def chunk_e(row0):
            for l0 in range(0, cw, lw):
                lanes = slice(l0, l0 + lw)
                c2 = []
                for pln in range(2):
                    acc = jnp.broadcast_to(b_ref[pln, :, lanes], (rbe, lw))
                    for k in range(K):
                        a, rr = divmod(offs_x[k], SUBLANES)
                        acc = acc + w_ref[pln, k:k + 1, lanes] * rotx[pln, slot_x[rr], pl.ds(row0 + SUBLANES * a, rbe), lanes]
                    c2.append(acc)
                _, pull = jax.vjp(_silu_mul, c2[0], c2[1])
                dval, dgt = pull(dext[pl.ds(row0, rbe), lanes])
                rotd[0, 0, pl.ds(row0, rbe), lanes] = dval
                rotd[1, 0, pl.ds(row0, rbe), lanes] = dgt

        _for_chunks(te, rbe, chunk_e)
        _fill_plane_rotations(rotd, slot_d)

        @pl.when(i == 0)
        def _():
            accw[...] = jnp.zeros_like(accw)

        def chunk(row0):
            for l0 in range(0, cw, lw):
                lanes = slice(l0, l0 + lw)
                for pln in range(2):
                    dcur = rotd[pln, 0, pl.ds(row0, rb), lanes]
                    dup = jnp.zeros((rb, lw), F32)
                    for k in range(K):
                        a, rr = divmod(offs_d[k], SUBLANES)
                        dup = dup + w_ref[pln, k:k + 1, lanes] * rotd[pln, slot_d[rr], pl.ds(row0 + SUBLANES * a, rb), lanes]
                        a, rr = divmod(offs_x[k], SUBLANES)
                        prod = dcur * rotx[pln, slot_x[rr], pl.ds(row0 + SUBLANES * a, rb), lanes]
                        accw[pln, SUBLANES * k:SUBLANES * (k + 1), lanes] += jnp.sum(
                            prod.reshape(rb // SUBLANES, SUBLANES, lw), axis=0)
                    accw[pln, SUBLANES * K:SUBLANES * (K + 1), lanes] += jnp.sum(
                        dcur.reshape(rb // SUBLANES, SUBLANES, lw), axis=0)
                    dup_ref[pln, pl.ds(row0, rb), lanes] = dup.astype(BF16)

        _for_chunks(tm, rb, chunk)

        @pl.when(i == nt - 1)
        def _():
            dwb_ref[...] = jnp.zeros_like(dwb_ref)
            for pln in range(2):
                for k in range(K + 1):
                    dwb_ref[pln, k:k + 1, :] = jnp.sum(accw[pln, SUBLANES * k:SUBLANES * (k + 1), :], axis=0,
                                                       keepdims=True)

    res = _call(
        body, name, (Fh // cw, nt),
        [pl.BlockSpec((2, tm, cw), lambda j, i: (0, i, j)),
         pl.BlockSpec((2, H, cw), lambda j, i: (0, jnp.maximum(i * r - 1, 0), j)),
         pl.BlockSpec((2, H, cw), lambda j, i: (0, jnp.minimum((i + 1) * r, nhb - 1), j)),
         pl.BlockSpec((tm, cw), lambda j, i: (i, j)),
         pl.BlockSpec((H, cw), lambda j, i: (jnp.minimum((i + 1) * r, nhb - 1), j)),
         pl.BlockSpec((2, K, cw), lambda j, i: (0, 0, j)),
         pl.BlockSpec((2, 1, cw), lambda j, i: (0, 0, j))],
        [pl.BlockSpec((2, tm, cw), lambda j, i: (0, i, j)), pl.BlockSpec((2, SUBLANES, cw), lambda j, i: (0, 0, j))],
        [jax.ShapeDtypeStruct((2, S, Fh), BF16), jax.ShapeDtypeStruct((2, SUBLANES, Fh), F32)],
        [pltpu.VMEM((2, len(slot_x), tm + 2 * H, cw), F32), pltpu.VMEM((te, cw), F32),
         pltpu.VMEM((2, len(slot_d), te, cw), F32), pltpu.VMEM((2, SUBLANES * (K + 1), cw), F32)],
        ("parallel", "arbitrary"), (up, up, up, dact, dact, w, b), job=job)
    return res if job is None else (res[:2], res[2:])


def _rms(x, g):
    return x * lax.rsqrt(jnp.mean(x * x, axis=-1, keepdims=True) + EPS) * g


def _gate_bwd_tile(i, dx, out_ref, gate_ref, dout_ref, dgate_ref):
    dout_ref[...] = (dx * gate_ref[...]).astype(BF16)
    _accumulate(i, dgate_ref, jnp.sum(dx * out_ref[...].astype(F32), axis=0, keepdims=True))


def _final(name, x2, target, gf, out, gate):
    S, D = x2.shape
    tm = _pick(S, 256, LANES)

    def body(x_ref, t_ref, g_ref, o_ref, gate_ref, dx_ref, dg_ref, loss_ref, do_ref, dgate_ref):
        i = pl.program_id(0)
        y, pull = jax.vjp(_rms, x_ref[...], g_ref[...])
        e = y - t_ref[...]
        dx, dg = pull(e / D)
        dx_ref[...] = dx
        _accumulate(i, dg_ref, dg)
        part = 0.5 * jnp.sum(jnp.mean(jnp.square(e), axis=-1, keepdims=True), axis=0, keepdims=True)
        _accumulate(i, loss_ref, jnp.broadcast_to(part, (SUBLANES, LANES)))
        _gate_bwd_tile(i, dx, o_ref, gate_ref, do_ref, dgate_ref)

    row = _rows(tm, D)
    vec = _whole((1, D))
    vsd = jax.ShapeDtypeStruct((1, D), F32)
    return pl.pallas_call(
        body, name=name, grid=(S // tm,), in_specs=[row, row, vec, row, vec],
        out_specs=[row, vec, _whole((SUBLANES, LANES)), row, vec],
        out_shape=[jax.ShapeDtypeStruct((S, D), F32), vsd, jax.ShapeDtypeStruct((SUBLANES, LANES), F32),
                   jax.ShapeDtypeStruct((S, D), BF16), vsd],
        compiler_params=_params(("arbitrary",)),
    )(x2, target, gf, out, gate)


def _ada_fwd(name, c_pad, w_ada, b_cols):
    nb, D = c_pad.shape
    n = w_ada.shape[1]
    tn = _pick(n, 1024, LANES)

    def body(c_ref, w_ref, b_ref, o_ref):
        o_ref[...] = jnp.dot(jax.nn.silu(c_ref[...]).astype(BF16), w_ref[...].astype(BF16),
                             preferred_element_type=F32) + b_ref[...]

    return pl.pallas_call(
        body, name=name, grid=(n // tn,),
        in_specs=[_whole((nb, D)), pl.BlockSpec((D, tn), lambda j: (0, j)), pl.BlockSpec((1, tn), lambda j: (0, j))],
        out_specs=pl.BlockSpec((nb, tn), lambda j: (0, j)),
        out_shape=jax.ShapeDtypeStruct((nb, n), F32), compiler_params=_params(("parallel",)),
    )(c_pad, w_ada, b_cols)


def _ada_wgrad(name, c_t, dmod_cols):
    D, nb = c_t.shape
    n = dmod_cols.shape[1]
    tr = _pick(D, 256, SUBLANES)

    def body(c_ref, d_ref, o_ref):
        ca = jax.nn.silu(c_ref[...])
        acc = ca[:, 0:1] * d_ref[0:1, :]
        for b in range(1, nb):
            acc = acc + ca[:, b:b + 1] * d_ref[b:b + 1, :]
        o_ref[...] = acc

    return pl.pallas_call(
        body, name=name, grid=(D // tr,),
        in_specs=[pl.BlockSpec((tr, nb), lambda i: (i, 0)), _whole((nb, n))],
        out_specs=pl.BlockSpec((tr, n), lambda i: (i, 0)),
        out_shape=jax.ShapeDtypeStruct((D, n), F32), compiler_params=_params(("parallel",)),
    )(c_t, dmod_cols)


def kernel(x, c, w_ada, b_ada, norm1_g, w_in, b_in, conv_dw_w, conv_dw_b, conv_ln_g, conv_ln_b, w_conv_out, sgu_ln_g, sgu_ln_b, w_spatial, b_spatial, w_sgu_out, w_out, norm2_g, w_up, ffn_dw_w, ffn_dw_b, w_down, final_g, loss_target, m_w_ada, m_b_ada, m_norm1_g, m_w_in, m_b_in, m_conv_dw_w, m_conv_dw_b, m_conv_ln_g, m_conv_ln_b, m_w_conv_out, m_sgu_ln_g, m_sgu_ln_b, m_w_spatial, m_b_spatial, m_w_sgu_out, m_w_out, m_norm2_g, m_w_up, m_ffn_dw_w, m_ffn_dw_b, m_w_down, m_final_g, v_w_ada, v_b_ada, v_norm1_g, v_w_in, v_b_in, v_conv_dw_w, v_conv_dw_b, v_conv_ln_g, v_conv_ln_b, v_w_conv_out, v_sgu_ln_g, v_sgu_ln_b, v_w_spatial, v_b_spatial, v_w_sgu_out, v_w_out, v_norm2_g, v_w_up, v_ffn_dw_w, v_ffn_dw_b, v_w_down, v_final_g):
    S, D = x.shape[1], x.shape[2]
    dc = w_conv_out.shape[1]
    ds = w_sgu_out.shape[1]
    G, CH = w_spatial.shape[1], w_spatial.shape[2]
    KC = conv_dw_w.shape[1]
    KF = ffn_dw_w.shape[1]
    F2 = ffn_dw_b.shape[1]
    Fh = F2 // 2
    n_ada = w_ada.shape[2]
    n_up = w_up.shape[2]
    ax, ay, ac = _axes()
    chip = 2 * ax + ay
    me = 2 * chip + ac
    c_idx = jnp.reshape(ac, (1,)).astype(jnp.int32)
    p_idx = jnp.reshape(chip, (1,)).astype(jnp.int32)

    xs = x[0]
    tgt = loss_target[0]

    g1 = _allgather8(_pack([c[0], conv_dw_w[0], ffn_dw_w[0]]), "gather_small_in")
    n_cw, n_fw = conv_dw_w.shape[2], ffn_dw_w.shape[2]
    parts = [_unpack(g1[2 * q], [(D,), conv_dw_w.shape[1:], ffn_dw_w.shape[1:]]) for q in range(N_CHIPS)]
    c_all = jnp.stack([_unpack(g1[d], [(D,)])[0] for d in range(N_DEV)])
    cw_full = jnp.concatenate([pt[1] for pt in parts], axis=1)
    fw_full = jnp.concatenate([pt[2] for pt in parts], axis=1)

    b_cols = lax.dynamic_slice(b_ada, (0, chip * n_ada), (1, n_ada))
    c_pad = jnp.concatenate([c_all, jnp.zeros_like(c_all)], axis=0)
    mod_blk = _ada_fwd("ada_fwd", c_pad, w_ada[0], b_cols)[:N_DEV]
    g2 = _allgather8(_pack([mod_blk]), "gather_mod")
    mod_all = jnp.concatenate([_unpack(g2[2 * q], [(N_DEV, n_ada)])[0] for q in range(N_CHIPS)], axis=1)
    mod = lax.dynamic_slice(mod_all, (me, 0), (1, 6 * D))
    shift1, scale1, gate1, shift2, scale2, gate2 = [mod[:, k * D:(k + 1) * D] for k in range(6)]

    shards = [w_in[0], w_conv_out[0], w_sgu_out[0], w_out[0], w_up[0], w_down[0]]
    names = ["in", "conv_out", "sgu_out", "out", "up", "down"]
    blk = {nm: _cast_into_block(s, p_idx, "cast_" + nm) for s, nm in zip(shards, names)}
    pc_idx = jnp.concatenate([p_idx, c_idx])

    h1 = _modnorm_fwd("modnorm1", xs, norm1_g, scale1, shift1)
    (proj,), (win_xy,) = _mm_fwd_block("proj_own", h1, blk["in"], b_in, p_idx, 0,
                                       job=_job_gather([blk["in"]], rel=(0, 1), fresh=True))
    (proj,), (win_d,) = _mm_fwd_block("proj_x", h1, win_xy, b_in, p_idx, 2, into=proj,
                                      job=_job_gather([blk["in"]], rel=(2,), fresh=True))
    (proj,), (wco_f, wso_f) = _mm_fwd_block("proj_y", h1, win_xy, b_in, p_idx, 1, into=proj,
                                            job=_job_gather([blk["conv_out"], blk["sgu_out"]]))
    (proj,), (wout_f,) = _mm_fwd_block("proj_diag", h1, win_d, b_in, p_idx, 3, into=proj,
                                       job=_job_gather([blk["out"]]))
    win_f = _copy_block(_copy_block(win_xy, blk["in"], p_idx, 0, "fill_w_in_own"), win_d, p_idx, 3, "fill_w_in_diag")
    wout_r = wout_f.reshape(-1, wout_f.shape[2])
    wco_c = _blocks_to_columns(wco_f, "columns_conv_out")
    wso_c = _blocks_to_columns(wso_f, "columns_sgu_out")
    (conv, a_act), (wup_xy,) = _conv_fwd("conv_fwd", proj, cw_full, conv_dw_b, conv_ln_g, conv_ln_b, dc,
                                         job=_job_gather([blk["up"]], rel=(0, 1)))
    bst = jnp.transpose(b_spatial[0])
    (ya, yb, uv, merged), (wup_f,) = _sgu_fwd("sgu_fwd", proj, a_act, wco_c, wso_c, sgu_ln_g, sgu_ln_b, w_spatial[0],
                                              bst, D, ds, job=_job_gather([wup_xy], rel=(2,)))
    out1, x1 = _mm_fwd_rows("out1", merged, wout_r, xs, gate1)
    h2 = _modnorm_fwd("modnorm2", x1, norm2_g, scale2, shift2)
    up, (wdown_f,) = _mm_fwd("up", h2, wup_f, planes=2, job=_job_gather([blk["down"]]))
    wdown_r = wdown_f.reshape(-1, wdown_f.shape[2])
    fw2 = jnp.stack([fw_full[:, :Fh], fw_full[:, Fh:]])
    fb2 = jnp.stack([ffn_dw_b[:, :Fh], ffn_dw_b[:, Fh:]])
    act = _ffn_fwd("ffn_fwd", up, fw2, fb2)
    out2, x2 = _mm_fwd_rows("out2", act, wdown_r, x1, gate2)
    dx2, d_final_g, loss_blk, dout2, d_gate2 = _final("final", x2, tgt, final_g.reshape(1, D), out2, gate2)
    loss = lax.psum(loss_blk[0, 0], ("x", "y", "c"))

    def add_cores(nm, g, r1):
        return _add_own_half(g, r1, pc_idx, "add_cores_" + nm)

    def add_chips(nm, g, r1, r2):
        return _add_chips(g, r1, r2, pc_idx, "add_chips_" + nm)

    g_wdown = _mm_wgrad_rows("wgrad_down", act, dout2, w_down.shape[1])
    dact, (r1_down,) = _mm_dgrad_rows("dgrad_down", dout2, wdown_r, job=_job_swap_halves([g_wdown]))
    s_down = add_cores("down", g_wdown, r1_down)
    (dup, d_ffn), (r2_down,) = _ffn_bwd("ffn_bwd", up, dact, fw2, fb2, job=_job_scatter_blocks([s_down]))
    h_down = add_chips("down", g_wdown, r1_down, r2_down)
    g_wup = _mm_wgrad_cols("wgrad_up", h2, dup, n_up)
    dh2, (r1_up,) = _mm_dgrad_cols("dgrad_up", dup, wup_f, job=_job_swap_halves([g_wup]))
    s_up = add_cores("up", g_wup, r1_up)
    dx1, d_norm2, d_scale2, d_shift2, dout1, d_gate1 = _modnorm_bwd(
        "modnorm2_bwd", x1, dh2, dx2, norm2_g, scale2, shift2, gated=(out1, gate1))
    g_wout = _mm_wgrad_rows("wgrad_out", merged, dout1, w_out.shape[1])
    dmerged = _mm_dgrad_rows("dgrad_out", dout1, wout_r)
    dya, dyb, dproj, dbin_g = _merge_bwd("merge_bwd", dmerged, proj, ya, yb, D)
    g_wco = _mm_wgrad_cols("wgrad_conv_out", a_act, dya[None], w_conv_out.shape[2])
    g_wso = _mm_wgrad_cols("wgrad_sgu_out", uv, dyb[None], w_sgu_out.shape[2])
    da_act, (r1_out, r1_co, r1_so) = _mm_dgrad_rows("dgrad_conv_out", dya, wco_c,
                                                    job=_job_swap_halves([g_wout, g_wco, g_wso]))
    s_out = add_cores("out", g_wout, r1_out)
    s_co = add_cores("conv_out", g_wco, r1_co)
    s_so = add_cores("sgu_out", g_wso, r1_so)
    duv = _mm_dgrad_rows("dgrad_sgu_out", dyb, wso_c)
    dproj, d_ws, d_bs, d_sgu_g, d_sgu_b, dbin_s = _sgu_bwd("sgu_bwd", proj, duv, sgu_ln_g, sgu_ln_b, w_spatial[0], bst,
                                                           dproj, ds)
    dconv, d_cln_g, d_cln_b, d_conv_b = _conv_bwd_ln("conv_ln_bwd", conv, da_act, conv_ln_g, conv_ln_b)
    d_fw = jnp.concatenate([d_ffn[0, :KF], d_ffn[1, :KF]], axis=1)
    d_fb = jnp.concatenate([d_ffn[0, KF:KF + 1], d_ffn[1, KF:KF + 1]], axis=1)
    early = [d_sgu_g, d_sgu_b, d_ws, d_bs[:, :, 0], d_norm2, d_fb, d_final_g, d_fw]
    (dproj, d_cw, dbin_a), (r2_up, g_early) = _conv_bwd(
        "conv_bwd", dconv, proj, cw_full, dproj, dc,
        job=_merge_jobs([_job_scatter_blocks([s_up]), _job_allgather8(_pack(early))]))
    h_up = add_chips("up", g_wup, r1_up, r2_up)
    g_win, (r2_out, r2_co, r2_so, sib_up, sib_down) = _mm_wgrad_cols(
        "wgrad_in", h1, dproj[None], w_in.shape[2],
        job=_merge_jobs([_job_scatter_blocks([s_out, s_co, s_so]), _job_to_sibling([h_up, h_down])]))
    h_out = add_chips("out", g_wout, r1_out, r2_out)
    h_co = add_chips("conv_out", g_wco, r1_co, r2_co)
    h_so = add_chips("sgu_out", g_wso, r1_so, r2_so)
    dh1, (r1_in, sib_out, sib_co, sib_so) = _mm_dgrad_cols(
        "dgrad_in_a", dproj[None], win_f, row_tiles=(0.0, 0.5),
        job=_merge_jobs([_job_swap_halves([g_win]), _job_to_sibling([h_out, h_co, h_so])]))
    s_in = add_cores("in", g_win, r1_in)
    dh1, (r2_in,) = _mm_dgrad_cols("dgrad_in_b", dproj[None], win_f, row_tiles=(0.5, 0.5), fill_into=dh1,
                                   job=_job_scatter_blocks([s_in]))
    h_in = add_chips("in", g_win, r1_in, r2_in)
    dxs, d_norm1, d_scale1, d_shift1 = _modnorm_bwd("modnorm1_bwd", xs, dh1, dx1, norm1_g, scale1, shift1)

    d_mod = jnp.concatenate([d_shift1, d_scale1, d_gate1, d_shift2, d_scale2, d_gate2], axis=1)
    d_b_in = jnp.concatenate([dbin_a, dbin_s, dbin_g], axis=1)
    late = [d_mod, d_norm1, d_b_in, d_conv_b, d_cln_g, d_cln_b, d_cw[:KC]]
    g3, sib_in = _run_job(_merge_jobs([_job_allgather8(_pack(late)), _job_to_sibling([h_in])]), "gather_small_grads")
    big_halves = {"w_in": (h_in, sib_in), "w_conv_out": (h_co, sib_co), "w_sgu_out": (h_so, sib_so),
                  "w_out": (h_out, sib_out), "w_up": (h_up, sib_up), "w_down": (h_down, sib_down)}
    g_b_ada, g_norm1, g_b_in, g_conv_b, g_cln_g, g_cln_b, g_cw_full = _unpack(
        _sum8(g3, "sum_small_grads"), [a.shape for a in late])
    g_sgu_g, g_sgu_b, g_ws, g_bs, g_norm2, g_fb, g_final, g_fw_full = _unpack(
        _sum8(g_early, "sum_early_grads"), [a.shape for a in early])
    g_cw = lax.dynamic_slice(g_cw_full, (0, chip * n_cw), (KC, n_cw))
    g_fw = lax.dynamic_slice(g_fw_full, (0, chip * n_fw), (KF, n_fw))
    dmod_all = g3.reshape(N_DEV, -1)[:, :6 * D]
    dmod_cols = lax.dynamic_slice(dmod_all, (0, chip * n_ada), (N_DEV, n_ada))
    g_wada = _ada_wgrad("ada_wgrad", jnp.transpose(c_all), dmod_cols)

    grads = {
        "w_ada": g_wada[None], "b_ada": g_b_ada, "norm1_g": g_norm1, "b_in": g_b_in,
        "conv_dw_w": g_cw[None], "conv_dw_b": g_conv_b, "conv_ln_g": g_cln_g, "conv_ln_b": g_cln_b,
        "sgu_ln_g": g_sgu_g, "sgu_ln_b": g_sgu_b, "w_spatial": g_ws[None],
        "b_spatial": g_bs[None], "norm2_g": g_norm2, "ffn_dw_w": g_fw[None], "ffn_dw_b": g_fb,
        "final_g": g_final.reshape(D),
    }
    weights = dict(w_ada=w_ada, b_ada=b_ada, norm1_g=norm1_g, w_in=w_in, b_in=b_in, conv_dw_w=conv_dw_w, conv_dw_b=conv_dw_b, conv_ln_g=conv_ln_g, conv_ln_b=conv_ln_b, w_conv_out=w_conv_out, sgu_ln_g=sgu_ln_g, sgu_ln_b=sgu_ln_b, w_spatial=w_spatial, b_spatial=b_spatial, w_sgu_out=w_sgu_out, w_out=w_out, norm2_g=norm2_g, w_up=w_up, ffn_dw_w=ffn_dw_w, ffn_dw_b=ffn_dw_b, w_down=w_down, final_g=final_g)
    m_in = dict(w_ada=m_w_ada, b_ada=m_b_ada, norm1_g=m_norm1_g, w_in=m_w_in, b_in=m_b_in, conv_dw_w=m_conv_dw_w, conv_dw_b=m_conv_dw_b, conv_ln_g=m_conv_ln_g, conv_ln_b=m_conv_ln_b, w_conv_out=m_w_conv_out, sgu_ln_g=m_sgu_ln_g, sgu_ln_b=m_sgu_ln_b, w_spatial=m_w_spatial, b_spatial=m_b_spatial, w_sgu_out=m_w_sgu_out, w_out=m_w_out, norm2_g=m_norm2_g, w_up=m_w_up, ffn_dw_w=m_ffn_dw_w, ffn_dw_b=m_ffn_dw_b, w_down=m_w_down, final_g=m_final_g)
    v_in = dict(w_ada=v_w_ada, b_ada=v_b_ada, norm1_g=v_norm1_g, w_in=v_w_in, b_in=v_b_in, conv_dw_w=v_conv_dw_w, conv_dw_b=v_conv_dw_b, conv_ln_g=v_conv_ln_g, conv_ln_b=v_conv_ln_b, w_conv_out=v_w_conv_out, sgu_ln_g=v_sgu_ln_g, sgu_ln_b=v_sgu_ln_b, w_spatial=v_w_spatial, b_spatial=v_b_spatial, w_sgu_out=v_w_sgu_out, w_out=v_w_out, norm2_g=v_norm2_g, w_up=v_w_up, ffn_dw_w=v_ffn_dw_w, ffn_dw_b=v_ffn_dw_b, w_down=v_w_down, final_g=v_final_g)
    order = list(weights.keys())
    large = ["w_ada", "w_in", "w_conv_out", "w_sgu_out", "w_out", "w_up", "w_down"]
    little = [n for n in order if n not in large]
    delta, new_m, new_v = {}, {}, {}
    for n in large:
        shp = weights[n].shape
        two = (shp[1], shp[2])
        if n in big_halves:
            g_, d_, m_, v_ = _adamw_halves(weights[n].reshape(two), big_halves[n][0], big_halves[n][1],
                                           m_in[n].reshape(two), v_in[n].reshape(two), c_idx, "adamw_" + n)
            grads[n] = g_.reshape(shp)
        else:
            d_, m_, v_ = _adamw(weights[n].reshape(two), grads[n].reshape(two), m_in[n].reshape(two),
                                v_in[n].reshape(two), "adamw_" + n)
        delta[n], new_m[n], new_v[n] = d_.reshape(shp), m_.reshape(shp), v_.reshape(shp)
    shapes = [weights[n].shape for n in little]
    d_, m_, v_ = _adamw(_pack([weights[n] for n in little]), _pack([grads[n] for n in little]),
                        _pack([m_in[n] for n in little]), _pack([v_in[n] for n in little]), "adamw_small")
    for n, dd, mm, vv in zip(little, _unpack(d_, shapes), _unpack(m_, shapes), _unpack(v_, shapes)):
        delta[n], new_m[n], new_v[n] = dd, mm, vv
    grad_out = [grads[n].reshape(weights[n].shape) for n in order]
    return (loss, dxs[None], *grad_out, *[delta[n] for n in order], *[new_m[n] for n in order],
            *[new_v[n] for n in order])
```
